```python
import jax
import jax.numpy as jnp
from jax import lax
import numpy as np

D_MODEL = 1024
BATCH = 8
SEQ = 16384
DEPTH = 2

MEM_LEN = 256
CONV_WIDTH = 512
CONV_K = 3
SG_WIDTH = 512
SG_GROUPS = 4
SG_GROUP_DIM = SG_WIDTH // SG_GROUPS
CHUNK = 128
FOX_HEADS = 8
FOX_HEAD_DIM = 64
FOX_WIDTH = FOX_HEADS * FOX_HEAD_DIM
Q_BLOCK = 128
FORGET_BIAS_CENTER = 3.0
N_BRANCH = 3
BRANCH_WIDTH = 512
XA_HEADS = 4
XA_HEAD_DIM = D_MODEL // XA_HEADS
D_FF = 2816
RMS_EPS = 1e-6
SPLIT_SIZES = (CONV_WIDTH, CONV_WIDTH, CONV_WIDTH, SG_WIDTH, SG_WIDTH,
               FOX_WIDTH, FOX_WIDTH, FOX_WIDTH, FOX_HEADS, N_BRANCH * D_MODEL)
D_IN_PROJ = sum(SPLIT_SIZES)

kernel_name = "hybrid_conv_gmlp_fox_macaron"


def rmsnorm(x, g):
    xf = x.astype(jnp.float32)
    y = xf * lax.rsqrt(jnp.mean(xf * xf, axis=-1, keepdims=True) + RMS_EPS)
    return (y * g.astype(jnp.float32)).astype(x.dtype)


def swiglu(h, w_gate, w_up, w_down):
    return (jax.nn.silu(h @ w_gate) * (h @ w_up)) @ w_down


def short_conv_mixer(b_gate, c_gate, h, conv_w):
    z = c_gate * h
    s = z.shape[1]
    zp = jnp.pad(z, ((0, 0), (CONV_K - 1, 0), (0, 0)))
    y = sum(conv_w[k] * zp[:, k:k + s] for k in range(CONV_K))
    return b_gate * y


def spatial_gating_mixer(u, v, v_norm_g, w_s, b_s):
    bsz, s, _ = v.shape
    u = jax.nn.gelu(u)
    v = rmsnorm(jax.nn.gelu(v), v_norm_g)
    vc = v.reshape(bsz, s // CHUNK, CHUNK, SG_GROUPS, SG_GROUP_DIM)
    causal = jnp.tril(jnp.ones((CHUNK, CHUNK), dtype=bool))
    w = jnp.where(causal[None], w_s, 0.0).astype(v.dtype)
    sv = jnp.einsum('gts,bnsgc->bntgc', w, vc) + b_s.T[:, :, None].astype(v.dtype)
    return u * sv.reshape(bsz, s, SG_WIDTH)


def forgetting_attention(q, k, v, f_logit, b_f):
    bsz, s, _ = q.shape
    q = q.reshape(bsz, s, FOX_HEADS, FOX_HEAD_DIM)
    k = k.reshape(bsz, s, FOX_HEADS, FOX_HEAD_DIM)
    v = v.reshape(bsz, s, FOX_HEADS, FOX_HEAD_DIM)
    log_f = jax.nn.log_sigmoid(f_logit.astype(jnp.float32) + b_f.astype(jnp.float32))
    c = jnp.cumsum(log_f, axis=1).transpose(0, 2, 1)
    scale = FOX_HEAD_DIM ** -0.5
    kpos = jnp.arange(s)

    def block(i):
        start = i * Q_BLOCK
        qb = lax.dynamic_slice_in_dim(q, start, Q_BLOCK, axis=1)
        cq = lax.dynamic_slice_in_dim(c, start, Q_BLOCK, axis=2)
        logits = jnp.einsum('bqhd,bkhd->bhqk', qb, k,
                            preferred_element_type=jnp.float32) * scale
        logits = logits + cq[..., :, None] - c[..., None, :]
        qpos = start + jnp.arange(Q_BLOCK)
        logits = jnp.where(kpos[None, :] <= qpos[:, None], logits, -jnp.inf)
        p = jax.nn.softmax(logits, axis=-1)
        return jnp.einsum('bhqk,bkhd->bqhd', p.astype(v.dtype), v)

    out = lax.map(block, jnp.arange(s // Q_BLOCK))
    return out.transpose(1, 0, 2, 3, 4).reshape(bsz, s, FOX_WIDTH)


def memory_cross_attention(h, m, wq, wk, wv, wo):
    bsz, s, _ = h.shape
    q = (h @ wq).reshape(bsz, s, XA_HEADS, XA_HEAD_DIM)
    k = (m @ wk).reshape(bsz, m.shape[1], XA_HEADS, XA_HEAD_DIM)
    v = (m @ wv).reshape(bsz, m.shape[1], XA_HEADS, XA_HEAD_DIM)
    logits = jnp.einsum('bqhd,bkhd->bhqk', q, k,
                        preferred_element_type=jnp.float32) * (XA_HEAD_DIM ** -0.5)
    p = jax.nn.softmax(logits, axis=-1)
    o = jnp.einsum('bhqk,bkhd->bqhd', p.astype(v.dtype), v).reshape(bsz, s, D_MODEL)
    return o @ wo


def _fwd_setup_inputs(seed: int = 0) -> dict:
    key = jax.random.key(seed)
    ks = jax.random.split(key, 32)
    f32 = jnp.float32
    L, D = DEPTH, D_MODEL

    def nrm(k, shape, scale):
        return jax.random.normal(k, shape, f32) * scale

    def gain(k, shape):
        return 1.0 + 0.05 * jax.random.normal(k, shape, f32)

    return {
        "x": nrm(ks[0], (BATCH, SEQ, D), 1.0),
        "mem": nrm(ks[1], (BATCH, MEM_LEN, D), 1.0),
        "ffn1_norm": gain(ks[2], (L, D)),
        "ffn1_w_gate": nrm(ks[3], (L, D, D_FF), D ** -0.5),
        "ffn1_w_up": nrm(ks[4], (L, D, D_FF), D ** -0.5),
        "ffn1_w_down": nrm(ks[5], (L, D_FF, D), D_FF ** -0.5),
        "mix_norm": gain(ks[6], (L, D)),
        "w_in": nrm(ks[7], (L, D, D_IN_PROJ), D ** -0.5),
        "conv_w": nrm(ks[8], (L, CONV_K, CONV_WIDTH), CONV_K ** -0.5),
        "sg_norm": gain(ks[9], (L, SG_WIDTH)),
        "sg_w": nrm(ks[10], (L, SG_GROUPS, CHUNK, CHUNK), CHUNK ** -0.5),
        "sg_b": 1.0 + nrm(ks[11], (L, SG_GROUPS, CHUNK), 0.1),
        "fox_b_f": FORGET_BIAS_CENTER + nrm(ks[12], (L, FOX_HEADS), 1.0),
        "w_branch": nrm(ks[13], (L, N_BRANCH, BRANCH_WIDTH, D), BRANCH_WIDTH ** -0.5),
        "w_out": nrm(ks[14], (L, D, D), D ** -0.5),
        "xa_norm": gain(ks[15], (L, D)),
        "mem_norm": gain(ks[16], (L, D)),
        "xa_wq": nrm(ks[17], (L, D, D), D ** -0.5),
        "xa_wk": nrm(ks[18], (L, D, D), D ** -0.5),
        "xa_wv": nrm(ks[19], (L, D, D), D ** -0.5),
        "xa_wo": nrm(ks[20], (L, D, D), D ** -0.5),
        "ffn2_norm": gain(ks[21], (L, D)),
        "ffn2_w_gate": nrm(ks[22], (L, D, D_FF), D ** -0.5),
        "ffn2_w_up": nrm(ks[23], (L, D, D_FF), D ** -0.5),
        "ffn2_w_down": nrm(ks[24], (L, D_FF, D), D_FF ** -0.5),
        "final_norm": gain(ks[25], (D,)),
    }


def _fwd_reference(x, mem, ffn1_norm, ffn1_w_gate, ffn1_w_up, ffn1_w_down, mix_norm, w_in,
              conv_w, sg_norm, sg_w, sg_b, fox_b_f, w_branch, w_out, xa_norm, mem_norm,
              xa_wq, xa_wk, xa_wv, xa_wo, ffn2_norm, ffn2_w_gate, ffn2_w_up, ffn2_w_down,
              final_norm):
    bsz, s, _ = x.shape
    split_idx = np.cumsum(SPLIT_SIZES)[:-1].tolist()
    for l in range(DEPTH):
        h = rmsnorm(x, ffn1_norm[l])
        x = x + 0.5 * swiglu(h, ffn1_w_gate[l], ffn1_w_up[l], ffn1_w_down[l])

        h = rmsnorm(x, mix_norm[l])
        proj = h @ w_in[l]
        (a_b, a_c, a_h, sg_u, sg_v, fq, fk, fv, f_logit, gate_logits) = jnp.split(
            proj, split_idx, axis=-1)
        y_a = short_conv_mixer(a_b, a_c, a_h, conv_w[l])
        y_b = spatial_gating_mixer(sg_u, sg_v, sg_norm[l], sg_w[l], sg_b[l])
        y_c = forgetting_attention(fq, fk, fv, f_logit, fox_b_f[l])
        branches = jnp.stack([y_a, y_b, y_c], axis=2)
        branches = jnp.einsum('bsnc,ncd->bsnd', branches, w_branch[l])
        gates = jax.nn.sigmoid(gate_logits.reshape(bsz, s, N_BRANCH, D_MODEL))
        merged = jnp.einsum('bsnd,bsnd->bsd', gates, branches)
        x = x + merged @ w_out[l]

        h = rmsnorm(x, xa_norm[l])
        m = rmsnorm(mem, mem_norm[l])
        x = x + memory_cross_attention(h, m, xa_wq[l], xa_wk[l], xa_wv[l], xa_wo[l])

        h = rmsnorm(x, ffn2_norm[l])
        x = x + 0.5 * swiglu(h, ffn2_w_gate[l], ffn2_w_up[l], ffn2_w_down[l])
    return rmsnorm(x, final_norm)


import jax as _jax
import jax.numpy as _jnp

TWIN_FORMAT = 'train_step'
FWD_PARAMS = ['x', 'mem', 'ffn1_norm', 'ffn1_w_gate', 'ffn1_w_up', 'ffn1_w_down', 'mix_norm', 'w_in', 'conv_w', 'sg_norm', 'sg_w', 'sg_b', 'fox_b_f', 'w_branch', 'w_out', 'xa_norm', 'mem_norm', 'xa_wq', 'xa_wk', 'xa_wv', 'xa_wo', 'ffn2_norm', 'ffn2_w_gate', 'ffn2_w_up', 'ffn2_w_down', 'final_norm']
TWIN_WEIGHTS = ['ffn1_norm', 'ffn1_w_gate', 'ffn1_w_up', 'ffn1_w_down', 'mix_norm', 'w_in', 'conv_w', 'sg_norm', 'sg_w', 'sg_b', 'fox_b_f', 'w_branch', 'w_out', 'xa_norm', 'mem_norm', 'xa_wq', 'xa_wk', 'xa_wv', 'xa_wo', 'ffn2_norm', 'ffn2_w_gate', 'ffn2_w_up', 'ffn2_w_down', 'final_norm']
TWIN_DIFF_INPUT = 'x'
TWIN_INPUTS = ['x', 'mem', 'ffn1_norm', 'ffn1_w_gate', 'ffn1_w_up', 'ffn1_w_down', 'mix_norm', 'w_in', 'conv_w', 'sg_norm', 'sg_w', 'sg_b', 'fox_b_f', 'w_branch', 'w_out', 'xa_norm', 'mem_norm', 'xa_wq', 'xa_wk', 'xa_wv', 'xa_wo', 'ffn2_norm', 'ffn2_w_gate', 'ffn2_w_up', 'ffn2_w_down', 'final_norm', 'loss_target', 'm_ffn1_norm', 'm_ffn1_w_gate', 'm_ffn1_w_up', 'm_ffn1_w_down', 'm_mix_norm', 'm_w_in', 'm_conv_w', 'm_sg_norm', 'm_sg_w', 'm_sg_b', 'm_fox_b_f', 'm_w_branch', 'm_w_out', 'm_xa_norm', 'm_mem_norm', 'm_xa_wq', 'm_xa_wk', 'm_xa_wv', 'm_xa_wo', 'm_ffn2_norm', 'm_ffn2_w_gate', 'm_ffn2_w_up', 'm_ffn2_w_down', 'm_final_norm', 'v_ffn1_norm', 'v_ffn1_w_gate', 'v_ffn1_w_up', 'v_ffn1_w_down', 'v_mix_norm', 'v_w_in', 'v_conv_w', 'v_sg_norm', 'v_sg_w', 'v_sg_b', 'v_fox_b_f', 'v_w_branch', 'v_w_out', 'v_xa_norm', 'v_mem_norm', 'v_xa_wq', 'v_xa_wk', 'v_xa_wv', 'v_xa_wo', 'v_ffn2_norm', 'v_ffn2_w_gate', 'v_ffn2_w_up', 'v_ffn2_w_down', 'v_final_norm']
TWIN_OUTPUTS = ['loss', 'grad_x', 'grad_ffn1_norm', 'grad_ffn1_w_gate', 'grad_ffn1_w_up', 'grad_ffn1_w_down', 'grad_mix_norm', 'grad_w_in', 'grad_conv_w', 'grad_sg_norm', 'grad_sg_w', 'grad_sg_b', 'grad_fox_b_f', 'grad_w_branch', 'grad_w_out', 'grad_xa_norm', 'grad_mem_norm', 'grad_xa_wq', 'grad_xa_wk', 'grad_xa_wv', 'grad_xa_wo', 'grad_ffn2_norm', 'grad_ffn2_w_gate', 'grad_ffn2_w_up', 'grad_ffn2_w_down', 'grad_final_norm', 'delta_ffn1_norm', 'delta_ffn1_w_gate', 'delta_ffn1_w_up', 'delta_ffn1_w_down', 'delta_mix_norm', 'delta_w_in', 'delta_conv_w', 'delta_sg_norm', 'delta_sg_w', 'delta_sg_b', 'delta_fox_b_f', 'delta_w_branch', 'delta_w_out', 'delta_xa_norm', 'delta_mem_norm', 'delta_xa_wq', 'delta_xa_wk', 'delta_xa_wv', 'delta_xa_wo', 'delta_ffn2_norm', 'delta_ffn2_w_gate', 'delta_ffn2_w_up', 'delta_ffn2_w_down', 'delta_final_norm', 'new_m_ffn1_norm', 'new_m_ffn1_w_gate', 'new_m_ffn1_w_up', 'new_m_ffn1_w_down', 'new_m_mix_norm', 'new_m_w_in', 'new_m_conv_w', 'new_m_sg_norm', 'new_m_sg_w', 'new_m_sg_b', 'new_m_fox_b_f', 'new_m_w_branch', 'new_m_w_out', 'new_m_xa_norm', 'new_m_mem_norm', 'new_m_xa_wq', 'new_m_xa_wk', 'new_m_xa_wv', 'new_m_xa_wo', 'new_m_ffn2_norm', 'new_m_ffn2_w_gate', 'new_m_ffn2_w_up', 'new_m_ffn2_w_down', 'new_m_final_norm', 'new_v_ffn1_norm', 'new_v_ffn1_w_gate', 'new_v_ffn1_w_up', 'new_v_ffn1_w_down', 'new_v_mix_norm', 'new_v_w_in', 'new_v_conv_w', 'new_v_sg_norm', 'new_v_sg_w', 'new_v_sg_b', 'new_v_fox_b_f', 'new_v_w_branch', 'new_v_w_out', 'new_v_xa_norm', 'new_v_mem_norm', 'new_v_xa_wq', 'new_v_xa_wk', 'new_v_xa_wv', 'new_v_xa_wo', 'new_v_ffn2_norm', 'new_v_ffn2_w_gate', 'new_v_ffn2_w_up', 'new_v_ffn2_w_down', 'new_v_final_norm']
TWIN_LEAF_KINDS = {'loss': 'loss', 'grad_x': 'grad_x', 'grad_ffn1_norm': 'grad_w', 'grad_ffn1_w_gate': 'grad_w', 'grad_ffn1_w_up': 'grad_w', 'grad_ffn1_w_down': 'grad_w', 'grad_mix_norm': 'grad_w', 'grad_w_in': 'grad_w', 'grad_conv_w': 'grad_w', 'grad_sg_norm': 'grad_w', 'grad_sg_w': 'grad_w', 'grad_sg_b': 'grad_w', 'grad_fox_b_f': 'grad_w', 'grad_w_branch': 'grad_w', 'grad_w_out': 'grad_w', 'grad_xa_norm': 'grad_w', 'grad_mem_norm': 'grad_w', 'grad_xa_wq': 'grad_w', 'grad_xa_wk': 'grad_w', 'grad_xa_wv': 'grad_w', 'grad_xa_wo': 'grad_w', 'grad_ffn2_norm': 'grad_w', 'grad_ffn2_w_gate': 'grad_w', 'grad_ffn2_w_up': 'grad_w', 'grad_ffn2_w_down': 'grad_w', 'grad_final_norm': 'grad_w', 'delta_ffn1_norm': 'delta_w', 'delta_ffn1_w_gate': 'delta_w', 'delta_ffn1_w_up': 'delta_w', 'delta_ffn1_w_down': 'delta_w', 'delta_mix_norm': 'delta_w', 'delta_w_in': 'delta_w', 'delta_conv_w': 'delta_w', 'delta_sg_norm': 'delta_w', 'delta_sg_w': 'delta_w', 'delta_sg_b': 'delta_w', 'delta_fox_b_f': 'delta_w', 'delta_w_branch': 'delta_w', 'delta_w_out': 'delta_w', 'delta_xa_norm': 'delta_w', 'delta_mem_norm': 'delta_w', 'delta_xa_wq': 'delta_w', 'delta_xa_wk': 'delta_w', 'delta_xa_wv': 'delta_w', 'delta_xa_wo': 'delta_w', 'delta_ffn2_norm': 'delta_w', 'delta_ffn2_w_gate': 'delta_w', 'delta_ffn2_w_up': 'delta_w', 'delta_ffn2_w_down': 'delta_w', 'delta_final_norm': 'delta_w', 'new_m_ffn1_norm': 'new_m', 'new_m_ffn1_w_gate': 'new_m', 'new_m_ffn1_w_up': 'new_m', 'new_m_ffn1_w_down': 'new_m', 'new_m_mix_norm': 'new_m', 'new_m_w_in': 'new_m', 'new_m_conv_w': 'new_m', 'new_m_sg_norm': 'new_m', 'new_m_sg_w': 'new_m', 'new_m_sg_b': 'new_m', 'new_m_fox_b_f': 'new_m', 'new_m_w_branch': 'new_m', 'new_m_w_out': 'new_m', 'new_m_xa_norm': 'new_m', 'new_m_mem_norm': 'new_m', 'new_m_xa_wq': 'new_m', 'new_m_xa_wk': 'new_m', 'new_m_xa_wv': 'new_m', 'new_m_xa_wo': 'new_m', 'new_m_ffn2_norm': 'new_m', 'new_m_ffn2_w_gate': 'new_m', 'new_m_ffn2_w_up': 'new_m', 'new_m_ffn2_w_down': 'new_m', 'new_m_final_norm': 'new_m', 'new_v_ffn1_norm': 'new_v', 'new_v_ffn1_w_gate': 'new_v', 'new_v_ffn1_w_up': 'new_v', 'new_v_ffn1_w_down': 'new_v', 'new_v_mix_norm': 'new_v', 'new_v_w_in': 'new_v', 'new_v_conv_w': 'new_v', 'new_v_sg_norm': 'new_v', 'new_v_sg_w': 'new_v', 'new_v_sg_b': 'new_v', 'new_v_fox_b_f': 'new_v', 'new_v_w_branch': 'new_v', 'new_v_w_out': 'new_v', 'new_v_xa_norm': 'new_v', 'new_v_mem_norm': 'new_v', 'new_v_xa_wq': 'new_v', 'new_v_xa_wk': 'new_v', 'new_v_xa_wv': 'new_v', 'new_v_xa_wo': 'new_v', 'new_v_ffn2_norm': 'new_v', 'new_v_ffn2_w_gate': 'new_v', 'new_v_ffn2_w_up': 'new_v', 'new_v_ffn2_w_down': 'new_v', 'new_v_final_norm': 'new_v'}


def _forward(args):
    return _fwd_reference(*[args[k] for k in FWD_PARAMS])


def _output_shape():
    def fwd():
        inp = _fwd_setup_inputs(0)
        return _fwd_reference(*[inp[k] for k in FWD_PARAMS])
    out = _jax.eval_shape(fwd)
    return out.shape, out.dtype

N_MICROBATCH = 1
ADAM_LR = 0.001
ADAM_B1 = 0.9
ADAM_B2 = 0.999
ADAM_EPS = 1e-08
ADAM_WD = 0.01
ADAM_STEP = 10
PER_EXAMPLE_BATCH_AXIS = {'x': 0, 'mem': 0, 'loss_target': 0}
SHARED_INPUTS = []
_WEIGHT_DTYPES = {'ffn1_norm': _jnp.float32, 'ffn1_w_gate': _jnp.float32, 'ffn1_w_up': _jnp.float32, 'ffn1_w_down': _jnp.float32, 'mix_norm': _jnp.float32, 'w_in': _jnp.float32, 'conv_w': _jnp.float32, 'sg_norm': _jnp.float32, 'sg_w': _jnp.float32, 'sg_b': _jnp.float32, 'fox_b_f': _jnp.float32, 'w_branch': _jnp.float32, 'w_out': _jnp.float32, 'xa_norm': _jnp.float32, 'mem_norm': _jnp.float32, 'xa_wq': _jnp.float32, 'xa_wk': _jnp.float32, 'xa_wv': _jnp.float32, 'xa_wo': _jnp.float32, 'ffn2_norm': _jnp.float32, 'ffn2_w_gate': _jnp.float32, 'ffn2_w_up': _jnp.float32, 'ffn2_w_down': _jnp.float32, 'final_norm': _jnp.float32}
MOMENT_SCALE = {'ffn1_norm': 1.848213e-01, 'ffn1_w_gate': 8.039915e-02, 'ffn1_w_up': 7.803374e-02, 'ffn1_w_down': 1.292636e-01, 'mix_norm': 3.399980e-01, 'w_in': 1.291661e-01, 'conv_w': 2.386548e-01, 'sg_norm': 1.243540e-01, 'sg_w': 1.096346e-01, 'sg_b': 1.775120e-01, 'fox_b_f': 5.365619e-01, 'w_branch': 1.327800e-01, 'w_out': 2.313515e-01, 'xa_norm': 3.219688e-02, 'mem_norm': 5.647435e-02, 'xa_wq': 3.236648e-02, 'xa_wk': 3.222715e-02, 'xa_wv': 3.459577e-02, 'xa_wo': 3.457680e-02, 'ffn2_norm': 1.279105e-01, 'ffn2_w_gate': 5.389226e-02, 'ffn2_w_up': 5.278082e-02, 'ffn2_w_down': 8.776519e-02, 'final_norm': 1.280938e+02}


def _to_microbatches(a, axis):
    t = _jnp.moveaxis(a, axis, 0)
    t = t.reshape((N_MICROBATCH, t.shape[0] // N_MICROBATCH) + t.shape[1:])
    return _jnp.moveaxis(t, 1, axis + 1)


def setup_inputs(seed: int = 0) -> dict:
    inp = _fwd_setup_inputs(seed)
    key = _jax.random.fold_in(_jax.random.key(seed), 7919)
    shape, _ = _output_shape()
    out = dict(inp)
    out["loss_target"] = _jax.random.normal(_jax.random.fold_in(key, 0), shape, _jnp.float32)
    for i, name in enumerate(TWIN_WEIGHTS):
        w = inp[name].astype(_jnp.float32)
        if MOMENT_SCALE is None:
            s = _jnp.sqrt(_jnp.mean(_jnp.square(w)) + 1e-30)
        else:
            s = MOMENT_SCALE[name]
        km, kv = _jax.random.split(_jax.random.fold_in(key, i + 1))
        out[name] = w
        out["m_" + name] = s * _jax.random.normal(km, w.shape, _jnp.float32)
        out["v_" + name] = (s * s) * _jax.random.uniform(kv, w.shape, _jnp.float32, 0.5, 1.5)
    if N_MICROBATCH > 1:
        for name, axis in PER_EXAMPLE_BATCH_AXIS.items():
            out[name] = _to_microbatches(out[name], axis)
    return {'x': out['x'], 'mem': out['mem'], 'ffn1_norm': out['ffn1_norm'], 'ffn1_w_gate': out['ffn1_w_gate'], 'ffn1_w_up': out['ffn1_w_up'], 'ffn1_w_down': out['ffn1_w_down'], 'mix_norm': out['mix_norm'], 'w_in': out['w_in'], 'conv_w': out['conv_w'], 'sg_norm': out['sg_norm'], 'sg_w': out['sg_w'], 'sg_b': out['sg_b'], 'fox_b_f': out['fox_b_f'], 'w_branch': out['w_branch'], 'w_out': out['w_out'], 'xa_norm': out['xa_norm'], 'mem_norm': out['mem_norm'], 'xa_wq': out['xa_wq'], 'xa_wk': out['xa_wk'], 'xa_wv': out['xa_wv'], 'xa_wo': out['xa_wo'], 'ffn2_norm': out['ffn2_norm'], 'ffn2_w_gate': out['ffn2_w_gate'], 'ffn2_w_up': out['ffn2_w_up'], 'ffn2_w_down': out['ffn2_w_down'], 'final_norm': out['final_norm'], 'loss_target': out['loss_target'], 'm_ffn1_norm': out['m_ffn1_norm'], 'm_ffn1_w_gate': out['m_ffn1_w_gate'], 'm_ffn1_w_up': out['m_ffn1_w_up'], 'm_ffn1_w_down': out['m_ffn1_w_down'], 'm_mix_norm': out['m_mix_norm'], 'm_w_in': out['m_w_in'], 'm_conv_w': out['m_conv_w'], 'm_sg_norm': out['m_sg_norm'], 'm_sg_w': out['m_sg_w'], 'm_sg_b': out['m_sg_b'], 'm_fox_b_f': out['m_fox_b_f'], 'm_w_branch': out['m_w_branch'], 'm_w_out': out['m_w_out'], 'm_xa_norm': out['m_xa_norm'], 'm_mem_norm': out['m_mem_norm'], 'm_xa_wq': out['m_xa_wq'], 'm_xa_wk': out['m_xa_wk'], 'm_xa_wv': out['m_xa_wv'], 'm_xa_wo': out['m_xa_wo'], 'm_ffn2_norm': out['m_ffn2_norm'], 'm_ffn2_w_gate': out['m_ffn2_w_gate'], 'm_ffn2_w_up': out['m_ffn2_w_up'], 'm_ffn2_w_down': out['m_ffn2_w_down'], 'm_final_norm': out['m_final_norm'], 'v_ffn1_norm': out['v_ffn1_norm'], 'v_ffn1_w_gate': out['v_ffn1_w_gate'], 'v_ffn1_w_up': out['v_ffn1_w_up'], 'v_ffn1_w_down': out['v_ffn1_w_down'], 'v_mix_norm': out['v_mix_norm'], 'v_w_in': out['v_w_in'], 'v_conv_w': out['v_conv_w'], 'v_sg_norm': out['v_sg_norm'], 'v_sg_w': out['v_sg_w'], 'v_sg_b': out['v_sg_b'], 'v_fox_b_f': out['v_fox_b_f'], 'v_w_branch': out['v_w_branch'], 'v_w_out': out['v_w_out'], 'v_xa_norm': out['v_xa_norm'], 'v_mem_norm': out['v_mem_norm'], 'v_xa_wq': out['v_xa_wq'], 'v_xa_wk': out['v_xa_wk'], 'v_xa_wv': out['v_xa_wv'], 'v_xa_wo': out['v_xa_wo'], 'v_ffn2_norm': out['v_ffn2_norm'], 'v_ffn2_w_gate': out['v_ffn2_w_gate'], 'v_ffn2_w_up': out['v_ffn2_w_up'], 'v_ffn2_w_down': out['v_ffn2_w_down'], 'v_final_norm': out['v_final_norm']}


def _loss(weights, diff, rest, loss_target):
    with _jax.named_scope("forward"):
        args = {**rest, TWIN_DIFF_INPUT: diff, **{k: w.astype(_WEIGHT_DTYPES[k]) for k, w in weights.items()}}
        y = _forward(args)
    with _jax.named_scope("loss_head"):
        err = _jnp.square(y.astype(_jnp.float32) - loss_target)
        return 0.5 * _jnp.sum(_jnp.mean(err, axis=-1)) if err.ndim else 0.5 * err


def _adamw(w, g, m, v):
    m = ADAM_B1 * m + (1.0 - ADAM_B1) * g
    v = ADAM_B2 * v + (1.0 - ADAM_B2) * _jnp.square(g)
    m_hat = m / (1.0 - ADAM_B1 ** ADAM_STEP)
    v_hat = v / (1.0 - ADAM_B2 ** ADAM_STEP)
    delta = -ADAM_LR * (m_hat / (_jnp.sqrt(v_hat) + ADAM_EPS) + ADAM_WD * w)
    return delta, m, v


def reference(x, mem, ffn1_norm, ffn1_w_gate, ffn1_w_up, ffn1_w_down, mix_norm, w_in, conv_w, sg_norm, sg_w, sg_b, fox_b_f, w_branch, w_out, xa_norm, mem_norm, xa_wq, xa_wk, xa_wv, xa_wo, ffn2_norm, ffn2_w_gate, ffn2_w_up, ffn2_w_down, final_norm, loss_target, m_ffn1_norm, m_ffn1_w_gate, m_ffn1_w_up, m_ffn1_w_down, m_mix_norm, m_w_in, m_conv_w, m_sg_norm, m_sg_w, m_sg_b, m_fox_b_f, m_w_branch, m_w_out, m_xa_norm, m_mem_norm, m_xa_wq, m_xa_wk, m_xa_wv, m_xa_wo, m_ffn2_norm, m_ffn2_w_gate, m_ffn2_w_up, m_ffn2_w_down, m_final_norm, v_ffn1_norm, v_ffn1_w_gate, v_ffn1_w_up, v_ffn1_w_down, v_mix_norm, v_w_in, v_conv_w, v_sg_norm, v_sg_w, v_sg_b, v_fox_b_f, v_w_branch, v_w_out, v_xa_norm, v_mem_norm, v_xa_wq, v_xa_wk, v_xa_wv, v_xa_wo, v_ffn2_norm, v_ffn2_w_gate, v_ffn2_w_up, v_ffn2_w_down, v_final_norm):
    given = dict(x=x, mem=mem, ffn1_norm=ffn1_norm, ffn1_w_gate=ffn1_w_gate, ffn1_w_up=ffn1_w_up, ffn1_w_down=ffn1_w_down, mix_norm=mix_norm, w_in=w_in, conv_w=conv_w, sg_norm=sg_norm, sg_w=sg_w, sg_b=sg_b, fox_b_f=fox_b_f, w_branch=w_branch, w_out=w_out, xa_norm=xa_norm, mem_norm=mem_norm, xa_wq=xa_wq, xa_wk=xa_wk, xa_wv=xa_wv, xa_wo=xa_wo, ffn2_norm=ffn2_norm, ffn2_w_gate=ffn2_w_gate, ffn2_w_up=ffn2_w_up, ffn2_w_down=ffn2_w_down, final_norm=final_norm, loss_target=loss_target, m_ffn1_norm=m_ffn1_norm, m_ffn1_w_gate=m_ffn1_w_gate, m_ffn1_w_up=m_ffn1_w_up, m_ffn1_w_down=m_ffn1_w_down, m_mix_norm=m_mix_norm, m_w_in=m_w_in, m_conv_w=m_conv_w, m_sg_norm=m_sg_norm, m_sg_w=m_sg_w, m_sg_b=m_sg_b, m_fox_b_f=m_fox_b_f, m_w_branch=m_w_branch, m_w_out=m_w_out, m_xa_norm=m_xa_norm, m_mem_norm=m_mem_norm, m_xa_wq=m_xa_wq, m_xa_wk=m_xa_wk, m_xa_wv=m_xa_wv, m_xa_wo=m_xa_wo, m_ffn2_norm=m_ffn2_norm, m_ffn2_w_gate=m_ffn2_w_gate, m_ffn2_w_up=m_ffn2_w_up, m_ffn2_w_down=m_ffn2_w_down, m_final_norm=m_final_norm, v_ffn1_norm=v_ffn1_norm, v_ffn1_w_gate=v_ffn1_w_gate, v_ffn1_w_up=v_ffn1_w_up, v_ffn1_w_down=v_ffn1_w_down, v_mix_norm=v_mix_norm, v_w_in=v_w_in, v_conv_w=v_conv_w, v_sg_norm=v_sg_norm, v_sg_w=v_sg_w, v_sg_b=v_sg_b, v_fox_b_f=v_fox_b_f, v_w_branch=v_w_branch, v_w_out=v_w_out, v_xa_norm=v_xa_norm, v_mem_norm=v_mem_norm, v_xa_wq=v_xa_wq, v_xa_wk=v_xa_wk, v_xa_wv=v_xa_wv, v_xa_wo=v_xa_wo, v_ffn2_norm=v_ffn2_norm, v_ffn2_w_gate=v_ffn2_w_gate, v_ffn2_w_up=v_ffn2_w_up, v_ffn2_w_down=v_ffn2_w_down, v_final_norm=v_final_norm)
    weights = {n: given[n] for n in TWIN_WEIGHTS}
    shared = {n: given[n] for n in SHARED_INPUTS}
    per_example = {n: given[n] for n in ['x', 'mem']}
    grad_fn = _jax.value_and_grad(_loss, argnums=(0, 1))

    def one_microbatch(ex, loss_target):
        ex = dict(ex)
        diff = ex.pop(TWIN_DIFF_INPUT)
        return grad_fn(weights, diff, {**shared, **ex}, loss_target)

    if N_MICROBATCH == 1:
        loss, (grad_w, grad_x) = one_microbatch(per_example, given["loss_target"])
    else:
        def body(carry, xs):
            loss_sum, grad_sum = carry
            l_k, (gw_k, gx_k) = one_microbatch(xs[0], xs[1])
            with _jax.named_scope("update"):
                return (loss_sum + l_k, _jax.tree.map(_jnp.add, grad_sum, gw_k)), gx_k

        init = (_jnp.zeros((), _jnp.float32), _jax.tree.map(_jnp.zeros_like, weights))
        (loss, grad_w), grad_x = _jax.lax.scan(body, init, (per_example, given["loss_target"]))
    with _jax.named_scope("update"):
        delta_w, new_m, new_v = {}, {}, {}
        for n in TWIN_WEIGHTS:
            delta_w[n], new_m[n], new_v[n] = _adamw(weights[n], grad_w[n], given["m_" + n], given["v_" + n])
    return (loss, grad_x, *[grad_w[n] for n in TWIN_WEIGHTS], *[delta_w[n] for n in TWIN_WEIGHTS],
            *[new_m[n] for n in TWIN_WEIGHTS], *[new_v[n] for n in TWIN_WEIGHTS])
```

```python
import functools

import jax
import jax.numpy as jnp
from jax import lax
from jax.experimental import pallas as pl
from jax.experimental.pallas import tpu as pltpu

F32, BF16 = jnp.float32, jnp.bfloat16
MESH_ID = pl.DeviceIdType.MESH

D_MODEL = 1024
DEPTH = 2
D_FF = 2816
CONV_W = 512
SG_W = 512
SG_G = 4
CHUNK = 128
FOX_H = 8
FOX_D = 64
FOX_W = FOX_H * FOX_D
XA_H = 4
XA_D = D_MODEL // XA_H
N_CHIPS = 4
RMS_EPS = 1e-6
W1_COLS = 3 * CONV_W + 2 * SG_W + 3 * FOX_W
LANES = 128
HALO = 16

ADAM_LR, ADAM_B1, ADAM_B2, ADAM_EPS, ADAM_WD, ADAM_STEP = 0.001, 0.9, 0.999, 1e-08, 0.01, 10

ROW_BLOCK = 256
FLASH_BLOCK = 512
FF_SPLIT = 1408
VMEM_LIMIT = 56 * 1024 * 1024
PACK_COLS = 1024
PACK_ROW_BLOCK = 256

BIG = (
    ("ffn1_w_gate", 2), ("ffn1_w_up", 2), ("ffn1_w_down", 1), ("w_in", 2), ("w_branch", 3), ("w_out", 1),
    ("xa_wq", 1), ("xa_wk", 1), ("xa_wv", 1), ("xa_wo", 1), ("ffn2_w_gate", 2), ("ffn2_w_up", 2), ("ffn2_w_down", 1),
)
SMALL = ("ffn1_norm", "mix_norm", "conv_w", "sg_norm", "sg_w", "sg_b", "fox_b_f", "xa_norm", "mem_norm", "ffn2_norm",
         "final_norm")
WEIGHTS = ("ffn1_norm", "ffn1_w_gate", "ffn1_w_up", "ffn1_w_down", "mix_norm", "w_in", "conv_w", "sg_norm", "sg_w",
           "sg_b", "fox_b_f", "w_branch", "w_out", "xa_norm", "mem_norm", "xa_wq", "xa_wk", "xa_wv", "xa_wo",
           "ffn2_norm", "ffn2_w_gate", "ffn2_w_up", "ffn2_w_down", "final_norm")


def _dot(a, b):
    return jnp.dot(a, b, preferred_element_type=F32)


def _dot_nt(a, b):
    return lax.dot_general(a, b, (((1,), (1,)), ((), ())), preferred_element_type=F32)


def _dot_tn(a, b):
    return lax.dot_general(a, b, (((0,), (0,)), ((), ())), preferred_element_type=F32)


def _rms_stats(x):
    r = lax.rsqrt(jnp.mean(x * x, axis=-1, keepdims=True) + RMS_EPS)
    return x * r, r


def _rms_bwd(xh, r, g, dy):
    dg = jnp.sum(dy * xh, axis=0, keepdims=True)
    dxh = dy * g
    dx = r * (dxh - xh * jnp.mean(dxh * xh, axis=-1, keepdims=True))
    return dx, dg


def _gelu(x):
    k = 0.7978845608028654
    t = jnp.tanh(k * (x + 0.044715 * x * x * x))
    return 0.5 * x * (1.0 + t), t


def _gelu_grad(x, t):
    k = 0.7978845608028654
    return 0.5 * (1.0 + t) + 0.5 * x * (1.0 - t * t) * k * (1.0 + 3.0 * 0.044715 * x * x)


def _split3_dot(tri, x):
    hi = x.astype(BF16)
    r1 = x - hi.astype(F32)
    mid = r1.astype(BF16)
    lo = (r1 - mid.astype(F32)).astype(BF16)
    return _dot(tri, hi) + _dot(tri, mid) + _dot(tri, lo)


def _params(n_grid):
    return pltpu.CompilerParams(dimension_semantics=("arbitrary",) * n_grid, vmem_limit_bytes=VMEM_LIMIT)


def _full_spec(shape):
    nd = len(shape)
    return pl.BlockSpec(tuple(shape), lambda *_: (0,) * nd)


def _rowcall(name, body, n_tokens, tm, rows, consts, residents, row_outs, acc_outs, scratch=()):
    n = n_tokens // tm
    rows = [r if isinstance(r, tuple) else (r, pl.BlockSpec((tm, r.shape[1]), lambda i: (i, 0))) for r in rows]
    n_in, n_w = len(rows) + len(consts), len(residents)
    n_out = len(row_outs) + len(acc_outs)
    in_specs = ([r[1] for r in rows] + [_full_spec(c.shape) for c in consts]
                + [pl.BlockSpec(memory_space=pl.ANY)] * n_w)
    out_shape = ([jax.ShapeDtypeStruct((n_tokens, c), dt) for c, dt in row_outs]
                 + [jax.ShapeDtypeStruct(s, dt) for s, dt in acc_outs])
    out_specs = ([pl.BlockSpec((tm, c), lambda i: (i, 0)) for c, _ in row_outs]
                 + [_full_spec(s) for s, _ in acc_outs])
    scratch_shapes = [pltpu.VMEM(w.shape, w.dtype) for w in residents]
    if n_w:
        scratch_shapes.append(pltpu.SemaphoreType.DMA((n_w,)))
    scratch_shapes += list(scratch)

    def kern(*refs):
        ins, w_hbm = refs[:n_in], refs[n_in:n_in + n_w]
        outs = refs[n_in + n_w:n_in + n_w + n_out]
        rest = refs[n_in + n_w + n_out:]
        w_vmem = rest[:n_w]
        extra = rest[n_w + 1:] if n_w else rest
        i = pl.program_id(0)

        @pl.when(i == 0)
        def _():
            copies = [pltpu.make_async_copy(w_hbm[k], w_vmem[k], rest[n_w].at[k]) for k in range(n_w)]
            for cp in copies:
                cp.start()
            for cp in copies:
                cp.wait()
            for a in outs[len(row_outs):]:
                a[...] = jnp.zeros(a.shape, a.dtype)

        body(i, n, *ins, *w_vmem, *outs, *extra)

    res = pl.pallas_call(
        kern, name=name, grid=(n,), in_specs=in_specs, out_specs=out_specs, out_shape=out_shape,
        scratch_shapes=scratch_shapes, compiler_params=_params(1),
    )(*[r[0] for r in rows], *consts, *residents)
    return res


def _mm_tn(name, x, y, k_dim, n_dim, *, tk, tn, x_off=0, y_off=0, tt=512):
    n_tok = x.shape[0]
    tt = min(tt, n_tok)
    n_t = n_tok // tt
    xb, yb = x_off // tk, y_off // tn

    def kern(x_ref, y_ref, o_ref):
        @pl.when(pl.program_id(2) == 0)
        def _():
            o_ref[...] = jnp.zeros(o_ref.shape, F32)

        o_ref[...] += _dot_tn(x_ref[...], y_ref[...])

    return pl.pallas_call(
        kern, name=name, grid=(k_dim // tk, n_dim // tn, n_t),
        in_specs=[pl.BlockSpec((tt, tk), lambda k, n, t: (t, xb + k)),
                  pl.BlockSpec((tt, tn), lambda k, n, t: (t, yb + n))],
        out_specs=pl.BlockSpec((tk, tn), lambda k, n, t: (k, n)),
        out_shape=jax.ShapeDtypeStruct((k_dim, n_dim), F32),
        compiler_params=_params(3),
    )(x, y)


def _ffn_fwd(name, x, gain, wg, wu, wd):
    n_tok, tm = x.shape[0], min(ROW_BLOCK, x.shape[0])

    def body(i, n, x_ref, g_ref, wg_ref, wu_ref, wd_ref, xo_ref, a_ref, b_ref):
        x_v = x_ref[...]
        xh, _ = _rms_stats(x_v)
        h = (xh * g_ref[...]).astype(BF16)
        y = jnp.zeros((tm, D_MODEL), F32)
        for f0 in range(0, D_FF, FF_SPLIT):
            sl = slice(f0, f0 + FF_SPLIT)
            a = _dot(h, wg_ref[:, sl])
            b = _dot(h, wu_ref[:, sl])
            a_ref[:, sl] = a.astype(BF16)
            b_ref[:, sl] = b.astype(BF16)
            s = (a * jax.nn.sigmoid(a) * b).astype(BF16)
            y = y + _dot(s, wd_ref[sl, :])
        xo_ref[...] = x_v + 0.5 * y

    return _rowcall(name, body, n_tok, tm, [x], [gain], [wg, wu, wd],
                    [(D_MODEL, F32), (D_FF, BF16), (D_FF, BF16)], [])


def _ffn_bwd(name, x, dxo, a, b, gain, wd_t, wg_t, wu_t):
    n_tok, tm = x.shape[0], min(ROW_BLOCK, x.shape[0])

    def body(i, n, x_ref, dxo_ref, a_ref, b_ref, g_ref, wdt_ref, wgt_ref, wut_ref,
             dx_ref, da_ref, db_ref, s_ref, h_ref, dy_ref, dg_ref):
        g = g_ref[...]
        xh, r = _rms_stats(x_ref[...])
        h_ref[...] = (xh * g).astype(BF16)
        dxo_v = dxo_ref[...]
        dy = (0.5 * dxo_v).astype(BF16)
        dy_ref[...] = dy
        dh = jnp.zeros((tm, D_MODEL), F32)
        for f0 in range(0, D_FF, FF_SPLIT):
            sl = slice(f0, f0 + FF_SPLIT)
            a_v = a_ref[:, sl].astype(F32)
            b_v = b_ref[:, sl].astype(F32)
            ds = _dot(dy, wdt_ref[:, sl])
            sig = jax.nn.sigmoid(a_v)
            sa = a_v * sig
            s_ref[:, sl] = (sa * b_v).astype(BF16)
            da = (ds * b_v * (sig * (1.0 + a_v * (1.0 - sig)))).astype(BF16)
            db = (ds * sa).astype(BF16)
            da_ref[:, sl] = da
            db_ref[:, sl] = db
            dh = dh + _dot(da, wgt_ref[sl, :]) + _dot(db, wut_ref[sl, :])
        dx, dg = _rms_bwd(xh, r, g, dh)
        dx_ref[...] = dxo_v + dx
        dg_ref[...] += dg

    return _rowcall(name, body, n_tok, tm, [x, dxo, a, b], [gain], [wd_t, wg_t, wu_t],
                    [(D_MODEL, F32), (D_FF, BF16), (D_FF, BF16), (D_FF, BF16), (D_MODEL, BF16), (D_MODEL, BF16)],
                    [((1, D_MODEL), F32)])


def _proj_fwd(name, x, gain, w1, wf, wgl):
    n_tok, tm = x.shape[0], min(ROW_BLOCK, x.shape[0])
    c0, c1 = 3 * CONV_W, 3 * CONV_W + 2 * SG_W

    def body(i, n, x_ref, g_ref, w1_ref, wf_ref, wgl_ref, cin_ref, sg_ref, qkv_ref, fl_ref, gl_ref):
        xh, _ = _rms_stats(x_ref[...])
        h = (xh * g_ref[...]).astype(BF16)
        cin_ref[...] = _dot(h, w1_ref[:, 0:c0]).astype(BF16)
        sg_ref[...] = _dot(h, w1_ref[:, c0:c1]).astype(BF16)
        qkv_ref[...] = _dot(h, w1_ref[:, c1:W1_COLS]).astype(BF16)
        fl_ref[...] = _dot(h, wf_ref[...])
        gl_ref[...] = _dot(h, wgl_ref[...]).astype(BF16)

    return _rowcall(name, body, n_tok, tm, [x], [gain], [w1, wf, wgl],
                    [(3 * CONV_W, BF16), (2 * SG_W, BF16), (3 * FOX_W, BF16), (LANES, F32), (3 * D_MODEL, BF16)], [])


def _proj_bwd(name, x, dxin, dcin, dsg, dqkv, dfl, dgl, gain, w1_t, wf_t, wgl_t):
    n_tok, tm = x.shape[0], min(ROW_BLOCK, x.shape[0])
    c0, c1 = 3 * CONV_W, 3 * CONV_W + 2 * SG_W

    def body(i, n, x_ref, dxin_ref, dcin_ref, dsg_ref, dqkv_ref, dfl_ref, dgl_ref, g_ref, w1t_ref, wft_ref, wglt_ref,
             dx_ref, h_ref, dfb_ref, dg_ref):
        g = g_ref[...]
        xh, r = _rms_stats(x_ref[...])
        h_ref[...] = (xh * g).astype(BF16)
        dfb = dfl_ref[...].astype(BF16)
        dfb_ref[...] = dfb
        dh = _dot(dcin_ref[...], w1t_ref[0:c0, :])
        dh = dh + _dot(dsg_ref[...], w1t_ref[c0:c1, :])
        dh = dh + _dot(dqkv_ref[...], w1t_ref[c1:W1_COLS, :])
        dh = dh + _dot(dfb, wft_ref[...])
        dh = dh + _dot(dgl_ref[...], wglt_ref[...])
        dx, dg = _rms_bwd(xh, r, g, dh)
        dx_ref[...] = dxin_ref[...] + dx
        dg_ref[...] += dg

    return _rowcall(name, body, n_tok, tm, [x, dxin, dcin, dsg, dqkv, dfl, dgl], [gain], [w1_t, wf_t, wgl_t],
                    [(D_MODEL, F32), (D_MODEL, BF16), (LANES, BF16)], [((1, D_MODEL), F32)])


def _conv_taps(z, prev_z, i):
    tm = z.shape[0]
    row = lax.broadcasted_iota(jnp.int32, (tm, 1), 0)
    live = (i > 0).astype(F32)
    p1, p2 = prev_z[HALO - 1:HALO, :] * live, prev_z[HALO - 2:HALO - 1, :] * live
    z1 = jnp.where(row == 0, p1, pltpu.roll(z, 1, 0))
    z2 = jnp.where(row == 0, p2, jnp.where(row == 1, p1, pltpu.roll(z, 2, 0)))
    return z1, z2


def _prev_spec(tm, cols):
    return pl.BlockSpec((HALO, cols), lambda i: (jnp.maximum(i * (tm // HALO) - 1, 0), 0))


def _next_spec(tm, cols, n):
    last = n * (tm // HALO) - 1
    return pl.BlockSpec((HALO, cols), lambda i: (jnp.minimum((i + 1) * (tm // HALO), last), 0))


def _conv_fwd(name, cin, cw):
    n_tok, tm = cin.shape[0], min(ROW_BLOCK, cin.shape[0])
    w = CONV_W

    def body(i, n, c_ref, p_ref, cw_ref, ya_ref):
        c_v, p_v = c_ref[...].astype(F32), p_ref[...].astype(F32)
        z = c_v[:, w:2 * w] * c_v[:, 2 * w:]
        z1, z2 = _conv_taps(z, p_v[:, w:2 * w] * p_v[:, 2 * w:], i)
        y = cw_ref[0:1, :] * z2 + cw_ref[1:2, :] * z1 + cw_ref[2:3, :] * z
        ya_ref[...] = (c_v[:, 0:w] * y).astype(BF16)

    return _rowcall(name, body, n_tok, tm, [cin, (cin, _prev_spec(tm, 3 * w))], [cw], [], [(w, BF16)], [])[0]


def _conv_bwd(name, cin, dya, cw):
    n_tok, tm = cin.shape[0], min(ROW_BLOCK, cin.shape[0])
    w = CONV_W
    n_blocks = n_tok // tm

    def body(i, n, c_ref, p_ref, nx_ref, dya_ref, ndya_ref, cw_ref, dc_ref, dcw_ref):
        c_v, p_v = c_ref[...].astype(F32), p_ref[...].astype(F32)
        ab, ac, ah = c_v[:, 0:w], c_v[:, w:2 * w], c_v[:, 2 * w:]
        z = ac * ah
        z1, z2 = _conv_taps(z, p_v[:, w:2 * w] * p_v[:, 2 * w:], i)
        w0, w1, w2 = cw_ref[0:1, :], cw_ref[1:2, :], cw_ref[2:3, :]
        y = w0 * z2 + w1 * z1 + w2 * z
        dya_v = dya_ref[...].astype(F32)
        dy = dya_v * ab
        live = (i < n - 1).astype(F32)
        ndy = ndya_ref[...].astype(F32) * nx_ref[:, 0:w].astype(F32) * live
        row = lax.broadcasted_iota(jnp.int32, (tm, 1), 0)
        dy1 = jnp.where(row == tm - 1, ndy[0:1, :], pltpu.roll(dy, tm - 1, 0))
        dy2 = jnp.where(row == tm - 1, ndy[1:2, :], jnp.where(row == tm - 2, ndy[0:1, :], pltpu.roll(dy, tm - 2, 0)))
        dz = w2 * dy + w1 * dy1 + w0 * dy2
        dc_ref[:, 0:w] = (dya_v * y).astype(BF16)
        dc_ref[:, w:2 * w] = (dz * ah).astype(BF16)
        dc_ref[:, 2 * w:] = (dz * ac).astype(BF16)
        dcw_ref[0:1, :] += jnp.sum(dy * z2, axis=0, keepdims=True)
        dcw_ref[1:2, :] += jnp.sum(dy * z1, axis=0, keepdims=True)
        dcw_ref[2:3, :] += jnp.sum(dy * z, axis=0, keepdims=True)

    return _rowcall(name, body, n_tok, tm,
                    [cin, (cin, _prev_spec(tm, 3 * w)), (cin, _next_spec(tm, 3 * w, n_blocks)),
                     dya, (dya, _next_spec(tm, w, n_blocks))],
                    [cw], [], [(3 * w, BF16)], [((8, w), F32)])


def _sg_common(sg_ref, gn_ref):
    s_v = sg_ref[...].astype(F32)
    u, v = s_v[:, 0:SG_W], s_v[:, SG_W:]
    ug, tu = _gelu(u)
    vg, tv = _gelu(v)
    vh, r = _rms_stats(vg)
    vn = (vh * gn_ref[...]).astype(BF16)
    return u, v, ug, tu, tv, vh, r, vn


def _sg_fwd(name, sgin, gn, sgw, bias_full):
    n_tok, tm = sgin.shape[0], min(ROW_BLOCK, sgin.shape[0])

    def body(i, n, sg_ref, gn_ref, w_ref, bias_ref, yb_ref):
        _, _, ug, _, _, _, _, vn = _sg_common(sg_ref, gn_ref)
        tril = lax.broadcasted_iota(jnp.int32, (CHUNK, CHUNK), 0) >= lax.broadcasted_iota(jnp.int32, (CHUNK, CHUNK), 1)
        wt = [jnp.where(tril, w_ref[g], 0.0).astype(BF16) for g in range(SG_G)]
        for c0 in range(0, tm, CHUNK):
            sv = jnp.concatenate(
                [_dot(wt[g], vn[c0:c0 + CHUNK, g * CHUNK:(g + 1) * CHUNK]) for g in range(SG_G)], axis=1)
            sv = sv + bias_ref[...]
            yb_ref[c0:c0 + CHUNK, :] = (ug[c0:c0 + CHUNK, :] * sv).astype(BF16)

    return _rowcall(name, body, n_tok, tm, [sgin], [gn, sgw, bias_full], [], [(SG_W, BF16)], [])[0]


def _sg_bwd(name, sgin, dyb, gn, sgw, sgw_t, bias_full):
    n_tok, tm = sgin.shape[0], min(ROW_BLOCK, sgin.shape[0])

    def body(i, n, sg_ref, dyb_ref, gn_ref, w_ref, wt_ref, bias_ref, dsg_ref, dgn_ref, dw_ref, db_ref):
        u, v, ug, tu, tv, vh, r, vn = _sg_common(sg_ref, gn_ref)
        r0 = lax.broadcasted_iota(jnp.int32, (CHUNK, CHUNK), 0)
        r1 = lax.broadcasted_iota(jnp.int32, (CHUNK, CHUNK), 1)
        wt = [jnp.where(r0 >= r1, w_ref[g], 0.0).astype(BF16) for g in range(SG_G)]
        wtt = [jnp.where(r0 <= r1, wt_ref[g], 0.0).astype(BF16) for g in range(SG_G)]
        dyb_v = dyb_ref[...].astype(F32)
        dvn_rows = []
        for c0 in range(0, tm, CHUNK):
            rows = slice(c0, c0 + CHUNK)
            svs, dvns = [], []
            dsv = dyb_v[rows, :] * ug[rows, :]
            for g in range(SG_G):
                cols = slice(g * CHUNK, (g + 1) * CHUNK)
                svs.append(_dot(wt[g], vn[rows, cols]))
                dsv_g = dsv[:, cols]
                dsv_b = dsv_g.astype(BF16)
                dvns.append(_dot(wtt[g], dsv_b))
                dw_ref[g] += jnp.where(r0 >= r1, _dot_nt(dsv_b, vn[rows, cols]), 0.0)
                db_ref[g] += jnp.broadcast_to(jnp.sum(dsv_g, axis=1, keepdims=True), (CHUNK, CHUNK))
            sv = jnp.concatenate(svs, axis=1) + bias_ref[...]
            dug = dyb_v[rows, :] * sv
            dsg_ref[rows, 0:SG_W] = (dug * _gelu_grad(u[rows, :], tu[rows, :])).astype(BF16)
            dvn_rows.append(jnp.concatenate(dvns, axis=1))
        dvn = jnp.concatenate(dvn_rows, axis=0)
        dvg, dgn = _rms_bwd(vh, r, gn_ref[...], dvn)
        dsg_ref[:, SG_W:] = (dvg * _gelu_grad(v, tv)).astype(BF16)
        dgn_ref[...] += dgn

    return _rowcall(name, body, n_tok, tm, [sgin, dyb], [gn, sgw, sgw_t, bias_full], [], [(2 * SG_W, BF16)],
                    [((1, SG_W), F32), ((SG_G, CHUNK, CHUNK), F32), ((SG_G, CHUNK, CHUNK), F32)])


def _forget_cumsum(name, fl, bf):
    n_tok, tm = fl.shape[0], min(ROW_BLOCK, fl.shape[0])

    def body(i, n, fl_ref, b_ref, c_ref, carry):
        @pl.when(i == 0)
        def _():
            carry[...] = jnp.zeros(carry.shape, F32)

        z = fl_ref[...] + b_ref[...]
        lf = jnp.minimum(z, 0.0) - jnp.log1p(jnp.exp(-jnp.abs(z)))
        tri = (lax.broadcasted_iota(jnp.int32, (tm, tm), 0) >= lax.broadcasted_iota(jnp.int32, (tm, tm), 1)).astype(BF16)
        c = _split3_dot(tri, lf) + carry[...]
        c_ref[...] = c
        carry[...] = c[tm - 1:tm, :]

    return _rowcall(name, body, n_tok, tm, [fl], [bf], [], [(LANES, F32)], [], scratch=[pltpu.VMEM((1, LANES), F32)])[0]


def _forget_bwd(name, dc, fl, bf):
    n_tok, tm = fl.shape[0], min(ROW_BLOCK, fl.shape[0])
    n = n_tok // tm
    rev = pl.BlockSpec((tm, LANES), lambda i: (n - 1 - i, 0))

    def kern(dc_ref, fl_ref, b_ref, dfl_ref, db_ref, carry):
        @pl.when(pl.program_id(0) == 0)
        def _():
            carry[...] = jnp.zeros(carry.shape, F32)
            db_ref[...] = jnp.zeros(db_ref.shape, F32)

        triu = (lax.broadcasted_iota(jnp.int32, (tm, tm), 0) <= lax.broadcasted_iota(jnp.int32, (tm, tm), 1)).astype(BF16)
        dlf = _split3_dot(triu, dc_ref[...]) + carry[...]
        carry[...] = dlf[0:1, :]
        z = fl_ref[...] + b_ref[...]
        dfl = dlf * jax.nn.sigmoid(-z)
        dfl_ref[...] = dfl
        db_ref[...] += jnp.sum(dfl, axis=0, keepdims=True)

    return pl.pallas_call(
        kern, name=name, grid=(n,), in_specs=[rev, rev, _full_spec((1, LANES))],
        out_specs=[rev, _full_spec((1, LANES))],
        out_shape=[jax.ShapeDtypeStruct((n_tok, LANES), F32), jax.ShapeDtypeStruct((1, LANES), F32)],
        scratch_shapes=[pltpu.VMEM((1, LANES), F32)], compiler_params=_params(1),
    )(dc, fl, bf)


def _flash_logits(q, k, cq, ck, i, j, blk):
    s = _dot_nt(q, k) * (FOX_D ** -0.5) + cq - ck
    row = i * blk + lax.broadcasted_iota(jnp.int32, (blk, blk), 0)
    col = j * blk + lax.broadcasted_iota(jnp.int32, (blk, blk), 1)
    return jnp.where(col <= row, s, -jnp.inf)


def _lane_pick(c_blk, h):
    lane = lax.broadcasted_iota(jnp.int32, c_blk.shape, 1)
    return jnp.sum(jnp.where(lane == h, c_blk, 0.0), axis=1, keepdims=True)


def _flash_fwd(name, q, k, v, c, ct):
    n_h, n_tok, dh = v.shape
    blk = min(FLASH_BLOCK, n_tok)
    nb = n_tok // blk

    def kern(q_ref, k_ref, v_ref, c_ref, ct_ref, o_ref, lse_ref, m_s, l_s, acc_s, cq_s):
        h, i, j = pl.program_id(0), pl.program_id(1), pl.program_id(2)

        @pl.when(j == 0)
        def _():
            m_s[...] = jnp.full(m_s.shape, -jnp.inf, F32)
            l_s[...] = jnp.zeros(l_s.shape, F32)
            acc_s[...] = jnp.zeros(acc_s.shape, F32)
            cq_s[...] = _lane_pick(c_ref[...], h)

        @pl.when(j <= i)
        def _():
            s = _flash_logits(q_ref[0], k_ref[0], cq_s[...], ct_ref[0], i, j, blk)
            m_new = jnp.maximum(m_s[...], jnp.max(s, axis=1, keepdims=True))
            alpha = jnp.exp(m_s[...] - m_new)
            p = jnp.exp(s - m_new)
            l_s[...] = alpha * l_s[...] + jnp.sum(p, axis=1, keepdims=True)
            acc_s[...] = alpha * acc_s[...] + _dot(p.astype(BF16), v_ref[0])
            m_s[...] = m_new

        @pl.when(j == i)
        def _():
            o_ref[0] = (acc_s[...] / l_s[...]).astype(BF16)
            lse_ref[0] = jnp.broadcast_to(m_s[...] + jnp.log(l_s[...]), (blk, LANES))

    def qs(w):
        return pl.BlockSpec((1, blk, w), lambda h, i, j: (h, i, 0))

    def ks(w):
        return pl.BlockSpec((1, blk, w), lambda h, i, j: (h, jnp.minimum(i, j), 0))

    return pl.pallas_call(
        kern, name=name, grid=(n_h, nb, nb),
        in_specs=[qs(LANES), ks(LANES), ks(dh), pl.BlockSpec((blk, LANES), lambda h, i, j: (i, 0)),
                  pl.BlockSpec((1, 1, blk), lambda h, i, j: (h, 0, jnp.minimum(i, j)))],
        out_specs=[qs(dh), qs(LANES)],
        out_shape=[jax.ShapeDtypeStruct((n_h, n_tok, dh), BF16), jax.ShapeDtypeStruct((n_h, n_tok, LANES), F32)],
        scratch_shapes=[pltpu.VMEM((blk, 1), F32), pltpu.VMEM((blk, 1), F32), pltpu.VMEM((blk, dh), F32),
                        pltpu.VMEM((blk, 1), F32)],
        compiler_params=_params(3),
    )(q, k, v, c, ct)


def _flash_bwd(name, q, k, v, o, do, lse, c, ct):
    n_h, n_tok, dh = v.shape
    blk = min(FLASH_BLOCK, n_tok)
    nb = n_tok // blk
    scale = FOX_D ** -0.5

    def kern(q_ref, k_ref, v_ref, o_ref, do_ref, lse_ref, c_ref, ct_ref, dq_ref, dk_ref, dv_ref, dct_ref):
        h, j, i = pl.program_id(0), pl.program_id(1), pl.program_id(2)

        @pl.when((j == 0) & (i == 0))
        def _():
            dq_ref[...] = jnp.zeros(dq_ref.shape, F32)

        @pl.when(i == 0)
        def _():
            dk_ref[...] = jnp.zeros(dk_ref.shape, F32)
            dv_ref[...] = jnp.zeros(dv_ref.shape, F32)
            dct_ref[...] = jnp.zeros(dct_ref.shape, F32)

        @pl.when(i >= j)
        def _():
            q_v, k_v, do_v = q_ref[0], k_ref[0], do_ref[0]
            s = _flash_logits(q_v, k_v, _lane_pick(c_ref[...], h), ct_ref[0], i, j, blk)
            p = jnp.exp(s - lse_ref[0][:, 0:1])
            delta = jnp.sum(do_v.astype(F32) * o_ref[0].astype(F32), axis=1, keepdims=True)
            dv_ref[0] += _dot_tn(p.astype(BF16), do_v)
            ds = p * (_dot_nt(do_v, v_ref[0]) - delta)
            dct_ref[0] -= jnp.sum(ds, axis=0, keepdims=True)
            ds_b = (ds * scale).astype(BF16)
            rows = pl.ds(pl.multiple_of(i * blk, blk), blk)
            lane = lax.broadcasted_iota(jnp.int32, (blk, LANES), 1)
            dq_ref[0, rows, :] += jnp.where(lane < FOX_D, _dot(ds_b, k_v), jnp.sum(ds, axis=1, keepdims=True))
            dk_ref[0] += _dot_tn(ds_b, q_v)

    def qs(w):
        return pl.BlockSpec((1, blk, w), lambda h, j, i: (h, jnp.maximum(i, j), 0))

    def ks(w):
        return pl.BlockSpec((1, blk, w), lambda h, j, i: (h, j, 0))

    return pl.pallas_call(
        kern, name=name, grid=(n_h, nb, nb),
        in_specs=[qs(LANES), ks(LANES), ks(dh), qs(dh), qs(dh), qs(LANES),
                  pl.BlockSpec((blk, LANES), lambda h, j, i: (jnp.maximum(i, j), 0)),
                  pl.BlockSpec((1, 1, blk), lambda h, j, i: (h, 0, j))],
        out_specs=[pl.BlockSpec((1, n_tok, LANES), lambda h, j, i: (h, 0, 0)), ks(LANES), ks(dh),
                   pl.BlockSpec((1, 1, blk), lambda h, j, i: (h, 0, j))],
        out_shape=[jax.ShapeDtypeStruct((n_h, n_tok, LANES), F32)] * 2 + [jax.ShapeDtypeStruct((n_h, n_tok, dh), F32),
                                                                          jax.ShapeDtypeStruct((n_h, 1, n_tok), F32)],
        compiler_params=_params(3),
    )(q, k, v, o, do, lse, c, ct)


def _merge_fwd(name, x, ya, yb, yc, gl, wb, wo):
    n_tok, tm = x.shape[0], min(ROW_BLOCK, x.shape[0])

    def body(i, n, x_ref, ya_ref, yb_ref, yc_ref, gl_ref, wb_ref, wo_ref, xo_ref):
        merged = jnp.zeros((tm, D_MODEL), F32)
        for k, y_ref in enumerate((ya_ref, yb_ref, yc_ref)):
            gate = jax.nn.sigmoid(gl_ref[:, k * D_MODEL:(k + 1) * D_MODEL].astype(F32))
            merged = merged + gate * _dot(y_ref[...], wb_ref[k])
        xo_ref[...] = x_ref[...] + _dot(merged.astype(BF16), wo_ref[...])

    return _rowcall(name, body, n_tok, tm, [x, ya, yb, yc, gl], [], [wb, wo], [(D_MODEL, F32)], [])[0]


def _merge_bwd(name, dxo, ya, yb, yc, gl, wb, wb_t, wo_t):
    n_tok, tm = dxo.shape[0], min(ROW_BLOCK, dxo.shape[0])

    def body(i, n, dxo_ref, ya_ref, yb_ref, yc_ref, gl_ref, wb_ref, wbt_ref, wot_ref,
             mg_ref, dob_ref, dgl_ref, dbr_ref, dya_ref, dyb_ref, dyc_ref):
        dob = dxo_ref[...].astype(BF16)
        dob_ref[...] = dob
        dm = _dot(dob, wot_ref[...])
        merged = jnp.zeros((tm, D_MODEL), F32)
        for k, (y_ref, dy_ref) in enumerate(((ya_ref, dya_ref), (yb_ref, dyb_ref), (yc_ref, dyc_ref))):
            cols = slice(k * D_MODEL, (k + 1) * D_MODEL)
            gate = jax.nn.sigmoid(gl_ref[:, cols].astype(F32))
            br = _dot(y_ref[...], wb_ref[k])
            merged = merged + gate * br
            dgl_ref[:, cols] = (dm * br * gate * (1.0 - gate)).astype(BF16)
            dbr = (dm * gate).astype(BF16)
            dbr_ref[:, cols] = dbr
            dy_ref[...] = _dot(dbr, wbt_ref[k]).astype(BF16)
        mg_ref[...] = merged.astype(BF16)

    return _rowcall(name, body, n_tok, tm, [dxo, ya, yb, yc, gl], [], [wb, wb_t, wo_t],
                    [(D_MODEL, BF16), (D_MODEL, BF16), (3 * D_MODEL, BF16), (3 * D_MODEL, BF16),
                     (CONV_W, BF16), (SG_W, BF16), (FOX_W, BF16)], [])


def _mem_fwd(name, mem, gain, wk, wv):
    n_mem = mem.shape[0]

    def kern(m_ref, g_ref, wk_ref, wv_ref, mn_ref, k_ref, v_ref):
        mh, _ = _rms_stats(m_ref[...])
        mn = (mh * g_ref[...]).astype(BF16)
        mn_ref[...] = mn
        k_ref[...] = _dot(mn, wk_ref[...]).astype(BF16)
        v_ref[...] = _dot(mn, wv_ref[...]).astype(BF16)

    shp = jax.ShapeDtypeStruct((n_mem, D_MODEL), BF16)
    return pl.pallas_call(
        kern, name=name, grid=(1,),
        in_specs=[_full_spec(mem.shape), _full_spec(gain.shape), _full_spec(wk.shape), _full_spec(wv.shape)],
        out_specs=[_full_spec(shp.shape)] * 3, out_shape=[shp] * 3, compiler_params=_params(1),
    )(mem, gain, wk, wv)


def _mem_bwd(name, mem, gain, dkx, dvx, wk_t, wv_t):
    n_mem = mem.shape[0]

    def kern(m_ref, g_ref, dk_ref, dv_ref, wkt_ref, wvt_ref, dkb_ref, dvb_ref, dg_ref):
        mh, _ = _rms_stats(m_ref[...])
        dkb, dvb = dk_ref[...].astype(BF16), dv_ref[...].astype(BF16)
        dkb_ref[...] = dkb
        dvb_ref[...] = dvb
        dm = _dot(dkb, wkt_ref[...]) + _dot(dvb, wvt_ref[...])
        dg_ref[...] = jnp.sum(dm * mh, axis=0, keepdims=True)

    shp = jax.ShapeDtypeStruct((n_mem, D_MODEL), BF16)
    args = (mem, gain, dkx, dvx, wk_t, wv_t)
    return pl.pallas_call(
        kern, name=name, grid=(1,), in_specs=[_full_spec(a.shape) for a in args],
        out_specs=[_full_spec(shp.shape)] * 2 + [_full_spec((1, D_MODEL))],
        out_shape=[shp, shp, jax.ShapeDtypeStruct((1, D_MODEL), F32)], compiler_params=_params(1),
    )(*args)


def _xa_probs(q_b, kx_ref, hd):
    cols = slice(hd * XA_D, (hd + 1) * XA_D)
    s = _dot_nt(q_b[:, cols], kx_ref[:, cols]) * (XA_D ** -0.5)
    e = jnp.exp(s - jnp.max(s, axis=1, keepdims=True))
    return e / jnp.sum(e, axis=1, keepdims=True)


def _xa_fwd(name, x, gain, kx, vx, wq, wo):
    n_tok, tm = x.shape[0], min(ROW_BLOCK, x.shape[0])

    def body(i, n, x_ref, g_ref, kx_ref, vx_ref, wq_ref, wo_ref, xo_ref):
        x_v = x_ref[...]
        xh, _ = _rms_stats(x_v)
        q_b = _dot((xh * g_ref[...]).astype(BF16), wq_ref[...]).astype(BF16)
        o = jnp.concatenate(
            [_dot(_xa_probs(q_b, kx_ref, hd).astype(BF16), vx_ref[:, hd * XA_D:(hd + 1) * XA_D]) for hd in range(XA_H)],
            axis=1)
        xo_ref[...] = x_v + _dot(o.astype(BF16), wo_ref[...])

    return _rowcall(name, body, n_tok, tm, [x], [gain, kx, vx], [wq, wo], [(D_MODEL, F32)], [])[0]


def _xa_bwd(name, x, dxo, gain, kx, vx, wq, wq_t, wo_t):
    n_tok, tm = x.shape[0], min(ROW_BLOCK, x.shape[0])
    n_mem = kx.shape[0]

    def body(i, n, x_ref, dxo_ref, g_ref, kx_ref, vx_ref, wq_ref, wqt_ref, wot_ref,
             dx_ref, h_ref, o_ref, dq_ref, dy_ref, dkx_ref, dvx_ref, dg_ref):
        g = g_ref[...]
        xh, r = _rms_stats(x_ref[...])
        h = (xh * g).astype(BF16)
        h_ref[...] = h
        q_b = _dot(h, wq_ref[...]).astype(BF16)
        dxo_v = dxo_ref[...]
        dy = dxo_v.astype(BF16)
        dy_ref[...] = dy
        do = _dot(dy, wot_ref[...])
        for hd in range(XA_H):
            cols = slice(hd * XA_D, (hd + 1) * XA_D)
            p = _xa_probs(q_b, kx_ref, hd)
            p_b = p.astype(BF16)
            o_ref[:, cols] = _dot(p_b, vx_ref[:, cols]).astype(BF16)
            do_h = do[:, cols].astype(BF16)
            dvx_ref[:, cols] += _dot_tn(p_b, do_h)
            dp = _dot_nt(do_h, vx_ref[:, cols])
            ds = p * (dp - jnp.sum(dp * p, axis=1, keepdims=True))
            ds_b = (ds * (XA_D ** -0.5)).astype(BF16)
            dq_ref[:, cols] = _dot(ds_b, kx_ref[:, cols]).astype(BF16)
            dkx_ref[:, cols] += _dot_tn(ds_b, q_b[:, cols])
        dh = _dot(dq_ref[...], wqt_ref[...])
        dx, dg = _rms_bwd(xh, r, g, dh)
        dx_ref[...] = dxo_v + dx
        dg_ref[...] += dg

    return _rowcall(name, body, n_tok, tm, [x, dxo], [gain, kx, vx], [wq, wq_t, wo_t],
                    [(D_MODEL, F32), (D_MODEL, BF16), (D_MODEL, BF16), (D_MODEL, BF16), (D_MODEL, BF16)],
                    [((n_mem, D_MODEL), F32), ((n_mem, D_MODEL), F32), ((1, D_MODEL), F32)])


def _loss_head(name, x, target, gain):
    n_tok, tm = x.shape[0], min(ROW_BLOCK, x.shape[0])

    def body(i, n, x_ref, t_ref, g_ref, dx_ref, loss_ref, dg_ref):
        g = g_ref[...]
        xh, r = _rms_stats(x_ref[...])
        err = xh * g - t_ref[...]
        loss_ref[...] += 0.5 * jnp.sum(jnp.sum(err * err, axis=1, keepdims=True) / D_MODEL, axis=0, keepdims=True)
        dx, dg = _rms_bwd(xh, r, g, err / D_MODEL)
        dx_ref[...] = dx
        dg_ref[...] += dg

    return _rowcall(name, body, n_tok, tm, [x, target], [gain], [], [(D_MODEL, F32)],
                    [((8, LANES), F32), ((1, D_MODEL), F32)])


def _adamw(name, w, g, m, v):
    rows, cols = w.shape
    tr = 128 if rows % 128 == 0 else rows

    def kern(w_ref, g_ref, m_ref, v_ref, d_ref, mo_ref, vo_ref):
        g_v = g_ref[...]
        m_new = ADAM_B1 * m_ref[...] + (1.0 - ADAM_B1) * g_v
        v_new = ADAM_B2 * v_ref[...] + (1.0 - ADAM_B2) * (g_v * g_v)
        m_hat = m_new / (1.0 - ADAM_B1 ** ADAM_STEP)
        v_hat = v_new / (1.0 - ADAM_B2 ** ADAM_STEP)
        d_ref[...] = -ADAM_LR * (m_hat / (jnp.sqrt(v_hat) + ADAM_EPS) + ADAM_WD * w_ref[...])
        mo_ref[...] = m_new
        vo_ref[...] = v_new

    spec = pl.BlockSpec((tr, cols), lambda i: (i, 0))
    shp = jax.ShapeDtypeStruct((rows, cols), F32)
    return pl.pallas_call(kern, name=name, grid=(rows // tr,), in_specs=[spec] * 4, out_specs=[spec] * 3,
                          out_shape=[shp] * 3, compiler_params=_params(1))(w, g, m, v)


def _add_pair(name, a, b):
    n, rows, cols = a.shape
    spec = pl.BlockSpec((1, PACK_ROW_BLOCK, cols), lambda j, r: (j, r, 0))

    def kern(a_ref, b_ref, o_ref):
        o_ref[...] = (a_ref[...] + b_ref[...]).astype(BF16)

    return pl.pallas_call(kern, name=name, grid=(n, rows // PACK_ROW_BLOCK), in_specs=[spec, spec], out_specs=spec,
                          out_shape=jax.ShapeDtypeStruct(a.shape, BF16), compiler_params=_params(2))(a, b)


def _sum_slots(name, a, tr):
    n, rows, cols = a.shape

    def kern(a_ref, o_ref):
        acc = a_ref[0].astype(F32)
        for k in range(1, n):
            acc = acc + a_ref[k].astype(F32)
        o_ref[...] = acc

    return pl.pallas_call(kern, name=name, grid=(rows // tr,),
                          in_specs=[pl.BlockSpec((n, tr, cols), lambda r: (0, r, 0))],
                          out_specs=pl.BlockSpec((tr, cols), lambda r: (r, 0)),
                          out_shape=jax.ShapeDtypeStruct((rows, cols), F32), compiler_params=_params(1))(a)


_ANY = pl.BlockSpec(memory_space=pl.ANY)


def _place():
    x, y, c = lax.axis_index("x"), lax.axis_index("y"), lax.axis_index("c")
    chips = [(1 - x, y), (x, 1 - y), (1 - x, 1 - y)]
    return x, y, c, chips


def _gather_chips(name, w):
    def kern(w_ref, o_ref, send_sems, recv_sems, local_sem):
        x, y, c, chips = _place()
        me, sib = 2 * x + y, (x, y, 1 - c)

        def copy(k, src, dst, to):
            return pltpu.make_async_remote_copy(src_ref=src, dst_ref=dst, send_sem=send_sems.at[k],
                                                recv_sem=recv_sems.at[k], device_id=to, device_id_type=MESH_ID)

        mine = pltpu.make_async_copy(w_ref, o_ref.at[me], local_sem)
        mine.start()
        first = [copy(k, w_ref.at[c], o_ref.at[me, c], (cx, cy, c)) for k, (cx, cy) in enumerate(chips)]
        for cp in first:
            cp.start()
        passed = [copy(3 + k, o_ref.at[2 * cx + cy, c], o_ref.at[2 * cx + cy, c], sib) for k, (cx, cy) in enumerate(chips)]
        for k, (cx, cy) in enumerate(chips):
            copy(k, w_ref.at[c], o_ref.at[2 * cx + cy, c], (x, y, c)).wait_recv()
            passed[k].start()
        for k, (cx, cy) in enumerate(chips):
            copy(3 + k, w_ref.at[c], o_ref.at[2 * cx + cy, 1 - c], (x, y, c)).wait_recv()
        for cp in first + passed:
            cp.wait_send()
        mine.wait()

    return pl.pallas_call(
        kern, name=name, in_specs=[_ANY], out_specs=_ANY,
        out_shape=jax.ShapeDtypeStruct((N_CHIPS,) + w.shape, w.dtype),
        scratch_shapes=[pltpu.SemaphoreType.DMA((6,)), pltpu.SemaphoreType.DMA((6,)), pltpu.SemaphoreType.DMA(())],
        compiler_params=pltpu.CompilerParams(has_side_effects=True),
    )(w)


def _gather_all(name, v):
    def kern(v_ref, o_ref, send_sems, recv_sems, local_sem):
        x, y, c, _ = _place()
        me = 4 * x + 2 * y + c
        mine = pltpu.make_async_copy(v_ref, o_ref.at[me], local_sem)
        mine.start()
        peers = []
        for k in range(1, 8):
            px = 1 - x if k & 4 else x
            py = 1 - y if k & 2 else y
            pc = 1 - c if k & 1 else c
            peers.append((px, py, pc))
        copies = [pltpu.make_async_remote_copy(src_ref=v_ref, dst_ref=o_ref.at[me], send_sem=send_sems.at[k],
                                               recv_sem=recv_sems.at[k], device_id=p, device_id_type=MESH_ID)
                  for k, p in enumerate(peers)]
        for cp in copies:
            cp.start()
        for k, (px, py, pc) in enumerate(peers):
            pltpu.make_async_remote_copy(src_ref=v_ref, dst_ref=o_ref.at[4 * px + 2 * py + pc], send_sem=send_sems.at[k],
                                         recv_sem=recv_sems.at[k], device_id=(x, y, c), device_id_type=MESH_ID).wait_recv()
        for cp in copies:
            cp.wait_send()
        mine.wait()

    return pl.pallas_call(
        kern, name=name, in_specs=[_ANY], out_specs=_ANY, out_shape=jax.ShapeDtypeStruct((8,) + v.shape, v.dtype),
        scratch_shapes=[pltpu.SemaphoreType.DMA((7,)), pltpu.SemaphoreType.DMA((7,)), pltpu.SemaphoreType.DMA(())],
        compiler_params=pltpu.CompilerParams(has_side_effects=True),
    )(v)


def _swap_halves(name, gp):
    def kern(g_ref, mine_ref, got_ref, send_sem, recv_sem, local_sem):
        x, y, c, _ = _place()
        mine = pltpu.make_async_copy(g_ref.at[c], mine_ref, local_sem)
        mine.start()
        swap = pltpu.make_async_remote_copy(src_ref=g_ref.at[1 - c], dst_ref=got_ref, send_sem=send_sem,
                                            recv_sem=recv_sem, device_id=(x, y, 1 - c), device_id_type=MESH_ID)
        swap.start()
        swap.wait()
        mine.wait()

    shp = jax.ShapeDtypeStruct(gp.shape[1:], gp.dtype)
    return pl.pallas_call(
        kern, name=name, in_specs=[_ANY], out_specs=[_ANY, _ANY], out_shape=[shp, shp],
        scratch_shapes=[pltpu.SemaphoreType.DMA(())] * 3, compiler_params=pltpu.CompilerParams(has_side_effects=True),
    )(gp)


def _scatter_chips(name, s):
    def kern(s_ref, o_ref, send_sems, recv_sems, local_sem):
        x, y, c, chips = _place()
        me = 2 * x + y
        mine = pltpu.make_async_copy(s_ref.at[me], o_ref.at[me], local_sem)
        mine.start()
        copies = [pltpu.make_async_remote_copy(src_ref=s_ref.at[2 * cx + cy], dst_ref=o_ref.at[me], send_sem=send_sems.at[k],
                                               recv_sem=recv_sems.at[k], device_id=(cx, cy, c), device_id_type=MESH_ID)
                  for k, (cx, cy) in enumerate(chips)]
        for cp in copies:
            cp.start()
        for k, (cx, cy) in enumerate(chips):
            pltpu.make_async_remote_copy(src_ref=s_ref.at[me], dst_ref=o_ref.at[2 * cx + cy], send_sem=send_sems.at[k],
                                         recv_sem=recv_sems.at[k], device_id=(x, y, c), device_id_type=MESH_ID).wait_recv()
        for cp in copies:
            cp.wait_send()
        mine.wait()

    return pl.pallas_call(
        kern, name=name, in_specs=[_ANY], out_specs=_ANY, out_shape=jax.ShapeDtypeStruct(s.shape, s.dtype),
        scratch_shapes=[pltpu.SemaphoreType.DMA((3,)), pltpu.SemaphoreType.DMA((3,)), pltpu.SemaphoreType.DMA(())],
        compiler_params=pltpu.CompilerParams(has_side_effects=True),
    )(s)


def _join_halves(name, r):
    def kern(r_ref, o_ref, send_sem, recv_sem, local_sem):
        x, y, c, _ = _place()
        mine = pltpu.make_async_copy(r_ref, o_ref.at[c], local_sem)
        mine.start()
        send = pltpu.make_async_remote_copy(src_ref=r_ref, dst_ref=o_ref.at[c], send_sem=send_sem, recv_sem=recv_sem,
                                            device_id=(x, y, 1 - c), device_id_type=MESH_ID)
        send.start()
        pltpu.make_async_remote_copy(src_ref=r_ref, dst_ref=o_ref.at[1 - c], send_sem=send_sem, recv_sem=recv_sem,
                                     device_id=(x, y, c), device_id_type=MESH_ID).wait_recv()
        send.wait_send()
        mine.wait()

    return pl.pallas_call(
        kern, name=name, in_specs=[_ANY], out_specs=_ANY, out_shape=jax.ShapeDtypeStruct((2,) + r.shape, r.dtype),
        scratch_shapes=[pltpu.SemaphoreType.DMA(())] * 3, compiler_params=pltpu.CompilerParams(has_side_effects=True),
    )(r)


def _pack_rows(n_elems):
    rows = -(-n_elems // (2 * PACK_COLS * PACK_ROW_BLOCK)) * PACK_ROW_BLOCK
    return rows


def _pack(pieces, dtype):
    flat = jnp.concatenate([p.astype(dtype).reshape(-1) for p in pieces])
    rows = _pack_rows(flat.shape[0])
    flat = jnp.pad(flat, (0, 2 * rows * PACK_COLS - flat.shape[0]))
    return flat.reshape(2, rows, PACK_COLS)


def _unpack(packed, shapes):
    lead = packed.shape[:-3]
    flat = packed.reshape(lead + (-1,))
    out, off = [], 0
    for shp in shapes:
        size = 1
        for s in shp:
            size *= s
        out.append(flat[..., off:off + size].reshape(lead + tuple(shp)))
        off += size
    return out


def _shard(a, axis, j):
    size = a.shape[axis] // N_CHIPS
    return lax.slice_in_dim(a, j * size, (j + 1) * size, axis=axis)


def _t(a):
    return jnp.swapaxes(a, -1, -2)


def _heads(a):
    return a.reshape(a.shape[0], FOX_H, FOX_D).transpose(1, 0, 2)


def _heads_padded(a):
    return jnp.pad(_heads(a), ((0, 0), (0, 0), (0, LANES - FOX_D)))


def _unheads(a):
    return a[:, :, :FOX_D].transpose(1, 0, 2).reshape(a.shape[1], FOX_W)


def _lanes_from_heads(a):
    return jnp.pad(a[:, 0, :].T, ((0, 0), (0, LANES - FOX_H)))


def _layer_weights(wts, small, l):
    w_in = wts["w_in"][l]
    w = {
        "wf": jnp.pad(w_in[:, W1_COLS:W1_COLS + FOX_H], ((0, 0), (0, LANES - FOX_H))),
        "w1": w_in[:, :W1_COLS], "wgl": w_in[:, W1_COLS + FOX_H:],
    }
    for nm in ("ffn1_w_gate", "ffn1_w_up", "ffn1_w_down", "w_branch", "w_out", "xa_wq", "xa_wk", "xa_wv", "xa_wo",
               "ffn2_w_gate", "ffn2_w_up", "ffn2_w_down"):
        w[nm] = wts[nm][l]
    for nm in list(w):
        w[nm + "_t"] = _t(w[nm])
    for nm in ("ffn1_norm", "mix_norm", "sg_norm", "xa_norm", "mem_norm", "ffn2_norm"):
        w[nm] = small[nm][l][None, :]
    w["conv_w"] = jnp.pad(small["conv_w"][l], ((0, 5), (0, 0)))
    w["sg_w"] = small["sg_w"][l]
    w["sg_w_t"] = _t(small["sg_w"][l])
    w["sg_bias"] = jnp.repeat(small["sg_b"][l].T, CHUNK, axis=1)
    w["fox_b"] = jnp.pad(small["fox_b_f"][l][None, :], ((0, 0), (0, LANES - FOX_H)))
    return w


def _layer_fwd(l, x, mem, w):
    s = {"x0": x}
    x1, s["a1"], s["b1"] = _ffn_fwd(f"ffn1_fwd_{l}", x, w["ffn1_norm"], w["ffn1_w_gate"], w["ffn1_w_up"], w["ffn1_w_down"])
    s["x1"] = x1
    cin, sgin, qkv, fl, gl = _proj_fwd(f"proj_fwd_{l}", x1, w["mix_norm"], w["w1"], w["wf"], w["wgl"])
    ya = _conv_fwd(f"conv_fwd_{l}", cin, w["conv_w"])
    yb = _sg_fwd(f"sg_fwd_{l}", sgin, w["sg_norm"], w["sg_w"], w["sg_bias"])
    q, k = (_heads_padded(qkv[:, j * FOX_W:(j + 1) * FOX_W]) for j in range(2))
    v = _heads(qkv[:, 2 * FOX_W:])
    c = _forget_cumsum(f"forget_fwd_{l}", fl, w["fox_b"])
    ct = c[:, :FOX_H].T[:, None, :]
    o, lse = _flash_fwd(f"flash_fwd_{l}", q, k, v, c, ct)
    yc = _unheads(o)
    x2 = _merge_fwd(f"merge_fwd_{l}", x1, ya, yb, yc, gl, w["w_branch"], w["w_out"])
    s.update(cin=cin, sgin=sgin, q=q, k=k, v=v, fl=fl, gl=gl, ya=ya, yb=yb, yc=yc, c=c, ct=ct, o=o, lse=lse, x2=x2)
    s["mn"], s["kx"], s["vx"] = _mem_fwd(f"mem_fwd_{l}", mem, w["mem_norm"], w["xa_wk"], w["xa_wv"])
    x3 = _xa_fwd(f"xa_fwd_{l}", x2, w["xa_norm"], s["kx"], s["vx"], w["xa_wq"], w["xa_wo"])
    s["x3"] = x3
    x4, s["a2"], s["b2"] = _ffn_fwd(f"ffn2_fwd_{l}", x3, w["ffn2_norm"], w["ffn2_w_gate"], w["ffn2_w_up"], w["ffn2_w_down"])
    return x4, s


def _ffn_grads(tag, x, dxo, a, b, w, pre):
    dx, da, db, sv, h, dy, dg = _ffn_bwd(f"{pre}_bwd_{tag}", x, dxo, a, b, w[pre + "_norm"], w[pre + "_w_down_t"],
                                         w[pre + "_w_gate_t"], w[pre + "_w_up_t"])
    g = {
        pre + "_norm": dg[0],
        pre + "_w_gate": _mm_tn(f"{pre}_dwg_{tag}", h, da, D_MODEL, D_FF, tk=D_MODEL, tn=FF_SPLIT),
        pre + "_w_up": _mm_tn(f"{pre}_dwu_{tag}", h, db, D_MODEL, D_FF, tk=D_MODEL, tn=FF_SPLIT),
        pre + "_w_down": _mm_tn(f"{pre}_dwd_{tag}", sv, dy, D_FF, D_MODEL, tk=FF_SPLIT, tn=D_MODEL),
    }
    return dx, g


def _layer_bwd(l, dx, mem, w, s):
    g = {}
    dx, gf = _ffn_grads(l, s["x3"], dx, s["a2"], s["b2"], w, "ffn2")
    g.update(gf)

    dx, h, o, dq, dy, dkx, dvx, dg = _xa_bwd(f"xa_bwd_{l}", s["x2"], dx, w["xa_norm"], s["kx"], s["vx"], w["xa_wq"],
                                              w["xa_wq_t"], w["xa_wo_t"])
    g["xa_norm"] = dg[0]
    g["xa_wq"] = _mm_tn(f"xa_dwq_{l}", h, dq, D_MODEL, D_MODEL, tk=D_MODEL, tn=D_MODEL)
    g["xa_wo"] = _mm_tn(f"xa_dwo_{l}", o, dy, D_MODEL, D_MODEL, tk=D_MODEL, tn=D_MODEL)
    dkb, dvb, dgm = _mem_bwd(f"mem_bwd_{l}", mem, w["mem_norm"], dkx, dvx, w["xa_wk_t"], w["xa_wv_t"])
    g["mem_norm"] = dgm[0]
    g["xa_wk"] = _mm_tn(f"xa_dwk_{l}", s["mn"], dkb, D_MODEL, D_MODEL, tk=D_MODEL, tn=D_MODEL)
    g["xa_wv"] = _mm_tn(f"xa_dwv_{l}", s["mn"], dvb, D_MODEL, D_MODEL, tk=D_MODEL, tn=D_MODEL)

    mg, dob, dgl, dbr, dya, dyb, dyc = _merge_bwd(f"merge_bwd_{l}", dx, s["ya"], s["yb"], s["yc"], s["gl"], w["w_branch"],
                                                  w["w_branch_t"], w["w_out_t"])
    g["w_out"] = _mm_tn(f"dwout_{l}", mg, dob, D_MODEL, D_MODEL, tk=D_MODEL, tn=D_MODEL)
    g["w_branch"] = jnp.stack([
        _mm_tn(f"dwbranch{k}_{l}", y, dbr, CONV_W, D_MODEL, tk=CONV_W, tn=D_MODEL, y_off=k * D_MODEL)
        for k, y in enumerate((s["ya"], s["yb"], s["yc"]))])

    dcin, dcw = _conv_bwd(f"conv_bwd_{l}", s["cin"], dya, w["conv_w"])
    g["conv_w"] = dcw[:3]
    dsg, dgn, dsw, dsb = _sg_bwd(f"sg_bwd_{l}", s["sgin"], dyb, w["sg_norm"], w["sg_w"], w["sg_w_t"], w["sg_bias"])
    g["sg_norm"], g["sg_w"], g["sg_b"] = dgn[0], dsw, dsb[:, :, 0]
    dq3, dk3, dv3, dct = _flash_bwd(f"flash_bwd_{l}", s["q"], s["k"], s["v"], s["o"], _heads(dyc), s["lse"], s["c"], s["ct"])
    dqkv = jnp.concatenate([_unheads(a) for a in (dq3, dk3, dv3)], axis=1).astype(BF16)
    dc = _lanes_from_heads(dct) + jnp.pad(dq3[:, :, FOX_D].T, ((0, 0), (0, LANES - FOX_H)))
    dfl, dbf = _forget_bwd(f"forget_bwd_{l}", dc, s["fl"], w["fox_b"])
    g["fox_b_f"] = dbf[0, :FOX_H]

    dx, h, dfb, dg = _proj_bwd(f"proj_bwd_{l}", s["x1"], dx, dcin, dsg, dqkv, dfl, dgl, w["mix_norm"], w["w1_t"],
                               w["wf_t"], w["wgl_t"])
    g["mix_norm"] = dg[0]
    c0, c1 = 3 * CONV_W, 3 * CONV_W + 2 * SG_W
    g["w_in"] = jnp.concatenate([
        _mm_tn(f"dwin_conv_{l}", h, dcin, D_MODEL, c0, tk=D_MODEL, tn=c0),
        _mm_tn(f"dwin_sg_{l}", h, dsg, D_MODEL, c1 - c0, tk=D_MODEL, tn=c1 - c0),
        _mm_tn(f"dwin_qkv_{l}", h, dqkv, D_MODEL, 3 * FOX_W, tk=D_MODEL, tn=3 * FOX_W),
        _mm_tn(f"dwin_f_{l}", h, dfb, D_MODEL, LANES, tk=D_MODEL, tn=LANES)[:, :FOX_H],
        _mm_tn(f"dwin_gl_{l}", h, dgl, D_MODEL, 3 * D_MODEL, tk=D_MODEL, tn=3 * D_MODEL // 2),
    ], axis=1)

    dx, gf = _ffn_grads(l, s["x0"], dx, s["a1"], s["b1"], w, "ffn1")
    g.update(gf)
    return dx, g


def _local_step(x, mem, target, wts, small):
    saved, lw = [], []
    for l in range(DEPTH):
        lw.append(_layer_weights(wts, small, l))
        x, s = _layer_fwd(l, x, mem, lw[l])
        saved.append(s)
    fin = small["final_norm"][None, :]
    dx, loss, dgf = _loss_head("loss_head", x, target, fin)
    layer_grads = [None] * DEPTH
    for l in reversed(range(DEPTH)):
        dx, layer_grads[l] = _layer_bwd(l, dx, mem, lw[l], saved[l])
    grads = {nm: jnp.stack([layer_grads[l][nm] for l in range(DEPTH)]) for nm in WEIGHTS if nm != "final_norm"}
    grads["final_norm"] = dgf[0]
    return loss[0, 0], dx, grads


def _small_slab(vals):
    rows = []
    for v in vals:
        flat = v.astype(F32).reshape(-1)
        n = -(-flat.shape[0] // LANES) * LANES
        rows.append(jnp.pad(flat, (0, n - flat.shape[0])).reshape(-1, LANES))
    slab = jnp.concatenate(rows)
    pad = -slab.shape[0] % 8
    return jnp.pad(slab, ((0, pad), (0, 0)))


def _small_unslab(slab, shapes):
    out, off = [], 0
    for shp in shapes:
        size = 1
        for s in shp:
            size *= s
        n_rows = -(-size // LANES)
        out.append(slab[off:off + n_rows].reshape(-1)[:size].reshape(shp))
        off += n_rows
    return out


def kernel(x, mem, ffn1_norm, ffn1_w_gate, ffn1_w_up, ffn1_w_down, mix_norm, w_in, conv_w, sg_norm, sg_w, sg_b, fox_b_f, w_branch, w_out, xa_norm, mem_norm, xa_wq, xa_wk, xa_wv, xa_wo, ffn2_norm, ffn2_w_gate, ffn2_w_up, ffn2_w_down, final_norm, loss_target, m_ffn1_norm, m_ffn1_w_gate, m_ffn1_w_up, m_ffn1_w_down, m_mix_norm, m_w_in, m_conv_w, m_sg_norm, m_sg_w, m_sg_b, m_fox_b_f, m_w_branch, m_w_out, m_xa_norm, m_mem_norm, m_xa_wq, m_xa_wk, m_xa_wv, m_xa_wo, m_ffn2_norm, m_ffn2_w_gate, m_ffn2_w_up, m_ffn2_w_down, m_final_norm, v_ffn1_norm, v_ffn1_w_gate, v_ffn1_w_up, v_ffn1_w_down, v_mix_norm, v_w_in, v_conv_w, v_sg_norm, v_sg_w, v_sg_b, v_fox_b_f, v_w_branch, v_w_out, v_xa_norm, v_mem_norm, v_xa_wq, v_xa_wk, v_xa_wv, v_xa_wo, v_ffn2_norm, v_ffn2_w_gate, v_ffn2_w_up, v_ffn2_w_down, v_final_norm):
    args = dict(locals())
    wv = {nm: args[nm] for nm in WEIGHTS}
    mv = {nm: args["m_" + nm] for nm in WEIGHTS}
    vv = {nm: args["v_" + nm] for nm in WEIGHTS}
    chip = 2 * lax.axis_index("x") + lax.axis_index("y")
    core = lax.axis_index("c")

    shard_shapes = [wv[nm].shape for nm, _ in BIG]
    gathered = _gather_chips("gather_weights", _pack([wv[nm] for nm, _ in BIG], BF16))
    pieces = _unpack(gathered, shard_shapes)
    wts = {nm: jnp.concatenate([p[j] for j in range(N_CHIPS)], axis=ax) for (nm, ax), p in zip(BIG, pieces)}
    taps = _gather_all("gather_taps", _small_slab([conv_w]))
    taps = [_small_unslab(taps[2 * j], [conv_w.shape])[0] for j in range(N_CHIPS)]
    small = {nm: wv[nm] for nm in SMALL}
    small["conv_w"] = jnp.concatenate(taps, axis=2)

    loss, dx, grads = _local_step(x[0], mem[0], loss_target[0], wts, small)
    loss = lax.psum(loss, ("x", "y", "c"))

    small_shapes = [grads[nm].shape for nm in SMALL]
    slots = _gather_all("gather_small_grads", _small_slab([grads[nm] for nm in SMALL]))
    small_sum = _sum_slots("sum_small_grads", slots, slots.shape[1])
    small_g = dict(zip(SMALL, _small_unslab(small_sum, small_shapes)))
    taps_g = small_g["conv_w"]
    small_g["conv_w"] = lax.dynamic_slice_in_dim(taps_g, chip * conv_w.shape[2], conv_w.shape[2], axis=2)

    gp = jnp.stack([_pack([_shard(grads[nm], ax, j) for nm, ax in BIG], F32) for j in range(N_CHIPS)], axis=1)
    mine, got = _swap_halves("reduce_swap_cores", gp)
    chip_sum = _add_pair("reduce_add_cores", mine, got)
    parts = _scatter_chips("reduce_scatter_chips", chip_sum)
    half = _sum_slots("reduce_add_chips", parts, PACK_ROW_BLOCK)
    full = _join_halves("reduce_join_cores", half)
    big_g = dict(zip([nm for nm, _ in BIG], _unpack(full, shard_shapes)))

    g_out = {**small_g, **big_g}
    delta, new_m, new_v = {}, {}, {}
    for nm, _ in BIG:
        shp = wv[nm].shape
        two_d = (-1, shp[-1])
        d, m2, v2 = _adamw("adamw_" + nm, wv[nm].reshape(two_d), g_out[nm].reshape(two_d), mv[nm].reshape(two_d),
                           vv[nm].reshape(two_d))
        delta[nm], new_m[nm], new_v[nm] = d.reshape(shp), m2.reshape(shp), v2.reshape(shp)
    slab_shapes = [wv[nm].shape for nm in SMALL]
    d, m2, v2 = _adamw("adamw_small", _small_slab([wv[nm] for nm in SMALL]), _small_slab([g_out[nm] for nm in SMALL]),
                       _small_slab([mv[nm] for nm in SMALL]), _small_slab([vv[nm] for nm in SMALL]))
    for out, slab in ((delta, d), (new_m, m2), (new_v, v2)):
        out.update(zip(SMALL, _small_unslab(slab, slab_shapes)))

    return (loss, dx[None], *[g_out[nm] for nm in WEIGHTS], *[delta[nm] for nm in WEIGHTS],
            *[new_m[nm] for nm in WEIGHTS], *[new_v[nm] for nm in WEIGHTS])
```

```python
import functools

import jax
import jax.numpy as jnp
from jax import lax
from jax.experimental import pallas as pl
from jax.experimental.pallas import tpu as pltpu

F32, BF16 = jnp.float32, jnp.bfloat16
MESH_ID = pl.DeviceIdType.MESH

D_MODEL = 1024
DEPTH = 2
D_FF = 2816
CONV_W = 512
SG_W = 512
SG_G = 4
CHUNK = 128
FOX_H = 8
FOX_D = 64
FOX_W = FOX_H * FOX_D
FOX_SCALE = FOX_D ** -0.5
LOG2E = 1.4426950408889634
XA_H = 4
XA_D = D_MODEL // XA_H
N_CHIPS = 4
RMS_EPS = 1e-6
W1_COLS = 3 * CONV_W + 2 * SG_W + 3 * FOX_W
LANES = 128
HALO = 16

ADAM_LR, ADAM_B1, ADAM_B2, ADAM_EPS, ADAM_WD, ADAM_STEP = 0.001, 0.9, 0.999, 1e-08, 0.01, 10

ROW_BLOCK = 256
FLASH_BLOCK = 512
FF_SPLIT = 1408
VMEM_LIMIT = 56 * 1024 * 1024
PACK_COLS = 1024
PACK_ROW_BLOCK = 256

BIG = (
    ("ffn1_w_gate", 2), ("ffn1_w_up", 2), ("ffn1_w_down", 1), ("w_in", 2), ("w_branch", 3), ("w_out", 1),
    ("xa_wq", 1), ("xa_wk", 1), ("xa_wv", 1), ("xa_wo", 1), ("ffn2_w_gate", 2), ("ffn2_w_up", 2), ("ffn2_w_down", 1),
)
SMALL = ("ffn1_norm", "mix_norm", "conv_w", "sg_norm", "sg_w", "sg_b", "fox_b_f", "xa_norm", "mem_norm", "ffn2_norm",
         "final_norm")
WEIGHTS = ("ffn1_norm", "ffn1_w_gate", "ffn1_w_up", "ffn1_w_down", "mix_norm", "w_in", "conv_w", "sg_norm", "sg_w",
           "sg_b", "fox_b_f", "w_branch", "w_out", "xa_norm", "mem_norm", "xa_wq", "xa_wk", "xa_wv", "xa_wo",
           "ffn2_norm", "ffn2_w_gate", "ffn2_w_up", "ffn2_w_down", "final_norm")


def _dot(a, b):
    return jnp.dot(a, b, preferred_element_type=F32)


def _dot_nt(a, b):
    return lax.dot_general(a, b, (((1,), (1,)), ((), ())), preferred_element_type=F32)


def _dot_tn(a, b):
    return lax.dot_general(a, b, (((0,), (0,)), ((), ())), preferred_element_type=F32)


def _rms_stats(x):
    r = lax.rsqrt(jnp.mean(x * x, axis=-1, keepdims=True) + RMS_EPS)
    return x * r, r


def _rms_bwd(xh, r, g, dy):
    dg = jnp.sum(dy * xh, axis=0, keepdims=True)
    dxh = dy * g
    dx = r * (dxh - xh * jnp.mean(dxh * xh, axis=-1, keepdims=True))
    return dx, dg


def _gelu(x):
    k = 0.7978845608028654
    t = jnp.tanh(k * (x + 0.044715 * x * x * x))
    return 0.5 * x * (1.0 + t), t


def _gelu_grad(x, t):
    k = 0.7978845608028654
    return 0.5 * (1.0 + t) + 0.5 * x * (1.0 - t * t) * k * (1.0 + 3.0 * 0.044715 * x * x)


def _split3_dot(tri, x):
    hi = x.astype(BF16)
    r1 = x - hi.astype(F32)
    mid = r1.astype(BF16)
    lo = (r1 - mid.astype(F32)).astype(BF16)
    return _dot(tri, hi) + _dot(tri, mid) + _dot(tri, lo)


def _params(n_grid):
    return pltpu.CompilerParams(dimension_semantics=("arbitrary",) * n_grid, vmem_limit_bytes=VMEM_LIMIT)


def _full_spec(shape):
    nd = len(shape)
    return pl.BlockSpec(tuple(shape), lambda *_: (0,) * nd)


def _rowcall(name, body, n_tokens, tm, rows, consts, residents, row_outs, acc_outs, scratch=()):
    n = n_tokens // tm
    rows = [r if isinstance(r, tuple) else (r, pl.BlockSpec((tm, r.shape[1]), lambda i: (i, 0))) for r in rows]
    n_in, n_w = len(rows) + len(consts), len(residents)
    n_out = len(row_outs) + len(acc_outs)
    in_specs = ([r[1] for r in rows] + [_full_spec(c.shape) for c in consts]
                + [pl.BlockSpec(memory_space=pl.ANY)] * n_w)
    out_shape = ([jax.ShapeDtypeStruct((n_tokens, c), dt) for c, dt in row_outs]
                 + [jax.ShapeDtypeStruct(s, dt) for s, dt in acc_outs])
    out_specs = ([pl.BlockSpec((tm, c), lambda i: (i, 0)) for c, _ in row_outs]
                 + [_full_spec(s) for s, _ in acc_outs])
    scratch_shapes = [pltpu.VMEM(w.shape, w.dtype) for w in residents]
    if n_w:
        scratch_shapes.append(pltpu.SemaphoreType.DMA((n_w,)))
    scratch_shapes += list(scratch)

    def kern(*refs):
        ins, w_hbm = refs[:n_in], refs[n_in:n_in + n_w]
        outs = refs[n_in + n_w:n_in + n_w + n_out]
        rest = refs[n_in + n_w + n_out:]
        w_vmem = rest[:n_w]
        extra = rest[n_w + 1:] if n_w else rest
        i = pl.program_id(0)

        @pl.when(i == 0)
        def _():
            copies = [pltpu.make_async_copy(w_hbm[k], w_vmem[k], rest[n_w].at[k]) for k in range(n_w)]
            for cp in copies:
                cp.start()
            for cp in copies:
                cp.wait()
            for a in outs[len(row_outs):]:
                a[...] = jnp.zeros(a.shape, a.dtype)

        body(i, n, *ins, *w_vmem, *outs, *extra)

    res = pl.pallas_call(
        kern, name=name, grid=(n,), in_specs=in_specs, out_specs=out_specs, out_shape=out_shape,
        scratch_shapes=scratch_shapes, compiler_params=_params(1),
    )(*[r[0] for r in rows], *consts, *residents)
    return res


def _mm_tn(name, x, y, k_dim, n_dim, *, tk, tn, x_off=0, y_off=0, tt=512):
    n_tok = x.shape[0]
    tt = min(tt, n_tok)
    n_t = n_tok // tt
    xb, yb = x_off // tk, y_off // tn

    def kern(x_ref, y_ref, o_ref):
        @pl.when(pl.program_id(2) == 0)
        def _():
            o_ref[...] = jnp.zeros(o_ref.shape, F32)

        o_ref[...] += _dot_tn(x_ref[...], y_ref[...])

    return pl.pallas_call(
        kern, name=name, grid=(k_dim // tk, n_dim // tn, n_t),
        in_specs=[pl.BlockSpec((tt, tk), lambda k, n, t: (t, xb + k)),
                  pl.BlockSpec((tt, tn), lambda k, n, t: (t, yb + n))],
        out_specs=pl.BlockSpec((tk, tn), lambda k, n, t: (k, n)),
        out_shape=jax.ShapeDtypeStruct((k_dim, n_dim), F32),
        compiler_params=_params(3),
    )(x, y)


def _ffn_fwd(name, x, gain, wg, wu, wd):
    n_tok, tm = x.shape[0], min(ROW_BLOCK, x.shape[0])

    def body(i, n, x_ref, g_ref, wg_ref, wu_ref, wd_ref, xo_ref, a_ref, b_ref):
        x_v = x_ref[...]
        xh, _ = _rms_stats(x_v)
        h = (xh * g_ref[...]).astype(BF16)
        y = jnp.zeros((tm, D_MODEL), F32)
        for f0 in range(0, D_FF, FF_SPLIT):
            sl = slice(f0, f0 + FF_SPLIT)
            a = _dot(h, wg_ref[:, sl])
            b = _dot(h, wu_ref[:, sl])
            a_ref[:, sl] = a.astype(BF16)
            b_ref[:, sl] = b.astype(BF16)
            s = (a * jax.nn.sigmoid(a) * b).astype(BF16)
            y = y + _dot(s, wd_ref[sl, :])
        xo_ref[...] = x_v + 0.5 * y

    return _rowcall(name, body, n_tok, tm, [x], [gain], [wg, wu, wd],
                    [(D_MODEL, F32), (D_FF, BF16), (D_FF, BF16)], [])


def _ffn_bwd(name, x, dxo, a, b, gain, wd_t, wg_t, wu_t):
    n_tok, tm = x.shape[0], min(ROW_BLOCK, x.shape[0])

    def body(i, n, x_ref, dxo_ref, a_ref, b_ref, g_ref, wdt_ref, wgt_ref, wut_ref,
             dx_ref, da_ref, db_ref, s_ref, h_ref, dy_ref, dg_ref):
        g = g_ref[...]
        xh, r = _rms_stats(x_ref[...])
        h_ref[...] = (xh * g).astype(BF16)
        dxo_v = dxo_ref[...]
        dy = (0.5 * dxo_v).astype(BF16)
        dy_ref[...] = dy
        dh = jnp.zeros((tm, D_MODEL), F32)
        for f0 in range(0, D_FF, FF_SPLIT):
            sl = slice(f0, f0 + FF_SPLIT)
            a_v = a_ref[:, sl].astype(F32)
            b_v = b_ref[:, sl].astype(F32)
            ds = _dot(dy, wdt_ref[:, sl])
            sig = jax.nn.sigmoid(a_v)
            sa = a_v * sig
            s_ref[:, sl] = (sa * b_v).astype(BF16)
            da = (ds * b_v * (sig * (1.0 + a_v * (1.0 - sig)))).astype(BF16)
            db = (ds * sa).astype(BF16)
            da_ref[:, sl] = da
            db_ref[:, sl] = db
            dh = dh + _dot(da, wgt_ref[sl, :]) + _dot(db, wut_ref[sl, :])
        dx, dg = _rms_bwd(xh, r, g, dh)
        dx_ref[...] = dxo_v + dx
        dg_ref[...] += dg

    return _rowcall(name, body, n_tok, tm, [x, dxo, a, b], [gain], [wd_t, wg_t, wu_t],
                    [(D_MODEL, F32), (D_FF, BF16), (D_FF, BF16), (D_FF, BF16), (D_MODEL, BF16), (D_MODEL, BF16)],
                    [((1, D_MODEL), F32)])


def _proj_fwd(name, x, gain, w1, wf, wgl):
    n_tok, tm = x.shape[0], min(ROW_BLOCK, x.shape[0])
    c0, c1 = 3 * CONV_W, 3 * CONV_W + 2 * SG_W

    def body(i, n, x_ref, g_ref, w1_ref, wf_ref, wgl_ref, cin_ref, sg_ref, qkv_ref, fl_ref, gl_ref):
        xh, _ = _rms_stats(x_ref[...])
        h = (xh * g_ref[...]).astype(BF16)
        cin_ref[...] = _dot(h, w1_ref[:, 0:c0]).astype(BF16)
        sg_ref[...] = _dot(h, w1_ref[:, c0:c1]).astype(BF16)
        qkv_ref[...] = _dot(h, w1_ref[:, c1:W1_COLS]).astype(BF16)
        fl_ref[...] = _dot(h, wf_ref[...])
        gl_ref[...] = _dot(h, wgl_ref[...]).astype(BF16)

    return _rowcall(name, body, n_tok, tm, [x], [gain], [w1, wf, wgl],
                    [(3 * CONV_W, BF16), (2 * SG_W, BF16), (3 * FOX_W, BF16), (LANES, F32), (3 * D_MODEL, BF16)], [])


def _proj_bwd(name, x, dxin, dcin, dsg, dq, dk, dv, dfl, dgl, gain, w1_t, wf_t, wgl_t):
    n_tok, tm = x.shape[0], min(ROW_BLOCK, x.shape[0])
    c0, c1 = 3 * CONV_W, 3 * CONV_W + 2 * SG_W

    def body(i, n, x_ref, dxin_ref, dcin_ref, dsg_ref, dq_ref, dk_ref, dv_ref, dfl_ref, dgl_ref, g_ref,
             w1t_ref, wft_ref, wglt_ref, dx_ref, h_ref, dqkv_ref, dfb_ref, dg_ref):
        g = g_ref[...]
        xh, r = _rms_stats(x_ref[...])
        h_ref[...] = (xh * g).astype(BF16)
        dfb = dfl_ref[...].astype(BF16)
        dfb_ref[...] = dfb
        dh = _dot(dcin_ref[...], w1t_ref[0:c0, :])
        dh = dh + _dot(dsg_ref[...], w1t_ref[c0:c1, :])
        for k, d_ref in enumerate((dq_ref, dk_ref, dv_ref)):
            d_b = d_ref[...].astype(BF16)
            dqkv_ref[:, k * FOX_W:(k + 1) * FOX_W] = d_b
            dh = dh + _dot(d_b, w1t_ref[c1 + k * FOX_W:c1 + (k + 1) * FOX_W, :])
        dh = dh + _dot(dfb, wft_ref[...])
        dh = dh + _dot(dgl_ref[...], wglt_ref[...])
        dx, dg = _rms_bwd(xh, r, g, dh)
        dx_ref[...] = dxin_ref[...] + dx
        dg_ref[...] += dg

    return _rowcall(name, body, n_tok, tm, [x, dxin, dcin, dsg, dq, dk, dv, dfl, dgl], [gain], [w1_t, wf_t, wgl_t],
                    [(D_MODEL, F32), (D_MODEL, BF16), (3 * FOX_W, BF16), (LANES, BF16)], [((1, D_MODEL), F32)])


def _conv_taps(z, prev_z, i):
    tm = z.shape[0]
    row = lax.broadcasted_iota(jnp.int32, (tm, 1), 0)
    live = (i > 0).astype(F32)
    p1, p2 = prev_z[HALO - 1:HALO, :] * live, prev_z[HALO - 2:HALO - 1, :] * live
    z1 = jnp.where(row == 0, p1, pltpu.roll(z, 1, 0))
    z2 = jnp.where(row == 0, p2, jnp.where(row == 1, p1, pltpu.roll(z, 2, 0)))
    return z1, z2


def _prev_spec(tm, cols):
    return pl.BlockSpec((HALO, cols), lambda i: (jnp.maximum(i * (tm // HALO) - 1, 0), 0))


def _next_spec(tm, cols, n):
    last = n * (tm // HALO) - 1
    return pl.BlockSpec((HALO, cols), lambda i: (jnp.minimum((i + 1) * (tm // HALO), last), 0))


def _conv_fwd(name, cin, cw):
    n_tok, tm = cin.shape[0], min(ROW_BLOCK, cin.shape[0])
    w = CONV_W

    def body(i, n, c_ref, p_ref, cw_ref, ya_ref):
        c_v, p_v = c_ref[...].astype(F32), p_ref[...].astype(F32)
        z = c_v[:, w:2 * w] * c_v[:, 2 * w:]
        z1, z2 = _conv_taps(z, p_v[:, w:2 * w] * p_v[:, 2 * w:], i)
        y = cw_ref[0:1, :] * z2 + cw_ref[1:2, :] * z1 + cw_ref[2:3, :] * z
        ya_ref[...] = (c_v[:, 0:w] * y).astype(BF16)

    return _rowcall(name, body, n_tok, tm, [cin, (cin, _prev_spec(tm, 3 * w))], [cw], [], [(w, BF16)], [])[0]


def _conv_bwd(name, cin, dya, cw):
    n_tok, tm = cin.shape[0], min(ROW_BLOCK, cin.shape[0])
    w = CONV_W
    n_blocks = n_tok // tm

    def body(i, n, c_ref, p_ref, nx_ref, dya_ref, ndya_ref, cw_ref, dc_ref, dcw_ref):
        c_v, p_v = c_ref[...].astype(F32), p_ref[...].astype(F32)
        ab, ac, ah = c_v[:, 0:w], c_v[:, w:2 * w], c_v[:, 2 * w:]
        z = ac * ah
        z1, z2 = _conv_taps(z, p_v[:, w:2 * w] * p_v[:, 2 * w:], i)
        w0, w1, w2 = cw_ref[0:1, :], cw_ref[1:2, :], cw_ref[2:3, :]
        y = w0 * z2 + w1 * z1 + w2 * z
        dya_v = dya_ref[...].astype(F32)
        dy = dya_v * ab
        live = (i < n - 1).astype(F32)
        ndy = ndya_ref[...].astype(F32) * nx_ref[:, 0:w].astype(F32) * live
        row = lax.broadcasted_iota(jnp.int32, (tm, 1), 0)
        dy1 = jnp.where(row == tm - 1, ndy[0:1, :], pltpu.roll(dy, tm - 1, 0))
        dy2 = jnp.where(row == tm - 1, ndy[1:2, :], jnp.where(row == tm - 2, ndy[0:1, :], pltpu.roll(dy, tm - 2, 0)))
        dz = w2 * dy + w1 * dy1 + w0 * dy2
        dc_ref[:, 0:w] = (dya_v * y).astype(BF16)
        dc_ref[:, w:2 * w] = (dz * ah).astype(BF16)
        dc_ref[:, 2 * w:] = (dz * ac).astype(BF16)
        dcw_ref[0:1, :] += jnp.sum(dy * z2, axis=0, keepdims=True)
        dcw_ref[1:2, :] += jnp.sum(dy * z1, axis=0, keepdims=True)
        dcw_ref[2:3, :] += jnp.sum(dy * z, axis=0, keepdims=True)

    return _rowcall(name, body, n_tok, tm,
                    [cin, (cin, _prev_spec(tm, 3 * w)), (cin, _next_spec(tm, 3 * w, n_blocks)),
                     dya, (dya, _next_spec(tm, w, n_blocks))],
                    [cw], [], [(3 * w, BF16)], [((8, w), F32)])


def _sg_common(sg_ref, gn_ref):
    s_v = sg_ref[...].astype(F32)
    u, v = s_v[:, 0:SG_W], s_v[:, SG_W:]
    ug, tu = _gelu(u)
    vg, tv = _gelu(v)
    vh, r = _rms_stats(vg)
    vn = (vh * gn_ref[...]).astype(BF16)
    return u, v, ug, tu, tv, vh, r, vn


def _sg_fwd(name, sgin, gn, sgw, bias_full):
    n_tok, tm = sgin.shape[0], min(ROW_BLOCK, sgin.shape[0])

    def body(i, n, sg_ref, gn_ref, w_ref, bias_ref, yb_ref):
        _, _, ug, _, _, _, _, vn = _sg_common(sg_ref, gn_ref)
        tril = lax.broadcasted_iota(jnp.int32, (CHUNK, CHUNK), 0) >= lax.broadcasted_iota(jnp.int32, (CHUNK, CHUNK), 1)
        wt = [jnp.where(tril, w_ref[g], 0.0).astype(BF16) for g in range(SG_G)]
        for c0 in range(0, tm, CHUNK):
            sv = jnp.concatenate(
                [_dot(wt[g], vn[c0:c0 + CHUNK, g * CHUNK:(g + 1) * CHUNK]) for g in range(SG_G)], axis=1)
            sv = sv + bias_ref[...]
            yb_ref[c0:c0 + CHUNK, :] = (ug[c0:c0 + CHUNK, :] * sv).astype(BF16)

    return _rowcall(name, body, n_tok, tm, [sgin], [gn, sgw, bias_full], [], [(SG_W, BF16)], [])[0]


def _sg_bwd(name, sgin, dyb, gn, sgw, sgw_t, bias_full):
    n_tok, tm = sgin.shape[0], min(ROW_BLOCK, sgin.shape[0])

    def body(i, n, sg_ref, dyb_ref, gn_ref, w_ref, wt_ref, bias_ref, dsg_ref, dgn_ref, dw_ref, db_ref):
        u, v, ug, tu, tv, vh, r, vn = _sg_common(sg_ref, gn_ref)
        r0 = lax.broadcasted_iota(jnp.int32, (CHUNK, CHUNK), 0)
        r1 = lax.broadcasted_iota(jnp.int32, (CHUNK, CHUNK), 1)
        wt = [jnp.where(r0 >= r1, w_ref[g], 0.0).astype(BF16) for g in range(SG_G)]
        wtt = [jnp.where(r0 <= r1, wt_ref[g], 0.0).astype(BF16) for g in range(SG_G)]
        dyb_v = dyb_ref[...].astype(F32)
        dvn_rows = []
        for c0 in range(0, tm, CHUNK):
            rows = slice(c0, c0 + CHUNK)
            svs, dvns = [], []
            dsv = dyb_v[rows, :] * ug[rows, :]
            for g in range(SG_G):
                cols = slice(g * CHUNK, (g + 1) * CHUNK)
                svs.append(_dot(wt[g], vn[rows, cols]))
                dsv_g = dsv[:, cols]
                dsv_b = dsv_g.astype(BF16)
                dvns.append(_dot(wtt[g], dsv_b))
                dw_ref[g] += jnp.where(r0 >= r1, _dot_nt(dsv_b, vn[rows, cols]), 0.0)
                db_ref[g] += jnp.broadcast_to(jnp.sum(dsv_g, axis=1, keepdims=True), (CHUNK, CHUNK))
            sv = jnp.concatenate(svs, axis=1) + bias_ref[...]
            dug = dyb_v[rows, :] * sv
            dsg_ref[rows, 0:SG_W] = (dug * _gelu_grad(u[rows, :], tu[rows, :])).astype(BF16)
            dvn_rows.append(jnp.concatenate(dvns, axis=1))
        dvn = jnp.concatenate(dvn_rows, axis=0)
        dvg, dgn = _rms_bwd(vh, r, gn_ref[...], dvn)
        dsg_ref[:, SG_W:] = (dvg * _gelu_grad(v, tv)).astype(BF16)
        dgn_ref[...] += dgn

    return _rowcall(name, body, n_tok, tm, [sgin, dyb], [gn, sgw, sgw_t, bias_full], [], [(2 * SG_W, BF16)],
                    [((1, SG_W), F32), ((SG_G, CHUNK, CHUNK), F32), ((SG_G, CHUNK, CHUNK), F32)])


def _forget_cumsum(name, fl, bf):
    n_tok, tm = fl.shape[0], min(ROW_BLOCK, fl.shape[0])

    def body(i, n, fl_ref, b_ref, c_ref, carry):
        @pl.when(i == 0)
        def _():
            carry[...] = jnp.zeros(carry.shape, F32)

        z = fl_ref[...] + b_ref[...]
        lf = jnp.minimum(z, 0.0) - jnp.log1p(jnp.exp(-jnp.abs(z)))
        tri = (lax.broadcasted_iota(jnp.int32, (tm, tm), 0) >= lax.broadcasted_iota(jnp.int32, (tm, tm), 1)).astype(BF16)
        c = _split3_dot(tri, lf) + carry[...]
        c_ref[...] = c * LOG2E
        carry[...] = c[tm - 1:tm, :]

    return _rowcall(name, body, n_tok, tm, [fl], [bf], [], [(LANES, F32)], [], scratch=[pltpu.VMEM((1, LANES), F32)])[0]


def _forget_bwd(name, dc, fl, bf):
    n_tok, tm = fl.shape[0], min(ROW_BLOCK, fl.shape[0])
    n = n_tok // tm
    rev = pl.BlockSpec((tm, LANES), lambda i: (n - 1 - i, 0))

    def kern(dc_ref, fl_ref, b_ref, dfl_ref, db_ref, carry):
        @pl.when(pl.program_id(0) == 0)
        def _():
            carry[...] = jnp.zeros(carry.shape, F32)
            db_ref[...] = jnp.zeros(db_ref.shape, F32)

        triu = (lax.broadcasted_iota(jnp.int32, (tm, tm), 0) <= lax.broadcasted_iota(jnp.int32, (tm, tm), 1)).astype(BF16)
        dlf = _split3_dot(triu, dc_ref[...]) + carry[...]
        carry[...] = dlf[0:1, :]
        z = fl_ref[...] + b_ref[...]
        dfl = dlf * jax.nn.sigmoid(-z)
        dfl_ref[...] = dfl
        db_ref[...] += jnp.sum(dfl, axis=0, keepdims=True)

    return pl.pallas_call(
        kern, name=name, grid=(n,), in_specs=[rev, rev, _full_spec((1, LANES))],
        out_specs=[rev, _full_spec((1, LANES))],
        out_shape=[jax.ShapeDtypeStruct((n_tok, LANES), F32), jax.ShapeDtypeStruct((1, LANES), F32)],
        scratch_shapes=[pltpu.VMEM((1, LANES), F32)], compiler_params=_params(1),
    )(dc, fl, bf)


def _lane_pick(c_blk, h):
    lane = lax.broadcasted_iota(jnp.int32, c_blk.shape, 1)
    return jnp.broadcast_to(jnp.sum(jnp.where(lane == h, c_blk, 0.0), axis=1, keepdims=True), c_blk.shape)


def _wide(stat, width):
    return jnp.tile(stat, (1, width // LANES))


def _pair_half(rows):
    return lax.broadcasted_iota(jnp.int32, (rows, LANES), 1) // FOX_D


def _tri_table(nb, lower):
    rows = [(a, e, b) for a in range(nb) for e in range(2) for b in (range(a + 1) if lower else range(a, nb))]
    return tuple(jnp.asarray([r[k] for r in rows], jnp.int32) for k in range(3))


def _flash_fwd(name, qkv, c2, ct2):
    n_tok = qkv.shape[0]
    blk = min(FLASH_BLOCK, n_tok)
    nb = n_tok // blk
    n_pair = FOX_H // 2
    it, et, jt = _tri_table(nb, True)

    def kern(it_ref, et_ref, jt_ref, q_ref, k_ref, v_ref, c_ref, ct_ref, o_ref, lse_ref, m_s, l_s, acc_s, cq_s, qm_s):
        n = pl.program_id(1)
        i, e, j = it_ref[n], et_ref[n], jt_ref[n]
        half = _pair_half(blk)

        @pl.when(j == 0)
        def _():
            m_s[...] = jnp.full(m_s.shape, -jnp.inf, F32)
            l_s[...] = jnp.zeros(l_s.shape, F32)
            acc_s[...] = jnp.zeros(acc_s.shape, F32)
            cq_s[...] = _lane_pick(c_ref[...], 2 * pl.program_id(0) + e)
            qm_s[...] = jnp.where(half == e, q_ref[...], jnp.zeros_like(q_ref[...]))

        def step(on_diagonal):
            t = _dot_nt(qm_s[...], k_ref[...]) * (FOX_SCALE * LOG2E) - ct_ref[0]
            if on_diagonal:
                keep = lax.broadcasted_iota(jnp.int32, (blk, blk), 1) <= lax.broadcasted_iota(jnp.int32, (blk, blk), 0)
                t = jnp.where(keep, t, -jnp.inf)
            m_old = m_s[...]
            m_new = jnp.maximum(m_old, jnp.max(t, axis=1, keepdims=True) + cq_s[...])
            p = jnp.exp2(t - _wide(m_new - cq_s[...], blk))
            alpha = jnp.exp2(m_old - m_new)
            l_s[...] = alpha * l_s[...] + jnp.sum(p, axis=1, keepdims=True)
            acc_s[...] = alpha * acc_s[...] + _dot(p.astype(BF16), v_ref[...])
            m_s[...] = m_new

        @pl.when(j < i)
        def _():
            step(False)

        @pl.when(j == i)
        def _():
            step(True)
            out = (acc_s[...] / l_s[...]).astype(BF16)
            lse_ref[0] = m_s[...] + jnp.log(l_s[...]) * LOG2E

            @pl.when(e == 0)
            def _():
                o_ref[...] = jnp.where(half == 0, out, jnp.zeros_like(out))

            @pl.when(e == 1)
            def _():
                o_ref[...] = jnp.where(half == 1, out, o_ref[...])

    def col_block(first, inner):
        def index(p, n, it_r, et_r, jt_r):
            return ((jt_r[n] if inner else it_r[n]), first + p)
        return pl.BlockSpec((blk, LANES), index)

    grid_spec = pltpu.PrefetchScalarGridSpec(
        num_scalar_prefetch=3, grid=(n_pair, int(it.shape[0])),
        in_specs=[col_block(0, False), col_block(n_pair, True), col_block(2 * n_pair, True),
                  pl.BlockSpec((blk, LANES), lambda p, n, it_r, et_r, jt_r: (it_r[n], 0)),
                  pl.BlockSpec((1, 1, blk), lambda p, n, it_r, et_r, jt_r: (2 * p + et_r[n], 0, jt_r[n]))],
        out_specs=[col_block(0, False),
                   pl.BlockSpec((1, blk, LANES), lambda p, n, it_r, et_r, jt_r: (2 * p + et_r[n], it_r[n], 0))],
        scratch_shapes=[pltpu.VMEM((blk, LANES), F32)] * 4 + [pltpu.VMEM((blk, LANES), BF16)],
    )
    return pl.pallas_call(
        kern, name=name, grid_spec=grid_spec,
        out_shape=[jax.ShapeDtypeStruct((n_tok, FOX_W), BF16), jax.ShapeDtypeStruct((FOX_H, n_tok, LANES), F32)],
        compiler_params=_params(2),
    )(it, et, jt, qkv, qkv, qkv, c2, ct2)


def _flash_delta(name, do, o):
    n_tok, tm = do.shape[0], min(ROW_BLOCK, do.shape[0])

    def body(i, n, do_ref, o_ref, d_ref):
        prod = do_ref[...].astype(F32) * o_ref[...].astype(F32)
        head = (lax.broadcasted_iota(jnp.int32, (FOX_W, LANES), 0) // FOX_D
                == lax.broadcasted_iota(jnp.int32, (FOX_W, LANES), 1)).astype(BF16)
        hi = prod.astype(BF16)
        r1 = prod - hi.astype(F32)
        mid = r1.astype(BF16)
        lo = (r1 - mid.astype(F32)).astype(BF16)
        d_ref[...] = _dot(hi, head) + _dot(mid, head) + _dot(lo, head)

    return _rowcall(name, body, n_tok, tm, [do, o], [], [], [(LANES, F32)], [])[0]


def _flash_bwd(name, qkv, do, r_row, d_row, c2):
    n_tok = qkv.shape[0]
    blk = min(FLASH_BLOCK, n_tok)
    nb = n_tok // blk
    n_pair = FOX_H // 2
    jt, et, it = _tri_table(nb, False)

    def kern(jt_ref, et_ref, it_ref, k_ref, v_ref, q_ref, do_ref, rr_ref, dr_ref, c_ref,
             dq_ref, dk_ref, dv_ref, dcq_ref, dck_ref, km_s, vm_s, ck_s):
        n = pl.program_id(1)
        j, e, i = jt_ref[n], et_ref[n], it_ref[n]
        half = _pair_half(blk)
        mine = half == e

        @pl.when(n == 0)
        def _():
            dq_ref[...] = jnp.zeros(dq_ref.shape, F32)
            dcq_ref[...] = jnp.zeros(dcq_ref.shape, F32)

        @pl.when((e == 0) & (i == j))
        def _():
            dk_ref[...] = jnp.zeros(dk_ref.shape, F32)
            dv_ref[...] = jnp.zeros(dv_ref.shape, F32)

        @pl.when(i == j)
        def _():
            km_s[...] = jnp.where(mine, k_ref[...], jnp.zeros_like(k_ref[...]))
            vm_s[...] = jnp.where(mine, v_ref[...], jnp.zeros_like(v_ref[...]))
            ck_s[...] = _lane_pick(c_ref[...], 2 * pl.program_id(0) + e)
            dck_ref[...] = jnp.zeros(dck_ref.shape, F32)

        def step(on_diagonal):
            q_v, do_v = q_ref[...], do_ref[...]
            t = _dot_nt(km_s[...], q_v) * (FOX_SCALE * LOG2E) - _wide(ck_s[...], blk) - rr_ref[0]
            if on_diagonal:
                keep = lax.broadcasted_iota(jnp.int32, (blk, blk), 1) >= lax.broadcasted_iota(jnp.int32, (blk, blk), 0)
                t = jnp.where(keep, t, -jnp.inf)
            p = jnp.exp2(t)
            ds = p * (_dot_nt(vm_s[...], do_v) - dr_ref[0])
            p_b, ds_b = p.astype(BF16), ds.astype(BF16)
            dv_ref[...] += jnp.where(mine, _dot(p_b, do_v), 0.0)
            dk_ref[...] += jnp.where(mine, _dot(ds_b, q_v) * FOX_SCALE, 0.0)
            rows = pl.ds(pl.multiple_of(i * blk, blk), blk)
            dq_ref[rows, :] += jnp.where(mine, _dot_tn(ds_b, k_ref[...]) * FOX_SCALE, 0.0)
            dck_ref[0] -= jnp.broadcast_to(jnp.sum(ds, axis=1, keepdims=True), (blk, LANES))
            sub = lax.broadcasted_iota(jnp.int32, (2, blk), 0)
            dcq_ref[0, :, rows] += jnp.where(sub == e, jnp.sum(ds, axis=0, keepdims=True), 0.0)

        @pl.when(i > j)
        def _():
            step(False)

        @pl.when(i == j)
        def _():
            step(True)

    def col_block(first, inner):
        def index(p, n, jt_r, et_r, it_r):
            return ((it_r[n] if inner else jt_r[n]), first + p)
        return pl.BlockSpec((blk, LANES), index)

    def row_stat():
        return pl.BlockSpec((1, 1, blk), lambda p, n, jt_r, et_r, it_r: (2 * p + et_r[n], 0, it_r[n]))

    grid_spec = pltpu.PrefetchScalarGridSpec(
        num_scalar_prefetch=3, grid=(n_pair, int(jt.shape[0])),
        in_specs=[col_block(n_pair, False), col_block(2 * n_pair, False), col_block(0, True), col_block(0, True),
                  row_stat(), row_stat(), pl.BlockSpec((blk, LANES), lambda p, n, jt_r, et_r, it_r: (jt_r[n], 0))],
        out_specs=[pl.BlockSpec((n_tok, LANES), lambda p, n, jt_r, et_r, it_r: (0, p)),
                   col_block(0, False), col_block(0, False),
                   pl.BlockSpec((1, 2, n_tok), lambda p, n, jt_r, et_r, it_r: (p, 0, 0)),
                   pl.BlockSpec((1, blk, LANES), lambda p, n, jt_r, et_r, it_r: (2 * p + et_r[n], jt_r[n], 0))],
        scratch_shapes=[pltpu.VMEM((blk, LANES), BF16), pltpu.VMEM((blk, LANES), BF16), pltpu.VMEM((blk, LANES), F32)],
    )
    wide = jax.ShapeDtypeStruct((n_tok, FOX_W), F32)
    return pl.pallas_call(
        kern, name=name, grid_spec=grid_spec,
        out_shape=[wide, wide, wide, jax.ShapeDtypeStruct((n_pair, 2, n_tok), F32),
                   jax.ShapeDtypeStruct((FOX_H, n_tok, LANES), F32)],
        compiler_params=_params(2),
    )(jt, et, it, qkv, qkv, qkv, do, r_row, d_row, c2)


def _merge_fwd(name, x, ya, yb, yc, gl, wb, wo):
    n_tok, tm = x.shape[0], min(ROW_BLOCK, x.shape[0])

    def body(i, n, x_ref, ya_ref, yb_ref, yc_ref, gl_ref, wb_ref, wo_ref, xo_ref):
        merged = jnp.zeros((tm, D_MODEL), F32)
        for k, y_ref in enumerate((ya_ref, yb_ref, yc_ref)):
            gate = jax.nn.sigmoid(gl_ref[:, k * D_MODEL:(k + 1) * D_MODEL].astype(F32))
            merged = merged + gate * _dot(y_ref[...], wb_ref[k])
        xo_ref[...] = x_ref[...] + _dot(merged.astype(BF16), wo_ref[...])

    return _rowcall(name, body, n_tok, tm, [x, ya, yb, yc, gl], [], [wb, wo], [(D_MODEL, F32)], [])[0]


def _merge_bwd(name, dxo, ya, yb, yc, gl, wb, wb_t, wo_t):
    n_tok, tm = dxo.shape[0], min(ROW_BLOCK, dxo.shape[0])

    def body(i, n, dxo_ref, ya_ref, yb_ref, yc_ref, gl_ref, wb_ref, wbt_ref, wot_ref,
             mg_ref, dob_ref, dgl_ref, dbr_ref, dya_ref, dyb_ref, dyc_ref):
        dob = dxo_ref[...].astype(BF16)
        dob_ref[...] = dob
        dm = _dot(dob, wot_ref[...])
        merged = jnp.zeros((tm, D_MODEL), F32)
        for k, (y_ref, dy_ref) in enumerate(((ya_ref, dya_ref), (yb_ref, dyb_ref), (yc_ref, dyc_ref))):
            cols = slice(k * D_MODEL, (k + 1) * D_MODEL)
            gate = jax.nn.sigmoid(gl_ref[:, cols].astype(F32))
            br = _dot(y_ref[...], wb_ref[k])
            merged = merged + gate * br
            dgl_ref[:, cols] = (dm * br * gate * (1.0 - gate)).astype(BF16)
            dbr = (dm * gate).astype(BF16)
            dbr_ref[:, cols] = dbr
            dy_ref[...] = _dot(dbr, wbt_ref[k]).astype(BF16)
        mg_ref[...] = merged.astype(BF16)

    return _rowcall(name, body, n_tok, tm, [dxo, ya, yb, yc, gl], [], [wb, wb_t, wo_t],
                    [(D_MODEL, BF16), (D_MODEL, BF16), (3 * D_MODEL, BF16), (3 * D_MODEL, BF16),
                     (CONV_W, BF16), (SG_W, BF16), (FOX_W, BF16)], [])


def _mem_fwd(name, mem, gain, wk, wv):
    n_mem = mem.shape[0]

    def kern(m_ref, g_ref, wk_ref, wv_ref, mn_ref, k_ref, v_ref):
        mh, _ = _rms_stats(m_ref[...])
        mn = (mh * g_ref[...]).astype(BF16)
        mn_ref[...] = mn
        k_ref[...] = _dot(mn, wk_ref[...]).astype(BF16)
        v_ref[...] = _dot(mn, wv_ref[...]).astype(BF16)

    shp = jax.ShapeDtypeStruct((n_mem, D_MODEL), BF16)
    return pl.pallas_call(
        kern, name=name, grid=(1,),
        in_specs=[_full_spec(mem.shape), _full_spec(gain.shape), _full_spec(wk.shape), _full_spec(wv.shape)],
        out_specs=[_full_spec(shp.shape)] * 3, out_shape=[shp] * 3, compiler_params=_params(1),
    )(mem, gain, wk, wv)


def _mem_bwd(name, mem, gain, dkx, dvx, wk_t, wv_t):
    n_mem = mem.shape[0]

    def kern(m_ref, g_ref, dk_ref, dv_ref, wkt_ref, wvt_ref, dkb_ref, dvb_ref, dg_ref):
        mh, _ = _rms_stats(m_ref[...])
        dkb, dvb = dk_ref[...].astype(BF16), dv_ref[...].astype(BF16)
        dkb_ref[...] = dkb
        dvb_ref[...] = dvb
        dm = _dot(dkb, wkt_ref[...]) + _dot(dvb, wvt_ref[...])
        dg_ref[...] = jnp.sum(dm * mh, axis=0, keepdims=True)

    shp = jax.ShapeDtypeStruct((n_mem, D_MODEL), BF16)
    args = (mem, gain, dkx, dvx, wk_t, wv_t)
    return pl.pallas_call(
        kern, name=name, grid=(1,), in_specs=[_full_spec(a.shape) for a in args],
        out_specs=[_full_spec(shp.shape)] * 2 + [_full_spec((1, D_MODEL))],
        out_shape=[shp, shp, jax.ShapeDtypeStruct((1, D_MODEL), F32)], compiler_params=_params(1),
    )(*args)


def _xa_probs(q_b, kx_ref, hd):
    cols = slice(hd * XA_D, (hd + 1) * XA_D)
    s = _dot_nt(q_b[:, cols], kx_ref[:, cols]) * (XA_D ** -0.5)
    e = jnp.exp(s - jnp.max(s, axis=1, keepdims=True))
    return e / jnp.sum(e, axis=1, keepdims=True)


def _xa_fwd(name, x, gain, kx, vx, wq, wo):
    n_tok, tm = x.shape[0], min(ROW_BLOCK, x.shape[0])

    def body(i, n, x_ref, g_ref, kx_ref, vx_ref, wq_ref, wo_ref, xo_ref):
        x_v = x_ref[...]
        xh, _ = _rms_stats(x_v)
        q_b = _dot((xh * g_ref[...]).astype(BF16), wq_ref[...]).astype(BF16)
        o = jnp.concatenate(
            [_dot(_xa_probs(q_b, kx_ref, hd).astype(BF16), vx_ref[:, hd * XA_D:(hd + 1) * XA_D]) for hd in range(XA_H)],
            axis=1)
        xo_ref[...] = x_v + _dot(o.astype(BF16), wo_ref[...])

    return _rowcall(name, body, n_tok, tm, [x], [gain, kx, vx], [wq, wo], [(D_MODEL, F32)], [])[0]


def _xa_bwd(name, x, dxo, gain, kx, vx, wq, wq_t, wo_t):
    n_tok, tm = x.shape[0], min(ROW_BLOCK, x.shape[0])
    n_mem = kx.shape[0]

    def body(i, n, x_ref, dxo_ref, g_ref, kx_ref, vx_ref, wq_ref, wqt_ref, wot_ref,
             dx_ref, h_ref, o_ref, dq_ref, dy_ref, dkx_ref, dvx_ref, dg_ref):
        g = g_ref[...]
        xh, r = _rms_stats(x_ref[...])
        h = (xh * g).astype(BF16)
        h_ref[...] = h
        q_b = _dot(h, wq_ref[...]).astype(BF16)
        dxo_v = dxo_ref[...]
        dy = dxo_v.astype(BF16)
        dy_ref[...] = dy
        do = _dot(dy, wot_ref[...])
        for hd in range(XA_H):
            cols = slice(hd * XA_D, (hd + 1) * XA_D)
            p = _xa_probs(q_b, kx_ref, hd)
            p_b = p.astype(BF16)
            o_ref[:, cols] = _dot(p_b, vx_ref[:, cols]).astype(BF16)
            do_h = do[:, cols].astype(BF16)
            dvx_ref[:, cols] += _dot_tn(p_b, do_h)
            dp = _dot_nt(do_h, vx_ref[:, cols])
            ds = p * (dp - jnp.sum(dp * p, axis=1, keepdims=True))
            ds_b = (ds * (XA_D ** -0.5)).astype(BF16)
            dq_ref[:, cols] = _dot(ds_b, kx_ref[:, cols]).astype(BF16)
            dkx_ref[:, cols] += _dot_tn(ds_b, q_b[:, cols])
        dh = _dot(dq_ref[...], wqt_ref[...])
        dx, dg = _rms_bwd(xh, r, g, dh)
        dx_ref[...] = dxo_v + dx
        dg_ref[...] += dg

    return _rowcall(name, body, n_tok, tm, [x, dxo], [gain, kx, vx], [wq, wq_t, wo_t],
                    [(D_MODEL, F32), (D_MODEL, BF16), (D_MODEL, BF16), (D_MODEL, BF16), (D_MODEL, BF16)],
                    [((n_mem, D_MODEL), F32), ((n_mem, D_MODEL), F32), ((1, D_MODEL), F32)])


def _loss_head(name, x, target, gain):
    n_tok, tm = x.shape[0], min(ROW_BLOCK, x.shape[0])

    def body(i, n, x_ref, t_ref, g_ref, dx_ref, loss_ref, dg_ref):
        g = g_ref[...]
        xh, r = _rms_stats(x_ref[...])
        err = xh * g - t_ref[...]
        loss_ref[...] += 0.5 * jnp.sum(jnp.sum(err * err, axis=1, keepdims=True) / D_MODEL, axis=0, keepdims=True)
        dx, dg = _rms_bwd(xh, r, g, err / D_MODEL)
        dx_ref[...] = dx
        dg_ref[...] += dg

    return _rowcall(name, body, n_tok, tm, [x, target], [gain], [], [(D_MODEL, F32)],
                    [((8, LANES), F32), ((1, D_MODEL), F32)])


def _adamw(name, w, g, m, v):
    rows, cols = w.shape
    tr = 128 if rows % 128 == 0 else rows

    def kern(w_ref, g_ref, m_ref, v_ref, d_ref, mo_ref, vo_ref):
        g_v = g_ref[...]
        m_new = ADAM_B1 * m_ref[...] + (1.0 - ADAM_B1) * g_v
        v_new = ADAM_B2 * v_ref[...] + (1.0 - ADAM_B2) * (g_v * g_v)
        m_hat = m_new / (1.0 - ADAM_B1 ** ADAM_STEP)
        v_hat = v_new / (1.0 - ADAM_B2 ** ADAM_STEP)
        d_ref[...] = -ADAM_LR * (m_hat / (jnp.sqrt(v_hat) + ADAM_EPS) + ADAM_WD * w_ref[...])
        mo_ref[...] = m_new
        vo_ref[...] = v_new

    spec = pl.BlockSpec((tr, cols), lambda i: (i, 0))
    shp = jax.ShapeDtypeStruct((rows, cols), F32)
    return pl.pallas_call(kern, name=name, grid=(rows // tr,), in_specs=[spec] * 4, out_specs=[spec] * 3,
                          out_shape=[shp] * 3, compiler_params=_params(1))(w, g, m, v)


def _add_pair(name, gp, got, core):
    _, n, rows, cols = gp.shape
    spec = pl.BlockSpec((1, PACK_ROW_BLOCK, cols), lambda j, r, core_ref: (j, r, 0))
    mine = pl.BlockSpec((1, PACK_ROW_BLOCK, cols), lambda j, r, core_ref: (core_ref[0] * n + j, r, 0))

    def kern(core_ref, a_ref, b_ref, o_ref):
        o_ref[...] = (a_ref[...] + b_ref[...]).astype(BF16)

    grid_spec = pltpu.PrefetchScalarGridSpec(num_scalar_prefetch=1, grid=(n, rows // PACK_ROW_BLOCK),
                                             in_specs=[mine, spec], out_specs=spec)
    return pl.pallas_call(kern, name=name, grid_spec=grid_spec, out_shape=jax.ShapeDtypeStruct(got.shape, BF16),
                          compiler_params=_params(2))(core, gp.reshape(2 * n, rows, cols), got)


def _sum_slots(name, a, tr):
    n, rows, cols = a.shape

    def kern(a_ref, o_ref):
        acc = a_ref[0].astype(F32)
        for k in range(1, n):
            acc = acc + a_ref[k].astype(F32)
        o_ref[...] = acc

    return pl.pallas_call(kern, name=name, grid=(rows // tr,),
                          in_specs=[pl.BlockSpec((n, tr, cols), lambda r: (0, r, 0))],
                          out_specs=pl.BlockSpec((tr, cols), lambda r: (r, 0)),
                          out_shape=jax.ShapeDtypeStruct((rows, cols), F32), compiler_params=_params(1))(a)


_ANY = pl.BlockSpec(memory_space=pl.ANY)


COMM_CHUNKS = 8


def _place():
    x, y, c = lax.axis_index("x"), lax.axis_index("y"), lax.axis_index("c")
    chips = [(1 - x, y), (x, 1 - y), (1 - x, 1 - y)]
    return x, y, c, chips


def _rows(ref, q, n):
    step = ref.shape[0] // n
    return ref.at[pl.ds(q * step, step)]


def _gather_chips(name, w):
    n = COMM_CHUNKS

    def kern(w_ref, o_ref, send_sems, recv_sems, local_sems):
        x, y, c, chips = _place()
        me, sib = 2 * x + y, (x, y, 1 - c)

        def copy(k, src, dst, to):
            return pltpu.make_async_remote_copy(src_ref=src, dst_ref=dst, send_sem=send_sems.at[k],
                                                recv_sem=recv_sems.at[k], device_id=to, device_id_type=MESH_ID)

        mine = [pltpu.make_async_copy(_rows(w_ref.at[h], q, n), _rows(o_ref.at[me, h], q, n), local_sems.at[h * n + q])
                for h in range(2) for q in range(n)]
        first = [copy(k * n + q, _rows(w_ref.at[c], q, n), _rows(o_ref.at[me, c], q, n), (cx, cy, c))
                 for k, (cx, cy) in enumerate(chips) for q in range(n)]
        for cp in mine + first:
            cp.start()
        passed = []
        for k, (cx, cy) in enumerate(chips):
            for q in range(n):
                landed = _rows(o_ref.at[2 * cx + cy, c], q, n)
                copy(k * n + q, landed, landed, (x, y, c)).wait_recv()
                passed.append(copy((3 + k) * n + q, landed, landed, sib))
                passed[-1].start()
        for k, (cx, cy) in enumerate(chips):
            for q in range(n):
                theirs = _rows(o_ref.at[2 * cx + cy, 1 - c], q, n)
                copy((3 + k) * n + q, theirs, theirs, (x, y, c)).wait_recv()
        for cp in first + passed:
            cp.wait_send()
        for cp in mine:
            cp.wait()

    return pl.pallas_call(
        kern, name=name, in_specs=[_ANY], out_specs=_ANY,
        out_shape=jax.ShapeDtypeStruct((N_CHIPS,) + w.shape, w.dtype),
        scratch_shapes=[pltpu.SemaphoreType.DMA((6 * n,)), pltpu.SemaphoreType.DMA((6 * n,)),
                        pltpu.SemaphoreType.DMA((2 * n,))],
        compiler_params=pltpu.CompilerParams(has_side_effects=True),
    )(w)


def _gather_all(name, v):
    def kern(v_ref, o_ref, send_sems, recv_sems, local_sem):
        x, y, c, _ = _place()
        me = 4 * x + 2 * y + c
        mine = pltpu.make_async_copy(v_ref, o_ref.at[me], local_sem)
        mine.start()
        peers = []
        for k in range(1, 8):
            px = 1 - x if k & 4 else x
            py = 1 - y if k & 2 else y
            pc = 1 - c if k & 1 else c
            peers.append((px, py, pc))
        copies = [pltpu.make_async_remote_copy(src_ref=v_ref, dst_ref=o_ref.at[me], send_sem=send_sems.at[k],
                                               recv_sem=recv_sems.at[k], device_id=p, device_id_type=MESH_ID)
                  for k, p in enumerate(peers)]
        for cp in copies:
            cp.start()
        for k, (px, py, pc) in enumerate(peers):
            pltpu.make_async_remote_copy(src_ref=v_ref, dst_ref=o_ref.at[4 * px + 2 * py + pc], send_sem=send_sems.at[k],
                                         recv_sem=recv_sems.at[k], device_id=(x, y, c), device_id_type=MESH_ID).wait_recv()
        for cp in copies:
            cp.wait_send()
        mine.wait()

    return pl.pallas_call(
        kern, name=name, in_specs=[_ANY], out_specs=_ANY, out_shape=jax.ShapeDtypeStruct((8,) + v.shape, v.dtype),
        scratch_shapes=[pltpu.SemaphoreType.DMA((7,)), pltpu.SemaphoreType.DMA((7,)), pltpu.SemaphoreType.DMA(())],
        compiler_params=pltpu.CompilerParams(has_side_effects=True),
    )(v)


def _swap_halves(name, gp):
    n = COMM_CHUNKS
    n_slots = gp.shape[1]

    def kern(g_ref, got_ref, send_sems, recv_sems):
        x, y, c, _ = _place()
        swaps = [pltpu.make_async_remote_copy(src_ref=_rows(g_ref.at[1 - c, j], q, n), dst_ref=_rows(got_ref.at[j], q, n),
                                              send_sem=send_sems.at[j * n + q], recv_sem=recv_sems.at[j * n + q],
                                              device_id=(x, y, 1 - c), device_id_type=MESH_ID)
                 for j in range(n_slots) for q in range(n)]
        for cp in swaps:
            cp.start()
        for cp in swaps:
            cp.wait()

    return pl.pallas_call(
        kern, name=name, in_specs=[_ANY], out_specs=_ANY, out_shape=jax.ShapeDtypeStruct(gp.shape[1:], gp.dtype),
        scratch_shapes=[pltpu.SemaphoreType.DMA((n_slots * n,)), pltpu.SemaphoreType.DMA((n_slots * n,))],
        compiler_params=pltpu.CompilerParams(has_side_effects=True),
    )(gp)


def _scatter_chips(name, s):
    n = COMM_CHUNKS

    def kern(s_ref, o_ref, send_sems, recv_sems, local_sems):
        x, y, c, chips = _place()
        me = 2 * x + y
        mine = [pltpu.make_async_copy(_rows(s_ref.at[me], q, n), _rows(o_ref.at[me], q, n), local_sems.at[q])
                for q in range(n)]
        copies = [pltpu.make_async_remote_copy(src_ref=_rows(s_ref.at[2 * cx + cy], q, n), dst_ref=_rows(o_ref.at[me], q, n),
                                               send_sem=send_sems.at[k * n + q], recv_sem=recv_sems.at[k * n + q],
                                               device_id=(cx, cy, c), device_id_type=MESH_ID)
                  for k, (cx, cy) in enumerate(chips) for q in range(n)]
        for cp in mine + copies:
            cp.start()
        for k, (cx, cy) in enumerate(chips):
            for q in range(n):
                landed = _rows(o_ref.at[2 * cx + cy], q, n)
                pltpu.make_async_remote_copy(src_ref=landed, dst_ref=landed, send_sem=send_sems.at[k * n + q],
                                             recv_sem=recv_sems.at[k * n + q], device_id=(x, y, c),
                                             device_id_type=MESH_ID).wait_recv()
        for cp in copies:
            cp.wait_send()
        for cp in mine:
            cp.wait()

    return pl.pallas_call(
        kern, name=name, in_specs=[_ANY], out_specs=_ANY, out_shape=jax.ShapeDtypeStruct(s.shape, s.dtype),
        scratch_shapes=[pltpu.SemaphoreType.DMA((3 * n,)), pltpu.SemaphoreType.DMA((3 * n,)), pltpu.SemaphoreType.DMA((n,))],
        compiler_params=pltpu.CompilerParams(has_side_effects=True),
    )(s)


def _join_halves(name, r):
    n = COMM_CHUNKS

    def kern(r_ref, o_ref, send_sems, recv_sems, local_sems):
        x, y, c, _ = _place()
        mine = [pltpu.make_async_copy(_rows(r_ref, q, n), _rows(o_ref.at[c], q, n), local_sems.at[q]) for q in range(n)]
        sends = [pltpu.make_async_remote_copy(src_ref=_rows(r_ref, q, n), dst_ref=_rows(o_ref.at[c], q, n),
                                              send_sem=send_sems.at[q], recv_sem=recv_sems.at[q],
                                              device_id=(x, y, 1 - c), device_id_type=MESH_ID) for q in range(n)]
        for cp in mine + sends:
            cp.start()
        for q in range(n):
            theirs = _rows(o_ref.at[1 - c], q, n)
            pltpu.make_async_remote_copy(src_ref=theirs, dst_ref=theirs, send_sem=send_sems.at[q], recv_sem=recv_sems.at[q],
                                         device_id=(x, y, c), device_id_type=MESH_ID).wait_recv()
        for cp in sends:
            cp.wait_send()
        for cp in mine:
            cp.wait()

    return pl.pallas_call(
        kern, name=name, in_specs=[_ANY], out_specs=_ANY, out_shape=jax.ShapeDtypeStruct((2,) + r.shape, r.dtype),
        scratch_shapes=[pltpu.SemaphoreType.DMA((n,))] * 3, compiler_params=pltpu.CompilerParams(has_side_effects=True),
    )(r)


def _pack_rows(n_elems):
    rows = -(-n_elems // (2 * PACK_COLS * PACK_ROW_BLOCK)) * PACK_ROW_BLOCK
    return rows


def _pack(pieces, dtype):
    flat = jnp.concatenate([p.astype(dtype).reshape(-1) for p in pieces])
    rows = _pack_rows(flat.shape[0])
    flat = jnp.pad(flat, (0, 2 * rows * PACK_COLS - flat.shape[0]))
    return flat.reshape(2, rows, PACK_COLS)


def _unpack(packed, shapes):
    lead = packed.shape[:-3]
    flat = packed.reshape(lead + (-1,))
    out, off = [], 0
    for shp in shapes:
        size = 1
        for s in shp:
            size *= s
        out.append(flat[..., off:off + size].reshape(lead + tuple(shp)))
        off += size
    return out


def _shard(a, axis, j):
    size = a.shape[axis] // N_CHIPS
    return lax.slice_in_dim(a, j * size, (j + 1) * size, axis=axis)


def _t(a):
    return jnp.swapaxes(a, -1, -2)


def _rows_from_lanes(a):
    return a[:, :FOX_H].T[:, None, :]


def _lanes_from_heads(a):
    return jnp.pad(a[:, 0, :].T, ((0, 0), (0, LANES - FOX_H)))


def _layer_weights(wts, small, l):
    w_in = wts["w_in"][l]
    w = {
        "wf": jnp.pad(w_in[:, W1_COLS:W1_COLS + FOX_H], ((0, 0), (0, LANES - FOX_H))),
        "w1": w_in[:, :W1_COLS], "wgl": w_in[:, W1_COLS + FOX_H:],
    }
    for nm in ("ffn1_w_gate", "ffn1_w_up", "ffn1_w_down", "w_branch", "w_out", "xa_wq", "xa_wk", "xa_wv", "xa_wo",
               "ffn2_w_gate", "ffn2_w_up", "ffn2_w_down"):
        w[nm] = wts[nm][l]
    for nm in list(w):
        w[nm + "_t"] = _t(w[nm])
    for nm in ("ffn1_norm", "mix_norm", "sg_norm", "xa_norm", "mem_norm", "ffn2_norm"):
        w[nm] = small[nm][l][None, :]
    w["conv_w"] = jnp.pad(small["conv_w"][l], ((0, 5), (0, 0)))
    w["sg_w"] = small["sg_w"][l]
    w["sg_w_t"] = _t(small["sg_w"][l])
    w["sg_bias"] = jnp.repeat(small["sg_b"][l].T, CHUNK, axis=1)
    w["fox_b"] = jnp.pad(small["fox_b_f"][l][None, :], ((0, 0), (0, LANES - FOX_H)))
    return w


def _layer_fwd(l, x, mem, w):
    s = {"x0": x}
    x1, s["a1"], s["b1"] = _ffn_fwd(f"ffn1_fwd_{l}", x, w["ffn1_norm"], w["ffn1_w_gate"], w["ffn1_w_up"], w["ffn1_w_down"])
    s["x1"] = x1
    cin, sgin, qkv, fl, gl = _proj_fwd(f"proj_fwd_{l}", x1, w["mix_norm"], w["w1"], w["wf"], w["wgl"])
    ya = _conv_fwd(f"conv_fwd_{l}", cin, w["conv_w"])
    yb = _sg_fwd(f"sg_fwd_{l}", sgin, w["sg_norm"], w["sg_w"], w["sg_bias"])
    c2 = _forget_cumsum(f"forget_fwd_{l}", fl, w["fox_b"])
    yc, lse2 = _flash_fwd(f"flash_fwd_{l}", qkv, c2, _rows_from_lanes(c2))
    x2 = _merge_fwd(f"merge_fwd_{l}", x1, ya, yb, yc, gl, w["w_branch"], w["w_out"])
    s.update(cin=cin, sgin=sgin, qkv=qkv, fl=fl, gl=gl, ya=ya, yb=yb, yc=yc, c2=c2, lse2=lse2, x2=x2)
    s["mn"], s["kx"], s["vx"] = _mem_fwd(f"mem_fwd_{l}", mem, w["mem_norm"], w["xa_wk"], w["xa_wv"])
    x3 = _xa_fwd(f"xa_fwd_{l}", x2, w["xa_norm"], s["kx"], s["vx"], w["xa_wq"], w["xa_wo"])
    s["x3"] = x3
    x4, s["a2"], s["b2"] = _ffn_fwd(f"ffn2_fwd_{l}", x3, w["ffn2_norm"], w["ffn2_w_gate"], w["ffn2_w_up"], w["ffn2_w_down"])
    return x4, s


def _ffn_grads(tag, x, dxo, a, b, w, pre):
    dx, da, db, sv, h, dy, dg = _ffn_bwd(f"{pre}_bwd_{tag}", x, dxo, a, b, w[pre + "_norm"], w[pre + "_w_down_t"],
                                         w[pre + "_w_gate_t"], w[pre + "_w_up_t"])
    g = {
        pre + "_norm": dg[0],
        pre + "_w_gate": _mm_tn(f"{pre}_dwg_{tag}", h, da, D_MODEL, D_FF, tk=D_MODEL, tn=FF_SPLIT),
        pre + "_w_up": _mm_tn(f"{pre}_dwu_{tag}", h, db, D_MODEL, D_FF, tk=D_MODEL, tn=FF_SPLIT),
        pre + "_w_down": _mm_tn(f"{pre}_dwd_{tag}", sv, dy, D_FF, D_MODEL, tk=FF_SPLIT, tn=D_MODEL),
    }
    return dx, g


def _layer_bwd(l, dx, mem, w, s):
    g = {}
    dx, gf = _ffn_grads(l, s["x3"], dx, s["a2"], s["b2"], w, "ffn2")
    g.update(gf)

    dx, h, o, dq, dy, dkx, dvx, dg = _xa_bwd(f"xa_bwd_{l}", s["x2"], dx, w["xa_norm"], s["kx"], s["vx"], w["xa_wq"],
                                              w["xa_wq_t"], w["xa_wo_t"])
    g["xa_norm"] = dg[0]
    g["xa_wq"] = _mm_tn(f"xa_dwq_{l}", h, dq, D_MODEL, D_MODEL, tk=D_MODEL, tn=D_MODEL)
    g["xa_wo"] = _mm_tn(f"xa_dwo_{l}", o, dy, D_MODEL, D_MODEL, tk=D_MODEL, tn=D_MODEL)
    dkb, dvb, dgm = _mem_bwd(f"mem_bwd_{l}", mem, w["mem_norm"], dkx, dvx, w["xa_wk_t"], w["xa_wv_t"])
    g["mem_norm"] = dgm[0]
    g["xa_wk"] = _mm_tn(f"xa_dwk_{l}", s["mn"], dkb, D_MODEL, D_MODEL, tk=D_MODEL, tn=D_MODEL)
    g["xa_wv"] = _mm_tn(f"xa_dwv_{l}", s["mn"], dvb, D_MODEL, D_MODEL, tk=D_MODEL, tn=D_MODEL)

    mg, dob, dgl, dbr, dya, dyb, dyc = _merge_bwd(f"merge_bwd_{l}", dx, s["ya"], s["yb"], s["yc"], s["gl"], w["w_branch"],
                                                  w["w_branch_t"], w["w_out_t"])
    g["w_out"] = _mm_tn(f"dwout_{l}", mg, dob, D_MODEL, D_MODEL, tk=D_MODEL, tn=D_MODEL)
    g["w_branch"] = jnp.stack([
        _mm_tn(f"dwbranch{k}_{l}", y, dbr, CONV_W, D_MODEL, tk=CONV_W, tn=D_MODEL, y_off=k * D_MODEL)
        for k, y in enumerate((s["ya"], s["yb"], s["yc"]))])

    dcin, dcw = _conv_bwd(f"conv_bwd_{l}", s["cin"], dya, w["conv_w"])
    g["conv_w"] = dcw[:3]
    dsg, dgn, dsw, dsb = _sg_bwd(f"sg_bwd_{l}", s["sgin"], dyb, w["sg_norm"], w["sg_w"], w["sg_w_t"], w["sg_bias"])
    g["sg_norm"], g["sg_w"], g["sg_b"] = dgn[0], dsw, dsb[:, :, 0]
    d_row = _rows_from_lanes(_flash_delta(f"flash_delta_{l}", dyc, s["yc"]))
    r_row = s["lse2"][:, :, 0][:, None, :] - _rows_from_lanes(s["c2"])
    dq, dk, dv, dcq, dck = _flash_bwd(f"flash_bwd_{l}", s["qkv"], dyc, r_row, d_row, s["c2"])
    dc = _lanes_from_heads(dcq.reshape(FOX_H, 1, -1) + dck[:, :, 0][:, None, :])
    dfl, dbf = _forget_bwd(f"forget_bwd_{l}", dc, s["fl"], w["fox_b"])
    g["fox_b_f"] = dbf[0, :FOX_H]

    dx, h, dqkv, dfb, dg = _proj_bwd(f"proj_bwd_{l}", s["x1"], dx, dcin, dsg, dq, dk, dv, dfl, dgl, w["mix_norm"],
                                     w["w1_t"], w["wf_t"], w["wgl_t"])
    g["mix_norm"] = dg[0]
    c0, c1 = 3 * CONV_W, 3 * CONV_W + 2 * SG_W
    g["w_in"] = jnp.concatenate([
        _mm_tn(f"dwin_conv_{l}", h, dcin, D_MODEL, c0, tk=D_MODEL, tn=c0),
        _mm_tn(f"dwin_sg_{l}", h, dsg, D_MODEL, c1 - c0, tk=D_MODEL, tn=c1 - c0),
        _mm_tn(f"dwin_qkv_{l}", h, dqkv, D_MODEL, 3 * FOX_W, tk=D_MODEL, tn=3 * FOX_W),
        _mm_tn(f"dwin_f_{l}", h, dfb, D_MODEL, LANES, tk=D_MODEL, tn=LANES)[:, :FOX_H],
        _mm_tn(f"dwin_gl_{l}", h, dgl, D_MODEL, 3 * D_MODEL, tk=D_MODEL, tn=3 * D_MODEL // 2),
    ], axis=1)

    dx, gf = _ffn_grads(l, s["x0"], dx, s["a1"], s["b1"], w, "ffn1")
    g.update(gf)
    return dx, g


def _local_step(x, mem, target, wts, small):
    saved, lw = [], []
    for l in range(DEPTH):
        lw.append(_layer_weights(wts, small, l))
        x, s = _layer_fwd(l, x, mem, lw[l])
        saved.append(s)
    fin = small["final_norm"][None, :]
    dx, loss, dgf = _loss_head("loss_head", x, target, fin)
    layer_grads = [None] * DEPTH
    for l in reversed(range(DEPTH)):
        dx, layer_grads[l] = _layer_bwd(l, dx, mem, lw[l], saved[l])
    grads = {nm: jnp.stack([layer_grads[l][nm] for l in range(DEPTH)]) for nm in WEIGHTS if nm != "final_norm"}
    grads["final_norm"] = dgf[0]
    return loss[0, 0], dx, grads


def _small_slab(vals):
    rows = []
    for v in vals:
        flat = v.astype(F32).reshape(-1)
        n = -(-flat.shape[0] // LANES) * LANES
        rows.append(jnp.pad(flat, (0, n - flat.shape[0])).reshape(-1, LANES))
    slab = jnp.concatenate(rows)
    pad = -slab.shape[0] % 8
    return jnp.pad(slab, ((0, pad), (0, 0)))


def _small_unslab(slab, shapes):
    out, off = [], 0
    for shp in shapes:
        size = 1
        for s in shp:
            size *= s
        n_rows = -(-size // LANES)
        out.append(slab[off:off + n_rows].reshape(-1)[:size].reshape(shp))
        off += n_rows
    return out


def kernel(x, mem, ffn1_norm, ffn1_w_gate, ffn1_w_up, ffn1_w_down, mix_norm, w_in, conv_w, sg_norm, sg_w, sg_b, fox_b_f, w_branch, w_out, xa_norm, mem_norm, xa_wq, xa_wk, xa_wv, xa_wo, ffn2_norm, ffn2_w_gate, ffn2_w_up, ffn2_w_down, final_norm, loss_target, m_ffn1_norm, m_ffn1_w_gate, m_ffn1_w_up, m_ffn1_w_down, m_mix_norm, m_w_in, m_conv_w, m_sg_norm, m_sg_w, m_sg_b, m_fox_b_f, m_w_branch, m_w_out, m_xa_norm, m_mem_norm, m_xa_wq, m_xa_wk, m_xa_wv, m_xa_wo, m_ffn2_norm, m_ffn2_w_gate, m_ffn2_w_up, m_ffn2_w_down, m_final_norm, v_ffn1_norm, v_ffn1_w_gate, v_ffn1_w_up, v_ffn1_w_down, v_mix_norm, v_w_in, v_conv_w, v_sg_norm, v_sg_w, v_sg_b, v_fox_b_f, v_w_branch, v_w_out, v_xa_norm, v_mem_norm, v_xa_wq, v_xa_wk, v_xa_wv, v_xa_wo, v_ffn2_norm, v_ffn2_w_gate, v_ffn2_w_up, v_ffn2_w_down, v_final_norm):
    args = dict(locals())
    wv = {nm: args[nm] for nm in WEIGHTS}
    mv = {nm: args["m_" + nm] for nm in WEIGHTS}
    vv = {nm: args["v_" + nm] for nm in WEIGHTS}
    chip = 2 * lax.axis_index("x") + lax.axis_index("y")
    core = lax.axis_index("c")

    shard_shapes = [wv[nm].shape for nm, _ in BIG]
    gathered = _gather_chips("gather_weights", _pack([wv[nm] for nm, _ in BIG], BF16))
    pieces = _unpack(gathered, shard_shapes)
    wts = {nm: jnp.concatenate([p[j] for j in range(N_CHIPS)], axis=ax) for (nm, ax), p in zip(BIG, pieces)}
    taps = _gather_all("gather_taps", _small_slab([conv_w]))
    taps = [_small_unslab(taps[2 * j], [conv_w.shape])[0] for j in range(N_CHIPS)]
    small = {nm: wv[nm] for nm in SMALL}
    small["conv_w"] = jnp.concatenate(taps, axis=2)

    loss, dx, grads = _local_step(x[0], mem[0], loss_target[0], wts, small)
    loss = lax.psum(loss, ("x", "y", "c"))

    small_shapes = [grads[nm].shape for nm in SMALL]
    slots = _gather_all("gather_small_grads", _small_slab([grads[nm] for nm in SMALL]))
    small_sum = _sum_slots("sum_small_grads", slots, slots.shape[1])
    small_g = dict(zip(SMALL, _small_unslab(small_sum, small_shapes)))
    taps_g = small_g["conv_w"]
    small_g["conv_w"] = lax.dynamic_slice_in_dim(taps_g, chip * conv_w.shape[2], conv_w.shape[2], axis=2)

    gp = jnp.stack([_pack([_shard(grads[nm], ax, j) for nm, ax in BIG], F32) for j in range(N_CHIPS)], axis=1)
    got = _swap_halves("reduce_swap_cores", gp)
    chip_sum = _add_pair("reduce_add_cores", gp, got, core.astype(jnp.int32).reshape(1))
    parts = _scatter_chips("reduce_scatter_chips", chip_sum)
    half = _sum_slots("reduce_add_chips", parts, PACK_ROW_BLOCK)
    full = _join_halves("reduce_join_cores", half)
    big_g = dict(zip([nm for nm, _ in BIG], _unpack(full, shard_shapes)))

    g_out = {**small_g, **big_g}
    delta, new_m, new_v = {}, {}, {}
    for nm, _ in BIG:
        shp = wv[nm].shape
        two_d = (-1, shp[-1])
        d, m2, v2 = _adamw("adamw_" + nm, wv[nm].reshape(two_d), g_out[nm].reshape(two_d), mv[nm].reshape(two_d),
                           vv[nm].reshape(two_d))
        delta[nm], new_m[nm], new_v[nm] = d.reshape(shp), m2.reshape(shp), v2.reshape(shp)
    slab_shapes = [wv[nm].shape for nm in SMALL]
    d, m2, v2 = _adamw("adamw_small", _small_slab([wv[nm] for nm in SMALL]), _small_slab([g_out[nm] for nm in SMALL]),
                       _small_slab([mv[nm] for nm in SMALL]), _small_slab([vv[nm] for nm in SMALL]))
    for out, slab in ((delta, d), (new_m, m2), (new_v, v2)):
        out.update(zip(SMALL, _small_unslab(slab, slab_shapes)))

    return (loss, dx[None], *[g_out[nm] for nm in WEIGHTS], *[delta[nm] for nm in WEIGHTS],
            *[new_m[nm] for nm in WEIGHTS], *[new_v[nm] for nm in WEIGHTS])
```

```python
import functools

import jax
import jax.numpy as jnp
from jax import lax
from jax.experimental import pallas as pl
from jax.experimental.pallas import tpu as pltpu

F32, BF16 = jnp.float32, jnp.bfloat16
MESH_ID = pl.DeviceIdType.MESH

D_MODEL = 1024
DEPTH = 2
D_FF = 2816
CONV_W = 512
SG_W = 512
SG_G = 4
CHUNK = 128
FOX_H = 8
FOX_D = 64
FOX_W = FOX_H * FOX_D
FOX_SCALE = FOX_D ** -0.5
LOG2E = 1.4426950408889634
XA_H = 4
XA_D = D_MODEL // XA_H
N_CHIPS = 4
RMS_EPS = 1e-6
W1_COLS = 3 * CONV_W + 2 * SG_W + 3 * FOX_W
LANES = 128
HALO = 16

ADAM_LR, ADAM_B1, ADAM_B2, ADAM_EPS, ADAM_WD, ADAM_STEP = 0.001, 0.9, 0.999, 1e-08, 0.01, 10

ROW_BLOCK = 256
FLASH_BLOCK = 512
FF_SPLIT = 1408
VMEM_LIMIT = 56 * 1024 * 1024
PACK_COLS = 1024
PACK_ROW_BLOCK = 256

BIG = (
    ("ffn1_w_gate", 2), ("ffn1_w_up", 2), ("ffn1_w_down", 1), ("w_in", 2), ("w_branch", 3), ("w_out", 1),
    ("xa_wq", 1), ("xa_wk", 1), ("xa_wv", 1), ("xa_wo", 1), ("ffn2_w_gate", 2), ("ffn2_w_up", 2), ("ffn2_w_down", 1),
)
SMALL = ("ffn1_norm", "mix_norm", "conv_w", "sg_norm", "sg_w", "sg_b", "fox_b_f", "xa_norm", "mem_norm", "ffn2_norm",
         "final_norm")
WEIGHTS = ("ffn1_norm", "ffn1_w_gate", "ffn1_w_up", "ffn1_w_down", "mix_norm", "w_in", "conv_w", "sg_norm", "sg_w",
           "sg_b", "fox_b_f", "w_branch", "w_out", "xa_norm", "mem_norm", "xa_wq", "xa_wk", "xa_wv", "xa_wo",
           "ffn2_norm", "ffn2_w_gate", "ffn2_w_up", "ffn2_w_down", "final_norm")


def _dot(a, b):
    return jnp.dot(a, b, preferred_element_type=F32)


def _dot_nt(a, b):
    return lax.dot_general(a, b, (((1,), (1,)), ((), ())), preferred_element_type=F32)


def _dot_tn(a, b):
    return lax.dot_general(a, b, (((0,), (0,)), ((), ())), preferred_element_type=F32)


def _rms_stats(x):
    r = lax.rsqrt(jnp.mean(x * x, axis=-1, keepdims=True) + RMS_EPS)
    return x * r, r


def _rms_bwd(xh, r, g, dy):
    dg = jnp.sum(dy * xh, axis=0, keepdims=True)
    dxh = dy * g
    dx = r * (dxh - xh * jnp.mean(dxh * xh, axis=-1, keepdims=True))
    return dx, dg


def _gelu(x):
    k = 0.7978845608028654
    t = jnp.tanh(k * (x + 0.044715 * x * x * x))
    return 0.5 * x * (1.0 + t), t


def _gelu_grad(x, t):
    k = 0.7978845608028654
    return 0.5 * (1.0 + t) + 0.5 * x * (1.0 - t * t) * k * (1.0 + 3.0 * 0.044715 * x * x)


def _split3_dot(tri, x):
    hi = x.astype(BF16)
    r1 = x - hi.astype(F32)
    mid = r1.astype(BF16)
    lo = (r1 - mid.astype(F32)).astype(BF16)
    return _dot(tri, hi) + _dot(tri, mid) + _dot(tri, lo)


def _params(n_grid):
    return pltpu.CompilerParams(dimension_semantics=("arbitrary",) * n_grid, vmem_limit_bytes=VMEM_LIMIT)


def _full_spec(shape):
    nd = len(shape)
    return pl.BlockSpec(tuple(shape), lambda *_: (0,) * nd)


def _rowcall(name, body, n_tokens, tm, rows, consts, residents, row_outs, acc_outs, scratch=()):
    n = n_tokens // tm
    rows = [r if isinstance(r, tuple) else (r, pl.BlockSpec((tm, r.shape[1]), lambda i: (i, 0))) for r in rows]
    n_in, n_w = len(rows) + len(consts), len(residents)
    n_out = len(row_outs) + len(acc_outs)
    in_specs = ([r[1] for r in rows] + [_full_spec(c.shape) for c in consts]
                + [pl.BlockSpec(memory_space=pl.ANY)] * n_w)
    out_shape = ([jax.ShapeDtypeStruct((n_tokens, c), dt) for c, dt in row_outs]
                 + [jax.ShapeDtypeStruct(s, dt) for s, dt in acc_outs])
    out_specs = ([pl.BlockSpec((tm, c), lambda i: (i, 0)) for c, _ in row_outs]
                 + [_full_spec(s) for s, _ in acc_outs])
    scratch_shapes = [pltpu.VMEM(w.shape, w.dtype) for w in residents]
    if n_w:
        scratch_shapes.append(pltpu.SemaphoreType.DMA((n_w,)))
    scratch_shapes += list(scratch)

    def kern(*refs):
        ins, w_hbm = refs[:n_in], refs[n_in:n_in + n_w]
        outs = refs[n_in + n_w:n_in + n_w + n_out]
        rest = refs[n_in + n_w + n_out:]
        w_vmem = rest[:n_w]
        extra = rest[n_w + 1:] if n_w else rest
        i = pl.program_id(0)

        @pl.when(i == 0)
        def _():
            copies = [pltpu.make_async_copy(w_hbm[k], w_vmem[k], rest[n_w].at[k]) for k in range(n_w)]
            for cp in copies:
                cp.start()
            for cp in copies:
                cp.wait()
            for a in outs[len(row_outs):]:
                a[...] = jnp.zeros(a.shape, a.dtype)

        body(i, n, *ins, *w_vmem, *outs, *extra)

    res = pl.pallas_call(
        kern, name=name, grid=(n,), in_specs=in_specs, out_specs=out_specs, out_shape=out_shape,
        scratch_shapes=scratch_shapes, compiler_params=_params(1),
    )(*[r[0] for r in rows], *consts, *residents)
    return res


def _mm_tn(name, x, y, k_dim, n_dim, *, tk, tn, x_off=0, y_off=0, tt=512):
    n_tok = x.shape[0]
    tt = min(tt, n_tok)
    n_t = n_tok // tt
    xb, yb = x_off // tk, y_off // tn

    def kern(x_ref, y_ref, o_ref):
        @pl.when(pl.program_id(2) == 0)
        def _():
            o_ref[...] = jnp.zeros(o_ref.shape, F32)

        o_ref[...] += _dot_tn(x_ref[...], y_ref[...])

    return pl.pallas_call(
        kern, name=name, grid=(k_dim // tk, n_dim // tn, n_t),
        in_specs=[pl.BlockSpec((tt, tk), lambda k, n, t: (t, xb + k)),
                  pl.BlockSpec((tt, tn), lambda k, n, t: (t, yb + n))],
        out_specs=pl.BlockSpec((tk, tn), lambda k, n, t: (k, n)),
        out_shape=jax.ShapeDtypeStruct((k_dim, n_dim), F32),
        compiler_params=_params(3),
    )(x, y)


def _ffn_fwd(name, x, gain, wg, wu, wd):
    n_tok, tm = x.shape[0], min(ROW_BLOCK, x.shape[0])

    def body(i, n, x_ref, g_ref, wg_ref, wu_ref, wd_ref, xo_ref, a_ref, b_ref):
        x_v = x_ref[...]
        xh, _ = _rms_stats(x_v)
        h = (xh * g_ref[...]).astype(BF16)
        y = jnp.zeros((tm, D_MODEL), F32)
        for f0 in range(0, D_FF, FF_SPLIT):
            sl = slice(f0, f0 + FF_SPLIT)
            a = _dot(h, wg_ref[:, sl])
            b = _dot(h, wu_ref[:, sl])
            a_ref[:, sl] = a.astype(BF16)
            b_ref[:, sl] = b.astype(BF16)
            s = (a * jax.nn.sigmoid(a) * b).astype(BF16)
            y = y + _dot(s, wd_ref[sl, :])
        xo_ref[...] = x_v + 0.5 * y

    return _rowcall(name, body, n_tok, tm, [x], [gain], [wg, wu, wd],
                    [(D_MODEL, F32), (D_FF, BF16), (D_FF, BF16)], [])


def _ffn_bwd(name, x, dxo, a, b, gain, wd_t, wg_t, wu_t):
    n_tok, tm = x.shape[0], min(ROW_BLOCK, x.shape[0])

    def body(i, n, x_ref, dxo_ref, a_ref, b_ref, g_ref, wdt_ref, wgt_ref, wut_ref,
             dx_ref, da_ref, db_ref, s_ref, h_ref, dy_ref, dg_ref):
        g = g_ref[...]
        xh, r = _rms_stats(x_ref[...])
        h_ref[...] = (xh * g).astype(BF16)
        dxo_v = dxo_ref[...]
        dy = (0.5 * dxo_v).astype(BF16)
        dy_ref[...] = dy
        dh = jnp.zeros((tm, D_MODEL), F32)
        for f0 in range(0, D_FF, FF_SPLIT):
            sl = slice(f0, f0 + FF_SPLIT)
            a_v = a_ref[:, sl].astype(F32)
            b_v = b_ref[:, sl].astype(F32)
            ds = _dot(dy, wdt_ref[:, sl])
            sig = jax.nn.sigmoid(a_v)
            sa = a_v * sig
            s_ref[:, sl] = (sa * b_v).astype(BF16)
            da = (ds * b_v * (sig * (1.0 + a_v * (1.0 - sig)))).astype(BF16)
            db = (ds * sa).astype(BF16)
            da_ref[:, sl] = da
            db_ref[:, sl] = db
            dh = dh + _dot(da, wgt_ref[sl, :]) + _dot(db, wut_ref[sl, :])
        dx, dg = _rms_bwd(xh, r, g, dh)
        dx_ref[...] = dxo_v + dx
        dg_ref[...] += dg

    return _rowcall(name, body, n_tok, tm, [x, dxo, a, b], [gain], [wd_t, wg_t, wu_t],
                    [(D_MODEL, F32), (D_FF, BF16), (D_FF, BF16), (D_FF, BF16), (D_MODEL, BF16), (D_MODEL, BF16)],
                    [((1, D_MODEL), F32)])


def _proj_fwd(name, x, gain, w1, wf, wgl):
    n_tok, tm = x.shape[0], min(ROW_BLOCK, x.shape[0])
    c0, c1 = 3 * CONV_W, 3 * CONV_W + 2 * SG_W

    def body(i, n, x_ref, g_ref, w1_ref, wf_ref, wgl_ref, cin_ref, sg_ref, qkv_ref, fl_ref, gl_ref):
        xh, _ = _rms_stats(x_ref[...])
        h = (xh * g_ref[...]).astype(BF16)
        cin_ref[...] = _dot(h, w1_ref[:, 0:c0]).astype(BF16)
        sg_ref[...] = _dot(h, w1_ref[:, c0:c1]).astype(BF16)
        qkv_ref[...] = _dot(h, w1_ref[:, c1:W1_COLS]).astype(BF16)
        fl_ref[...] = _dot(h, wf_ref[...])
        gl_ref[...] = _dot(h, wgl_ref[...]).astype(BF16)

    return _rowcall(name, body, n_tok, tm, [x], [gain], [w1, wf, wgl],
                    [(3 * CONV_W, BF16), (2 * SG_W, BF16), (3 * FOX_W, BF16), (LANES, F32), (3 * D_MODEL, BF16)], [])


def _proj_bwd(name, x, dxin, dcin, dsg, dq, dk, dv, dfl, dgl, gain, w1_t, wf_t, wgl_t):
    n_tok, tm = x.shape[0], min(ROW_BLOCK, x.shape[0])
    c0, c1 = 3 * CONV_W, 3 * CONV_W + 2 * SG_W

    def body(i, n, x_ref, dxin_ref, dcin_ref, dsg_ref, dq_ref, dk_ref, dv_ref, dfl_ref, dgl_ref, g_ref,
             w1t_ref, wft_ref, wglt_ref, dx_ref, h_ref, dqkv_ref, dfb_ref, dg_ref):
        g = g_ref[...]
        xh, r = _rms_stats(x_ref[...])
        h_ref[...] = (xh * g).astype(BF16)
        dfb = dfl_ref[...].astype(BF16)
        dfb_ref[...] = dfb
        dh = _dot(dcin_ref[...], w1t_ref[0:c0, :])
        dh = dh + _dot(dsg_ref[...], w1t_ref[c0:c1, :])
        for k, d_ref in enumerate((dq_ref, dk_ref, dv_ref)):
            d_b = d_ref[...].astype(BF16)
            dqkv_ref[:, k * FOX_W:(k + 1) * FOX_W] = d_b
            dh = dh + _dot(d_b, w1t_ref[c1 + k * FOX_W:c1 + (k + 1) * FOX_W, :])
        dh = dh + _dot(dfb, wft_ref[...])
        dh = dh + _dot(dgl_ref[...], wglt_ref[...])
        dx, dg = _rms_bwd(xh, r, g, dh)
        dx_ref[...] = dxin_ref[...] + dx
        dg_ref[...] += dg

    return _rowcall(name, body, n_tok, tm, [x, dxin, dcin, dsg, dq, dk, dv, dfl, dgl], [gain], [w1_t, wf_t, wgl_t],
                    [(D_MODEL, F32), (D_MODEL, BF16), (3 * FOX_W, BF16), (LANES, BF16)], [((1, D_MODEL), F32)])


def _conv_taps(z, prev_z, i):
    tm = z.shape[0]
    row = lax.broadcasted_iota(jnp.int32, (tm, 1), 0)
    live = (i > 0).astype(F32)
    p1, p2 = prev_z[HALO - 1:HALO, :] * live, prev_z[HALO - 2:HALO - 1, :] * live
    z1 = jnp.where(row == 0, p1, pltpu.roll(z, 1, 0))
    z2 = jnp.where(row == 0, p2, jnp.where(row == 1, p1, pltpu.roll(z, 2, 0)))
    return z1, z2


def _prev_spec(tm, cols):
    return pl.BlockSpec((HALO, cols), lambda i: (jnp.maximum(i * (tm // HALO) - 1, 0), 0))


def _next_spec(tm, cols, n):
    last = n * (tm // HALO) - 1
    return pl.BlockSpec((HALO, cols), lambda i: (jnp.minimum((i + 1) * (tm // HALO), last), 0))


def _conv_fwd(name, cin, cw):
    n_tok, tm = cin.shape[0], min(ROW_BLOCK, cin.shape[0])
    w = CONV_W

    def body(i, n, c_ref, p_ref, cw_ref, ya_ref):
        c_v, p_v = c_ref[...].astype(F32), p_ref[...].astype(F32)
        z = c_v[:, w:2 * w] * c_v[:, 2 * w:]
        z1, z2 = _conv_taps(z, p_v[:, w:2 * w] * p_v[:, 2 * w:], i)
        y = cw_ref[0:1, :] * z2 + cw_ref[1:2, :] * z1 + cw_ref[2:3, :] * z
        ya_ref[...] = (c_v[:, 0:w] * y).astype(BF16)

    return _rowcall(name, body, n_tok, tm, [cin, (cin, _prev_spec(tm, 3 * w))], [cw], [], [(w, BF16)], [])[0]


def _conv_bwd(name, cin, dya, cw):
    n_tok, tm = cin.shape[0], min(ROW_BLOCK, cin.shape[0])
    w = CONV_W
    n_blocks = n_tok // tm

    def body(i, n, c_ref, p_ref, nx_ref, dya_ref, ndya_ref, cw_ref, dc_ref, dcw_ref):
        c_v, p_v = c_ref[...].astype(F32), p_ref[...].astype(F32)
        ab, ac, ah = c_v[:, 0:w], c_v[:, w:2 * w], c_v[:, 2 * w:]
        z = ac * ah
        z1, z2 = _conv_taps(z, p_v[:, w:2 * w] * p_v[:, 2 * w:], i)
        w0, w1, w2 = cw_ref[0:1, :], cw_ref[1:2, :], cw_ref[2:3, :]
        y = w0 * z2 + w1 * z1 + w2 * z
        dya_v = dya_ref[...].astype(F32)
        dy = dya_v * ab
        live = (i < n - 1).astype(F32)
        ndy = ndya_ref[...].astype(F32) * nx_ref[:, 0:w].astype(F32) * live
        row = lax.broadcasted_iota(jnp.int32, (tm, 1), 0)
        dy1 = jnp.where(row == tm - 1, ndy[0:1, :], pltpu.roll(dy, tm - 1, 0))
        dy2 = jnp.where(row == tm - 1, ndy[1:2, :], jnp.where(row == tm - 2, ndy[0:1, :], pltpu.roll(dy, tm - 2, 0)))
        dz = w2 * dy + w1 * dy1 + w0 * dy2
        dc_ref[:, 0:w] = (dya_v * y).astype(BF16)
        dc_ref[:, w:2 * w] = (dz * ah).astype(BF16)
        dc_ref[:, 2 * w:] = (dz * ac).astype(BF16)
        dcw_ref[0:1, :] += jnp.sum(dy * z2, axis=0, keepdims=True)
        dcw_ref[1:2, :] += jnp.sum(dy * z1, axis=0, keepdims=True)
        dcw_ref[2:3, :] += jnp.sum(dy * z, axis=0, keepdims=True)

    return _rowcall(name, body, n_tok, tm,
                    [cin, (cin, _prev_spec(tm, 3 * w)), (cin, _next_spec(tm, 3 * w, n_blocks)),
                     dya, (dya, _next_spec(tm, w, n_blocks))],
                    [cw], [], [(3 * w, BF16)], [((8, w), F32)])


def _sg_common(sg_ref, gn_ref):
    s_v = sg_ref[...].astype(F32)
    u, v = s_v[:, 0:SG_W], s_v[:, SG_W:]
    ug, tu = _gelu(u)
    vg, tv = _gelu(v)
    vh, r = _rms_stats(vg)
    vn = (vh * gn_ref[...]).astype(BF16)
    return u, v, ug, tu, tv, vh, r, vn


def _sg_fwd(name, sgin, gn, sgw, bias_full):
    n_tok, tm = sgin.shape[0], min(ROW_BLOCK, sgin.shape[0])

    def body(i, n, sg_ref, gn_ref, w_ref, bias_ref, yb_ref):
        _, _, ug, _, _, _, _, vn = _sg_common(sg_ref, gn_ref)
        tril = lax.broadcasted_iota(jnp.int32, (CHUNK, CHUNK), 0) >= lax.broadcasted_iota(jnp.int32, (CHUNK, CHUNK), 1)
        wt = [jnp.where(tril, w_ref[g], 0.0).astype(BF16) for g in range(SG_G)]
        for c0 in range(0, tm, CHUNK):
            sv = jnp.concatenate(
                [_dot(wt[g], vn[c0:c0 + CHUNK, g * CHUNK:(g + 1) * CHUNK]) for g in range(SG_G)], axis=1)
            sv = sv + bias_ref[...]
            yb_ref[c0:c0 + CHUNK, :] = (ug[c0:c0 + CHUNK, :] * sv).astype(BF16)

    return _rowcall(name, body, n_tok, tm, [sgin], [gn, sgw, bias_full], [], [(SG_W, BF16)], [])[0]


def _sg_bwd(name, sgin, dyb, gn, sgw, sgw_t, bias_full):
    n_tok, tm = sgin.shape[0], min(ROW_BLOCK, sgin.shape[0])

    def body(i, n, sg_ref, dyb_ref, gn_ref, w_ref, wt_ref, bias_ref, dsg_ref, dgn_ref, dw_ref, db_ref):
        u, v, ug, tu, tv, vh, r, vn = _sg_common(sg_ref, gn_ref)
        r0 = lax.broadcasted_iota(jnp.int32, (CHUNK, CHUNK), 0)
        r1 = lax.broadcasted_iota(jnp.int32, (CHUNK, CHUNK), 1)
        wt = [jnp.where(r0 >= r1, w_ref[g], 0.0).astype(BF16) for g in range(SG_G)]
        wtt = [jnp.where(r0 <= r1, wt_ref[g], 0.0).astype(BF16) for g in range(SG_G)]
        dyb_v = dyb_ref[...].astype(F32)
        dvn_rows = []
        for c0 in range(0, tm, CHUNK):
            rows = slice(c0, c0 + CHUNK)
            svs, dvns = [], []
            dsv = dyb_v[rows, :] * ug[rows, :]
            for g in range(SG_G):
                cols = slice(g * CHUNK, (g + 1) * CHUNK)
                svs.append(_dot(wt[g], vn[rows, cols]))
                dsv_g = dsv[:, cols]
                dsv_b = dsv_g.astype(BF16)
                dvns.append(_dot(wtt[g], dsv_b))
                dw_ref[g] += jnp.where(r0 >= r1, _dot_nt(dsv_b, vn[rows, cols]), 0.0)
                db_ref[g] += jnp.broadcast_to(jnp.sum(dsv_g, axis=1, keepdims=True), (CHUNK, CHUNK))
            sv = jnp.concatenate(svs, axis=1) + bias_ref[...]
            dug = dyb_v[rows, :] * sv
            dsg_ref[rows, 0:SG_W] = (dug * _gelu_grad(u[rows, :], tu[rows, :])).astype(BF16)
            dvn_rows.append(jnp.concatenate(dvns, axis=1))
        dvn = jnp.concatenate(dvn_rows, axis=0)
        dvg, dgn = _rms_bwd(vh, r, gn_ref[...], dvn)
        dsg_ref[:, SG_W:] = (dvg * _gelu_grad(v, tv)).astype(BF16)
        dgn_ref[...] += dgn

    return _rowcall(name, body, n_tok, tm, [sgin, dyb], [gn, sgw, sgw_t, bias_full], [], [(2 * SG_W, BF16)],
                    [((1, SG_W), F32), ((SG_G, CHUNK, CHUNK), F32), ((SG_G, CHUNK, CHUNK), F32)])


def _forget_cumsum(name, fl, bf):
    n_tok, tm = fl.shape[0], min(ROW_BLOCK, fl.shape[0])

    def body(i, n, fl_ref, b_ref, c_ref, carry):
        @pl.when(i == 0)
        def _():
            carry[...] = jnp.zeros(carry.shape, F32)

        z = fl_ref[...] + b_ref[...]
        lf = jnp.minimum(z, 0.0) - jnp.log1p(jnp.exp(-jnp.abs(z)))
        tri = (lax.broadcasted_iota(jnp.int32, (tm, tm), 0) >= lax.broadcasted_iota(jnp.int32, (tm, tm), 1)).astype(BF16)
        c = _split3_dot(tri, lf) + carry[...]
        c_ref[...] = c * LOG2E
        carry[...] = c[tm - 1:tm, :]

    return _rowcall(name, body, n_tok, tm, [fl], [bf], [], [(LANES, F32)], [], scratch=[pltpu.VMEM((1, LANES), F32)])[0]


def _forget_bwd(name, dc, fl, bf):
    n_tok, tm = fl.shape[0], min(ROW_BLOCK, fl.shape[0])
    n = n_tok // tm
    rev = pl.BlockSpec((tm, LANES), lambda i: (n - 1 - i, 0))

    def kern(dc_ref, fl_ref, b_ref, dfl_ref, db_ref, carry):
        @pl.when(pl.program_id(0) == 0)
        def _():
            carry[...] = jnp.zeros(carry.shape, F32)
            db_ref[...] = jnp.zeros(db_ref.shape, F32)

        triu = (lax.broadcasted_iota(jnp.int32, (tm, tm), 0) <= lax.broadcasted_iota(jnp.int32, (tm, tm), 1)).astype(BF16)
        dlf = _split3_dot(triu, dc_ref[...]) + carry[...]
        carry[...] = dlf[0:1, :]
        z = fl_ref[...] + b_ref[...]
        dfl = dlf * jax.nn.sigmoid(-z)
        dfl_ref[...] = dfl
        db_ref[...] += jnp.sum(dfl, axis=0, keepdims=True)

    return pl.pallas_call(
        kern, name=name, grid=(n,), in_specs=[rev, rev, _full_spec((1, LANES))],
        out_specs=[rev, _full_spec((1, LANES))],
        out_shape=[jax.ShapeDtypeStruct((n_tok, LANES), F32), jax.ShapeDtypeStruct((1, LANES), F32)],
        scratch_shapes=[pltpu.VMEM((1, LANES), F32)], compiler_params=_params(1),
    )(dc, fl, bf)


def _lane_pick(c_blk, h):
    lane = lax.broadcasted_iota(jnp.int32, c_blk.shape, 1)
    return jnp.broadcast_to(jnp.sum(jnp.where(lane == h, c_blk, 0.0), axis=1, keepdims=True), c_blk.shape)


def _wide(stat, width):
    return jnp.tile(stat, (1, width // LANES))


def _pair_half(rows):
    return lax.broadcasted_iota(jnp.int32, (rows, LANES), 1) // FOX_D


def _col_to_row(col):
    pick = (lax.broadcasted_iota(jnp.int32, (8, LANES), 1) == 0).astype(BF16)
    hi = col.astype(BF16)
    r1 = col - hi.astype(F32)
    mid = r1.astype(BF16)
    lo = (r1 - mid.astype(F32)).astype(BF16)
    return (_dot_nt(pick, hi) + _dot_nt(pick, mid) + _dot_nt(pick, lo))[0:1, :]


def _tri_table(nb, lower):
    rows = [(a, e, b) for a in range(nb) for e in range(2) for b in (range(a + 1) if lower else range(a, nb))]
    return tuple(jnp.asarray([r[k] for r in rows], jnp.int32) for k in range(3))


def _flash_fwd(name, qkv, c2, ct2):
    n_tok = qkv.shape[0]
    blk = min(FLASH_BLOCK, n_tok)
    nb = n_tok // blk
    n_pair = FOX_H // 2
    it, et, jt = _tri_table(nb, True)

    def kern(it_ref, et_ref, jt_ref, q_ref, k_ref, v_ref, c_ref, ct_ref, o_ref, lse_ref, m_s, l_s, acc_s, cq_s, qm_s):
        n = pl.program_id(1)
        i, e, j = it_ref[n], et_ref[n], jt_ref[n]
        half = _pair_half(blk)

        @pl.when(j == 0)
        def _():
            m_s[...] = jnp.full(m_s.shape, -jnp.inf, F32)
            l_s[...] = jnp.zeros(l_s.shape, F32)
            acc_s[...] = jnp.zeros(acc_s.shape, F32)
            cq_s[...] = _lane_pick(c_ref[...], 2 * pl.program_id(0) + e)
            qm_s[...] = jnp.where(half == e, q_ref[...], jnp.zeros_like(q_ref[...]))

        def step(on_diagonal):
            t = _dot_nt(qm_s[...], k_ref[...]) * (FOX_SCALE * LOG2E) - ct_ref[0]
            if on_diagonal:
                keep = lax.broadcasted_iota(jnp.int32, (blk, blk), 1) <= lax.broadcasted_iota(jnp.int32, (blk, blk), 0)
                t = jnp.where(keep, t, -jnp.inf)
            m_old = m_s[...]
            m_new = jnp.maximum(m_old, jnp.max(t, axis=1, keepdims=True) + cq_s[...])
            p = jnp.exp2(t - _wide(m_new - cq_s[...], blk))
            alpha = jnp.exp2(m_old - m_new)
            l_s[...] = alpha * l_s[...] + jnp.sum(p, axis=1, keepdims=True)
            acc_s[...] = alpha * acc_s[...] + _dot(p.astype(BF16), v_ref[...])
            m_s[...] = m_new

        @pl.when(j < i)
        def _():
            step(False)

        @pl.when(j == i)
        def _():
            step(True)
            out = (acc_s[...] / l_s[...]).astype(BF16)
            lse_row = _col_to_row(m_s[...] + jnp.log(l_s[...]) * LOG2E)
            rows = pl.ds(pl.multiple_of(i * blk, blk), blk)

            @pl.when(e == 0)
            def _():
                o_ref[...] = jnp.where(half == 0, out, jnp.zeros_like(out))
                lse_ref[0, :, rows] = jnp.broadcast_to(lse_row, (2, blk))

            @pl.when(e == 1)
            def _():
                o_ref[...] = jnp.where(half == 1, out, o_ref[...])
                second = lax.broadcasted_iota(jnp.int32, (2, blk), 0) == 1
                lse_ref[0, :, rows] = jnp.where(second, lse_row, lse_ref[0, :, rows])

    def col_block(first, inner):
        def index(p, n, it_r, et_r, jt_r):
            return ((jt_r[n] if inner else it_r[n]), first + p)
        return pl.BlockSpec((blk, LANES), index)

    grid_spec = pltpu.PrefetchScalarGridSpec(
        num_scalar_prefetch=3, grid=(n_pair, int(it.shape[0])),
        in_specs=[col_block(0, False), col_block(n_pair, True), col_block(2 * n_pair, True),
                  pl.BlockSpec((blk, LANES), lambda p, n, it_r, et_r, jt_r: (it_r[n], 0)),
                  pl.BlockSpec((1, 1, blk), lambda p, n, it_r, et_r, jt_r: (2 * p + et_r[n], 0, jt_r[n]))],
        out_specs=[col_block(0, False), pl.BlockSpec((1, 2, n_tok), lambda p, n, it_r, et_r, jt_r: (p, 0, 0))],
        scratch_shapes=[pltpu.VMEM((blk, LANES), F32)] * 4 + [pltpu.VMEM((blk, LANES), BF16)],
    )
    return pl.pallas_call(
        kern, name=name, grid_spec=grid_spec,
        out_shape=[jax.ShapeDtypeStruct((n_tok, FOX_W), BF16), jax.ShapeDtypeStruct((n_pair, 2, n_tok), F32)],
        compiler_params=_params(2),
    )(it, et, jt, qkv, qkv, qkv, c2, ct2)


def _flash_delta(name, do, o):
    n_tok, tm = do.shape[0], min(ROW_BLOCK, do.shape[0])

    def body(i, n, do_ref, o_ref, d_ref):
        prod = do_ref[...].astype(F32) * o_ref[...].astype(F32)
        head = (lax.broadcasted_iota(jnp.int32, (FOX_W, LANES), 0) // FOX_D
                == lax.broadcasted_iota(jnp.int32, (FOX_W, LANES), 1)).astype(BF16)
        hi = prod.astype(BF16)
        r1 = prod - hi.astype(F32)
        mid = r1.astype(BF16)
        lo = (r1 - mid.astype(F32)).astype(BF16)
        d_ref[...] = _dot(hi, head) + _dot(mid, head) + _dot(lo, head)

    return _rowcall(name, body, n_tok, tm, [do, o], [], [], [(LANES, F32)], [])[0]


def _flash_bwd(name, qkv, do, r_row, d_row, c2):
    n_tok = qkv.shape[0]
    blk = min(FLASH_BLOCK, n_tok)
    nb = n_tok // blk
    n_pair = FOX_H // 2
    jt, et, it = _tri_table(nb, False)

    def kern(jt_ref, et_ref, it_ref, k_ref, v_ref, q_ref, do_ref, rr_ref, dr_ref, c_ref,
             dq_ref, dk_ref, dv_ref, dc_ref, km_s, vm_s, ck_s, dck_s):
        n = pl.program_id(1)
        j, e, i = jt_ref[n], et_ref[n], it_ref[n]
        half = _pair_half(blk)
        mine = half == e
        sub = lax.broadcasted_iota(jnp.int32, (2, blk), 0)

        @pl.when(n == 0)
        def _():
            dq_ref[...] = jnp.zeros(dq_ref.shape, F32)
            dc_ref[...] = jnp.zeros(dc_ref.shape, F32)

        @pl.when((e == 0) & (i == j))
        def _():
            dk_ref[...] = jnp.zeros(dk_ref.shape, F32)
            dv_ref[...] = jnp.zeros(dv_ref.shape, F32)

        @pl.when(i == j)
        def _():
            km_s[...] = jnp.where(mine, k_ref[...], jnp.zeros_like(k_ref[...]))
            vm_s[...] = jnp.where(mine, v_ref[...], jnp.zeros_like(v_ref[...]))
            ck_s[...] = _lane_pick(c_ref[...], 2 * pl.program_id(0) + e)
            dck_s[...] = jnp.zeros(dck_s.shape, F32)

        def step(on_diagonal):
            q_v, do_v = q_ref[...], do_ref[...]
            t = _dot_nt(km_s[...], q_v) * (FOX_SCALE * LOG2E) - _wide(ck_s[...], blk) - rr_ref[0]
            if on_diagonal:
                keep = lax.broadcasted_iota(jnp.int32, (blk, blk), 1) >= lax.broadcasted_iota(jnp.int32, (blk, blk), 0)
                t = jnp.where(keep, t, -jnp.inf)
            p = jnp.exp2(t)
            ds = p * (_dot_nt(vm_s[...], do_v) - dr_ref[0])
            p_b, ds_b = p.astype(BF16), ds.astype(BF16)
            dv_ref[...] += jnp.where(mine, _dot(p_b, do_v), 0.0)
            dk_ref[...] += jnp.where(mine, _dot(ds_b, q_v) * FOX_SCALE, 0.0)
            rows = pl.ds(pl.multiple_of(i * blk, blk), blk)
            dq_ref[rows, :] += jnp.where(mine, _dot_tn(ds_b, k_ref[...]) * FOX_SCALE, 0.0)
            dck_s[...] += jnp.broadcast_to(jnp.sum(ds, axis=1, keepdims=True), (blk, LANES))
            dc_ref[0, :, rows] += jnp.where(sub == e, jnp.sum(ds, axis=0, keepdims=True), 0.0)

        @pl.when(i > j)
        def _():
            step(False)

        @pl.when(i == j)
        def _():
            step(True)

        @pl.when(i == nb - 1)
        def _():
            keys = pl.ds(pl.multiple_of(j * blk, blk), blk)
            dc_ref[0, :, keys] -= jnp.where(sub == e, _col_to_row(dck_s[...]), 0.0)

    def col_block(first, inner):
        def index(p, n, jt_r, et_r, it_r):
            return ((it_r[n] if inner else jt_r[n]), first + p)
        return pl.BlockSpec((blk, LANES), index)

    def row_stat():
        return pl.BlockSpec((1, 1, blk), lambda p, n, jt_r, et_r, it_r: (2 * p + et_r[n], 0, it_r[n]))

    grid_spec = pltpu.PrefetchScalarGridSpec(
        num_scalar_prefetch=3, grid=(n_pair, int(jt.shape[0])),
        in_specs=[col_block(n_pair, False), col_block(2 * n_pair, False), col_block(0, True), col_block(0, True),
                  row_stat(), row_stat(), pl.BlockSpec((blk, LANES), lambda p, n, jt_r, et_r, it_r: (jt_r[n], 0))],
        out_specs=[pl.BlockSpec((n_tok, LANES), lambda p, n, jt_r, et_r, it_r: (0, p)),
                   col_block(0, False), col_block(0, False),
                   pl.BlockSpec((1, 2, n_tok), lambda p, n, jt_r, et_r, it_r: (p, 0, 0))],
        scratch_shapes=[pltpu.VMEM((blk, LANES), BF16), pltpu.VMEM((blk, LANES), BF16), pltpu.VMEM((blk, LANES), F32),
                        pltpu.VMEM((blk, LANES), F32)],
    )
    wide = jax.ShapeDtypeStruct((n_tok, FOX_W), F32)
    return pl.pallas_call(
        kern, name=name, grid_spec=grid_spec,
        out_shape=[wide, wide, wide, jax.ShapeDtypeStruct((n_pair, 2, n_tok), F32)],
        compiler_params=_params(2),
    )(jt, et, it, qkv, qkv, qkv, do, r_row, d_row, c2)


def _merge_fwd(name, x, ya, yb, yc, gl, wb, wo):
    n_tok, tm = x.shape[0], min(ROW_BLOCK, x.shape[0])

    def body(i, n, x_ref, ya_ref, yb_ref, yc_ref, gl_ref, wb_ref, wo_ref, xo_ref):
        merged = jnp.zeros((tm, D_MODEL), F32)
        for k, y_ref in enumerate((ya_ref, yb_ref, yc_ref)):
            gate = jax.nn.sigmoid(gl_ref[:, k * D_MODEL:(k + 1) * D_MODEL].astype(F32))
            merged = merged + gate * _dot(y_ref[...], wb_ref[k])
        xo_ref[...] = x_ref[...] + _dot(merged.astype(BF16), wo_ref[...])

    return _rowcall(name, body, n_tok, tm, [x, ya, yb, yc, gl], [], [wb, wo], [(D_MODEL, F32)], [])[0]


def _merge_bwd(name, dxo, ya, yb, yc, gl, wb, wb_t, wo_t):
    n_tok, tm = dxo.shape[0], min(ROW_BLOCK, dxo.shape[0])

    def body(i, n, dxo_ref, ya_ref, yb_ref, yc_ref, gl_ref, wb_ref, wbt_ref, wot_ref,
             mg_ref, dob_ref, dgl_ref, dbr_ref, dya_ref, dyb_ref, dyc_ref):
        dob = dxo_ref[...].astype(BF16)
        dob_ref[...] = dob
        dm = _dot(dob, wot_ref[...])
        merged = jnp.zeros((tm, D_MODEL), F32)
        for k, (y_ref, dy_ref) in enumerate(((ya_ref, dya_ref), (yb_ref, dyb_ref), (yc_ref, dyc_ref))):
            cols = slice(k * D_MODEL, (k + 1) * D_MODEL)
            gate = jax.nn.sigmoid(gl_ref[:, cols].astype(F32))
            br = _dot(y_ref[...], wb_ref[k])
            merged = merged + gate * br
            dgl_ref[:, cols] = (dm * br * gate * (1.0 - gate)).astype(BF16)
            dbr = (dm * gate).astype(BF16)
            dbr_ref[:, cols] = dbr
            dy_ref[...] = _dot(dbr, wbt_ref[k]).astype(BF16)
        mg_ref[...] = merged.astype(BF16)

    return _rowcall(name, body, n_tok, tm, [dxo, ya, yb, yc, gl], [], [wb, wb_t, wo_t],
                    [(D_MODEL, BF16), (D_MODEL, BF16), (3 * D_MODEL, BF16), (3 * D_MODEL, BF16),
                     (CONV_W, BF16), (SG_W, BF16), (FOX_W, BF16)], [])


def _mem_fwd(name, mem, gain, wk, wv):
    n_mem = mem.shape[0]

    def kern(m_ref, g_ref, wk_ref, wv_ref, mn_ref, k_ref, v_ref):
        mh, _ = _rms_stats(m_ref[...])
        mn = (mh * g_ref[...]).astype(BF16)
        mn_ref[...] = mn
        k_ref[...] = _dot(mn, wk_ref[...]).astype(BF16)
        v_ref[...] = _dot(mn, wv_ref[...]).astype(BF16)

    shp = jax.ShapeDtypeStruct((n_mem, D_MODEL), BF16)
    return pl.pallas_call(
        kern, name=name, grid=(1,),
        in_specs=[_full_spec(mem.shape), _full_spec(gain.shape), _full_spec(wk.shape), _full_spec(wv.shape)],
        out_specs=[_full_spec(shp.shape)] * 3, out_shape=[shp] * 3, compiler_params=_params(1),
    )(mem, gain, wk, wv)


def _mem_bwd(name, mem, gain, dkx, dvx, wk_t, wv_t):
    n_mem = mem.shape[0]

    def kern(m_ref, g_ref, dk_ref, dv_ref, wkt_ref, wvt_ref, dkb_ref, dvb_ref, dg_ref):
        mh, _ = _rms_stats(m_ref[...])
        dkb, dvb = dk_ref[...].astype(BF16), dv_ref[...].astype(BF16)
        dkb_ref[...] = dkb
        dvb_ref[...] = dvb
        dm = _dot(dkb, wkt_ref[...]) + _dot(dvb, wvt_ref[...])
        dg_ref[...] = jnp.sum(dm * mh, axis=0, keepdims=True)

    shp = jax.ShapeDtypeStruct((n_mem, D_MODEL), BF16)
    args = (mem, gain, dkx, dvx, wk_t, wv_t)
    return pl.pallas_call(
        kern, name=name, grid=(1,), in_specs=[_full_spec(a.shape) for a in args],
        out_specs=[_full_spec(shp.shape)] * 2 + [_full_spec((1, D_MODEL))],
        out_shape=[shp, shp, jax.ShapeDtypeStruct((1, D_MODEL), F32)], compiler_params=_params(1),
    )(*args)


def _xa_probs(q_b, kx_ref, hd):
    cols = slice(hd * XA_D, (hd + 1) * XA_D)
    s = _dot_nt(q_b[:, cols], kx_ref[:, cols]) * (XA_D ** -0.5)
    e = jnp.exp(s - jnp.max(s, axis=1, keepdims=True))
    return e / jnp.sum(e, axis=1, keepdims=True)


def _xa_fwd(name, x, gain, kx, vx, wq, wo):
    n_tok, tm = x.shape[0], min(ROW_BLOCK, x.shape[0])

    def body(i, n, x_ref, g_ref, kx_ref, vx_ref, wq_ref, wo_ref, xo_ref):
        x_v = x_ref[...]
        xh, _ = _rms_stats(x_v)
        q_b = _dot((xh * g_ref[...]).astype(BF16), wq_ref[...]).astype(BF16)
        o = jnp.concatenate(
            [_dot(_xa_probs(q_b, kx_ref, hd).astype(BF16), vx_ref[:, hd * XA_D:(hd + 1) * XA_D]) for hd in range(XA_H)],
            axis=1)
        xo_ref[...] = x_v + _dot(o.astype(BF16), wo_ref[...])

    return _rowcall(name, body, n_tok, tm, [x], [gain, kx, vx], [wq, wo], [(D_MODEL, F32)], [])[0]


def _xa_bwd(name, x, dxo, gain, kx, vx, wq, wq_t, wo_t):
    n_tok, tm = x.shape[0], min(ROW_BLOCK, x.shape[0])
    n_mem = kx.shape[0]

    def body(i, n, x_ref, dxo_ref, g_ref, kx_ref, vx_ref, wq_ref, wqt_ref, wot_ref,
             dx_ref, h_ref, o_ref, dq_ref, dy_ref, dkx_ref, dvx_ref, dg_ref):
        g = g_ref[...]
        xh, r = _rms_stats(x_ref[...])
        h = (xh * g).astype(BF16)
        h_ref[...] = h
        q_b = _dot(h, wq_ref[...]).astype(BF16)
        dxo_v = dxo_ref[...]
        dy = dxo_v.astype(BF16)
        dy_ref[...] = dy
        do = _dot(dy, wot_ref[...])
        for hd in range(XA_H):
            cols = slice(hd * XA_D, (hd + 1) * XA_D)
            p = _xa_probs(q_b, kx_ref, hd)
            p_b = p.astype(BF16)
            o_ref[:, cols] = _dot(p_b, vx_ref[:, cols]).astype(BF16)
            do_h = do[:, cols].astype(BF16)
            dvx_ref[:, cols] += _dot_tn(p_b, do_h)
            dp = _dot_nt(do_h, vx_ref[:, cols])
            ds = p * (dp - jnp.sum(dp * p, axis=1, keepdims=True))
            ds_b = (ds * (XA_D ** -0.5)).astype(BF16)
            dq_ref[:, cols] = _dot(ds_b, kx_ref[:, cols]).astype(BF16)
            dkx_ref[:, cols] += _dot_tn(ds_b, q_b[:, cols])
        dh = _dot(dq_ref[...], wqt_ref[...])
        dx, dg = _rms_bwd(xh, r, g, dh)
        dx_ref[...] = dxo_v + dx
        dg_ref[...] += dg

    return _rowcall(name, body, n_tok, tm, [x, dxo], [gain, kx, vx], [wq, wq_t, wo_t],
                    [(D_MODEL, F32), (D_MODEL, BF16), (D_MODEL, BF16), (D_MODEL, BF16), (D_MODEL, BF16)],
                    [((n_mem, D_MODEL), F32), ((n_mem, D_MODEL), F32), ((1, D_MODEL), F32)])


def _loss_head(name, x, target, gain):
    n_tok, tm = x.shape[0], min(ROW_BLOCK, x.shape[0])

    def body(i, n, x_ref, t_ref, g_ref, dx_ref, loss_ref, dg_ref):
        g = g_ref[...]
        xh, r = _rms_stats(x_ref[...])
        err = xh * g - t_ref[...]
        loss_ref[...] += 0.5 * jnp.sum(jnp.sum(err * err, axis=1, keepdims=True) / D_MODEL, axis=0, keepdims=True)
        dx, dg = _rms_bwd(xh, r, g, err / D_MODEL)
        dx_ref[...] = dx
        dg_ref[...] += dg

    return _rowcall(name, body, n_tok, tm, [x, target], [gain], [], [(D_MODEL, F32)],
                    [((8, LANES), F32), ((1, D_MODEL), F32)])


def _adamw(name, w, g, m, v):
    rows, cols = w.shape
    tr = 128 if rows % 128 == 0 else rows

    def kern(w_ref, g_ref, m_ref, v_ref, d_ref, mo_ref, vo_ref):
        g_v = g_ref[...]
        m_new = ADAM_B1 * m_ref[...] + (1.0 - ADAM_B1) * g_v
        v_new = ADAM_B2 * v_ref[...] + (1.0 - ADAM_B2) * (g_v * g_v)
        m_hat = m_new / (1.0 - ADAM_B1 ** ADAM_STEP)
        v_hat = v_new / (1.0 - ADAM_B2 ** ADAM_STEP)
        d_ref[...] = -ADAM_LR * (m_hat / (jnp.sqrt(v_hat) + ADAM_EPS) + ADAM_WD * w_ref[...])
        mo_ref[...] = m_new
        vo_ref[...] = v_new

    spec = pl.BlockSpec((tr, cols), lambda i: (i, 0))
    shp = jax.ShapeDtypeStruct((rows, cols), F32)
    return pl.pallas_call(kern, name=name, grid=(rows // tr,), in_specs=[spec] * 4, out_specs=[spec] * 3,
                          out_shape=[shp] * 3, compiler_params=_params(1))(w, g, m, v)


def _add_pair(name, gp, got, core):
    _, n, rows, cols = gp.shape
    spec = pl.BlockSpec((1, PACK_ROW_BLOCK, cols), lambda j, r, core_ref: (j, r, 0))
    mine = pl.BlockSpec((1, PACK_ROW_BLOCK, cols), lambda j, r, core_ref: (core_ref[0] * n + j, r, 0))

    def kern(core_ref, a_ref, b_ref, o_ref):
        o_ref[...] = (a_ref[...] + b_ref[...]).astype(BF16)

    grid_spec = pltpu.PrefetchScalarGridSpec(num_scalar_prefetch=1, grid=(n, rows // PACK_ROW_BLOCK),
                                             in_specs=[mine, spec], out_specs=spec)
    return pl.pallas_call(kern, name=name, grid_spec=grid_spec, out_shape=jax.ShapeDtypeStruct(got.shape, BF16),
                          compiler_params=_params(2))(core, gp.reshape(2 * n, rows, cols), got)


def _sum_parts(name, s, got, chip):
    _, rows, cols = s.shape
    tr = PACK_ROW_BLOCK

    def kern(chip_ref, s_ref, g_ref, o_ref):
        acc = s_ref[0].astype(F32)
        for k in range(3):
            acc = acc + g_ref[k].astype(F32)
        o_ref[...] = acc

    grid_spec = pltpu.PrefetchScalarGridSpec(
        num_scalar_prefetch=1, grid=(rows // tr,),
        in_specs=[pl.BlockSpec((1, tr, cols), lambda r, chip_ref: (chip_ref[0], r, 0)),
                  pl.BlockSpec((3, tr, cols), lambda r, chip_ref: (0, r, 0))],
        out_specs=pl.BlockSpec((tr, cols), lambda r, chip_ref: (r, 0)))
    return pl.pallas_call(kern, name=name, grid_spec=grid_spec, out_shape=jax.ShapeDtypeStruct((rows, cols), F32),
                          compiler_params=_params(1))(chip, s, got)


def _sum_slots(name, a, tr):
    n, rows, cols = a.shape

    def kern(a_ref, o_ref):
        acc = a_ref[0].astype(F32)
        for k in range(1, n):
            acc = acc + a_ref[k].astype(F32)
        o_ref[...] = acc

    return pl.pallas_call(kern, name=name, grid=(rows // tr,),
                          in_specs=[pl.BlockSpec((n, tr, cols), lambda r: (0, r, 0))],
                          out_specs=pl.BlockSpec((tr, cols), lambda r: (r, 0)),
                          out_shape=jax.ShapeDtypeStruct((rows, cols), F32), compiler_params=_params(1))(a)


_ANY = pl.BlockSpec(memory_space=pl.ANY)


COMM_CHUNKS = 8


def _place():
    x, y, c = lax.axis_index("x"), lax.axis_index("y"), lax.axis_index("c")
    chips = [(1 - x, y), (x, 1 - y), (1 - x, 1 - y)]
    return x, y, c, chips


def _rows(ref, q, n):
    step = ref.shape[0] // n
    return ref.at[pl.ds(q * step, step)]


def _gather_chips(name, w):
    n = COMM_CHUNKS

    def kern(w_ref, o_ref, send_sems, recv_sems):
        x, y, c, chips = _place()
        me, sib = 2 * x + y, (x, y, 1 - c)

        def copy(k, src, dst, to):
            return pltpu.make_async_remote_copy(src_ref=src, dst_ref=dst, send_sem=send_sems.at[k],
                                                recv_sem=recv_sems.at[k], device_id=to, device_id_type=MESH_ID)

        first = [copy(k * n + q, _rows(w_ref.at[c], q, n), _rows(o_ref.at[me, c], q, n), (cx, cy, c))
                 for k, (cx, cy) in enumerate(chips) for q in range(n)]
        for cp in first:
            cp.start()
        passed = []
        for k, (cx, cy) in enumerate(chips):
            for q in range(n):
                landed = _rows(o_ref.at[2 * cx + cy, c], q, n)
                copy(k * n + q, landed, landed, (x, y, c)).wait_recv()
                passed.append(copy((3 + k) * n + q, landed, landed, sib))
                passed[-1].start()
        for k, (cx, cy) in enumerate(chips):
            for q in range(n):
                theirs = _rows(o_ref.at[2 * cx + cy, 1 - c], q, n)
                copy((3 + k) * n + q, theirs, theirs, (x, y, c)).wait_recv()
        for cp in first + passed:
            cp.wait_send()

    return pl.pallas_call(
        kern, name=name, in_specs=[_ANY], out_specs=_ANY,
        out_shape=jax.ShapeDtypeStruct((N_CHIPS,) + w.shape, w.dtype),
        scratch_shapes=[pltpu.SemaphoreType.DMA((6 * n,)), pltpu.SemaphoreType.DMA((6 * n,))],
        compiler_params=pltpu.CompilerParams(has_side_effects=True),
    )(w)


def _gather_all(name, v):
    def kern(v_ref, o_ref, send_sems, recv_sems, local_sem):
        x, y, c, _ = _place()
        me = 4 * x + 2 * y + c
        mine = pltpu.make_async_copy(v_ref, o_ref.at[me], local_sem)
        mine.start()
        peers = []
        for k in range(1, 8):
            px = 1 - x if k & 4 else x
            py = 1 - y if k & 2 else y
            pc = 1 - c if k & 1 else c
            peers.append((px, py, pc))
        copies = [pltpu.make_async_remote_copy(src_ref=v_ref, dst_ref=o_ref.at[me], send_sem=send_sems.at[k],
                                               recv_sem=recv_sems.at[k], device_id=p, device_id_type=MESH_ID)
                  for k, p in enumerate(peers)]
        for cp in copies:
            cp.start()
        for k, (px, py, pc) in enumerate(peers):
            pltpu.make_async_remote_copy(src_ref=v_ref, dst_ref=o_ref.at[4 * px + 2 * py + pc], send_sem=send_sems.at[k],
                                         recv_sem=recv_sems.at[k], device_id=(x, y, c), device_id_type=MESH_ID).wait_recv()
        for cp in copies:
            cp.wait_send()
        mine.wait()

    return pl.pallas_call(
        kern, name=name, in_specs=[_ANY], out_specs=_ANY, out_shape=jax.ShapeDtypeStruct((8,) + v.shape, v.dtype),
        scratch_shapes=[pltpu.SemaphoreType.DMA((7,)), pltpu.SemaphoreType.DMA((7,)), pltpu.SemaphoreType.DMA(())],
        compiler_params=pltpu.CompilerParams(has_side_effects=True),
    )(v)


def _swap_halves(name, gp):
    n = COMM_CHUNKS
    n_slots = gp.shape[1]

    def kern(g_ref, got_ref, send_sems, recv_sems):
        x, y, c, _ = _place()
        swaps = [pltpu.make_async_remote_copy(src_ref=_rows(g_ref.at[1 - c, j], q, n), dst_ref=_rows(got_ref.at[j], q, n),
                                              send_sem=send_sems.at[j * n + q], recv_sem=recv_sems.at[j * n + q],
                                              device_id=(x, y, 1 - c), device_id_type=MESH_ID)
                 for j in range(n_slots) for q in range(n)]
        for cp in swaps:
            cp.start()
        for cp in swaps:
            cp.wait()

    return pl.pallas_call(
        kern, name=name, in_specs=[_ANY], out_specs=_ANY, out_shape=jax.ShapeDtypeStruct(gp.shape[1:], gp.dtype),
        scratch_shapes=[pltpu.SemaphoreType.DMA((n_slots * n,)), pltpu.SemaphoreType.DMA((n_slots * n,))],
        compiler_params=pltpu.CompilerParams(has_side_effects=True),
    )(gp)


def _scatter_chips(name, s):
    n = COMM_CHUNKS

    def kern(s_ref, o_ref, send_sems, recv_sems):
        x, y, c, chips = _place()
        copies = [pltpu.make_async_remote_copy(src_ref=_rows(s_ref.at[2 * cx + cy], q, n), dst_ref=_rows(o_ref.at[k], q, n),
                                               send_sem=send_sems.at[k * n + q], recv_sem=recv_sems.at[k * n + q],
                                               device_id=(cx, cy, c), device_id_type=MESH_ID)
                  for k, (cx, cy) in enumerate(chips) for q in range(n)]
        for cp in copies:
            cp.start()
        for k in range(3):
            for q in range(n):
                landed = _rows(o_ref.at[k], q, n)
                pltpu.make_async_remote_copy(src_ref=landed, dst_ref=landed, send_sem=send_sems.at[k * n + q],
                                             recv_sem=recv_sems.at[k * n + q], device_id=(x, y, c),
                                             device_id_type=MESH_ID).wait_recv()
        for cp in copies:
            cp.wait_send()

    return pl.pallas_call(
        kern, name=name, in_specs=[_ANY], out_specs=_ANY, out_shape=jax.ShapeDtypeStruct((3,) + s.shape[1:], s.dtype),
        scratch_shapes=[pltpu.SemaphoreType.DMA((3 * n,)), pltpu.SemaphoreType.DMA((3 * n,))],
        compiler_params=pltpu.CompilerParams(has_side_effects=True),
    )(s)


def _join_halves(name, r):
    n = COMM_CHUNKS

    def kern(r_ref, o_ref, send_sems, recv_sems):
        x, y, c, _ = _place()
        swaps = [pltpu.make_async_remote_copy(src_ref=_rows(r_ref, q, n), dst_ref=_rows(o_ref, q, n),
                                              send_sem=send_sems.at[q], recv_sem=recv_sems.at[q],
                                              device_id=(x, y, 1 - c), device_id_type=MESH_ID) for q in range(n)]
        for cp in swaps:
            cp.start()
        for cp in swaps:
            cp.wait()

    return pl.pallas_call(
        kern, name=name, in_specs=[_ANY], out_specs=_ANY, out_shape=jax.ShapeDtypeStruct(r.shape, r.dtype),
        scratch_shapes=[pltpu.SemaphoreType.DMA((n,))] * 2, compiler_params=pltpu.CompilerParams(has_side_effects=True),
    )(r)


def _pack_rows(n_elems):
    rows = -(-n_elems // (2 * PACK_COLS * PACK_ROW_BLOCK)) * PACK_ROW_BLOCK
    return rows


def _pack(pieces, dtype):
    flat = jnp.concatenate([p.astype(dtype).reshape(-1) for p in pieces])
    rows = _pack_rows(flat.shape[0])
    flat = jnp.pad(flat, (0, 2 * rows * PACK_COLS - flat.shape[0]))
    return flat.reshape(2, rows, PACK_COLS)


def _unpack(packed, shapes):
    lead = packed.shape[:-3]
    flat = packed.reshape(lead + (-1,))
    out, off = [], 0
    for shp in shapes:
        size = 1
        for s in shp:
            size *= s
        out.append(flat[..., off:off + size].reshape(lead + tuple(shp)))
        off += size
    return out


def _shard(a, axis, j):
    size = a.shape[axis] // N_CHIPS
    return lax.slice_in_dim(a, j * size, (j + 1) * size, axis=axis)


def _t(a):
    return jnp.swapaxes(a, -1, -2)


def _rows_from_lanes(a):
    return a[:, :FOX_H].T[:, None, :]


def _lanes_from_heads(a):
    return jnp.pad(a[:, 0, :].T, ((0, 0), (0, LANES - FOX_H)))


def _layer_weights(wts, small, l):
    w_in = wts["w_in"][l]
    w = {
        "wf": jnp.pad(w_in[:, W1_COLS:W1_COLS + FOX_H], ((0, 0), (0, LANES - FOX_H))),
        "w1": w_in[:, :W1_COLS], "wgl": w_in[:, W1_COLS + FOX_H:],
    }
    for nm in ("ffn1_w_gate", "ffn1_w_up", "ffn1_w_down", "w_branch", "w_out", "xa_wq", "xa_wk", "xa_wv", "xa_wo",
               "ffn2_w_gate", "ffn2_w_up", "ffn2_w_down"):
        w[nm] = wts[nm][l]
    for nm in list(w):
        w[nm + "_t"] = _t(w[nm])
    for nm in ("ffn1_norm", "mix_norm", "sg_norm", "xa_norm", "mem_norm", "ffn2_norm"):
        w[nm] = small[nm][l][None, :]
    w["conv_w"] = jnp.pad(small["conv_w"][l], ((0, 5), (0, 0)))
    w["sg_w"] = small["sg_w"][l]
    w["sg_w_t"] = _t(small["sg_w"][l])
    w["sg_bias"] = jnp.repeat(small["sg_b"][l].T, CHUNK, axis=1)
    w["fox_b"] = jnp.pad(small["fox_b_f"][l][None, :], ((0, 0), (0, LANES - FOX_H)))
    return w


def _layer_fwd(l, x, mem, w):
    s = {"x0": x}
    x1, s["a1"], s["b1"] = _ffn_fwd(f"ffn1_fwd_{l}", x, w["ffn1_norm"], w["ffn1_w_gate"], w["ffn1_w_up"], w["ffn1_w_down"])
    s["x1"] = x1
    cin, sgin, qkv, fl, gl = _proj_fwd(f"proj_fwd_{l}", x1, w["mix_norm"], w["w1"], w["wf"], w["wgl"])
    ya = _conv_fwd(f"conv_fwd_{l}", cin, w["conv_w"])
    yb = _sg_fwd(f"sg_fwd_{l}", sgin, w["sg_norm"], w["sg_w"], w["sg_bias"])
    c2 = _forget_cumsum(f"forget_fwd_{l}", fl, w["fox_b"])
    yc, lse2 = _flash_fwd(f"flash_fwd_{l}", qkv, c2, _rows_from_lanes(c2))
    x2 = _merge_fwd(f"merge_fwd_{l}", x1, ya, yb, yc, gl, w["w_branch"], w["w_out"])
    s.update(cin=cin, sgin=sgin, qkv=qkv, fl=fl, gl=gl, ya=ya, yb=yb, yc=yc, c2=c2, lse2=lse2, x2=x2)
    s["mn"], s["kx"], s["vx"] = _mem_fwd(f"mem_fwd_{l}", mem, w["mem_norm"], w["xa_wk"], w["xa_wv"])
    x3 = _xa_fwd(f"xa_fwd_{l}", x2, w["xa_norm"], s["kx"], s["vx"], w["xa_wq"], w["xa_wo"])
    s["x3"] = x3
    x4, s["a2"], s["b2"] = _ffn_fwd(f"ffn2_fwd_{l}", x3, w["ffn2_norm"], w["ffn2_w_gate"], w["ffn2_w_up"], w["ffn2_w_down"])
    return x4, s


def _ffn_grads(tag, x, dxo, a, b, w, pre):
    dx, da, db, sv, h, dy, dg = _ffn_bwd(f"{pre}_bwd_{tag}", x, dxo, a, b, w[pre + "_norm"], w[pre + "_w_down_t"],
                                         w[pre + "_w_gate_t"], w[pre + "_w_up_t"])
    g = {
        pre + "_norm": dg[0],
        pre + "_w_gate": _mm_tn(f"{pre}_dwg_{tag}", h, da, D_MODEL, D_FF, tk=D_MODEL, tn=FF_SPLIT),
        pre + "_w_up": _mm_tn(f"{pre}_dwu_{tag}", h, db, D_MODEL, D_FF, tk=D_MODEL, tn=FF_SPLIT),
        pre + "_w_down": _mm_tn(f"{pre}_dwd_{tag}", sv, dy, D_FF, D_MODEL, tk=FF_SPLIT, tn=D_MODEL),
    }
    return dx, g


def _layer_bwd(l, dx, mem, w, s):
    g = {}
    dx, gf = _ffn_grads(l, s["x3"], dx, s["a2"], s["b2"], w, "ffn2")
    g.update(gf)

    dx, h, o, dq, dy, dkx, dvx, dg = _xa_bwd(f"xa_bwd_{l}", s["x2"], dx, w["xa_norm"], s["kx"], s["vx"], w["xa_wq"],
                                              w["xa_wq_t"], w["xa_wo_t"])
    g["xa_norm"] = dg[0]
    g["xa_wq"] = _mm_tn(f"xa_dwq_{l}", h, dq, D_MODEL, D_MODEL, tk=D_MODEL, tn=D_MODEL)
    g["xa_wo"] = _mm_tn(f"xa_dwo_{l}", o, dy, D_MODEL, D_MODEL, tk=D_MODEL, tn=D_MODEL)
    dkb, dvb, dgm = _mem_bwd(f"mem_bwd_{l}", mem, w["mem_norm"], dkx, dvx, w["xa_wk_t"], w["xa_wv_t"])
    g["mem_norm"] = dgm[0]
    g["xa_wk"] = _mm_tn(f"xa_dwk_{l}", s["mn"], dkb, D_MODEL, D_MODEL, tk=D_MODEL, tn=D_MODEL)
    g["xa_wv"] = _mm_tn(f"xa_dwv_{l}", s["mn"], dvb, D_MODEL, D_MODEL, tk=D_MODEL, tn=D_MODEL)

    mg, dob, dgl, dbr, dya, dyb, dyc = _merge_bwd(f"merge_bwd_{l}", dx, s["ya"], s["yb"], s["yc"], s["gl"], w["w_branch"],
                                                  w["w_branch_t"], w["w_out_t"])
    g["w_out"] = _mm_tn(f"dwout_{l}", mg, dob, D_MODEL, D_MODEL, tk=D_MODEL, tn=D_MODEL)
    g["w_branch"] = jnp.stack([
        _mm_tn(f"dwbranch{k}_{l}", y, dbr, CONV_W, D_MODEL, tk=CONV_W, tn=D_MODEL, y_off=k * D_MODEL)
        for k, y in enumerate((s["ya"], s["yb"], s["yc"]))])

    dcin, dcw = _conv_bwd(f"conv_bwd_{l}", s["cin"], dya, w["conv_w"])
    g["conv_w"] = dcw[:3]
    dsg, dgn, dsw, dsb = _sg_bwd(f"sg_bwd_{l}", s["sgin"], dyb, w["sg_norm"], w["sg_w"], w["sg_w_t"], w["sg_bias"])
    g["sg_norm"], g["sg_w"], g["sg_b"] = dgn[0], dsw, dsb[:, :, 0]
    d_row = _rows_from_lanes(_flash_delta(f"flash_delta_{l}", dyc, s["yc"]))
    r_row = s["lse2"].reshape(FOX_H, 1, -1) - _rows_from_lanes(s["c2"])
    dq, dk, dv, dc_rows = _flash_bwd(f"flash_bwd_{l}", s["qkv"], dyc, r_row, d_row, s["c2"])
    dc = _lanes_from_heads(dc_rows.reshape(FOX_H, 1, -1))
    dfl, dbf = _forget_bwd(f"forget_bwd_{l}", dc, s["fl"], w["fox_b"])
    g["fox_b_f"] = dbf[0, :FOX_H]

    dx, h, dqkv, dfb, dg = _proj_bwd(f"proj_bwd_{l}", s["x1"], dx, dcin, dsg, dq, dk, dv, dfl, dgl, w["mix_norm"],
                                     w["w1_t"], w["wf_t"], w["wgl_t"])
    g["mix_norm"] = dg[0]
    c0, c1 = 3 * CONV_W, 3 * CONV_W + 2 * SG_W
    g["w_in"] = jnp.concatenate([
        _mm_tn(f"dwin_conv_{l}", h, dcin, D_MODEL, c0, tk=D_MODEL, tn=c0),
        _mm_tn(f"dwin_sg_{l}", h, dsg, D_MODEL, c1 - c0, tk=D_MODEL, tn=c1 - c0),
        _mm_tn(f"dwin_qkv_{l}", h, dqkv, D_MODEL, 3 * FOX_W, tk=D_MODEL, tn=3 * FOX_W),
        _mm_tn(f"dwin_f_{l}", h, dfb, D_MODEL, LANES, tk=D_MODEL, tn=LANES)[:, :FOX_H],
        _mm_tn(f"dwin_gl_{l}", h, dgl, D_MODEL, 3 * D_MODEL, tk=D_MODEL, tn=3 * D_MODEL // 2),
    ], axis=1)

    dx, gf = _ffn_grads(l, s["x0"], dx, s["a1"], s["b1"], w, "ffn1")
    g.update(gf)
    return dx, g


def _local_step(x, mem, target, wts, small):
    saved, lw = [], []
    for l in range(DEPTH):
        lw.append(_layer_weights(wts, small, l))
        x, s = _layer_fwd(l, x, mem, lw[l])
        saved.append(s)
    fin = small["final_norm"][None, :]
    dx, loss, dgf = _loss_head("loss_head", x, target, fin)
    layer_grads = [None] * DEPTH
    for l in reversed(range(DEPTH)):
        dx, layer_grads[l] = _layer_bwd(l, dx, mem, lw[l], saved[l])
    grads = {nm: jnp.stack([layer_grads[l][nm] for l in range(DEPTH)]) for nm in WEIGHTS if nm != "final_norm"}
    grads["final_norm"] = dgf[0]
    return loss[0, 0], dx, grads


def _small_slab(vals):
    rows = []
    for v in vals:
        flat = v.astype(F32).reshape(-1)
        n = -(-flat.shape[0] // LANES) * LANES
        rows.append(jnp.pad(flat, (0, n - flat.shape[0])).reshape(-1, LANES))
    slab = jnp.concatenate(rows)
    pad = -slab.shape[0] % 8
    return jnp.pad(slab, ((0, pad), (0, 0)))


def _small_unslab(slab, shapes):
    out, off = [], 0
    for shp in shapes:
        size = 1
        for s in shp:
            size *= s
        n_rows = -(-size // LANES)
        out.append(slab[off:off + n_rows].reshape(-1)[:size].reshape(shp))
        off += n_rows
    return out


def kernel(x, mem, ffn1_norm, ffn1_w_gate, ffn1_w_up, ffn1_w_down, mix_norm, w_in, conv_w, sg_norm, sg_w, sg_b, fox_b_f, w_branch, w_out, xa_norm, mem_norm, xa_wq, xa_wk, xa_wv, xa_wo, ffn2_norm, ffn2_w_gate, ffn2_w_up, ffn2_w_down, final_norm, loss_target, m_ffn1_norm, m_ffn1_w_gate, m_ffn1_w_up, m_ffn1_w_down, m_mix_norm, m_w_in, m_conv_w, m_sg_norm, m_sg_w, m_sg_b, m_fox_b_f, m_w_branch, m_w_out, m_xa_norm, m_mem_norm, m_xa_wq, m_xa_wk, m_xa_wv, m_xa_wo, m_ffn2_norm, m_ffn2_w_gate, m_ffn2_w_up, m_ffn2_w_down, m_final_norm, v_ffn1_norm, v_ffn1_w_gate, v_ffn1_w_up, v_ffn1_w_down, v_mix_norm, v_w_in, v_conv_w, v_sg_norm, v_sg_w, v_sg_b, v_fox_b_f, v_w_branch, v_w_out, v_xa_norm, v_mem_norm, v_xa_wq, v_xa_wk, v_xa_wv, v_xa_wo, v_ffn2_norm, v_ffn2_w_gate, v_ffn2_w_up, v_ffn2_w_down, v_final_norm):
    args = dict(locals())
    wv = {nm: args[nm] for nm in WEIGHTS}
    mv = {nm: args["m_" + nm] for nm in WEIGHTS}
    vv = {nm: args["v_" + nm] for nm in WEIGHTS}
    chip = 2 * lax.axis_index("x") + lax.axis_index("y")
    core = lax.axis_index("c")

    shard_shapes = [wv[nm].shape for nm, _ in BIG]
    gathered = _gather_chips("gather_weights", _pack([wv[nm] for nm, _ in BIG], BF16))
    pieces = _unpack(gathered, shard_shapes)
    wts = {nm: jnp.concatenate([jnp.where(chip == j, wv[nm].astype(BF16), p[j]) for j in range(N_CHIPS)], axis=ax)
           for (nm, ax), p in zip(BIG, pieces)}
    taps = _gather_all("gather_taps", _small_slab([conv_w]))
    taps = [_small_unslab(taps[2 * j], [conv_w.shape])[0] for j in range(N_CHIPS)]
    small = {nm: wv[nm] for nm in SMALL}
    small["conv_w"] = jnp.concatenate(taps, axis=2)

    loss, dx, grads = _local_step(x[0], mem[0], loss_target[0], wts, small)
    loss = lax.psum(loss, ("x", "y", "c"))

    small_shapes = [grads[nm].shape for nm in SMALL]
    slots = _gather_all("gather_small_grads", _small_slab([grads[nm] for nm in SMALL]))
    small_sum = _sum_slots("sum_small_grads", slots, slots.shape[1])
    small_g = dict(zip(SMALL, _small_unslab(small_sum, small_shapes)))
    taps_g = small_g["conv_w"]
    small_g["conv_w"] = lax.dynamic_slice_in_dim(taps_g, chip * conv_w.shape[2], conv_w.shape[2], axis=2)

    gp = jnp.stack([_pack([_shard(grads[nm], ax, j) for nm, ax in BIG], F32) for j in range(N_CHIPS)], axis=1)
    got = _swap_halves("reduce_swap_cores", gp)
    chip_sum = _add_pair("reduce_add_cores", gp, got, core.astype(jnp.int32).reshape(1))
    parts = _scatter_chips("reduce_scatter_chips", chip_sum)
    half = _sum_parts("reduce_add_chips", chip_sum, parts, chip.astype(jnp.int32).reshape(1))
    other = _join_halves("reduce_join_cores", half)
    full = jnp.where(core == 0, jnp.stack([half, other]), jnp.stack([other, half]))
    big_g = dict(zip([nm for nm, _ in BIG], _unpack(full, shard_shapes)))

    g_out = {**small_g, **big_g}
    delta, new_m, new_v = {}, {}, {}
    for nm, _ in BIG:
        shp = wv[nm].shape
        two_d = (-1, shp[-1])
        d, m2, v2 = _adamw("adamw_" + nm, wv[nm].reshape(two_d), g_out[nm].reshape(two_d), mv[nm].reshape(two_d),
                           vv[nm].reshape(two_d))
        delta[nm], new_m[nm], new_v[nm] = d.reshape(shp), m2.reshape(shp), v2.reshape(shp)
    slab_shapes = [wv[nm].shape for nm in SMALL]
    d, m2, v2 = _adamw("adamw_small", _small_slab([wv[nm] for nm in SMALL]), _small_slab([g_out[nm] for nm in SMALL]),
                       _small_slab([mv[nm] for nm in SMALL]), _small_slab([vv[nm] for nm in SMALL]))
    for out, slab in ((delta, d), (new_m, m2), (new_v, v2)):
        out.update(zip(SMALL, _small_unslab(slab, slab_shapes)))

    return (loss, dx[None], *[g_out[nm] for nm in WEIGHTS], *[delta[nm] for nm in WEIGHTS],
            *[new_m[nm] for nm in WEIGHTS], *[new_v[nm] for nm in WEIGHTS])
```

```python
import functools

import jax
import jax.numpy as jnp
from jax import lax
from jax.experimental import pallas as pl
from jax.experimental.pallas import tpu as pltpu

F32, BF16 = jnp.float32, jnp.bfloat16
MESH_ID = pl.DeviceIdType.MESH

D_MODEL = 1024
DEPTH = 2
D_FF = 2816
CONV_W = 512
SG_W = 512
SG_G = 4
CHUNK = 128
FOX_H = 8
FOX_D = 64
FOX_W = FOX_H * FOX_D
FOX_SCALE = FOX_D ** -0.5
LOG2E = 1.4426950408889634
XA_H = 4
XA_D = D_MODEL // XA_H
N_CHIPS = 4
RMS_EPS = 1e-6
W1_COLS = 3 * CONV_W + 2 * SG_W + 3 * FOX_W
LANES = 128
HALO = 16

ADAM_LR, ADAM_B1, ADAM_B2, ADAM_EPS, ADAM_WD, ADAM_STEP = 0.001, 0.9, 0.999, 1e-08, 0.01, 10

ROW_BLOCK = 256
FLASH_BLOCK = 512
FLASH_STRIP = 32
PRUNE_LOG2 = 40.0
FF_SPLIT = 1408
VMEM_LIMIT = 56 * 1024 * 1024
PACK_COLS = 1024
PACK_ROW_BLOCK = 256

BIG = (
    ("ffn1_w_gate", 2), ("ffn1_w_up", 2), ("ffn1_w_down", 1), ("w_in", 2), ("w_branch", 3), ("w_out", 1),
    ("xa_wq", 1), ("xa_wk", 1), ("xa_wv", 1), ("xa_wo", 1), ("ffn2_w_gate", 2), ("ffn2_w_up", 2), ("ffn2_w_down", 1),
)
SMALL = ("ffn1_norm", "mix_norm", "conv_w", "sg_norm", "sg_w", "sg_b", "fox_b_f", "xa_norm", "mem_norm", "ffn2_norm",
         "final_norm")
WEIGHTS = ("ffn1_norm", "ffn1_w_gate", "ffn1_w_up", "ffn1_w_down", "mix_norm", "w_in", "conv_w", "sg_norm", "sg_w",
           "sg_b", "fox_b_f", "w_branch", "w_out", "xa_norm", "mem_norm", "xa_wq", "xa_wk", "xa_wv", "xa_wo",
           "ffn2_norm", "ffn2_w_gate", "ffn2_w_up", "ffn2_w_down", "final_norm")


def _dot(a, b):
    return jnp.dot(a, b, preferred_element_type=F32)


def _dot_nt(a, b):
    return lax.dot_general(a, b, (((1,), (1,)), ((), ())), preferred_element_type=F32)


def _dot_tn(a, b):
    return lax.dot_general(a, b, (((0,), (0,)), ((), ())), preferred_element_type=F32)


def _rms_stats(x):
    r = lax.rsqrt(jnp.mean(x * x, axis=-1, keepdims=True) + RMS_EPS)
    return x * r, r


def _rms_bwd(xh, r, g, dy):
    dg = jnp.sum(dy * xh, axis=0, keepdims=True)
    dxh = dy * g
    dx = r * (dxh - xh * jnp.mean(dxh * xh, axis=-1, keepdims=True))
    return dx, dg


def _gelu(x):
    k = 0.7978845608028654
    t = jnp.tanh(k * (x + 0.044715 * x * x * x))
    return 0.5 * x * (1.0 + t), t


def _gelu_grad(x, t):
    k = 0.7978845608028654
    return 0.5 * (1.0 + t) + 0.5 * x * (1.0 - t * t) * k * (1.0 + 3.0 * 0.044715 * x * x)


def _split3_dot(tri, x):
    hi = x.astype(BF16)
    r1 = x - hi.astype(F32)
    mid = r1.astype(BF16)
    lo = (r1 - mid.astype(F32)).astype(BF16)
    return _dot(tri, hi) + _dot(tri, mid) + _dot(tri, lo)


def _params(n_grid):
    return pltpu.CompilerParams(dimension_semantics=("arbitrary",) * n_grid, vmem_limit_bytes=VMEM_LIMIT)


def _full_spec(shape):
    nd = len(shape)
    return pl.BlockSpec(tuple(shape), lambda *_: (0,) * nd)


def _rowcall(name, body, n_tokens, tm, rows, consts, residents, row_outs, acc_outs, scratch=()):
    n = n_tokens // tm
    rows = [r if isinstance(r, tuple) else (r, pl.BlockSpec((tm, r.shape[1]), lambda i: (i, 0))) for r in rows]
    n_in, n_w = len(rows) + len(consts), len(residents)
    n_out = len(row_outs) + len(acc_outs)
    in_specs = ([r[1] for r in rows] + [_full_spec(c.shape) for c in consts]
                + [pl.BlockSpec(memory_space=pl.ANY)] * n_w)
    out_shape = ([jax.ShapeDtypeStruct((n_tokens, c), dt) for c, dt in row_outs]
                 + [jax.ShapeDtypeStruct(s, dt) for s, dt in acc_outs])
    out_specs = ([pl.BlockSpec((tm, c), lambda i: (i, 0)) for c, _ in row_outs]
                 + [_full_spec(s) for s, _ in acc_outs])
    scratch_shapes = [pltpu.VMEM(w.shape, w.dtype) for w in residents]
    if n_w:
        scratch_shapes.append(pltpu.SemaphoreType.DMA((n_w,)))
    scratch_shapes += list(scratch)

    def kern(*refs):
        ins, w_hbm = refs[:n_in], refs[n_in:n_in + n_w]
        outs = refs[n_in + n_w:n_in + n_w + n_out]
        rest = refs[n_in + n_w + n_out:]
        w_vmem = rest[:n_w]
        extra = rest[n_w + 1:] if n_w else rest
        i = pl.program_id(0)

        @pl.when(i == 0)
        def _():
            copies = [pltpu.make_async_copy(w_hbm[k], w_vmem[k], rest[n_w].at[k]) for k in range(n_w)]
            for cp in copies:
                cp.start()
            for cp in copies:
                cp.wait()
            for a in outs[len(row_outs):]:
                a[...] = jnp.zeros(a.shape, a.dtype)

        body(i, n, *ins, *w_vmem, *outs, *extra)

    res = pl.pallas_call(
        kern, name=name, grid=(n,), in_specs=in_specs, out_specs=out_specs, out_shape=out_shape,
        scratch_shapes=scratch_shapes, compiler_params=_params(1),
    )(*[r[0] for r in rows], *consts, *residents)
    return res


def _mm_tn(name, x, y, k_dim, n_dim, *, tk, tn, x_off=0, y_off=0, tt=512):
    n_tok = x.shape[0]
    tt = min(tt, n_tok)
    n_t = n_tok // tt
    xb, yb = x_off // tk, y_off // tn

    def kern(x_ref, y_ref, o_ref):
        @pl.when(pl.program_id(2) == 0)
        def _():
            o_ref[...] = jnp.zeros(o_ref.shape, F32)

        o_ref[...] += _dot_tn(x_ref[...], y_ref[...])

    return pl.pallas_call(
        kern, name=name, grid=(k_dim // tk, n_dim // tn, n_t),
        in_specs=[pl.BlockSpec((tt, tk), lambda k, n, t: (t, xb + k)),
                  pl.BlockSpec((tt, tn), lambda k, n, t: (t, yb + n))],
        out_specs=pl.BlockSpec((tk, tn), lambda k, n, t: (k, n)),
        out_shape=jax.ShapeDtypeStruct((k_dim, n_dim), F32),
        compiler_params=_params(3),
    )(x, y)


def _ffn_fwd(name, x, gain, wg, wu, wd):
    n_tok, tm = x.shape[0], min(ROW_BLOCK, x.shape[0])

    def body(i, n, x_ref, g_ref, wg_ref, wu_ref, wd_ref, xo_ref, a_ref, b_ref):
        x_v = x_ref[...]
        xh, _ = _rms_stats(x_v)
        h = (xh * g_ref[...]).astype(BF16)
        y = jnp.zeros((tm, D_MODEL), F32)
        for f0 in range(0, D_FF, FF_SPLIT):
            sl = slice(f0, f0 + FF_SPLIT)
            a = _dot(h, wg_ref[:, sl])
            b = _dot(h, wu_ref[:, sl])
            a_ref[:, sl] = a.astype(BF16)
            b_ref[:, sl] = b.astype(BF16)
            s = (a * jax.nn.sigmoid(a) * b).astype(BF16)
            y = y + _dot(s, wd_ref[sl, :])
        xo_ref[...] = x_v + 0.5 * y

    return _rowcall(name, body, n_tok, tm, [x], [gain], [wg, wu, wd],
                    [(D_MODEL, F32), (D_FF, BF16), (D_FF, BF16)], [])


def _ffn_bwd(name, x, dxo, a, b, gain, wd_t, wg_t, wu_t):
    n_tok, tm = x.shape[0], min(ROW_BLOCK, x.shape[0])

    def body(i, n, x_ref, dxo_ref, a_ref, b_ref, g_ref, wdt_ref, wgt_ref, wut_ref,
             dx_ref, da_ref, db_ref, s_ref, h_ref, dy_ref, dg_ref):
        g = g_ref[...]
        xh, r = _rms_stats(x_ref[...])
        h_ref[...] = (xh * g).astype(BF16)
        dxo_v = dxo_ref[...]
        dy = (0.5 * dxo_v).astype(BF16)
        dy_ref[...] = dy
        dh = jnp.zeros((tm, D_MODEL), F32)
        for f0 in range(0, D_FF, FF_SPLIT):
            sl = slice(f0, f0 + FF_SPLIT)
            a_v = a_ref[:, sl].astype(F32)
            b_v = b_ref[:, sl].astype(F32)
            ds = _dot(dy, wdt_ref[:, sl])
            sig = jax.nn.sigmoid(a_v)
            sa = a_v * sig
            s_ref[:, sl] = (sa * b_v).astype(BF16)
            da = (ds * b_v * (sig * (1.0 + a_v * (1.0 - sig)))).astype(BF16)
            db = (ds * sa).astype(BF16)
            da_ref[:, sl] = da
            db_ref[:, sl] = db
            dh = dh + _dot(da, wgt_ref[sl, :]) + _dot(db, wut_ref[sl, :])
        dx, dg = _rms_bwd(xh, r, g, dh)
        dx_ref[...] = dxo_v + dx
        dg_ref[...] += dg

    return _rowcall(name, body, n_tok, tm, [x, dxo, a, b], [gain], [wd_t, wg_t, wu_t],
                    [(D_MODEL, F32), (D_FF, BF16), (D_FF, BF16), (D_FF, BF16), (D_MODEL, BF16), (D_MODEL, BF16)],
                    [((1, D_MODEL), F32)])


def _proj_fwd(name, x, gain, w1, wf, wgl):
    n_tok, tm = x.shape[0], min(ROW_BLOCK, x.shape[0])
    c0, c1 = 3 * CONV_W, 3 * CONV_W + 2 * SG_W

    def body(i, n, x_ref, g_ref, w1_ref, wf_ref, wgl_ref, cin_ref, sg_ref, qkv_ref, fl_ref, gl_ref):
        xh, _ = _rms_stats(x_ref[...])
        h = (xh * g_ref[...]).astype(BF16)
        cin_ref[...] = _dot(h, w1_ref[:, 0:c0]).astype(BF16)
        sg_ref[...] = _dot(h, w1_ref[:, c0:c1]).astype(BF16)
        qkv_ref[...] = _dot(h, w1_ref[:, c1:W1_COLS]).astype(BF16)
        fl_ref[...] = _dot(h, wf_ref[...])
        gl_ref[...] = _dot(h, wgl_ref[...]).astype(BF16)

    return _rowcall(name, body, n_tok, tm, [x], [gain], [w1, wf, wgl],
                    [(3 * CONV_W, BF16), (2 * SG_W, BF16), (3 * FOX_W, BF16), (LANES, F32), (3 * D_MODEL, BF16)], [])


def _proj_bwd(name, x, dxin, dcin, dsg, dq, dk, dv, dfl, dgl, gain, w1_t, wf_t, wgl_t):
    n_tok, tm = x.shape[0], min(ROW_BLOCK, x.shape[0])
    c0, c1 = 3 * CONV_W, 3 * CONV_W + 2 * SG_W

    def body(i, n, x_ref, dxin_ref, dcin_ref, dsg_ref, dq_ref, dk_ref, dv_ref, dfl_ref, dgl_ref, g_ref,
             w1t_ref, wft_ref, wglt_ref, dx_ref, h_ref, dqkv_ref, dfb_ref, dg_ref):
        g = g_ref[...]
        xh, r = _rms_stats(x_ref[...])
        h_ref[...] = (xh * g).astype(BF16)
        dfb = dfl_ref[...].astype(BF16)
        dfb_ref[...] = dfb
        dh = _dot(dcin_ref[...], w1t_ref[0:c0, :])
        dh = dh + _dot(dsg_ref[...], w1t_ref[c0:c1, :])
        for k, d_ref in enumerate((dq_ref, dk_ref, dv_ref)):
            d_b = d_ref[...].astype(BF16)
            dqkv_ref[:, k * FOX_W:(k + 1) * FOX_W] = d_b
            dh = dh + _dot(d_b, w1t_ref[c1 + k * FOX_W:c1 + (k + 1) * FOX_W, :])
        dh = dh + _dot(dfb, wft_ref[...])
        dh = dh + _dot(dgl_ref[...], wglt_ref[...])
        dx, dg = _rms_bwd(xh, r, g, dh)
        dx_ref[...] = dxin_ref[...] + dx
        dg_ref[...] += dg

    return _rowcall(name, body, n_tok, tm, [x, dxin, dcin, dsg, dq, dk, dv, dfl, dgl], [gain], [w1_t, wf_t, wgl_t],
                    [(D_MODEL, F32), (D_MODEL, BF16), (3 * FOX_W, BF16), (LANES, BF16)], [((1, D_MODEL), F32)])


def _conv_taps(z, prev_z, i):
    tm = z.shape[0]
    row = lax.broadcasted_iota(jnp.int32, (tm, 1), 0)
    live = (i > 0).astype(F32)
    p1, p2 = prev_z[HALO - 1:HALO, :] * live, prev_z[HALO - 2:HALO - 1, :] * live
    z1 = jnp.where(row == 0, p1, pltpu.roll(z, 1, 0))
    z2 = jnp.where(row == 0, p2, jnp.where(row == 1, p1, pltpu.roll(z, 2, 0)))
    return z1, z2


def _prev_spec(tm, cols):
    return pl.BlockSpec((HALO, cols), lambda i: (jnp.maximum(i * (tm // HALO) - 1, 0), 0))


def _next_spec(tm, cols, n):
    last = n * (tm // HALO) - 1
    return pl.BlockSpec((HALO, cols), lambda i: (jnp.minimum((i + 1) * (tm // HALO), last), 0))


def _conv_fwd(name, cin, cw):
    n_tok, tm = cin.shape[0], min(ROW_BLOCK, cin.shape[0])
    w = CONV_W

    def body(i, n, c_ref, p_ref, cw_ref, ya_ref):
        c_v, p_v = c_ref[...].astype(F32), p_ref[...].astype(F32)
        z = c_v[:, w:2 * w] * c_v[:, 2 * w:]
        z1, z2 = _conv_taps(z, p_v[:, w:2 * w] * p_v[:, 2 * w:], i)
        y = cw_ref[0:1, :] * z2 + cw_ref[1:2, :] * z1 + cw_ref[2:3, :] * z
        ya_ref[...] = (c_v[:, 0:w] * y).astype(BF16)

    return _rowcall(name, body, n_tok, tm, [cin, (cin, _prev_spec(tm, 3 * w))], [cw], [], [(w, BF16)], [])[0]


def _conv_bwd(name, cin, dya, cw):
    n_tok, tm = cin.shape[0], min(ROW_BLOCK, cin.shape[0])
    w = CONV_W
    n_blocks = n_tok // tm

    def body(i, n, c_ref, p_ref, nx_ref, dya_ref, ndya_ref, cw_ref, dc_ref, dcw_ref):
        c_v, p_v = c_ref[...].astype(F32), p_ref[...].astype(F32)
        ab, ac, ah = c_v[:, 0:w], c_v[:, w:2 * w], c_v[:, 2 * w:]
        z = ac * ah
        z1, z2 = _conv_taps(z, p_v[:, w:2 * w] * p_v[:, 2 * w:], i)
        w0, w1, w2 = cw_ref[0:1, :], cw_ref[1:2, :], cw_ref[2:3, :]
        y = w0 * z2 + w1 * z1 + w2 * z
        dya_v = dya_ref[...].astype(F32)
        dy = dya_v * ab
        live = (i < n - 1).astype(F32)
        ndy = ndya_ref[...].astype(F32) * nx_ref[:, 0:w].astype(F32) * live
        row = lax.broadcasted_iota(jnp.int32, (tm, 1), 0)
        dy1 = jnp.where(row == tm - 1, ndy[0:1, :], pltpu.roll(dy, tm - 1, 0))
        dy2 = jnp.where(row == tm - 1, ndy[1:2, :], jnp.where(row == tm - 2, ndy[0:1, :], pltpu.roll(dy, tm - 2, 0)))
        dz = w2 * dy + w1 * dy1 + w0 * dy2
        dc_ref[:, 0:w] = (dya_v * y).astype(BF16)
        dc_ref[:, w:2 * w] = (dz * ah).astype(BF16)
        dc_ref[:, 2 * w:] = (dz * ac).astype(BF16)
        dcw_ref[0:1, :] += jnp.sum(dy * z2, axis=0, keepdims=True)
        dcw_ref[1:2, :] += jnp.sum(dy * z1, axis=0, keepdims=True)
        dcw_ref[2:3, :] += jnp.sum(dy * z, axis=0, keepdims=True)

    return _rowcall(name, body, n_tok, tm,
                    [cin, (cin, _prev_spec(tm, 3 * w)), (cin, _next_spec(tm, 3 * w, n_blocks)),
                     dya, (dya, _next_spec(tm, w, n_blocks))],
                    [cw], [], [(3 * w, BF16)], [((8, w), F32)])


def _sg_common(sg_ref, gn_ref):
    s_v = sg_ref[...].astype(F32)
    u, v = s_v[:, 0:SG_W], s_v[:, SG_W:]
    ug, tu = _gelu(u)
    vg, tv = _gelu(v)
    vh, r = _rms_stats(vg)
    vn = (vh * gn_ref[...]).astype(BF16)
    return u, v, ug, tu, tv, vh, r, vn


def _sg_fwd(name, sgin, gn, sgw, bias_full):
    n_tok, tm = sgin.shape[0], min(ROW_BLOCK, sgin.shape[0])

    def body(i, n, sg_ref, gn_ref, w_ref, bias_ref, yb_ref):
        _, _, ug, _, _, _, _, vn = _sg_common(sg_ref, gn_ref)
        tril = lax.broadcasted_iota(jnp.int32, (CHUNK, CHUNK), 0) >= lax.broadcasted_iota(jnp.int32, (CHUNK, CHUNK), 1)
        wt = [jnp.where(tril, w_ref[g], 0.0).astype(BF16) for g in range(SG_G)]
        for c0 in range(0, tm, CHUNK):
            sv = jnp.concatenate(
                [_dot(wt[g], vn[c0:c0 + CHUNK, g * CHUNK:(g + 1) * CHUNK]) for g in range(SG_G)], axis=1)
            sv = sv + bias_ref[...]
            yb_ref[c0:c0 + CHUNK, :] = (ug[c0:c0 + CHUNK, :] * sv).astype(BF16)

    return _rowcall(name, body, n_tok, tm, [sgin], [gn, sgw, bias_full], [], [(SG_W, BF16)], [])[0]


def _sg_bwd(name, sgin, dyb, gn, sgw, sgw_t, bias_full):
    n_tok, tm = sgin.shape[0], min(ROW_BLOCK, sgin.shape[0])

    def body(i, n, sg_ref, dyb_ref, gn_ref, w_ref, wt_ref, bias_ref, dsg_ref, dgn_ref, dw_ref, db_ref):
        u, v, ug, tu, tv, vh, r, vn = _sg_common(sg_ref, gn_ref)
        r0 = lax.broadcasted_iota(jnp.int32, (CHUNK, CHUNK), 0)
        r1 = lax.broadcasted_iota(jnp.int32, (CHUNK, CHUNK), 1)
        wt = [jnp.where(r0 >= r1, w_ref[g], 0.0).astype(BF16) for g in range(SG_G)]
        wtt = [jnp.where(r0 <= r1, wt_ref[g], 0.0).astype(BF16) for g in range(SG_G)]
        dyb_v = dyb_ref[...].astype(F32)
        dvn_rows = []
        for c0 in range(0, tm, CHUNK):
            rows = slice(c0, c0 + CHUNK)
            svs, dvns = [], []
            dsv = dyb_v[rows, :] * ug[rows, :]
            for g in range(SG_G):
                cols = slice(g * CHUNK, (g + 1) * CHUNK)
                svs.append(_dot(wt[g], vn[rows, cols]))
                dsv_g = dsv[:, cols]
                dsv_b = dsv_g.astype(BF16)
                dvns.append(_dot(wtt[g], dsv_b))
                dw_ref[g] += jnp.where(r0 >= r1, _dot_nt(dsv_b, vn[rows, cols]), 0.0)
                db_ref[g] += jnp.broadcast_to(jnp.sum(dsv_g, axis=1, keepdims=True), (CHUNK, CHUNK))
            sv = jnp.concatenate(svs, axis=1) + bias_ref[...]
            dug = dyb_v[rows, :] * sv
            dsg_ref[rows, 0:SG_W] = (dug * _gelu_grad(u[rows, :], tu[rows, :])).astype(BF16)
            dvn_rows.append(jnp.concatenate(dvns, axis=1))
        dvn = jnp.concatenate(dvn_rows, axis=0)
        dvg, dgn = _rms_bwd(vh, r, gn_ref[...], dvn)
        dsg_ref[:, SG_W:] = (dvg * _gelu_grad(v, tv)).astype(BF16)
        dgn_ref[...] += dgn

    return _rowcall(name, body, n_tok, tm, [sgin, dyb], [gn, sgw, sgw_t, bias_full], [], [(2 * SG_W, BF16)],
                    [((1, SG_W), F32), ((SG_G, CHUNK, CHUNK), F32), ((SG_G, CHUNK, CHUNK), F32)])


def _forget_cumsum(name, fl, bf):
    n_tok, tm = fl.shape[0], min(ROW_BLOCK, fl.shape[0])

    def body(i, n, fl_ref, b_ref, c_ref, carry):
        @pl.when(i == 0)
        def _():
            carry[...] = jnp.zeros(carry.shape, F32)

        z = fl_ref[...] + b_ref[...]
        lf = jnp.minimum(z, 0.0) - jnp.log1p(jnp.exp(-jnp.abs(z)))
        tri = (lax.broadcasted_iota(jnp.int32, (tm, tm), 0) >= lax.broadcasted_iota(jnp.int32, (tm, tm), 1)).astype(BF16)
        c = _split3_dot(tri, lf) + carry[...]
        c_ref[...] = c * LOG2E
        carry[...] = c[tm - 1:tm, :]

    return _rowcall(name, body, n_tok, tm, [fl], [bf], [], [(LANES, F32)], [], scratch=[pltpu.VMEM((1, LANES), F32)])[0]


def _forget_bwd(name, dc, fl, bf):
    n_tok, tm = fl.shape[0], min(ROW_BLOCK, fl.shape[0])
    n = n_tok // tm
    rev = pl.BlockSpec((tm, LANES), lambda i: (n - 1 - i, 0))

    def kern(dc_ref, fl_ref, b_ref, dfl_ref, db_ref, carry):
        @pl.when(pl.program_id(0) == 0)
        def _():
            carry[...] = jnp.zeros(carry.shape, F32)
            db_ref[...] = jnp.zeros(db_ref.shape, F32)

        triu = (lax.broadcasted_iota(jnp.int32, (tm, tm), 0) <= lax.broadcasted_iota(jnp.int32, (tm, tm), 1)).astype(BF16)
        dlf = _split3_dot(triu, dc_ref[...]) + carry[...]
        carry[...] = dlf[0:1, :]
        z = fl_ref[...] + b_ref[...]
        dfl = dlf * jax.nn.sigmoid(-z)
        dfl_ref[...] = dfl
        db_ref[...] += jnp.sum(dfl, axis=0, keepdims=True)

    return pl.pallas_call(
        kern, name=name, grid=(n,), in_specs=[rev, rev, _full_spec((1, LANES))],
        out_specs=[rev, _full_spec((1, LANES))],
        out_shape=[jax.ShapeDtypeStruct((n_tok, LANES), F32), jax.ShapeDtypeStruct((1, LANES), F32)],
        scratch_shapes=[pltpu.VMEM((1, LANES), F32)], compiler_params=_params(1),
    )(dc, fl, bf)


def _lane_pick(c_blk, h):
    lane = lax.broadcasted_iota(jnp.int32, c_blk.shape, 1)
    return jnp.broadcast_to(jnp.sum(jnp.where(lane == h, c_blk, 0.0), axis=1, keepdims=True), c_blk.shape)


def _wide(stat, width):
    return jnp.tile(stat, (1, width // LANES))


def _pair_half(rows):
    return lax.broadcasted_iota(jnp.int32, (rows, LANES), 1) // FOX_D


def _col_to_row(col):
    pick = (lax.broadcasted_iota(jnp.int32, (8, LANES), 1) == 0).astype(BF16)
    hi = col.astype(BF16)
    r1 = col - hi.astype(F32)
    mid = r1.astype(BF16)
    lo = (r1 - mid.astype(F32)).astype(BF16)
    return (_dot_nt(pick, hi) + _dot_nt(pick, mid) + _dot_nt(pick, lo))[0:1, :]


def _tri_table(nb, lower):
    rows = [(a, b) for a in range(nb) for b in (range(a + 1) if lower else range(a, nb))]
    return tuple(jnp.asarray([r[k] for r in rows], jnp.int32) for k in range(2))


def _flash_bounds(name, qkv):
    n_tok = qkv.shape[0]
    blk = min(FLASH_BLOCK, n_tok)

    def kern(x_ref, o_ref):
        head = (lax.broadcasted_iota(jnp.int32, (FOX_W, LANES), 0) // FOX_D
                == lax.broadcasted_iota(jnp.int32, (FOX_W, LANES), 1)).astype(BF16)
        rows = []
        for part in range(2):
            x = x_ref[:, part * FOX_W:(part + 1) * FOX_W].astype(F32)
            ssq = _dot((x * x).astype(BF16), head) * 1.01
            rows.append(jnp.sqrt(jnp.max(ssq, axis=0, keepdims=True)))
        o_ref[...] = jnp.concatenate(rows + [jnp.zeros((6, LANES), F32)], axis=0)

    return pl.pallas_call(
        kern, name=name, grid=(n_tok // blk,),
        in_specs=[pl.BlockSpec((blk, 2 * FOX_W), lambda b: (b, 0))], out_specs=pl.BlockSpec((8, LANES), lambda b: (b, 0)),
        out_shape=jax.ShapeDtypeStruct((n_tok // blk * 8, LANES), F32), compiler_params=_params(1),
    )(qkv)


def _prune_tables(bounds, c2, blk):
    nb = c2.shape[0] // blk
    tab = bounds.reshape(nb, 8, LANES)
    c_blocks = c2.reshape(nb, blk, LANES)
    return tuple(t[:, :FOX_H].reshape(-1) for t in (tab[:, 0], tab[:, 1], c_blocks[:, 0], c_blocks[:, blk - 1]))


def _block_is_live(tables, p, i, j):
    qn, kn, c_first, c_last = tables
    live = None
    for e in range(2):
        h = 2 * p + e
        a = FOX_SCALE * LOG2E
        u = a * qn[8 * i + h] * kn[8 * j + h] + c_first[8 * i + h] - c_last[8 * j + h]
        lo = -a * qn[8 * i + h] * kn[8 * i + h]
        this = (u - lo) > -PRUNE_LOG2
        live = this if live is None else (live | this)
    return live


def _flash_fwd(name, qkv, c2, ct2, tables):
    n_tok = qkv.shape[0]
    blk = min(FLASH_BLOCK, n_tok)
    nb = n_tok // blk
    n_pair = FOX_H // 2
    it, jt = _tri_table(nb, True)

    strip = min(FLASH_STRIP, blk)

    def kern(it_ref, jt_ref, qn_ref, kn_ref, cf_ref, cl_ref, q_ref, k_ref, v_ref, c_ref, ct_ref, o_ref, lse_ref,
             m_s, l_s, acc_s, cq_s, qm_s, al_s, s_s, pb_s):
        n = pl.program_id(1)
        i, j = it_ref[n], jt_ref[n]
        half = _pair_half(blk)
        live = _block_is_live((qn_ref, kn_ref, cf_ref, cl_ref), pl.program_id(0), i, j)

        @pl.when(j == 0)
        def _():
            m_s[...] = jnp.full(m_s.shape, -jnp.inf, F32)
            l_s[...] = jnp.zeros(l_s.shape, F32)
            acc_s[...] = jnp.zeros(acc_s.shape, F32)
            for e in range(2):
                cq_s[e] = _lane_pick(c_ref[...], 2 * pl.program_id(0) + e)
                qm_s[e] = jnp.where(half == e, q_ref[...], jnp.zeros_like(q_ref[...]))

        def step(on_diagonal):
            k_v, v_v = k_ref[...], v_ref[...]
            for e in range(2):
                s_s[e] = _dot_nt(qm_s[e], k_v)
            for e in range(2):
                ck = ct_ref[e]
                for r0 in range(0, blk, strip):
                    rs = slice(r0, r0 + strip)
                    t = s_s[e, rs, :] * (FOX_SCALE * LOG2E) - ck
                    if on_diagonal:
                        keep = (lax.broadcasted_iota(jnp.int32, (strip, blk), 1)
                                <= lax.broadcasted_iota(jnp.int32, (strip, blk), 0) + r0)
                        t = jnp.where(keep, t, -jnp.inf)
                    m_old, cq = m_s[e, rs, :], cq_s[e, rs, :]
                    m_new = jnp.maximum(m_old, jnp.max(t, axis=1, keepdims=True) + cq)
                    p = jnp.exp2(t - _wide(m_new - cq, blk))
                    alpha = jnp.exp2(m_old - m_new)
                    l_s[e, rs, :] = alpha * l_s[e, rs, :] + jnp.sum(p, axis=1, keepdims=True)
                    m_s[e, rs, :] = m_new
                    al_s[e, rs, :] = alpha
                    pb_s[e, rs, :] = p.astype(BF16)
            pv = jnp.where(half == 0, _dot(pb_s[0], v_v), _dot(pb_s[1], v_v))
            acc_s[...] = jnp.where(half == 0, al_s[0], al_s[1]) * acc_s[...] + pv

        @pl.when((j < i) & live)
        def _():
            step(False)

        @pl.when(j == i)
        def _():
            step(True)
            o_ref[...] = (acc_s[...] / jnp.where(half == 0, l_s[0], l_s[1])).astype(BF16)
            rows = pl.ds(pl.multiple_of(i * blk, blk), blk)
            lse_ref[0, :, rows] = jnp.concatenate(
                [_col_to_row(m_s[e] + jnp.log(l_s[e]) * LOG2E) for e in range(2)], axis=0)

    def col_block(first, inner):
        def index(p, n, it_r, jt_r, *_):
            return ((jt_r[n] if inner else it_r[n]), first + p)
        return pl.BlockSpec((blk, LANES), index)

    pair_stat = (2, blk, LANES)
    grid_spec = pltpu.PrefetchScalarGridSpec(
        num_scalar_prefetch=6, grid=(n_pair, int(it.shape[0])),
        in_specs=[col_block(0, False), col_block(n_pair, True), col_block(2 * n_pair, True),
                  pl.BlockSpec((blk, LANES), lambda p, n, it_r, jt_r, *_: (it_r[n], 0)),
                  pl.BlockSpec((2, 1, blk), lambda p, n, it_r, jt_r, *_: (p, 0, jt_r[n]))],
        out_specs=[col_block(0, False), pl.BlockSpec((1, 2, n_tok), lambda p, n, it_r, jt_r, *_: (p, 0, 0))],
        scratch_shapes=[pltpu.VMEM(pair_stat, F32), pltpu.VMEM(pair_stat, F32), pltpu.VMEM((blk, LANES), F32),
                        pltpu.VMEM(pair_stat, F32), pltpu.VMEM(pair_stat, BF16), pltpu.VMEM(pair_stat, F32),
                        pltpu.VMEM((2, blk, blk), F32), pltpu.VMEM((2, blk, blk), BF16)],
    )
    return pl.pallas_call(
        kern, name=name, grid_spec=grid_spec,
        out_shape=[jax.ShapeDtypeStruct((n_tok, FOX_W), BF16), jax.ShapeDtypeStruct((n_pair, 2, n_tok), F32)],
        compiler_params=_params(2),
    )(it, jt, *tables, qkv, qkv, qkv, c2, ct2)


def _flash_delta(name, do, o):
    n_tok, tm = do.shape[0], min(ROW_BLOCK, do.shape[0])

    def body(i, n, do_ref, o_ref, d_ref):
        prod = do_ref[...].astype(F32) * o_ref[...].astype(F32)
        head = (lax.broadcasted_iota(jnp.int32, (FOX_W, LANES), 0) // FOX_D
                == lax.broadcasted_iota(jnp.int32, (FOX_W, LANES), 1)).astype(BF16)
        hi = prod.astype(BF16)
        r1 = prod - hi.astype(F32)
        mid = r1.astype(BF16)
        lo = (r1 - mid.astype(F32)).astype(BF16)
        d_ref[...] = _dot(hi, head) + _dot(mid, head) + _dot(lo, head)

    return _rowcall(name, body, n_tok, tm, [do, o], [], [], [(LANES, F32)], [])[0]


def _flash_bwd(name, qkv, do, r_row, d_row, c2, tables):
    n_tok = qkv.shape[0]
    blk = min(FLASH_BLOCK, n_tok)
    nb = n_tok // blk
    n_pair = FOX_H // 2
    jt, it = _tri_table(nb, False)

    def kern(jt_ref, it_ref, qn_ref, kn_ref, cf_ref, cl_ref, k_ref, v_ref, q_ref, do_ref, rr_ref, dr_ref, c_ref,
             dq_ref, dk_ref, dv_ref, dc_ref, km_s, vm_s, ck_s, dck_s):
        n = pl.program_id(1)
        j, i = jt_ref[n], it_ref[n]
        first = _pair_half(blk) == 0
        live = _block_is_live((qn_ref, kn_ref, cf_ref, cl_ref), pl.program_id(0), i, j)

        @pl.when(n == 0)
        def _():
            dq_ref[...] = jnp.zeros(dq_ref.shape, F32)
            dc_ref[...] = jnp.zeros(dc_ref.shape, F32)

        @pl.when(i == j)
        def _():
            dk_ref[...] = jnp.zeros(dk_ref.shape, F32)
            dv_ref[...] = jnp.zeros(dv_ref.shape, F32)
            dck_s[...] = jnp.zeros(dck_s.shape, F32)
            for e in range(2):
                mine = first if e == 0 else jnp.logical_not(first)
                km_s[e] = jnp.where(mine, k_ref[...], jnp.zeros_like(k_ref[...]))
                vm_s[e] = jnp.where(mine, v_ref[...], jnp.zeros_like(v_ref[...]))
                ck_s[e] = _lane_pick(c_ref[...], 2 * pl.program_id(0) + e)

        def step(on_diagonal):
            q_v, do_v, k_v = q_ref[...], do_ref[...], k_ref[...]
            dvs, dks, dqs, sums = [], [], [], []
            for e in range(2):
                t = _dot_nt(km_s[e], q_v) * (FOX_SCALE * LOG2E) - _wide(ck_s[e], blk) - rr_ref[e]
                if on_diagonal:
                    keep = (lax.broadcasted_iota(jnp.int32, (blk, blk), 1)
                            >= lax.broadcasted_iota(jnp.int32, (blk, blk), 0))
                    t = jnp.where(keep, t, -jnp.inf)
                p = jnp.exp2(t)
                ds = p * (_dot_nt(vm_s[e], do_v) - dr_ref[e])
                p_b, ds_b = p.astype(BF16), ds.astype(BF16)
                dvs.append(_dot(p_b, do_v))
                dks.append(_dot(ds_b, q_v))
                dqs.append(_dot_tn(ds_b, k_v))
                dck_s[e] += jnp.broadcast_to(jnp.sum(ds, axis=1, keepdims=True), (blk, LANES))
                sums.append(jnp.sum(ds, axis=0, keepdims=True))
            rows = pl.ds(pl.multiple_of(i * blk, blk), blk)
            dv_ref[...] += jnp.where(first, dvs[0], dvs[1])
            dk_ref[...] += jnp.where(first, dks[0], dks[1]) * FOX_SCALE
            dq_ref[rows, :] += jnp.where(first, dqs[0], dqs[1]) * FOX_SCALE
            dc_ref[0, :, rows] += jnp.concatenate(sums, axis=0)

        @pl.when((i > j) & live)
        def _():
            step(False)

        @pl.when(i == j)
        def _():
            step(True)

        @pl.when(i == nb - 1)
        def _():
            keys = pl.ds(pl.multiple_of(j * blk, blk), blk)
            dc_ref[0, :, keys] -= jnp.concatenate([_col_to_row(dck_s[e]) for e in range(2)], axis=0)

    def col_block(first_col, inner):
        def index(p, n, jt_r, it_r, *_):
            return ((it_r[n] if inner else jt_r[n]), first_col + p)
        return pl.BlockSpec((blk, LANES), index)

    def row_stat():
        return pl.BlockSpec((2, 1, blk), lambda p, n, jt_r, it_r, *_: (p, 0, it_r[n]))

    pair_stat = (2, blk, LANES)
    grid_spec = pltpu.PrefetchScalarGridSpec(
        num_scalar_prefetch=6, grid=(n_pair, int(jt.shape[0])),
        in_specs=[col_block(n_pair, False), col_block(2 * n_pair, False), col_block(0, True), col_block(0, True),
                  row_stat(), row_stat(), pl.BlockSpec((blk, LANES), lambda p, n, jt_r, it_r, *_: (jt_r[n], 0))],
        out_specs=[pl.BlockSpec((n_tok, LANES), lambda p, n, jt_r, it_r, *_: (0, p)),
                   col_block(0, False), col_block(0, False),
                   pl.BlockSpec((1, 2, n_tok), lambda p, n, jt_r, it_r, *_: (p, 0, 0))],
        scratch_shapes=[pltpu.VMEM(pair_stat, BF16), pltpu.VMEM(pair_stat, BF16), pltpu.VMEM(pair_stat, F32),
                        pltpu.VMEM(pair_stat, F32)],
    )
    wide = jax.ShapeDtypeStruct((n_tok, FOX_W), F32)
    return pl.pallas_call(
        kern, name=name, grid_spec=grid_spec,
        out_shape=[wide, wide, wide, jax.ShapeDtypeStruct((n_pair, 2, n_tok), F32)],
        compiler_params=_params(2),
    )(jt, it, *tables, qkv, qkv, qkv, do, r_row, d_row, c2)


def _merge_fwd(name, x, ya, yb, yc, gl, wb, wo):
    n_tok, tm = x.shape[0], min(ROW_BLOCK, x.shape[0])

    def body(i, n, x_ref, ya_ref, yb_ref, yc_ref, gl_ref, wb_ref, wo_ref, xo_ref):
        merged = jnp.zeros((tm, D_MODEL), F32)
        for k, y_ref in enumerate((ya_ref, yb_ref, yc_ref)):
            gate = jax.nn.sigmoid(gl_ref[:, k * D_MODEL:(k + 1) * D_MODEL].astype(F32))
            merged = merged + gate * _dot(y_ref[...], wb_ref[k])
        xo_ref[...] = x_ref[...] + _dot(merged.astype(BF16), wo_ref[...])

    return _rowcall(name, body, n_tok, tm, [x, ya, yb, yc, gl], [], [wb, wo], [(D_MODEL, F32)], [])[0]


def _merge_bwd(name, dxo, ya, yb, yc, gl, wb, wb_t, wo_t):
    n_tok, tm = dxo.shape[0], min(ROW_BLOCK, dxo.shape[0])

    def body(i, n, dxo_ref, ya_ref, yb_ref, yc_ref, gl_ref, wb_ref, wbt_ref, wot_ref,
             mg_ref, dob_ref, dgl_ref, dbr_ref, dya_ref, dyb_ref, dyc_ref):
        dob = dxo_ref[...].astype(BF16)
        dob_ref[...] = dob
        dm = _dot(dob, wot_ref[...])
        merged = jnp.zeros((tm, D_MODEL), F32)
        for k, (y_ref, dy_ref) in enumerate(((ya_ref, dya_ref), (yb_ref, dyb_ref), (yc_ref, dyc_ref))):
            cols = slice(k * D_MODEL, (k + 1) * D_MODEL)
            gate = jax.nn.sigmoid(gl_ref[:, cols].astype(F32))
            br = _dot(y_ref[...], wb_ref[k])
            merged = merged + gate * br
            dgl_ref[:, cols] = (dm * br * gate * (1.0 - gate)).astype(BF16)
            dbr = (dm * gate).astype(BF16)
            dbr_ref[:, cols] = dbr
            dy_ref[...] = _dot(dbr, wbt_ref[k]).astype(BF16)
        mg_ref[...] = merged.astype(BF16)

    return _rowcall(name, body, n_tok, tm, [dxo, ya, yb, yc, gl], [], [wb, wb_t, wo_t],
                    [(D_MODEL, BF16), (D_MODEL, BF16), (3 * D_MODEL, BF16), (3 * D_MODEL, BF16),
                     (CONV_W, BF16), (SG_W, BF16), (FOX_W, BF16)], [])


def _mem_fwd(name, mem, gain, wk, wv):
    n_mem = mem.shape[0]

    def kern(m_ref, g_ref, wk_ref, wv_ref, mn_ref, k_ref, v_ref):
        mh, _ = _rms_stats(m_ref[...])
        mn = (mh * g_ref[...]).astype(BF16)
        mn_ref[...] = mn
        k_ref[...] = _dot(mn, wk_ref[...]).astype(BF16)
        v_ref[...] = _dot(mn, wv_ref[...]).astype(BF16)

    shp = jax.ShapeDtypeStruct((n_mem, D_MODEL), BF16)
    return pl.pallas_call(
        kern, name=name, grid=(1,),
        in_specs=[_full_spec(mem.shape), _full_spec(gain.shape), _full_spec(wk.shape), _full_spec(wv.shape)],
        out_specs=[_full_spec(shp.shape)] * 3, out_shape=[shp] * 3, compiler_params=_params(1),
    )(mem, gain, wk, wv)


def _mem_bwd(name, mem, gain, dkx, dvx, wk_t, wv_t):
    n_mem = mem.shape[0]

    def kern(m_ref, g_ref, dk_ref, dv_ref, wkt_ref, wvt_ref, dkb_ref, dvb_ref, dg_ref):
        mh, _ = _rms_stats(m_ref[...])
        dkb, dvb = dk_ref[...].astype(BF16), dv_ref[...].astype(BF16)
        dkb_ref[...] = dkb
        dvb_ref[...] = dvb
        dm = _dot(dkb, wkt_ref[...]) + _dot(dvb, wvt_ref[...])
        dg_ref[...] = jnp.sum(dm * mh, axis=0, keepdims=True)

    shp = jax.ShapeDtypeStruct((n_mem, D_MODEL), BF16)
    args = (mem, gain, dkx, dvx, wk_t, wv_t)
    return pl.pallas_call(
        kern, name=name, grid=(1,), in_specs=[_full_spec(a.shape) for a in args],
        out_specs=[_full_spec(shp.shape)] * 2 + [_full_spec((1, D_MODEL))],
        out_shape=[shp, shp, jax.ShapeDtypeStruct((1, D_MODEL), F32)], compiler_params=_params(1),
    )(*args)


def _xa_probs(q_b, kx_ref, hd):
    cols = slice(hd * XA_D, (hd + 1) * XA_D)
    s = _dot_nt(q_b[:, cols], kx_ref[:, cols]) * (XA_D ** -0.5)
    e = jnp.exp(s - jnp.max(s, axis=1, keepdims=True))
    return e / jnp.sum(e, axis=1, keepdims=True)


def _xa_fwd(name, x, gain, kx, vx, wq, wo):
    n_tok, tm = x.shape[0], min(ROW_BLOCK, x.shape[0])

    def body(i, n, x_ref, g_ref, kx_ref, vx_ref, wq_ref, wo_ref, xo_ref):
        x_v = x_ref[...]
        xh, _ = _rms_stats(x_v)
        q_b = _dot((xh * g_ref[...]).astype(BF16), wq_ref[...]).astype(BF16)
        o = jnp.concatenate(
            [_dot(_xa_probs(q_b, kx_ref, hd).astype(BF16), vx_ref[:, hd * XA_D:(hd + 1) * XA_D]) for hd in range(XA_H)],
            axis=1)
        xo_ref[...] = x_v + _dot(o.astype(BF16), wo_ref[...])

    return _rowcall(name, body, n_tok, tm, [x], [gain, kx, vx], [wq, wo], [(D_MODEL, F32)], [])[0]


def _xa_bwd(name, x, dxo, gain, kx, vx, wq, wq_t, wo_t):
    n_tok, tm = x.shape[0], min(ROW_BLOCK, x.shape[0])
    n_mem = kx.shape[0]

    def body(i, n, x_ref, dxo_ref, g_ref, kx_ref, vx_ref, wq_ref, wqt_ref, wot_ref,
             dx_ref, h_ref, o_ref, dq_ref, dy_ref, dkx_ref, dvx_ref, dg_ref):
        g = g_ref[...]
        xh, r = _rms_stats(x_ref[...])
        h = (xh * g).astype(BF16)
        h_ref[...] = h
        q_b = _dot(h, wq_ref[...]).astype(BF16)
        dxo_v = dxo_ref[...]
        dy = dxo_v.astype(BF16)
        dy_ref[...] = dy
        do = _dot(dy, wot_ref[...])
        for hd in range(XA_H):
            cols = slice(hd * XA_D, (hd + 1) * XA_D)
            p = _xa_probs(q_b, kx_ref, hd)
            p_b = p.astype(BF16)
            o_ref[:, cols] = _dot(p_b, vx_ref[:, cols]).astype(BF16)
            do_h = do[:, cols].astype(BF16)
            dvx_ref[:, cols] += _dot_tn(p_b, do_h)
            dp = _dot_nt(do_h, vx_ref[:, cols])
            ds = p * (dp - jnp.sum(dp * p, axis=1, keepdims=True))
            ds_b = (ds * (XA_D ** -0.5)).astype(BF16)
            dq_ref[:, cols] = _dot(ds_b, kx_ref[:, cols]).astype(BF16)
            dkx_ref[:, cols] += _dot_tn(ds_b, q_b[:, cols])
        dh = _dot(dq_ref[...], wqt_ref[...])
        dx, dg = _rms_bwd(xh, r, g, dh)
        dx_ref[...] = dxo_v + dx
        dg_ref[...] += dg

    return _rowcall(name, body, n_tok, tm, [x, dxo], [gain, kx, vx], [wq, wq_t, wo_t],
                    [(D_MODEL, F32), (D_MODEL, BF16), (D_MODEL, BF16), (D_MODEL, BF16), (D_MODEL, BF16)],
                    [((n_mem, D_MODEL), F32), ((n_mem, D_MODEL), F32), ((1, D_MODEL), F32)])


def _loss_head(name, x, target, gain):
    n_tok, tm = x.shape[0], min(ROW_BLOCK, x.shape[0])

    def body(i, n, x_ref, t_ref, g_ref, dx_ref, loss_ref, dg_ref):
        g = g_ref[...]
        xh, r = _rms_stats(x_ref[...])
        err = xh * g - t_ref[...]
        loss_ref[...] += 0.5 * jnp.sum(jnp.sum(err * err, axis=1, keepdims=True) / D_MODEL, axis=0, keepdims=True)
        dx, dg = _rms_bwd(xh, r, g, err / D_MODEL)
        dx_ref[...] = dx
        dg_ref[...] += dg

    return _rowcall(name, body, n_tok, tm, [x, target], [gain], [], [(D_MODEL, F32)],
                    [((8, LANES), F32), ((1, D_MODEL), F32)])


def _adamw(name, w, g, m, v):
    rows, cols = w.shape
    tr = 128 if rows % 128 == 0 else rows

    def kern(w_ref, g_ref, m_ref, v_ref, d_ref, mo_ref, vo_ref):
        g_v = g_ref[...]
        m_new = ADAM_B1 * m_ref[...] + (1.0 - ADAM_B1) * g_v
        v_new = ADAM_B2 * v_ref[...] + (1.0 - ADAM_B2) * (g_v * g_v)
        m_hat = m_new / (1.0 - ADAM_B1 ** ADAM_STEP)
        v_hat = v_new / (1.0 - ADAM_B2 ** ADAM_STEP)
        d_ref[...] = -ADAM_LR * (m_hat / (jnp.sqrt(v_hat) + ADAM_EPS) + ADAM_WD * w_ref[...])
        mo_ref[...] = m_new
        vo_ref[...] = v_new

    spec = pl.BlockSpec((tr, cols), lambda i: (i, 0))
    shp = jax.ShapeDtypeStruct((rows, cols), F32)
    return pl.pallas_call(kern, name=name, grid=(rows // tr,), in_specs=[spec] * 4, out_specs=[spec] * 3,
                          out_shape=[shp] * 3, compiler_params=_params(1))(w, g, m, v)


def _add_pair(name, gp, got, core):
    _, n, rows, cols = gp.shape
    spec = pl.BlockSpec((1, PACK_ROW_BLOCK, cols), lambda j, r, core_ref: (j, r, 0))
    mine = pl.BlockSpec((1, PACK_ROW_BLOCK, cols), lambda j, r, core_ref: (core_ref[0] * n + j, r, 0))

    def kern(core_ref, a_ref, b_ref, o_ref):
        o_ref[...] = (a_ref[...] + b_ref[...]).astype(BF16)

    grid_spec = pltpu.PrefetchScalarGridSpec(num_scalar_prefetch=1, grid=(n, rows // PACK_ROW_BLOCK),
                                             in_specs=[mine, spec], out_specs=spec)
    return pl.pallas_call(kern, name=name, grid_spec=grid_spec, out_shape=jax.ShapeDtypeStruct(got.shape, BF16),
                          compiler_params=_params(2))(core, gp.reshape(2 * n, rows, cols), got)


def _sum_parts(name, s, got, chip):
    _, rows, cols = s.shape
    tr = PACK_ROW_BLOCK

    def kern(chip_ref, s_ref, g_ref, o_ref):
        acc = s_ref[0].astype(F32)
        for k in range(3):
            acc = acc + g_ref[k].astype(F32)
        o_ref[...] = acc

    grid_spec = pltpu.PrefetchScalarGridSpec(
        num_scalar_prefetch=1, grid=(rows // tr,),
        in_specs=[pl.BlockSpec((1, tr, cols), lambda r, chip_ref: (chip_ref[0], r, 0)),
                  pl.BlockSpec((3, tr, cols), lambda r, chip_ref: (0, r, 0))],
        out_specs=pl.BlockSpec((tr, cols), lambda r, chip_ref: (r, 0)))
    return pl.pallas_call(kern, name=name, grid_spec=grid_spec, out_shape=jax.ShapeDtypeStruct((rows, cols), F32),
                          compiler_params=_params(1))(chip, s, got)


def _sum_slots(name, a, tr):
    n, rows, cols = a.shape

    def kern(a_ref, o_ref):
        acc = a_ref[0].astype(F32)
        for k in range(1, n):
            acc = acc + a_ref[k].astype(F32)
        o_ref[...] = acc

    return pl.pallas_call(kern, name=name, grid=(rows // tr,),
                          in_specs=[pl.BlockSpec((n, tr, cols), lambda r: (0, r, 0))],
                          out_specs=pl.BlockSpec((tr, cols), lambda r: (r, 0)),
                          out_shape=jax.ShapeDtypeStruct((rows, cols), F32), compiler_params=_params(1))(a)


_ANY = pl.BlockSpec(memory_space=pl.ANY)


COMM_CHUNKS = 8


def _place():
    x, y, c = lax.axis_index("x"), lax.axis_index("y"), lax.axis_index("c")
    chips = [(1 - x, y), (x, 1 - y), (1 - x, 1 - y)]
    return x, y, c, chips


def _rows(ref, q, n):
    step = ref.shape[0] // n
    return ref.at[pl.ds(q * step, step)]


def _gather_chips(name, w):
    n = COMM_CHUNKS

    def kern(w_ref, o_ref, send_sems, recv_sems):
        x, y, c, chips = _place()
        me, sib = 2 * x + y, (x, y, 1 - c)

        def copy(k, src, dst, to):
            return pltpu.make_async_remote_copy(src_ref=src, dst_ref=dst, send_sem=send_sems.at[k],
                                                recv_sem=recv_sems.at[k], device_id=to, device_id_type=MESH_ID)

        first = [copy(k * n + q, _rows(w_ref.at[c], q, n), _rows(o_ref.at[me, c], q, n), (cx, cy, c))
                 for k, (cx, cy) in enumerate(chips) for q in range(n)]
        for cp in first:
            cp.start()
        passed = []
        for k, (cx, cy) in enumerate(chips):
            for q in range(n):
                landed = _rows(o_ref.at[2 * cx + cy, c], q, n)
                copy(k * n + q, landed, landed, (x, y, c)).wait_recv()
                passed.append(copy((3 + k) * n + q, landed, landed, sib))
                passed[-1].start()
        for k, (cx, cy) in enumerate(chips):
            for q in range(n):
                theirs = _rows(o_ref.at[2 * cx + cy, 1 - c], q, n)
                copy((3 + k) * n + q, theirs, theirs, (x, y, c)).wait_recv()
        for cp in first + passed:
            cp.wait_send()

    return pl.pallas_call(
        kern, name=name, in_specs=[_ANY], out_specs=_ANY,
        out_shape=jax.ShapeDtypeStruct((N_CHIPS,) + w.shape, w.dtype),
        scratch_shapes=[pltpu.SemaphoreType.DMA((6 * n,)), pltpu.SemaphoreType.DMA((6 * n,))],
        compiler_params=pltpu.CompilerParams(has_side_effects=True),
    )(w)


def _gather_all(name, v):
    def kern(v_ref, o_ref, send_sems, recv_sems, local_sem):
        x, y, c, _ = _place()
        me = 4 * x + 2 * y + c
        mine = pltpu.make_async_copy(v_ref, o_ref.at[me], local_sem)
        mine.start()
        peers = []
        for k in range(1, 8):
            px = 1 - x if k & 4 else x
            py = 1 - y if k & 2 else y
            pc = 1 - c if k & 1 else c
            peers.append((px, py, pc))
        copies = [pltpu.make_async_remote_copy(src_ref=v_ref, dst_ref=o_ref.at[me], send_sem=send_sems.at[k],
                                               recv_sem=recv_sems.at[k], device_id=p, device_id_type=MESH_ID)
                  for k, p in enumerate(peers)]
        for cp in copies:
            cp.start()
        for k, (px, py, pc) in enumerate(peers):
            pltpu.make_async_remote_copy(src_ref=v_ref, dst_ref=o_ref.at[4 * px + 2 * py + pc], send_sem=send_sems.at[k],
                                         recv_sem=recv_sems.at[k], device_id=(x, y, c), device_id_type=MESH_ID).wait_recv()
        for cp in copies:
            cp.wait_send()
        mine.wait()

    return pl.pallas_call(
        kern, name=name, in_specs=[_ANY], out_specs=_ANY, out_shape=jax.ShapeDtypeStruct((8,) + v.shape, v.dtype),
        scratch_shapes=[pltpu.SemaphoreType.DMA((7,)), pltpu.SemaphoreType.DMA((7,)), pltpu.SemaphoreType.DMA(())],
        compiler_params=pltpu.CompilerParams(has_side_effects=True),
    )(v)


def _swap_halves(name, gp):
    n = COMM_CHUNKS
    n_slots = gp.shape[1]

    def kern(g_ref, got_ref, send_sems, recv_sems):
        x, y, c, _ = _place()
        swaps = [pltpu.make_async_remote_copy(src_ref=_rows(g_ref.at[1 - c, j], q, n), dst_ref=_rows(got_ref.at[j], q, n),
                                              send_sem=send_sems.at[j * n + q], recv_sem=recv_sems.at[j * n + q],
                                              device_id=(x, y, 1 - c), device_id_type=MESH_ID)
                 for j in range(n_slots) for q in range(n)]
        for cp in swaps:
            cp.start()
        for cp in swaps:
            cp.wait()

    return pl.pallas_call(
        kern, name=name, in_specs=[_ANY], out_specs=_ANY, out_shape=jax.ShapeDtypeStruct(gp.shape[1:], gp.dtype),
        scratch_shapes=[pltpu.SemaphoreType.DMA((n_slots * n,)), pltpu.SemaphoreType.DMA((n_slots * n,))],
        compiler_params=pltpu.CompilerParams(has_side_effects=True),
    )(gp)


def _scatter_chips(name, s):
    n = COMM_CHUNKS

    def kern(s_ref, o_ref, send_sems, recv_sems):
        x, y, c, chips = _place()
        copies = [pltpu.make_async_remote_copy(src_ref=_rows(s_ref.at[2 * cx + cy], q, n), dst_ref=_rows(o_ref.at[k], q, n),
                                               send_sem=send_sems.at[k * n + q], recv_sem=recv_sems.at[k * n + q],
                                               device_id=(cx, cy, c), device_id_type=MESH_ID)
                  for k, (cx, cy) in enumerate(chips) for q in range(n)]
        for cp in copies:
            cp.start()
        for k in range(3):
            for q in range(n):
                landed = _rows(o_ref.at[k], q, n)
                pltpu.make_async_remote_copy(src_ref=landed, dst_ref=landed, send_sem=send_sems.at[k * n + q],
                                             recv_sem=recv_sems.at[k * n + q], device_id=(x, y, c),
                                             device_id_type=MESH_ID).wait_recv()
        for cp in copies:
            cp.wait_send()

    return pl.pallas_call(
        kern, name=name, in_specs=[_ANY], out_specs=_ANY, out_shape=jax.ShapeDtypeStruct((3,) + s.shape[1:], s.dtype),
        scratch_shapes=[pltpu.SemaphoreType.DMA((3 * n,)), pltpu.SemaphoreType.DMA((3 * n,))],
        compiler_params=pltpu.CompilerParams(has_side_effects=True),
    )(s)


def _join_halves(name, r):
    n = COMM_CHUNKS

    def kern(r_ref, o_ref, send_sems, recv_sems):
        x, y, c, _ = _place()
        swaps = [pltpu.make_async_remote_copy(src_ref=_rows(r_ref, q, n), dst_ref=_rows(o_ref, q, n),
                                              send_sem=send_sems.at[q], recv_sem=recv_sems.at[q],
                                              device_id=(x, y, 1 - c), device_id_type=MESH_ID) for q in range(n)]
        for cp in swaps:
            cp.start()
        for cp in swaps:
            cp.wait()

    return pl.pallas_call(
        kern, name=name, in_specs=[_ANY], out_specs=_ANY, out_shape=jax.ShapeDtypeStruct(r.shape, r.dtype),
        scratch_shapes=[pltpu.SemaphoreType.DMA((n,))] * 2, compiler_params=pltpu.CompilerParams(has_side_effects=True),
    )(r)


def _pack_rows(n_elems):
    rows = -(-n_elems // (2 * PACK_COLS * PACK_ROW_BLOCK)) * PACK_ROW_BLOCK
    return rows


def _pack(pieces, dtype):
    flat = jnp.concatenate([p.astype(dtype).reshape(-1) for p in pieces])
    rows = _pack_rows(flat.shape[0])
    flat = jnp.pad(flat, (0, 2 * rows * PACK_COLS - flat.shape[0]))
    return flat.reshape(2, rows, PACK_COLS)


def _unpack(packed, shapes):
    lead = packed.shape[:-3]
    flat = packed.reshape(lead + (-1,))
    out, off = [], 0
    for shp in shapes:
        size = 1
        for s in shp:
            size *= s
        out.append(flat[..., off:off + size].reshape(lead + tuple(shp)))
        off += size
    return out


def _shard(a, axis, j):
    size = a.shape[axis] // N_CHIPS
    return lax.slice_in_dim(a, j * size, (j + 1) * size, axis=axis)


def _t(a):
    return jnp.swapaxes(a, -1, -2)


def _rows_from_lanes(a):
    return a[:, :FOX_H].T[:, None, :]


def _lanes_from_heads(a):
    return jnp.pad(a[:, 0, :].T, ((0, 0), (0, LANES - FOX_H)))


def _layer_weights(wts, small, l):
    w_in = wts["w_in"][l]
    w = {
        "wf": jnp.pad(w_in[:, W1_COLS:W1_COLS + FOX_H], ((0, 0), (0, LANES - FOX_H))),
        "w1": w_in[:, :W1_COLS], "wgl": w_in[:, W1_COLS + FOX_H:],
    }
    for nm in ("ffn1_w_gate", "ffn1_w_up", "ffn1_w_down", "w_branch", "w_out", "xa_wq", "xa_wk", "xa_wv", "xa_wo",
               "ffn2_w_gate", "ffn2_w_up", "ffn2_w_down"):
        w[nm] = wts[nm][l]
    for nm in list(w):
        w[nm + "_t"] = _t(w[nm])
    for nm in ("ffn1_norm", "mix_norm", "sg_norm", "xa_norm", "mem_norm", "ffn2_norm"):
        w[nm] = small[nm][l][None, :]
    w["conv_w"] = jnp.pad(small["conv_w"][l], ((0, 5), (0, 0)))
    w["sg_w"] = small["sg_w"][l]
    w["sg_w_t"] = _t(small["sg_w"][l])
    w["sg_bias"] = jnp.repeat(small["sg_b"][l].T, CHUNK, axis=1)
    w["fox_b"] = jnp.pad(small["fox_b_f"][l][None, :], ((0, 0), (0, LANES - FOX_H)))
    return w


def _layer_fwd(l, x, mem, w):
    s = {"x0": x}
    x1, s["a1"], s["b1"] = _ffn_fwd(f"ffn1_fwd_{l}", x, w["ffn1_norm"], w["ffn1_w_gate"], w["ffn1_w_up"], w["ffn1_w_down"])
    s["x1"] = x1
    cin, sgin, qkv, fl, gl = _proj_fwd(f"proj_fwd_{l}", x1, w["mix_norm"], w["w1"], w["wf"], w["wgl"])
    ya = _conv_fwd(f"conv_fwd_{l}", cin, w["conv_w"])
    yb = _sg_fwd(f"sg_fwd_{l}", sgin, w["sg_norm"], w["sg_w"], w["sg_bias"])
    c2 = _forget_cumsum(f"forget_fwd_{l}", fl, w["fox_b"])
    tables = _prune_tables(_flash_bounds(f"flash_bounds_{l}", qkv), c2, min(FLASH_BLOCK, qkv.shape[0]))
    yc, lse2 = _flash_fwd(f"flash_fwd_{l}", qkv, c2, _rows_from_lanes(c2), tables)
    x2 = _merge_fwd(f"merge_fwd_{l}", x1, ya, yb, yc, gl, w["w_branch"], w["w_out"])
    s.update(cin=cin, sgin=sgin, qkv=qkv, fl=fl, gl=gl, ya=ya, yb=yb, yc=yc, c2=c2, lse2=lse2, x2=x2, tables=tables)
    s["mn"], s["kx"], s["vx"] = _mem_fwd(f"mem_fwd_{l}", mem, w["mem_norm"], w["xa_wk"], w["xa_wv"])
    x3 = _xa_fwd(f"xa_fwd_{l}", x2, w["xa_norm"], s["kx"], s["vx"], w["xa_wq"], w["xa_wo"])
    s["x3"] = x3
    x4, s["a2"], s["b2"] = _ffn_fwd(f"ffn2_fwd_{l}", x3, w["ffn2_norm"], w["ffn2_w_gate"], w["ffn2_w_up"], w["ffn2_w_down"])
    return x4, s


def _ffn_grads(tag, x, dxo, a, b, w, pre):
    dx, da, db, sv, h, dy, dg = _ffn_bwd(f"{pre}_bwd_{tag}", x, dxo, a, b, w[pre + "_norm"], w[pre + "_w_down_t"],
                                         w[pre + "_w_gate_t"], w[pre + "_w_up_t"])
    g = {
        pre + "_norm": dg[0],
        pre + "_w_gate": _mm_tn(f"{pre}_dwg_{tag}", h, da, D_MODEL, D_FF, tk=D_MODEL, tn=FF_SPLIT),
        pre + "_w_up": _mm_tn(f"{pre}_dwu_{tag}", h, db, D_MODEL, D_FF, tk=D_MODEL, tn=FF_SPLIT),
        pre + "_w_down": _mm_tn(f"{pre}_dwd_{tag}", sv, dy, D_FF, D_MODEL, tk=FF_SPLIT, tn=D_MODEL),
    }
    return dx, g


def _layer_bwd(l, dx, mem, w, s):
    g = {}
    dx, gf = _ffn_grads(l, s["x3"], dx, s["a2"], s["b2"], w, "ffn2")
    g.update(gf)

    dx, h, o, dq, dy, dkx, dvx, dg = _xa_bwd(f"xa_bwd_{l}", s["x2"], dx, w["xa_norm"], s["kx"], s["vx"], w["xa_wq"],
                                              w["xa_wq_t"], w["xa_wo_t"])
    g["xa_norm"] = dg[0]
    g["xa_wq"] = _mm_tn(f"xa_dwq_{l}", h, dq, D_MODEL, D_MODEL, tk=D_MODEL, tn=D_MODEL)
    g["xa_wo"] = _mm_tn(f"xa_dwo_{l}", o, dy, D_MODEL, D_MODEL, tk=D_MODEL, tn=D_MODEL)
    dkb, dvb, dgm = _mem_bwd(f"mem_bwd_{l}", mem, w["mem_norm"], dkx, dvx, w["xa_wk_t"], w["xa_wv_t"])
    g["mem_norm"] = dgm[0]
    g["xa_wk"] = _mm_tn(f"xa_dwk_{l}", s["mn"], dkb, D_MODEL, D_MODEL, tk=D_MODEL, tn=D_MODEL)
    g["xa_wv"] = _mm_tn(f"xa_dwv_{l}", s["mn"], dvb, D_MODEL, D_MODEL, tk=D_MODEL, tn=D_MODEL)

    mg, dob, dgl, dbr, dya, dyb, dyc = _merge_bwd(f"merge_bwd_{l}", dx, s["ya"], s["yb"], s["yc"], s["gl"], w["w_branch"],
                                                  w["w_branch_t"], w["w_out_t"])
    g["w_out"] = _mm_tn(f"dwout_{l}", mg, dob, D_MODEL, D_MODEL, tk=D_MODEL, tn=D_MODEL)
    g["w_branch"] = jnp.stack([
        _mm_tn(f"dwbranch{k}_{l}", y, dbr, CONV_W, D_MODEL, tk=CONV_W, tn=D_MODEL, y_off=k * D_MODEL)
        for k, y in enumerate((s["ya"], s["yb"], s["yc"]))])

    dcin, dcw = _conv_bwd(f"conv_bwd_{l}", s["cin"], dya, w["conv_w"])
    g["conv_w"] = dcw[:3]
    dsg, dgn, dsw, dsb = _sg_bwd(f"sg_bwd_{l}", s["sgin"], dyb, w["sg_norm"], w["sg_w"], w["sg_w_t"], w["sg_bias"])
    g["sg_norm"], g["sg_w"], g["sg_b"] = dgn[0], dsw, dsb[:, :, 0]
    d_row = _rows_from_lanes(_flash_delta(f"flash_delta_{l}", dyc, s["yc"]))
    r_row = s["lse2"].reshape(FOX_H, 1, -1) - _rows_from_lanes(s["c2"])
    dq, dk, dv, dc_rows = _flash_bwd(f"flash_bwd_{l}", s["qkv"], dyc, r_row, d_row, s["c2"], s["tables"])
    dc = _lanes_from_heads(dc_rows.reshape(FOX_H, 1, -1))
    dfl, dbf = _forget_bwd(f"forget_bwd_{l}", dc, s["fl"], w["fox_b"])
    g["fox_b_f"] = dbf[0, :FOX_H]

    dx, h, dqkv, dfb, dg = _proj_bwd(f"proj_bwd_{l}", s["x1"], dx, dcin, dsg, dq, dk, dv, dfl, dgl, w["mix_norm"],
                                     w["w1_t"], w["wf_t"], w["wgl_t"])
    g["mix_norm"] = dg[0]
    c0, c1 = 3 * CONV_W, 3 * CONV_W + 2 * SG_W
    g["w_in"] = jnp.concatenate([
        _mm_tn(f"dwin_conv_{l}", h, dcin, D_MODEL, c0, tk=D_MODEL, tn=c0),
        _mm_tn(f"dwin_sg_{l}", h, dsg, D_MODEL, c1 - c0, tk=D_MODEL, tn=c1 - c0),
        _mm_tn(f"dwin_qkv_{l}", h, dqkv, D_MODEL, 3 * FOX_W, tk=D_MODEL, tn=3 * FOX_W),
        _mm_tn(f"dwin_f_{l}", h, dfb, D_MODEL, LANES, tk=D_MODEL, tn=LANES)[:, :FOX_H],
        _mm_tn(f"dwin_gl_{l}", h, dgl, D_MODEL, 3 * D_MODEL, tk=D_MODEL, tn=3 * D_MODEL // 2),
    ], axis=1)

    dx, gf = _ffn_grads(l, s["x0"], dx, s["a1"], s["b1"], w, "ffn1")
    g.update(gf)
    return dx, g


def _local_step(x, mem, target, wts, small):
    saved, lw = [], []
    for l in range(DEPTH):
        lw.append(_layer_weights(wts, small, l))
        x, s = _layer_fwd(l, x, mem, lw[l])
        saved.append(s)
    fin = small["final_norm"][None, :]
    dx, loss, dgf = _loss_head("loss_head", x, target, fin)
    layer_grads = [None] * DEPTH
    for l in reversed(range(DEPTH)):
        dx, layer_grads[l] = _layer_bwd(l, dx, mem, lw[l], saved[l])
    grads = {nm: jnp.stack([layer_grads[l][nm] for l in range(DEPTH)]) for nm in WEIGHTS if nm != "final_norm"}
    grads["final_norm"] = dgf[0]
    return loss[0, 0], dx, grads


def _small_slab(vals):
    rows = []
    for v in vals:
        flat = v.astype(F32).reshape(-1)
        n = -(-flat.shape[0] // LANES) * LANES
        rows.append(jnp.pad(flat, (0, n - flat.shape[0])).reshape(-1, LANES))
    slab = jnp.concatenate(rows)
    pad = -slab.shape[0] % 8
    return jnp.pad(slab, ((0, pad), (0, 0)))


def _small_unslab(slab, shapes):
    out, off = [], 0
    for shp in shapes:
        size = 1
        for s in shp:
            size *= s
        n_rows = -(-size // LANES)
        out.append(slab[off:off + n_rows].reshape(-1)[:size].reshape(shp))
        off += n_rows
    return out


def kernel(x, mem, ffn1_norm, ffn1_w_gate, ffn1_w_up, ffn1_w_down, mix_norm, w_in, conv_w, sg_norm, sg_w, sg_b, fox_b_f, w_branch, w_out, xa_norm, mem_norm, xa_wq, xa_wk, xa_wv, xa_wo, ffn2_norm, ffn2_w_gate, ffn2_w_up, ffn2_w_down, final_norm, loss_target, m_ffn1_norm, m_ffn1_w_gate, m_ffn1_w_up, m_ffn1_w_down, m_mix_norm, m_w_in, m_conv_w, m_sg_norm, m_sg_w, m_sg_b, m_fox_b_f, m_w_branch, m_w_out, m_xa_norm, m_mem_norm, m_xa_wq, m_xa_wk, m_xa_wv, m_xa_wo, m_ffn2_norm, m_ffn2_w_gate, m_ffn2_w_up, m_ffn2_w_down, m_final_norm, v_ffn1_norm, v_ffn1_w_gate, v_ffn1_w_up, v_ffn1_w_down, v_mix_norm, v_w_in, v_conv_w, v_sg_norm, v_sg_w, v_sg_b, v_fox_b_f, v_w_branch, v_w_out, v_xa_norm, v_mem_norm, v_xa_wq, v_xa_wk, v_xa_wv, v_xa_wo, v_ffn2_norm, v_ffn2_w_gate, v_ffn2_w_up, v_ffn2_w_down, v_final_norm):
    args = dict(locals())
    wv = {nm: args[nm] for nm in WEIGHTS}
    mv = {nm: args["m_" + nm] for nm in WEIGHTS}
    vv = {nm: args["v_" + nm] for nm in WEIGHTS}
    chip = 2 * lax.axis_index("x") + lax.axis_index("y")
    core = lax.axis_index("c")

    shard_shapes = [wv[nm].shape for nm, _ in BIG]
    gathered = _gather_chips("gather_weights", _pack([wv[nm] for nm, _ in BIG], BF16))
    pieces = _unpack(gathered, shard_shapes)
    wts = {nm: jnp.concatenate([jnp.where(chip == j, wv[nm].astype(BF16), p[j]) for j in range(N_CHIPS)], axis=ax)
           for (nm, ax), p in zip(BIG, pieces)}
    taps = _gather_all("gather_taps", _small_slab([conv_w]))
    taps = [_small_unslab(taps[2 * j], [conv_w.shape])[0] for j in range(N_CHIPS)]
    small = {nm: wv[nm] for nm in SMALL}
    small["conv_w"] = jnp.concatenate(taps, axis=2)

    loss, dx, grads = _local_step(x[0], mem[0], loss_target[0], wts, small)
    loss = lax.psum(loss, ("x", "y", "c"))

    small_shapes = [grads[nm].shape for nm in SMALL]
    slots = _gather_all("gather_small_grads", _small_slab([grads[nm] for nm in SMALL]))
    small_sum = _sum_slots("sum_small_grads", slots, slots.shape[1])
    small_g = dict(zip(SMALL, _small_unslab(small_sum, small_shapes)))
    taps_g = small_g["conv_w"]
    small_g["conv_w"] = lax.dynamic_slice_in_dim(taps_g, chip * conv_w.shape[2], conv_w.shape[2], axis=2)

    gp = jnp.stack([_pack([_shard(grads[nm], ax, j) for nm, ax in BIG], F32) for j in range(N_CHIPS)], axis=1)
    got = _swap_halves("reduce_swap_cores", gp)
    chip_sum = _add_pair("reduce_add_cores", gp, got, core.astype(jnp.int32).reshape(1))
    parts = _scatter_chips("reduce_scatter_chips", chip_sum)
    half = _sum_parts("reduce_add_chips", chip_sum, parts, chip.astype(jnp.int32).reshape(1))
    other = _join_halves("reduce_join_cores", half)
    full = jnp.where(core == 0, jnp.stack([half, other]), jnp.stack([other, half]))
    big_g = dict(zip([nm for nm, _ in BIG], _unpack(full, shard_shapes)))

    g_out = {**small_g, **big_g}
    delta, new_m, new_v = {}, {}, {}
    for nm, _ in BIG:
        shp = wv[nm].shape
        two_d = (-1, shp[-1])
        d, m2, v2 = _adamw("adamw_" + nm, wv[nm].reshape(two_d), g_out[nm].reshape(two_d), mv[nm].reshape(two_d),
                           vv[nm].reshape(two_d))
        delta[nm], new_m[nm], new_v[nm] = d.reshape(shp), m2.reshape(shp), v2.reshape(shp)
    slab_shapes = [wv[nm].shape for nm in SMALL]
    d, m2, v2 = _adamw("adamw_small", _small_slab([wv[nm] for nm in SMALL]), _small_slab([g_out[nm] for nm in SMALL]),
                       _small_slab([mv[nm] for nm in SMALL]), _small_slab([vv[nm] for nm in SMALL]))
    for out, slab in ((delta, d), (new_m, m2), (new_v, v2)):
        out.update(zip(SMALL, _small_unslab(slab, slab_shapes)))

    return (loss, dx[None], *[g_out[nm] for nm in WEIGHTS], *[delta[nm] for nm in WEIGHTS],
            *[new_m[nm] for nm in WEIGHTS], *[new_v[nm] for nm in WEIGHTS])
```

```python
import functools

import jax
import jax.numpy as jnp
from jax import lax
from jax.experimental import pallas as pl
from jax.experimental.pallas import tpu as pltpu

F32, BF16 = jnp.float32, jnp.bfloat16
MESH_ID = pl.DeviceIdType.MESH

D_MODEL = 1024
DEPTH = 2
D_FF = 2816
CONV_W = 512
SG_W = 512
SG_G = 4
CHUNK = 128
FOX_H = 8
FOX_D = 64
FOX_W = FOX_H * FOX_D
FOX_SCALE = FOX_D ** -0.5
LOG2E = 1.4426950408889634
XA_H = 4
XA_D = D_MODEL // XA_H
N_CHIPS = 4
RMS_EPS = 1e-6
W1_COLS = 3 * CONV_W + 2 * SG_W + 3 * FOX_W
LANES = 128
HALO = 16

ADAM_LR, ADAM_B1, ADAM_B2, ADAM_EPS, ADAM_WD, ADAM_STEP = 0.001, 0.9, 0.999, 1e-08, 0.01, 10

ROW_BLOCK = 256
FLASH_BLOCK = 512
FLASH_STRIP = 32
PRUNE_LOG2 = 40.0
FF_SPLIT = 1408
VMEM_LIMIT = 56 * 1024 * 1024
PACK_COLS = 1024
PACK_ROW_BLOCK = 256

BIG = (
    ("ffn1_w_gate", 2), ("ffn1_w_up", 2), ("ffn1_w_down", 1), ("w_in", 2), ("w_branch", 3), ("w_out", 1),
    ("xa_wq", 1), ("xa_wk", 1), ("xa_wv", 1), ("xa_wo", 1), ("ffn2_w_gate", 2), ("ffn2_w_up", 2), ("ffn2_w_down", 1),
)
BIG_KIND = {nm: ("branch" if nm == "w_branch" else "cols" if ax == 2 else "rows") for nm, ax in BIG}
SMALL = ("ffn1_norm", "mix_norm", "conv_w", "sg_norm", "sg_w", "sg_b", "fox_b_f", "xa_norm", "mem_norm", "ffn2_norm",
         "final_norm")
WEIGHTS = ("ffn1_norm", "ffn1_w_gate", "ffn1_w_up", "ffn1_w_down", "mix_norm", "w_in", "conv_w", "sg_norm", "sg_w",
           "sg_b", "fox_b_f", "w_branch", "w_out", "xa_norm", "mem_norm", "xa_wq", "xa_wk", "xa_wv", "xa_wo",
           "ffn2_norm", "ffn2_w_gate", "ffn2_w_up", "ffn2_w_down", "final_norm")


def _dot(a, b):
    return jnp.dot(a, b, preferred_element_type=F32)


def _dot_nt(a, b):
    return lax.dot_general(a, b, (((1,), (1,)), ((), ())), preferred_element_type=F32)


def _dot_tn(a, b):
    return lax.dot_general(a, b, (((0,), (0,)), ((), ())), preferred_element_type=F32)


def _rms_stats(x):
    r = lax.rsqrt(jnp.mean(x * x, axis=-1, keepdims=True) + RMS_EPS)
    return x * r, r


def _rms_bwd(xh, r, g, dy):
    dg = jnp.sum(dy * xh, axis=0, keepdims=True)
    dxh = dy * g
    dx = r * (dxh - xh * jnp.mean(dxh * xh, axis=-1, keepdims=True))
    return dx, dg


def _gelu(x):
    k = 0.7978845608028654
    t = jnp.tanh(k * (x + 0.044715 * x * x * x))
    return 0.5 * x * (1.0 + t), t


def _gelu_grad(x, t):
    k = 0.7978845608028654
    return 0.5 * (1.0 + t) + 0.5 * x * (1.0 - t * t) * k * (1.0 + 3.0 * 0.044715 * x * x)


def _split3_dot(tri, x):
    hi = x.astype(BF16)
    r1 = x - hi.astype(F32)
    mid = r1.astype(BF16)
    lo = (r1 - mid.astype(F32)).astype(BF16)
    return _dot(tri, hi) + _dot(tri, mid) + _dot(tri, lo)


def _params(n_grid):
    return pltpu.CompilerParams(dimension_semantics=("arbitrary",) * n_grid, vmem_limit_bytes=VMEM_LIMIT)


def _full_spec(shape):
    nd = len(shape)
    return pl.BlockSpec(tuple(shape), lambda *_: (0,) * nd)


def _rowcall(name, body, n_tokens, tm, rows, consts, residents, row_outs, acc_outs, scratch=()):
    n = n_tokens // tm
    rows = [r if isinstance(r, tuple) else (r, pl.BlockSpec((tm, r.shape[1]), lambda i: (i, 0))) for r in rows]
    n_in, n_w = len(rows) + len(consts), len(residents)
    n_out = len(row_outs) + len(acc_outs)
    in_specs = ([r[1] for r in rows] + [_full_spec(c.shape) for c in consts]
                + [pl.BlockSpec(memory_space=pl.ANY)] * n_w)
    out_shape = ([jax.ShapeDtypeStruct((n_tokens, c), dt) for c, dt in row_outs]
                 + [jax.ShapeDtypeStruct(s, dt) for s, dt in acc_outs])
    out_specs = ([pl.BlockSpec((tm, c), lambda i: (i, 0)) for c, _ in row_outs]
                 + [_full_spec(s) for s, _ in acc_outs])
    scratch_shapes = [pltpu.VMEM(w.shape, w.dtype) for w in residents]
    if n_w:
        scratch_shapes.append(pltpu.SemaphoreType.DMA((n_w,)))
    scratch_shapes += list(scratch)

    def kern(*refs):
        ins, w_hbm = refs[:n_in], refs[n_in:n_in + n_w]
        outs = refs[n_in + n_w:n_in + n_w + n_out]
        rest = refs[n_in + n_w + n_out:]
        w_vmem = rest[:n_w]
        extra = rest[n_w + 1:] if n_w else rest
        i = pl.program_id(0)

        @pl.when(i == 0)
        def _():
            copies = [pltpu.make_async_copy(w_hbm[k], w_vmem[k], rest[n_w].at[k]) for k in range(n_w)]
            for cp in copies:
                cp.start()
            for cp in copies:
                cp.wait()
            for a in outs[len(row_outs):]:
                a[...] = jnp.zeros(a.shape, a.dtype)

        body(i, n, *ins, *w_vmem, *outs, *extra)

    res = pl.pallas_call(
        kern, name=name, grid=(n,), in_specs=in_specs, out_specs=out_specs, out_shape=out_shape,
        scratch_shapes=scratch_shapes, compiler_params=_params(1),
    )(*[r[0] for r in rows], *consts, *residents)
    return res


def _mm_tn(name, x, y, k_dim, n_dim, *, tk, tn, x_off=0, y_off=0, tt=512):
    n_tok = x.shape[0]
    tt = min(tt, n_tok)
    n_t = n_tok // tt
    xb, yb = x_off // tk, y_off // tn

    def kern(x_ref, y_ref, o_ref):
        @pl.when(pl.program_id(2) == 0)
        def _():
            o_ref[...] = jnp.zeros(o_ref.shape, F32)

        o_ref[...] += _dot_tn(x_ref[...], y_ref[...])

    return pl.pallas_call(
        kern, name=name, grid=(k_dim // tk, n_dim // tn, n_t),
        in_specs=[pl.BlockSpec((tt, tk), lambda k, n, t: (t, xb + k)),
                  pl.BlockSpec((tt, tn), lambda k, n, t: (t, yb + n))],
        out_specs=pl.BlockSpec((tk, tn), lambda k, n, t: (k, n)),
        out_shape=jax.ShapeDtypeStruct((k_dim, n_dim), F32),
        compiler_params=_params(3),
    )(x, y)


def _ffn_fwd(name, x, gain, wg, wu, wd):
    n_tok, tm = x.shape[0], min(ROW_BLOCK, x.shape[0])

    def body(i, n, x_ref, g_ref, wg_ref, wu_ref, wd_ref, xo_ref, a_ref, b_ref):
        x_v = x_ref[...]
        xh, _ = _rms_stats(x_v)
        h = (xh * g_ref[...]).astype(BF16)
        y = jnp.zeros((tm, D_MODEL), F32)
        for f0 in range(0, D_FF, FF_SPLIT):
            sl = slice(f0, f0 + FF_SPLIT)
            a = _dot(h, wg_ref[:, sl])
            b = _dot(h, wu_ref[:, sl])
            a_ref[:, sl] = a.astype(BF16)
            b_ref[:, sl] = b.astype(BF16)
            s = (a * jax.nn.sigmoid(a) * b).astype(BF16)
            y = y + _dot(s, wd_ref[sl, :])
        xo_ref[...] = x_v + 0.5 * y

    return _rowcall(name, body, n_tok, tm, [x], [gain], [wg, wu, wd],
                    [(D_MODEL, F32), (D_FF, BF16), (D_FF, BF16)], [])


def _ffn_bwd(name, x, dxo, a, b, gain, wd_t, wg_t, wu_t):
    n_tok, tm = x.shape[0], min(ROW_BLOCK, x.shape[0])

    def body(i, n, x_ref, dxo_ref, a_ref, b_ref, g_ref, wdt_ref, wgt_ref, wut_ref,
             dx_ref, da_ref, db_ref, s_ref, h_ref, dy_ref, dg_ref):
        g = g_ref[...]
        xh, r = _rms_stats(x_ref[...])
        h_ref[...] = (xh * g).astype(BF16)
        dxo_v = dxo_ref[...]
        dy = (0.5 * dxo_v).astype(BF16)
        dy_ref[...] = dy
        dh = jnp.zeros((tm, D_MODEL), F32)
        for f0 in range(0, D_FF, FF_SPLIT):
            sl = slice(f0, f0 + FF_SPLIT)
            a_v = a_ref[:, sl].astype(F32)
            b_v = b_ref[:, sl].astype(F32)
            ds = _dot(dy, wdt_ref[:, sl])
            sig = jax.nn.sigmoid(a_v)
            sa = a_v * sig
            s_ref[:, sl] = (sa * b_v).astype(BF16)
            da = (ds * b_v * (sig * (1.0 + a_v * (1.0 - sig)))).astype(BF16)
            db = (ds * sa).astype(BF16)
            da_ref[:, sl] = da
            db_ref[:, sl] = db
            dh = dh + _dot(da, wgt_ref[sl, :]) + _dot(db, wut_ref[sl, :])
        dx, dg = _rms_bwd(xh, r, g, dh)
        dx_ref[...] = dxo_v + dx
        dg_ref[...] += dg

    return _rowcall(name, body, n_tok, tm, [x, dxo, a, b], [gain], [wd_t, wg_t, wu_t],
                    [(D_MODEL, F32), (D_FF, BF16), (D_FF, BF16), (D_FF, BF16), (D_MODEL, BF16), (D_MODEL, BF16)],
                    [((1, D_MODEL), F32)])


def _proj_fwd(name, x, gain, w1, wf, wgl):
    n_tok, tm = x.shape[0], min(ROW_BLOCK, x.shape[0])
    c0, c1 = 3 * CONV_W, 3 * CONV_W + 2 * SG_W

    def body(i, n, x_ref, g_ref, w1_ref, wf_ref, wgl_ref, cin_ref, sg_ref, qkv_ref, fl_ref, gl_ref):
        xh, _ = _rms_stats(x_ref[...])
        h = (xh * g_ref[...]).astype(BF16)
        cin_ref[...] = _dot(h, w1_ref[:, 0:c0]).astype(BF16)
        sg_ref[...] = _dot(h, w1_ref[:, c0:c1]).astype(BF16)
        qkv_ref[...] = _dot(h, w1_ref[:, c1:W1_COLS]).astype(BF16)
        fl_ref[...] = _dot(h, wf_ref[...])
        gl_ref[...] = _dot(h, wgl_ref[...]).astype(BF16)

    return _rowcall(name, body, n_tok, tm, [x], [gain], [w1, wf, wgl],
                    [(3 * CONV_W, BF16), (2 * SG_W, BF16), (3 * FOX_W, BF16), (LANES, F32), (3 * D_MODEL, BF16)], [])


def _proj_bwd(name, x, dxin, dcin, dsg, dq, dk, dv, dfl, dgl, gain, w1_t, wf_t, wgl_t):
    n_tok, tm = x.shape[0], min(ROW_BLOCK, x.shape[0])
    c0, c1 = 3 * CONV_W, 3 * CONV_W + 2 * SG_W

    def body(i, n, x_ref, dxin_ref, dcin_ref, dsg_ref, dq_ref, dk_ref, dv_ref, dfl_ref, dgl_ref, g_ref,
             w1t_ref, wft_ref, wglt_ref, dx_ref, h_ref, dqkv_ref, dfb_ref, dg_ref):
        g = g_ref[...]
        xh, r = _rms_stats(x_ref[...])
        h_ref[...] = (xh * g).astype(BF16)
        dfb = dfl_ref[...].astype(BF16)
        dfb_ref[...] = dfb
        dh = _dot(dcin_ref[...], w1t_ref[0:c0, :])
        dh = dh + _dot(dsg_ref[...], w1t_ref[c0:c1, :])
        for k, d_ref in enumerate((dq_ref, dk_ref, dv_ref)):
            d_b = d_ref[...].astype(BF16)
            dqkv_ref[:, k * FOX_W:(k + 1) * FOX_W] = d_b
            dh = dh + _dot(d_b, w1t_ref[c1 + k * FOX_W:c1 + (k + 1) * FOX_W, :])
        dh = dh + _dot(dfb, wft_ref[...])
        dh = dh + _dot(dgl_ref[...], wglt_ref[...])
        dx, dg = _rms_bwd(xh, r, g, dh)
        dx_ref[...] = dxin_ref[...] + dx
        dg_ref[...] += dg

    return _rowcall(name, body, n_tok, tm, [x, dxin, dcin, dsg, dq, dk, dv, dfl, dgl], [gain], [w1_t, wf_t, wgl_t],
                    [(D_MODEL, F32), (D_MODEL, BF16), (3 * FOX_W, BF16), (LANES, BF16)], [((1, D_MODEL), F32)])


def _conv_taps(z, prev_z, i):
    tm = z.shape[0]
    row = lax.broadcasted_iota(jnp.int32, (tm, 1), 0)
    live = (i > 0).astype(F32)
    p1, p2 = prev_z[HALO - 1:HALO, :] * live, prev_z[HALO - 2:HALO - 1, :] * live
    z1 = jnp.where(row == 0, p1, pltpu.roll(z, 1, 0))
    z2 = jnp.where(row == 0, p2, jnp.where(row == 1, p1, pltpu.roll(z, 2, 0)))
    return z1, z2


def _prev_spec(tm, cols):
    return pl.BlockSpec((HALO, cols), lambda i: (jnp.maximum(i * (tm // HALO) - 1, 0), 0))


def _next_spec(tm, cols, n):
    last = n * (tm // HALO) - 1
    return pl.BlockSpec((HALO, cols), lambda i: (jnp.minimum((i + 1) * (tm // HALO), last), 0))


def _conv_fwd(name, cin, cw):
    n_tok, tm = cin.shape[0], min(ROW_BLOCK, cin.shape[0])
    w = CONV_W

    def body(i, n, c_ref, p_ref, cw_ref, ya_ref):
        c_v, p_v = c_ref[...].astype(F32), p_ref[...].astype(F32)
        z = c_v[:, w:2 * w] * c_v[:, 2 * w:]
        z1, z2 = _conv_taps(z, p_v[:, w:2 * w] * p_v[:, 2 * w:], i)
        y = cw_ref[0:1, :] * z2 + cw_ref[1:2, :] * z1 + cw_ref[2:3, :] * z
        ya_ref[...] = (c_v[:, 0:w] * y).astype(BF16)

    return _rowcall(name, body, n_tok, tm, [cin, (cin, _prev_spec(tm, 3 * w))], [cw], [], [(w, BF16)], [])[0]


def _conv_bwd(name, cin, dya, cw):
    n_tok, tm = cin.shape[0], min(ROW_BLOCK, cin.shape[0])
    w = CONV_W
    n_blocks = n_tok // tm

    def body(i, n, c_ref, p_ref, nx_ref, dya_ref, ndya_ref, cw_ref, dc_ref, dcw_ref):
        c_v, p_v = c_ref[...].astype(F32), p_ref[...].astype(F32)
        ab, ac, ah = c_v[:, 0:w], c_v[:, w:2 * w], c_v[:, 2 * w:]
        z = ac * ah
        z1, z2 = _conv_taps(z, p_v[:, w:2 * w] * p_v[:, 2 * w:], i)
        w0, w1, w2 = cw_ref[0:1, :], cw_ref[1:2, :], cw_ref[2:3, :]
        y = w0 * z2 + w1 * z1 + w2 * z
        dya_v = dya_ref[...].astype(F32)
        dy = dya_v * ab
        live = (i < n - 1).astype(F32)
        ndy = ndya_ref[...].astype(F32) * nx_ref[:, 0:w].astype(F32) * live
        row = lax.broadcasted_iota(jnp.int32, (tm, 1), 0)
        dy1 = jnp.where(row == tm - 1, ndy[0:1, :], pltpu.roll(dy, tm - 1, 0))
        dy2 = jnp.where(row == tm - 1, ndy[1:2, :], jnp.where(row == tm - 2, ndy[0:1, :], pltpu.roll(dy, tm - 2, 0)))
        dz = w2 * dy + w1 * dy1 + w0 * dy2
        dc_ref[:, 0:w] = (dya_v * y).astype(BF16)
        dc_ref[:, w:2 * w] = (dz * ah).astype(BF16)
        dc_ref[:, 2 * w:] = (dz * ac).astype(BF16)
        dcw_ref[0:1, :] += jnp.sum(dy * z2, axis=0, keepdims=True)
        dcw_ref[1:2, :] += jnp.sum(dy * z1, axis=0, keepdims=True)
        dcw_ref[2:3, :] += jnp.sum(dy * z, axis=0, keepdims=True)

    return _rowcall(name, body, n_tok, tm,
                    [cin, (cin, _prev_spec(tm, 3 * w)), (cin, _next_spec(tm, 3 * w, n_blocks)),
                     dya, (dya, _next_spec(tm, w, n_blocks))],
                    [cw], [], [(3 * w, BF16)], [((8, w), F32)])


def _sg_common(sg_ref, gn_ref):
    s_v = sg_ref[...].astype(F32)
    u, v = s_v[:, 0:SG_W], s_v[:, SG_W:]
    ug, tu = _gelu(u)
    vg, tv = _gelu(v)
    vh, r = _rms_stats(vg)
    vn = (vh * gn_ref[...]).astype(BF16)
    return u, v, ug, tu, tv, vh, r, vn


def _sg_fwd(name, sgin, gn, sgw, bias_full):
    n_tok, tm = sgin.shape[0], min(ROW_BLOCK, sgin.shape[0])

    def body(i, n, sg_ref, gn_ref, w_ref, bias_ref, yb_ref):
        _, _, ug, _, _, _, _, vn = _sg_common(sg_ref, gn_ref)
        tril = lax.broadcasted_iota(jnp.int32, (CHUNK, CHUNK), 0) >= lax.broadcasted_iota(jnp.int32, (CHUNK, CHUNK), 1)
        wt = [jnp.where(tril, w_ref[g], 0.0).astype(BF16) for g in range(SG_G)]
        for c0 in range(0, tm, CHUNK):
            sv = jnp.concatenate(
                [_dot(wt[g], vn[c0:c0 + CHUNK, g * CHUNK:(g + 1) * CHUNK]) for g in range(SG_G)], axis=1)
            sv = sv + bias_ref[...]
            yb_ref[c0:c0 + CHUNK, :] = (ug[c0:c0 + CHUNK, :] * sv).astype(BF16)

    return _rowcall(name, body, n_tok, tm, [sgin], [gn, sgw, bias_full], [], [(SG_W, BF16)], [])[0]


def _sg_bwd(name, sgin, dyb, gn, sgw, sgw_t, bias_full):
    n_tok, tm = sgin.shape[0], min(ROW_BLOCK, sgin.shape[0])

    def body(i, n, sg_ref, dyb_ref, gn_ref, w_ref, wt_ref, bias_ref, dsg_ref, dgn_ref, dw_ref, db_ref):
        u, v, ug, tu, tv, vh, r, vn = _sg_common(sg_ref, gn_ref)
        r0 = lax.broadcasted_iota(jnp.int32, (CHUNK, CHUNK), 0)
        r1 = lax.broadcasted_iota(jnp.int32, (CHUNK, CHUNK), 1)
        wt = [jnp.where(r0 >= r1, w_ref[g], 0.0).astype(BF16) for g in range(SG_G)]
        wtt = [jnp.where(r0 <= r1, wt_ref[g], 0.0).astype(BF16) for g in range(SG_G)]
        dyb_v = dyb_ref[...].astype(F32)
        dvn_rows = []
        for c0 in range(0, tm, CHUNK):
            rows = slice(c0, c0 + CHUNK)
            svs, dvns = [], []
            dsv = dyb_v[rows, :] * ug[rows, :]
            for g in range(SG_G):
                cols = slice(g * CHUNK, (g + 1) * CHUNK)
                svs.append(_dot(wt[g], vn[rows, cols]))
                dsv_g = dsv[:, cols]
                dsv_b = dsv_g.astype(BF16)
                dvns.append(_dot(wtt[g], dsv_b))
                dw_ref[g] += jnp.where(r0 >= r1, _dot_nt(dsv_b, vn[rows, cols]), 0.0)
                db_ref[g] += jnp.broadcast_to(jnp.sum(dsv_g, axis=1, keepdims=True), (CHUNK, CHUNK))
            sv = jnp.concatenate(svs, axis=1) + bias_ref[...]
            dug = dyb_v[rows, :] * sv
            dsg_ref[rows, 0:SG_W] = (dug * _gelu_grad(u[rows, :], tu[rows, :])).astype(BF16)
            dvn_rows.append(jnp.concatenate(dvns, axis=1))
        dvn = jnp.concatenate(dvn_rows, axis=0)
        dvg, dgn = _rms_bwd(vh, r, gn_ref[...], dvn)
        dsg_ref[:, SG_W:] = (dvg * _gelu_grad(v, tv)).astype(BF16)
        dgn_ref[...] += dgn

    return _rowcall(name, body, n_tok, tm, [sgin, dyb], [gn, sgw, sgw_t, bias_full], [], [(2 * SG_W, BF16)],
                    [((1, SG_W), F32), ((SG_G, CHUNK, CHUNK), F32), ((SG_G, CHUNK, CHUNK), F32)])


def _forget_cumsum(name, fl, bf):
    n_tok, tm = fl.shape[0], min(ROW_BLOCK, fl.shape[0])

    def body(i, n, fl_ref, b_ref, c_ref, carry):
        @pl.when(i == 0)
        def _():
            carry[...] = jnp.zeros(carry.shape, F32)

        z = fl_ref[...] + b_ref[...]
        lf = jnp.minimum(z, 0.0) - jnp.log1p(jnp.exp(-jnp.abs(z)))
        tri = (lax.broadcasted_iota(jnp.int32, (tm, tm), 0) >= lax.broadcasted_iota(jnp.int32, (tm, tm), 1)).astype(BF16)
        c = _split3_dot(tri, lf) + carry[...]
        c_ref[...] = c * LOG2E
        carry[...] = c[tm - 1:tm, :]

    return _rowcall(name, body, n_tok, tm, [fl], [bf], [], [(LANES, F32)], [], scratch=[pltpu.VMEM((1, LANES), F32)])[0]


def _forget_bwd(name, dc, fl, bf):
    n_tok, tm = fl.shape[0], min(ROW_BLOCK, fl.shape[0])
    n = n_tok // tm
    rev = pl.BlockSpec((tm, LANES), lambda i: (n - 1 - i, 0))

    def kern(dc_ref, fl_ref, b_ref, dfl_ref, db_ref, carry):
        @pl.when(pl.program_id(0) == 0)
        def _():
            carry[...] = jnp.zeros(carry.shape, F32)
            db_ref[...] = jnp.zeros(db_ref.shape, F32)

        triu = (lax.broadcasted_iota(jnp.int32, (tm, tm), 0) <= lax.broadcasted_iota(jnp.int32, (tm, tm), 1)).astype(BF16)
        dlf = _split3_dot(triu, dc_ref[...]) + carry[...]
        carry[...] = dlf[0:1, :]
        z = fl_ref[...] + b_ref[...]
        dfl = dlf * jax.nn.sigmoid(-z)
        dfl_ref[...] = dfl
        db_ref[...] += jnp.sum(dfl, axis=0, keepdims=True)

    return pl.pallas_call(
        kern, name=name, grid=(n,), in_specs=[rev, rev, _full_spec((1, LANES))],
        out_specs=[rev, _full_spec((1, LANES))],
        out_shape=[jax.ShapeDtypeStruct((n_tok, LANES), F32), jax.ShapeDtypeStruct((1, LANES), F32)],
        scratch_shapes=[pltpu.VMEM((1, LANES), F32)], compiler_params=_params(1),
    )(dc, fl, bf)


def _lane_pick(c_blk, h):
    lane = lax.broadcasted_iota(jnp.int32, c_blk.shape, 1)
    return jnp.broadcast_to(jnp.sum(jnp.where(lane == h, c_blk, 0.0), axis=1, keepdims=True), c_blk.shape)


def _wide(stat, width):
    return jnp.tile(stat, (1, width // LANES))


def _pair_half(rows):
    return lax.broadcasted_iota(jnp.int32, (rows, LANES), 1) // FOX_D


def _col_to_row(col):
    pick = (lax.broadcasted_iota(jnp.int32, (8, LANES), 1) == 0).astype(BF16)
    hi = col.astype(BF16)
    r1 = col - hi.astype(F32)
    mid = r1.astype(BF16)
    lo = (r1 - mid.astype(F32)).astype(BF16)
    return (_dot_nt(pick, hi) + _dot_nt(pick, mid) + _dot_nt(pick, lo))[0:1, :]


def _tri_table(nb, lower):
    rows = [(a, b) for a in range(nb) for b in (range(a + 1) if lower else range(a, nb))]
    return tuple(jnp.asarray([r[k] for r in rows], jnp.int32) for k in range(2))


def _flash_bounds(name, qkv):
    n_tok = qkv.shape[0]
    blk = min(FLASH_BLOCK, n_tok)

    def kern(x_ref, o_ref):
        head = (lax.broadcasted_iota(jnp.int32, (FOX_W, LANES), 0) // FOX_D
                == lax.broadcasted_iota(jnp.int32, (FOX_W, LANES), 1)).astype(BF16)
        rows = []
        for part in range(2):
            x = x_ref[:, part * FOX_W:(part + 1) * FOX_W].astype(F32)
            ssq = _dot((x * x).astype(BF16), head) * 1.01
            rows.append(jnp.sqrt(jnp.max(ssq, axis=0, keepdims=True)))
        o_ref[...] = jnp.concatenate(rows + [jnp.zeros((6, LANES), F32)], axis=0)

    return pl.pallas_call(
        kern, name=name, grid=(n_tok // blk,),
        in_specs=[pl.BlockSpec((blk, 2 * FOX_W), lambda b: (b, 0))], out_specs=pl.BlockSpec((8, LANES), lambda b: (b, 0)),
        out_shape=jax.ShapeDtypeStruct((n_tok // blk * 8, LANES), F32), compiler_params=_params(1),
    )(qkv)


def _prune_tables(bounds, c2, blk):
    nb = c2.shape[0] // blk
    tab = bounds.reshape(nb, 8, LANES)
    c_blocks = c2.reshape(nb, blk, LANES)
    return tuple(t[:, :FOX_H].reshape(-1) for t in (tab[:, 0], tab[:, 1], c_blocks[:, 0], c_blocks[:, blk - 1]))


def _block_is_live(tables, p, i, j):
    qn, kn, c_first, c_last = tables
    live = None
    for e in range(2):
        h = 2 * p + e
        a = FOX_SCALE * LOG2E
        u = a * qn[8 * i + h] * kn[8 * j + h] + c_first[8 * i + h] - c_last[8 * j + h]
        lo = -a * qn[8 * i + h] * kn[8 * i + h]
        this = (u - lo) > -PRUNE_LOG2
        live = this if live is None else (live | this)
    return live


def _flash_fwd(name, qkv, c2, ct2, tables):
    n_tok = qkv.shape[0]
    blk = min(FLASH_BLOCK, n_tok)
    nb = n_tok // blk
    n_pair = FOX_H // 2
    it, jt = _tri_table(nb, True)

    strip = min(FLASH_STRIP, blk)

    def kern(it_ref, jt_ref, qn_ref, kn_ref, cf_ref, cl_ref, q_ref, k_ref, v_ref, c_ref, ct_ref, o_ref, lse_ref,
             m_s, l_s, acc_s, cq_s, qm_s, al_s, s_s, pb_s):
        n = pl.program_id(1)
        i, j = it_ref[n], jt_ref[n]
        half = _pair_half(blk)
        live = _block_is_live((qn_ref, kn_ref, cf_ref, cl_ref), pl.program_id(0), i, j)

        @pl.when(j == 0)
        def _():
            m_s[...] = jnp.full(m_s.shape, -jnp.inf, F32)
            l_s[...] = jnp.zeros(l_s.shape, F32)
            acc_s[...] = jnp.zeros(acc_s.shape, F32)
            for e in range(2):
                cq_s[e] = _lane_pick(c_ref[...], 2 * pl.program_id(0) + e)
                qm_s[e] = jnp.where(half == e, q_ref[...], jnp.zeros_like(q_ref[...]))

        def step(on_diagonal):
            k_v, v_v = k_ref[...], v_ref[...]
            for e in range(2):
                s_s[e] = _dot_nt(qm_s[e], k_v)
            for e in range(2):
                ck = ct_ref[e]
                for r0 in range(0, blk, strip):
                    rs = slice(r0, r0 + strip)
                    t = s_s[e, rs, :] * (FOX_SCALE * LOG2E) - ck
                    if on_diagonal:
                        keep = (lax.broadcasted_iota(jnp.int32, (strip, blk), 1)
                                <= lax.broadcasted_iota(jnp.int32, (strip, blk), 0) + r0)
                        t = jnp.where(keep, t, -jnp.inf)
                    m_old, cq = m_s[e, rs, :], cq_s[e, rs, :]
                    m_new = jnp.maximum(m_old, jnp.max(t, axis=1, keepdims=True) + cq)
                    p = jnp.exp2(t - _wide(m_new - cq, blk))
                    alpha = jnp.exp2(m_old - m_new)
                    l_s[e, rs, :] = alpha * l_s[e, rs, :] + jnp.sum(p, axis=1, keepdims=True)
                    m_s[e, rs, :] = m_new
                    al_s[e, rs, :] = alpha
                    pb_s[e, rs, :] = p.astype(BF16)
            pv = jnp.where(half == 0, _dot(pb_s[0], v_v), _dot(pb_s[1], v_v))
            acc_s[...] = jnp.where(half == 0, al_s[0], al_s[1]) * acc_s[...] + pv

        @pl.when((j < i) & live)
        def _():
            step(False)

        @pl.when(j == i)
        def _():
            step(True)
            o_ref[...] = (acc_s[...] / jnp.where(half == 0, l_s[0], l_s[1])).astype(BF16)
            rows = pl.ds(pl.multiple_of(i * blk, blk), blk)
            lse_ref[0, :, rows] = jnp.concatenate(
                [_col_to_row(m_s[e] + jnp.log(l_s[e]) * LOG2E) for e in range(2)], axis=0)

    def col_block(first, inner):
        def index(p, n, it_r, jt_r, *_):
            return ((jt_r[n] if inner else it_r[n]), first + p)
        return pl.BlockSpec((blk, LANES), index)

    pair_stat = (2, blk, LANES)
    grid_spec = pltpu.PrefetchScalarGridSpec(
        num_scalar_prefetch=6, grid=(n_pair, int(it.shape[0])),
        in_specs=[col_block(0, False), col_block(n_pair, True), col_block(2 * n_pair, True),
                  pl.BlockSpec((blk, LANES), lambda p, n, it_r, jt_r, *_: (it_r[n], 0)),
                  pl.BlockSpec((2, 1, blk), lambda p, n, it_r, jt_r, *_: (p, 0, jt_r[n]))],
        out_specs=[col_block(0, False), pl.BlockSpec((1, 2, n_tok), lambda p, n, it_r, jt_r, *_: (p, 0, 0))],
        scratch_shapes=[pltpu.VMEM(pair_stat, F32), pltpu.VMEM(pair_stat, F32), pltpu.VMEM((blk, LANES), F32),
                        pltpu.VMEM(pair_stat, F32), pltpu.VMEM(pair_stat, BF16), pltpu.VMEM(pair_stat, F32),
                        pltpu.VMEM((2, blk, blk), F32), pltpu.VMEM((2, blk, blk), BF16)],
    )
    return pl.pallas_call(
        kern, name=name, grid_spec=grid_spec,
        out_shape=[jax.ShapeDtypeStruct((n_tok, FOX_W), BF16), jax.ShapeDtypeStruct((n_pair, 2, n_tok), F32)],
        compiler_params=_params(2),
    )(it, jt, *tables, qkv, qkv, qkv, c2, ct2)


def _flash_delta(name, do, o):
    n_tok, tm = do.shape[0], min(ROW_BLOCK, do.shape[0])

    def body(i, n, do_ref, o_ref, d_ref):
        prod = do_ref[...].astype(F32) * o_ref[...].astype(F32)
        head = (lax.broadcasted_iota(jnp.int32, (FOX_W, LANES), 0) // FOX_D
                == lax.broadcasted_iota(jnp.int32, (FOX_W, LANES), 1)).astype(BF16)
        hi = prod.astype(BF16)
        r1 = prod - hi.astype(F32)
        mid = r1.astype(BF16)
        lo = (r1 - mid.astype(F32)).astype(BF16)
        d_ref[...] = _dot(hi, head) + _dot(mid, head) + _dot(lo, head)

    return _rowcall(name, body, n_tok, tm, [do, o], [], [], [(LANES, F32)], [])[0]


def _flash_bwd(name, qkv, do, r_row, d_row, c2, tables):
    n_tok = qkv.shape[0]
    blk = min(FLASH_BLOCK, n_tok)
    nb = n_tok // blk
    n_pair = FOX_H // 2
    jt, it = _tri_table(nb, False)

    def kern(jt_ref, it_ref, qn_ref, kn_ref, cf_ref, cl_ref, k_ref, v_ref, q_ref, do_ref, rr_ref, dr_ref, c_ref,
             dq_ref, dk_ref, dv_ref, dc_ref, km_s, vm_s, ck_s, dck_s):
        n = pl.program_id(1)
        j, i = jt_ref[n], it_ref[n]
        first = _pair_half(blk) == 0
        live = _block_is_live((qn_ref, kn_ref, cf_ref, cl_ref), pl.program_id(0), i, j)

        @pl.when(n == 0)
        def _():
            dq_ref[...] = jnp.zeros(dq_ref.shape, F32)
            dc_ref[...] = jnp.zeros(dc_ref.shape, F32)

        @pl.when(i == j)
        def _():
            dk_ref[...] = jnp.zeros(dk_ref.shape, F32)
            dv_ref[...] = jnp.zeros(dv_ref.shape, F32)
            dck_s[...] = jnp.zeros(dck_s.shape, F32)
            for e in range(2):
                mine = first if e == 0 else jnp.logical_not(first)
                km_s[e] = jnp.where(mine, k_ref[...], jnp.zeros_like(k_ref[...]))
                vm_s[e] = jnp.where(mine, v_ref[...], jnp.zeros_like(v_ref[...]))
                ck_s[e] = _lane_pick(c_ref[...], 2 * pl.program_id(0) + e)

        def step(on_diagonal):
            q_v, do_v, k_v = q_ref[...], do_ref[...], k_ref[...]
            dvs, dks, dqs, sums = [], [], [], []
            for e in range(2):
                t = _dot_nt(km_s[e], q_v) * (FOX_SCALE * LOG2E) - _wide(ck_s[e], blk) - rr_ref[e]
                if on_diagonal:
                    keep = (lax.broadcasted_iota(jnp.int32, (blk, blk), 1)
                            >= lax.broadcasted_iota(jnp.int32, (blk, blk), 0))
                    t = jnp.where(keep, t, -jnp.inf)
                p = jnp.exp2(t)
                ds = p * (_dot_nt(vm_s[e], do_v) - dr_ref[e])
                p_b, ds_b = p.astype(BF16), ds.astype(BF16)
                dvs.append(_dot(p_b, do_v))
                dks.append(_dot(ds_b, q_v))
                dqs.append(_dot_tn(ds_b, k_v))
                dck_s[e] += jnp.broadcast_to(jnp.sum(ds, axis=1, keepdims=True), (blk, LANES))
                sums.append(jnp.sum(ds, axis=0, keepdims=True))
            rows = pl.ds(pl.multiple_of(i * blk, blk), blk)
            dv_ref[...] += jnp.where(first, dvs[0], dvs[1])
            dk_ref[...] += jnp.where(first, dks[0], dks[1]) * FOX_SCALE
            dq_ref[rows, :] += jnp.where(first, dqs[0], dqs[1]) * FOX_SCALE
            dc_ref[0, :, rows] += jnp.concatenate(sums, axis=0)

        @pl.when((i > j) & live)
        def _():
            step(False)

        @pl.when(i == j)
        def _():
            step(True)

        @pl.when(i == nb - 1)
        def _():
            keys = pl.ds(pl.multiple_of(j * blk, blk), blk)
            dc_ref[0, :, keys] -= jnp.concatenate([_col_to_row(dck_s[e]) for e in range(2)], axis=0)

    def col_block(first_col, inner):
        def index(p, n, jt_r, it_r, *_):
            return ((it_r[n] if inner else jt_r[n]), first_col + p)
        return pl.BlockSpec((blk, LANES), index)

    def row_stat():
        return pl.BlockSpec((2, 1, blk), lambda p, n, jt_r, it_r, *_: (p, 0, it_r[n]))

    pair_stat = (2, blk, LANES)
    grid_spec = pltpu.PrefetchScalarGridSpec(
        num_scalar_prefetch=6, grid=(n_pair, int(jt.shape[0])),
        in_specs=[col_block(n_pair, False), col_block(2 * n_pair, False), col_block(0, True), col_block(0, True),
                  row_stat(), row_stat(), pl.BlockSpec((blk, LANES), lambda p, n, jt_r, it_r, *_: (jt_r[n], 0))],
        out_specs=[pl.BlockSpec((n_tok, LANES), lambda p, n, jt_r, it_r, *_: (0, p)),
                   col_block(0, False), col_block(0, False),
                   pl.BlockSpec((1, 2, n_tok), lambda p, n, jt_r, it_r, *_: (p, 0, 0))],
        scratch_shapes=[pltpu.VMEM(pair_stat, BF16), pltpu.VMEM(pair_stat, BF16), pltpu.VMEM(pair_stat, F32),
                        pltpu.VMEM(pair_stat, F32)],
    )
    wide = jax.ShapeDtypeStruct((n_tok, FOX_W), F32)
    return pl.pallas_call(
        kern, name=name, grid_spec=grid_spec,
        out_shape=[wide, wide, wide, jax.ShapeDtypeStruct((n_pair, 2, n_tok), F32)],
        compiler_params=_params(2),
    )(jt, it, *tables, qkv, qkv, qkv, do, r_row, d_row, c2)


def _merge_fwd(name, x, ya, yb, yc, gl, wb, wo):
    n_tok, tm = x.shape[0], min(ROW_BLOCK, x.shape[0])

    def body(i, n, x_ref, ya_ref, yb_ref, yc_ref, gl_ref, wb_ref, wo_ref, xo_ref):
        merged = jnp.zeros((tm, D_MODEL), F32)
        for k, y_ref in enumerate((ya_ref, yb_ref, yc_ref)):
            gate = jax.nn.sigmoid(gl_ref[:, k * D_MODEL:(k + 1) * D_MODEL].astype(F32))
            merged = merged + gate * _dot(y_ref[...], wb_ref[k])
        xo_ref[...] = x_ref[...] + _dot(merged.astype(BF16), wo_ref[...])

    return _rowcall(name, body, n_tok, tm, [x, ya, yb, yc, gl], [], [wb, wo], [(D_MODEL, F32)], [])[0]


def _merge_bwd(name, dxo, ya, yb, yc, gl, wb, wb_t, wo_t):
    n_tok, tm = dxo.shape[0], min(ROW_BLOCK, dxo.shape[0])

    def body(i, n, dxo_ref, ya_ref, yb_ref, yc_ref, gl_ref, wb_ref, wbt_ref, wot_ref,
             mg_ref, dob_ref, dgl_ref, dbr_ref, dya_ref, dyb_ref, dyc_ref):
        dob = dxo_ref[...].astype(BF16)
        dob_ref[...] = dob
        dm = _dot(dob, wot_ref[...])
        merged = jnp.zeros((tm, D_MODEL), F32)
        for k, (y_ref, dy_ref) in enumerate(((ya_ref, dya_ref), (yb_ref, dyb_ref), (yc_ref, dyc_ref))):
            cols = slice(k * D_MODEL, (k + 1) * D_MODEL)
            gate = jax.nn.sigmoid(gl_ref[:, cols].astype(F32))
            br = _dot(y_ref[...], wb_ref[k])
            merged = merged + gate * br
            dgl_ref[:, cols] = (dm * br * gate * (1.0 - gate)).astype(BF16)
            dbr = (dm * gate).astype(BF16)
            dbr_ref[:, cols] = dbr
            dy_ref[...] = _dot(dbr, wbt_ref[k]).astype(BF16)
        mg_ref[...] = merged.astype(BF16)

    return _rowcall(name, body, n_tok, tm, [dxo, ya, yb, yc, gl], [], [wb, wb_t, wo_t],
                    [(D_MODEL, BF16), (D_MODEL, BF16), (3 * D_MODEL, BF16), (3 * D_MODEL, BF16),
                     (CONV_W, BF16), (SG_W, BF16), (FOX_W, BF16)], [])


def _mem_fwd(name, mem, gain, wk, wv):
    n_mem = mem.shape[0]

    def kern(m_ref, g_ref, wk_ref, wv_ref, mn_ref, k_ref, v_ref):
        mh, _ = _rms_stats(m_ref[...])
        mn = (mh * g_ref[...]).astype(BF16)
        mn_ref[...] = mn
        k_ref[...] = _dot(mn, wk_ref[...]).astype(BF16)
        v_ref[...] = _dot(mn, wv_ref[...]).astype(BF16)

    shp = jax.ShapeDtypeStruct((n_mem, D_MODEL), BF16)
    return pl.pallas_call(
        kern, name=name, grid=(1,),
        in_specs=[_full_spec(mem.shape), _full_spec(gain.shape), _full_spec(wk.shape), _full_spec(wv.shape)],
        out_specs=[_full_spec(shp.shape)] * 3, out_shape=[shp] * 3, compiler_params=_params(1),
    )(mem, gain, wk, wv)


def _mem_bwd(name, mem, gain, dkx, dvx, wk_t, wv_t):
    n_mem = mem.shape[0]

    def kern(m_ref, g_ref, dk_ref, dv_ref, wkt_ref, wvt_ref, dkb_ref, dvb_ref, dg_ref):
        mh, _ = _rms_stats(m_ref[...])
        dkb, dvb = dk_ref[...].astype(BF16), dv_ref[...].astype(BF16)
        dkb_ref[...] = dkb
        dvb_ref[...] = dvb
        dm = _dot(dkb, wkt_ref[...]) + _dot(dvb, wvt_ref[...])
        dg_ref[...] = jnp.sum(dm * mh, axis=0, keepdims=True)

    shp = jax.ShapeDtypeStruct((n_mem, D_MODEL), BF16)
    args = (mem, gain, dkx, dvx, wk_t, wv_t)
    return pl.pallas_call(
        kern, name=name, grid=(1,), in_specs=[_full_spec(a.shape) for a in args],
        out_specs=[_full_spec(shp.shape)] * 2 + [_full_spec((1, D_MODEL))],
        out_shape=[shp, shp, jax.ShapeDtypeStruct((1, D_MODEL), F32)], compiler_params=_params(1),
    )(*args)


def _xa_probs(q_b, kx_ref, hd):
    cols = slice(hd * XA_D, (hd + 1) * XA_D)
    s = _dot_nt(q_b[:, cols], kx_ref[:, cols]) * (XA_D ** -0.5)
    e = jnp.exp(s - jnp.max(s, axis=1, keepdims=True))
    return e / jnp.sum(e, axis=1, keepdims=True)


def _xa_fwd(name, x, gain, kx, vx, wq, wo):
    n_tok, tm = x.shape[0], min(ROW_BLOCK, x.shape[0])

    def body(i, n, x_ref, g_ref, kx_ref, vx_ref, wq_ref, wo_ref, xo_ref):
        x_v = x_ref[...]
        xh, _ = _rms_stats(x_v)
        q_b = _dot((xh * g_ref[...]).astype(BF16), wq_ref[...]).astype(BF16)
        o = jnp.concatenate(
            [_dot(_xa_probs(q_b, kx_ref, hd).astype(BF16), vx_ref[:, hd * XA_D:(hd + 1) * XA_D]) for hd in range(XA_H)],
            axis=1)
        xo_ref[...] = x_v + _dot(o.astype(BF16), wo_ref[...])

    return _rowcall(name, body, n_tok, tm, [x], [gain, kx, vx], [wq, wo], [(D_MODEL, F32)], [])[0]


def _xa_bwd(name, x, dxo, gain, kx, vx, wq, wq_t, wo_t):
    n_tok, tm = x.shape[0], min(ROW_BLOCK, x.shape[0])
    n_mem = kx.shape[0]

    def body(i, n, x_ref, dxo_ref, g_ref, kx_ref, vx_ref, wq_ref, wqt_ref, wot_ref,
             dx_ref, h_ref, o_ref, dq_ref, dy_ref, dkx_ref, dvx_ref, dg_ref):
        g = g_ref[...]
        xh, r = _rms_stats(x_ref[...])
        h = (xh * g).astype(BF16)
        h_ref[...] = h
        q_b = _dot(h, wq_ref[...]).astype(BF16)
        dxo_v = dxo_ref[...]
        dy = dxo_v.astype(BF16)
        dy_ref[...] = dy
        do = _dot(dy, wot_ref[...])
        for hd in range(XA_H):
            cols = slice(hd * XA_D, (hd + 1) * XA_D)
            p = _xa_probs(q_b, kx_ref, hd)
            p_b = p.astype(BF16)
            o_ref[:, cols] = _dot(p_b, vx_ref[:, cols]).astype(BF16)
            do_h = do[:, cols].astype(BF16)
            dvx_ref[:, cols] += _dot_tn(p_b, do_h)
            dp = _dot_nt(do_h, vx_ref[:, cols])
            ds = p * (dp - jnp.sum(dp * p, axis=1, keepdims=True))
            ds_b = (ds * (XA_D ** -0.5)).astype(BF16)
            dq_ref[:, cols] = _dot(ds_b, kx_ref[:, cols]).astype(BF16)
            dkx_ref[:, cols] += _dot_tn(ds_b, q_b[:, cols])
        dh = _dot(dq_ref[...], wqt_ref[...])
        dx, dg = _rms_bwd(xh, r, g, dh)
        dx_ref[...] = dxo_v + dx
        dg_ref[...] += dg

    return _rowcall(name, body, n_tok, tm, [x, dxo], [gain, kx, vx], [wq, wq_t, wo_t],
                    [(D_MODEL, F32), (D_MODEL, BF16), (D_MODEL, BF16), (D_MODEL, BF16), (D_MODEL, BF16)],
                    [((n_mem, D_MODEL), F32), ((n_mem, D_MODEL), F32), ((1, D_MODEL), F32)])


def _loss_head(name, x, target, gain):
    n_tok, tm = x.shape[0], min(ROW_BLOCK, x.shape[0])

    def body(i, n, x_ref, t_ref, g_ref, dx_ref, loss_ref, dg_ref):
        g = g_ref[...]
        xh, r = _rms_stats(x_ref[...])
        err = xh * g - t_ref[...]
        loss_ref[...] += 0.5 * jnp.sum(jnp.sum(err * err, axis=1, keepdims=True) / D_MODEL, axis=0, keepdims=True)
        dx, dg = _rms_bwd(xh, r, g, err / D_MODEL)
        dx_ref[...] = dx
        dg_ref[...] += dg

    return _rowcall(name, body, n_tok, tm, [x, target], [gain], [], [(D_MODEL, F32)],
                    [((8, LANES), F32), ((1, D_MODEL), F32)])


def _adamw(name, w, g, m, v):
    rows, cols = w.shape
    tr = 128 if rows % 128 == 0 else rows

    def kern(w_ref, g_ref, m_ref, v_ref, d_ref, mo_ref, vo_ref):
        g_v = g_ref[...]
        m_new = ADAM_B1 * m_ref[...] + (1.0 - ADAM_B1) * g_v
        v_new = ADAM_B2 * v_ref[...] + (1.0 - ADAM_B2) * (g_v * g_v)
        m_hat = m_new / (1.0 - ADAM_B1 ** ADAM_STEP)
        v_hat = v_new / (1.0 - ADAM_B2 ** ADAM_STEP)
        d_ref[...] = -ADAM_LR * (m_hat / (jnp.sqrt(v_hat) + ADAM_EPS) + ADAM_WD * w_ref[...])
        mo_ref[...] = m_new
        vo_ref[...] = v_new

    spec = pl.BlockSpec((tr, cols), lambda i: (i, 0))
    shp = jax.ShapeDtypeStruct((rows, cols), F32)
    return pl.pallas_call(kern, name=name, grid=(rows // tr,), in_specs=[spec] * 4, out_specs=[spec] * 3,
                          out_shape=[shp] * 3, compiler_params=_params(1))(w, g, m, v)


def _add_pair(name, gp, got, core):
    _, n, rows, cols = gp.shape
    spec = pl.BlockSpec((1, PACK_ROW_BLOCK, cols), lambda j, r, core_ref: (j, r, 0))
    mine = pl.BlockSpec((1, PACK_ROW_BLOCK, cols), lambda j, r, core_ref: (core_ref[0] * n + j, r, 0))

    def kern(core_ref, a_ref, b_ref, o_ref):
        o_ref[...] = (a_ref[...] + b_ref[...]).astype(BF16)

    grid_spec = pltpu.PrefetchScalarGridSpec(num_scalar_prefetch=1, grid=(n, rows // PACK_ROW_BLOCK),
                                             in_specs=[mine, spec], out_specs=spec)
    return pl.pallas_call(kern, name=name, grid_spec=grid_spec, out_shape=jax.ShapeDtypeStruct(got.shape, BF16),
                          compiler_params=_params(2))(core, gp.reshape(2 * n, rows, cols), got)


def _sum_parts(name, s, got, chip):
    _, rows, cols = s.shape
    tr = PACK_ROW_BLOCK

    def kern(chip_ref, s_ref, g_ref, o_ref):
        acc = s_ref[0].astype(F32)
        for k in range(3):
            acc = acc + g_ref[k].astype(F32)
        o_ref[...] = acc

    grid_spec = pltpu.PrefetchScalarGridSpec(
        num_scalar_prefetch=1, grid=(rows // tr,),
        in_specs=[pl.BlockSpec((1, tr, cols), lambda r, chip_ref: (chip_ref[0], r, 0)),
                  pl.BlockSpec((3, tr, cols), lambda r, chip_ref: (0, r, 0))],
        out_specs=pl.BlockSpec((tr, cols), lambda r, chip_ref: (r, 0)))
    return pl.pallas_call(kern, name=name, grid_spec=grid_spec, out_shape=jax.ShapeDtypeStruct((rows, cols), F32),
                          compiler_params=_params(1))(chip, s, got)


def _sum_slots(name, a, tr):
    n, rows, cols = a.shape

    def kern(a_ref, o_ref):
        acc = a_ref[0].astype(F32)
        for k in range(1, n):
            acc = acc + a_ref[k].astype(F32)
        o_ref[...] = acc

    return pl.pallas_call(kern, name=name, grid=(rows // tr,),
                          in_specs=[pl.BlockSpec((n, tr, cols), lambda r: (0, r, 0))],
                          out_specs=pl.BlockSpec((tr, cols), lambda r: (r, 0)),
                          out_shape=jax.ShapeDtypeStruct((rows, cols), F32), compiler_params=_params(1))(a)


_ANY = pl.BlockSpec(memory_space=pl.ANY)


COMM_CHUNKS = 8


def _place():
    x, y, c = lax.axis_index("x"), lax.axis_index("y"), lax.axis_index("c")
    chips = [(1 - x, y), (x, 1 - y), (1 - x, 1 - y)]
    return x, y, c, chips


def _rows(ref, q, n):
    step = ref.shape[0] // n
    return ref.at[pl.ds(q * step, step)]


def _gather_chips(name, w):
    n = COMM_CHUNKS

    def kern(w_ref, o_ref, send_sems, recv_sems):
        x, y, c, chips = _place()
        me, sib = 2 * x + y, (x, y, 1 - c)

        def copy(k, src, dst, to):
            return pltpu.make_async_remote_copy(src_ref=src, dst_ref=dst, send_sem=send_sems.at[k],
                                                recv_sem=recv_sems.at[k], device_id=to, device_id_type=MESH_ID)

        first = [copy(k * n + q, _rows(w_ref.at[c], q, n), _rows(o_ref.at[me, c], q, n), (cx, cy, c))
                 for k, (cx, cy) in enumerate(chips) for q in range(n)]
        for cp in first:
            cp.start()
        passed = []
        for k, (cx, cy) in enumerate(chips):
            for q in range(n):
                landed = _rows(o_ref.at[2 * cx + cy, c], q, n)
                copy(k * n + q, landed, landed, (x, y, c)).wait_recv()
                passed.append(copy((3 + k) * n + q, landed, landed, sib))
                passed[-1].start()
        for k, (cx, cy) in enumerate(chips):
            for q in range(n):
                theirs = _rows(o_ref.at[2 * cx + cy, 1 - c], q, n)
                copy((3 + k) * n + q, theirs, theirs, (x, y, c)).wait_recv()
        for cp in first + passed:
            cp.wait_send()

    return pl.pallas_call(
        kern, name=name, in_specs=[_ANY], out_specs=_ANY,
        out_shape=jax.ShapeDtypeStruct((N_CHIPS,) + w.shape, w.dtype),
        scratch_shapes=[pltpu.SemaphoreType.DMA((6 * n,)), pltpu.SemaphoreType.DMA((6 * n,))],
        compiler_params=pltpu.CompilerParams(has_side_effects=True),
    )(w)


def _gather_all(name, v):
    def kern(v_ref, o_ref, send_sems, recv_sems, local_sem):
        x, y, c, _ = _place()
        me = 4 * x + 2 * y + c
        mine = pltpu.make_async_copy(v_ref, o_ref.at[me], local_sem)
        mine.start()
        peers = []
        for k in range(1, 8):
            px = 1 - x if k & 4 else x
            py = 1 - y if k & 2 else y
            pc = 1 - c if k & 1 else c
            peers.append((px, py, pc))
        copies = [pltpu.make_async_remote_copy(src_ref=v_ref, dst_ref=o_ref.at[me], send_sem=send_sems.at[k],
                                               recv_sem=recv_sems.at[k], device_id=p, device_id_type=MESH_ID)
                  for k, p in enumerate(peers)]
        for cp in copies:
            cp.start()
        for k, (px, py, pc) in enumerate(peers):
            pltpu.make_async_remote_copy(src_ref=v_ref, dst_ref=o_ref.at[4 * px + 2 * py + pc], send_sem=send_sems.at[k],
                                         recv_sem=recv_sems.at[k], device_id=(x, y, c), device_id_type=MESH_ID).wait_recv()
        for cp in copies:
            cp.wait_send()
        mine.wait()

    return pl.pallas_call(
        kern, name=name, in_specs=[_ANY], out_specs=_ANY, out_shape=jax.ShapeDtypeStruct((8,) + v.shape, v.dtype),
        scratch_shapes=[pltpu.SemaphoreType.DMA((7,)), pltpu.SemaphoreType.DMA((7,)), pltpu.SemaphoreType.DMA(())],
        compiler_params=pltpu.CompilerParams(has_side_effects=True),
    )(v)


def _swap_halves(name, gp):
    n = COMM_CHUNKS
    n_slots = gp.shape[1]

    def kern(g_ref, got_ref, send_sems, recv_sems):
        x, y, c, _ = _place()
        swaps = [pltpu.make_async_remote_copy(src_ref=_rows(g_ref.at[1 - c, j], q, n), dst_ref=_rows(got_ref.at[j], q, n),
                                              send_sem=send_sems.at[j * n + q], recv_sem=recv_sems.at[j * n + q],
                                              device_id=(x, y, 1 - c), device_id_type=MESH_ID)
                 for j in range(n_slots) for q in range(n)]
        for cp in swaps:
            cp.start()
        for cp in swaps:
            cp.wait()

    return pl.pallas_call(
        kern, name=name, in_specs=[_ANY], out_specs=_ANY, out_shape=jax.ShapeDtypeStruct(gp.shape[1:], gp.dtype),
        scratch_shapes=[pltpu.SemaphoreType.DMA((n_slots * n,)), pltpu.SemaphoreType.DMA((n_slots * n,))],
        compiler_params=pltpu.CompilerParams(has_side_effects=True),
    )(gp)


def _scatter_chips(name, s):
    n = COMM_CHUNKS

    def kern(s_ref, o_ref, send_sems, recv_sems):
        x, y, c, chips = _place()
        copies = [pltpu.make_async_remote_copy(src_ref=_rows(s_ref.at[2 * cx + cy], q, n), dst_ref=_rows(o_ref.at[k], q, n),
                                               send_sem=send_sems.at[k * n + q], recv_sem=recv_sems.at[k * n + q],
                                               device_id=(cx, cy, c), device_id_type=MESH_ID)
                  for k, (cx, cy) in enumerate(chips) for q in range(n)]
        for cp in copies:
            cp.start()
        for k in range(3):
            for q in range(n):
                landed = _rows(o_ref.at[k], q, n)
                pltpu.make_async_remote_copy(src_ref=landed, dst_ref=landed, send_sem=send_sems.at[k * n + q],
                                             recv_sem=recv_sems.at[k * n + q], device_id=(x, y, c),
                                             device_id_type=MESH_ID).wait_recv()
        for cp in copies:
            cp.wait_send()

    return pl.pallas_call(
        kern, name=name, in_specs=[_ANY], out_specs=_ANY, out_shape=jax.ShapeDtypeStruct((3,) + s.shape[1:], s.dtype),
        scratch_shapes=[pltpu.SemaphoreType.DMA((3 * n,)), pltpu.SemaphoreType.DMA((3 * n,))],
        compiler_params=pltpu.CompilerParams(has_side_effects=True),
    )(s)


def _join_halves(name, r):
    n = COMM_CHUNKS

    def kern(r_ref, o_ref, send_sems, recv_sems):
        x, y, c, _ = _place()
        swaps = [pltpu.make_async_remote_copy(src_ref=_rows(r_ref, q, n), dst_ref=_rows(o_ref, q, n),
                                              send_sem=send_sems.at[q], recv_sem=recv_sems.at[q],
                                              device_id=(x, y, 1 - c), device_id_type=MESH_ID) for q in range(n)]
        for cp in swaps:
            cp.start()
        for cp in swaps:
            cp.wait()

    return pl.pallas_call(
        kern, name=name, in_specs=[_ANY], out_specs=_ANY, out_shape=jax.ShapeDtypeStruct(r.shape, r.dtype),
        scratch_shapes=[pltpu.SemaphoreType.DMA((n,))] * 2, compiler_params=pltpu.CompilerParams(has_side_effects=True),
    )(r)


PACK_ALIGN = 16


def _exchange_rows(nm, shard_shape):
    if BIG_KIND[nm] == "cols":
        rows = shard_shape[-1]
    else:
        rows = 1
        for s in shard_shape[1:]:
            rows *= s
        rows //= PACK_COLS
    return -(-rows // PACK_ALIGN) * PACK_ALIGN


def _to_exchange(nm, a, transposed=False):
    if BIG_KIND[nm] == "cols":
        if not transposed:
            a = jnp.swapaxes(a, 1, 2)
        rows = -(-a.shape[1] // PACK_ALIGN) * PACK_ALIGN
        a = jnp.pad(a, ((0, 0), (0, rows - a.shape[1]), (0, 0)))
        return a.reshape(a.shape[0] * rows, PACK_COLS)
    return a.reshape(a.shape[0] * _exchange_rows(nm, a.shape), PACK_COLS)


def _from_exchange(nm, rows2d, shard_shape):
    lead = rows2d.shape[:-2]
    n_layers = shard_shape[0]
    if BIG_KIND[nm] == "cols":
        a = rows2d.reshape(lead + (n_layers, -1, PACK_COLS))
        return lax.slice_in_dim(a, 0, shard_shape[-1], axis=a.ndim - 2)
    return rows2d.reshape(lead + tuple(shard_shape))


def _pack(pieces, dtype):
    slab = jnp.concatenate([p.astype(dtype) for p in pieces])
    half = -(-slab.shape[0] // (2 * PACK_ROW_BLOCK)) * PACK_ROW_BLOCK
    slab = jnp.pad(slab, ((0, 2 * half - slab.shape[0]), (0, 0)))
    return slab.reshape(2, half, PACK_COLS)


def _unpack(packed, row_counts):
    lead = packed.shape[:-3]
    slab = packed.reshape(lead + (-1, PACK_COLS))
    out, off = [], 0
    for rows in row_counts:
        out.append(lax.slice_in_dim(slab, off, off + rows, axis=slab.ndim - 2))
        off += rows
    return out


def _t(a):
    return jnp.swapaxes(a, -1, -2)


def _rows_from_lanes(a):
    return a[:, :FOX_H].T[:, None, :]


def _lanes_from_heads(a):
    return jnp.pad(a[:, 0, :].T, ((0, 0), (0, LANES - FOX_H)))


def _layer_weights(wts, small, l):
    w_in_t = wts["w_in"][l]
    w = {
        "wf_t": jnp.pad(w_in_t[W1_COLS:W1_COLS + FOX_H], ((0, LANES - FOX_H), (0, 0))),
        "w1_t": w_in_t[:W1_COLS], "wgl_t": w_in_t[W1_COLS + FOX_H:],
    }
    for nm, _ in BIG:
        if nm != "w_in":
            w[nm + "_t" if BIG_KIND[nm] == "cols" else nm] = wts[nm][l]
    for nm in list(w):
        other = nm[:-2] if nm.endswith("_t") else nm + "_t"
        w[other] = _t(w[nm])
    for nm in ("ffn1_norm", "mix_norm", "sg_norm", "xa_norm", "mem_norm", "ffn2_norm"):
        w[nm] = small[nm][l][None, :]
    w["conv_w"] = jnp.pad(small["conv_w"][l], ((0, 5), (0, 0)))
    w["sg_w"] = small["sg_w"][l]
    w["sg_w_t"] = _t(small["sg_w"][l])
    w["sg_bias"] = jnp.repeat(small["sg_b"][l].T, CHUNK, axis=1)
    w["fox_b"] = jnp.pad(small["fox_b_f"][l][None, :], ((0, 0), (0, LANES - FOX_H)))
    return w


def _layer_fwd(l, x, mem, w):
    s = {"x0": x}
    x1, s["a1"], s["b1"] = _ffn_fwd(f"ffn1_fwd_{l}", x, w["ffn1_norm"], w["ffn1_w_gate"], w["ffn1_w_up"], w["ffn1_w_down"])
    s["x1"] = x1
    cin, sgin, qkv, fl, gl = _proj_fwd(f"proj_fwd_{l}", x1, w["mix_norm"], w["w1"], w["wf"], w["wgl"])
    ya = _conv_fwd(f"conv_fwd_{l}", cin, w["conv_w"])
    yb = _sg_fwd(f"sg_fwd_{l}", sgin, w["sg_norm"], w["sg_w"], w["sg_bias"])
    c2 = _forget_cumsum(f"forget_fwd_{l}", fl, w["fox_b"])
    tables = _prune_tables(_flash_bounds(f"flash_bounds_{l}", qkv), c2, min(FLASH_BLOCK, qkv.shape[0]))
    yc, lse2 = _flash_fwd(f"flash_fwd_{l}", qkv, c2, _rows_from_lanes(c2), tables)
    x2 = _merge_fwd(f"merge_fwd_{l}", x1, ya, yb, yc, gl, w["w_branch"], w["w_out"])
    s.update(cin=cin, sgin=sgin, qkv=qkv, fl=fl, gl=gl, ya=ya, yb=yb, yc=yc, c2=c2, lse2=lse2, x2=x2, tables=tables)
    s["mn"], s["kx"], s["vx"] = _mem_fwd(f"mem_fwd_{l}", mem, w["mem_norm"], w["xa_wk"], w["xa_wv"])
    x3 = _xa_fwd(f"xa_fwd_{l}", x2, w["xa_norm"], s["kx"], s["vx"], w["xa_wq"], w["xa_wo"])
    s["x3"] = x3
    x4, s["a2"], s["b2"] = _ffn_fwd(f"ffn2_fwd_{l}", x3, w["ffn2_norm"], w["ffn2_w_gate"], w["ffn2_w_up"], w["ffn2_w_down"])
    return x4, s


def _ffn_grads(tag, x, dxo, a, b, w, pre):
    dx, da, db, sv, h, dy, dg = _ffn_bwd(f"{pre}_bwd_{tag}", x, dxo, a, b, w[pre + "_norm"], w[pre + "_w_down_t"],
                                         w[pre + "_w_gate_t"], w[pre + "_w_up_t"])
    g = {
        pre + "_norm": dg[0],
        pre + "_w_gate": _mm_tn(f"{pre}_dwg_{tag}", da, h, D_FF, D_MODEL, tk=FF_SPLIT, tn=D_MODEL),
        pre + "_w_up": _mm_tn(f"{pre}_dwu_{tag}", db, h, D_FF, D_MODEL, tk=FF_SPLIT, tn=D_MODEL),
        pre + "_w_down": _mm_tn(f"{pre}_dwd_{tag}", sv, dy, D_FF, D_MODEL, tk=FF_SPLIT, tn=D_MODEL),
    }
    return dx, g


def _layer_bwd(l, dx, mem, w, s):
    g = {}
    dx, gf = _ffn_grads(l, s["x3"], dx, s["a2"], s["b2"], w, "ffn2")
    g.update(gf)

    dx, h, o, dq, dy, dkx, dvx, dg = _xa_bwd(f"xa_bwd_{l}", s["x2"], dx, w["xa_norm"], s["kx"], s["vx"], w["xa_wq"],
                                              w["xa_wq_t"], w["xa_wo_t"])
    g["xa_norm"] = dg[0]
    g["xa_wq"] = _mm_tn(f"xa_dwq_{l}", h, dq, D_MODEL, D_MODEL, tk=D_MODEL, tn=D_MODEL)
    g["xa_wo"] = _mm_tn(f"xa_dwo_{l}", o, dy, D_MODEL, D_MODEL, tk=D_MODEL, tn=D_MODEL)
    dkb, dvb, dgm = _mem_bwd(f"mem_bwd_{l}", mem, w["mem_norm"], dkx, dvx, w["xa_wk_t"], w["xa_wv_t"])
    g["mem_norm"] = dgm[0]
    g["xa_wk"] = _mm_tn(f"xa_dwk_{l}", s["mn"], dkb, D_MODEL, D_MODEL, tk=D_MODEL, tn=D_MODEL)
    g["xa_wv"] = _mm_tn(f"xa_dwv_{l}", s["mn"], dvb, D_MODEL, D_MODEL, tk=D_MODEL, tn=D_MODEL)

    mg, dob, dgl, dbr, dya, dyb, dyc = _merge_bwd(f"merge_bwd_{l}", dx, s["ya"], s["yb"], s["yc"], s["gl"], w["w_branch"],
                                                  w["w_branch_t"], w["w_out_t"])
    g["w_out"] = _mm_tn(f"dwout_{l}", mg, dob, D_MODEL, D_MODEL, tk=D_MODEL, tn=D_MODEL)
    g["w_branch"] = jnp.stack([
        _mm_tn(f"dwbranch{k}_{l}", y, dbr, CONV_W, D_MODEL, tk=CONV_W, tn=D_MODEL, y_off=k * D_MODEL)
        for k, y in enumerate((s["ya"], s["yb"], s["yc"]))])

    dcin, dcw = _conv_bwd(f"conv_bwd_{l}", s["cin"], dya, w["conv_w"])
    g["conv_w"] = dcw[:3]
    dsg, dgn, dsw, dsb = _sg_bwd(f"sg_bwd_{l}", s["sgin"], dyb, w["sg_norm"], w["sg_w"], w["sg_w_t"], w["sg_bias"])
    g["sg_norm"], g["sg_w"], g["sg_b"] = dgn[0], dsw, dsb[:, :, 0]
    d_row = _rows_from_lanes(_flash_delta(f"flash_delta_{l}", dyc, s["yc"]))
    r_row = s["lse2"].reshape(FOX_H, 1, -1) - _rows_from_lanes(s["c2"])
    dq, dk, dv, dc_rows = _flash_bwd(f"flash_bwd_{l}", s["qkv"], dyc, r_row, d_row, s["c2"], s["tables"])
    dc = _lanes_from_heads(dc_rows.reshape(FOX_H, 1, -1))
    dfl, dbf = _forget_bwd(f"forget_bwd_{l}", dc, s["fl"], w["fox_b"])
    g["fox_b_f"] = dbf[0, :FOX_H]

    dx, h, dqkv, dfb, dg = _proj_bwd(f"proj_bwd_{l}", s["x1"], dx, dcin, dsg, dq, dk, dv, dfl, dgl, w["mix_norm"],
                                     w["w1_t"], w["wf_t"], w["wgl_t"])
    g["mix_norm"] = dg[0]
    c0, c1 = 3 * CONV_W, 3 * CONV_W + 2 * SG_W
    g["w_in"] = jnp.concatenate([
        _mm_tn(f"dwin_conv_{l}", dcin, h, c0, D_MODEL, tk=c0, tn=D_MODEL),
        _mm_tn(f"dwin_sg_{l}", dsg, h, c1 - c0, D_MODEL, tk=c1 - c0, tn=D_MODEL),
        _mm_tn(f"dwin_qkv_{l}", dqkv, h, 3 * FOX_W, D_MODEL, tk=3 * FOX_W, tn=D_MODEL),
        _mm_tn(f"dwin_f_{l}", dfb, h, LANES, D_MODEL, tk=LANES, tn=D_MODEL)[:FOX_H],
        _mm_tn(f"dwin_gl_{l}", dgl, h, 3 * D_MODEL, D_MODEL, tk=3 * D_MODEL // 2, tn=D_MODEL),
    ], axis=0)

    dx, gf = _ffn_grads(l, s["x0"], dx, s["a1"], s["b1"], w, "ffn1")
    g.update(gf)
    return dx, g


def _local_step(x, mem, target, wts, small):
    saved, lw = [], []
    for l in range(DEPTH):
        lw.append(_layer_weights(wts, small, l))
        x, s = _layer_fwd(l, x, mem, lw[l])
        saved.append(s)
    fin = small["final_norm"][None, :]
    dx, loss, dgf = _loss_head("loss_head", x, target, fin)
    layer_grads = [None] * DEPTH
    for l in reversed(range(DEPTH)):
        dx, layer_grads[l] = _layer_bwd(l, dx, mem, lw[l], saved[l])
    grads = {nm: jnp.stack([layer_grads[l][nm] for l in range(DEPTH)]) for nm in WEIGHTS if nm != "final_norm"}
    grads["final_norm"] = dgf[0]
    return loss[0, 0], dx, grads


def _small_slab(vals):
    rows = []
    for v in vals:
        flat = v.astype(F32).reshape(-1)
        n = -(-flat.shape[0] // LANES) * LANES
        rows.append(jnp.pad(flat, (0, n - flat.shape[0])).reshape(-1, LANES))
    slab = jnp.concatenate(rows)
    pad = -slab.shape[0] % 8
    return jnp.pad(slab, ((0, pad), (0, 0)))


def _small_unslab(slab, shapes):
    out, off = [], 0
    for shp in shapes:
        size = 1
        for s in shp:
            size *= s
        n_rows = -(-size // LANES)
        out.append(slab[off:off + n_rows].reshape(-1)[:size].reshape(shp))
        off += n_rows
    return out


def kernel(x, mem, ffn1_norm, ffn1_w_gate, ffn1_w_up, ffn1_w_down, mix_norm, w_in, conv_w, sg_norm, sg_w, sg_b, fox_b_f, w_branch, w_out, xa_norm, mem_norm, xa_wq, xa_wk, xa_wv, xa_wo, ffn2_norm, ffn2_w_gate, ffn2_w_up, ffn2_w_down, final_norm, loss_target, m_ffn1_norm, m_ffn1_w_gate, m_ffn1_w_up, m_ffn1_w_down, m_mix_norm, m_w_in, m_conv_w, m_sg_norm, m_sg_w, m_sg_b, m_fox_b_f, m_w_branch, m_w_out, m_xa_norm, m_mem_norm, m_xa_wq, m_xa_wk, m_xa_wv, m_xa_wo, m_ffn2_norm, m_ffn2_w_gate, m_ffn2_w_up, m_ffn2_w_down, m_final_norm, v_ffn1_norm, v_ffn1_w_gate, v_ffn1_w_up, v_ffn1_w_down, v_mix_norm, v_w_in, v_conv_w, v_sg_norm, v_sg_w, v_sg_b, v_fox_b_f, v_w_branch, v_w_out, v_xa_norm, v_mem_norm, v_xa_wq, v_xa_wk, v_xa_wv, v_xa_wo, v_ffn2_norm, v_ffn2_w_gate, v_ffn2_w_up, v_ffn2_w_down, v_final_norm):
    args = dict(locals())
    wv = {nm: args[nm] for nm in WEIGHTS}
    mv = {nm: args["m_" + nm] for nm in WEIGHTS}
    vv = {nm: args["v_" + nm] for nm in WEIGHTS}
    chip = 2 * lax.axis_index("x") + lax.axis_index("y")
    core = lax.axis_index("c")

    names = [nm for nm, _ in BIG]
    mine = {nm: _to_exchange(nm, wv[nm].astype(BF16)) for nm in names}
    row_counts = [mine[nm].shape[0] for nm in names]
    gathered = _gather_chips("gather_weights", _pack([mine[nm] for nm in names], BF16))
    wts = {}
    for nm, slab in zip(names, _unpack(gathered, row_counts)):
        p = _from_exchange(nm, jnp.where((jnp.arange(N_CHIPS) == chip)[:, None, None], mine[nm][None], slab), wv[nm].shape)
        if BIG_KIND[nm] == "branch":
            wts[nm] = jnp.concatenate([p[j] for j in range(N_CHIPS)], axis=3)
        else:
            wts[nm] = jnp.swapaxes(p, 0, 1).reshape(p.shape[1], N_CHIPS * p.shape[2], PACK_COLS)
    taps = _gather_all("gather_taps", _small_slab([conv_w]))
    taps = [_small_unslab(taps[2 * j], [conv_w.shape])[0] for j in range(N_CHIPS)]
    small = {nm: wv[nm] for nm in SMALL}
    small["conv_w"] = jnp.concatenate(taps, axis=2)

    loss, dx, grads = _local_step(x[0], mem[0], loss_target[0], wts, small)
    loss = lax.psum(loss, ("x", "y", "c"))

    small_shapes = [grads[nm].shape for nm in SMALL]
    slots = _gather_all("gather_small_grads", _small_slab([grads[nm] for nm in SMALL]))
    small_sum = _sum_slots("sum_small_grads", slots, slots.shape[1])
    small_g = dict(zip(SMALL, _small_unslab(small_sum, small_shapes)))
    taps_g = small_g["conv_w"]
    small_g["conv_w"] = lax.dynamic_slice_in_dim(taps_g, chip * conv_w.shape[2], conv_w.shape[2], axis=2)

    def grad_shard(nm, ax, j):
        g = grads[nm]
        ax = 1 if BIG_KIND[nm] == "cols" else ax
        size = g.shape[ax] // N_CHIPS
        return _to_exchange(nm, lax.slice_in_dim(g, j * size, (j + 1) * size, axis=ax), transposed=True)

    gp = jnp.stack([_pack([grad_shard(nm, ax, j) for nm, ax in BIG], F32) for j in range(N_CHIPS)], axis=1)
    got = _swap_halves("reduce_swap_cores", gp)
    chip_sum = _add_pair("reduce_add_cores", gp, got, core.astype(jnp.int32).reshape(1))
    parts = _scatter_chips("reduce_scatter_chips", chip_sum)
    half = _sum_parts("reduce_add_chips", chip_sum, parts, chip.astype(jnp.int32).reshape(1))
    other = _join_halves("reduce_join_cores", half)
    full = jnp.where(core == 0, jnp.stack([half, other]), jnp.stack([other, half]))
    big_g = {}
    for nm, slab in zip(names, _unpack(full, row_counts)):
        g = _from_exchange(nm, slab, wv[nm].shape)
        big_g[nm] = jnp.swapaxes(g, 1, 2) if BIG_KIND[nm] == "cols" else g

    g_out = {**small_g, **big_g}
    delta, new_m, new_v = {}, {}, {}
    for nm, _ in BIG:
        shp = wv[nm].shape
        two_d = (-1, shp[-1])
        d, m2, v2 = _adamw("adamw_" + nm, wv[nm].reshape(two_d), g_out[nm].reshape(two_d), mv[nm].reshape(two_d),
                           vv[nm].reshape(two_d))
        delta[nm], new_m[nm], new_v[nm] = d.reshape(shp), m2.reshape(shp), v2.reshape(shp)
    slab_shapes = [wv[nm].shape for nm in SMALL]
    d, m2, v2 = _adamw("adamw_small", _small_slab([wv[nm] for nm in SMALL]), _small_slab([g_out[nm] for nm in SMALL]),
                       _small_slab([mv[nm] for nm in SMALL]), _small_slab([vv[nm] for nm in SMALL]))
    for out, slab in ((delta, d), (new_m, m2), (new_v, v2)):
        out.update(zip(SMALL, _small_unslab(slab, slab_shapes)))

    return (loss, dx[None], *[g_out[nm] for nm in WEIGHTS], *[delta[nm] for nm in WEIGHTS],
            *[new_m[nm] for nm in WEIGHTS], *[new_v[nm] for nm in WEIGHTS])
```

```python
import functools

import jax
import jax.numpy as jnp
from jax import lax
from jax.experimental import pallas as pl
from jax.experimental.pallas import tpu as pltpu

F32, BF16 = jnp.float32, jnp.bfloat16
MESH_ID = pl.DeviceIdType.MESH

D_MODEL = 1024
DEPTH = 2
D_FF = 2816
CONV_W = 512
SG_W = 512
SG_G = 4
CHUNK = 128
FOX_H = 8
FOX_D = 64
FOX_W = FOX_H * FOX_D
FOX_SCALE = FOX_D ** -0.5
LOG2E = 1.4426950408889634
XA_H = 4
XA_D = D_MODEL // XA_H
N_CHIPS = 4
RMS_EPS = 1e-6
W1_COLS = 3 * CONV_W + 2 * SG_W + 3 * FOX_W
LANES = 128
HALO = 16

ADAM_LR, ADAM_B1, ADAM_B2, ADAM_EPS, ADAM_WD, ADAM_STEP = 0.001, 0.9, 0.999, 1e-08, 0.01, 10

ROW_BLOCK = 256
FLASH_BLOCK = 512
FLASH_STRIP = 32
PRUNE_LOG2 = 40.0
FF_SPLIT = 1408
VMEM_LIMIT = 56 * 1024 * 1024
PACK_COLS = 1024
PACK_ROW_BLOCK = 208

BIG = (
    ("ffn1_w_gate", 2), ("ffn1_w_up", 2), ("ffn1_w_down", 1), ("w_in", 2), ("w_branch", 3), ("w_out", 1),
    ("xa_wq", 1), ("xa_wk", 1), ("xa_wv", 1), ("xa_wo", 1), ("ffn2_w_gate", 2), ("ffn2_w_up", 2), ("ffn2_w_down", 1),
)
BIG_KIND = {nm: ("branch" if nm == "w_branch" else "cols" if ax == 2 else "rows") for nm, ax in BIG}
SMALL = ("ffn1_norm", "mix_norm", "conv_w", "sg_norm", "sg_w", "sg_b", "fox_b_f", "xa_norm", "mem_norm", "ffn2_norm",
         "final_norm")
WEIGHTS = ("ffn1_norm", "ffn1_w_gate", "ffn1_w_up", "ffn1_w_down", "mix_norm", "w_in", "conv_w", "sg_norm", "sg_w",
           "sg_b", "fox_b_f", "w_branch", "w_out", "xa_norm", "mem_norm", "xa_wq", "xa_wk", "xa_wv", "xa_wo",
           "ffn2_norm", "ffn2_w_gate", "ffn2_w_up", "ffn2_w_down", "final_norm")


def _dot(a, b):
    return jnp.dot(a, b, preferred_element_type=F32)


def _dot_nt(a, b):
    return lax.dot_general(a, b, (((1,), (1,)), ((), ())), preferred_element_type=F32)


def _dot_tn(a, b):
    return lax.dot_general(a, b, (((0,), (0,)), ((), ())), preferred_element_type=F32)


def _rms_stats(x):
    r = lax.rsqrt(jnp.mean(x * x, axis=-1, keepdims=True) + RMS_EPS)
    return x * r, r


def _rms_bwd(xh, r, g, dy):
    dg = jnp.sum(dy * xh, axis=0, keepdims=True)
    dxh = dy * g
    dx = r * (dxh - xh * jnp.mean(dxh * xh, axis=-1, keepdims=True))
    return dx, dg


def _gelu(x):
    k = 0.7978845608028654
    t = jnp.tanh(k * (x + 0.044715 * x * x * x))
    return 0.5 * x * (1.0 + t), t


def _gelu_grad(x, t):
    k = 0.7978845608028654
    return 0.5 * (1.0 + t) + 0.5 * x * (1.0 - t * t) * k * (1.0 + 3.0 * 0.044715 * x * x)


def _split3_dot(tri, x):
    hi = x.astype(BF16)
    r1 = x - hi.astype(F32)
    mid = r1.astype(BF16)
    lo = (r1 - mid.astype(F32)).astype(BF16)
    return _dot(tri, hi) + _dot(tri, mid) + _dot(tri, lo)


def _params(n_grid):
    return pltpu.CompilerParams(dimension_semantics=("arbitrary",) * n_grid, vmem_limit_bytes=VMEM_LIMIT)


def _full_spec(shape):
    nd = len(shape)
    return pl.BlockSpec(tuple(shape), lambda *_: (0,) * nd)


def _rowcall(name, body, n_tokens, tm, rows, consts, residents, row_outs, acc_outs, scratch=()):
    n = n_tokens // tm
    rows = [r if isinstance(r, tuple) else (r, pl.BlockSpec((tm, r.shape[1]), lambda i: (i, 0))) for r in rows]
    n_in, n_w = len(rows) + len(consts), len(residents)
    n_out = len(row_outs) + len(acc_outs)
    in_specs = ([r[1] for r in rows] + [_full_spec(c.shape) for c in consts]
                + [pl.BlockSpec(memory_space=pl.ANY)] * n_w)
    out_shape = ([jax.ShapeDtypeStruct((n_tokens, c), dt) for c, dt in row_outs]
                 + [jax.ShapeDtypeStruct(s, dt) for s, dt in acc_outs])
    out_specs = ([pl.BlockSpec((tm, c), lambda i: (i, 0)) for c, _ in row_outs]
                 + [_full_spec(s) for s, _ in acc_outs])
    scratch_shapes = [pltpu.VMEM(w.shape, w.dtype) for w in residents]
    if n_w:
        scratch_shapes.append(pltpu.SemaphoreType.DMA((n_w,)))
    scratch_shapes += list(scratch)

    def kern(*refs):
        ins, w_hbm = refs[:n_in], refs[n_in:n_in + n_w]
        outs = refs[n_in + n_w:n_in + n_w + n_out]
        rest = refs[n_in + n_w + n_out:]
        w_vmem = rest[:n_w]
        extra = rest[n_w + 1:] if n_w else rest
        i = pl.program_id(0)

        @pl.when(i == 0)
        def _():
            copies = [pltpu.make_async_copy(w_hbm[k], w_vmem[k], rest[n_w].at[k]) for k in range(n_w)]
            for cp in copies:
                cp.start()
            for cp in copies:
                cp.wait()
            for a in outs[len(row_outs):]:
                a[...] = jnp.zeros(a.shape, a.dtype)

        body(i, n, *ins, *w_vmem, *outs, *extra)

    res = pl.pallas_call(
        kern, name=name, grid=(n,), in_specs=in_specs, out_specs=out_specs, out_shape=out_shape,
        scratch_shapes=scratch_shapes, compiler_params=_params(1),
    )(*[r[0] for r in rows], *consts, *residents)
    return res


def _mm_tn(name, x, y, k_dim, n_dim, *, tk, tn, x_off=0, y_off=0, tt=512):
    n_tok = x.shape[0]
    tt = min(tt, n_tok)
    n_t = n_tok // tt
    xb, yb = x_off // tk, y_off // tn

    def kern(x_ref, y_ref, o_ref):
        @pl.when(pl.program_id(2) == 0)
        def _():
            o_ref[...] = jnp.zeros(o_ref.shape, F32)

        o_ref[...] += _dot_tn(x_ref[...], y_ref[...])

    return pl.pallas_call(
        kern, name=name, grid=(k_dim // tk, n_dim // tn, n_t),
        in_specs=[pl.BlockSpec((tt, tk), lambda k, n, t: (t, xb + k)),
                  pl.BlockSpec((tt, tn), lambda k, n, t: (t, yb + n))],
        out_specs=pl.BlockSpec((tk, tn), lambda k, n, t: (k, n)),
        out_shape=jax.ShapeDtypeStruct((k_dim, n_dim), F32),
        compiler_params=_params(3),
    )(x, y)


def _ffn_fwd(name, x, gain, wg, wu, wd):
    n_tok, tm = x.shape[0], min(ROW_BLOCK, x.shape[0])

    def body(i, n, x_ref, g_ref, wg_ref, wu_ref, wd_ref, xo_ref, a_ref, b_ref):
        x_v = x_ref[...]
        xh, _ = _rms_stats(x_v)
        h = (xh * g_ref[...]).astype(BF16)
        y = jnp.zeros((tm, D_MODEL), F32)
        for f0 in range(0, D_FF, FF_SPLIT):
            sl = slice(f0, f0 + FF_SPLIT)
            a = _dot(h, wg_ref[:, sl])
            b = _dot(h, wu_ref[:, sl])
            a_ref[:, sl] = a.astype(BF16)
            b_ref[:, sl] = b.astype(BF16)
            s = (a * jax.nn.sigmoid(a) * b).astype(BF16)
            y = y + _dot(s, wd_ref[sl, :])
        xo_ref[...] = x_v + 0.5 * y

    return _rowcall(name, body, n_tok, tm, [x], [gain], [wg, wu, wd],
                    [(D_MODEL, F32), (D_FF, BF16), (D_FF, BF16)], [])


def _ffn_bwd(name, x, dxo, a, b, gain, wd_t, wg_t, wu_t):
    n_tok, tm = x.shape[0], min(ROW_BLOCK, x.shape[0])

    def body(i, n, x_ref, dxo_ref, a_ref, b_ref, g_ref, wdt_ref, wgt_ref, wut_ref,
             dx_ref, da_ref, db_ref, s_ref, h_ref, dy_ref, dg_ref):
        g = g_ref[...]
        xh, r = _rms_stats(x_ref[...])
        h_ref[...] = (xh * g).astype(BF16)
        dxo_v = dxo_ref[...]
        dy = (0.5 * dxo_v).astype(BF16)
        dy_ref[...] = dy
        dh = jnp.zeros((tm, D_MODEL), F32)
        for f0 in range(0, D_FF, FF_SPLIT):
            sl = slice(f0, f0 + FF_SPLIT)
            a_v = a_ref[:, sl].astype(F32)
            b_v = b_ref[:, sl].astype(F32)
            ds = _dot(dy, wdt_ref[:, sl])
            sig = jax.nn.sigmoid(a_v)
            sa = a_v * sig
            s_ref[:, sl] = (sa * b_v).astype(BF16)
            da = (ds * b_v * (sig * (1.0 + a_v * (1.0 - sig)))).astype(BF16)
            db = (ds * sa).astype(BF16)
            da_ref[:, sl] = da
            db_ref[:, sl] = db
            dh = dh + _dot(da, wgt_ref[sl, :]) + _dot(db, wut_ref[sl, :])
        dx, dg = _rms_bwd(xh, r, g, dh)
        dx_ref[...] = dxo_v + dx
        dg_ref[...] += dg

    return _rowcall(name, body, n_tok, tm, [x, dxo, a, b], [gain], [wd_t, wg_t, wu_t],
                    [(D_MODEL, F32), (D_FF, BF16), (D_FF, BF16), (D_FF, BF16), (D_MODEL, BF16), (D_MODEL, BF16)],
                    [((1, D_MODEL), F32)])


def _proj_fwd(name, x, gain, w1, wf, wgl):
    n_tok, tm = x.shape[0], min(ROW_BLOCK, x.shape[0])
    c0, c1 = 3 * CONV_W, 3 * CONV_W + 2 * SG_W

    def body(i, n, x_ref, g_ref, w1_ref, wf_ref, wgl_ref, cin_ref, sg_ref, qkv_ref, fl_ref, gl_ref):
        xh, _ = _rms_stats(x_ref[...])
        h = (xh * g_ref[...]).astype(BF16)
        cin_ref[...] = _dot(h, w1_ref[:, 0:c0]).astype(BF16)
        sg_ref[...] = _dot(h, w1_ref[:, c0:c1]).astype(BF16)
        qkv_ref[...] = _dot(h, w1_ref[:, c1:W1_COLS]).astype(BF16)
        fl_ref[...] = _dot(h, wf_ref[...])
        gl_ref[...] = _dot(h, wgl_ref[...]).astype(BF16)

    return _rowcall(name, body, n_tok, tm, [x], [gain], [w1, wf, wgl],
                    [(3 * CONV_W, BF16), (2 * SG_W, BF16), (3 * FOX_W, BF16), (LANES, F32), (3 * D_MODEL, BF16)], [])


def _proj_bwd(name, x, dxin, dcin, dsg, dq, dk, dv, dfl, dgl, gain, w1_t, wf_t, wgl_t):
    n_tok, tm = x.shape[0], min(ROW_BLOCK, x.shape[0])
    c0, c1 = 3 * CONV_W, 3 * CONV_W + 2 * SG_W

    def body(i, n, x_ref, dxin_ref, dcin_ref, dsg_ref, dq_ref, dk_ref, dv_ref, dfl_ref, dgl_ref, g_ref,
             w1t_ref, wft_ref, wglt_ref, dx_ref, h_ref, dqkv_ref, dfb_ref, dg_ref):
        g = g_ref[...]
        xh, r = _rms_stats(x_ref[...])
        h_ref[...] = (xh * g).astype(BF16)
        dfb = dfl_ref[...].astype(BF16)
        dfb_ref[...] = dfb
        dh = _dot(dcin_ref[...], w1t_ref[0:c0, :])
        dh = dh + _dot(dsg_ref[...], w1t_ref[c0:c1, :])
        for k, d_ref in enumerate((dq_ref, dk_ref, dv_ref)):
            d_b = d_ref[...].astype(BF16)
            dqkv_ref[:, k * FOX_W:(k + 1) * FOX_W] = d_b
            dh = dh + _dot(d_b, w1t_ref[c1 + k * FOX_W:c1 + (k + 1) * FOX_W, :])
        dh = dh + _dot(dfb, wft_ref[...])
        dh = dh + _dot(dgl_ref[...], wglt_ref[...])
        dx, dg = _rms_bwd(xh, r, g, dh)
        dx_ref[...] = dxin_ref[...] + dx
        dg_ref[...] += dg

    return _rowcall(name, body, n_tok, tm, [x, dxin, dcin, dsg, dq, dk, dv, dfl, dgl], [gain], [w1_t, wf_t, wgl_t],
                    [(D_MODEL, F32), (D_MODEL, BF16), (3 * FOX_W, BF16), (LANES, BF16)], [((1, D_MODEL), F32)])


def _conv_taps(z, prev_z, i):
    tm = z.shape[0]
    row = lax.broadcasted_iota(jnp.int32, (tm, 1), 0)
    live = (i > 0).astype(F32)
    p1, p2 = prev_z[HALO - 1:HALO, :] * live, prev_z[HALO - 2:HALO - 1, :] * live
    z1 = jnp.where(row == 0, p1, pltpu.roll(z, 1, 0))
    z2 = jnp.where(row == 0, p2, jnp.where(row == 1, p1, pltpu.roll(z, 2, 0)))
    return z1, z2


def _prev_spec(tm, cols):
    return pl.BlockSpec((HALO, cols), lambda i: (jnp.maximum(i * (tm // HALO) - 1, 0), 0))


def _next_spec(tm, cols, n):
    last = n * (tm // HALO) - 1
    return pl.BlockSpec((HALO, cols), lambda i: (jnp.minimum((i + 1) * (tm // HALO), last), 0))


def _conv_fwd(name, cin, cw):
    n_tok, tm = cin.shape[0], min(ROW_BLOCK, cin.shape[0])
    w = CONV_W

    def body(i, n, c_ref, p_ref, cw_ref, ya_ref):
        c_v, p_v = c_ref[...].astype(F32), p_ref[...].astype(F32)
        z = c_v[:, w:2 * w] * c_v[:, 2 * w:]
        z1, z2 = _conv_taps(z, p_v[:, w:2 * w] * p_v[:, 2 * w:], i)
        y = cw_ref[0:1, :] * z2 + cw_ref[1:2, :] * z1 + cw_ref[2:3, :] * z
        ya_ref[...] = (c_v[:, 0:w] * y).astype(BF16)

    return _rowcall(name, body, n_tok, tm, [cin, (cin, _prev_spec(tm, 3 * w))], [cw], [], [(w, BF16)], [])[0]


def _conv_bwd(name, cin, dya, cw):
    n_tok, tm = cin.shape[0], min(ROW_BLOCK, cin.shape[0])
    w = CONV_W
    n_blocks = n_tok // tm

    def body(i, n, c_ref, p_ref, nx_ref, dya_ref, ndya_ref, cw_ref, dc_ref, dcw_ref):
        c_v, p_v = c_ref[...].astype(F32), p_ref[...].astype(F32)
        ab, ac, ah = c_v[:, 0:w], c_v[:, w:2 * w], c_v[:, 2 * w:]
        z = ac * ah
        z1, z2 = _conv_taps(z, p_v[:, w:2 * w] * p_v[:, 2 * w:], i)
        w0, w1, w2 = cw_ref[0:1, :], cw_ref[1:2, :], cw_ref[2:3, :]
        y = w0 * z2 + w1 * z1 + w2 * z
        dya_v = dya_ref[...].astype(F32)
        dy = dya_v * ab
        live = (i < n - 1).astype(F32)
        ndy = ndya_ref[...].astype(F32) * nx_ref[:, 0:w].astype(F32) * live
        row = lax.broadcasted_iota(jnp.int32, (tm, 1), 0)
        dy1 = jnp.where(row == tm - 1, ndy[0:1, :], pltpu.roll(dy, tm - 1, 0))
        dy2 = jnp.where(row == tm - 1, ndy[1:2, :], jnp.where(row == tm - 2, ndy[0:1, :], pltpu.roll(dy, tm - 2, 0)))
        dz = w2 * dy + w1 * dy1 + w0 * dy2
        dc_ref[:, 0:w] = (dya_v * y).astype(BF16)
        dc_ref[:, w:2 * w] = (dz * ah).astype(BF16)
        dc_ref[:, 2 * w:] = (dz * ac).astype(BF16)
        dcw_ref[0:1, :] += jnp.sum(dy * z2, axis=0, keepdims=True)
        dcw_ref[1:2, :] += jnp.sum(dy * z1, axis=0, keepdims=True)
        dcw_ref[2:3, :] += jnp.sum(dy * z, axis=0, keepdims=True)

    return _rowcall(name, body, n_tok, tm,
                    [cin, (cin, _prev_spec(tm, 3 * w)), (cin, _next_spec(tm, 3 * w, n_blocks)),
                     dya, (dya, _next_spec(tm, w, n_blocks))],
                    [cw], [], [(3 * w, BF16)], [((8, w), F32)])


def _sg_common(sg_ref, gn_ref):
    s_v = sg_ref[...].astype(F32)
    u, v = s_v[:, 0:SG_W], s_v[:, SG_W:]
    ug, tu = _gelu(u)
    vg, tv = _gelu(v)
    vh, r = _rms_stats(vg)
    vn = (vh * gn_ref[...]).astype(BF16)
    return u, v, ug, tu, tv, vh, r, vn


def _sg_fwd(name, sgin, gn, sgw, bias_full):
    n_tok, tm = sgin.shape[0], min(ROW_BLOCK, sgin.shape[0])

    def body(i, n, sg_ref, gn_ref, w_ref, bias_ref, yb_ref):
        _, _, ug, _, _, _, _, vn = _sg_common(sg_ref, gn_ref)
        tril = lax.broadcasted_iota(jnp.int32, (CHUNK, CHUNK), 0) >= lax.broadcasted_iota(jnp.int32, (CHUNK, CHUNK), 1)
        wt = [jnp.where(tril, w_ref[g], 0.0).astype(BF16) for g in range(SG_G)]
        for c0 in range(0, tm, CHUNK):
            sv = jnp.concatenate(
                [_dot(wt[g], vn[c0:c0 + CHUNK, g * CHUNK:(g + 1) * CHUNK]) for g in range(SG_G)], axis=1)
            sv = sv + bias_ref[...]
            yb_ref[c0:c0 + CHUNK, :] = (ug[c0:c0 + CHUNK, :] * sv).astype(BF16)

    return _rowcall(name, body, n_tok, tm, [sgin], [gn, sgw, bias_full], [], [(SG_W, BF16)], [])[0]


def _sg_bwd(name, sgin, dyb, gn, sgw, sgw_t, bias_full):
    n_tok, tm = sgin.shape[0], min(ROW_BLOCK, sgin.shape[0])

    def body(i, n, sg_ref, dyb_ref, gn_ref, w_ref, wt_ref, bias_ref, dsg_ref, dgn_ref, dw_ref, db_ref):
        u, v, ug, tu, tv, vh, r, vn = _sg_common(sg_ref, gn_ref)
        r0 = lax.broadcasted_iota(jnp.int32, (CHUNK, CHUNK), 0)
        r1 = lax.broadcasted_iota(jnp.int32, (CHUNK, CHUNK), 1)
        wt = [jnp.where(r0 >= r1, w_ref[g], 0.0).astype(BF16) for g in range(SG_G)]
        wtt = [jnp.where(r0 <= r1, wt_ref[g], 0.0).astype(BF16) for g in range(SG_G)]
        dyb_v = dyb_ref[...].astype(F32)
        dvn_rows = []
        for c0 in range(0, tm, CHUNK):
            rows = slice(c0, c0 + CHUNK)
            svs, dvns = [], []
            dsv = dyb_v[rows, :] * ug[rows, :]
            for g in range(SG_G):
                cols = slice(g * CHUNK, (g + 1) * CHUNK)
                svs.append(_dot(wt[g], vn[rows, cols]))
                dsv_g = dsv[:, cols]
                dsv_b = dsv_g.astype(BF16)
                dvns.append(_dot(wtt[g], dsv_b))
                dw_ref[g] += jnp.where(r0 >= r1, _dot_nt(dsv_b, vn[rows, cols]), 0.0)
                db_ref[g] += jnp.broadcast_to(jnp.sum(dsv_g, axis=1, keepdims=True), (CHUNK, CHUNK))
            sv = jnp.concatenate(svs, axis=1) + bias_ref[...]
            dug = dyb_v[rows, :] * sv
            dsg_ref[rows, 0:SG_W] = (dug * _gelu_grad(u[rows, :], tu[rows, :])).astype(BF16)
            dvn_rows.append(jnp.concatenate(dvns, axis=1))
        dvn = jnp.concatenate(dvn_rows, axis=0)
        dvg, dgn = _rms_bwd(vh, r, gn_ref[...], dvn)
        dsg_ref[:, SG_W:] = (dvg * _gelu_grad(v, tv)).astype(BF16)
        dgn_ref[...] += dgn

    return _rowcall(name, body, n_tok, tm, [sgin, dyb], [gn, sgw, sgw_t, bias_full], [], [(2 * SG_W, BF16)],
                    [((1, SG_W), F32), ((SG_G, CHUNK, CHUNK), F32), ((SG_G, CHUNK, CHUNK), F32)])


def _forget_cumsum(name, fl, bf):
    n_tok, tm = fl.shape[0], min(ROW_BLOCK, fl.shape[0])

    def body(i, n, fl_ref, b_ref, c_ref, carry):
        @pl.when(i == 0)
        def _():
            carry[...] = jnp.zeros(carry.shape, F32)

        z = fl_ref[...] + b_ref[...]
        lf = jnp.minimum(z, 0.0) - jnp.log1p(jnp.exp(-jnp.abs(z)))
        tri = (lax.broadcasted_iota(jnp.int32, (tm, tm), 0) >= lax.broadcasted_iota(jnp.int32, (tm, tm), 1)).astype(BF16)
        c = _split3_dot(tri, lf) + carry[...]
        c_ref[...] = c * LOG2E
        carry[...] = c[tm - 1:tm, :]

    return _rowcall(name, body, n_tok, tm, [fl], [bf], [], [(LANES, F32)], [], scratch=[pltpu.VMEM((1, LANES), F32)])[0]


def _forget_bwd(name, dc, fl, bf):
    n_tok, tm = fl.shape[0], min(ROW_BLOCK, fl.shape[0])
    n = n_tok // tm
    rev = pl.BlockSpec((tm, LANES), lambda i: (n - 1 - i, 0))

    def kern(dc_ref, fl_ref, b_ref, dfl_ref, db_ref, carry):
        @pl.when(pl.program_id(0) == 0)
        def _():
            carry[...] = jnp.zeros(carry.shape, F32)
            db_ref[...] = jnp.zeros(db_ref.shape, F32)

        triu = (lax.broadcasted_iota(jnp.int32, (tm, tm), 0) <= lax.broadcasted_iota(jnp.int32, (tm, tm), 1)).astype(BF16)
        dlf = _split3_dot(triu, dc_ref[...]) + carry[...]
        carry[...] = dlf[0:1, :]
        z = fl_ref[...] + b_ref[...]
        dfl = dlf * jax.nn.sigmoid(-z)
        dfl_ref[...] = dfl
        db_ref[...] += jnp.sum(dfl, axis=0, keepdims=True)

    return pl.pallas_call(
        kern, name=name, grid=(n,), in_specs=[rev, rev, _full_spec((1, LANES))],
        out_specs=[rev, _full_spec((1, LANES))],
        out_shape=[jax.ShapeDtypeStruct((n_tok, LANES), F32), jax.ShapeDtypeStruct((1, LANES), F32)],
        scratch_shapes=[pltpu.VMEM((1, LANES), F32)], compiler_params=_params(1),
    )(dc, fl, bf)


def _lane_pick(c_blk, h):
    lane = lax.broadcasted_iota(jnp.int32, c_blk.shape, 1)
    return jnp.broadcast_to(jnp.sum(jnp.where(lane == h, c_blk, 0.0), axis=1, keepdims=True), c_blk.shape)


def _wide(stat, width):
    return jnp.tile(stat, (1, width // LANES))


def _pair_half(rows):
    return lax.broadcasted_iota(jnp.int32, (rows, LANES), 1) // FOX_D


def _col_to_row(col):
    pick = (lax.broadcasted_iota(jnp.int32, (8, LANES), 1) == 0).astype(BF16)
    hi = col.astype(BF16)
    r1 = col - hi.astype(F32)
    mid = r1.astype(BF16)
    lo = (r1 - mid.astype(F32)).astype(BF16)
    return (_dot_nt(pick, hi) + _dot_nt(pick, mid) + _dot_nt(pick, lo))[0:1, :]


def _tri_table(nb, lower):
    rows = [(a, b) for a in range(nb) for b in (range(a + 1) if lower else range(a, nb))]
    return tuple(jnp.asarray([r[k] for r in rows], jnp.int32) for k in range(2))


def _flash_bounds(name, qkv):
    n_tok = qkv.shape[0]
    blk = min(FLASH_BLOCK, n_tok)

    def kern(x_ref, o_ref):
        head = (lax.broadcasted_iota(jnp.int32, (FOX_W, LANES), 0) // FOX_D
                == lax.broadcasted_iota(jnp.int32, (FOX_W, LANES), 1)).astype(BF16)
        q, k = x_ref[:, 0:FOX_W].astype(F32), x_ref[:, FOX_W:2 * FOX_W].astype(F32)
        qn = jnp.sqrt(jnp.max(_dot((q * q).astype(BF16), head) * 1.01, axis=0, keepdims=True))
        kn = jnp.sqrt(jnp.max(_dot((k * k).astype(BF16), head) * 1.01, axis=0, keepdims=True))
        diag = jnp.min(_dot((q * k).astype(BF16), head), axis=0, keepdims=True) - qn * kn * (2.0 ** -8)
        o_ref[...] = jnp.concatenate([qn, kn, diag, jnp.zeros((5, LANES), F32)], axis=0)

    return pl.pallas_call(
        kern, name=name, grid=(n_tok // blk,),
        in_specs=[pl.BlockSpec((blk, 2 * FOX_W), lambda b: (b, 0))], out_specs=pl.BlockSpec((8, LANES), lambda b: (b, 0)),
        out_shape=jax.ShapeDtypeStruct((n_tok // blk * 8, LANES), F32), compiler_params=_params(1),
    )(qkv)


def _prune_tables(bounds, c2, blk):
    nb = c2.shape[0] // blk
    tab = bounds.reshape(nb, 8, LANES)
    c_blocks = c2.reshape(nb, blk, LANES)
    return tuple(t[:, :FOX_H].reshape(-1)
                 for t in (tab[:, 0], tab[:, 1], tab[:, 2], c_blocks[:, 0], c_blocks[:, blk - 1]))


def _block_is_live(tables, p, i, j):
    qn, kn, diag, c_first, c_last = tables
    a = FOX_SCALE * LOG2E
    live = []
    for e in range(2):
        h = 2 * p + e
        u = a * qn[8 * i + h] * kn[8 * j + h] + c_first[8 * i + h] - c_last[8 * j + h]
        lo = a * diag[8 * i + h]
        live.append((u - lo) > -PRUNE_LOG2)
    return live


def _flash_fwd(name, qkv, c2, ct2, tables):
    n_tok = qkv.shape[0]
    blk = min(FLASH_BLOCK, n_tok)
    nb = n_tok // blk
    n_pair = FOX_H // 2
    it, jt = _tri_table(nb, True)

    strip = min(FLASH_STRIP, blk)

    def kern(it_ref, jt_ref, qn_ref, kn_ref, dg_ref, cf_ref, cl_ref, q_ref, k_ref, v_ref, c_ref, ct_ref, o_ref, lse_ref,
             m_s, l_s, acc_s, cq_s, qm_s, al_s, s_s, pb_s):
        n = pl.program_id(1)
        i, j = it_ref[n], jt_ref[n]
        half = _pair_half(blk)
        live = _block_is_live((qn_ref, kn_ref, dg_ref, cf_ref, cl_ref), pl.program_id(0), i, j)

        @pl.when(j == 0)
        def _():
            m_s[...] = jnp.full(m_s.shape, -jnp.inf, F32)
            l_s[...] = jnp.zeros(l_s.shape, F32)
            acc_s[...] = jnp.zeros(acc_s.shape, F32)
            for e in range(2):
                cq_s[e] = _lane_pick(c_ref[...], 2 * pl.program_id(0) + e)
                qm_s[e] = jnp.where(half == e, q_ref[...], jnp.zeros_like(q_ref[...]))

        def step(heads, on_diagonal):
            k_v, v_v = k_ref[...], v_ref[...]
            for e in heads:
                s_s[e] = _dot_nt(qm_s[e], k_v)
            for e in heads:
                ck = ct_ref[e]
                for r0 in range(0, blk, strip):
                    rs = slice(r0, r0 + strip)
                    t = s_s[e, rs, :] * (FOX_SCALE * LOG2E) - ck
                    if on_diagonal:
                        keep = (lax.broadcasted_iota(jnp.int32, (strip, blk), 1)
                                <= lax.broadcasted_iota(jnp.int32, (strip, blk), 0) + r0)
                        t = jnp.where(keep, t, -jnp.inf)
                    m_old, cq = m_s[e, rs, :], cq_s[e, rs, :]
                    m_new = jnp.maximum(m_old, jnp.max(t, axis=1, keepdims=True) + cq)
                    p = jnp.exp2(t - _wide(m_new - cq, blk))
                    alpha = jnp.exp2(m_old - m_new)
                    l_s[e, rs, :] = alpha * l_s[e, rs, :] + jnp.sum(p, axis=1, keepdims=True)
                    m_s[e, rs, :] = m_new
                    al_s[e, rs, :] = alpha
                    pb_s[e, rs, :] = p.astype(BF16)
            if len(heads) == 2:
                pv = jnp.where(half == 0, _dot(pb_s[0], v_v), _dot(pb_s[1], v_v))
                acc_s[...] = jnp.where(half == 0, al_s[0], al_s[1]) * acc_s[...] + pv
            else:
                e = heads[0]
                acc_s[...] = jnp.where(half == e, al_s[e] * acc_s[...] + _dot(pb_s[e], v_v), acc_s[...])

        below = j < i

        @pl.when(below & live[0] & live[1])
        def _():
            step((0, 1), False)

        @pl.when(below & live[0] & jnp.logical_not(live[1]))
        def _():
            step((0,), False)

        @pl.when(below & jnp.logical_not(live[0]) & live[1])
        def _():
            step((1,), False)

        @pl.when(j == i)
        def _():
            step((0, 1), True)
            o_ref[...] = (acc_s[...] / jnp.where(half == 0, l_s[0], l_s[1])).astype(BF16)
            rows = pl.ds(pl.multiple_of(i * blk, blk), blk)
            lse_ref[0, :, rows] = jnp.concatenate(
                [_col_to_row(m_s[e] + jnp.log(l_s[e]) * LOG2E) for e in range(2)], axis=0)

    def col_block(first, inner):
        def index(p, n, it_r, jt_r, *_):
            return ((jt_r[n] if inner else it_r[n]), first + p)
        return pl.BlockSpec((blk, LANES), index)

    pair_stat = (2, blk, LANES)
    grid_spec = pltpu.PrefetchScalarGridSpec(
        num_scalar_prefetch=2 + len(tables), grid=(n_pair, int(it.shape[0])),
        in_specs=[col_block(0, False), col_block(n_pair, True), col_block(2 * n_pair, True),
                  pl.BlockSpec((blk, LANES), lambda p, n, it_r, jt_r, *_: (it_r[n], 0)),
                  pl.BlockSpec((2, 1, blk), lambda p, n, it_r, jt_r, *_: (p, 0, jt_r[n]))],
        out_specs=[col_block(0, False), pl.BlockSpec((1, 2, n_tok), lambda p, n, it_r, jt_r, *_: (p, 0, 0))],
        scratch_shapes=[pltpu.VMEM(pair_stat, F32), pltpu.VMEM(pair_stat, F32), pltpu.VMEM((blk, LANES), F32),
                        pltpu.VMEM(pair_stat, F32), pltpu.VMEM(pair_stat, BF16), pltpu.VMEM(pair_stat, F32),
                        pltpu.VMEM((2, blk, blk), F32), pltpu.VMEM((2, blk, blk), BF16)],
    )
    return pl.pallas_call(
        kern, name=name, grid_spec=grid_spec,
        out_shape=[jax.ShapeDtypeStruct((n_tok, FOX_W), BF16), jax.ShapeDtypeStruct((n_pair, 2, n_tok), F32)],
        compiler_params=_params(2),
    )(it, jt, *tables, qkv, qkv, qkv, c2, ct2)


def _flash_delta(name, do, o):
    n_tok, tm = do.shape[0], min(ROW_BLOCK, do.shape[0])

    def body(i, n, do_ref, o_ref, d_ref):
        prod = do_ref[...].astype(F32) * o_ref[...].astype(F32)
        head = (lax.broadcasted_iota(jnp.int32, (FOX_W, LANES), 0) // FOX_D
                == lax.broadcasted_iota(jnp.int32, (FOX_W, LANES), 1)).astype(BF16)
        hi = prod.astype(BF16)
        r1 = prod - hi.astype(F32)
        mid = r1.astype(BF16)
        lo = (r1 - mid.astype(F32)).astype(BF16)
        d_ref[...] = _dot(hi, head) + _dot(mid, head) + _dot(lo, head)

    return _rowcall(name, body, n_tok, tm, [do, o], [], [], [(LANES, F32)], [])[0]


def _flash_bwd(name, qkv, do, r_row, d_row, c2, tables):
    n_tok = qkv.shape[0]
    blk = min(FLASH_BLOCK, n_tok)
    nb = n_tok // blk
    n_pair = FOX_H // 2
    jt, it = _tri_table(nb, False)

    def kern(jt_ref, it_ref, qn_ref, kn_ref, dg_ref, cf_ref, cl_ref, k_ref, v_ref, q_ref, do_ref, rr_ref, dr_ref, c_ref,
             dq_ref, dk_ref, dv_ref, dc_ref, km_s, vm_s, ck_s, dck_s):
        n = pl.program_id(1)
        j, i = jt_ref[n], it_ref[n]
        first = _pair_half(blk) == 0
        live = _block_is_live((qn_ref, kn_ref, dg_ref, cf_ref, cl_ref), pl.program_id(0), i, j)

        @pl.when(n == 0)
        def _():
            dq_ref[...] = jnp.zeros(dq_ref.shape, F32)
            dc_ref[...] = jnp.zeros(dc_ref.shape, F32)

        @pl.when(i == j)
        def _():
            dk_ref[...] = jnp.zeros(dk_ref.shape, F32)
            dv_ref[...] = jnp.zeros(dv_ref.shape, F32)
            dck_s[...] = jnp.zeros(dck_s.shape, F32)
            for e in range(2):
                mine = first if e == 0 else jnp.logical_not(first)
                km_s[e] = jnp.where(mine, k_ref[...], jnp.zeros_like(k_ref[...]))
                vm_s[e] = jnp.where(mine, v_ref[...], jnp.zeros_like(v_ref[...]))
                ck_s[e] = _lane_pick(c_ref[...], 2 * pl.program_id(0) + e)

        def step(heads, on_diagonal):
            q_v, do_v, k_v = q_ref[...], do_ref[...], k_ref[...]
            dvs, dks, dqs, sums = {}, {}, {}, {}
            for e in heads:
                t = _dot_nt(km_s[e], q_v) * (FOX_SCALE * LOG2E) - _wide(ck_s[e], blk) - rr_ref[e]
                if on_diagonal:
                    keep = (lax.broadcasted_iota(jnp.int32, (blk, blk), 1)
                            >= lax.broadcasted_iota(jnp.int32, (blk, blk), 0))
                    t = jnp.where(keep, t, -jnp.inf)
                p = jnp.exp2(t)
                ds = p * (_dot_nt(vm_s[e], do_v) - dr_ref[e])
                p_b, ds_b = p.astype(BF16), ds.astype(BF16)
                dvs[e] = _dot(p_b, do_v)
                dks[e] = _dot(ds_b, q_v)
                dqs[e] = _dot_tn(ds_b, k_v)
                dck_s[e] += jnp.broadcast_to(jnp.sum(ds, axis=1, keepdims=True), (blk, LANES))
                sums[e] = jnp.sum(ds, axis=0, keepdims=True)
            rows = pl.ds(pl.multiple_of(i * blk, blk), blk)

            def merged(parts):
                zero = jnp.zeros((blk, LANES), F32)
                return jnp.where(first, parts.get(0, zero), parts.get(1, zero))

            dv_ref[...] += merged(dvs)
            dk_ref[...] += merged(dks) * FOX_SCALE
            dq_ref[rows, :] += merged(dqs) * FOX_SCALE
            none = jnp.zeros((1, blk), F32)
            dc_ref[0, :, rows] += jnp.concatenate([sums.get(0, none), sums.get(1, none)], axis=0)

        above = i > j

        @pl.when(above & live[0] & live[1])
        def _():
            step((0, 1), False)

        @pl.when(above & live[0] & jnp.logical_not(live[1]))
        def _():
            step((0,), False)

        @pl.when(above & jnp.logical_not(live[0]) & live[1])
        def _():
            step((1,), False)

        @pl.when(i == j)
        def _():
            step((0, 1), True)

        @pl.when(i == nb - 1)
        def _():
            keys = pl.ds(pl.multiple_of(j * blk, blk), blk)
            dc_ref[0, :, keys] -= jnp.concatenate([_col_to_row(dck_s[e]) for e in range(2)], axis=0)

    def col_block(first_col, inner):
        def index(p, n, jt_r, it_r, *_):
            return ((it_r[n] if inner else jt_r[n]), first_col + p)
        return pl.BlockSpec((blk, LANES), index)

    def row_stat():
        return pl.BlockSpec((2, 1, blk), lambda p, n, jt_r, it_r, *_: (p, 0, it_r[n]))

    pair_stat = (2, blk, LANES)
    grid_spec = pltpu.PrefetchScalarGridSpec(
        num_scalar_prefetch=2 + len(tables), grid=(n_pair, int(jt.shape[0])),
        in_specs=[col_block(n_pair, False), col_block(2 * n_pair, False), col_block(0, True), col_block(0, True),
                  row_stat(), row_stat(), pl.BlockSpec((blk, LANES), lambda p, n, jt_r, it_r, *_: (jt_r[n], 0))],
        out_specs=[pl.BlockSpec((n_tok, LANES), lambda p, n, jt_r, it_r, *_: (0, p)),
                   col_block(0, False), col_block(0, False),
                   pl.BlockSpec((1, 2, n_tok), lambda p, n, jt_r, it_r, *_: (p, 0, 0))],
        scratch_shapes=[pltpu.VMEM(pair_stat, BF16), pltpu.VMEM(pair_stat, BF16), pltpu.VMEM(pair_stat, F32),
                        pltpu.VMEM(pair_stat, F32)],
    )
    wide = jax.ShapeDtypeStruct((n_tok, FOX_W), F32)
    return pl.pallas_call(
        kern, name=name, grid_spec=grid_spec,
        out_shape=[wide, wide, wide, jax.ShapeDtypeStruct((n_pair, 2, n_tok), F32)],
        compiler_params=_params(2),
    )(jt, it, *tables, qkv, qkv, qkv, do, r_row, d_row, c2)


def _merge_fwd(name, x, ya, yb, yc, gl, wb, wo):
    n_tok, tm = x.shape[0], min(ROW_BLOCK, x.shape[0])

    def body(i, n, x_ref, ya_ref, yb_ref, yc_ref, gl_ref, wb_ref, wo_ref, xo_ref):
        merged = jnp.zeros((tm, D_MODEL), F32)
        for k, y_ref in enumerate((ya_ref, yb_ref, yc_ref)):
            gate = jax.nn.sigmoid(gl_ref[:, k * D_MODEL:(k + 1) * D_MODEL].astype(F32))
            merged = merged + gate * _dot(y_ref[...], wb_ref[k])
        xo_ref[...] = x_ref[...] + _dot(merged.astype(BF16), wo_ref[...])

    return _rowcall(name, body, n_tok, tm, [x, ya, yb, yc, gl], [], [wb, wo], [(D_MODEL, F32)], [])[0]


def _merge_bwd(name, dxo, ya, yb, yc, gl, wb, wb_t, wo_t):
    n_tok, tm = dxo.shape[0], min(ROW_BLOCK, dxo.shape[0])

    def body(i, n, dxo_ref, ya_ref, yb_ref, yc_ref, gl_ref, wb_ref, wbt_ref, wot_ref,
             mg_ref, dob_ref, dgl_ref, dbr_ref, dya_ref, dyb_ref, dyc_ref):
        dob = dxo_ref[...].astype(BF16)
        dob_ref[...] = dob
        dm = _dot(dob, wot_ref[...])
        merged = jnp.zeros((tm, D_MODEL), F32)
        for k, (y_ref, dy_ref) in enumerate(((ya_ref, dya_ref), (yb_ref, dyb_ref), (yc_ref, dyc_ref))):
            cols = slice(k * D_MODEL, (k + 1) * D_MODEL)
            gate = jax.nn.sigmoid(gl_ref[:, cols].astype(F32))
            br = _dot(y_ref[...], wb_ref[k])
            merged = merged + gate * br
            dgl_ref[:, cols] = (dm * br * gate * (1.0 - gate)).astype(BF16)
            dbr = (dm * gate).astype(BF16)
            dbr_ref[:, cols] = dbr
            dy_ref[...] = _dot(dbr, wbt_ref[k]).astype(BF16)
        mg_ref[...] = merged.astype(BF16)

    return _rowcall(name, body, n_tok, tm, [dxo, ya, yb, yc, gl], [], [wb, wb_t, wo_t],
                    [(D_MODEL, BF16), (D_MODEL, BF16), (3 * D_MODEL, BF16), (3 * D_MODEL, BF16),
                     (CONV_W, BF16), (SG_W, BF16), (FOX_W, BF16)], [])


def _mem_fwd(name, mem, gain, wk, wv):
    n_mem = mem.shape[0]

    def kern(m_ref, g_ref, wk_ref, wv_ref, mn_ref, k_ref, v_ref):
        mh, _ = _rms_stats(m_ref[...])
        mn = (mh * g_ref[...]).astype(BF16)
        mn_ref[...] = mn
        k_ref[...] = _dot(mn, wk_ref[...]).astype(BF16)
        v_ref[...] = _dot(mn, wv_ref[...]).astype(BF16)

    shp = jax.ShapeDtypeStruct((n_mem, D_MODEL), BF16)
    return pl.pallas_call(
        kern, name=name, grid=(1,),
        in_specs=[_full_spec(mem.shape), _full_spec(gain.shape), _full_spec(wk.shape), _full_spec(wv.shape)],
        out_specs=[_full_spec(shp.shape)] * 3, out_shape=[shp] * 3, compiler_params=_params(1),
    )(mem, gain, wk, wv)


def _mem_bwd(name, mem, gain, dkx, dvx, wk_t, wv_t):
    n_mem = mem.shape[0]

    def kern(m_ref, g_ref, dk_ref, dv_ref, wkt_ref, wvt_ref, dkb_ref, dvb_ref, dg_ref):
        mh, _ = _rms_stats(m_ref[...])
        dkb, dvb = dk_ref[...].astype(BF16), dv_ref[...].astype(BF16)
        dkb_ref[...] = dkb
        dvb_ref[...] = dvb
        dm = _dot(dkb, wkt_ref[...]) + _dot(dvb, wvt_ref[...])
        dg_ref[...] = jnp.sum(dm * mh, axis=0, keepdims=True)

    shp = jax.ShapeDtypeStruct((n_mem, D_MODEL), BF16)
    args = (mem, gain, dkx, dvx, wk_t, wv_t)
    return pl.pallas_call(
        kern, name=name, grid=(1,), in_specs=[_full_spec(a.shape) for a in args],
        out_specs=[_full_spec(shp.shape)] * 2 + [_full_spec((1, D_MODEL))],
        out_shape=[shp, shp, jax.ShapeDtypeStruct((1, D_MODEL), F32)], compiler_params=_params(1),
    )(*args)


def _xa_probs(q_b, kx_ref, hd):
    cols = slice(hd * XA_D, (hd + 1) * XA_D)
    s = _dot_nt(q_b[:, cols], kx_ref[:, cols]) * (XA_D ** -0.5)
    e = jnp.exp(s - jnp.max(s, axis=1, keepdims=True))
    return e / jnp.sum(e, axis=1, keepdims=True)


def _xa_fwd(name, x, gain, kx, vx, wq, wo):
    n_tok, tm = x.shape[0], min(ROW_BLOCK, x.shape[0])

    def body(i, n, x_ref, g_ref, kx_ref, vx_ref, wq_ref, wo_ref, xo_ref):
        x_v = x_ref[...]
        xh, _ = _rms_stats(x_v)
        q_b = _dot((xh * g_ref[...]).astype(BF16), wq_ref[...]).astype(BF16)
        o = jnp.concatenate(
            [_dot(_xa_probs(q_b, kx_ref, hd).astype(BF16), vx_ref[:, hd * XA_D:(hd + 1) * XA_D]) for hd in range(XA_H)],
            axis=1)
        xo_ref[...] = x_v + _dot(o.astype(BF16), wo_ref[...])

    return _rowcall(name, body, n_tok, tm, [x], [gain, kx, vx], [wq, wo], [(D_MODEL, F32)], [])[0]


def _xa_bwd(name, x, dxo, gain, kx, vx, wq, wq_t, wo_t):
    n_tok, tm = x.shape[0], min(ROW_BLOCK, x.shape[0])
    n_mem = kx.shape[0]

    def body(i, n, x_ref, dxo_ref, g_ref, kx_ref, vx_ref, wq_ref, wqt_ref, wot_ref,
             dx_ref, h_ref, o_ref, dq_ref, dy_ref, dkx_ref, dvx_ref, dg_ref):
        g = g_ref[...]
        xh, r = _rms_stats(x_ref[...])
        h = (xh * g).astype(BF16)
        h_ref[...] = h
        q_b = _dot(h, wq_ref[...]).astype(BF16)
        dxo_v = dxo_ref[...]
        dy = dxo_v.astype(BF16)
        dy_ref[...] = dy
        do = _dot(dy, wot_ref[...])
        for hd in range(XA_H):
            cols = slice(hd * XA_D, (hd + 1) * XA_D)
            p = _xa_probs(q_b, kx_ref, hd)
            p_b = p.astype(BF16)
            o_ref[:, cols] = _dot(p_b, vx_ref[:, cols]).astype(BF16)
            do_h = do[:, cols].astype(BF16)
            dvx_ref[:, cols] += _dot_tn(p_b, do_h)
            dp = _dot_nt(do_h, vx_ref[:, cols])
            ds = p * (dp - jnp.sum(dp * p, axis=1, keepdims=True))
            ds_b = (ds * (XA_D ** -0.5)).astype(BF16)
            dq_ref[:, cols] = _dot(ds_b, kx_ref[:, cols]).astype(BF16)
            dkx_ref[:, cols] += _dot_tn(ds_b, q_b[:, cols])
        dh = _dot(dq_ref[...], wqt_ref[...])
        dx, dg = _rms_bwd(xh, r, g, dh)
        dx_ref[...] = dxo_v + dx
        dg_ref[...] += dg

    return _rowcall(name, body, n_tok, tm, [x, dxo], [gain, kx, vx], [wq, wq_t, wo_t],
                    [(D_MODEL, F32), (D_MODEL, BF16), (D_MODEL, BF16), (D_MODEL, BF16), (D_MODEL, BF16)],
                    [((n_mem, D_MODEL), F32), ((n_mem, D_MODEL), F32), ((1, D_MODEL), F32)])


def _loss_head(name, x, target, gain):
    n_tok, tm = x.shape[0], min(ROW_BLOCK, x.shape[0])

    def body(i, n, x_ref, t_ref, g_ref, dx_ref, loss_ref, dg_ref):
        g = g_ref[...]
        xh, r = _rms_stats(x_ref[...])
        err = xh * g - t_ref[...]
        loss_ref[...] += 0.5 * jnp.sum(jnp.sum(err * err, axis=1, keepdims=True) / D_MODEL, axis=0, keepdims=True)
        dx, dg = _rms_bwd(xh, r, g, err / D_MODEL)
        dx_ref[...] = dx
        dg_ref[...] += dg

    return _rowcall(name, body, n_tok, tm, [x, target], [gain], [], [(D_MODEL, F32)],
                    [((8, LANES), F32), ((1, D_MODEL), F32)])


def _adamw(name, w, g, m, v):
    rows, cols = w.shape
    tr = 128 if rows % 128 == 0 else rows

    def kern(w_ref, g_ref, m_ref, v_ref, d_ref, mo_ref, vo_ref):
        g_v = g_ref[...]
        m_new = ADAM_B1 * m_ref[...] + (1.0 - ADAM_B1) * g_v
        v_new = ADAM_B2 * v_ref[...] + (1.0 - ADAM_B2) * (g_v * g_v)
        m_hat = m_new / (1.0 - ADAM_B1 ** ADAM_STEP)
        v_hat = v_new / (1.0 - ADAM_B2 ** ADAM_STEP)
        d_ref[...] = -ADAM_LR * (m_hat / (jnp.sqrt(v_hat) + ADAM_EPS) + ADAM_WD * w_ref[...])
        mo_ref[...] = m_new
        vo_ref[...] = v_new

    spec = pl.BlockSpec((tr, cols), lambda i: (i, 0))
    shp = jax.ShapeDtypeStruct((rows, cols), F32)
    return pl.pallas_call(kern, name=name, grid=(rows // tr,), in_specs=[spec] * 4, out_specs=[spec] * 3,
                          out_shape=[shp] * 3, compiler_params=_params(1))(w, g, m, v)


def _add_pair(name, gp, got, core):
    _, n, rows, cols = gp.shape
    spec = pl.BlockSpec((1, PACK_ROW_BLOCK, cols), lambda j, r, core_ref: (j, r, 0))
    mine = pl.BlockSpec((1, PACK_ROW_BLOCK, cols), lambda j, r, core_ref: (core_ref[0] * n + j, r, 0))

    def kern(core_ref, a_ref, b_ref, o_ref):
        o_ref[...] = (a_ref[...] + b_ref[...]).astype(BF16)

    grid_spec = pltpu.PrefetchScalarGridSpec(num_scalar_prefetch=1, grid=(n, rows // PACK_ROW_BLOCK),
                                             in_specs=[mine, spec], out_specs=spec)
    return pl.pallas_call(kern, name=name, grid_spec=grid_spec, out_shape=jax.ShapeDtypeStruct(got.shape, BF16),
                          compiler_params=_params(2))(core, gp.reshape(2 * n, rows, cols), got)


def _sum_parts(name, s, got, chip):
    _, rows, cols = s.shape
    tr = PACK_ROW_BLOCK

    def kern(chip_ref, s_ref, g_ref, o_ref):
        acc = s_ref[0].astype(F32)
        for k in range(3):
            acc = acc + g_ref[k].astype(F32)
        o_ref[...] = acc

    grid_spec = pltpu.PrefetchScalarGridSpec(
        num_scalar_prefetch=1, grid=(rows // tr,),
        in_specs=[pl.BlockSpec((1, tr, cols), lambda r, chip_ref: (chip_ref[0], r, 0)),
                  pl.BlockSpec((3, tr, cols), lambda r, chip_ref: (0, r, 0))],
        out_specs=pl.BlockSpec((tr, cols), lambda r, chip_ref: (r, 0)))
    return pl.pallas_call(kern, name=name, grid_spec=grid_spec, out_shape=jax.ShapeDtypeStruct((rows, cols), F32),
                          compiler_params=_params(1))(chip, s, got)


def _sum_slots(name, a, tr):
    n, rows, cols = a.shape

    def kern(a_ref, o_ref):
        acc = a_ref[0].astype(F32)
        for k in range(1, n):
            acc = acc + a_ref[k].astype(F32)
        o_ref[...] = acc

    return pl.pallas_call(kern, name=name, grid=(rows // tr,),
                          in_specs=[pl.BlockSpec((n, tr, cols), lambda r: (0, r, 0))],
                          out_specs=pl.BlockSpec((tr, cols), lambda r: (r, 0)),
                          out_shape=jax.ShapeDtypeStruct((rows, cols), F32), compiler_params=_params(1))(a)


_ANY = pl.BlockSpec(memory_space=pl.ANY)


COMM_CHUNKS = 13


def _place():
    x, y, c = lax.axis_index("x"), lax.axis_index("y"), lax.axis_index("c")
    chips = [(1 - x, y), (x, 1 - y), (1 - x, 1 - y)]
    return x, y, c, chips


def _rows(ref, q, n):
    step = ref.shape[0] // n
    return ref.at[pl.ds(q * step, step)]


def _gather_chips(name, w):
    n = COMM_CHUNKS

    def kern(w_ref, o_ref, send_sems, recv_sems):
        x, y, c, chips = _place()
        me, sib = 2 * x + y, (x, y, 1 - c)

        def copy(k, src, dst, to):
            return pltpu.make_async_remote_copy(src_ref=src, dst_ref=dst, send_sem=send_sems.at[k],
                                                recv_sem=recv_sems.at[k], device_id=to, device_id_type=MESH_ID)

        first = [copy(k * n + q, _rows(w_ref.at[c], q, n), _rows(o_ref.at[me, c], q, n), (cx, cy, c))
                 for k, (cx, cy) in enumerate(chips) for q in range(n)]
        for cp in first:
            cp.start()
        passed = []
        for k, (cx, cy) in enumerate(chips):
            for q in range(n):
                landed = _rows(o_ref.at[2 * cx + cy, c], q, n)
                copy(k * n + q, landed, landed, (x, y, c)).wait_recv()
                passed.append(copy((3 + k) * n + q, landed, landed, sib))
                passed[-1].start()
        for k, (cx, cy) in enumerate(chips):
            for q in range(n):
                theirs = _rows(o_ref.at[2 * cx + cy, 1 - c], q, n)
                copy((3 + k) * n + q, theirs, theirs, (x, y, c)).wait_recv()
        for cp in first + passed:
            cp.wait_send()

    return pl.pallas_call(
        kern, name=name, in_specs=[_ANY], out_specs=_ANY,
        out_shape=jax.ShapeDtypeStruct((N_CHIPS,) + w.shape, w.dtype),
        scratch_shapes=[pltpu.SemaphoreType.DMA((6 * n,)), pltpu.SemaphoreType.DMA((6 * n,))],
        compiler_params=pltpu.CompilerParams(has_side_effects=True),
    )(w)


def _gather_all(name, v):
    def kern(v_ref, o_ref, send_sems, recv_sems, local_sem):
        x, y, c, _ = _place()
        me = 4 * x + 2 * y + c
        mine = pltpu.make_async_copy(v_ref, o_ref.at[me], local_sem)
        mine.start()
        peers = []
        for k in range(1, 8):
            px = 1 - x if k & 4 else x
            py = 1 - y if k & 2 else y
            pc = 1 - c if k & 1 else c
            peers.append((px, py, pc))
        copies = [pltpu.make_async_remote_copy(src_ref=v_ref, dst_ref=o_ref.at[me], send_sem=send_sems.at[k],
                                               recv_sem=recv_sems.at[k], device_id=p, device_id_type=MESH_ID)
                  for k, p in enumerate(peers)]
        for cp in copies:
            cp.start()
        for k, (px, py, pc) in enumerate(peers):
            pltpu.make_async_remote_copy(src_ref=v_ref, dst_ref=o_ref.at[4 * px + 2 * py + pc], send_sem=send_sems.at[k],
                                         recv_sem=recv_sems.at[k], device_id=(x, y, c), device_id_type=MESH_ID).wait_recv()
        for cp in copies:
            cp.wait_send()
        mine.wait()

    return pl.pallas_call(
        kern, name=name, in_specs=[_ANY], out_specs=_ANY, out_shape=jax.ShapeDtypeStruct((8,) + v.shape, v.dtype),
        scratch_shapes=[pltpu.SemaphoreType.DMA((7,)), pltpu.SemaphoreType.DMA((7,)), pltpu.SemaphoreType.DMA(())],
        compiler_params=pltpu.CompilerParams(has_side_effects=True),
    )(v)


def _swap_halves(name, gp):
    n = COMM_CHUNKS
    n_slots = gp.shape[1]

    def kern(g_ref, got_ref, send_sems, recv_sems):
        x, y, c, _ = _place()
        swaps = [pltpu.make_async_remote_copy(src_ref=_rows(g_ref.at[1 - c, j], q, n), dst_ref=_rows(got_ref.at[j], q, n),
                                              send_sem=send_sems.at[j * n + q], recv_sem=recv_sems.at[j * n + q],
                                              device_id=(x, y, 1 - c), device_id_type=MESH_ID)
                 for j in range(n_slots) for q in range(n)]
        for cp in swaps:
            cp.start()
        for cp in swaps:
            cp.wait()

    return pl.pallas_call(
        kern, name=name, in_specs=[_ANY], out_specs=_ANY, out_shape=jax.ShapeDtypeStruct(gp.shape[1:], gp.dtype),
        scratch_shapes=[pltpu.SemaphoreType.DMA((n_slots * n,)), pltpu.SemaphoreType.DMA((n_slots * n,))],
        compiler_params=pltpu.CompilerParams(has_side_effects=True),
    )(gp)


def _scatter_chips(name, s):
    n = COMM_CHUNKS

    def kern(s_ref, o_ref, send_sems, recv_sems):
        x, y, c, chips = _place()
        copies = [pltpu.make_async_remote_copy(src_ref=_rows(s_ref.at[2 * cx + cy], q, n), dst_ref=_rows(o_ref.at[k], q, n),
                                               send_sem=send_sems.at[k * n + q], recv_sem=recv_sems.at[k * n + q],
                                               device_id=(cx, cy, c), device_id_type=MESH_ID)
                  for k, (cx, cy) in enumerate(chips) for q in range(n)]
        for cp in copies:
            cp.start()
        for k in range(3):
            for q in range(n):
                landed = _rows(o_ref.at[k], q, n)
                pltpu.make_async_remote_copy(src_ref=landed, dst_ref=landed, send_sem=send_sems.at[k * n + q],
                                             recv_sem=recv_sems.at[k * n + q], device_id=(x, y, c),
                                             device_id_type=MESH_ID).wait_recv()
        for cp in copies:
            cp.wait_send()

    return pl.pallas_call(
        kern, name=name, in_specs=[_ANY], out_specs=_ANY, out_shape=jax.ShapeDtypeStruct((3,) + s.shape[1:], s.dtype),
        scratch_shapes=[pltpu.SemaphoreType.DMA((3 * n,)), pltpu.SemaphoreType.DMA((3 * n,))],
        compiler_params=pltpu.CompilerParams(has_side_effects=True),
    )(s)


def _join_halves(name, r):
    n = COMM_CHUNKS

    def kern(r_ref, o_ref, send_sems, recv_sems):
        x, y, c, _ = _place()
        swaps = [pltpu.make_async_remote_copy(src_ref=_rows(r_ref, q, n), dst_ref=_rows(o_ref, q, n),
                                              send_sem=send_sems.at[q], recv_sem=recv_sems.at[q],
                                              device_id=(x, y, 1 - c), device_id_type=MESH_ID) for q in range(n)]
        for cp in swaps:
            cp.start()
        for cp in swaps:
            cp.wait()

    return pl.pallas_call(
        kern, name=name, in_specs=[_ANY], out_specs=_ANY, out_shape=jax.ShapeDtypeStruct(r.shape, r.dtype),
        scratch_shapes=[pltpu.SemaphoreType.DMA((n,))] * 2, compiler_params=pltpu.CompilerParams(has_side_effects=True),
    )(r)


PACK_ALIGN = 16


def _exchange_rows(nm, shard_shape):
    if BIG_KIND[nm] == "cols":
        rows = shard_shape[-1]
    else:
        rows = 1
        for s in shard_shape[1:]:
            rows *= s
        rows //= PACK_COLS
    return -(-rows // PACK_ALIGN) * PACK_ALIGN


def _to_exchange(nm, a, transposed=False):
    if BIG_KIND[nm] == "cols":
        if not transposed:
            a = jnp.swapaxes(a, 1, 2)
        rows = -(-a.shape[1] // PACK_ALIGN) * PACK_ALIGN
        a = jnp.pad(a, ((0, 0), (0, rows - a.shape[1]), (0, 0)))
        return a.reshape(a.shape[0] * rows, PACK_COLS)
    return a.reshape(a.shape[0] * _exchange_rows(nm, a.shape), PACK_COLS)


def _from_exchange(nm, rows2d, shard_shape):
    lead = rows2d.shape[:-2]
    n_layers = shard_shape[0]
    if BIG_KIND[nm] == "cols":
        a = rows2d.reshape(lead + (n_layers, -1, PACK_COLS))
        return lax.slice_in_dim(a, 0, shard_shape[-1], axis=a.ndim - 2)
    return rows2d.reshape(lead + tuple(shard_shape))


def _pack(pieces, dtype):
    slab = jnp.concatenate([p.astype(dtype) for p in pieces])
    half = -(-slab.shape[0] // (2 * PACK_ROW_BLOCK)) * PACK_ROW_BLOCK
    slab = jnp.pad(slab, ((0, 2 * half - slab.shape[0]), (0, 0)))
    return slab.reshape(2, half, PACK_COLS)


def _unpack(packed, row_counts):
    lead = packed.shape[:-3]
    slab = packed.reshape(lead + (-1, PACK_COLS))
    out, off = [], 0
    for rows in row_counts:
        out.append(lax.slice_in_dim(slab, off, off + rows, axis=slab.ndim - 2))
        off += rows
    return out


def _t(a):
    return jnp.swapaxes(a, -1, -2)


def _rows_from_lanes(a):
    return a[:, :FOX_H].T[:, None, :]


def _lanes_from_heads(a):
    return jnp.pad(a[:, 0, :].T, ((0, 0), (0, LANES - FOX_H)))


def _layer_weights(wts, small, l):
    w_in_t = wts["w_in"][l]
    w = {
        "wf_t": jnp.pad(w_in_t[W1_COLS:W1_COLS + FOX_H], ((0, LANES - FOX_H), (0, 0))),
        "w1_t": w_in_t[:W1_COLS], "wgl_t": w_in_t[W1_COLS + FOX_H:],
    }
    for nm, _ in BIG:
        if nm != "w_in":
            w[nm + "_t" if BIG_KIND[nm] == "cols" else nm] = wts[nm][l]
    for nm in list(w):
        other = nm[:-2] if nm.endswith("_t") else nm + "_t"
        w[other] = _t(w[nm])
    for nm in ("ffn1_norm", "mix_norm", "sg_norm", "xa_norm", "mem_norm", "ffn2_norm"):
        w[nm] = small[nm][l][None, :]
    w["conv_w"] = jnp.pad(small["conv_w"][l], ((0, 5), (0, 0)))
    w["sg_w"] = small["sg_w"][l]
    w["sg_w_t"] = _t(small["sg_w"][l])
    w["sg_bias"] = jnp.repeat(small["sg_b"][l].T, CHUNK, axis=1)
    w["fox_b"] = jnp.pad(small["fox_b_f"][l][None, :], ((0, 0), (0, LANES - FOX_H)))
    return w


def _layer_fwd(l, x, mem, w):
    s = {"x0": x}
    x1, s["a1"], s["b1"] = _ffn_fwd(f"ffn1_fwd_{l}", x, w["ffn1_norm"], w["ffn1_w_gate"], w["ffn1_w_up"], w["ffn1_w_down"])
    s["x1"] = x1
    cin, sgin, qkv, fl, gl = _proj_fwd(f"proj_fwd_{l}", x1, w["mix_norm"], w["w1"], w["wf"], w["wgl"])
    ya = _conv_fwd(f"conv_fwd_{l}", cin, w["conv_w"])
    yb = _sg_fwd(f"sg_fwd_{l}", sgin, w["sg_norm"], w["sg_w"], w["sg_bias"])
    c2 = _forget_cumsum(f"forget_fwd_{l}", fl, w["fox_b"])
    tables = _prune_tables(_flash_bounds(f"flash_bounds_{l}", qkv), c2, min(FLASH_BLOCK, qkv.shape[0]))
    yc, lse2 = _flash_fwd(f"flash_fwd_{l}", qkv, c2, _rows_from_lanes(c2), tables)
    x2 = _merge_fwd(f"merge_fwd_{l}", x1, ya, yb, yc, gl, w["w_branch"], w["w_out"])
    s.update(cin=cin, sgin=sgin, qkv=qkv, fl=fl, gl=gl, ya=ya, yb=yb, yc=yc, c2=c2, lse2=lse2, x2=x2, tables=tables)
    s["mn"], s["kx"], s["vx"] = _mem_fwd(f"mem_fwd_{l}", mem, w["mem_norm"], w["xa_wk"], w["xa_wv"])
    x3 = _xa_fwd(f"xa_fwd_{l}", x2, w["xa_norm"], s["kx"], s["vx"], w["xa_wq"], w["xa_wo"])
    s["x3"] = x3
    x4, s["a2"], s["b2"] = _ffn_fwd(f"ffn2_fwd_{l}", x3, w["ffn2_norm"], w["ffn2_w_gate"], w["ffn2_w_up"], w["ffn2_w_down"])
    return x4, s


def _ffn_grads(tag, x, dxo, a, b, w, pre):
    dx, da, db, sv, h, dy, dg = _ffn_bwd(f"{pre}_bwd_{tag}", x, dxo, a, b, w[pre + "_norm"], w[pre + "_w_down_t"],
                                         w[pre + "_w_gate_t"], w[pre + "_w_up_t"])
    g = {
        pre + "_norm": dg[0],
        pre + "_w_gate": _mm_tn(f"{pre}_dwg_{tag}", da, h, D_FF, D_MODEL, tk=FF_SPLIT, tn=D_MODEL),
        pre + "_w_up": _mm_tn(f"{pre}_dwu_{tag}", db, h, D_FF, D_MODEL, tk=FF_SPLIT, tn=D_MODEL),
        pre + "_w_down": _mm_tn(f"{pre}_dwd_{tag}", sv, dy, D_FF, D_MODEL, tk=FF_SPLIT, tn=D_MODEL),
    }
    return dx, g


def _layer_bwd(l, dx, mem, w, s):
    g = {}
    dx, gf = _ffn_grads(l, s["x3"], dx, s["a2"], s["b2"], w, "ffn2")
    g.update(gf)

    dx, h, o, dq, dy, dkx, dvx, dg = _xa_bwd(f"xa_bwd_{l}", s["x2"], dx, w["xa_norm"], s["kx"], s["vx"], w["xa_wq"],
                                              w["xa_wq_t"], w["xa_wo_t"])
    g["xa_norm"] = dg[0]
    g["xa_wq"] = _mm_tn(f"xa_dwq_{l}", h, dq, D_MODEL, D_MODEL, tk=D_MODEL, tn=D_MODEL)
    g["xa_wo"] = _mm_tn(f"xa_dwo_{l}", o, dy, D_MODEL, D_MODEL, tk=D_MODEL, tn=D_MODEL)
    dkb, dvb, dgm = _mem_bwd(f"mem_bwd_{l}", mem, w["mem_norm"], dkx, dvx, w["xa_wk_t"], w["xa_wv_t"])
    g["mem_norm"] = dgm[0]
    g["xa_wk"] = _mm_tn(f"xa_dwk_{l}", s["mn"], dkb, D_MODEL, D_MODEL, tk=D_MODEL, tn=D_MODEL)
    g["xa_wv"] = _mm_tn(f"xa_dwv_{l}", s["mn"], dvb, D_MODEL, D_MODEL, tk=D_MODEL, tn=D_MODEL)

    mg, dob, dgl, dbr, dya, dyb, dyc = _merge_bwd(f"merge_bwd_{l}", dx, s["ya"], s["yb"], s["yc"], s["gl"], w["w_branch"],
                                                  w["w_branch_t"], w["w_out_t"])
    g["w_out"] = _mm_tn(f"dwout_{l}", mg, dob, D_MODEL, D_MODEL, tk=D_MODEL, tn=D_MODEL)
    g["w_branch"] = jnp.stack([
        _mm_tn(f"dwbranch{k}_{l}", y, dbr, CONV_W, D_MODEL, tk=CONV_W, tn=D_MODEL, y_off=k * D_MODEL)
        for k, y in enumerate((s["ya"], s["yb"], s["yc"]))])

    dcin, dcw = _conv_bwd(f"conv_bwd_{l}", s["cin"], dya, w["conv_w"])
    g["conv_w"] = dcw[:3]
    dsg, dgn, dsw, dsb = _sg_bwd(f"sg_bwd_{l}", s["sgin"], dyb, w["sg_norm"], w["sg_w"], w["sg_w_t"], w["sg_bias"])
    g["sg_norm"], g["sg_w"], g["sg_b"] = dgn[0], dsw, dsb[:, :, 0]
    d_row = _rows_from_lanes(_flash_delta(f"flash_delta_{l}", dyc, s["yc"]))
    r_row = s["lse2"].reshape(FOX_H, 1, -1) - _rows_from_lanes(s["c2"])
    dq, dk, dv, dc_rows = _flash_bwd(f"flash_bwd_{l}", s["qkv"], dyc, r_row, d_row, s["c2"], s["tables"])
    dc = _lanes_from_heads(dc_rows.reshape(FOX_H, 1, -1))
    dfl, dbf = _forget_bwd(f"forget_bwd_{l}", dc, s["fl"], w["fox_b"])
    g["fox_b_f"] = dbf[0, :FOX_H]

    dx, h, dqkv, dfb, dg = _proj_bwd(f"proj_bwd_{l}", s["x1"], dx, dcin, dsg, dq, dk, dv, dfl, dgl, w["mix_norm"],
                                     w["w1_t"], w["wf_t"], w["wgl_t"])
    g["mix_norm"] = dg[0]
    c0, c1 = 3 * CONV_W, 3 * CONV_W + 2 * SG_W
    g["w_in"] = jnp.concatenate([
        _mm_tn(f"dwin_conv_{l}", dcin, h, c0, D_MODEL, tk=c0, tn=D_MODEL),
        _mm_tn(f"dwin_sg_{l}", dsg, h, c1 - c0, D_MODEL, tk=c1 - c0, tn=D_MODEL),
        _mm_tn(f"dwin_qkv_{l}", dqkv, h, 3 * FOX_W, D_MODEL, tk=3 * FOX_W, tn=D_MODEL),
        _mm_tn(f"dwin_f_{l}", dfb, h, LANES, D_MODEL, tk=LANES, tn=D_MODEL)[:FOX_H],
        _mm_tn(f"dwin_gl_{l}", dgl, h, 3 * D_MODEL, D_MODEL, tk=3 * D_MODEL // 2, tn=D_MODEL),
    ], axis=0)

    dx, gf = _ffn_grads(l, s["x0"], dx, s["a1"], s["b1"], w, "ffn1")
    g.update(gf)
    return dx, g


def _local_step(x, mem, target, wts, small):
    saved, lw = [], []
    for l in range(DEPTH):
        lw.append(_layer_weights(wts, small, l))
        x, s = _layer_fwd(l, x, mem, lw[l])
        saved.append(s)
    fin = small["final_norm"][None, :]
    dx, loss, dgf = _loss_head("loss_head", x, target, fin)
    layer_grads = [None] * DEPTH
    for l in reversed(range(DEPTH)):
        dx, layer_grads[l] = _layer_bwd(l, dx, mem, lw[l], saved[l])
    grads = {nm: jnp.stack([layer_grads[l][nm] for l in range(DEPTH)]) for nm in WEIGHTS if nm != "final_norm"}
    grads["final_norm"] = dgf[0]
    return loss[0, 0], dx, grads


def _small_slab(vals):
    rows = []
    for v in vals:
        flat = v.astype(F32).reshape(-1)
        n = -(-flat.shape[0] // LANES) * LANES
        rows.append(jnp.pad(flat, (0, n - flat.shape[0])).reshape(-1, LANES))
    slab = jnp.concatenate(rows)
    pad = -slab.shape[0] % 8
    return jnp.pad(slab, ((0, pad), (0, 0)))


def _small_unslab(slab, shapes):
    out, off = [], 0
    for shp in shapes:
        size = 1
        for s in shp:
            size *= s
        n_rows = -(-size // LANES)
        out.append(slab[off:off + n_rows].reshape(-1)[:size].reshape(shp))
        off += n_rows
    return out


def kernel(x, mem, ffn1_norm, ffn1_w_gate, ffn1_w_up, ffn1_w_down, mix_norm, w_in, conv_w, sg_norm, sg_w, sg_b, fox_b_f, w_branch, w_out, xa_norm, mem_norm, xa_wq, xa_wk, xa_wv, xa_wo, ffn2_norm, ffn2_w_gate, ffn2_w_up, ffn2_w_down, final_norm, loss_target, m_ffn1_norm, m_ffn1_w_gate, m_ffn1_w_up, m_ffn1_w_down, m_mix_norm, m_w_in, m_conv_w, m_sg_norm, m_sg_w, m_sg_b, m_fox_b_f, m_w_branch, m_w_out, m_xa_norm, m_mem_norm, m_xa_wq, m_xa_wk, m_xa_wv, m_xa_wo, m_ffn2_norm, m_ffn2_w_gate, m_ffn2_w_up, m_ffn2_w_down, m_final_norm, v_ffn1_norm, v_ffn1_w_gate, v_ffn1_w_up, v_ffn1_w_down, v_mix_norm, v_w_in, v_conv_w, v_sg_norm, v_sg_w, v_sg_b, v_fox_b_f, v_w_branch, v_w_out, v_xa_norm, v_mem_norm, v_xa_wq, v_xa_wk, v_xa_wv, v_xa_wo, v_ffn2_norm, v_ffn2_w_gate, v_ffn2_w_up, v_ffn2_w_down, v_final_norm):
    args = dict(locals())
    wv = {nm: args[nm] for nm in WEIGHTS}
    mv = {nm: args["m_" + nm] for nm in WEIGHTS}
    vv = {nm: args["v_" + nm] for nm in WEIGHTS}
    chip = 2 * lax.axis_index("x") + lax.axis_index("y")
    core = lax.axis_index("c")

    names = [nm for nm, _ in BIG]
    mine = {nm: _to_exchange(nm, wv[nm].astype(BF16)) for nm in names}
    row_counts = [mine[nm].shape[0] for nm in names]
    gathered = _gather_chips("gather_weights", _pack([mine[nm] for nm in names], BF16))
    wts = {}
    for nm, slab in zip(names, _unpack(gathered, row_counts)):
        p = _from_exchange(nm, jnp.where((jnp.arange(N_CHIPS) == chip)[:, None, None], mine[nm][None], slab), wv[nm].shape)
        if BIG_KIND[nm] == "branch":
            wts[nm] = jnp.concatenate([p[j] for j in range(N_CHIPS)], axis=3)
        else:
            wts[nm] = jnp.swapaxes(p, 0, 1).reshape(p.shape[1], N_CHIPS * p.shape[2], PACK_COLS)
    taps = _gather_all("gather_taps", _small_slab([conv_w]))
    taps = [_small_unslab(taps[2 * j], [conv_w.shape])[0] for j in range(N_CHIPS)]
    small = {nm: wv[nm] for nm in SMALL}
    small["conv_w"] = jnp.concatenate(taps, axis=2)

    loss, dx, grads = _local_step(x[0], mem[0], loss_target[0], wts, small)
    loss = lax.psum(loss, ("x", "y", "c"))

    small_shapes = [grads[nm].shape for nm in SMALL]
    slots = _gather_all("gather_small_grads", _small_slab([grads[nm] for nm in SMALL]))
    small_sum = _sum_slots("sum_small_grads", slots, slots.shape[1])
    small_g = dict(zip(SMALL, _small_unslab(small_sum, small_shapes)))
    taps_g = small_g["conv_w"]
    small_g["conv_w"] = lax.dynamic_slice_in_dim(taps_g, chip * conv_w.shape[2], conv_w.shape[2], axis=2)

    def grad_shard(nm, ax, j):
        g = grads[nm]
        ax = 1 if BIG_KIND[nm] == "cols" else ax
        size = g.shape[ax] // N_CHIPS
        return _to_exchange(nm, lax.slice_in_dim(g, j * size, (j + 1) * size, axis=ax), transposed=True)

    gp = jnp.stack([_pack([grad_shard(nm, ax, j) for nm, ax in BIG], F32) for j in range(N_CHIPS)], axis=1)
    got = _swap_halves("reduce_swap_cores", gp)
    chip_sum = _add_pair("reduce_add_cores", gp, got, core.astype(jnp.int32).reshape(1))
    parts = _scatter_chips("reduce_scatter_chips", chip_sum)
    half = _sum_parts("reduce_add_chips", chip_sum, parts, chip.astype(jnp.int32).reshape(1))
    other = _join_halves("reduce_join_cores", half)
    full = jnp.where(core == 0, jnp.stack([half, other]), jnp.stack([other, half]))
    big_g = {}
    for nm, slab in zip(names, _unpack(full, row_counts)):
        g = _from_exchange(nm, slab, wv[nm].shape)
        big_g[nm] = jnp.swapaxes(g, 1, 2) if BIG_KIND[nm] == "cols" else g

    g_out = {**small_g, **big_g}
    delta, new_m, new_v = {}, {}, {}
    for nm, _ in BIG:
        shp = wv[nm].shape
        two_d = (-1, shp[-1])
        d, m2, v2 = _adamw("adamw_" + nm, wv[nm].reshape(two_d), g_out[nm].reshape(two_d), mv[nm].reshape(two_d),
                           vv[nm].reshape(two_d))
        delta[nm], new_m[nm], new_v[nm] = d.reshape(shp), m2.reshape(shp), v2.reshape(shp)
    slab_shapes = [wv[nm].shape for nm in SMALL]
    d, m2, v2 = _adamw("adamw_small", _small_slab([wv[nm] for nm in SMALL]), _small_slab([g_out[nm] for nm in SMALL]),
                       _small_slab([mv[nm] for nm in SMALL]), _small_slab([vv[nm] for nm in SMALL]))
    for out, slab in ((delta, d), (new_m, m2), (new_v, v2)):
        out.update(zip(SMALL, _small_unslab(slab, slab_shapes)))

    return (loss, dx[None], *[g_out[nm] for nm in WEIGHTS], *[delta[nm] for nm in WEIGHTS],
            *[new_m[nm] for nm in WEIGHTS], *[new_v[nm] for nm in WEIGHTS])
```

```python
import functools

import jax
import jax.numpy as jnp
from jax import lax
from jax.experimental import pallas as pl
from jax.experimental.pallas import tpu as pltpu

F32, BF16 = jnp.float32, jnp.bfloat16
MESH_ID = pl.DeviceIdType.MESH

D_MODEL = 1024
DEPTH = 2
D_FF = 2816
CONV_W = 512
SG_W = 512
SG_G = 4
CHUNK = 128
FOX_H = 8
FOX_D = 64
FOX_W = FOX_H * FOX_D
FOX_SCALE = FOX_D ** -0.5
LOG2E = 1.4426950408889634
XA_H = 4
XA_D = D_MODEL // XA_H
N_CHIPS = 4
RMS_EPS = 1e-6
W1_COLS = 3 * CONV_W + 2 * SG_W + 3 * FOX_W
LANES = 128
HALO = 16

ADAM_LR, ADAM_B1, ADAM_B2, ADAM_EPS, ADAM_WD, ADAM_STEP = 0.001, 0.9, 0.999, 1e-08, 0.01, 10

ROW_BLOCK = 512
FFN_BWD_ROW_BLOCK = 256
FLASH_BLOCK = 512
FLASH_STRIP = 32
PRUNE_LOG2 = 40.0
FF_SPLIT = 1408
VMEM_LIMIT = 56 * 1024 * 1024
PACK_COLS = 1024
PACK_ROW_BLOCK = 208

BIG = (
    ("ffn1_w_gate", 2), ("ffn1_w_up", 2), ("ffn1_w_down", 1), ("w_in", 2), ("w_branch", 3), ("w_out", 1),
    ("xa_wq", 1), ("xa_wk", 1), ("xa_wv", 1), ("xa_wo", 1), ("ffn2_w_gate", 2), ("ffn2_w_up", 2), ("ffn2_w_down", 1),
)
BIG_KIND = {nm: ("branch" if nm == "w_branch" else "cols" if ax == 2 else "rows") for nm, ax in BIG}
SMALL = ("ffn1_norm", "mix_norm", "conv_w", "sg_norm", "sg_w", "sg_b", "fox_b_f", "xa_norm", "mem_norm", "ffn2_norm",
         "final_norm")
WEIGHTS = ("ffn1_norm", "ffn1_w_gate", "ffn1_w_up", "ffn1_w_down", "mix_norm", "w_in", "conv_w", "sg_norm", "sg_w",
           "sg_b", "fox_b_f", "w_branch", "w_out", "xa_norm", "mem_norm", "xa_wq", "xa_wk", "xa_wv", "xa_wo",
           "ffn2_norm", "ffn2_w_gate", "ffn2_w_up", "ffn2_w_down", "final_norm")


def _dot(a, b):
    return jnp.dot(a, b, preferred_element_type=F32)


def _dot_nt(a, b):
    return lax.dot_general(a, b, (((1,), (1,)), ((), ())), preferred_element_type=F32)


def _dot_tn(a, b):
    return lax.dot_general(a, b, (((0,), (0,)), ((), ())), preferred_element_type=F32)


def _rms_stats(x):
    r = lax.rsqrt(jnp.mean(x * x, axis=-1, keepdims=True) + RMS_EPS)
    return x * r, r


def _rms_bwd(xh, r, g, dy):
    dg = jnp.sum(dy * xh, axis=0, keepdims=True)
    dxh = dy * g
    dx = r * (dxh - xh * jnp.mean(dxh * xh, axis=-1, keepdims=True))
    return dx, dg


def _gelu(x):
    k = 0.7978845608028654
    t = jnp.tanh(k * (x + 0.044715 * x * x * x))
    return 0.5 * x * (1.0 + t), t


def _gelu_grad(x, t):
    k = 0.7978845608028654
    return 0.5 * (1.0 + t) + 0.5 * x * (1.0 - t * t) * k * (1.0 + 3.0 * 0.044715 * x * x)


def _split3_dot(tri, x):
    hi = x.astype(BF16)
    r1 = x - hi.astype(F32)
    mid = r1.astype(BF16)
    lo = (r1 - mid.astype(F32)).astype(BF16)
    return _dot(tri, hi) + _dot(tri, mid) + _dot(tri, lo)


def _params(n_grid):
    return pltpu.CompilerParams(dimension_semantics=("arbitrary",) * n_grid, vmem_limit_bytes=VMEM_LIMIT)


def _full_spec(shape):
    nd = len(shape)
    return pl.BlockSpec(tuple(shape), lambda *_: (0,) * nd)


def _rowcall(name, body, n_tokens, tm, rows, consts, residents, row_outs, acc_outs, scratch=()):
    n = n_tokens // tm
    rows = [r if isinstance(r, tuple) else (r, pl.BlockSpec((tm, r.shape[1]), lambda i: (i, 0))) for r in rows]
    n_in, n_w = len(rows) + len(consts), len(residents)
    n_out = len(row_outs) + len(acc_outs)
    in_specs = ([r[1] for r in rows] + [_full_spec(c.shape) for c in consts]
                + [pl.BlockSpec(memory_space=pl.ANY)] * n_w)
    out_shape = ([jax.ShapeDtypeStruct((n_tokens, c), dt) for c, dt in row_outs]
                 + [jax.ShapeDtypeStruct(s, dt) for s, dt in acc_outs])
    out_specs = ([pl.BlockSpec((tm, c), lambda i: (i, 0)) for c, _ in row_outs]
                 + [_full_spec(s) for s, _ in acc_outs])
    scratch_shapes = [pltpu.VMEM(w.shape, w.dtype) for w in residents]
    if n_w:
        scratch_shapes.append(pltpu.SemaphoreType.DMA((n_w,)))
    scratch_shapes += list(scratch)

    def kern(*refs):
        ins, w_hbm = refs[:n_in], refs[n_in:n_in + n_w]
        outs = refs[n_in + n_w:n_in + n_w + n_out]
        rest = refs[n_in + n_w + n_out:]
        w_vmem = rest[:n_w]
        extra = rest[n_w + 1:] if n_w else rest
        i = pl.program_id(0)

        @pl.when(i == 0)
        def _():
            copies = [pltpu.make_async_copy(w_hbm[k], w_vmem[k], rest[n_w].at[k]) for k in range(n_w)]
            for cp in copies:
                cp.start()
            for cp in copies:
                cp.wait()
            for a in outs[len(row_outs):]:
                a[...] = jnp.zeros(a.shape, a.dtype)

        body(i, n, *ins, *w_vmem, *outs, *extra)

    res = pl.pallas_call(
        kern, name=name, grid=(n,), in_specs=in_specs, out_specs=out_specs, out_shape=out_shape,
        scratch_shapes=scratch_shapes, compiler_params=_params(1),
    )(*[r[0] for r in rows], *consts, *residents)
    return res


def _mm_tn(name, x, y, k_dim, n_dim, *, tk, tn, x_off=0, y_off=0, tt=512):
    n_tok = x.shape[0]
    tt = min(tt, n_tok)
    n_t = n_tok // tt
    xb, yb = x_off // tk, y_off // tn

    def kern(x_ref, y_ref, o_ref):
        @pl.when(pl.program_id(2) == 0)
        def _():
            o_ref[...] = jnp.zeros(o_ref.shape, F32)

        o_ref[...] += _dot_tn(x_ref[...], y_ref[...])

    return pl.pallas_call(
        kern, name=name, grid=(k_dim // tk, n_dim // tn, n_t),
        in_specs=[pl.BlockSpec((tt, tk), lambda k, n, t: (t, xb + k)),
                  pl.BlockSpec((tt, tn), lambda k, n, t: (t, yb + n))],
        out_specs=pl.BlockSpec((tk, tn), lambda k, n, t: (k, n)),
        out_shape=jax.ShapeDtypeStruct((k_dim, n_dim), F32),
        compiler_params=_params(3),
    )(x, y)


def _ffn_fwd(name, x, gain, wg, wu, wd):
    n_tok, tm = x.shape[0], min(ROW_BLOCK, x.shape[0])

    def body(i, n, x_ref, g_ref, wg_ref, wu_ref, wd_ref, xo_ref, a_ref, b_ref):
        x_v = x_ref[...]
        xh, _ = _rms_stats(x_v)
        h = (xh * g_ref[...]).astype(BF16)
        y = jnp.zeros((tm, D_MODEL), F32)
        for f0 in range(0, D_FF, FF_SPLIT):
            sl = slice(f0, f0 + FF_SPLIT)
            a = _dot(h, wg_ref[:, sl])
            b = _dot(h, wu_ref[:, sl])
            a_ref[:, sl] = a.astype(BF16)
            b_ref[:, sl] = b.astype(BF16)
            s = (a * jax.nn.sigmoid(a) * b).astype(BF16)
            y = y + _dot(s, wd_ref[sl, :])
        xo_ref[...] = x_v + 0.5 * y

    return _rowcall(name, body, n_tok, tm, [x], [gain], [wg, wu, wd],
                    [(D_MODEL, F32), (D_FF, BF16), (D_FF, BF16)], [])


def _ffn_bwd(name, x, dxo, a, b, gain, wd_t, wg_t, wu_t):
    n_tok, tm = x.shape[0], min(FFN_BWD_ROW_BLOCK, x.shape[0])

    def body(i, n, x_ref, dxo_ref, a_ref, b_ref, g_ref, wdt_ref, wgt_ref, wut_ref,
             dx_ref, da_ref, db_ref, s_ref, h_ref, dy_ref, dg_ref):
        g = g_ref[...]
        xh, r = _rms_stats(x_ref[...])
        h_ref[...] = (xh * g).astype(BF16)
        dxo_v = dxo_ref[...]
        dy = (0.5 * dxo_v).astype(BF16)
        dy_ref[...] = dy
        dh = jnp.zeros((tm, D_MODEL), F32)
        for f0 in range(0, D_FF, FF_SPLIT):
            sl = slice(f0, f0 + FF_SPLIT)
            a_v = a_ref[:, sl].astype(F32)
            b_v = b_ref[:, sl].astype(F32)
            ds = _dot(dy, wdt_ref[:, sl])
            sig = jax.nn.sigmoid(a_v)
            sa = a_v * sig
            s_ref[:, sl] = (sa * b_v).astype(BF16)
            da = (ds * b_v * (sig * (1.0 + a_v * (1.0 - sig)))).astype(BF16)
            db = (ds * sa).astype(BF16)
            da_ref[:, sl] = da
            db_ref[:, sl] = db
            dh = dh + _dot(da, wgt_ref[sl, :]) + _dot(db, wut_ref[sl, :])
        dx, dg = _rms_bwd(xh, r, g, dh)
        dx_ref[...] = dxo_v + dx
        dg_ref[...] += dg

    return _rowcall(name, body, n_tok, tm, [x, dxo, a, b], [gain], [wd_t, wg_t, wu_t],
                    [(D_MODEL, F32), (D_FF, BF16), (D_FF, BF16), (D_FF, BF16), (D_MODEL, BF16), (D_MODEL, BF16)],
                    [((1, D_MODEL), F32)])


def _proj_fwd(name, x, gain, w1, wf, wgl):
    n_tok, tm = x.shape[0], min(ROW_BLOCK, x.shape[0])
    c0, c1 = 3 * CONV_W, 3 * CONV_W + 2 * SG_W

    def body(i, n, x_ref, g_ref, w1_ref, wf_ref, wgl_ref, cin_ref, sg_ref, qkv_ref, fl_ref, gl_ref):
        xh, _ = _rms_stats(x_ref[...])
        h = (xh * g_ref[...]).astype(BF16)
        cin_ref[...] = _dot(h, w1_ref[:, 0:c0]).astype(BF16)
        sg_ref[...] = _dot(h, w1_ref[:, c0:c1]).astype(BF16)
        qkv_ref[...] = _dot(h, w1_ref[:, c1:W1_COLS]).astype(BF16)
        fl_ref[...] = _dot(h, wf_ref[...])
        gl_ref[...] = _dot(h, wgl_ref[...]).astype(BF16)

    return _rowcall(name, body, n_tok, tm, [x], [gain], [w1, wf, wgl],
                    [(3 * CONV_W, BF16), (2 * SG_W, BF16), (3 * FOX_W, BF16), (LANES, F32), (3 * D_MODEL, BF16)], [])


def _proj_bwd(name, x, dxin, dcin, dsg, dq, dk, dv, dfl, dgl, gain, w1_t, wf_t, wgl_t):
    n_tok, tm = x.shape[0], min(ROW_BLOCK, x.shape[0])
    c0, c1 = 3 * CONV_W, 3 * CONV_W + 2 * SG_W

    def body(i, n, x_ref, dxin_ref, dcin_ref, dsg_ref, dq_ref, dk_ref, dv_ref, dfl_ref, dgl_ref, g_ref,
             w1t_ref, wft_ref, wglt_ref, dx_ref, h_ref, dqkv_ref, dfb_ref, dg_ref):
        g = g_ref[...]
        xh, r = _rms_stats(x_ref[...])
        h_ref[...] = (xh * g).astype(BF16)
        dfb = dfl_ref[...].astype(BF16)
        dfb_ref[...] = dfb
        dh = _dot(dcin_ref[...], w1t_ref[0:c0, :])
        dh = dh + _dot(dsg_ref[...], w1t_ref[c0:c1, :])
        for k, d_ref in enumerate((dq_ref, dk_ref, dv_ref)):
            d_b = d_ref[...].astype(BF16)
            dqkv_ref[:, k * FOX_W:(k + 1) * FOX_W] = d_b
            dh = dh + _dot(d_b, w1t_ref[c1 + k * FOX_W:c1 + (k + 1) * FOX_W, :])
        dh = dh + _dot(dfb, wft_ref[...])
        dh = dh + _dot(dgl_ref[...], wglt_ref[...])
        dx, dg = _rms_bwd(xh, r, g, dh)
        dx_ref[...] = dxin_ref[...] + dx
        dg_ref[...] += dg

    return _rowcall(name, body, n_tok, tm, [x, dxin, dcin, dsg, dq, dk, dv, dfl, dgl], [gain], [w1_t, wf_t, wgl_t],
                    [(D_MODEL, F32), (D_MODEL, BF16), (3 * FOX_W, BF16), (LANES, BF16)], [((1, D_MODEL), F32)])


def _conv_taps(z, prev_z, i):
    tm = z.shape[0]
    row = lax.broadcasted_iota(jnp.int32, (tm, 1), 0)
    live = (i > 0).astype(F32)
    p1, p2 = prev_z[HALO - 1:HALO, :] * live, prev_z[HALO - 2:HALO - 1, :] * live
    z1 = jnp.where(row == 0, p1, pltpu.roll(z, 1, 0))
    z2 = jnp.where(row == 0, p2, jnp.where(row == 1, p1, pltpu.roll(z, 2, 0)))
    return z1, z2


def _prev_spec(tm, cols):
    return pl.BlockSpec((HALO, cols), lambda i: (jnp.maximum(i * (tm // HALO) - 1, 0), 0))


def _next_spec(tm, cols, n):
    last = n * (tm // HALO) - 1
    return pl.BlockSpec((HALO, cols), lambda i: (jnp.minimum((i + 1) * (tm // HALO), last), 0))


def _conv_fwd(name, cin, cw):
    n_tok, tm = cin.shape[0], min(ROW_BLOCK, cin.shape[0])
    w = CONV_W

    def body(i, n, c_ref, p_ref, cw_ref, ya_ref):
        c_v, p_v = c_ref[...].astype(F32), p_ref[...].astype(F32)
        z = c_v[:, w:2 * w] * c_v[:, 2 * w:]
        z1, z2 = _conv_taps(z, p_v[:, w:2 * w] * p_v[:, 2 * w:], i)
        y = cw_ref[0:1, :] * z2 + cw_ref[1:2, :] * z1 + cw_ref[2:3, :] * z
        ya_ref[...] = (c_v[:, 0:w] * y).astype(BF16)

    return _rowcall(name, body, n_tok, tm, [cin, (cin, _prev_spec(tm, 3 * w))], [cw], [], [(w, BF16)], [])[0]


def _conv_bwd(name, cin, dya, cw):
    n_tok, tm = cin.shape[0], min(ROW_BLOCK, cin.shape[0])
    w = CONV_W
    n_blocks = n_tok // tm

    def body(i, n, c_ref, p_ref, nx_ref, dya_ref, ndya_ref, cw_ref, dc_ref, dcw_ref):
        c_v, p_v = c_ref[...].astype(F32), p_ref[...].astype(F32)
        ab, ac, ah = c_v[:, 0:w], c_v[:, w:2 * w], c_v[:, 2 * w:]
        z = ac * ah
        z1, z2 = _conv_taps(z, p_v[:, w:2 * w] * p_v[:, 2 * w:], i)
        w0, w1, w2 = cw_ref[0:1, :], cw_ref[1:2, :], cw_ref[2:3, :]
        y = w0 * z2 + w1 * z1 + w2 * z
        dya_v = dya_ref[...].astype(F32)
        dy = dya_v * ab
        live = (i < n - 1).astype(F32)
        ndy = ndya_ref[...].astype(F32) * nx_ref[:, 0:w].astype(F32) * live
        row = lax.broadcasted_iota(jnp.int32, (tm, 1), 0)
        dy1 = jnp.where(row == tm - 1, ndy[0:1, :], pltpu.roll(dy, tm - 1, 0))
        dy2 = jnp.where(row == tm - 1, ndy[1:2, :], jnp.where(row == tm - 2, ndy[0:1, :], pltpu.roll(dy, tm - 2, 0)))
        dz = w2 * dy + w1 * dy1 + w0 * dy2
        dc_ref[:, 0:w] = (dya_v * y).astype(BF16)
        dc_ref[:, w:2 * w] = (dz * ah).astype(BF16)
        dc_ref[:, 2 * w:] = (dz * ac).astype(BF16)
        dcw_ref[0:1, :] += jnp.sum(dy * z2, axis=0, keepdims=True)
        dcw_ref[1:2, :] += jnp.sum(dy * z1, axis=0, keepdims=True)
        dcw_ref[2:3, :] += jnp.sum(dy * z, axis=0, keepdims=True)

    return _rowcall(name, body, n_tok, tm,
                    [cin, (cin, _prev_spec(tm, 3 * w)), (cin, _next_spec(tm, 3 * w, n_blocks)),
                     dya, (dya, _next_spec(tm, w, n_blocks))],
                    [cw], [], [(3 * w, BF16)], [((8, w), F32)])


def _sg_common(sg_ref, gn_ref):
    s_v = sg_ref[...].astype(F32)
    u, v = s_v[:, 0:SG_W], s_v[:, SG_W:]
    ug, tu = _gelu(u)
    vg, tv = _gelu(v)
    vh, r = _rms_stats(vg)
    vn = (vh * gn_ref[...]).astype(BF16)
    return u, v, ug, tu, tv, vh, r, vn


def _sg_fwd(name, sgin, gn, sgw, bias_full):
    n_tok, tm = sgin.shape[0], min(ROW_BLOCK, sgin.shape[0])

    def body(i, n, sg_ref, gn_ref, w_ref, bias_ref, yb_ref):
        _, _, ug, _, _, _, _, vn = _sg_common(sg_ref, gn_ref)
        tril = lax.broadcasted_iota(jnp.int32, (CHUNK, CHUNK), 0) >= lax.broadcasted_iota(jnp.int32, (CHUNK, CHUNK), 1)
        wt = [jnp.where(tril, w_ref[g], 0.0).astype(BF16) for g in range(SG_G)]
        for c0 in range(0, tm, CHUNK):
            sv = jnp.concatenate(
                [_dot(wt[g], vn[c0:c0 + CHUNK, g * CHUNK:(g + 1) * CHUNK]) for g in range(SG_G)], axis=1)
            sv = sv + bias_ref[...]
            yb_ref[c0:c0 + CHUNK, :] = (ug[c0:c0 + CHUNK, :] * sv).astype(BF16)

    return _rowcall(name, body, n_tok, tm, [sgin], [gn, sgw, bias_full], [], [(SG_W, BF16)], [])[0]


def _sg_bwd(name, sgin, dyb, gn, sgw, sgw_t, bias_full):
    n_tok, tm = sgin.shape[0], min(ROW_BLOCK, sgin.shape[0])

    def body(i, n, sg_ref, dyb_ref, gn_ref, w_ref, wt_ref, bias_ref, dsg_ref, dgn_ref, dw_ref, db_ref):
        u, v, ug, tu, tv, vh, r, vn = _sg_common(sg_ref, gn_ref)
        r0 = lax.broadcasted_iota(jnp.int32, (CHUNK, CHUNK), 0)
        r1 = lax.broadcasted_iota(jnp.int32, (CHUNK, CHUNK), 1)
        wt = [jnp.where(r0 >= r1, w_ref[g], 0.0).astype(BF16) for g in range(SG_G)]
        wtt = [jnp.where(r0 <= r1, wt_ref[g], 0.0).astype(BF16) for g in range(SG_G)]
        dyb_v = dyb_ref[...].astype(F32)
        dvn_rows = []
        for c0 in range(0, tm, CHUNK):
            rows = slice(c0, c0 + CHUNK)
            svs, dvns = [], []
            dsv = dyb_v[rows, :] * ug[rows, :]
            for g in range(SG_G):
                cols = slice(g * CHUNK, (g + 1) * CHUNK)
                svs.append(_dot(wt[g], vn[rows, cols]))
                dsv_g = dsv[:, cols]
                dsv_b = dsv_g.astype(BF16)
                dvns.append(_dot(wtt[g], dsv_b))
                dw_ref[g] += jnp.where(r0 >= r1, _dot_nt(dsv_b, vn[rows, cols]), 0.0)
                db_ref[g] += jnp.broadcast_to(jnp.sum(dsv_g, axis=1, keepdims=True), (CHUNK, CHUNK))
            sv = jnp.concatenate(svs, axis=1) + bias_ref[...]
            dug = dyb_v[rows, :] * sv
            dsg_ref[rows, 0:SG_W] = (dug * _gelu_grad(u[rows, :], tu[rows, :])).astype(BF16)
            dvn_rows.append(jnp.concatenate(dvns, axis=1))
        dvn = jnp.concatenate(dvn_rows, axis=0)
        dvg, dgn = _rms_bwd(vh, r, gn_ref[...], dvn)
        dsg_ref[:, SG_W:] = (dvg * _gelu_grad(v, tv)).astype(BF16)
        dgn_ref[...] += dgn

    return _rowcall(name, body, n_tok, tm, [sgin, dyb], [gn, sgw, sgw_t, bias_full], [], [(2 * SG_W, BF16)],
                    [((1, SG_W), F32), ((SG_G, CHUNK, CHUNK), F32), ((SG_G, CHUNK, CHUNK), F32)])


def _forget_cumsum(name, fl, bf):
    n_tok, tm = fl.shape[0], min(ROW_BLOCK, fl.shape[0])

    def body(i, n, fl_ref, b_ref, c_ref, carry):
        @pl.when(i == 0)
        def _():
            carry[...] = jnp.zeros(carry.shape, F32)

        z = fl_ref[...] + b_ref[...]
        lf = jnp.minimum(z, 0.0) - jnp.log1p(jnp.exp(-jnp.abs(z)))
        tri = (lax.broadcasted_iota(jnp.int32, (tm, tm), 0) >= lax.broadcasted_iota(jnp.int32, (tm, tm), 1)).astype(BF16)
        c = _split3_dot(tri, lf) + carry[...]
        c_ref[...] = c * LOG2E
        carry[...] = c[tm - 1:tm, :]

    return _rowcall(name, body, n_tok, tm, [fl], [bf], [], [(LANES, F32)], [], scratch=[pltpu.VMEM((1, LANES), F32)])[0]


def _forget_bwd(name, dc, fl, bf):
    n_tok, tm = fl.shape[0], min(ROW_BLOCK, fl.shape[0])
    n = n_tok // tm
    rev = pl.BlockSpec((tm, LANES), lambda i: (n - 1 - i, 0))

    def kern(dc_ref, fl_ref, b_ref, dfl_ref, db_ref, carry):
        @pl.when(pl.program_id(0) == 0)
        def _():
            carry[...] = jnp.zeros(carry.shape, F32)
            db_ref[...] = jnp.zeros(db_ref.shape, F32)

        triu = (lax.broadcasted_iota(jnp.int32, (tm, tm), 0) <= lax.broadcasted_iota(jnp.int32, (tm, tm), 1)).astype(BF16)
        dlf = _split3_dot(triu, dc_ref[...]) + carry[...]
        carry[...] = dlf[0:1, :]
        z = fl_ref[...] + b_ref[...]
        dfl = dlf * jax.nn.sigmoid(-z)
        dfl_ref[...] = dfl
        db_ref[...] += jnp.sum(dfl, axis=0, keepdims=True)

    return pl.pallas_call(
        kern, name=name, grid=(n,), in_specs=[rev, rev, _full_spec((1, LANES))],
        out_specs=[rev, _full_spec((1, LANES))],
        out_shape=[jax.ShapeDtypeStruct((n_tok, LANES), F32), jax.ShapeDtypeStruct((1, LANES), F32)],
        scratch_shapes=[pltpu.VMEM((1, LANES), F32)], compiler_params=_params(1),
    )(dc, fl, bf)


def _lane_pick(c_blk, h):
    lane = lax.broadcasted_iota(jnp.int32, c_blk.shape, 1)
    return jnp.broadcast_to(jnp.sum(jnp.where(lane == h, c_blk, 0.0), axis=1, keepdims=True), c_blk.shape)


def _wide(stat, width):
    return jnp.tile(stat, (1, width // LANES))


def _pair_half(rows):
    return lax.broadcasted_iota(jnp.int32, (rows, LANES), 1) // FOX_D


def _col_to_row(col):
    pick = (lax.broadcasted_iota(jnp.int32, (8, LANES), 1) == 0).astype(BF16)
    hi = col.astype(BF16)
    r1 = col - hi.astype(F32)
    mid = r1.astype(BF16)
    lo = (r1 - mid.astype(F32)).astype(BF16)
    return (_dot_nt(pick, hi) + _dot_nt(pick, mid) + _dot_nt(pick, lo))[0:1, :]


def _tri_table(nb, lower):
    rows = [(a, b) for a in range(nb) for b in (range(a + 1) if lower else range(a, nb))]
    return tuple(jnp.asarray([r[k] for r in rows], jnp.int32) for k in range(2))


def _flash_bounds(name, qkv):
    n_tok = qkv.shape[0]
    blk = min(FLASH_BLOCK, n_tok)

    def kern(x_ref, o_ref):
        head = (lax.broadcasted_iota(jnp.int32, (FOX_W, LANES), 0) // FOX_D
                == lax.broadcasted_iota(jnp.int32, (FOX_W, LANES), 1)).astype(BF16)
        q, k = x_ref[:, 0:FOX_W].astype(F32), x_ref[:, FOX_W:2 * FOX_W].astype(F32)
        qn = jnp.sqrt(jnp.max(_dot((q * q).astype(BF16), head) * 1.01, axis=0, keepdims=True))
        kn = jnp.sqrt(jnp.max(_dot((k * k).astype(BF16), head) * 1.01, axis=0, keepdims=True))
        diag = jnp.min(_dot((q * k).astype(BF16), head), axis=0, keepdims=True) - qn * kn * (2.0 ** -8)
        o_ref[...] = jnp.concatenate([qn, kn, diag, jnp.zeros((5, LANES), F32)], axis=0)

    return pl.pallas_call(
        kern, name=name, grid=(n_tok // blk,),
        in_specs=[pl.BlockSpec((blk, 2 * FOX_W), lambda b: (b, 0))], out_specs=pl.BlockSpec((8, LANES), lambda b: (b, 0)),
        out_shape=jax.ShapeDtypeStruct((n_tok // blk * 8, LANES), F32), compiler_params=_params(1),
    )(qkv)


def _prune_tables(bounds, c2, blk):
    nb = c2.shape[0] // blk
    tab = bounds.reshape(nb, 8, LANES)
    c_blocks = c2.reshape(nb, blk, LANES)
    return tuple(t[:, :FOX_H].reshape(-1)
                 for t in (tab[:, 0], tab[:, 1], tab[:, 2], c_blocks[:, 0], c_blocks[:, blk - 1]))


def _block_is_live(tables, p, i, j):
    qn, kn, diag, c_first, c_last = tables
    a = FOX_SCALE * LOG2E
    live = []
    for e in range(2):
        h = 2 * p + e
        u = a * qn[8 * i + h] * kn[8 * j + h] + c_first[8 * i + h] - c_last[8 * j + h]
        lo = a * diag[8 * i + h]
        live.append((u - lo) > -PRUNE_LOG2)
    return live


def _flash_fwd(name, qkv, c2, ct2, tables):
    n_tok = qkv.shape[0]
    blk = min(FLASH_BLOCK, n_tok)
    nb = n_tok // blk
    n_pair = FOX_H // 2
    it, jt = _tri_table(nb, True)

    strip = min(FLASH_STRIP, blk)

    def kern(it_ref, jt_ref, qn_ref, kn_ref, dg_ref, cf_ref, cl_ref, q_ref, k_ref, v_ref, c_ref, ct_ref, o_ref, lse_ref,
             m_s, l_s, acc_s, cq_s, qm_s, al_s, s_s, pb_s):
        n = pl.program_id(1)
        i, j = it_ref[n], jt_ref[n]
        half = _pair_half(blk)
        live = _block_is_live((qn_ref, kn_ref, dg_ref, cf_ref, cl_ref), pl.program_id(0), i, j)

        @pl.when(j == 0)
        def _():
            m_s[...] = jnp.full(m_s.shape, -jnp.inf, F32)
            l_s[...] = jnp.zeros(l_s.shape, F32)
            acc_s[...] = jnp.zeros(acc_s.shape, F32)
            for e in range(2):
                cq_s[e] = _lane_pick(c_ref[...], 2 * pl.program_id(0) + e)
                qm_s[e] = jnp.where(half == e, q_ref[...], jnp.zeros_like(q_ref[...]))

        def step(heads, on_diagonal):
            k_v, v_v = k_ref[...], v_ref[...]
            for e in heads:
                s_s[e] = _dot_nt(qm_s[e], k_v)
            for e in heads:
                ck = ct_ref[e]
                for r0 in range(0, blk, strip):
                    rs = slice(r0, r0 + strip)
                    t = s_s[e, rs, :] * (FOX_SCALE * LOG2E) - ck
                    if on_diagonal:
                        keep = (lax.broadcasted_iota(jnp.int32, (strip, blk), 1)
                                <= lax.broadcasted_iota(jnp.int32, (strip, blk), 0) + r0)
                        t = jnp.where(keep, t, -jnp.inf)
                    m_old, cq = m_s[e, rs, :], cq_s[e, rs, :]
                    m_new = jnp.maximum(m_old, jnp.max(t, axis=1, keepdims=True) + cq)
                    p = jnp.exp2(t - _wide(m_new - cq, blk))
                    alpha = jnp.exp2(m_old - m_new)
                    l_s[e, rs, :] = alpha * l_s[e, rs, :] + jnp.sum(p, axis=1, keepdims=True)
                    m_s[e, rs, :] = m_new
                    al_s[e, rs, :] = alpha
                    pb_s[e, rs, :] = p.astype(BF16)
            if len(heads) == 2:
                pv = jnp.where(half == 0, _dot(pb_s[0], v_v), _dot(pb_s[1], v_v))
                acc_s[...] = jnp.where(half == 0, al_s[0], al_s[1]) * acc_s[...] + pv
            else:
                e = heads[0]
                acc_s[...] = jnp.where(half == e, al_s[e] * acc_s[...] + _dot(pb_s[e], v_v), acc_s[...])

        below = j < i

        @pl.when(below & live[0] & live[1])
        def _():
            step((0, 1), False)

        @pl.when(below & live[0] & jnp.logical_not(live[1]))
        def _():
            step((0,), False)

        @pl.when(below & jnp.logical_not(live[0]) & live[1])
        def _():
            step((1,), False)

        @pl.when(j == i)
        def _():
            step((0, 1), True)
            o_ref[...] = (acc_s[...] / jnp.where(half == 0, l_s[0], l_s[1])).astype(BF16)
            rows = pl.ds(pl.multiple_of(i * blk, blk), blk)
            lse_ref[0, :, rows] = jnp.concatenate(
                [_col_to_row(m_s[e] + jnp.log(l_s[e]) * LOG2E) for e in range(2)], axis=0)

    def col_block(first, inner):
        def index(p, n, it_r, jt_r, *_):
            return ((jt_r[n] if inner else it_r[n]), first + p)
        return pl.BlockSpec((blk, LANES), index)

    pair_stat = (2, blk, LANES)
    grid_spec = pltpu.PrefetchScalarGridSpec(
        num_scalar_prefetch=2 + len(tables), grid=(n_pair, int(it.shape[0])),
        in_specs=[col_block(0, False), col_block(n_pair, True), col_block(2 * n_pair, True),
                  pl.BlockSpec((blk, LANES), lambda p, n, it_r, jt_r, *_: (it_r[n], 0)),
                  pl.BlockSpec((2, 1, blk), lambda p, n, it_r, jt_r, *_: (p, 0, jt_r[n]))],
        out_specs=[col_block(0, False), pl.BlockSpec((1, 2, n_tok), lambda p, n, it_r, jt_r, *_: (p, 0, 0))],
        scratch_shapes=[pltpu.VMEM(pair_stat, F32), pltpu.VMEM(pair_stat, F32), pltpu.VMEM((blk, LANES), F32),
                        pltpu.VMEM(pair_stat, F32), pltpu.VMEM(pair_stat, BF16), pltpu.VMEM(pair_stat, F32),
                        pltpu.VMEM((2, blk, blk), F32), pltpu.VMEM((2, blk, blk), BF16)],
    )
    return pl.pallas_call(
        kern, name=name, grid_spec=grid_spec,
        out_shape=[jax.ShapeDtypeStruct((n_tok, FOX_W), BF16), jax.ShapeDtypeStruct((n_pair, 2, n_tok), F32)],
        compiler_params=_params(2),
    )(it, jt, *tables, qkv, qkv, qkv, c2, ct2)


def _flash_delta(name, do, o):
    n_tok, tm = do.shape[0], min(ROW_BLOCK, do.shape[0])

    def body(i, n, do_ref, o_ref, d_ref):
        prod = do_ref[...].astype(F32) * o_ref[...].astype(F32)
        head = (lax.broadcasted_iota(jnp.int32, (FOX_W, LANES), 0) // FOX_D
                == lax.broadcasted_iota(jnp.int32, (FOX_W, LANES), 1)).astype(BF16)
        hi = prod.astype(BF16)
        r1 = prod - hi.astype(F32)
        mid = r1.astype(BF16)
        lo = (r1 - mid.astype(F32)).astype(BF16)
        d_ref[...] = _dot(hi, head) + _dot(mid, head) + _dot(lo, head)

    return _rowcall(name, body, n_tok, tm, [do, o], [], [], [(LANES, F32)], [])[0]


def _flash_bwd(name, qkv, do, r_row, d_row, c2, tables):
    n_tok = qkv.shape[0]
    blk = min(FLASH_BLOCK, n_tok)
    nb = n_tok // blk
    n_pair = FOX_H // 2
    jt, it = _tri_table(nb, False)

    def kern(jt_ref, it_ref, qn_ref, kn_ref, dg_ref, cf_ref, cl_ref, k_ref, v_ref, q_ref, do_ref, rr_ref, dr_ref, c_ref,
             dq_ref, dk_ref, dv_ref, dc_ref, km_s, vm_s, ck_s, dck_s):
        n = pl.program_id(1)
        j, i = jt_ref[n], it_ref[n]
        first = _pair_half(blk) == 0
        live = _block_is_live((qn_ref, kn_ref, dg_ref, cf_ref, cl_ref), pl.program_id(0), i, j)

        @pl.when(n == 0)
        def _():
            dq_ref[...] = jnp.zeros(dq_ref.shape, F32)
            dc_ref[...] = jnp.zeros(dc_ref.shape, F32)

        @pl.when(i == j)
        def _():
            dk_ref[...] = jnp.zeros(dk_ref.shape, F32)
            dv_ref[...] = jnp.zeros(dv_ref.shape, F32)
            dck_s[...] = jnp.zeros(dck_s.shape, F32)
            for e in range(2):
                mine = first if e == 0 else jnp.logical_not(first)
                km_s[e] = jnp.where(mine, k_ref[...], jnp.zeros_like(k_ref[...]))
                vm_s[e] = jnp.where(mine, v_ref[...], jnp.zeros_like(v_ref[...]))
                ck_s[e] = _lane_pick(c_ref[...], 2 * pl.program_id(0) + e)

        def step(heads, on_diagonal):
            q_v, do_v, k_v = q_ref[...], do_ref[...], k_ref[...]
            dvs, dks, dqs, sums = {}, {}, {}, {}
            for e in heads:
                t = _dot_nt(km_s[e], q_v) * (FOX_SCALE * LOG2E) - _wide(ck_s[e], blk) - rr_ref[e]
                if on_diagonal:
                    keep = (lax.broadcasted_iota(jnp.int32, (blk, blk), 1)
                            >= lax.broadcasted_iota(jnp.int32, (blk, blk), 0))
                    t = jnp.where(keep, t, -jnp.inf)
                p = jnp.exp2(t)
                ds = p * (_dot_nt(vm_s[e], do_v) - dr_ref[e])
                p_b, ds_b = p.astype(BF16), ds.astype(BF16)
                dvs[e] = _dot(p_b, do_v)
                dks[e] = _dot(ds_b, q_v)
                dqs[e] = _dot_tn(ds_b, k_v)
                dck_s[e] += jnp.broadcast_to(jnp.sum(ds, axis=1, keepdims=True), (blk, LANES))
                sums[e] = jnp.sum(ds, axis=0, keepdims=True)
            rows = pl.ds(pl.multiple_of(i * blk, blk), blk)

            def merged(parts):
                zero = jnp.zeros((blk, LANES), F32)
                return jnp.where(first, parts.get(0, zero), parts.get(1, zero))

            dv_ref[...] += merged(dvs)
            dk_ref[...] += merged(dks) * FOX_SCALE
            dq_ref[rows, :] += merged(dqs) * FOX_SCALE
            none = jnp.zeros((1, blk), F32)
            dc_ref[0, :, rows] += jnp.concatenate([sums.get(0, none), sums.get(1, none)], axis=0)

        above = i > j

        @pl.when(above & live[0] & live[1])
        def _():
            step((0, 1), False)

        @pl.when(above & live[0] & jnp.logical_not(live[1]))
        def _():
            step((0,), False)

        @pl.when(above & jnp.logical_not(live[0]) & live[1])
        def _():
            step((1,), False)

        @pl.when(i == j)
        def _():
            step((0, 1), True)

        @pl.when(i == nb - 1)
        def _():
            keys = pl.ds(pl.multiple_of(j * blk, blk), blk)
            dc_ref[0, :, keys] -= jnp.concatenate([_col_to_row(dck_s[e]) for e in range(2)], axis=0)

    def col_block(first_col, inner):
        def index(p, n, jt_r, it_r, *_):
            return ((it_r[n] if inner else jt_r[n]), first_col + p)
        return pl.BlockSpec((blk, LANES), index)

    def row_stat():
        return pl.BlockSpec((2, 1, blk), lambda p, n, jt_r, it_r, *_: (p, 0, it_r[n]))

    pair_stat = (2, blk, LANES)
    grid_spec = pltpu.PrefetchScalarGridSpec(
        num_scalar_prefetch=2 + len(tables), grid=(n_pair, int(jt.shape[0])),
        in_specs=[col_block(n_pair, False), col_block(2 * n_pair, False), col_block(0, True), col_block(0, True),
                  row_stat(), row_stat(), pl.BlockSpec((blk, LANES), lambda p, n, jt_r, it_r, *_: (jt_r[n], 0))],
        out_specs=[pl.BlockSpec((n_tok, LANES), lambda p, n, jt_r, it_r, *_: (0, p)),
                   col_block(0, False), col_block(0, False),
                   pl.BlockSpec((1, 2, n_tok), lambda p, n, jt_r, it_r, *_: (p, 0, 0))],
        scratch_shapes=[pltpu.VMEM(pair_stat, BF16), pltpu.VMEM(pair_stat, BF16), pltpu.VMEM(pair_stat, F32),
                        pltpu.VMEM(pair_stat, F32)],
    )
    wide = jax.ShapeDtypeStruct((n_tok, FOX_W), F32)
    return pl.pallas_call(
        kern, name=name, grid_spec=grid_spec,
        out_shape=[wide, wide, wide, jax.ShapeDtypeStruct((n_pair, 2, n_tok), F32)],
        compiler_params=_params(2),
    )(jt, it, *tables, qkv, qkv, qkv, do, r_row, d_row, c2)


def _merge_fwd(name, x, ya, yb, yc, gl, wb, wo):
    n_tok, tm = x.shape[0], min(ROW_BLOCK, x.shape[0])

    def body(i, n, x_ref, ya_ref, yb_ref, yc_ref, gl_ref, wb_ref, wo_ref, xo_ref):
        merged = jnp.zeros((tm, D_MODEL), F32)
        for k, y_ref in enumerate((ya_ref, yb_ref, yc_ref)):
            gate = jax.nn.sigmoid(gl_ref[:, k * D_MODEL:(k + 1) * D_MODEL].astype(F32))
            merged = merged + gate * _dot(y_ref[...], wb_ref[k])
        xo_ref[...] = x_ref[...] + _dot(merged.astype(BF16), wo_ref[...])

    return _rowcall(name, body, n_tok, tm, [x, ya, yb, yc, gl], [], [wb, wo], [(D_MODEL, F32)], [])[0]


def _merge_bwd(name, dxo, ya, yb, yc, gl, wb, wb_t, wo_t):
    n_tok, tm = dxo.shape[0], min(ROW_BLOCK, dxo.shape[0])

    def body(i, n, dxo_ref, ya_ref, yb_ref, yc_ref, gl_ref, wb_ref, wbt_ref, wot_ref,
             mg_ref, dob_ref, dgl_ref, dbr_ref, dya_ref, dyb_ref, dyc_ref):
        dob = dxo_ref[...].astype(BF16)
        dob_ref[...] = dob
        dm = _dot(dob, wot_ref[...])
        merged = jnp.zeros((tm, D_MODEL), F32)
        for k, (y_ref, dy_ref) in enumerate(((ya_ref, dya_ref), (yb_ref, dyb_ref), (yc_ref, dyc_ref))):
            cols = slice(k * D_MODEL, (k + 1) * D_MODEL)
            gate = jax.nn.sigmoid(gl_ref[:, cols].astype(F32))
            br = _dot(y_ref[...], wb_ref[k])
            merged = merged + gate * br
            dgl_ref[:, cols] = (dm * br * gate * (1.0 - gate)).astype(BF16)
            dbr = (dm * gate).astype(BF16)
            dbr_ref[:, cols] = dbr
            dy_ref[...] = _dot(dbr, wbt_ref[k]).astype(BF16)
        mg_ref[...] = merged.astype(BF16)

    return _rowcall(name, body, n_tok, tm, [dxo, ya, yb, yc, gl], [], [wb, wb_t, wo_t],
                    [(D_MODEL, BF16), (D_MODEL, BF16), (3 * D_MODEL, BF16), (3 * D_MODEL, BF16),
                     (CONV_W, BF16), (SG_W, BF16), (FOX_W, BF16)], [])


def _mem_fwd(name, mem, gain, wk, wv):
    n_mem = mem.shape[0]

    def kern(m_ref, g_ref, wk_ref, wv_ref, mn_ref, k_ref, v_ref):
        mh, _ = _rms_stats(m_ref[...])
        mn = (mh * g_ref[...]).astype(BF16)
        mn_ref[...] = mn
        k_ref[...] = _dot(mn, wk_ref[...]).astype(BF16)
        v_ref[...] = _dot(mn, wv_ref[...]).astype(BF16)

    shp = jax.ShapeDtypeStruct((n_mem, D_MODEL), BF16)
    return pl.pallas_call(
        kern, name=name, grid=(1,),
        in_specs=[_full_spec(mem.shape), _full_spec(gain.shape), _full_spec(wk.shape), _full_spec(wv.shape)],
        out_specs=[_full_spec(shp.shape)] * 3, out_shape=[shp] * 3, compiler_params=_params(1),
    )(mem, gain, wk, wv)


def _mem_bwd(name, mem, gain, dkx, dvx, wk_t, wv_t):
    n_mem = mem.shape[0]

    def kern(m_ref, g_ref, dk_ref, dv_ref, wkt_ref, wvt_ref, dkb_ref, dvb_ref, dg_ref):
        mh, _ = _rms_stats(m_ref[...])
        dkb, dvb = dk_ref[...].astype(BF16), dv_ref[...].astype(BF16)
        dkb_ref[...] = dkb
        dvb_ref[...] = dvb
        dm = _dot(dkb, wkt_ref[...]) + _dot(dvb, wvt_ref[...])
        dg_ref[...] = jnp.sum(dm * mh, axis=0, keepdims=True)

    shp = jax.ShapeDtypeStruct((n_mem, D_MODEL), BF16)
    args = (mem, gain, dkx, dvx, wk_t, wv_t)
    return pl.pallas_call(
        kern, name=name, grid=(1,), in_specs=[_full_spec(a.shape) for a in args],
        out_specs=[_full_spec(shp.shape)] * 2 + [_full_spec((1, D_MODEL))],
        out_shape=[shp, shp, jax.ShapeDtypeStruct((1, D_MODEL), F32)], compiler_params=_params(1),
    )(*args)


def _xa_probs(q_b, kx_ref, hd):
    cols = slice(hd * XA_D, (hd + 1) * XA_D)
    s = _dot_nt(q_b[:, cols], kx_ref[:, cols]) * (XA_D ** -0.5)
    e = jnp.exp(s - jnp.max(s, axis=1, keepdims=True))
    return e / jnp.sum(e, axis=1, keepdims=True)


def _xa_fwd(name, x, gain, kx, vx, wq, wo):
    n_tok, tm = x.shape[0], min(ROW_BLOCK, x.shape[0])

    def body(i, n, x_ref, g_ref, kx_ref, vx_ref, wq_ref, wo_ref, xo_ref):
        x_v = x_ref[...]
        xh, _ = _rms_stats(x_v)
        q_b = _dot((xh * g_ref[...]).astype(BF16), wq_ref[...]).astype(BF16)
        o = jnp.concatenate(
            [_dot(_xa_probs(q_b, kx_ref, hd).astype(BF16), vx_ref[:, hd * XA_D:(hd + 1) * XA_D]) for hd in range(XA_H)],
            axis=1)
        xo_ref[...] = x_v + _dot(o.astype(BF16), wo_ref[...])

    return _rowcall(name, body, n_tok, tm, [x], [gain, kx, vx], [wq, wo], [(D_MODEL, F32)], [])[0]


def _xa_bwd(name, x, dxo, gain, kx, vx, wq, wq_t, wo_t):
    n_tok, tm = x.shape[0], min(ROW_BLOCK, x.shape[0])
    n_mem = kx.shape[0]

    def body(i, n, x_ref, dxo_ref, g_ref, kx_ref, vx_ref, wq_ref, wqt_ref, wot_ref,
             dx_ref, h_ref, o_ref, dq_ref, dy_ref, dkx_ref, dvx_ref, dg_ref):
        g = g_ref[...]
        xh, r = _rms_stats(x_ref[...])
        h = (xh * g).astype(BF16)
        h_ref[...] = h
        q_b = _dot(h, wq_ref[...]).astype(BF16)
        dxo_v = dxo_ref[...]
        dy = dxo_v.astype(BF16)
        dy_ref[...] = dy
        do = _dot(dy, wot_ref[...])
        for hd in range(XA_H):
            cols = slice(hd * XA_D, (hd + 1) * XA_D)
            p = _xa_probs(q_b, kx_ref, hd)
            p_b = p.astype(BF16)
            o_ref[:, cols] = _dot(p_b, vx_ref[:, cols]).astype(BF16)
            do_h = do[:, cols].astype(BF16)
            dvx_ref[:, cols] += _dot_tn(p_b, do_h)
            dp = _dot_nt(do_h, vx_ref[:, cols])
            ds = p * (dp - jnp.sum(dp * p, axis=1, keepdims=True))
            ds_b = (ds * (XA_D ** -0.5)).astype(BF16)
            dq_ref[:, cols] = _dot(ds_b, kx_ref[:, cols]).astype(BF16)
            dkx_ref[:, cols] += _dot_tn(ds_b, q_b[:, cols])
        dh = _dot(dq_ref[...], wqt_ref[...])
        dx, dg = _rms_bwd(xh, r, g, dh)
        dx_ref[...] = dxo_v + dx
        dg_ref[...] += dg

    return _rowcall(name, body, n_tok, tm, [x, dxo], [gain, kx, vx], [wq, wq_t, wo_t],
                    [(D_MODEL, F32), (D_MODEL, BF16), (D_MODEL, BF16), (D_MODEL, BF16), (D_MODEL, BF16)],
                    [((n_mem, D_MODEL), F32), ((n_mem, D_MODEL), F32), ((1, D_MODEL), F32)])


def _loss_head(name, x, target, gain):
    n_tok, tm = x.shape[0], min(ROW_BLOCK, x.shape[0])

    def body(i, n, x_ref, t_ref, g_ref, dx_ref, loss_ref, dg_ref):
        g = g_ref[...]
        xh, r = _rms_stats(x_ref[...])
        err = xh * g - t_ref[...]
        loss_ref[...] += 0.5 * jnp.sum(jnp.sum(err * err, axis=1, keepdims=True) / D_MODEL, axis=0, keepdims=True)
        dx, dg = _rms_bwd(xh, r, g, err / D_MODEL)
        dx_ref[...] = dx
        dg_ref[...] += dg

    return _rowcall(name, body, n_tok, tm, [x, target], [gain], [], [(D_MODEL, F32)],
                    [((8, LANES), F32), ((1, D_MODEL), F32)])


def _adamw(name, w, g, m, v):
    rows, cols = w.shape
    tr = 128 if rows % 128 == 0 else rows

    def kern(w_ref, g_ref, m_ref, v_ref, d_ref, mo_ref, vo_ref):
        g_v = g_ref[...]
        m_new = ADAM_B1 * m_ref[...] + (1.0 - ADAM_B1) * g_v
        v_new = ADAM_B2 * v_ref[...] + (1.0 - ADAM_B2) * (g_v * g_v)
        m_hat = m_new / (1.0 - ADAM_B1 ** ADAM_STEP)
        v_hat = v_new / (1.0 - ADAM_B2 ** ADAM_STEP)
        d_ref[...] = -ADAM_LR * (m_hat / (jnp.sqrt(v_hat) + ADAM_EPS) + ADAM_WD * w_ref[...])
        mo_ref[...] = m_new
        vo_ref[...] = v_new

    spec = pl.BlockSpec((tr, cols), lambda i: (i, 0))
    shp = jax.ShapeDtypeStruct((rows, cols), F32)
    return pl.pallas_call(kern, name=name, grid=(rows // tr,), in_specs=[spec] * 4, out_specs=[spec] * 3,
                          out_shape=[shp] * 3, compiler_params=_params(1))(w, g, m, v)


def _add_pair(name, gp, got, core):
    _, n, rows, cols = gp.shape
    spec = pl.BlockSpec((1, PACK_ROW_BLOCK, cols), lambda j, r, core_ref: (j, r, 0))
    mine = pl.BlockSpec((1, PACK_ROW_BLOCK, cols), lambda j, r, core_ref: (core_ref[0] * n + j, r, 0))

    def kern(core_ref, a_ref, b_ref, o_ref):
        o_ref[...] = (a_ref[...].astype(F32) + b_ref[...].astype(F32)).astype(BF16)

    grid_spec = pltpu.PrefetchScalarGridSpec(num_scalar_prefetch=1, grid=(n, rows // PACK_ROW_BLOCK),
                                             in_specs=[mine, spec], out_specs=spec)
    return pl.pallas_call(kern, name=name, grid_spec=grid_spec, out_shape=jax.ShapeDtypeStruct(got.shape, BF16),
                          compiler_params=_params(2))(core, gp.reshape(2 * n, rows, cols), got)


def _sum_parts(name, s, got, chip):
    _, rows, cols = s.shape
    tr = PACK_ROW_BLOCK

    def kern(chip_ref, s_ref, g_ref, o_ref):
        acc = s_ref[0].astype(F32)
        for k in range(3):
            acc = acc + g_ref[k].astype(F32)
        o_ref[...] = acc

    grid_spec = pltpu.PrefetchScalarGridSpec(
        num_scalar_prefetch=1, grid=(rows // tr,),
        in_specs=[pl.BlockSpec((1, tr, cols), lambda r, chip_ref: (chip_ref[0], r, 0)),
                  pl.BlockSpec((3, tr, cols), lambda r, chip_ref: (0, r, 0))],
        out_specs=pl.BlockSpec((tr, cols), lambda r, chip_ref: (r, 0)))
    return pl.pallas_call(kern, name=name, grid_spec=grid_spec, out_shape=jax.ShapeDtypeStruct((rows, cols), F32),
                          compiler_params=_params(1))(chip, s, got)


def _sum_slots(name, a, tr):
    n, rows, cols = a.shape

    def kern(a_ref, o_ref):
        acc = a_ref[0].astype(F32)
        for k in range(1, n):
            acc = acc + a_ref[k].astype(F32)
        o_ref[...] = acc

    return pl.pallas_call(kern, name=name, grid=(rows // tr,),
                          in_specs=[pl.BlockSpec((n, tr, cols), lambda r: (0, r, 0))],
                          out_specs=pl.BlockSpec((tr, cols), lambda r: (r, 0)),
                          out_shape=jax.ShapeDtypeStruct((rows, cols), F32), compiler_params=_params(1))(a)


_ANY = pl.BlockSpec(memory_space=pl.ANY)


COMM_CHUNKS = 13


def _place():
    x, y, c = lax.axis_index("x"), lax.axis_index("y"), lax.axis_index("c")
    chips = [(1 - x, y), (x, 1 - y), (1 - x, 1 - y)]
    return x, y, c, chips


def _rows(ref, q, n):
    step = ref.shape[0] // n
    return ref.at[pl.ds(q * step, step)]


def _gather_chips(name, w):
    n = COMM_CHUNKS

    def kern(w_ref, o_ref, send_sems, recv_sems):
        x, y, c, chips = _place()
        me, sib = 2 * x + y, (x, y, 1 - c)

        def copy(k, src, dst, to):
            return pltpu.make_async_remote_copy(src_ref=src, dst_ref=dst, send_sem=send_sems.at[k],
                                                recv_sem=recv_sems.at[k], device_id=to, device_id_type=MESH_ID)

        first = [copy(k * n + q, _rows(w_ref.at[c], q, n), _rows(o_ref.at[me, c], q, n), (cx, cy, c))
                 for k, (cx, cy) in enumerate(chips) for q in range(n)]
        for cp in first:
            cp.start()
        passed = []
        for k, (cx, cy) in enumerate(chips):
            for q in range(n):
                landed = _rows(o_ref.at[2 * cx + cy, c], q, n)
                copy(k * n + q, landed, landed, (x, y, c)).wait_recv()
                passed.append(copy((3 + k) * n + q, landed, landed, sib))
                passed[-1].start()
        for k, (cx, cy) in enumerate(chips):
            for q in range(n):
                theirs = _rows(o_ref.at[2 * cx + cy, 1 - c], q, n)
                copy((3 + k) * n + q, theirs, theirs, (x, y, c)).wait_recv()
        for cp in first + passed:
            cp.wait_send()

    return pl.pallas_call(
        kern, name=name, in_specs=[_ANY], out_specs=_ANY,
        out_shape=jax.ShapeDtypeStruct((N_CHIPS,) + w.shape, w.dtype),
        scratch_shapes=[pltpu.SemaphoreType.DMA((6 * n,)), pltpu.SemaphoreType.DMA((6 * n,))],
        compiler_params=pltpu.CompilerParams(has_side_effects=True),
    )(w)


def _gather_all(name, v):
    def kern(v_ref, o_ref, send_sems, recv_sems, local_sem):
        x, y, c, _ = _place()
        me = 4 * x + 2 * y + c
        mine = pltpu.make_async_copy(v_ref, o_ref.at[me], local_sem)
        mine.start()
        peers = []
        for k in range(1, 8):
            px = 1 - x if k & 4 else x
            py = 1 - y if k & 2 else y
            pc = 1 - c if k & 1 else c
            peers.append((px, py, pc))
        copies = [pltpu.make_async_remote_copy(src_ref=v_ref, dst_ref=o_ref.at[me], send_sem=send_sems.at[k],
                                               recv_sem=recv_sems.at[k], device_id=p, device_id_type=MESH_ID)
                  for k, p in enumerate(peers)]
        for cp in copies:
            cp.start()
        for k, (px, py, pc) in enumerate(peers):
            pltpu.make_async_remote_copy(src_ref=v_ref, dst_ref=o_ref.at[4 * px + 2 * py + pc], send_sem=send_sems.at[k],
                                         recv_sem=recv_sems.at[k], device_id=(x, y, c), device_id_type=MESH_ID).wait_recv()
        for cp in copies:
            cp.wait_send()
        mine.wait()

    return pl.pallas_call(
        kern, name=name, in_specs=[_ANY], out_specs=_ANY, out_shape=jax.ShapeDtypeStruct((8,) + v.shape, v.dtype),
        scratch_shapes=[pltpu.SemaphoreType.DMA((7,)), pltpu.SemaphoreType.DMA((7,)), pltpu.SemaphoreType.DMA(())],
        compiler_params=pltpu.CompilerParams(has_side_effects=True),
    )(v)


def _swap_halves(name, gp):
    n = COMM_CHUNKS
    n_slots = gp.shape[1]

    def kern(g_ref, got_ref, send_sems, recv_sems):
        x, y, c, _ = _place()
        swaps = [pltpu.make_async_remote_copy(src_ref=_rows(g_ref.at[1 - c, j], q, n), dst_ref=_rows(got_ref.at[j], q, n),
                                              send_sem=send_sems.at[j * n + q], recv_sem=recv_sems.at[j * n + q],
                                              device_id=(x, y, 1 - c), device_id_type=MESH_ID)
                 for j in range(n_slots) for q in range(n)]
        for cp in swaps:
            cp.start()
        for cp in swaps:
            cp.wait()

    return pl.pallas_call(
        kern, name=name, in_specs=[_ANY], out_specs=_ANY, out_shape=jax.ShapeDtypeStruct(gp.shape[1:], gp.dtype),
        scratch_shapes=[pltpu.SemaphoreType.DMA((n_slots * n,)), pltpu.SemaphoreType.DMA((n_slots * n,))],
        compiler_params=pltpu.CompilerParams(has_side_effects=True),
    )(gp)


def _scatter_chips(name, s):
    n = COMM_CHUNKS

    def kern(s_ref, o_ref, send_sems, recv_sems):
        x, y, c, chips = _place()
        copies = [pltpu.make_async_remote_copy(src_ref=_rows(s_ref.at[2 * cx + cy], q, n), dst_ref=_rows(o_ref.at[k], q, n),
                                               send_sem=send_sems.at[k * n + q], recv_sem=recv_sems.at[k * n + q],
                                               device_id=(cx, cy, c), device_id_type=MESH_ID)
                  for k, (cx, cy) in enumerate(chips) for q in range(n)]
        for cp in copies:
            cp.start()
        for k in range(3):
            for q in range(n):
                landed = _rows(o_ref.at[k], q, n)
                pltpu.make_async_remote_copy(src_ref=landed, dst_ref=landed, send_sem=send_sems.at[k * n + q],
                                             recv_sem=recv_sems.at[k * n + q], device_id=(x, y, c),
                                             device_id_type=MESH_ID).wait_recv()
        for cp in copies:
            cp.wait_send()

    return pl.pallas_call(
        kern, name=name, in_specs=[_ANY], out_specs=_ANY, out_shape=jax.ShapeDtypeStruct((3,) + s.shape[1:], s.dtype),
        scratch_shapes=[pltpu.SemaphoreType.DMA((3 * n,)), pltpu.SemaphoreType.DMA((3 * n,))],
        compiler_params=pltpu.CompilerParams(has_side_effects=True),
    )(s)


def _join_halves(name, r):
    n = COMM_CHUNKS

    def kern(r_ref, o_ref, send_sems, recv_sems):
        x, y, c, _ = _place()
        swaps = [pltpu.make_async_remote_copy(src_ref=_rows(r_ref, q, n), dst_ref=_rows(o_ref, q, n),
                                              send_sem=send_sems.at[q], recv_sem=recv_sems.at[q],
                                              device_id=(x, y, 1 - c), device_id_type=MESH_ID) for q in range(n)]
        for cp in swaps:
            cp.start()
        for cp in swaps:
            cp.wait()

    return pl.pallas_call(
        kern, name=name, in_specs=[_ANY], out_specs=_ANY, out_shape=jax.ShapeDtypeStruct(r.shape, r.dtype),
        scratch_shapes=[pltpu.SemaphoreType.DMA((n,))] * 2, compiler_params=pltpu.CompilerParams(has_side_effects=True),
    )(r)


PACK_ALIGN = 16


def _exchange_rows(nm, shard_shape):
    if BIG_KIND[nm] == "cols":
        rows = shard_shape[-1]
    else:
        rows = 1
        for s in shard_shape[1:]:
            rows *= s
        rows //= PACK_COLS
    return -(-rows // PACK_ALIGN) * PACK_ALIGN


def _to_exchange(nm, a, transposed=False):
    if BIG_KIND[nm] == "cols":
        if not transposed:
            a = jnp.swapaxes(a, 1, 2)
        rows = -(-a.shape[1] // PACK_ALIGN) * PACK_ALIGN
        a = jnp.pad(a, ((0, 0), (0, rows - a.shape[1]), (0, 0)))
        return a.reshape(a.shape[0] * rows, PACK_COLS)
    return a.reshape(a.shape[0] * _exchange_rows(nm, a.shape), PACK_COLS)


def _from_exchange(nm, rows2d, shard_shape):
    lead = rows2d.shape[:-2]
    n_layers = shard_shape[0]
    if BIG_KIND[nm] == "cols":
        a = rows2d.reshape(lead + (n_layers, -1, PACK_COLS))
        return lax.slice_in_dim(a, 0, shard_shape[-1], axis=a.ndim - 2)
    return rows2d.reshape(lead + tuple(shard_shape))


def _pack(pieces, dtype):
    slab = jnp.concatenate([p.astype(dtype) for p in pieces])
    half = -(-slab.shape[0] // (2 * PACK_ROW_BLOCK)) * PACK_ROW_BLOCK
    slab = jnp.pad(slab, ((0, 2 * half - slab.shape[0]), (0, 0)))
    return slab.reshape(2, half, PACK_COLS)


def _unpack(packed, row_counts):
    lead = packed.shape[:-3]
    slab = packed.reshape(lead + (-1, PACK_COLS))
    out, off = [], 0
    for rows in row_counts:
        out.append(lax.slice_in_dim(slab, off, off + rows, axis=slab.ndim - 2))
        off += rows
    return out


def _t(a):
    return jnp.swapaxes(a, -1, -2)


def _rows_from_lanes(a):
    return a[:, :FOX_H].T[:, None, :]


def _lanes_from_heads(a):
    return jnp.pad(a[:, 0, :].T, ((0, 0), (0, LANES - FOX_H)))


def _layer_weights(wts, small, l):
    w_in_t = wts["w_in"][l]
    w = {
        "wf_t": jnp.pad(w_in_t[W1_COLS:W1_COLS + FOX_H], ((0, LANES - FOX_H), (0, 0))),
        "w1_t": w_in_t[:W1_COLS], "wgl_t": w_in_t[W1_COLS + FOX_H:],
    }
    for nm, _ in BIG:
        if nm != "w_in":
            w[nm + "_t" if BIG_KIND[nm] == "cols" else nm] = wts[nm][l]
    for nm in list(w):
        other = nm[:-2] if nm.endswith("_t") else nm + "_t"
        w[other] = _t(w[nm])
    for nm in ("ffn1_norm", "mix_norm", "sg_norm", "xa_norm", "mem_norm", "ffn2_norm"):
        w[nm] = small[nm][l][None, :]
    w["conv_w"] = jnp.pad(small["conv_w"][l], ((0, 5), (0, 0)))
    w["sg_w"] = small["sg_w"][l]
    w["sg_w_t"] = _t(small["sg_w"][l])
    w["sg_bias"] = jnp.repeat(small["sg_b"][l].T, CHUNK, axis=1)
    w["fox_b"] = jnp.pad(small["fox_b_f"][l][None, :], ((0, 0), (0, LANES - FOX_H)))
    return w


def _layer_fwd(l, x, mem, w):
    s = {"x0": x}
    x1, s["a1"], s["b1"] = _ffn_fwd(f"ffn1_fwd_{l}", x, w["ffn1_norm"], w["ffn1_w_gate"], w["ffn1_w_up"], w["ffn1_w_down"])
    s["x1"] = x1
    cin, sgin, qkv, fl, gl = _proj_fwd(f"proj_fwd_{l}", x1, w["mix_norm"], w["w1"], w["wf"], w["wgl"])
    ya = _conv_fwd(f"conv_fwd_{l}", cin, w["conv_w"])
    yb = _sg_fwd(f"sg_fwd_{l}", sgin, w["sg_norm"], w["sg_w"], w["sg_bias"])
    c2 = _forget_cumsum(f"forget_fwd_{l}", fl, w["fox_b"])
    tables = _prune_tables(_flash_bounds(f"flash_bounds_{l}", qkv), c2, min(FLASH_BLOCK, qkv.shape[0]))
    yc, lse2 = _flash_fwd(f"flash_fwd_{l}", qkv, c2, _rows_from_lanes(c2), tables)
    x2 = _merge_fwd(f"merge_fwd_{l}", x1, ya, yb, yc, gl, w["w_branch"], w["w_out"])
    s.update(cin=cin, sgin=sgin, qkv=qkv, fl=fl, gl=gl, ya=ya, yb=yb, yc=yc, c2=c2, lse2=lse2, x2=x2, tables=tables)
    s["mn"], s["kx"], s["vx"] = _mem_fwd(f"mem_fwd_{l}", mem, w["mem_norm"], w["xa_wk"], w["xa_wv"])
    x3 = _xa_fwd(f"xa_fwd_{l}", x2, w["xa_norm"], s["kx"], s["vx"], w["xa_wq"], w["xa_wo"])
    s["x3"] = x3
    x4, s["a2"], s["b2"] = _ffn_fwd(f"ffn2_fwd_{l}", x3, w["ffn2_norm"], w["ffn2_w_gate"], w["ffn2_w_up"], w["ffn2_w_down"])
    return x4, s


def _ffn_grads(tag, x, dxo, a, b, w, pre):
    dx, da, db, sv, h, dy, dg = _ffn_bwd(f"{pre}_bwd_{tag}", x, dxo, a, b, w[pre + "_norm"], w[pre + "_w_down_t"],
                                         w[pre + "_w_gate_t"], w[pre + "_w_up_t"])
    g = {
        pre + "_norm": dg[0],
        pre + "_w_gate": _mm_tn(f"{pre}_dwg_{tag}", da, h, D_FF, D_MODEL, tk=FF_SPLIT, tn=D_MODEL),
        pre + "_w_up": _mm_tn(f"{pre}_dwu_{tag}", db, h, D_FF, D_MODEL, tk=FF_SPLIT, tn=D_MODEL),
        pre + "_w_down": _mm_tn(f"{pre}_dwd_{tag}", sv, dy, D_FF, D_MODEL, tk=FF_SPLIT, tn=D_MODEL),
    }
    return dx, g


def _layer_bwd(l, dx, mem, w, s):
    g = {}
    dx, gf = _ffn_grads(l, s["x3"], dx, s["a2"], s["b2"], w, "ffn2")
    g.update(gf)

    dx, h, o, dq, dy, dkx, dvx, dg = _xa_bwd(f"xa_bwd_{l}", s["x2"], dx, w["xa_norm"], s["kx"], s["vx"], w["xa_wq"],
                                              w["xa_wq_t"], w["xa_wo_t"])
    g["xa_norm"] = dg[0]
    g["xa_wq"] = _mm_tn(f"xa_dwq_{l}", h, dq, D_MODEL, D_MODEL, tk=D_MODEL, tn=D_MODEL)
    g["xa_wo"] = _mm_tn(f"xa_dwo_{l}", o, dy, D_MODEL, D_MODEL, tk=D_MODEL, tn=D_MODEL)
    dkb, dvb, dgm = _mem_bwd(f"mem_bwd_{l}", mem, w["mem_norm"], dkx, dvx, w["xa_wk_t"], w["xa_wv_t"])
    g["mem_norm"] = dgm[0]
    g["xa_wk"] = _mm_tn(f"xa_dwk_{l}", s["mn"], dkb, D_MODEL, D_MODEL, tk=D_MODEL, tn=D_MODEL)
    g["xa_wv"] = _mm_tn(f"xa_dwv_{l}", s["mn"], dvb, D_MODEL, D_MODEL, tk=D_MODEL, tn=D_MODEL)

    mg, dob, dgl, dbr, dya, dyb, dyc = _merge_bwd(f"merge_bwd_{l}", dx, s["ya"], s["yb"], s["yc"], s["gl"], w["w_branch"],
                                                  w["w_branch_t"], w["w_out_t"])
    g["w_out"] = _mm_tn(f"dwout_{l}", mg, dob, D_MODEL, D_MODEL, tk=D_MODEL, tn=D_MODEL)
    g["w_branch"] = jnp.stack([
        _mm_tn(f"dwbranch{k}_{l}", y, dbr, CONV_W, D_MODEL, tk=CONV_W, tn=D_MODEL, y_off=k * D_MODEL)
        for k, y in enumerate((s["ya"], s["yb"], s["yc"]))])

    dcin, dcw = _conv_bwd(f"conv_bwd_{l}", s["cin"], dya, w["conv_w"])
    g["conv_w"] = dcw[:3]
    dsg, dgn, dsw, dsb = _sg_bwd(f"sg_bwd_{l}", s["sgin"], dyb, w["sg_norm"], w["sg_w"], w["sg_w_t"], w["sg_bias"])
    g["sg_norm"], g["sg_w"], g["sg_b"] = dgn[0], dsw, dsb[:, :, 0]
    d_row = _rows_from_lanes(_flash_delta(f"flash_delta_{l}", dyc, s["yc"]))
    r_row = s["lse2"].reshape(FOX_H, 1, -1) - _rows_from_lanes(s["c2"])
    dq, dk, dv, dc_rows = _flash_bwd(f"flash_bwd_{l}", s["qkv"], dyc, r_row, d_row, s["c2"], s["tables"])
    dc = _lanes_from_heads(dc_rows.reshape(FOX_H, 1, -1))
    dfl, dbf = _forget_bwd(f"forget_bwd_{l}", dc, s["fl"], w["fox_b"])
    g["fox_b_f"] = dbf[0, :FOX_H]

    dx, h, dqkv, dfb, dg = _proj_bwd(f"proj_bwd_{l}", s["x1"], dx, dcin, dsg, dq, dk, dv, dfl, dgl, w["mix_norm"],
                                     w["w1_t"], w["wf_t"], w["wgl_t"])
    g["mix_norm"] = dg[0]
    c0, c1 = 3 * CONV_W, 3 * CONV_W + 2 * SG_W
    g["w_in"] = jnp.concatenate([
        _mm_tn(f"dwin_conv_{l}", dcin, h, c0, D_MODEL, tk=c0, tn=D_MODEL),
        _mm_tn(f"dwin_sg_{l}", dsg, h, c1 - c0, D_MODEL, tk=c1 - c0, tn=D_MODEL),
        _mm_tn(f"dwin_qkv_{l}", dqkv, h, 3 * FOX_W, D_MODEL, tk=3 * FOX_W, tn=D_MODEL),
        _mm_tn(f"dwin_f_{l}", dfb, h, LANES, D_MODEL, tk=LANES, tn=D_MODEL)[:FOX_H],
        _mm_tn(f"dwin_gl_{l}", dgl, h, 3 * D_MODEL, D_MODEL, tk=3 * D_MODEL // 2, tn=D_MODEL),
    ], axis=0)

    dx, gf = _ffn_grads(l, s["x0"], dx, s["a1"], s["b1"], w, "ffn1")
    g.update(gf)
    return dx, g


def _local_step(x, mem, target, wts, small):
    saved, lw = [], []
    for l in range(DEPTH):
        lw.append(_layer_weights(wts, small, l))
        x, s = _layer_fwd(l, x, mem, lw[l])
        saved.append(s)
    fin = small["final_norm"][None, :]
    dx, loss, dgf = _loss_head("loss_head", x, target, fin)
    layer_grads = [None] * DEPTH
    for l in reversed(range(DEPTH)):
        dx, layer_grads[l] = _layer_bwd(l, dx, mem, lw[l], saved[l])
    grads = {nm: jnp.stack([layer_grads[l][nm] for l in range(DEPTH)]) for nm in WEIGHTS if nm != "final_norm"}
    grads["final_norm"] = dgf[0]
    return loss[0, 0], dx, grads


def _small_slab(vals):
    rows = []
    for v in vals:
        flat = v.astype(F32).reshape(-1)
        n = -(-flat.shape[0] // LANES) * LANES
        rows.append(jnp.pad(flat, (0, n - flat.shape[0])).reshape(-1, LANES))
    slab = jnp.concatenate(rows)
    pad = -slab.shape[0] % 8
    return jnp.pad(slab, ((0, pad), (0, 0)))


def _small_unslab(slab, shapes):
    out, off = [], 0
    for shp in shapes:
        size = 1
        for s in shp:
            size *= s
        n_rows = -(-size // LANES)
        out.append(slab[off:off + n_rows].reshape(-1)[:size].reshape(shp))
        off += n_rows
    return out


def kernel(x, mem, ffn1_norm, ffn1_w_gate, ffn1_w_up, ffn1_w_down, mix_norm, w_in, conv_w, sg_norm, sg_w, sg_b, fox_b_f, w_branch, w_out, xa_norm, mem_norm, xa_wq, xa_wk, xa_wv, xa_wo, ffn2_norm, ffn2_w_gate, ffn2_w_up, ffn2_w_down, final_norm, loss_target, m_ffn1_norm, m_ffn1_w_gate, m_ffn1_w_up, m_ffn1_w_down, m_mix_norm, m_w_in, m_conv_w, m_sg_norm, m_sg_w, m_sg_b, m_fox_b_f, m_w_branch, m_w_out, m_xa_norm, m_mem_norm, m_xa_wq, m_xa_wk, m_xa_wv, m_xa_wo, m_ffn2_norm, m_ffn2_w_gate, m_ffn2_w_up, m_ffn2_w_down, m_final_norm, v_ffn1_norm, v_ffn1_w_gate, v_ffn1_w_up, v_ffn1_w_down, v_mix_norm, v_w_in, v_conv_w, v_sg_norm, v_sg_w, v_sg_b, v_fox_b_f, v_w_branch, v_w_out, v_xa_norm, v_mem_norm, v_xa_wq, v_xa_wk, v_xa_wv, v_xa_wo, v_ffn2_norm, v_ffn2_w_gate, v_ffn2_w_up, v_ffn2_w_down, v_final_norm):
    args = dict(locals())
    wv = {nm: args[nm] for nm in WEIGHTS}
    mv = {nm: args["m_" + nm] for nm in WEIGHTS}
    vv = {nm: args["v_" + nm] for nm in WEIGHTS}
    chip = 2 * lax.axis_index("x") + lax.axis_index("y")
    core = lax.axis_index("c")

    names = [nm for nm, _ in BIG]
    mine = {nm: _to_exchange(nm, wv[nm].astype(BF16)) for nm in names}
    row_counts = [mine[nm].shape[0] for nm in names]
    gathered = _gather_chips("gather_weights", _pack([mine[nm] for nm in names], BF16))
    wts = {}
    for nm, slab in zip(names, _unpack(gathered, row_counts)):
        p = _from_exchange(nm, jnp.where((jnp.arange(N_CHIPS) == chip)[:, None, None], mine[nm][None], slab), wv[nm].shape)
        if BIG_KIND[nm] == "branch":
            wts[nm] = jnp.concatenate([p[j] for j in range(N_CHIPS)], axis=3)
        else:
            wts[nm] = jnp.swapaxes(p, 0, 1).reshape(p.shape[1], N_CHIPS * p.shape[2], PACK_COLS)
    taps = _gather_all("gather_taps", _small_slab([conv_w]))
    taps = [_small_unslab(taps[2 * j], [conv_w.shape])[0] for j in range(N_CHIPS)]
    small = {nm: wv[nm] for nm in SMALL}
    small["conv_w"] = jnp.concatenate(taps, axis=2)

    loss, dx, grads = _local_step(x[0], mem[0], loss_target[0], wts, small)
    loss = lax.psum(loss, ("x", "y", "c"))

    small_shapes = [grads[nm].shape for nm in SMALL]
    slots = _gather_all("gather_small_grads", _small_slab([grads[nm] for nm in SMALL]))
    small_sum = _sum_slots("sum_small_grads", slots, slots.shape[1])
    small_g = dict(zip(SMALL, _small_unslab(small_sum, small_shapes)))
    taps_g = small_g["conv_w"]
    small_g["conv_w"] = lax.dynamic_slice_in_dim(taps_g, chip * conv_w.shape[2], conv_w.shape[2], axis=2)

    def grad_shard(nm, ax, j):
        g = grads[nm]
        ax = 1 if BIG_KIND[nm] == "cols" else ax
        size = g.shape[ax] // N_CHIPS
        return _to_exchange(nm, lax.slice_in_dim(g, j * size, (j + 1) * size, axis=ax), transposed=True)

    gp = jnp.stack([_pack([grad_shard(nm, ax, j) for nm, ax in BIG], BF16) for j in range(N_CHIPS)], axis=1)
    got = _swap_halves("reduce_swap_cores", gp)
    chip_sum = _add_pair("reduce_add_cores", gp, got, core.astype(jnp.int32).reshape(1))
    parts = _scatter_chips("reduce_scatter_chips", chip_sum)
    half = _sum_parts("reduce_add_chips", chip_sum, parts, chip.astype(jnp.int32).reshape(1))
    other = _join_halves("reduce_join_cores", half)
    full = jnp.where(core == 0, jnp.stack([half, other]), jnp.stack([other, half]))
    big_g = {}
    for nm, slab in zip(names, _unpack(full, row_counts)):
        g = _from_exchange(nm, slab, wv[nm].shape)
        big_g[nm] = jnp.swapaxes(g, 1, 2) if BIG_KIND[nm] == "cols" else g

    g_out = {**small_g, **big_g}
    delta, new_m, new_v = {}, {}, {}
    for nm, _ in BIG:
        shp = wv[nm].shape
        two_d = (-1, shp[-1])
        d, m2, v2 = _adamw("adamw_" + nm, wv[nm].reshape(two_d), g_out[nm].reshape(two_d), mv[nm].reshape(two_d),
                           vv[nm].reshape(two_d))
        delta[nm], new_m[nm], new_v[nm] = d.reshape(shp), m2.reshape(shp), v2.reshape(shp)
    slab_shapes = [wv[nm].shape for nm in SMALL]
    d, m2, v2 = _adamw("adamw_small", _small_slab([wv[nm] for nm in SMALL]), _small_slab([g_out[nm] for nm in SMALL]),
                       _small_slab([mv[nm] for nm in SMALL]), _small_slab([vv[nm] for nm in SMALL]))
    for out, slab in ((delta, d), (new_m, m2), (new_v, v2)):
        out.update(zip(SMALL, _small_unslab(slab, slab_shapes)))

    return (loss, dx[None], *[g_out[nm] for nm in WEIGHTS], *[delta[nm] for nm in WEIGHTS],
            *[new_m[nm] for nm in WEIGHTS], *[new_v[nm] for nm in WEIGHTS])
```

```python
import functools

import jax
import jax.numpy as jnp
from jax import lax
from jax.experimental import pallas as pl
from jax.experimental.pallas import tpu as pltpu

F32, BF16 = jnp.float32, jnp.bfloat16
MESH_ID = pl.DeviceIdType.MESH

D_MODEL = 1024
DEPTH = 2
D_FF = 2816
CONV_W = 512
SG_W = 512
SG_G = 4
CHUNK = 128
FOX_H = 8
FOX_D = 64
FOX_W = FOX_H * FOX_D
FOX_SCALE = FOX_D ** -0.5
LOG2E = 1.4426950408889634
XA_H = 4
XA_D = D_MODEL // XA_H
N_CHIPS = 4
RMS_EPS = 1e-6
W1_COLS = 3 * CONV_W + 2 * SG_W + 3 * FOX_W
LANES = 128
HALO = 16

ADAM_LR, ADAM_B1, ADAM_B2, ADAM_EPS, ADAM_WD, ADAM_STEP = 0.001, 0.9, 0.999, 1e-08, 0.01, 10

ROW_BLOCK = 512
FFN_BWD_ROW_BLOCK = 256
MM_TT = 2048
FLASH_BLOCK = 512
FLASH_STRIP = 32
PRUNE_LOG2 = 40.0
FF_SPLIT = 1408
VMEM_LIMIT = 56 * 1024 * 1024
PACK_COLS = 1024
PACK_ROW_BLOCK = 208

BIG = (
    ("ffn1_w_gate", 2), ("ffn1_w_up", 2), ("ffn1_w_down", 1), ("w_in", 2), ("w_branch", 3), ("w_out", 1),
    ("xa_wq", 1), ("xa_wk", 1), ("xa_wv", 1), ("xa_wo", 1), ("ffn2_w_gate", 2), ("ffn2_w_up", 2), ("ffn2_w_down", 1),
)
BIG_KIND = {nm: ("branch" if nm == "w_branch" else "cols" if ax == 2 else "rows") for nm, ax in BIG}
SMALL = ("ffn1_norm", "mix_norm", "conv_w", "sg_norm", "sg_w", "sg_b", "fox_b_f", "xa_norm", "mem_norm", "ffn2_norm",
         "final_norm")
WEIGHTS = ("ffn1_norm", "ffn1_w_gate", "ffn1_w_up", "ffn1_w_down", "mix_norm", "w_in", "conv_w", "sg_norm", "sg_w",
           "sg_b", "fox_b_f", "w_branch", "w_out", "xa_norm", "mem_norm", "xa_wq", "xa_wk", "xa_wv", "xa_wo",
           "ffn2_norm", "ffn2_w_gate", "ffn2_w_up", "ffn2_w_down", "final_norm")


def _dot(a, b):
    return jnp.dot(a, b, preferred_element_type=F32)


def _dot_nt(a, b):
    return lax.dot_general(a, b, (((1,), (1,)), ((), ())), preferred_element_type=F32)


def _dot_tn(a, b):
    return lax.dot_general(a, b, (((0,), (0,)), ((), ())), preferred_element_type=F32)


def _rms_stats(x):
    r = lax.rsqrt(jnp.mean(x * x, axis=-1, keepdims=True) + RMS_EPS)
    return x * r, r


def _rms_bwd(xh, r, g, dy):
    dg = jnp.sum(dy * xh, axis=0, keepdims=True)
    dxh = dy * g
    dx = r * (dxh - xh * jnp.mean(dxh * xh, axis=-1, keepdims=True))
    return dx, dg


def _gelu(x):
    k = 0.7978845608028654
    t = jnp.tanh(k * (x + 0.044715 * x * x * x))
    return 0.5 * x * (1.0 + t), t


def _gelu_grad(x, t):
    k = 0.7978845608028654
    return 0.5 * (1.0 + t) + 0.5 * x * (1.0 - t * t) * k * (1.0 + 3.0 * 0.044715 * x * x)


def _split3_dot(tri, x):
    hi = x.astype(BF16)
    r1 = x - hi.astype(F32)
    mid = r1.astype(BF16)
    lo = (r1 - mid.astype(F32)).astype(BF16)
    return _dot(tri, hi) + _dot(tri, mid) + _dot(tri, lo)


def _params(n_grid):
    return pltpu.CompilerParams(dimension_semantics=("arbitrary",) * n_grid, vmem_limit_bytes=VMEM_LIMIT)


def _full_spec(shape):
    nd = len(shape)
    return pl.BlockSpec(tuple(shape), lambda *_: (0,) * nd)


def _rowcall(name, body, n_tokens, tm, rows, consts, residents, row_outs, acc_outs, scratch=()):
    n = n_tokens // tm
    rows = [r if isinstance(r, tuple) else (r, pl.BlockSpec((tm, r.shape[1]), lambda i: (i, 0))) for r in rows]
    n_in, n_w = len(rows) + len(consts), len(residents)
    n_out = len(row_outs) + len(acc_outs)
    in_specs = ([r[1] for r in rows] + [_full_spec(c.shape) for c in consts]
                + [pl.BlockSpec(memory_space=pl.ANY)] * n_w)
    out_shape = ([jax.ShapeDtypeStruct((n_tokens, c), dt) for c, dt in row_outs]
                 + [jax.ShapeDtypeStruct(s, dt) for s, dt in acc_outs])
    out_specs = ([pl.BlockSpec((tm, c), lambda i: (i, 0)) for c, _ in row_outs]
                 + [_full_spec(s) for s, _ in acc_outs])
    scratch_shapes = [pltpu.VMEM(w.shape, w.dtype) for w in residents]
    if n_w:
        scratch_shapes.append(pltpu.SemaphoreType.DMA((n_w,)))
    scratch_shapes += list(scratch)

    def kern(*refs):
        ins, w_hbm = refs[:n_in], refs[n_in:n_in + n_w]
        outs = refs[n_in + n_w:n_in + n_w + n_out]
        rest = refs[n_in + n_w + n_out:]
        w_vmem = rest[:n_w]
        extra = rest[n_w + 1:] if n_w else rest
        i = pl.program_id(0)

        @pl.when(i == 0)
        def _():
            copies = [pltpu.make_async_copy(w_hbm[k], w_vmem[k], rest[n_w].at[k]) for k in range(n_w)]
            for cp in copies:
                cp.start()
            for cp in copies:
                cp.wait()
            for a in outs[len(row_outs):]:
                a[...] = jnp.zeros(a.shape, a.dtype)

        body(i, n, *ins, *w_vmem, *outs, *extra)

    res = pl.pallas_call(
        kern, name=name, grid=(n,), in_specs=in_specs, out_specs=out_specs, out_shape=out_shape,
        scratch_shapes=scratch_shapes, compiler_params=_params(1),
    )(*[r[0] for r in rows], *consts, *residents)
    return res


def _mm_tn(name, x, y, k_dim, n_dim, *, tk, tn, x_off=0, y_off=0, tt=MM_TT):
    n_tok = x.shape[0]
    tt = min(tt, n_tok)
    n_t = n_tok // tt
    xb, yb = x_off // tk, y_off // tn

    def kern(x_ref, y_ref, o_ref):
        @pl.when(pl.program_id(2) == 0)
        def _():
            o_ref[...] = jnp.zeros(o_ref.shape, F32)

        o_ref[...] += _dot_tn(x_ref[...], y_ref[...])

    return pl.pallas_call(
        kern, name=name, grid=(k_dim // tk, n_dim // tn, n_t),
        in_specs=[pl.BlockSpec((tt, tk), lambda k, n, t: (t, xb + k)),
                  pl.BlockSpec((tt, tn), lambda k, n, t: (t, yb + n))],
        out_specs=pl.BlockSpec((tk, tn), lambda k, n, t: (k, n)),
        out_shape=jax.ShapeDtypeStruct((k_dim, n_dim), F32),
        compiler_params=_params(3),
    )(x, y)


def _ffn_fwd(name, x, gain, wg, wu, wd):
    n_tok, tm = x.shape[0], min(ROW_BLOCK, x.shape[0])

    def body(i, n, x_ref, g_ref, wg_ref, wu_ref, wd_ref, xo_ref, a_ref, b_ref):
        x_v = x_ref[...]
        xh, _ = _rms_stats(x_v)
        h = (xh * g_ref[...]).astype(BF16)
        y = jnp.zeros((tm, D_MODEL), F32)
        for f0 in range(0, D_FF, FF_SPLIT):
            sl = slice(f0, f0 + FF_SPLIT)
            a = _dot(h, wg_ref[:, sl])
            b = _dot(h, wu_ref[:, sl])
            a_ref[:, sl] = a.astype(BF16)
            b_ref[:, sl] = b.astype(BF16)
            s = (a * jax.nn.sigmoid(a) * b).astype(BF16)
            y = y + _dot(s, wd_ref[sl, :])
        xo_ref[...] = x_v + 0.5 * y

    return _rowcall(name, body, n_tok, tm, [x], [gain], [wg, wu, wd],
                    [(D_MODEL, F32), (D_FF, BF16), (D_FF, BF16)], [])


def _ffn_bwd(name, x, dxo, a, b, gain, wd_t, wg_t, wu_t):
    n_tok, tm = x.shape[0], min(FFN_BWD_ROW_BLOCK, x.shape[0])

    def body(i, n, x_ref, dxo_ref, a_ref, b_ref, g_ref, wdt_ref, wgt_ref, wut_ref,
             dx_ref, da_ref, db_ref, s_ref, h_ref, dy_ref, dg_ref):
        g = g_ref[...]
        xh, r = _rms_stats(x_ref[...])
        h_ref[...] = (xh * g).astype(BF16)
        dxo_v = dxo_ref[...]
        dy = (0.5 * dxo_v).astype(BF16)
        dy_ref[...] = dy
        dh = jnp.zeros((tm, D_MODEL), F32)
        for f0 in range(0, D_FF, FF_SPLIT):
            sl = slice(f0, f0 + FF_SPLIT)
            a_v = a_ref[:, sl].astype(F32)
            b_v = b_ref[:, sl].astype(F32)
            ds = _dot(dy, wdt_ref[:, sl])
            sig = jax.nn.sigmoid(a_v)
            sa = a_v * sig
            s_ref[:, sl] = (sa * b_v).astype(BF16)
            da = (ds * b_v * (sig * (1.0 + a_v * (1.0 - sig)))).astype(BF16)
            db = (ds * sa).astype(BF16)
            da_ref[:, sl] = da
            db_ref[:, sl] = db
            dh = dh + _dot(da, wgt_ref[sl, :]) + _dot(db, wut_ref[sl, :])
        dx, dg = _rms_bwd(xh, r, g, dh)
        dx_ref[...] = dxo_v + dx
        dg_ref[...] += dg

    return _rowcall(name, body, n_tok, tm, [x, dxo, a, b], [gain], [wd_t, wg_t, wu_t],
                    [(D_MODEL, F32), (D_FF, BF16), (D_FF, BF16), (D_FF, BF16), (D_MODEL, BF16), (D_MODEL, BF16)],
                    [((1, D_MODEL), F32)])


def _proj_fwd(name, x, gain, w1, wf, wgl):
    n_tok, tm = x.shape[0], min(ROW_BLOCK, x.shape[0])
    c0, c1 = 3 * CONV_W, 3 * CONV_W + 2 * SG_W

    def body(i, n, x_ref, g_ref, w1_ref, wf_ref, wgl_ref, cin_ref, sg_ref, qkv_ref, fl_ref, gl_ref):
        xh, _ = _rms_stats(x_ref[...])
        h = (xh * g_ref[...]).astype(BF16)
        cin_ref[...] = _dot(h, w1_ref[:, 0:c0]).astype(BF16)
        sg_ref[...] = _dot(h, w1_ref[:, c0:c1]).astype(BF16)
        qkv_ref[...] = _dot(h, w1_ref[:, c1:W1_COLS]).astype(BF16)
        fl_ref[...] = _dot(h, wf_ref[...])
        gl_ref[...] = _dot(h, wgl_ref[...]).astype(BF16)

    return _rowcall(name, body, n_tok, tm, [x], [gain], [w1, wf, wgl],
                    [(3 * CONV_W, BF16), (2 * SG_W, BF16), (3 * FOX_W, BF16), (LANES, F32), (3 * D_MODEL, BF16)], [])


def _proj_bwd(name, x, dxin, dcin, dsg, dq, dk, dv, dfl, dgl, gain, w1_t, wf_t, wgl_t):
    n_tok, tm = x.shape[0], min(ROW_BLOCK, x.shape[0])
    c0, c1 = 3 * CONV_W, 3 * CONV_W + 2 * SG_W

    def body(i, n, x_ref, dxin_ref, dcin_ref, dsg_ref, dq_ref, dk_ref, dv_ref, dfl_ref, dgl_ref, g_ref,
             w1t_ref, wft_ref, wglt_ref, dx_ref, h_ref, dqkv_ref, dfb_ref, dg_ref):
        g = g_ref[...]
        xh, r = _rms_stats(x_ref[...])
        h_ref[...] = (xh * g).astype(BF16)
        dfb = dfl_ref[...].astype(BF16)
        dfb_ref[...] = dfb
        dh = _dot(dcin_ref[...], w1t_ref[0:c0, :])
        dh = dh + _dot(dsg_ref[...], w1t_ref[c0:c1, :])
        for k, d_ref in enumerate((dq_ref, dk_ref, dv_ref)):
            d_b = d_ref[...].astype(BF16)
            dqkv_ref[:, k * FOX_W:(k + 1) * FOX_W] = d_b
            dh = dh + _dot(d_b, w1t_ref[c1 + k * FOX_W:c1 + (k + 1) * FOX_W, :])
        dh = dh + _dot(dfb, wft_ref[...])
        dh = dh + _dot(dgl_ref[...], wglt_ref[...])
        dx, dg = _rms_bwd(xh, r, g, dh)
        dx_ref[...] = dxin_ref[...] + dx
        dg_ref[...] += dg

    return _rowcall(name, body, n_tok, tm, [x, dxin, dcin, dsg, dq, dk, dv, dfl, dgl], [gain], [w1_t, wf_t, wgl_t],
                    [(D_MODEL, F32), (D_MODEL, BF16), (3 * FOX_W, BF16), (LANES, BF16)], [((1, D_MODEL), F32)])


def _conv_taps(z, prev_z, i):
    tm = z.shape[0]
    row = lax.broadcasted_iota(jnp.int32, (tm, 1), 0)
    live = (i > 0).astype(F32)
    p1, p2 = prev_z[HALO - 1:HALO, :] * live, prev_z[HALO - 2:HALO - 1, :] * live
    z1 = jnp.where(row == 0, p1, pltpu.roll(z, 1, 0))
    z2 = jnp.where(row == 0, p2, jnp.where(row == 1, p1, pltpu.roll(z, 2, 0)))
    return z1, z2


def _prev_spec(tm, cols):
    return pl.BlockSpec((HALO, cols), lambda i: (jnp.maximum(i * (tm // HALO) - 1, 0), 0))


def _next_spec(tm, cols, n):
    last = n * (tm // HALO) - 1
    return pl.BlockSpec((HALO, cols), lambda i: (jnp.minimum((i + 1) * (tm // HALO), last), 0))


def _conv_fwd(name, cin, cw):
    n_tok, tm = cin.shape[0], min(ROW_BLOCK, cin.shape[0])
    w = CONV_W

    def body(i, n, c_ref, p_ref, cw_ref, ya_ref):
        c_v, p_v = c_ref[...].astype(F32), p_ref[...].astype(F32)
        z = c_v[:, w:2 * w] * c_v[:, 2 * w:]
        z1, z2 = _conv_taps(z, p_v[:, w:2 * w] * p_v[:, 2 * w:], i)
        y = cw_ref[0:1, :] * z2 + cw_ref[1:2, :] * z1 + cw_ref[2:3, :] * z
        ya_ref[...] = (c_v[:, 0:w] * y).astype(BF16)

    return _rowcall(name, body, n_tok, tm, [cin, (cin, _prev_spec(tm, 3 * w))], [cw], [], [(w, BF16)], [])[0]


def _conv_bwd(name, cin, dya, cw):
    n_tok, tm = cin.shape[0], min(ROW_BLOCK, cin.shape[0])
    w = CONV_W
    n_blocks = n_tok // tm

    def body(i, n, c_ref, p_ref, nx_ref, dya_ref, ndya_ref, cw_ref, dc_ref, dcw_ref):
        c_v, p_v = c_ref[...].astype(F32), p_ref[...].astype(F32)
        ab, ac, ah = c_v[:, 0:w], c_v[:, w:2 * w], c_v[:, 2 * w:]
        z = ac * ah
        z1, z2 = _conv_taps(z, p_v[:, w:2 * w] * p_v[:, 2 * w:], i)
        w0, w1, w2 = cw_ref[0:1, :], cw_ref[1:2, :], cw_ref[2:3, :]
        y = w0 * z2 + w1 * z1 + w2 * z
        dya_v = dya_ref[...].astype(F32)
        dy = dya_v * ab
        live = (i < n - 1).astype(F32)
        ndy = ndya_ref[...].astype(F32) * nx_ref[:, 0:w].astype(F32) * live
        row = lax.broadcasted_iota(jnp.int32, (tm, 1), 0)
        dy1 = jnp.where(row == tm - 1, ndy[0:1, :], pltpu.roll(dy, tm - 1, 0))
        dy2 = jnp.where(row == tm - 1, ndy[1:2, :], jnp.where(row == tm - 2, ndy[0:1, :], pltpu.roll(dy, tm - 2, 0)))
        dz = w2 * dy + w1 * dy1 + w0 * dy2
        dc_ref[:, 0:w] = (dya_v * y).astype(BF16)
        dc_ref[:, w:2 * w] = (dz * ah).astype(BF16)
        dc_ref[:, 2 * w:] = (dz * ac).astype(BF16)
        dcw_ref[0:1, :] += jnp.sum(dy * z2, axis=0, keepdims=True)
        dcw_ref[1:2, :] += jnp.sum(dy * z1, axis=0, keepdims=True)
        dcw_ref[2:3, :] += jnp.sum(dy * z, axis=0, keepdims=True)

    return _rowcall(name, body, n_tok, tm,
                    [cin, (cin, _prev_spec(tm, 3 * w)), (cin, _next_spec(tm, 3 * w, n_blocks)),
                     dya, (dya, _next_spec(tm, w, n_blocks))],
                    [cw], [], [(3 * w, BF16)], [((8, w), F32)])


def _sg_common(sg_ref, gn_ref):
    s_v = sg_ref[...].astype(F32)
    u, v = s_v[:, 0:SG_W], s_v[:, SG_W:]
    ug, tu = _gelu(u)
    vg, tv = _gelu(v)
    vh, r = _rms_stats(vg)
    vn = (vh * gn_ref[...]).astype(BF16)
    return u, v, ug, tu, tv, vh, r, vn


def _sg_fwd(name, sgin, gn, sgw, bias_full):
    n_tok, tm = sgin.shape[0], min(ROW_BLOCK, sgin.shape[0])

    def body(i, n, sg_ref, gn_ref, w_ref, bias_ref, yb_ref):
        _, _, ug, _, _, _, _, vn = _sg_common(sg_ref, gn_ref)
        tril = lax.broadcasted_iota(jnp.int32, (CHUNK, CHUNK), 0) >= lax.broadcasted_iota(jnp.int32, (CHUNK, CHUNK), 1)
        wt = [jnp.where(tril, w_ref[g], 0.0).astype(BF16) for g in range(SG_G)]
        for c0 in range(0, tm, CHUNK):
            sv = jnp.concatenate(
                [_dot(wt[g], vn[c0:c0 + CHUNK, g * CHUNK:(g + 1) * CHUNK]) for g in range(SG_G)], axis=1)
            sv = sv + bias_ref[...]
            yb_ref[c0:c0 + CHUNK, :] = (ug[c0:c0 + CHUNK, :] * sv).astype(BF16)

    return _rowcall(name, body, n_tok, tm, [sgin], [gn, sgw, bias_full], [], [(SG_W, BF16)], [])[0]


def _sg_bwd(name, sgin, dyb, gn, sgw, sgw_t, bias_full):
    n_tok, tm = sgin.shape[0], min(ROW_BLOCK, sgin.shape[0])

    def body(i, n, sg_ref, dyb_ref, gn_ref, w_ref, wt_ref, bias_ref, dsg_ref, dgn_ref, dw_ref, db_ref):
        u, v, ug, tu, tv, vh, r, vn = _sg_common(sg_ref, gn_ref)
        r0 = lax.broadcasted_iota(jnp.int32, (CHUNK, CHUNK), 0)
        r1 = lax.broadcasted_iota(jnp.int32, (CHUNK, CHUNK), 1)
        wt = [jnp.where(r0 >= r1, w_ref[g], 0.0).astype(BF16) for g in range(SG_G)]
        wtt = [jnp.where(r0 <= r1, wt_ref[g], 0.0).astype(BF16) for g in range(SG_G)]
        dyb_v = dyb_ref[...].astype(F32)
        dvn_rows = []
        for c0 in range(0, tm, CHUNK):
            rows = slice(c0, c0 + CHUNK)
            svs, dvns = [], []
            dsv = dyb_v[rows, :] * ug[rows, :]
            for g in range(SG_G):
                cols = slice(g * CHUNK, (g + 1) * CHUNK)
                svs.append(_dot(wt[g], vn[rows, cols]))
                dsv_g = dsv[:, cols]
                dsv_b = dsv_g.astype(BF16)
                dvns.append(_dot(wtt[g], dsv_b))
                dw_ref[g] += jnp.where(r0 >= r1, _dot_nt(dsv_b, vn[rows, cols]), 0.0)
                db_ref[g] += jnp.broadcast_to(jnp.sum(dsv_g, axis=1, keepdims=True), (CHUNK, CHUNK))
            sv = jnp.concatenate(svs, axis=1) + bias_ref[...]
            dug = dyb_v[rows, :] * sv
            dsg_ref[rows, 0:SG_W] = (dug * _gelu_grad(u[rows, :], tu[rows, :])).astype(BF16)
            dvn_rows.append(jnp.concatenate(dvns, axis=1))
        dvn = jnp.concatenate(dvn_rows, axis=0)
        dvg, dgn = _rms_bwd(vh, r, gn_ref[...], dvn)
        dsg_ref[:, SG_W:] = (dvg * _gelu_grad(v, tv)).astype(BF16)
        dgn_ref[...] += dgn

    return _rowcall(name, body, n_tok, tm, [sgin, dyb], [gn, sgw, sgw_t, bias_full], [], [(2 * SG_W, BF16)],
                    [((1, SG_W), F32), ((SG_G, CHUNK, CHUNK), F32), ((SG_G, CHUNK, CHUNK), F32)])


def _forget_cumsum(name, fl, bf):
    n_tok, tm = fl.shape[0], min(ROW_BLOCK, fl.shape[0])

    def body(i, n, fl_ref, b_ref, c_ref, carry):
        @pl.when(i == 0)
        def _():
            carry[...] = jnp.zeros(carry.shape, F32)

        z = fl_ref[...] + b_ref[...]
        lf = jnp.minimum(z, 0.0) - jnp.log1p(jnp.exp(-jnp.abs(z)))
        tri = (lax.broadcasted_iota(jnp.int32, (tm, tm), 0) >= lax.broadcasted_iota(jnp.int32, (tm, tm), 1)).astype(BF16)
        c = _split3_dot(tri, lf) + carry[...]
        c_ref[...] = c * LOG2E
        carry[...] = c[tm - 1:tm, :]

    return _rowcall(name, body, n_tok, tm, [fl], [bf], [], [(LANES, F32)], [], scratch=[pltpu.VMEM((1, LANES), F32)])[0]


def _forget_bwd(name, dc, fl, bf):
    n_tok, tm = fl.shape[0], min(ROW_BLOCK, fl.shape[0])
    n = n_tok // tm
    rev = pl.BlockSpec((tm, LANES), lambda i: (n - 1 - i, 0))

    def kern(dc_ref, fl_ref, b_ref, dfl_ref, db_ref, carry):
        @pl.when(pl.program_id(0) == 0)
        def _():
            carry[...] = jnp.zeros(carry.shape, F32)
            db_ref[...] = jnp.zeros(db_ref.shape, F32)

        triu = (lax.broadcasted_iota(jnp.int32, (tm, tm), 0) <= lax.broadcasted_iota(jnp.int32, (tm, tm), 1)).astype(BF16)
        dlf = _split3_dot(triu, dc_ref[...]) + carry[...]
        carry[...] = dlf[0:1, :]
        z = fl_ref[...] + b_ref[...]
        dfl = dlf * jax.nn.sigmoid(-z)
        dfl_ref[...] = dfl
        db_ref[...] += jnp.sum(dfl, axis=0, keepdims=True)

    return pl.pallas_call(
        kern, name=name, grid=(n,), in_specs=[rev, rev, _full_spec((1, LANES))],
        out_specs=[rev, _full_spec((1, LANES))],
        out_shape=[jax.ShapeDtypeStruct((n_tok, LANES), F32), jax.ShapeDtypeStruct((1, LANES), F32)],
        scratch_shapes=[pltpu.VMEM((1, LANES), F32)], compiler_params=_params(1),
    )(dc, fl, bf)


def _lane_pick(c_blk, h):
    lane = lax.broadcasted_iota(jnp.int32, c_blk.shape, 1)
    return jnp.broadcast_to(jnp.sum(jnp.where(lane == h, c_blk, 0.0), axis=1, keepdims=True), c_blk.shape)


def _wide(stat, width):
    return jnp.tile(stat, (1, width // LANES))


def _pair_half(rows):
    return lax.broadcasted_iota(jnp.int32, (rows, LANES), 1) // FOX_D


def _col_to_row(col):
    pick = (lax.broadcasted_iota(jnp.int32, (8, LANES), 1) == 0).astype(BF16)
    hi = col.astype(BF16)
    r1 = col - hi.astype(F32)
    mid = r1.astype(BF16)
    lo = (r1 - mid.astype(F32)).astype(BF16)
    return (_dot_nt(pick, hi) + _dot_nt(pick, mid) + _dot_nt(pick, lo))[0:1, :]


def _tri_table(nb, lower):
    rows = [(a, b) for a in range(nb) for b in (range(a + 1) if lower else range(a, nb))]
    return tuple(jnp.asarray([r[k] for r in rows], jnp.int32) for k in range(2))


def _flash_bounds(name, qkv):
    n_tok = qkv.shape[0]
    blk = min(FLASH_BLOCK, n_tok)

    def kern(x_ref, o_ref):
        head = (lax.broadcasted_iota(jnp.int32, (FOX_W, LANES), 0) // FOX_D
                == lax.broadcasted_iota(jnp.int32, (FOX_W, LANES), 1)).astype(BF16)
        q, k = x_ref[:, 0:FOX_W].astype(F32), x_ref[:, FOX_W:2 * FOX_W].astype(F32)
        qn = jnp.sqrt(jnp.max(_dot((q * q).astype(BF16), head) * 1.01, axis=0, keepdims=True))
        kn = jnp.sqrt(jnp.max(_dot((k * k).astype(BF16), head) * 1.01, axis=0, keepdims=True))
        diag = jnp.min(_dot((q * k).astype(BF16), head), axis=0, keepdims=True) - qn * kn * (2.0 ** -8)
        o_ref[...] = jnp.concatenate([qn, kn, diag, jnp.zeros((5, LANES), F32)], axis=0)

    return pl.pallas_call(
        kern, name=name, grid=(n_tok // blk,),
        in_specs=[pl.BlockSpec((blk, 2 * FOX_W), lambda b: (b, 0))], out_specs=pl.BlockSpec((8, LANES), lambda b: (b, 0)),
        out_shape=jax.ShapeDtypeStruct((n_tok // blk * 8, LANES), F32), compiler_params=_params(1),
    )(qkv)


def _prune_tables(bounds, c2, blk):
    nb = c2.shape[0] // blk
    tab = bounds.reshape(nb, 8, LANES)
    c_blocks = c2.reshape(nb, blk, LANES)
    return tuple(t[:, :FOX_H].reshape(-1)
                 for t in (tab[:, 0], tab[:, 1], tab[:, 2], c_blocks[:, 0], c_blocks[:, blk - 1]))


def _block_is_live(tables, p, i, j):
    qn, kn, diag, c_first, c_last = tables
    a = FOX_SCALE * LOG2E
    live = []
    for e in range(2):
        h = 2 * p + e
        qi, kj = FOX_H * i + h, FOX_H * j + h
        u = a * qn[qi] * kn[kj] + c_first[qi] - c_last[kj]
        lo = a * diag[qi]
        live.append((u - lo) > -PRUNE_LOG2)
    return live


def _flash_fwd(name, qkv, c2, ct2, tables):
    n_tok = qkv.shape[0]
    blk = min(FLASH_BLOCK, n_tok)
    nb = n_tok // blk
    n_pair = FOX_H // 2
    it, jt = _tri_table(nb, True)

    strip = min(FLASH_STRIP, blk)

    def kern(it_ref, jt_ref, qn_ref, kn_ref, dg_ref, cf_ref, cl_ref, q_ref, k_ref, v_ref, c_ref, ct_ref, o_ref, lse_ref,
             m_s, l_s, acc_s, cq_s, qm_s, al_s, s_s, pb_s):
        n = pl.program_id(1)
        i, j = it_ref[n], jt_ref[n]
        half = _pair_half(blk)
        live = _block_is_live((qn_ref, kn_ref, dg_ref, cf_ref, cl_ref), pl.program_id(0), i, j)

        @pl.when(j == 0)
        def _():
            m_s[...] = jnp.full(m_s.shape, -jnp.inf, F32)
            l_s[...] = jnp.zeros(l_s.shape, F32)
            acc_s[...] = jnp.zeros(acc_s.shape, F32)
            for e in range(2):
                cq_s[e] = _lane_pick(c_ref[...], 2 * pl.program_id(0) + e)
                qm_s[e] = jnp.where(half == e, q_ref[...], jnp.zeros_like(q_ref[...]))

        def step(heads, on_diagonal):
            k_v, v_v = k_ref[...], v_ref[...]
            for e in heads:
                s_s[e] = _dot_nt(qm_s[e], k_v)
            for e in heads:
                ck = ct_ref[e]
                for r0 in range(0, blk, strip):
                    rs = slice(r0, r0 + strip)
                    t = s_s[e, rs, :] * (FOX_SCALE * LOG2E) - ck
                    if on_diagonal:
                        keep = (lax.broadcasted_iota(jnp.int32, (strip, blk), 1)
                                <= lax.broadcasted_iota(jnp.int32, (strip, blk), 0) + r0)
                        t = jnp.where(keep, t, -jnp.inf)
                    m_old, cq = m_s[e, rs, :], cq_s[e, rs, :]
                    m_new = jnp.maximum(m_old, jnp.max(t, axis=1, keepdims=True) + cq)
                    p = jnp.exp2(t - _wide(m_new - cq, blk))
                    alpha = jnp.exp2(m_old - m_new)
                    l_s[e, rs, :] = alpha * l_s[e, rs, :] + jnp.sum(p, axis=1, keepdims=True)
                    m_s[e, rs, :] = m_new
                    al_s[e, rs, :] = alpha
                    pb_s[e, rs, :] = p.astype(BF16)
            if len(heads) == 2:
                pv = jnp.where(half == 0, _dot(pb_s[0], v_v), _dot(pb_s[1], v_v))
                acc_s[...] = jnp.where(half == 0, al_s[0], al_s[1]) * acc_s[...] + pv
            else:
                e = heads[0]
                acc_s[...] = jnp.where(half == e, al_s[e] * acc_s[...] + _dot(pb_s[e], v_v), acc_s[...])

        below = j < i

        @pl.when(below & live[0] & live[1])
        def _():
            step((0, 1), False)

        @pl.when(below & live[0] & jnp.logical_not(live[1]))
        def _():
            step((0,), False)

        @pl.when(below & jnp.logical_not(live[0]) & live[1])
        def _():
            step((1,), False)

        @pl.when(j == i)
        def _():
            step((0, 1), True)
            o_ref[...] = (acc_s[...] / jnp.where(half == 0, l_s[0], l_s[1])).astype(BF16)
            rows = pl.ds(pl.multiple_of(i * blk, blk), blk)
            lse_ref[0, :, rows] = jnp.concatenate(
                [_col_to_row(m_s[e] + jnp.log(l_s[e]) * LOG2E) for e in range(2)], axis=0)

    def key_block(p, n, it_r, jt_r, tabs):
        i, j = it_r[n], jt_r[n]
        live = _block_is_live(tabs, p, i, j)
        return jnp.where(live[0] | live[1], j, i)

    def col_block(first, inner):
        def index(p, n, it_r, jt_r, *tabs):
            return ((key_block(p, n, it_r, jt_r, tabs) if inner else it_r[n]), first + p)
        return pl.BlockSpec((blk, LANES), index)

    pair_stat = (2, blk, LANES)
    grid_spec = pltpu.PrefetchScalarGridSpec(
        num_scalar_prefetch=2 + len(tables), grid=(n_pair, int(it.shape[0])),
        in_specs=[col_block(0, False), col_block(n_pair, True), col_block(2 * n_pair, True),
                  pl.BlockSpec((blk, LANES), lambda p, n, it_r, jt_r, *_: (it_r[n], 0)),
                  pl.BlockSpec((2, 1, blk), lambda p, n, it_r, jt_r, *tabs: (p, 0, key_block(p, n, it_r, jt_r, tabs)))],
        out_specs=[col_block(0, False), pl.BlockSpec((1, 2, n_tok), lambda p, n, it_r, jt_r, *_: (p, 0, 0))],
        scratch_shapes=[pltpu.VMEM(pair_stat, F32), pltpu.VMEM(pair_stat, F32), pltpu.VMEM((blk, LANES), F32),
                        pltpu.VMEM(pair_stat, F32), pltpu.VMEM(pair_stat, BF16), pltpu.VMEM(pair_stat, F32),
                        pltpu.VMEM((2, blk, blk), F32), pltpu.VMEM((2, blk, blk), BF16)],
    )
    return pl.pallas_call(
        kern, name=name, grid_spec=grid_spec,
        out_shape=[jax.ShapeDtypeStruct((n_tok, FOX_W), BF16), jax.ShapeDtypeStruct((n_pair, 2, n_tok), F32)],
        compiler_params=_params(2),
    )(it, jt, *tables, qkv, qkv, qkv, c2, ct2)


def _flash_delta(name, do, o):
    n_tok, tm = do.shape[0], min(ROW_BLOCK, do.shape[0])

    def body(i, n, do_ref, o_ref, d_ref):
        prod = do_ref[...].astype(F32) * o_ref[...].astype(F32)
        head = (lax.broadcasted_iota(jnp.int32, (FOX_W, LANES), 0) // FOX_D
                == lax.broadcasted_iota(jnp.int32, (FOX_W, LANES), 1)).astype(BF16)
        hi = prod.astype(BF16)
        r1 = prod - hi.astype(F32)
        mid = r1.astype(BF16)
        lo = (r1 - mid.astype(F32)).astype(BF16)
        d_ref[...] = _dot(hi, head) + _dot(mid, head) + _dot(lo, head)

    return _rowcall(name, body, n_tok, tm, [do, o], [], [], [(LANES, F32)], [])[0]


def _flash_bwd(name, qkv, do, r_row, d_row, c2, tables):
    n_tok = qkv.shape[0]
    blk = min(FLASH_BLOCK, n_tok)
    nb = n_tok // blk
    n_pair = FOX_H // 2
    jt, it = _tri_table(nb, False)

    def kern(jt_ref, it_ref, qn_ref, kn_ref, dg_ref, cf_ref, cl_ref, k_ref, v_ref, q_ref, do_ref, rr_ref, dr_ref, c_ref,
             dq_ref, dk_ref, dv_ref, dc_ref, km_s, vm_s, ck_s, dck_s):
        n = pl.program_id(1)
        j, i = jt_ref[n], it_ref[n]
        first = _pair_half(blk) == 0
        live = _block_is_live((qn_ref, kn_ref, dg_ref, cf_ref, cl_ref), pl.program_id(0), i, j)

        @pl.when(n == 0)
        def _():
            dq_ref[...] = jnp.zeros(dq_ref.shape, F32)
            dc_ref[...] = jnp.zeros(dc_ref.shape, F32)

        @pl.when(i == j)
        def _():
            dk_ref[...] = jnp.zeros(dk_ref.shape, F32)
            dv_ref[...] = jnp.zeros(dv_ref.shape, F32)
            dck_s[...] = jnp.zeros(dck_s.shape, F32)
            for e in range(2):
                mine = first if e == 0 else jnp.logical_not(first)
                km_s[e] = jnp.where(mine, k_ref[...], jnp.zeros_like(k_ref[...]))
                vm_s[e] = jnp.where(mine, v_ref[...], jnp.zeros_like(v_ref[...]))
                ck_s[e] = _lane_pick(c_ref[...], 2 * pl.program_id(0) + e)

        def step(heads, on_diagonal):
            q_v, do_v, k_v = q_ref[...], do_ref[...], k_ref[...]
            dvs, dks, dqs, sums = {}, {}, {}, {}
            for e in heads:
                t = _dot_nt(km_s[e], q_v) * (FOX_SCALE * LOG2E) - _wide(ck_s[e], blk) - rr_ref[e]
                if on_diagonal:
                    keep = (lax.broadcasted_iota(jnp.int32, (blk, blk), 1)
                            >= lax.broadcasted_iota(jnp.int32, (blk, blk), 0))
                    t = jnp.where(keep, t, -jnp.inf)
                p = jnp.exp2(t)
                ds = p * (_dot_nt(vm_s[e], do_v) - dr_ref[e])
                p_b, ds_b = p.astype(BF16), ds.astype(BF16)
                dvs[e] = _dot(p_b, do_v)
                dks[e] = _dot(ds_b, q_v)
                dqs[e] = _dot_tn(ds_b, k_v)
                dck_s[e] += jnp.broadcast_to(jnp.sum(ds, axis=1, keepdims=True), (blk, LANES))
                sums[e] = jnp.sum(ds, axis=0, keepdims=True)
            rows = pl.ds(pl.multiple_of(i * blk, blk), blk)

            def merged(parts):
                zero = jnp.zeros((blk, LANES), F32)
                return jnp.where(first, parts.get(0, zero), parts.get(1, zero))

            dv_ref[...] += merged(dvs)
            dk_ref[...] += merged(dks) * FOX_SCALE
            dq_ref[rows, :] += merged(dqs) * FOX_SCALE
            none = jnp.zeros((1, blk), F32)
            dc_ref[0, :, rows] += jnp.concatenate([sums.get(0, none), sums.get(1, none)], axis=0)

        above = i > j

        @pl.when(above & live[0] & live[1])
        def _():
            step((0, 1), False)

        @pl.when(above & live[0] & jnp.logical_not(live[1]))
        def _():
            step((0,), False)

        @pl.when(above & jnp.logical_not(live[0]) & live[1])
        def _():
            step((1,), False)

        @pl.when(i == j)
        def _():
            step((0, 1), True)

        @pl.when(i == nb - 1)
        def _():
            keys = pl.ds(pl.multiple_of(j * blk, blk), blk)
            dc_ref[0, :, keys] -= jnp.concatenate([_col_to_row(dck_s[e]) for e in range(2)], axis=0)

    def query_block(p, n, jt_r, it_r, tabs):
        j, i = jt_r[n], it_r[n]
        live = _block_is_live(tabs, p, i, j)
        return jnp.where(live[0] | live[1] | (i == j), i, nb - 1)

    def col_block(first_col, inner):
        def index(p, n, jt_r, it_r, *tabs):
            return ((query_block(p, n, jt_r, it_r, tabs) if inner else jt_r[n]), first_col + p)
        return pl.BlockSpec((blk, LANES), index)

    def row_stat():
        return pl.BlockSpec((2, 1, blk), lambda p, n, jt_r, it_r, *tabs: (p, 0, query_block(p, n, jt_r, it_r, tabs)))

    pair_stat = (2, blk, LANES)
    grid_spec = pltpu.PrefetchScalarGridSpec(
        num_scalar_prefetch=2 + len(tables), grid=(n_pair, int(jt.shape[0])),
        in_specs=[col_block(n_pair, False), col_block(2 * n_pair, False), col_block(0, True), col_block(0, True),
                  row_stat(), row_stat(), pl.BlockSpec((blk, LANES), lambda p, n, jt_r, it_r, *_: (jt_r[n], 0))],
        out_specs=[pl.BlockSpec((n_tok, LANES), lambda p, n, jt_r, it_r, *_: (0, p)),
                   col_block(0, False), col_block(0, False),
                   pl.BlockSpec((1, 2, n_tok), lambda p, n, jt_r, it_r, *_: (p, 0, 0))],
        scratch_shapes=[pltpu.VMEM(pair_stat, BF16), pltpu.VMEM(pair_stat, BF16), pltpu.VMEM(pair_stat, F32),
                        pltpu.VMEM(pair_stat, F32)],
    )
    wide = jax.ShapeDtypeStruct((n_tok, FOX_W), F32)
    return pl.pallas_call(
        kern, name=name, grid_spec=grid_spec,
        out_shape=[wide, wide, wide, jax.ShapeDtypeStruct((n_pair, 2, n_tok), F32)],
        compiler_params=_params(2),
    )(jt, it, *tables, qkv, qkv, qkv, do, r_row, d_row, c2)


def _merge_fwd(name, x, ya, yb, yc, gl, wb, wo):
    n_tok, tm = x.shape[0], min(ROW_BLOCK, x.shape[0])

    def body(i, n, x_ref, ya_ref, yb_ref, yc_ref, gl_ref, wb_ref, wo_ref, xo_ref):
        merged = jnp.zeros((tm, D_MODEL), F32)
        for k, y_ref in enumerate((ya_ref, yb_ref, yc_ref)):
            gate = jax.nn.sigmoid(gl_ref[:, k * D_MODEL:(k + 1) * D_MODEL].astype(F32))
            merged = merged + gate * _dot(y_ref[...], wb_ref[k])
        xo_ref[...] = x_ref[...] + _dot(merged.astype(BF16), wo_ref[...])

    return _rowcall(name, body, n_tok, tm, [x, ya, yb, yc, gl], [], [wb, wo], [(D_MODEL, F32)], [])[0]


def _merge_bwd(name, dxo, ya, yb, yc, gl, wb, wb_t, wo_t):
    n_tok, tm = dxo.shape[0], min(ROW_BLOCK, dxo.shape[0])

    def body(i, n, dxo_ref, ya_ref, yb_ref, yc_ref, gl_ref, wb_ref, wbt_ref, wot_ref,
             mg_ref, dob_ref, dgl_ref, dbr_ref, dya_ref, dyb_ref, dyc_ref):
        dob = dxo_ref[...].astype(BF16)
        dob_ref[...] = dob
        dm = _dot(dob, wot_ref[...])
        merged = jnp.zeros((tm, D_MODEL), F32)
        for k, (y_ref, dy_ref) in enumerate(((ya_ref, dya_ref), (yb_ref, dyb_ref), (yc_ref, dyc_ref))):
            cols = slice(k * D_MODEL, (k + 1) * D_MODEL)
            gate = jax.nn.sigmoid(gl_ref[:, cols].astype(F32))
            br = _dot(y_ref[...], wb_ref[k])
            merged = merged + gate * br
            dgl_ref[:, cols] = (dm * br * gate * (1.0 - gate)).astype(BF16)
            dbr = (dm * gate).astype(BF16)
            dbr_ref[:, cols] = dbr
            dy_ref[...] = _dot(dbr, wbt_ref[k]).astype(BF16)
        mg_ref[...] = merged.astype(BF16)

    return _rowcall(name, body, n_tok, tm, [dxo, ya, yb, yc, gl], [], [wb, wb_t, wo_t],
                    [(D_MODEL, BF16), (D_MODEL, BF16), (3 * D_MODEL, BF16), (3 * D_MODEL, BF16),
                     (CONV_W, BF16), (SG_W, BF16), (FOX_W, BF16)], [])


def _mem_fwd(name, mem, gain, wk, wv):
    n_mem = mem.shape[0]

    def kern(m_ref, g_ref, wk_ref, wv_ref, mn_ref, k_ref, v_ref):
        mh, _ = _rms_stats(m_ref[...])
        mn = (mh * g_ref[...]).astype(BF16)
        mn_ref[...] = mn
        k_ref[...] = _dot(mn, wk_ref[...]).astype(BF16)
        v_ref[...] = _dot(mn, wv_ref[...]).astype(BF16)

    shp = jax.ShapeDtypeStruct((n_mem, D_MODEL), BF16)
    return pl.pallas_call(
        kern, name=name, grid=(1,),
        in_specs=[_full_spec(mem.shape), _full_spec(gain.shape), _full_spec(wk.shape), _full_spec(wv.shape)],
        out_specs=[_full_spec(shp.shape)] * 3, out_shape=[shp] * 3, compiler_params=_params(1),
    )(mem, gain, wk, wv)


def _mem_bwd(name, mem, gain, dkx, dvx, wk_t, wv_t):
    n_mem = mem.shape[0]

    def kern(m_ref, g_ref, dk_ref, dv_ref, wkt_ref, wvt_ref, dkb_ref, dvb_ref, dg_ref):
        mh, _ = _rms_stats(m_ref[...])
        dkb, dvb = dk_ref[...].astype(BF16), dv_ref[...].astype(BF16)
        dkb_ref[...] = dkb
        dvb_ref[...] = dvb
        dm = _dot(dkb, wkt_ref[...]) + _dot(dvb, wvt_ref[...])
        dg_ref[...] = jnp.sum(dm * mh, axis=0, keepdims=True)

    shp = jax.ShapeDtypeStruct((n_mem, D_MODEL), BF16)
    args = (mem, gain, dkx, dvx, wk_t, wv_t)
    return pl.pallas_call(
        kern, name=name, grid=(1,), in_specs=[_full_spec(a.shape) for a in args],
        out_specs=[_full_spec(shp.shape)] * 2 + [_full_spec((1, D_MODEL))],
        out_shape=[shp, shp, jax.ShapeDtypeStruct((1, D_MODEL), F32)], compiler_params=_params(1),
    )(*args)


def _xa_probs(q_b, kx_ref, hd):
    cols = slice(hd * XA_D, (hd + 1) * XA_D)
    s = _dot_nt(q_b[:, cols], kx_ref[:, cols]) * (XA_D ** -0.5)
    e = jnp.exp(s - jnp.max(s, axis=1, keepdims=True))
    return e / jnp.sum(e, axis=1, keepdims=True)


def _xa_fwd(name, x, gain, kx, vx, wq, wo):
    n_tok, tm = x.shape[0], min(ROW_BLOCK, x.shape[0])

    def body(i, n, x_ref, g_ref, kx_ref, vx_ref, wq_ref, wo_ref, xo_ref):
        x_v = x_ref[...]
        xh, _ = _rms_stats(x_v)
        q_b = _dot((xh * g_ref[...]).astype(BF16), wq_ref[...]).astype(BF16)
        o = jnp.concatenate(
            [_dot(_xa_probs(q_b, kx_ref, hd).astype(BF16), vx_ref[:, hd * XA_D:(hd + 1) * XA_D]) for hd in range(XA_H)],
            axis=1)
        xo_ref[...] = x_v + _dot(o.astype(BF16), wo_ref[...])

    return _rowcall(name, body, n_tok, tm, [x], [gain, kx, vx], [wq, wo], [(D_MODEL, F32)], [])[0]


def _xa_bwd(name, x, dxo, gain, kx, vx, wq, wq_t, wo_t):
    n_tok, tm = x.shape[0], min(ROW_BLOCK, x.shape[0])
    n_mem = kx.shape[0]

    def body(i, n, x_ref, dxo_ref, g_ref, kx_ref, vx_ref, wq_ref, wqt_ref, wot_ref,
             dx_ref, h_ref, o_ref, dq_ref, dy_ref, dkx_ref, dvx_ref, dg_ref):
        g = g_ref[...]
        xh, r = _rms_stats(x_ref[...])
        h = (xh * g).astype(BF16)
        h_ref[...] = h
        q_b = _dot(h, wq_ref[...]).astype(BF16)
        dxo_v = dxo_ref[...]
        dy = dxo_v.astype(BF16)
        dy_ref[...] = dy
        do = _dot(dy, wot_ref[...])
        for hd in range(XA_H):
            cols = slice(hd * XA_D, (hd + 1) * XA_D)
            p = _xa_probs(q_b, kx_ref, hd)
            p_b = p.astype(BF16)
            o_ref[:, cols] = _dot(p_b, vx_ref[:, cols]).astype(BF16)
            do_h = do[:, cols].astype(BF16)
            dvx_ref[:, cols] += _dot_tn(p_b, do_h)
            dp = _dot_nt(do_h, vx_ref[:, cols])
            ds = p * (dp - jnp.sum(dp * p, axis=1, keepdims=True))
            ds_b = (ds * (XA_D ** -0.5)).astype(BF16)
            dq_ref[:, cols] = _dot(ds_b, kx_ref[:, cols]).astype(BF16)
            dkx_ref[:, cols] += _dot_tn(ds_b, q_b[:, cols])
        dh = _dot(dq_ref[...], wqt_ref[...])
        dx, dg = _rms_bwd(xh, r, g, dh)
        dx_ref[...] = dxo_v + dx
        dg_ref[...] += dg

    return _rowcall(name, body, n_tok, tm, [x, dxo], [gain, kx, vx], [wq, wq_t, wo_t],
                    [(D_MODEL, F32), (D_MODEL, BF16), (D_MODEL, BF16), (D_MODEL, BF16), (D_MODEL, BF16)],
                    [((n_mem, D_MODEL), F32), ((n_mem, D_MODEL), F32), ((1, D_MODEL), F32)])


def _loss_head(name, x, target, gain):
    n_tok, tm = x.shape[0], min(ROW_BLOCK, x.shape[0])

    def body(i, n, x_ref, t_ref, g_ref, dx_ref, loss_ref, dg_ref):
        g = g_ref[...]
        xh, r = _rms_stats(x_ref[...])
        err = xh * g - t_ref[...]
        loss_ref[...] += 0.5 * jnp.sum(jnp.sum(err * err, axis=1, keepdims=True) / D_MODEL, axis=0, keepdims=True)
        dx, dg = _rms_bwd(xh, r, g, err / D_MODEL)
        dx_ref[...] = dx
        dg_ref[...] += dg

    return _rowcall(name, body, n_tok, tm, [x, target], [gain], [], [(D_MODEL, F32)],
                    [((8, LANES), F32), ((1, D_MODEL), F32)])


def _adamw(name, w, g, m, v):
    rows, cols = w.shape
    tr = 128 if rows % 128 == 0 else rows

    def kern(w_ref, g_ref, m_ref, v_ref, d_ref, mo_ref, vo_ref):
        g_v = g_ref[...]
        m_new = ADAM_B1 * m_ref[...] + (1.0 - ADAM_B1) * g_v
        v_new = ADAM_B2 * v_ref[...] + (1.0 - ADAM_B2) * (g_v * g_v)
        m_hat = m_new / (1.0 - ADAM_B1 ** ADAM_STEP)
        v_hat = v_new / (1.0 - ADAM_B2 ** ADAM_STEP)
        d_ref[...] = -ADAM_LR * (m_hat / (jnp.sqrt(v_hat) + ADAM_EPS) + ADAM_WD * w_ref[...])
        mo_ref[...] = m_new
        vo_ref[...] = v_new

    spec = pl.BlockSpec((tr, cols), lambda i: (i, 0))
    shp = jax.ShapeDtypeStruct((rows, cols), F32)
    return pl.pallas_call(kern, name=name, grid=(rows // tr,), in_specs=[spec] * 4, out_specs=[spec] * 3,
                          out_shape=[shp] * 3, compiler_params=_params(1))(w, g, m, v)


def _add_pair(name, gp, got, core):
    _, n, rows, cols = gp.shape
    spec = pl.BlockSpec((1, PACK_ROW_BLOCK, cols), lambda j, r, core_ref: (j, r, 0))
    mine = pl.BlockSpec((1, PACK_ROW_BLOCK, cols), lambda j, r, core_ref: (core_ref[0] * n + j, r, 0))

    def kern(core_ref, a_ref, b_ref, o_ref):
        o_ref[...] = (a_ref[...].astype(F32) + b_ref[...].astype(F32)).astype(BF16)

    grid_spec = pltpu.PrefetchScalarGridSpec(num_scalar_prefetch=1, grid=(n, rows // PACK_ROW_BLOCK),
                                             in_specs=[mine, spec], out_specs=spec)
    return pl.pallas_call(kern, name=name, grid_spec=grid_spec, out_shape=jax.ShapeDtypeStruct(got.shape, BF16),
                          compiler_params=_params(2))(core, gp.reshape(2 * n, rows, cols), got)


def _sum_parts(name, s, got, chip):
    _, rows, cols = s.shape
    tr = PACK_ROW_BLOCK

    def kern(chip_ref, s_ref, g_ref, o_ref):
        acc = s_ref[0].astype(F32)
        for k in range(3):
            acc = acc + g_ref[k].astype(F32)
        o_ref[...] = acc

    grid_spec = pltpu.PrefetchScalarGridSpec(
        num_scalar_prefetch=1, grid=(rows // tr,),
        in_specs=[pl.BlockSpec((1, tr, cols), lambda r, chip_ref: (chip_ref[0], r, 0)),
                  pl.BlockSpec((3, tr, cols), lambda r, chip_ref: (0, r, 0))],
        out_specs=pl.BlockSpec((tr, cols), lambda r, chip_ref: (r, 0)))
    return pl.pallas_call(kern, name=name, grid_spec=grid_spec, out_shape=jax.ShapeDtypeStruct((rows, cols), F32),
                          compiler_params=_params(1))(chip, s, got)


def _sum_slots(name, a, tr):
    n, rows, cols = a.shape

    def kern(a_ref, o_ref):
        acc = a_ref[0].astype(F32)
        for k in range(1, n):
            acc = acc + a_ref[k].astype(F32)
        o_ref[...] = acc

    return pl.pallas_call(kern, name=name, grid=(rows // tr,),
                          in_specs=[pl.BlockSpec((n, tr, cols), lambda r: (0, r, 0))],
                          out_specs=pl.BlockSpec((tr, cols), lambda r: (r, 0)),
                          out_shape=jax.ShapeDtypeStruct((rows, cols), F32), compiler_params=_params(1))(a)


_ANY = pl.BlockSpec(memory_space=pl.ANY)


COMM_CHUNKS = 13


def _place():
    x, y, c = lax.axis_index("x"), lax.axis_index("y"), lax.axis_index("c")
    chips = [(1 - x, y), (x, 1 - y), (1 - x, 1 - y)]
    return x, y, c, chips


def _rows(ref, q, n):
    step = ref.shape[0] // n
    return ref.at[pl.ds(q * step, step)]


def _gather_chips(name, w):
    n = COMM_CHUNKS

    def kern(w_ref, o_ref, send_sems, recv_sems):
        x, y, c, chips = _place()
        me, sib = 2 * x + y, (x, y, 1 - c)

        def copy(k, src, dst, to):
            return pltpu.make_async_remote_copy(src_ref=src, dst_ref=dst, send_sem=send_sems.at[k],
                                                recv_sem=recv_sems.at[k], device_id=to, device_id_type=MESH_ID)

        first = [copy(k * n + q, _rows(w_ref.at[c], q, n), _rows(o_ref.at[me, c], q, n), (cx, cy, c))
                 for k, (cx, cy) in enumerate(chips) for q in range(n)]
        for cp in first:
            cp.start()
        passed = []
        for k, (cx, cy) in enumerate(chips):
            for q in range(n):
                landed = _rows(o_ref.at[2 * cx + cy, c], q, n)
                copy(k * n + q, landed, landed, (x, y, c)).wait_recv()
                passed.append(copy((3 + k) * n + q, landed, landed, sib))
                passed[-1].start()
        for k, (cx, cy) in enumerate(chips):
            for q in range(n):
                theirs = _rows(o_ref.at[2 * cx + cy, 1 - c], q, n)
                copy((3 + k) * n + q, theirs, theirs, (x, y, c)).wait_recv()
        for cp in first + passed:
            cp.wait_send()

    return pl.pallas_call(
        kern, name=name, in_specs=[_ANY], out_specs=_ANY,
        out_shape=jax.ShapeDtypeStruct((N_CHIPS,) + w.shape, w.dtype),
        scratch_shapes=[pltpu.SemaphoreType.DMA((6 * n,)), pltpu.SemaphoreType.DMA((6 * n,))],
        compiler_params=pltpu.CompilerParams(has_side_effects=True),
    )(w)


def _gather_all(name, v):
    def kern(v_ref, o_ref, send_sems, recv_sems, local_sem):
        x, y, c, _ = _place()
        me = 4 * x + 2 * y + c
        mine = pltpu.make_async_copy(v_ref, o_ref.at[me], local_sem)
        mine.start()
        peers = []
        for k in range(1, 8):
            px = 1 - x if k & 4 else x
            py = 1 - y if k & 2 else y
            pc = 1 - c if k & 1 else c
            peers.append((px, py, pc))
        copies = [pltpu.make_async_remote_copy(src_ref=v_ref, dst_ref=o_ref.at[me], send_sem=send_sems.at[k],
                                               recv_sem=recv_sems.at[k], device_id=p, device_id_type=MESH_ID)
                  for k, p in enumerate(peers)]
        for cp in copies:
            cp.start()
        for k, (px, py, pc) in enumerate(peers):
            pltpu.make_async_remote_copy(src_ref=v_ref, dst_ref=o_ref.at[4 * px + 2 * py + pc], send_sem=send_sems.at[k],
                                         recv_sem=recv_sems.at[k], device_id=(x, y, c), device_id_type=MESH_ID).wait_recv()
        for cp in copies:
            cp.wait_send()
        mine.wait()

    return pl.pallas_call(
        kern, name=name, in_specs=[_ANY], out_specs=_ANY, out_shape=jax.ShapeDtypeStruct((8,) + v.shape, v.dtype),
        scratch_shapes=[pltpu.SemaphoreType.DMA((7,)), pltpu.SemaphoreType.DMA((7,)), pltpu.SemaphoreType.DMA(())],
        compiler_params=pltpu.CompilerParams(has_side_effects=True),
    )(v)


def _swap_halves(name, gp):
    n = COMM_CHUNKS
    n_slots = gp.shape[1]

    def kern(g_ref, got_ref, send_sems, recv_sems):
        x, y, c, _ = _place()
        swaps = [pltpu.make_async_remote_copy(src_ref=_rows(g_ref.at[1 - c, j], q, n), dst_ref=_rows(got_ref.at[j], q, n),
                                              send_sem=send_sems.at[j * n + q], recv_sem=recv_sems.at[j * n + q],
                                              device_id=(x, y, 1 - c), device_id_type=MESH_ID)
                 for j in range(n_slots) for q in range(n)]
        for cp in swaps:
            cp.start()
        for cp in swaps:
            cp.wait()

    return pl.pallas_call(
        kern, name=name, in_specs=[_ANY], out_specs=_ANY, out_shape=jax.ShapeDtypeStruct(gp.shape[1:], gp.dtype),
        scratch_shapes=[pltpu.SemaphoreType.DMA((n_slots * n,)), pltpu.SemaphoreType.DMA((n_slots * n,))],
        compiler_params=pltpu.CompilerParams(has_side_effects=True),
    )(gp)


def _scatter_chips(name, s):
    n = COMM_CHUNKS

    def kern(s_ref, o_ref, send_sems, recv_sems):
        x, y, c, chips = _place()
        copies = [pltpu.make_async_remote_copy(src_ref=_rows(s_ref.at[2 * cx + cy], q, n), dst_ref=_rows(o_ref.at[k], q, n),
                                               send_sem=send_sems.at[k * n + q], recv_sem=recv_sems.at[k * n + q],
                                               device_id=(cx, cy, c), device_id_type=MESH_ID)
                  for k, (cx, cy) in enumerate(chips) for q in range(n)]
        for cp in copies:
            cp.start()
        for k in range(3):
            for q in range(n):
                landed = _rows(o_ref.at[k], q, n)
                pltpu.make_async_remote_copy(src_ref=landed, dst_ref=landed, send_sem=send_sems.at[k * n + q],
                                             recv_sem=recv_sems.at[k * n + q], device_id=(x, y, c),
                                             device_id_type=MESH_ID).wait_recv()
        for cp in copies:
            cp.wait_send()

    return pl.pallas_call(
        kern, name=name, in_specs=[_ANY], out_specs=_ANY, out_shape=jax.ShapeDtypeStruct((3,) + s.shape[1:], s.dtype),
        scratch_shapes=[pltpu.SemaphoreType.DMA((3 * n,)), pltpu.SemaphoreType.DMA((3 * n,))],
        compiler_params=pltpu.CompilerParams(has_side_effects=True),
    )(s)


def _join_halves(name, r):
    n = COMM_CHUNKS

    def kern(r_ref, o_ref, send_sems, recv_sems):
        x, y, c, _ = _place()
        swaps = [pltpu.make_async_remote_copy(src_ref=_rows(r_ref, q, n), dst_ref=_rows(o_ref, q, n),
                                              send_sem=send_sems.at[q], recv_sem=recv_sems.at[q],
                                              device_id=(x, y, 1 - c), device_id_type=MESH_ID) for q in range(n)]
        for cp in swaps:
            cp.start()
        for cp in swaps:
            cp.wait()

    return pl.pallas_call(
        kern, name=name, in_specs=[_ANY], out_specs=_ANY, out_shape=jax.ShapeDtypeStruct(r.shape, r.dtype),
        scratch_shapes=[pltpu.SemaphoreType.DMA((n,))] * 2, compiler_params=pltpu.CompilerParams(has_side_effects=True),
    )(r)


PACK_ALIGN = 16


def _exchange_rows(nm, shard_shape):
    if BIG_KIND[nm] == "cols":
        rows = shard_shape[-1]
    else:
        rows = 1
        for s in shard_shape[1:]:
            rows *= s
        rows //= PACK_COLS
    return -(-rows // PACK_ALIGN) * PACK_ALIGN


def _to_exchange(nm, a, transposed=False):
    if BIG_KIND[nm] == "cols":
        if not transposed:
            a = jnp.swapaxes(a, 1, 2)
        rows = -(-a.shape[1] // PACK_ALIGN) * PACK_ALIGN
        a = jnp.pad(a, ((0, 0), (0, rows - a.shape[1]), (0, 0)))
        return a.reshape(a.shape[0] * rows, PACK_COLS)
    return a.reshape(a.shape[0] * _exchange_rows(nm, a.shape), PACK_COLS)


def _from_exchange(nm, rows2d, shard_shape):
    lead = rows2d.shape[:-2]
    n_layers = shard_shape[0]
    if BIG_KIND[nm] == "cols":
        a = rows2d.reshape(lead + (n_layers, -1, PACK_COLS))
        return lax.slice_in_dim(a, 0, shard_shape[-1], axis=a.ndim - 2)
    return rows2d.reshape(lead + tuple(shard_shape))


def _pack(pieces, dtype):
    slab = jnp.concatenate([p.astype(dtype) for p in pieces])
    half = -(-slab.shape[0] // (2 * PACK_ROW_BLOCK)) * PACK_ROW_BLOCK
    slab = jnp.pad(slab, ((0, 2 * half - slab.shape[0]), (0, 0)))
    return slab.reshape(2, half, PACK_COLS)


def _unpack(packed, row_counts):
    lead = packed.shape[:-3]
    slab = packed.reshape(lead + (-1, PACK_COLS))
    out, off = [], 0
    for rows in row_counts:
        out.append(lax.slice_in_dim(slab, off, off + rows, axis=slab.ndim - 2))
        off += rows
    return out


def _t(a):
    return jnp.swapaxes(a, -1, -2)


def _rows_from_lanes(a):
    return a[:, :FOX_H].T[:, None, :]


def _lanes_from_heads(a):
    return jnp.pad(a[:, 0, :].T, ((0, 0), (0, LANES - FOX_H)))


def _layer_weights(wts, small, l):
    w_in_t = wts["w_in"][l]
    w = {
        "wf_t": jnp.pad(w_in_t[W1_COLS:W1_COLS + FOX_H], ((0, LANES - FOX_H), (0, 0))),
        "w1_t": w_in_t[:W1_COLS], "wgl_t": w_in_t[W1_COLS + FOX_H:],
    }
    for nm, _ in BIG:
        if nm != "w_in":
            w[nm + "_t" if BIG_KIND[nm] == "cols" else nm] = wts[nm][l]
    for nm in list(w):
        other = nm[:-2] if nm.endswith("_t") else nm + "_t"
        w[other] = _t(w[nm])
    for nm in ("ffn1_norm", "mix_norm", "sg_norm", "xa_norm", "mem_norm", "ffn2_norm"):
        w[nm] = small[nm][l][None, :]
    w["conv_w"] = jnp.pad(small["conv_w"][l], ((0, 5), (0, 0)))
    w["sg_w"] = small["sg_w"][l]
    w["sg_w_t"] = _t(small["sg_w"][l])
    w["sg_bias"] = jnp.repeat(small["sg_b"][l].T, CHUNK, axis=1)
    w["fox_b"] = jnp.pad(small["fox_b_f"][l][None, :], ((0, 0), (0, LANES - FOX_H)))
    return w


def _layer_fwd(l, x, mem, w):
    s = {"x0": x}
    x1, s["a1"], s["b1"] = _ffn_fwd(f"ffn1_fwd_{l}", x, w["ffn1_norm"], w["ffn1_w_gate"], w["ffn1_w_up"], w["ffn1_w_down"])
    s["x1"] = x1
    cin, sgin, qkv, fl, gl = _proj_fwd(f"proj_fwd_{l}", x1, w["mix_norm"], w["w1"], w["wf"], w["wgl"])
    ya = _conv_fwd(f"conv_fwd_{l}", cin, w["conv_w"])
    yb = _sg_fwd(f"sg_fwd_{l}", sgin, w["sg_norm"], w["sg_w"], w["sg_bias"])
    c2 = _forget_cumsum(f"forget_fwd_{l}", fl, w["fox_b"])
    tables = _prune_tables(_flash_bounds(f"flash_bounds_{l}", qkv), c2, min(FLASH_BLOCK, qkv.shape[0]))
    yc, lse2 = _flash_fwd(f"flash_fwd_{l}", qkv, c2, _rows_from_lanes(c2), tables)
    x2 = _merge_fwd(f"merge_fwd_{l}", x1, ya, yb, yc, gl, w["w_branch"], w["w_out"])
    s.update(cin=cin, sgin=sgin, qkv=qkv, fl=fl, gl=gl, ya=ya, yb=yb, yc=yc, c2=c2, lse2=lse2, x2=x2, tables=tables)
    s["mn"], s["kx"], s["vx"] = _mem_fwd(f"mem_fwd_{l}", mem, w["mem_norm"], w["xa_wk"], w["xa_wv"])
    x3 = _xa_fwd(f"xa_fwd_{l}", x2, w["xa_norm"], s["kx"], s["vx"], w["xa_wq"], w["xa_wo"])
    s["x3"] = x3
    x4, s["a2"], s["b2"] = _ffn_fwd(f"ffn2_fwd_{l}", x3, w["ffn2_norm"], w["ffn2_w_gate"], w["ffn2_w_up"], w["ffn2_w_down"])
    return x4, s


def _ffn_grads(tag, x, dxo, a, b, w, pre):
    dx, da, db, sv, h, dy, dg = _ffn_bwd(f"{pre}_bwd_{tag}", x, dxo, a, b, w[pre + "_norm"], w[pre + "_w_down_t"],
                                         w[pre + "_w_gate_t"], w[pre + "_w_up_t"])
    g = {
        pre + "_norm": dg[0],
        pre + "_w_gate": _mm_tn(f"{pre}_dwg_{tag}", da, h, D_FF, D_MODEL, tk=FF_SPLIT, tn=D_MODEL),
        pre + "_w_up": _mm_tn(f"{pre}_dwu_{tag}", db, h, D_FF, D_MODEL, tk=FF_SPLIT, tn=D_MODEL),
        pre + "_w_down": _mm_tn(f"{pre}_dwd_{tag}", sv, dy, D_FF, D_MODEL, tk=FF_SPLIT, tn=D_MODEL),
    }
    return dx, g


def _layer_bwd(l, dx, mem, w, s):
    g = {}
    dx, gf = _ffn_grads(l, s["x3"], dx, s["a2"], s["b2"], w, "ffn2")
    g.update(gf)

    dx, h, o, dq, dy, dkx, dvx, dg = _xa_bwd(f"xa_bwd_{l}", s["x2"], dx, w["xa_norm"], s["kx"], s["vx"], w["xa_wq"],
                                              w["xa_wq_t"], w["xa_wo_t"])
    g["xa_norm"] = dg[0]
    g["xa_wq"] = _mm_tn(f"xa_dwq_{l}", h, dq, D_MODEL, D_MODEL, tk=D_MODEL, tn=D_MODEL)
    g["xa_wo"] = _mm_tn(f"xa_dwo_{l}", o, dy, D_MODEL, D_MODEL, tk=D_MODEL, tn=D_MODEL)
    dkb, dvb, dgm = _mem_bwd(f"mem_bwd_{l}", mem, w["mem_norm"], dkx, dvx, w["xa_wk_t"], w["xa_wv_t"])
    g["mem_norm"] = dgm[0]
    g["xa_wk"] = _mm_tn(f"xa_dwk_{l}", s["mn"], dkb, D_MODEL, D_MODEL, tk=D_MODEL, tn=D_MODEL)
    g["xa_wv"] = _mm_tn(f"xa_dwv_{l}", s["mn"], dvb, D_MODEL, D_MODEL, tk=D_MODEL, tn=D_MODEL)

    mg, dob, dgl, dbr, dya, dyb, dyc = _merge_bwd(f"merge_bwd_{l}", dx, s["ya"], s["yb"], s["yc"], s["gl"], w["w_branch"],
                                                  w["w_branch_t"], w["w_out_t"])
    g["w_out"] = _mm_tn(f"dwout_{l}", mg, dob, D_MODEL, D_MODEL, tk=D_MODEL, tn=D_MODEL)
    g["w_branch"] = jnp.stack([
        _mm_tn(f"dwbranch{k}_{l}", y, dbr, CONV_W, D_MODEL, tk=CONV_W, tn=D_MODEL, y_off=k * D_MODEL)
        for k, y in enumerate((s["ya"], s["yb"], s["yc"]))])

    dcin, dcw = _conv_bwd(f"conv_bwd_{l}", s["cin"], dya, w["conv_w"])
    g["conv_w"] = dcw[:3]
    dsg, dgn, dsw, dsb = _sg_bwd(f"sg_bwd_{l}", s["sgin"], dyb, w["sg_norm"], w["sg_w"], w["sg_w_t"], w["sg_bias"])
    g["sg_norm"], g["sg_w"], g["sg_b"] = dgn[0], dsw, dsb[:, :, 0]
    d_row = _rows_from_lanes(_flash_delta(f"flash_delta_{l}", dyc, s["yc"]))
    r_row = s["lse2"].reshape(FOX_H, 1, -1) - _rows_from_lanes(s["c2"])
    dq, dk, dv, dc_rows = _flash_bwd(f"flash_bwd_{l}", s["qkv"], dyc, r_row, d_row, s["c2"], s["tables"])
    dc = _lanes_from_heads(dc_rows.reshape(FOX_H, 1, -1))
    dfl, dbf = _forget_bwd(f"forget_bwd_{l}", dc, s["fl"], w["fox_b"])
    g["fox_b_f"] = dbf[0, :FOX_H]

    dx, h, dqkv, dfb, dg = _proj_bwd(f"proj_bwd_{l}", s["x1"], dx, dcin, dsg, dq, dk, dv, dfl, dgl, w["mix_norm"],
                                     w["w1_t"], w["wf_t"], w["wgl_t"])
    g["mix_norm"] = dg[0]
    c0, c1 = 3 * CONV_W, 3 * CONV_W + 2 * SG_W
    g["w_in"] = jnp.concatenate([
        _mm_tn(f"dwin_conv_{l}", dcin, h, c0, D_MODEL, tk=c0, tn=D_MODEL),
        _mm_tn(f"dwin_sg_{l}", dsg, h, c1 - c0, D_MODEL, tk=c1 - c0, tn=D_MODEL),
        _mm_tn(f"dwin_qkv_{l}", dqkv, h, 3 * FOX_W, D_MODEL, tk=3 * FOX_W, tn=D_MODEL),
        _mm_tn(f"dwin_f_{l}", dfb, h, LANES, D_MODEL, tk=LANES, tn=D_MODEL)[:FOX_H],
        _mm_tn(f"dwin_gl_{l}", dgl, h, 3 * D_MODEL, D_MODEL, tk=3 * D_MODEL // 2, tn=D_MODEL),
    ], axis=0)

    dx, gf = _ffn_grads(l, s["x0"], dx, s["a1"], s["b1"], w, "ffn1")
    g.update(gf)
    return dx, g


def _local_step(x, mem, target, wts, small):
    saved, lw = [], []
    for l in range(DEPTH):
        lw.append(_layer_weights(wts, small, l))
        x, s = _layer_fwd(l, x, mem, lw[l])
        saved.append(s)
    fin = small["final_norm"][None, :]
    dx, loss, dgf = _loss_head("loss_head", x, target, fin)
    layer_grads = [None] * DEPTH
    for l in reversed(range(DEPTH)):
        dx, layer_grads[l] = _layer_bwd(l, dx, mem, lw[l], saved[l])
    grads = {nm: jnp.stack([layer_grads[l][nm] for l in range(DEPTH)]) for nm in WEIGHTS if nm != "final_norm"}
    grads["final_norm"] = dgf[0]
    return loss[0, 0], dx, grads


def _small_slab(vals):
    rows = []
    for v in vals:
        flat = v.astype(F32).reshape(-1)
        n = -(-flat.shape[0] // LANES) * LANES
        rows.append(jnp.pad(flat, (0, n - flat.shape[0])).reshape(-1, LANES))
    slab = jnp.concatenate(rows)
    pad = -slab.shape[0] % 8
    return jnp.pad(slab, ((0, pad), (0, 0)))


def _small_unslab(slab, shapes):
    out, off = [], 0
    for shp in shapes:
        size = 1
        for s in shp:
            size *= s
        n_rows = -(-size // LANES)
        out.append(slab[off:off + n_rows].reshape(-1)[:size].reshape(shp))
        off += n_rows
    return out


def kernel(x, mem, ffn1_norm, ffn1_w_gate, ffn1_w_up, ffn1_w_down, mix_norm, w_in, conv_w, sg_norm, sg_w, sg_b, fox_b_f, w_branch, w_out, xa_norm, mem_norm, xa_wq, xa_wk, xa_wv, xa_wo, ffn2_norm, ffn2_w_gate, ffn2_w_up, ffn2_w_down, final_norm, loss_target, m_ffn1_norm, m_ffn1_w_gate, m_ffn1_w_up, m_ffn1_w_down, m_mix_norm, m_w_in, m_conv_w, m_sg_norm, m_sg_w, m_sg_b, m_fox_b_f, m_w_branch, m_w_out, m_xa_norm, m_mem_norm, m_xa_wq, m_xa_wk, m_xa_wv, m_xa_wo, m_ffn2_norm, m_ffn2_w_gate, m_ffn2_w_up, m_ffn2_w_down, m_final_norm, v_ffn1_norm, v_ffn1_w_gate, v_ffn1_w_up, v_ffn1_w_down, v_mix_norm, v_w_in, v_conv_w, v_sg_norm, v_sg_w, v_sg_b, v_fox_b_f, v_w_branch, v_w_out, v_xa_norm, v_mem_norm, v_xa_wq, v_xa_wk, v_xa_wv, v_xa_wo, v_ffn2_norm, v_ffn2_w_gate, v_ffn2_w_up, v_ffn2_w_down, v_final_norm):
    args = dict(locals())
    wv = {nm: args[nm] for nm in WEIGHTS}
    mv = {nm: args["m_" + nm] for nm in WEIGHTS}
    vv = {nm: args["v_" + nm] for nm in WEIGHTS}
    chip = 2 * lax.axis_index("x") + lax.axis_index("y")
    core = lax.axis_index("c")

    names = [nm for nm, _ in BIG]
    mine = {nm: _to_exchange(nm, wv[nm].astype(BF16)) for nm in names}
    row_counts = [mine[nm].shape[0] for nm in names]
    gathered = _gather_chips("gather_weights", _pack([mine[nm] for nm in names], BF16))
    wts = {}
    for nm, slab in zip(names, _unpack(gathered, row_counts)):
        p = _from_exchange(nm, jnp.where((jnp.arange(N_CHIPS) == chip)[:, None, None], mine[nm][None], slab), wv[nm].shape)
        if BIG_KIND[nm] == "branch":
            wts[nm] = jnp.concatenate([p[j] for j in range(N_CHIPS)], axis=3)
        else:
            wts[nm] = jnp.swapaxes(p, 0, 1).reshape(p.shape[1], N_CHIPS * p.shape[2], PACK_COLS)
    taps = _gather_all("gather_taps", _small_slab([conv_w]))
    taps = [_small_unslab(taps[2 * j], [conv_w.shape])[0] for j in range(N_CHIPS)]
    small = {nm: wv[nm] for nm in SMALL}
    small["conv_w"] = jnp.concatenate(taps, axis=2)

    loss, dx, grads = _local_step(x[0], mem[0], loss_target[0], wts, small)
    loss = lax.psum(loss, ("x", "y", "c"))

    small_shapes = [grads[nm].shape for nm in SMALL]
    slots = _gather_all("gather_small_grads", _small_slab([grads[nm] for nm in SMALL]))
    small_sum = _sum_slots("sum_small_grads", slots, slots.shape[1])
    small_g = dict(zip(SMALL, _small_unslab(small_sum, small_shapes)))
    taps_g = small_g["conv_w"]
    small_g["conv_w"] = lax.dynamic_slice_in_dim(taps_g, chip * conv_w.shape[2], conv_w.shape[2], axis=2)

    def grad_shard(nm, ax, j):
        g = grads[nm]
        ax = 1 if BIG_KIND[nm] == "cols" else ax
        size = g.shape[ax] // N_CHIPS
        return _to_exchange(nm, lax.slice_in_dim(g, j * size, (j + 1) * size, axis=ax), transposed=True)

    gp = jnp.stack([_pack([grad_shard(nm, ax, j) for nm, ax in BIG], BF16) for j in range(N_CHIPS)], axis=1)
    got = _swap_halves("reduce_swap_cores", gp)
    chip_sum = _add_pair("reduce_add_cores", gp, got, core.astype(jnp.int32).reshape(1))
    parts = _scatter_chips("reduce_scatter_chips", chip_sum)
    half = _sum_parts("reduce_add_chips", chip_sum, parts, chip.astype(jnp.int32).reshape(1))
    other = _join_halves("reduce_join_cores", half)
    full = jnp.where(core == 0, jnp.stack([half, other]), jnp.stack([other, half]))
    big_g = {}
    for nm, slab in zip(names, _unpack(full, row_counts)):
        g = _from_exchange(nm, slab, wv[nm].shape)
        big_g[nm] = jnp.swapaxes(g, 1, 2) if BIG_KIND[nm] == "cols" else g

    g_out = {**small_g, **big_g}
    delta, new_m, new_v = {}, {}, {}
    for nm, _ in BIG:
        shp = wv[nm].shape
        two_d = (-1, shp[-1])
        d, m2, v2 = _adamw("adamw_" + nm, wv[nm].reshape(two_d), g_out[nm].reshape(two_d), mv[nm].reshape(two_d),
                           vv[nm].reshape(two_d))
        delta[nm], new_m[nm], new_v[nm] = d.reshape(shp), m2.reshape(shp), v2.reshape(shp)
    slab_shapes = [wv[nm].shape for nm in SMALL]
    d, m2, v2 = _adamw("adamw_small", _small_slab([wv[nm] for nm in SMALL]), _small_slab([g_out[nm] for nm in SMALL]),
                       _small_slab([mv[nm] for nm in SMALL]), _small_slab([vv[nm] for nm in SMALL]))
    for out, slab in ((delta, d), (new_m, m2), (new_v, v2)):
        out.update(zip(SMALL, _small_unslab(slab, slab_shapes)))

    return (loss, dx[None], *[g_out[nm] for nm in WEIGHTS], *[delta[nm] for nm in WEIGHTS],
            *[new_m[nm] for nm in WEIGHTS], *[new_v[nm] for nm in WEIGHTS])
```

```python
import functools

import jax
import jax.numpy as jnp
from jax import lax
from jax.experimental import pallas as pl
from jax.experimental.pallas import tpu as pltpu

F32, BF16 = jnp.float32, jnp.bfloat16
MESH_ID = pl.DeviceIdType.MESH

D_MODEL = 1024
DEPTH = 2
D_FF = 2816
CONV_W = 512
SG_W = 512
SG_G = 4
CHUNK = 128
FOX_H = 8
FOX_D = 64
FOX_W = FOX_H * FOX_D
FOX_SCALE = FOX_D ** -0.5
LOG2E = 1.4426950408889634
XA_H = 4
XA_D = D_MODEL // XA_H
N_CHIPS = 4
RMS_EPS = 1e-6
W1_COLS = 3 * CONV_W + 2 * SG_W + 3 * FOX_W
LANES = 128
HALO = 16

ADAM_LR, ADAM_B1, ADAM_B2, ADAM_EPS, ADAM_WD, ADAM_STEP = 0.001, 0.9, 0.999, 1e-08, 0.01, 10

ROW_BLOCK = 512
FFN_BWD_ROW_BLOCK = 256
MM_TT = 2048
FLASH_BLOCK = 512
FLASH_STRIP = 32
PRUNE_LOG2 = 40.0
FF_SPLIT = 1408
VMEM_LIMIT = 56 * 1024 * 1024
PACK_COLS = 1024
PACK_ROW_BLOCK = 208

BIG = (
    ("ffn1_w_gate", 2), ("ffn1_w_up", 2), ("ffn1_w_down", 1), ("w_in", 2), ("w_branch", 3), ("w_out", 1),
    ("xa_wq", 1), ("xa_wk", 1), ("xa_wv", 1), ("xa_wo", 1), ("ffn2_w_gate", 2), ("ffn2_w_up", 2), ("ffn2_w_down", 1),
)
BIG_KIND = {nm: ("branch" if nm == "w_branch" else "cols" if ax == 2 else "rows") for nm, ax in BIG}
SMALL = ("ffn1_norm", "mix_norm", "conv_w", "sg_norm", "sg_w", "sg_b", "fox_b_f", "xa_norm", "mem_norm", "ffn2_norm",
         "final_norm")
WEIGHTS = ("ffn1_norm", "ffn1_w_gate", "ffn1_w_up", "ffn1_w_down", "mix_norm", "w_in", "conv_w", "sg_norm", "sg_w",
           "sg_b", "fox_b_f", "w_branch", "w_out", "xa_norm", "mem_norm", "xa_wq", "xa_wk", "xa_wv", "xa_wo",
           "ffn2_norm", "ffn2_w_gate", "ffn2_w_up", "ffn2_w_down", "final_norm")


def _dot(a, b):
    return jnp.dot(a, b, preferred_element_type=F32)


def _dot_nt(a, b):
    return lax.dot_general(a, b, (((1,), (1,)), ((), ())), preferred_element_type=F32)


def _dot_tn(a, b):
    return lax.dot_general(a, b, (((0,), (0,)), ((), ())), preferred_element_type=F32)


def _rms_stats(x):
    r = lax.rsqrt(jnp.mean(x * x, axis=-1, keepdims=True) + RMS_EPS)
    return x * r, r


def _rms_bwd(xh, r, g, dy):
    dg = jnp.sum(dy * xh, axis=0, keepdims=True)
    dxh = dy * g
    dx = r * (dxh - xh * jnp.mean(dxh * xh, axis=-1, keepdims=True))
    return dx, dg


def _gelu(x):
    k = 0.7978845608028654
    t = jnp.tanh(k * (x + 0.044715 * x * x * x))
    return 0.5 * x * (1.0 + t), t


def _gelu_grad(x, t):
    k = 0.7978845608028654
    return 0.5 * (1.0 + t) + 0.5 * x * (1.0 - t * t) * k * (1.0 + 3.0 * 0.044715 * x * x)


def _split3_dot(tri, x):
    hi = x.astype(BF16)
    r1 = x - hi.astype(F32)
    mid = r1.astype(BF16)
    lo = (r1 - mid.astype(F32)).astype(BF16)
    return _dot(tri, hi) + _dot(tri, mid) + _dot(tri, lo)


def _params(n_grid):
    return pltpu.CompilerParams(dimension_semantics=("arbitrary",) * n_grid, vmem_limit_bytes=VMEM_LIMIT)


def _full_spec(shape):
    nd = len(shape)
    return pl.BlockSpec(tuple(shape), lambda *_: (0,) * nd)


def _rowcall(name, body, n_tokens, tm, rows, consts, residents, row_outs, acc_outs, scratch=()):
    n = n_tokens // tm
    rows = [r if isinstance(r, tuple) else (r, pl.BlockSpec((tm, r.shape[1]), lambda i: (i, 0))) for r in rows]
    n_in, n_w = len(rows) + len(consts), len(residents)
    n_out = len(row_outs) + len(acc_outs)
    in_specs = ([r[1] for r in rows] + [_full_spec(c.shape) for c in consts]
                + [pl.BlockSpec(memory_space=pl.ANY)] * n_w)
    out_shape = ([jax.ShapeDtypeStruct((n_tokens, c), dt) for c, dt in row_outs]
                 + [jax.ShapeDtypeStruct(s, dt) for s, dt in acc_outs])
    out_specs = ([pl.BlockSpec((tm, c), lambda i: (i, 0)) for c, _ in row_outs]
                 + [_full_spec(s) for s, _ in acc_outs])
    scratch_shapes = [pltpu.VMEM(w.shape, w.dtype) for w in residents]
    if n_w:
        scratch_shapes.append(pltpu.SemaphoreType.DMA((n_w,)))
    scratch_shapes += list(scratch)

    def kern(*refs):
        ins, w_hbm = refs[:n_in], refs[n_in:n_in + n_w]
        outs = refs[n_in + n_w:n_in + n_w + n_out]
        rest = refs[n_in + n_w + n_out:]
        w_vmem = rest[:n_w]
        extra = rest[n_w + 1:] if n_w else rest
        i = pl.program_id(0)

        @pl.when(i == 0)
        def _():
            copies = [pltpu.make_async_copy(w_hbm[k], w_vmem[k], rest[n_w].at[k]) for k in range(n_w)]
            for cp in copies:
                cp.start()
            for cp in copies:
                cp.wait()
            for a in outs[len(row_outs):]:
                a[...] = jnp.zeros(a.shape, a.dtype)

        body(i, n, *ins, *w_vmem, *outs, *extra)

    res = pl.pallas_call(
        kern, name=name, grid=(n,), in_specs=in_specs, out_specs=out_specs, out_shape=out_shape,
        scratch_shapes=scratch_shapes, compiler_params=_params(1),
    )(*[r[0] for r in rows], *consts, *residents)
    return res


def _mm_tn(name, x, y, k_dim, n_dim, *, tk, tn, x_off=0, y_off=0, tt=MM_TT):
    n_tok = x.shape[0]
    tt = min(tt, n_tok)
    n_t = n_tok // tt
    xb, yb = x_off // tk, y_off // tn

    def kern(x_ref, y_ref, o_ref):
        @pl.when(pl.program_id(2) == 0)
        def _():
            o_ref[...] = jnp.zeros(o_ref.shape, F32)

        o_ref[...] += _dot_tn(x_ref[...], y_ref[...])

    return pl.pallas_call(
        kern, name=name, grid=(k_dim // tk, n_dim // tn, n_t),
        in_specs=[pl.BlockSpec((tt, tk), lambda k, n, t: (t, xb + k)),
                  pl.BlockSpec((tt, tn), lambda k, n, t: (t, yb + n))],
        out_specs=pl.BlockSpec((tk, tn), lambda k, n, t: (k, n)),
        out_shape=jax.ShapeDtypeStruct((k_dim, n_dim), F32),
        compiler_params=_params(3),
    )(x, y)


def _ffn_fwd(name, x, gain, wg_t, wu_t, wd):
    n_tok, tm = x.shape[0], min(ROW_BLOCK, x.shape[0])

    def body(i, n, x_ref, g_ref, wg_ref, wu_ref, wd_ref, xo_ref, a_ref, b_ref):
        x_v = x_ref[...]
        xh, _ = _rms_stats(x_v)
        h = (xh * g_ref[...]).astype(BF16)
        y = jnp.zeros((tm, D_MODEL), F32)
        for f0 in range(0, D_FF, FF_SPLIT):
            sl = slice(f0, f0 + FF_SPLIT)
            a = _dot_nt(h, wg_ref[sl, :])
            b = _dot_nt(h, wu_ref[sl, :])
            a_ref[:, sl] = a.astype(BF16)
            b_ref[:, sl] = b.astype(BF16)
            s = (a * jax.nn.sigmoid(a) * b).astype(BF16)
            y = y + _dot(s, wd_ref[sl, :])
        xo_ref[...] = x_v + 0.5 * y

    return _rowcall(name, body, n_tok, tm, [x], [gain], [wg_t, wu_t, wd],
                    [(D_MODEL, F32), (D_FF, BF16), (D_FF, BF16)], [])


def _ffn_bwd(name, x, dxo, a, b, gain, wd, wg_t, wu_t):
    n_tok, tm = x.shape[0], min(FFN_BWD_ROW_BLOCK, x.shape[0])

    def body(i, n, x_ref, dxo_ref, a_ref, b_ref, g_ref, wdt_ref, wgt_ref, wut_ref,
             dx_ref, da_ref, db_ref, s_ref, h_ref, dy_ref, dg_ref):
        g = g_ref[...]
        xh, r = _rms_stats(x_ref[...])
        h_ref[...] = (xh * g).astype(BF16)
        dxo_v = dxo_ref[...]
        dy = (0.5 * dxo_v).astype(BF16)
        dy_ref[...] = dy
        dh = jnp.zeros((tm, D_MODEL), F32)
        for f0 in range(0, D_FF, FF_SPLIT):
            sl = slice(f0, f0 + FF_SPLIT)
            a_v = a_ref[:, sl].astype(F32)
            b_v = b_ref[:, sl].astype(F32)
            ds = _dot_nt(dy, wdt_ref[sl, :])
            sig = jax.nn.sigmoid(a_v)
            sa = a_v * sig
            s_ref[:, sl] = (sa * b_v).astype(BF16)
            da = (ds * b_v * (sig * (1.0 + a_v * (1.0 - sig)))).astype(BF16)
            db = (ds * sa).astype(BF16)
            da_ref[:, sl] = da
            db_ref[:, sl] = db
            dh = dh + _dot(da, wgt_ref[sl, :]) + _dot(db, wut_ref[sl, :])
        dx, dg = _rms_bwd(xh, r, g, dh)
        dx_ref[...] = dxo_v + dx
        dg_ref[...] += dg

    return _rowcall(name, body, n_tok, tm, [x, dxo, a, b], [gain], [wd, wg_t, wu_t],
                    [(D_MODEL, F32), (D_FF, BF16), (D_FF, BF16), (D_FF, BF16), (D_MODEL, BF16), (D_MODEL, BF16)],
                    [((1, D_MODEL), F32)])


def _proj_fwd(name, x, gain, w1_t, wf_t, wgl_t):
    n_tok, tm = x.shape[0], min(ROW_BLOCK, x.shape[0])
    c0, c1 = 3 * CONV_W, 3 * CONV_W + 2 * SG_W

    def body(i, n, x_ref, g_ref, w1_ref, wf_ref, wgl_ref, cin_ref, sg_ref, qkv_ref, fl_ref, gl_ref):
        xh, _ = _rms_stats(x_ref[...])
        h = (xh * g_ref[...]).astype(BF16)
        cin_ref[...] = _dot_nt(h, w1_ref[0:c0, :]).astype(BF16)
        sg_ref[...] = _dot_nt(h, w1_ref[c0:c1, :]).astype(BF16)
        qkv_ref[...] = _dot_nt(h, w1_ref[c1:W1_COLS, :]).astype(BF16)
        fl_ref[...] = _dot_nt(h, wf_ref[...])
        gl_ref[...] = _dot_nt(h, wgl_ref[...]).astype(BF16)

    return _rowcall(name, body, n_tok, tm, [x], [gain], [w1_t, wf_t, wgl_t],
                    [(3 * CONV_W, BF16), (2 * SG_W, BF16), (3 * FOX_W, BF16), (LANES, F32), (3 * D_MODEL, BF16)], [])


def _proj_bwd(name, x, dxin, dcin, dsg, dq, dk, dv, dfl, dgl, gain, w1_t, wf_t, wgl_t):
    n_tok, tm = x.shape[0], min(ROW_BLOCK, x.shape[0])
    c0, c1 = 3 * CONV_W, 3 * CONV_W + 2 * SG_W

    def body(i, n, x_ref, dxin_ref, dcin_ref, dsg_ref, dq_ref, dk_ref, dv_ref, dfl_ref, dgl_ref, g_ref,
             w1t_ref, wft_ref, wglt_ref, dx_ref, h_ref, dqkv_ref, dfb_ref, dg_ref):
        g = g_ref[...]
        xh, r = _rms_stats(x_ref[...])
        h_ref[...] = (xh * g).astype(BF16)
        dfb = dfl_ref[...].astype(BF16)
        dfb_ref[...] = dfb
        dh = _dot(dcin_ref[...], w1t_ref[0:c0, :])
        dh = dh + _dot(dsg_ref[...], w1t_ref[c0:c1, :])
        for k, d_ref in enumerate((dq_ref, dk_ref, dv_ref)):
            d_b = d_ref[...].astype(BF16)
            dqkv_ref[:, k * FOX_W:(k + 1) * FOX_W] = d_b
            dh = dh + _dot(d_b, w1t_ref[c1 + k * FOX_W:c1 + (k + 1) * FOX_W, :])
        dh = dh + _dot(dfb, wft_ref[...])
        dh = dh + _dot(dgl_ref[...], wglt_ref[...])
        dx, dg = _rms_bwd(xh, r, g, dh)
        dx_ref[...] = dxin_ref[...] + dx
        dg_ref[...] += dg

    return _rowcall(name, body, n_tok, tm, [x, dxin, dcin, dsg, dq, dk, dv, dfl, dgl], [gain], [w1_t, wf_t, wgl_t],
                    [(D_MODEL, F32), (D_MODEL, BF16), (3 * FOX_W, BF16), (LANES, BF16)], [((1, D_MODEL), F32)])


def _conv_taps(z, prev_z, i):
    tm = z.shape[0]
    row = lax.broadcasted_iota(jnp.int32, (tm, 1), 0)
    live = (i > 0).astype(F32)
    p1, p2 = prev_z[HALO - 1:HALO, :] * live, prev_z[HALO - 2:HALO - 1, :] * live
    z1 = jnp.where(row == 0, p1, pltpu.roll(z, 1, 0))
    z2 = jnp.where(row == 0, p2, jnp.where(row == 1, p1, pltpu.roll(z, 2, 0)))
    return z1, z2


def _prev_spec(tm, cols):
    return pl.BlockSpec((HALO, cols), lambda i: (jnp.maximum(i * (tm // HALO) - 1, 0), 0))


def _next_spec(tm, cols, n):
    last = n * (tm // HALO) - 1
    return pl.BlockSpec((HALO, cols), lambda i: (jnp.minimum((i + 1) * (tm // HALO), last), 0))


def _conv_fwd(name, cin, cw):
    n_tok, tm = cin.shape[0], min(ROW_BLOCK, cin.shape[0])
    w = CONV_W

    def body(i, n, c_ref, p_ref, cw_ref, ya_ref):
        c_v, p_v = c_ref[...].astype(F32), p_ref[...].astype(F32)
        z = c_v[:, w:2 * w] * c_v[:, 2 * w:]
        z1, z2 = _conv_taps(z, p_v[:, w:2 * w] * p_v[:, 2 * w:], i)
        y = cw_ref[0:1, :] * z2 + cw_ref[1:2, :] * z1 + cw_ref[2:3, :] * z
        ya_ref[...] = (c_v[:, 0:w] * y).astype(BF16)

    return _rowcall(name, body, n_tok, tm, [cin, (cin, _prev_spec(tm, 3 * w))], [cw], [], [(w, BF16)], [])[0]


def _conv_bwd(name, cin, dya, cw):
    n_tok, tm = cin.shape[0], min(ROW_BLOCK, cin.shape[0])
    w = CONV_W
    n_blocks = n_tok // tm

    def body(i, n, c_ref, p_ref, nx_ref, dya_ref, ndya_ref, cw_ref, dc_ref, dcw_ref):
        c_v, p_v = c_ref[...].astype(F32), p_ref[...].astype(F32)
        ab, ac, ah = c_v[:, 0:w], c_v[:, w:2 * w], c_v[:, 2 * w:]
        z = ac * ah
        z1, z2 = _conv_taps(z, p_v[:, w:2 * w] * p_v[:, 2 * w:], i)
        w0, w1, w2 = cw_ref[0:1, :], cw_ref[1:2, :], cw_ref[2:3, :]
        y = w0 * z2 + w1 * z1 + w2 * z
        dya_v = dya_ref[...].astype(F32)
        dy = dya_v * ab
        live = (i < n - 1).astype(F32)
        ndy = ndya_ref[...].astype(F32) * nx_ref[:, 0:w].astype(F32) * live
        row = lax.broadcasted_iota(jnp.int32, (tm, 1), 0)
        dy1 = jnp.where(row == tm - 1, ndy[0:1, :], pltpu.roll(dy, tm - 1, 0))
        dy2 = jnp.where(row == tm - 1, ndy[1:2, :], jnp.where(row == tm - 2, ndy[0:1, :], pltpu.roll(dy, tm - 2, 0)))
        dz = w2 * dy + w1 * dy1 + w0 * dy2
        dc_ref[:, 0:w] = (dya_v * y).astype(BF16)
        dc_ref[:, w:2 * w] = (dz * ah).astype(BF16)
        dc_ref[:, 2 * w:] = (dz * ac).astype(BF16)
        dcw_ref[0:1, :] += jnp.sum(dy * z2, axis=0, keepdims=True)
        dcw_ref[1:2, :] += jnp.sum(dy * z1, axis=0, keepdims=True)
        dcw_ref[2:3, :] += jnp.sum(dy * z, axis=0, keepdims=True)

    return _rowcall(name, body, n_tok, tm,
                    [cin, (cin, _prev_spec(tm, 3 * w)), (cin, _next_spec(tm, 3 * w, n_blocks)),
                     dya, (dya, _next_spec(tm, w, n_blocks))],
                    [cw], [], [(3 * w, BF16)], [((8, w), F32)])


def _sg_common(sg_ref, gn_ref):
    s_v = sg_ref[...].astype(F32)
    u, v = s_v[:, 0:SG_W], s_v[:, SG_W:]
    ug, tu = _gelu(u)
    vg, tv = _gelu(v)
    vh, r = _rms_stats(vg)
    vn = (vh * gn_ref[...]).astype(BF16)
    return u, v, ug, tu, tv, vh, r, vn


def _sg_fwd(name, sgin, gn, sgw, bias_full):
    n_tok, tm = sgin.shape[0], min(ROW_BLOCK, sgin.shape[0])

    def body(i, n, sg_ref, gn_ref, w_ref, bias_ref, yb_ref):
        _, _, ug, _, _, _, _, vn = _sg_common(sg_ref, gn_ref)
        tril = lax.broadcasted_iota(jnp.int32, (CHUNK, CHUNK), 0) >= lax.broadcasted_iota(jnp.int32, (CHUNK, CHUNK), 1)
        wt = [jnp.where(tril, w_ref[g], 0.0).astype(BF16) for g in range(SG_G)]
        for c0 in range(0, tm, CHUNK):
            sv = jnp.concatenate(
                [_dot(wt[g], vn[c0:c0 + CHUNK, g * CHUNK:(g + 1) * CHUNK]) for g in range(SG_G)], axis=1)
            sv = sv + bias_ref[...]
            yb_ref[c0:c0 + CHUNK, :] = (ug[c0:c0 + CHUNK, :] * sv).astype(BF16)

    return _rowcall(name, body, n_tok, tm, [sgin], [gn, sgw, bias_full], [], [(SG_W, BF16)], [])[0]


def _sg_bwd(name, sgin, dyb, gn, sgw, sgw_t, bias_full):
    n_tok, tm = sgin.shape[0], min(ROW_BLOCK, sgin.shape[0])

    def body(i, n, sg_ref, dyb_ref, gn_ref, w_ref, wt_ref, bias_ref, dsg_ref, dgn_ref, dw_ref, db_ref):
        u, v, ug, tu, tv, vh, r, vn = _sg_common(sg_ref, gn_ref)
        r0 = lax.broadcasted_iota(jnp.int32, (CHUNK, CHUNK), 0)
        r1 = lax.broadcasted_iota(jnp.int32, (CHUNK, CHUNK), 1)
        wt = [jnp.where(r0 >= r1, w_ref[g], 0.0).astype(BF16) for g in range(SG_G)]
        wtt = [jnp.where(r0 <= r1, wt_ref[g], 0.0).astype(BF16) for g in range(SG_G)]
        dyb_v = dyb_ref[...].astype(F32)
        dvn_rows = []
        for c0 in range(0, tm, CHUNK):
            rows = slice(c0, c0 + CHUNK)
            svs, dvns = [], []
            dsv = dyb_v[rows, :] * ug[rows, :]
            for g in range(SG_G):
                cols = slice(g * CHUNK, (g + 1) * CHUNK)
                svs.append(_dot(wt[g], vn[rows, cols]))
                dsv_g = dsv[:, cols]
                dsv_b = dsv_g.astype(BF16)
                dvns.append(_dot(wtt[g], dsv_b))
                dw_ref[g] += jnp.where(r0 >= r1, _dot_nt(dsv_b, vn[rows, cols]), 0.0)
                db_ref[g] += jnp.broadcast_to(jnp.sum(dsv_g, axis=1, keepdims=True), (CHUNK, CHUNK))
            sv = jnp.concatenate(svs, axis=1) + bias_ref[...]
            dug = dyb_v[rows, :] * sv
            dsg_ref[rows, 0:SG_W] = (dug * _gelu_grad(u[rows, :], tu[rows, :])).astype(BF16)
            dvn_rows.append(jnp.concatenate(dvns, axis=1))
        dvn = jnp.concatenate(dvn_rows, axis=0)
        dvg, dgn = _rms_bwd(vh, r, gn_ref[...], dvn)
        dsg_ref[:, SG_W:] = (dvg * _gelu_grad(v, tv)).astype(BF16)
        dgn_ref[...] += dgn

    return _rowcall(name, body, n_tok, tm, [sgin, dyb], [gn, sgw, sgw_t, bias_full], [], [(2 * SG_W, BF16)],
                    [((1, SG_W), F32), ((SG_G, CHUNK, CHUNK), F32), ((SG_G, CHUNK, CHUNK), F32)])


def _forget_cumsum(name, fl, bf):
    n_tok, tm = fl.shape[0], min(ROW_BLOCK, fl.shape[0])

    def body(i, n, fl_ref, b_ref, c_ref, carry):
        @pl.when(i == 0)
        def _():
            carry[...] = jnp.zeros(carry.shape, F32)

        z = fl_ref[...] + b_ref[...]
        lf = jnp.minimum(z, 0.0) - jnp.log1p(jnp.exp(-jnp.abs(z)))
        tri = (lax.broadcasted_iota(jnp.int32, (tm, tm), 0) >= lax.broadcasted_iota(jnp.int32, (tm, tm), 1)).astype(BF16)
        c = _split3_dot(tri, lf) + carry[...]
        c_ref[...] = c * LOG2E
        carry[...] = c[tm - 1:tm, :]

    return _rowcall(name, body, n_tok, tm, [fl], [bf], [], [(LANES, F32)], [], scratch=[pltpu.VMEM((1, LANES), F32)])[0]


def _forget_bwd(name, dc, fl, bf):
    n_tok, tm = fl.shape[0], min(ROW_BLOCK, fl.shape[0])
    n = n_tok // tm
    rev = pl.BlockSpec((tm, LANES), lambda i: (n - 1 - i, 0))

    def kern(dc_ref, fl_ref, b_ref, dfl_ref, db_ref, carry):
        @pl.when(pl.program_id(0) == 0)
        def _():
            carry[...] = jnp.zeros(carry.shape, F32)
            db_ref[...] = jnp.zeros(db_ref.shape, F32)

        triu = (lax.broadcasted_iota(jnp.int32, (tm, tm), 0) <= lax.broadcasted_iota(jnp.int32, (tm, tm), 1)).astype(BF16)
        dlf = _split3_dot(triu, dc_ref[...]) + carry[...]
        carry[...] = dlf[0:1, :]
        z = fl_ref[...] + b_ref[...]
        dfl = dlf * jax.nn.sigmoid(-z)
        dfl_ref[...] = dfl
        db_ref[...] += jnp.sum(dfl, axis=0, keepdims=True)

    return pl.pallas_call(
        kern, name=name, grid=(n,), in_specs=[rev, rev, _full_spec((1, LANES))],
        out_specs=[rev, _full_spec((1, LANES))],
        out_shape=[jax.ShapeDtypeStruct((n_tok, LANES), F32), jax.ShapeDtypeStruct((1, LANES), F32)],
        scratch_shapes=[pltpu.VMEM((1, LANES), F32)], compiler_params=_params(1),
    )(dc, fl, bf)


def _lane_pick(c_blk, h):
    lane = lax.broadcasted_iota(jnp.int32, c_blk.shape, 1)
    return jnp.broadcast_to(jnp.sum(jnp.where(lane == h, c_blk, 0.0), axis=1, keepdims=True), c_blk.shape)


def _wide(stat, width):
    return jnp.tile(stat, (1, width // LANES))


def _pair_half(rows):
    return lax.broadcasted_iota(jnp.int32, (rows, LANES), 1) // FOX_D


def _col_to_row(col):
    pick = (lax.broadcasted_iota(jnp.int32, (8, LANES), 1) == 0).astype(BF16)
    hi = col.astype(BF16)
    r1 = col - hi.astype(F32)
    mid = r1.astype(BF16)
    lo = (r1 - mid.astype(F32)).astype(BF16)
    return (_dot_nt(pick, hi) + _dot_nt(pick, mid) + _dot_nt(pick, lo))[0:1, :]


def _tri_table(nb, lower):
    rows = [(a, b) for a in range(nb) for b in (range(a + 1) if lower else range(a, nb))]
    return tuple(jnp.asarray([r[k] for r in rows], jnp.int32) for k in range(2))


def _flash_bounds(name, qkv):
    n_tok = qkv.shape[0]
    blk = min(FLASH_BLOCK, n_tok)

    def kern(x_ref, o_ref):
        head = (lax.broadcasted_iota(jnp.int32, (FOX_W, LANES), 0) // FOX_D
                == lax.broadcasted_iota(jnp.int32, (FOX_W, LANES), 1)).astype(BF16)
        q, k = x_ref[:, 0:FOX_W].astype(F32), x_ref[:, FOX_W:2 * FOX_W].astype(F32)
        qn = jnp.sqrt(jnp.max(_dot((q * q).astype(BF16), head) * 1.01, axis=0, keepdims=True))
        kn = jnp.sqrt(jnp.max(_dot((k * k).astype(BF16), head) * 1.01, axis=0, keepdims=True))
        diag = jnp.min(_dot((q * k).astype(BF16), head), axis=0, keepdims=True) - qn * kn * (2.0 ** -8)
        o_ref[...] = jnp.concatenate([qn, kn, diag, jnp.zeros((5, LANES), F32)], axis=0)

    return pl.pallas_call(
        kern, name=name, grid=(n_tok // blk,),
        in_specs=[pl.BlockSpec((blk, 2 * FOX_W), lambda b: (b, 0))], out_specs=pl.BlockSpec((8, LANES), lambda b: (b, 0)),
        out_shape=jax.ShapeDtypeStruct((n_tok // blk * 8, LANES), F32), compiler_params=_params(1),
    )(qkv)


def _prune_tables(bounds, c2, blk):
    nb = c2.shape[0] // blk
    tab = bounds.reshape(nb, 8, LANES)
    c_blocks = c2.reshape(nb, blk, LANES)
    return tuple(t[:, :FOX_H].reshape(-1)
                 for t in (tab[:, 0], tab[:, 1], tab[:, 2], c_blocks[:, 0], c_blocks[:, blk - 1]))


def _block_is_live(tables, p, i, j):
    qn, kn, diag, c_first, c_last = tables
    a = FOX_SCALE * LOG2E
    live = []
    for e in range(2):
        h = 2 * p + e
        qi, kj = FOX_H * i + h, FOX_H * j + h
        u = a * qn[qi] * kn[kj] + c_first[qi] - c_last[kj]
        lo = a * diag[qi]
        live.append((u - lo) > -PRUNE_LOG2)
    return live


def _flash_fwd(name, qkv, c2, ct2, tables):
    n_tok = qkv.shape[0]
    blk = min(FLASH_BLOCK, n_tok)
    nb = n_tok // blk
    n_pair = FOX_H // 2
    it, jt = _tri_table(nb, True)

    strip = min(FLASH_STRIP, blk)

    def kern(it_ref, jt_ref, qn_ref, kn_ref, dg_ref, cf_ref, cl_ref, q_ref, k_ref, v_ref, c_ref, ct_ref, o_ref, lse_ref,
             m_s, l_s, acc_s, cq_s, qm_s, al_s, s_s, pb_s):
        n = pl.program_id(1)
        i, j = it_ref[n], jt_ref[n]
        half = _pair_half(blk)
        live = _block_is_live((qn_ref, kn_ref, dg_ref, cf_ref, cl_ref), pl.program_id(0), i, j)

        @pl.when(j == 0)
        def _():
            m_s[...] = jnp.full(m_s.shape, -jnp.inf, F32)
            l_s[...] = jnp.zeros(l_s.shape, F32)
            acc_s[...] = jnp.zeros(acc_s.shape, F32)
            for e in range(2):
                cq_s[e] = _lane_pick(c_ref[...], 2 * pl.program_id(0) + e)
                qm_s[e] = jnp.where(half == e, q_ref[...], jnp.zeros_like(q_ref[...]))

        def step(heads, on_diagonal):
            k_v, v_v = k_ref[...], v_ref[...]
            for e in heads:
                s_s[e] = _dot_nt(qm_s[e], k_v)
            for e in heads:
                ck = ct_ref[e]
                for r0 in range(0, blk, strip):
                    rs = slice(r0, r0 + strip)
                    t = s_s[e, rs, :] * (FOX_SCALE * LOG2E) - ck
                    if on_diagonal:
                        keep = (lax.broadcasted_iota(jnp.int32, (strip, blk), 1)
                                <= lax.broadcasted_iota(jnp.int32, (strip, blk), 0) + r0)
                        t = jnp.where(keep, t, -jnp.inf)
                    m_old, cq = m_s[e, rs, :], cq_s[e, rs, :]
                    m_new = jnp.maximum(m_old, jnp.max(t, axis=1, keepdims=True) + cq)
                    p = jnp.exp2(t - _wide(m_new - cq, blk))
                    alpha = jnp.exp2(m_old - m_new)
                    l_s[e, rs, :] = alpha * l_s[e, rs, :] + jnp.sum(p, axis=1, keepdims=True)
                    m_s[e, rs, :] = m_new
                    al_s[e, rs, :] = alpha
                    pb_s[e, rs, :] = p.astype(BF16)
            if len(heads) == 2:
                pv = jnp.where(half == 0, _dot(pb_s[0], v_v), _dot(pb_s[1], v_v))
                acc_s[...] = jnp.where(half == 0, al_s[0], al_s[1]) * acc_s[...] + pv
            else:
                e = heads[0]
                acc_s[...] = jnp.where(half == e, al_s[e] * acc_s[...] + _dot(pb_s[e], v_v), acc_s[...])

        below = j < i

        @pl.when(below & live[0] & live[1])
        def _():
            step((0, 1), False)

        @pl.when(below & live[0] & jnp.logical_not(live[1]))
        def _():
            step((0,), False)

        @pl.when(below & jnp.logical_not(live[0]) & live[1])
        def _():
            step((1,), False)

        @pl.when(j == i)
        def _():
            step((0, 1), True)
            o_ref[...] = (acc_s[...] / jnp.where(half == 0, l_s[0], l_s[1])).astype(BF16)
            rows = pl.ds(pl.multiple_of(i * blk, blk), blk)
            lse_ref[0, :, rows] = jnp.concatenate(
                [_col_to_row(m_s[e] + jnp.log(l_s[e]) * LOG2E) for e in range(2)], axis=0)

    def key_block(p, n, it_r, jt_r, tabs):
        i, j = it_r[n], jt_r[n]
        live = _block_is_live(tabs, p, i, j)
        return jnp.where(live[0] | live[1], j, i)

    def col_block(first, inner):
        def index(p, n, it_r, jt_r, *tabs):
            return ((key_block(p, n, it_r, jt_r, tabs) if inner else it_r[n]), first + p)
        return pl.BlockSpec((blk, LANES), index)

    pair_stat = (2, blk, LANES)
    grid_spec = pltpu.PrefetchScalarGridSpec(
        num_scalar_prefetch=2 + len(tables), grid=(n_pair, int(it.shape[0])),
        in_specs=[col_block(0, False), col_block(n_pair, True), col_block(2 * n_pair, True),
                  pl.BlockSpec((blk, LANES), lambda p, n, it_r, jt_r, *_: (it_r[n], 0)),
                  pl.BlockSpec((2, 1, blk), lambda p, n, it_r, jt_r, *tabs: (p, 0, key_block(p, n, it_r, jt_r, tabs)))],
        out_specs=[col_block(0, False), pl.BlockSpec((1, 2, n_tok), lambda p, n, it_r, jt_r, *_: (p, 0, 0))],
        scratch_shapes=[pltpu.VMEM(pair_stat, F32), pltpu.VMEM(pair_stat, F32), pltpu.VMEM((blk, LANES), F32),
                        pltpu.VMEM(pair_stat, F32), pltpu.VMEM(pair_stat, BF16), pltpu.VMEM(pair_stat, F32),
                        pltpu.VMEM((2, blk, blk), F32), pltpu.VMEM((2, blk, blk), BF16)],
    )
    return pl.pallas_call(
        kern, name=name, grid_spec=grid_spec,
        out_shape=[jax.ShapeDtypeStruct((n_tok, FOX_W), BF16), jax.ShapeDtypeStruct((n_pair, 2, n_tok), F32)],
        compiler_params=_params(2),
    )(it, jt, *tables, qkv, qkv, qkv, c2, ct2)


def _flash_delta(name, do, o):
    n_tok, tm = do.shape[0], min(ROW_BLOCK, do.shape[0])

    def body(i, n, do_ref, o_ref, d_ref):
        prod = do_ref[...].astype(F32) * o_ref[...].astype(F32)
        head = (lax.broadcasted_iota(jnp.int32, (FOX_W, LANES), 0) // FOX_D
                == lax.broadcasted_iota(jnp.int32, (FOX_W, LANES), 1)).astype(BF16)
        hi = prod.astype(BF16)
        r1 = prod - hi.astype(F32)
        mid = r1.astype(BF16)
        lo = (r1 - mid.astype(F32)).astype(BF16)
        d_ref[...] = _dot(hi, head) + _dot(mid, head) + _dot(lo, head)

    return _rowcall(name, body, n_tok, tm, [do, o], [], [], [(LANES, F32)], [])[0]


def _flash_bwd(name, qkv, do, r_row, d_row, c2, tables):
    n_tok = qkv.shape[0]
    blk = min(FLASH_BLOCK, n_tok)
    nb = n_tok // blk
    n_pair = FOX_H // 2
    jt, it = _tri_table(nb, False)

    def kern(jt_ref, it_ref, qn_ref, kn_ref, dg_ref, cf_ref, cl_ref, k_ref, v_ref, q_ref, do_ref, rr_ref, dr_ref, c_ref,
             dq_ref, dk_ref, dv_ref, dc_ref, km_s, vm_s, ck_s, dck_s):
        n = pl.program_id(1)
        j, i = jt_ref[n], it_ref[n]
        first = _pair_half(blk) == 0
        live = _block_is_live((qn_ref, kn_ref, dg_ref, cf_ref, cl_ref), pl.program_id(0), i, j)

        @pl.when(n == 0)
        def _():
            dq_ref[...] = jnp.zeros(dq_ref.shape, F32)
            dc_ref[...] = jnp.zeros(dc_ref.shape, F32)

        @pl.when(i == j)
        def _():
            dk_ref[...] = jnp.zeros(dk_ref.shape, F32)
            dv_ref[...] = jnp.zeros(dv_ref.shape, F32)
            dck_s[...] = jnp.zeros(dck_s.shape, F32)
            for e in range(2):
                mine = first if e == 0 else jnp.logical_not(first)
                km_s[e] = jnp.where(mine, k_ref[...], jnp.zeros_like(k_ref[...]))
                vm_s[e] = jnp.where(mine, v_ref[...], jnp.zeros_like(v_ref[...]))
                ck_s[e] = _lane_pick(c_ref[...], 2 * pl.program_id(0) + e)

        def step(heads, on_diagonal):
            q_v, do_v, k_v = q_ref[...], do_ref[...], k_ref[...]
            dvs, dks, dqs, sums = {}, {}, {}, {}
            for e in heads:
                t = _dot_nt(km_s[e], q_v) * (FOX_SCALE * LOG2E) - _wide(ck_s[e], blk) - rr_ref[e]
                if on_diagonal:
                    keep = (lax.broadcasted_iota(jnp.int32, (blk, blk), 1)
                            >= lax.broadcasted_iota(jnp.int32, (blk, blk), 0))
                    t = jnp.where(keep, t, -jnp.inf)
                p = jnp.exp2(t)
                ds = p * (_dot_nt(vm_s[e], do_v) - dr_ref[e])
                p_b, ds_b = p.astype(BF16), ds.astype(BF16)
                dvs[e] = _dot(p_b, do_v)
                dks[e] = _dot(ds_b, q_v)
                dqs[e] = _dot_tn(ds_b, k_v)
                dck_s[e] += jnp.broadcast_to(jnp.sum(ds, axis=1, keepdims=True), (blk, LANES))
                sums[e] = jnp.sum(ds, axis=0, keepdims=True)
            rows = pl.ds(pl.multiple_of(i * blk, blk), blk)

            def merged(parts):
                zero = jnp.zeros((blk, LANES), F32)
                return jnp.where(first, parts.get(0, zero), parts.get(1, zero))

            dv_ref[...] += merged(dvs)
            dk_ref[...] += merged(dks) * FOX_SCALE
            dq_ref[rows, :] += merged(dqs) * FOX_SCALE
            none = jnp.zeros((1, blk), F32)
            dc_ref[0, :, rows] += jnp.concatenate([sums.get(0, none), sums.get(1, none)], axis=0)

        above = i > j

        @pl.when(above & live[0] & live[1])
        def _():
            step((0, 1), False)

        @pl.when(above & live[0] & jnp.logical_not(live[1]))
        def _():
            step((0,), False)

        @pl.when(above & jnp.logical_not(live[0]) & live[1])
        def _():
            step((1,), False)

        @pl.when(i == j)
        def _():
            step((0, 1), True)

        @pl.when(i == nb - 1)
        def _():
            keys = pl.ds(pl.multiple_of(j * blk, blk), blk)
            dc_ref[0, :, keys] -= jnp.concatenate([_col_to_row(dck_s[e]) for e in range(2)], axis=0)

    def query_block(p, n, jt_r, it_r, tabs):
        j, i = jt_r[n], it_r[n]
        live = _block_is_live(tabs, p, i, j)
        return jnp.where(live[0] | live[1] | (i == j), i, nb - 1)

    def col_block(first_col, inner):
        def index(p, n, jt_r, it_r, *tabs):
            return ((query_block(p, n, jt_r, it_r, tabs) if inner else jt_r[n]), first_col + p)
        return pl.BlockSpec((blk, LANES), index)

    def row_stat():
        return pl.BlockSpec((2, 1, blk), lambda p, n, jt_r, it_r, *tabs: (p, 0, query_block(p, n, jt_r, it_r, tabs)))

    pair_stat = (2, blk, LANES)
    grid_spec = pltpu.PrefetchScalarGridSpec(
        num_scalar_prefetch=2 + len(tables), grid=(n_pair, int(jt.shape[0])),
        in_specs=[col_block(n_pair, False), col_block(2 * n_pair, False), col_block(0, True), col_block(0, True),
                  row_stat(), row_stat(), pl.BlockSpec((blk, LANES), lambda p, n, jt_r, it_r, *_: (jt_r[n], 0))],
        out_specs=[pl.BlockSpec((n_tok, LANES), lambda p, n, jt_r, it_r, *_: (0, p)),
                   col_block(0, False), col_block(0, False),
                   pl.BlockSpec((1, 2, n_tok), lambda p, n, jt_r, it_r, *_: (p, 0, 0))],
        scratch_shapes=[pltpu.VMEM(pair_stat, BF16), pltpu.VMEM(pair_stat, BF16), pltpu.VMEM(pair_stat, F32),
                        pltpu.VMEM(pair_stat, F32)],
    )
    wide = jax.ShapeDtypeStruct((n_tok, FOX_W), F32)
    return pl.pallas_call(
        kern, name=name, grid_spec=grid_spec,
        out_shape=[wide, wide, wide, jax.ShapeDtypeStruct((n_pair, 2, n_tok), F32)],
        compiler_params=_params(2),
    )(jt, it, *tables, qkv, qkv, qkv, do, r_row, d_row, c2)


def _merge_fwd(name, x, ya, yb, yc, gl, wb, wo):
    n_tok, tm = x.shape[0], min(ROW_BLOCK, x.shape[0])

    def body(i, n, x_ref, ya_ref, yb_ref, yc_ref, gl_ref, wb_ref, wo_ref, xo_ref):
        merged = jnp.zeros((tm, D_MODEL), F32)
        for k, y_ref in enumerate((ya_ref, yb_ref, yc_ref)):
            gate = jax.nn.sigmoid(gl_ref[:, k * D_MODEL:(k + 1) * D_MODEL].astype(F32))
            merged = merged + gate * _dot(y_ref[...], wb_ref[k])
        xo_ref[...] = x_ref[...] + _dot(merged.astype(BF16), wo_ref[...])

    return _rowcall(name, body, n_tok, tm, [x, ya, yb, yc, gl], [], [wb, wo], [(D_MODEL, F32)], [])[0]


def _merge_bwd(name, dxo, ya, yb, yc, gl, wb, wo):
    n_tok, tm = dxo.shape[0], min(ROW_BLOCK, dxo.shape[0])

    def body(i, n, dxo_ref, ya_ref, yb_ref, yc_ref, gl_ref, wb_ref, wo_ref,
             mg_ref, dob_ref, dgl_ref, dbr_ref, dya_ref, dyb_ref, dyc_ref):
        dob = dxo_ref[...].astype(BF16)
        dob_ref[...] = dob
        dm = _dot_nt(dob, wo_ref[...])
        merged = jnp.zeros((tm, D_MODEL), F32)
        for k, (y_ref, dy_ref) in enumerate(((ya_ref, dya_ref), (yb_ref, dyb_ref), (yc_ref, dyc_ref))):
            cols = slice(k * D_MODEL, (k + 1) * D_MODEL)
            gate = jax.nn.sigmoid(gl_ref[:, cols].astype(F32))
            br = _dot(y_ref[...], wb_ref[k])
            merged = merged + gate * br
            dgl_ref[:, cols] = (dm * br * gate * (1.0 - gate)).astype(BF16)
            dbr = (dm * gate).astype(BF16)
            dbr_ref[:, cols] = dbr
            dy_ref[...] = _dot_nt(dbr, wb_ref[k]).astype(BF16)
        mg_ref[...] = merged.astype(BF16)

    return _rowcall(name, body, n_tok, tm, [dxo, ya, yb, yc, gl], [], [wb, wo],
                    [(D_MODEL, BF16), (D_MODEL, BF16), (3 * D_MODEL, BF16), (3 * D_MODEL, BF16),
                     (CONV_W, BF16), (SG_W, BF16), (FOX_W, BF16)], [])


def _mem_fwd(name, mem, gain, wk, wv):
    n_mem = mem.shape[0]

    def kern(m_ref, g_ref, wk_ref, wv_ref, mn_ref, k_ref, v_ref):
        mh, _ = _rms_stats(m_ref[...])
        mn = (mh * g_ref[...]).astype(BF16)
        mn_ref[...] = mn
        k_ref[...] = _dot(mn, wk_ref[...]).astype(BF16)
        v_ref[...] = _dot(mn, wv_ref[...]).astype(BF16)

    shp = jax.ShapeDtypeStruct((n_mem, D_MODEL), BF16)
    return pl.pallas_call(
        kern, name=name, grid=(1,),
        in_specs=[_full_spec(mem.shape), _full_spec(gain.shape), _full_spec(wk.shape), _full_spec(wv.shape)],
        out_specs=[_full_spec(shp.shape)] * 3, out_shape=[shp] * 3, compiler_params=_params(1),
    )(mem, gain, wk, wv)


def _mem_bwd(name, mem, gain, dkx, dvx, wk, wv):
    n_mem = mem.shape[0]

    def kern(m_ref, g_ref, dk_ref, dv_ref, wk_ref, wv_ref, dkb_ref, dvb_ref, dg_ref):
        mh, _ = _rms_stats(m_ref[...])
        dkb, dvb = dk_ref[...].astype(BF16), dv_ref[...].astype(BF16)
        dkb_ref[...] = dkb
        dvb_ref[...] = dvb
        dm = _dot_nt(dkb, wk_ref[...]) + _dot_nt(dvb, wv_ref[...])
        dg_ref[...] = jnp.sum(dm * mh, axis=0, keepdims=True)

    shp = jax.ShapeDtypeStruct((n_mem, D_MODEL), BF16)
    args = (mem, gain, dkx, dvx, wk, wv)
    return pl.pallas_call(
        kern, name=name, grid=(1,), in_specs=[_full_spec(a.shape) for a in args],
        out_specs=[_full_spec(shp.shape)] * 2 + [_full_spec((1, D_MODEL))],
        out_shape=[shp, shp, jax.ShapeDtypeStruct((1, D_MODEL), F32)], compiler_params=_params(1),
    )(*args)


def _xa_probs(q_b, kx_ref, hd):
    cols = slice(hd * XA_D, (hd + 1) * XA_D)
    s = _dot_nt(q_b[:, cols], kx_ref[:, cols]) * (XA_D ** -0.5)
    e = jnp.exp(s - jnp.max(s, axis=1, keepdims=True))
    return e / jnp.sum(e, axis=1, keepdims=True)


def _xa_fwd(name, x, gain, kx, vx, wq, wo):
    n_tok, tm = x.shape[0], min(ROW_BLOCK, x.shape[0])

    def body(i, n, x_ref, g_ref, kx_ref, vx_ref, wq_ref, wo_ref, xo_ref):
        x_v = x_ref[...]
        xh, _ = _rms_stats(x_v)
        q_b = _dot((xh * g_ref[...]).astype(BF16), wq_ref[...]).astype(BF16)
        o = jnp.concatenate(
            [_dot(_xa_probs(q_b, kx_ref, hd).astype(BF16), vx_ref[:, hd * XA_D:(hd + 1) * XA_D]) for hd in range(XA_H)],
            axis=1)
        xo_ref[...] = x_v + _dot(o.astype(BF16), wo_ref[...])

    return _rowcall(name, body, n_tok, tm, [x], [gain, kx, vx], [wq, wo], [(D_MODEL, F32)], [])[0]


def _xa_bwd(name, x, dxo, gain, kx, vx, wq, wo):
    n_tok, tm = x.shape[0], min(ROW_BLOCK, x.shape[0])
    n_mem = kx.shape[0]

    def body(i, n, x_ref, dxo_ref, g_ref, kx_ref, vx_ref, wq_ref, wo_ref,
             dx_ref, h_ref, o_ref, dq_ref, dy_ref, dkx_ref, dvx_ref, dg_ref):
        g = g_ref[...]
        xh, r = _rms_stats(x_ref[...])
        h = (xh * g).astype(BF16)
        h_ref[...] = h
        q_b = _dot(h, wq_ref[...]).astype(BF16)
        dxo_v = dxo_ref[...]
        dy = dxo_v.astype(BF16)
        dy_ref[...] = dy
        do = _dot_nt(dy, wo_ref[...])
        for hd in range(XA_H):
            cols = slice(hd * XA_D, (hd + 1) * XA_D)
            p = _xa_probs(q_b, kx_ref, hd)
            p_b = p.astype(BF16)
            o_ref[:, cols] = _dot(p_b, vx_ref[:, cols]).astype(BF16)
            do_h = do[:, cols].astype(BF16)
            dvx_ref[:, cols] += _dot_tn(p_b, do_h)
            dp = _dot_nt(do_h, vx_ref[:, cols])
            ds = p * (dp - jnp.sum(dp * p, axis=1, keepdims=True))
            ds_b = (ds * (XA_D ** -0.5)).astype(BF16)
            dq_ref[:, cols] = _dot(ds_b, kx_ref[:, cols]).astype(BF16)
            dkx_ref[:, cols] += _dot_tn(ds_b, q_b[:, cols])
        dh = _dot_nt(dq_ref[...], wq_ref[...])
        dx, dg = _rms_bwd(xh, r, g, dh)
        dx_ref[...] = dxo_v + dx
        dg_ref[...] += dg

    return _rowcall(name, body, n_tok, tm, [x, dxo], [gain, kx, vx], [wq, wo],
                    [(D_MODEL, F32), (D_MODEL, BF16), (D_MODEL, BF16), (D_MODEL, BF16), (D_MODEL, BF16)],
                    [((n_mem, D_MODEL), F32), ((n_mem, D_MODEL), F32), ((1, D_MODEL), F32)])


def _loss_head(name, x, target, gain):
    n_tok, tm = x.shape[0], min(ROW_BLOCK, x.shape[0])

    def body(i, n, x_ref, t_ref, g_ref, dx_ref, loss_ref, dg_ref):
        g = g_ref[...]
        xh, r = _rms_stats(x_ref[...])
        err = xh * g - t_ref[...]
        loss_ref[...] += 0.5 * jnp.sum(jnp.sum(err * err, axis=1, keepdims=True) / D_MODEL, axis=0, keepdims=True)
        dx, dg = _rms_bwd(xh, r, g, err / D_MODEL)
        dx_ref[...] = dx
        dg_ref[...] += dg

    return _rowcall(name, body, n_tok, tm, [x, target], [gain], [], [(D_MODEL, F32)],
                    [((8, LANES), F32), ((1, D_MODEL), F32)])


def _adamw(name, w, g, m, v):
    rows, cols = w.shape
    tr = 128 if rows % 128 == 0 else rows

    def kern(w_ref, g_ref, m_ref, v_ref, d_ref, mo_ref, vo_ref):
        g_v = g_ref[...]
        m_new = ADAM_B1 * m_ref[...] + (1.0 - ADAM_B1) * g_v
        v_new = ADAM_B2 * v_ref[...] + (1.0 - ADAM_B2) * (g_v * g_v)
        m_hat = m_new / (1.0 - ADAM_B1 ** ADAM_STEP)
        v_hat = v_new / (1.0 - ADAM_B2 ** ADAM_STEP)
        d_ref[...] = -ADAM_LR * (m_hat / (jnp.sqrt(v_hat) + ADAM_EPS) + ADAM_WD * w_ref[...])
        mo_ref[...] = m_new
        vo_ref[...] = v_new

    spec = pl.BlockSpec((tr, cols), lambda i: (i, 0))
    shp = jax.ShapeDtypeStruct((rows, cols), F32)
    return pl.pallas_call(kern, name=name, grid=(rows // tr,), in_specs=[spec] * 4, out_specs=[spec] * 3,
                          out_shape=[shp] * 3, compiler_params=_params(1))(w, g, m, v)


def _add_pair(name, gp, got, core):
    _, n, rows, cols = gp.shape
    spec = pl.BlockSpec((1, PACK_ROW_BLOCK, cols), lambda j, r, core_ref: (j, r, 0))
    mine = pl.BlockSpec((1, PACK_ROW_BLOCK, cols), lambda j, r, core_ref: (core_ref[0] * n + j, r, 0))

    def kern(core_ref, a_ref, b_ref, o_ref):
        o_ref[...] = (a_ref[...].astype(F32) + b_ref[...].astype(F32)).astype(BF16)

    grid_spec = pltpu.PrefetchScalarGridSpec(num_scalar_prefetch=1, grid=(n, rows // PACK_ROW_BLOCK),
                                             in_specs=[mine, spec], out_specs=spec)
    return pl.pallas_call(kern, name=name, grid_spec=grid_spec, out_shape=jax.ShapeDtypeStruct(got.shape, BF16),
                          compiler_params=_params(2))(core, gp.reshape(2 * n, rows, cols), got)


def _sum_parts(name, s, got, chip):
    _, rows, cols = s.shape
    tr = PACK_ROW_BLOCK

    def kern(chip_ref, s_ref, g_ref, o_ref):
        acc = s_ref[0].astype(F32)
        for k in range(3):
            acc = acc + g_ref[k].astype(F32)
        o_ref[...] = acc

    grid_spec = pltpu.PrefetchScalarGridSpec(
        num_scalar_prefetch=1, grid=(rows // tr,),
        in_specs=[pl.BlockSpec((1, tr, cols), lambda r, chip_ref: (chip_ref[0], r, 0)),
                  pl.BlockSpec((3, tr, cols), lambda r, chip_ref: (0, r, 0))],
        out_specs=pl.BlockSpec((tr, cols), lambda r, chip_ref: (r, 0)))
    return pl.pallas_call(kern, name=name, grid_spec=grid_spec, out_shape=jax.ShapeDtypeStruct((rows, cols), F32),
                          compiler_params=_params(1))(chip, s, got)


def _sum_slots(name, a, tr):
    n, rows, cols = a.shape

    def kern(a_ref, o_ref):
        acc = a_ref[0].astype(F32)
        for k in range(1, n):
            acc = acc + a_ref[k].astype(F32)
        o_ref[...] = acc

    return pl.pallas_call(kern, name=name, grid=(rows // tr,),
                          in_specs=[pl.BlockSpec((n, tr, cols), lambda r: (0, r, 0))],
                          out_specs=pl.BlockSpec((tr, cols), lambda r: (r, 0)),
                          out_shape=jax.ShapeDtypeStruct((rows, cols), F32), compiler_params=_params(1))(a)


_ANY = pl.BlockSpec(memory_space=pl.ANY)


COMM_CHUNKS = 13


def _place():
    x, y, c = lax.axis_index("x"), lax.axis_index("y"), lax.axis_index("c")
    chips = [(1 - x, y), (x, 1 - y), (1 - x, 1 - y)]
    return x, y, c, chips


def _rows(ref, q, n):
    step = ref.shape[0] // n
    return ref.at[pl.ds(q * step, step)]


def _gather_chips(name, w):
    n = COMM_CHUNKS

    def kern(w_ref, o_ref, send_sems, recv_sems):
        x, y, c, chips = _place()
        me, sib = 2 * x + y, (x, y, 1 - c)

        def copy(k, src, dst, to):
            return pltpu.make_async_remote_copy(src_ref=src, dst_ref=dst, send_sem=send_sems.at[k],
                                                recv_sem=recv_sems.at[k], device_id=to, device_id_type=MESH_ID)

        first = [copy(k * n + q, _rows(w_ref.at[c], q, n), _rows(o_ref.at[me, c], q, n), (cx, cy, c))
                 for k, (cx, cy) in enumerate(chips) for q in range(n)]
        for cp in first:
            cp.start()
        passed = []
        for k, (cx, cy) in enumerate(chips):
            for q in range(n):
                landed = _rows(o_ref.at[2 * cx + cy, c], q, n)
                copy(k * n + q, landed, landed, (x, y, c)).wait_recv()
                passed.append(copy((3 + k) * n + q, landed, landed, sib))
                passed[-1].start()
        for k, (cx, cy) in enumerate(chips):
            for q in range(n):
                theirs = _rows(o_ref.at[2 * cx + cy, 1 - c], q, n)
                copy((3 + k) * n + q, theirs, theirs, (x, y, c)).wait_recv()
        for cp in first + passed:
            cp.wait_send()

    return pl.pallas_call(
        kern, name=name, in_specs=[_ANY], out_specs=_ANY,
        out_shape=jax.ShapeDtypeStruct((N_CHIPS,) + w.shape, w.dtype),
        scratch_shapes=[pltpu.SemaphoreType.DMA((6 * n,)), pltpu.SemaphoreType.DMA((6 * n,))],
        compiler_params=pltpu.CompilerParams(has_side_effects=True),
    )(w)


def _gather_all(name, v):
    def kern(v_ref, o_ref, send_sems, recv_sems, local_sem):
        x, y, c, _ = _place()
        me = 4 * x + 2 * y + c
        mine = pltpu.make_async_copy(v_ref, o_ref.at[me], local_sem)
        mine.start()
        peers = []
        for k in range(1, 8):
            px = 1 - x if k & 4 else x
            py = 1 - y if k & 2 else y
            pc = 1 - c if k & 1 else c
            peers.append((px, py, pc))
        copies = [pltpu.make_async_remote_copy(src_ref=v_ref, dst_ref=o_ref.at[me], send_sem=send_sems.at[k],
                                               recv_sem=recv_sems.at[k], device_id=p, device_id_type=MESH_ID)
                  for k, p in enumerate(peers)]
        for cp in copies:
            cp.start()
        for k, (px, py, pc) in enumerate(peers):
            pltpu.make_async_remote_copy(src_ref=v_ref, dst_ref=o_ref.at[4 * px + 2 * py + pc], send_sem=send_sems.at[k],
                                         recv_sem=recv_sems.at[k], device_id=(x, y, c), device_id_type=MESH_ID).wait_recv()
        for cp in copies:
            cp.wait_send()
        mine.wait()

    return pl.pallas_call(
        kern, name=name, in_specs=[_ANY], out_specs=_ANY, out_shape=jax.ShapeDtypeStruct((8,) + v.shape, v.dtype),
        scratch_shapes=[pltpu.SemaphoreType.DMA((7,)), pltpu.SemaphoreType.DMA((7,)), pltpu.SemaphoreType.DMA(())],
        compiler_params=pltpu.CompilerParams(has_side_effects=True),
    )(v)


def _swap_halves(name, gp):
    n = COMM_CHUNKS
    n_slots = gp.shape[1]

    def kern(g_ref, got_ref, send_sems, recv_sems):
        x, y, c, _ = _place()
        swaps = [pltpu.make_async_remote_copy(src_ref=_rows(g_ref.at[1 - c, j], q, n), dst_ref=_rows(got_ref.at[j], q, n),
                                              send_sem=send_sems.at[j * n + q], recv_sem=recv_sems.at[j * n + q],
                                              device_id=(x, y, 1 - c), device_id_type=MESH_ID)
                 for j in range(n_slots) for q in range(n)]
        for cp in swaps:
            cp.start()
        for cp in swaps:
            cp.wait()

    return pl.pallas_call(
        kern, name=name, in_specs=[_ANY], out_specs=_ANY, out_shape=jax.ShapeDtypeStruct(gp.shape[1:], gp.dtype),
        scratch_shapes=[pltpu.SemaphoreType.DMA((n_slots * n,)), pltpu.SemaphoreType.DMA((n_slots * n,))],
        compiler_params=pltpu.CompilerParams(has_side_effects=True),
    )(gp)


def _scatter_chips(name, s):
    n = COMM_CHUNKS

    def kern(s_ref, o_ref, send_sems, recv_sems):
        x, y, c, chips = _place()
        copies = [pltpu.make_async_remote_copy(src_ref=_rows(s_ref.at[2 * cx + cy], q, n), dst_ref=_rows(o_ref.at[k], q, n),
                                               send_sem=send_sems.at[k * n + q], recv_sem=recv_sems.at[k * n + q],
                                               device_id=(cx, cy, c), device_id_type=MESH_ID)
                  for k, (cx, cy) in enumerate(chips) for q in range(n)]
        for cp in copies:
            cp.start()
        for k in range(3):
            for q in range(n):
                landed = _rows(o_ref.at[k], q, n)
                pltpu.make_async_remote_copy(src_ref=landed, dst_ref=landed, send_sem=send_sems.at[k * n + q],
                                             recv_sem=recv_sems.at[k * n + q], device_id=(x, y, c),
                                             device_id_type=MESH_ID).wait_recv()
        for cp in copies:
            cp.wait_send()

    return pl.pallas_call(
        kern, name=name, in_specs=[_ANY], out_specs=_ANY, out_shape=jax.ShapeDtypeStruct((3,) + s.shape[1:], s.dtype),
        scratch_shapes=[pltpu.SemaphoreType.DMA((3 * n,)), pltpu.SemaphoreType.DMA((3 * n,))],
        compiler_params=pltpu.CompilerParams(has_side_effects=True),
    )(s)


def _join_halves(name, r):
    n = COMM_CHUNKS

    def kern(r_ref, o_ref, send_sems, recv_sems):
        x, y, c, _ = _place()
        swaps = [pltpu.make_async_remote_copy(src_ref=_rows(r_ref, q, n), dst_ref=_rows(o_ref, q, n),
                                              send_sem=send_sems.at[q], recv_sem=recv_sems.at[q],
                                              device_id=(x, y, 1 - c), device_id_type=MESH_ID) for q in range(n)]
        for cp in swaps:
            cp.start()
        for cp in swaps:
            cp.wait()

    return pl.pallas_call(
        kern, name=name, in_specs=[_ANY], out_specs=_ANY, out_shape=jax.ShapeDtypeStruct(r.shape, r.dtype),
        scratch_shapes=[pltpu.SemaphoreType.DMA((n,))] * 2, compiler_params=pltpu.CompilerParams(has_side_effects=True),
    )(r)


PACK_ALIGN = 16


def _exchange_rows(nm, shard_shape):
    if BIG_KIND[nm] == "cols":
        rows = shard_shape[-1]
    else:
        rows = 1
        for s in shard_shape[1:]:
            rows *= s
        rows //= PACK_COLS
    return -(-rows // PACK_ALIGN) * PACK_ALIGN


def _to_exchange(nm, a, transposed=False):
    if BIG_KIND[nm] == "cols":
        if not transposed:
            a = jnp.swapaxes(a, 1, 2)
        rows = -(-a.shape[1] // PACK_ALIGN) * PACK_ALIGN
        a = jnp.pad(a, ((0, 0), (0, rows - a.shape[1]), (0, 0)))
        return a.reshape(a.shape[0] * rows, PACK_COLS)
    return a.reshape(a.shape[0] * _exchange_rows(nm, a.shape), PACK_COLS)


def _from_exchange(nm, rows2d, shard_shape):
    lead = rows2d.shape[:-2]
    n_layers = shard_shape[0]
    if BIG_KIND[nm] == "cols":
        a = rows2d.reshape(lead + (n_layers, -1, PACK_COLS))
        return lax.slice_in_dim(a, 0, shard_shape[-1], axis=a.ndim - 2)
    return rows2d.reshape(lead + tuple(shard_shape))


def _pack(pieces, dtype):
    slab = jnp.concatenate([p.astype(dtype) for p in pieces])
    half = -(-slab.shape[0] // (2 * PACK_ROW_BLOCK)) * PACK_ROW_BLOCK
    slab = jnp.pad(slab, ((0, 2 * half - slab.shape[0]), (0, 0)))
    return slab.reshape(2, half, PACK_COLS)


def _unpack(packed, row_counts):
    lead = packed.shape[:-3]
    slab = packed.reshape(lead + (-1, PACK_COLS))
    out, off = [], 0
    for rows in row_counts:
        out.append(lax.slice_in_dim(slab, off, off + rows, axis=slab.ndim - 2))
        off += rows
    return out


def _t(a):
    return jnp.swapaxes(a, -1, -2)


def _rows_from_lanes(a):
    return a[:, :FOX_H].T[:, None, :]


def _lanes_from_heads(a):
    return jnp.pad(a[:, 0, :].T, ((0, 0), (0, LANES - FOX_H)))


def _layer_weights(wts, small, l):
    w_in_t = wts["w_in"][l]
    w = {
        "wf_t": jnp.pad(w_in_t[W1_COLS:W1_COLS + FOX_H], ((0, LANES - FOX_H), (0, 0))),
        "w1_t": w_in_t[:W1_COLS], "wgl_t": w_in_t[W1_COLS + FOX_H:],
    }
    for nm, _ in BIG:
        if nm != "w_in":
            w[nm + "_t" if BIG_KIND[nm] == "cols" else nm] = wts[nm][l]
    for nm in ("ffn1_norm", "mix_norm", "sg_norm", "xa_norm", "mem_norm", "ffn2_norm"):
        w[nm] = small[nm][l][None, :]
    w["conv_w"] = jnp.pad(small["conv_w"][l], ((0, 5), (0, 0)))
    w["sg_w"] = small["sg_w"][l]
    w["sg_w_t"] = _t(small["sg_w"][l])
    w["sg_bias"] = jnp.repeat(small["sg_b"][l].T, CHUNK, axis=1)
    w["fox_b"] = jnp.pad(small["fox_b_f"][l][None, :], ((0, 0), (0, LANES - FOX_H)))
    return w


def _layer_fwd(l, x, mem, w):
    s = {"x0": x}
    x1, s["a1"], s["b1"] = _ffn_fwd(f"ffn1_fwd_{l}", x, w["ffn1_norm"], w["ffn1_w_gate_t"], w["ffn1_w_up_t"],
                                    w["ffn1_w_down"])
    s["x1"] = x1
    cin, sgin, qkv, fl, gl = _proj_fwd(f"proj_fwd_{l}", x1, w["mix_norm"], w["w1_t"], w["wf_t"], w["wgl_t"])
    ya = _conv_fwd(f"conv_fwd_{l}", cin, w["conv_w"])
    yb = _sg_fwd(f"sg_fwd_{l}", sgin, w["sg_norm"], w["sg_w"], w["sg_bias"])
    c2 = _forget_cumsum(f"forget_fwd_{l}", fl, w["fox_b"])
    tables = _prune_tables(_flash_bounds(f"flash_bounds_{l}", qkv), c2, min(FLASH_BLOCK, qkv.shape[0]))
    yc, lse2 = _flash_fwd(f"flash_fwd_{l}", qkv, c2, _rows_from_lanes(c2), tables)
    x2 = _merge_fwd(f"merge_fwd_{l}", x1, ya, yb, yc, gl, w["w_branch"], w["w_out"])
    s.update(cin=cin, sgin=sgin, qkv=qkv, fl=fl, gl=gl, ya=ya, yb=yb, yc=yc, c2=c2, lse2=lse2, x2=x2, tables=tables)
    s["mn"], s["kx"], s["vx"] = _mem_fwd(f"mem_fwd_{l}", mem, w["mem_norm"], w["xa_wk"], w["xa_wv"])
    x3 = _xa_fwd(f"xa_fwd_{l}", x2, w["xa_norm"], s["kx"], s["vx"], w["xa_wq"], w["xa_wo"])
    s["x3"] = x3
    x4, s["a2"], s["b2"] = _ffn_fwd(f"ffn2_fwd_{l}", x3, w["ffn2_norm"], w["ffn2_w_gate_t"], w["ffn2_w_up_t"],
                                    w["ffn2_w_down"])
    return x4, s


def _ffn_grads(tag, x, dxo, a, b, w, pre):
    dx, da, db, sv, h, dy, dg = _ffn_bwd(f"{pre}_bwd_{tag}", x, dxo, a, b, w[pre + "_norm"], w[pre + "_w_down"],
                                         w[pre + "_w_gate_t"], w[pre + "_w_up_t"])
    g = {
        pre + "_norm": dg[0],
        pre + "_w_gate": _mm_tn(f"{pre}_dwg_{tag}", da, h, D_FF, D_MODEL, tk=FF_SPLIT, tn=D_MODEL),
        pre + "_w_up": _mm_tn(f"{pre}_dwu_{tag}", db, h, D_FF, D_MODEL, tk=FF_SPLIT, tn=D_MODEL),
        pre + "_w_down": _mm_tn(f"{pre}_dwd_{tag}", sv, dy, D_FF, D_MODEL, tk=FF_SPLIT, tn=D_MODEL),
    }
    return dx, g


def _layer_bwd(l, dx, mem, w, s):
    g = {}
    dx, gf = _ffn_grads(l, s["x3"], dx, s["a2"], s["b2"], w, "ffn2")
    g.update(gf)

    dx, h, o, dq, dy, dkx, dvx, dg = _xa_bwd(f"xa_bwd_{l}", s["x2"], dx, w["xa_norm"], s["kx"], s["vx"], w["xa_wq"],
                                              w["xa_wo"])
    g["xa_norm"] = dg[0]
    g["xa_wq"] = _mm_tn(f"xa_dwq_{l}", h, dq, D_MODEL, D_MODEL, tk=D_MODEL, tn=D_MODEL)
    g["xa_wo"] = _mm_tn(f"xa_dwo_{l}", o, dy, D_MODEL, D_MODEL, tk=D_MODEL, tn=D_MODEL)
    dkb, dvb, dgm = _mem_bwd(f"mem_bwd_{l}", mem, w["mem_norm"], dkx, dvx, w["xa_wk"], w["xa_wv"])
    g["mem_norm"] = dgm[0]
    g["xa_wk"] = _mm_tn(f"xa_dwk_{l}", s["mn"], dkb, D_MODEL, D_MODEL, tk=D_MODEL, tn=D_MODEL)
    g["xa_wv"] = _mm_tn(f"xa_dwv_{l}", s["mn"], dvb, D_MODEL, D_MODEL, tk=D_MODEL, tn=D_MODEL)

    mg, dob, dgl, dbr, dya, dyb, dyc = _merge_bwd(f"merge_bwd_{l}", dx, s["ya"], s["yb"], s["yc"], s["gl"], w["w_branch"],
                                                  w["w_out"])
    g["w_out"] = _mm_tn(f"dwout_{l}", mg, dob, D_MODEL, D_MODEL, tk=D_MODEL, tn=D_MODEL)
    g["w_branch"] = jnp.stack([
        _mm_tn(f"dwbranch{k}_{l}", y, dbr, CONV_W, D_MODEL, tk=CONV_W, tn=D_MODEL, y_off=k * D_MODEL)
        for k, y in enumerate((s["ya"], s["yb"], s["yc"]))])

    dcin, dcw = _conv_bwd(f"conv_bwd_{l}", s["cin"], dya, w["conv_w"])
    g["conv_w"] = dcw[:3]
    dsg, dgn, dsw, dsb = _sg_bwd(f"sg_bwd_{l}", s["sgin"], dyb, w["sg_norm"], w["sg_w"], w["sg_w_t"], w["sg_bias"])
    g["sg_norm"], g["sg_w"], g["sg_b"] = dgn[0], dsw, dsb[:, :, 0]
    d_row = _rows_from_lanes(_flash_delta(f"flash_delta_{l}", dyc, s["yc"]))
    r_row = s["lse2"].reshape(FOX_H, 1, -1) - _rows_from_lanes(s["c2"])
    dq, dk, dv, dc_rows = _flash_bwd(f"flash_bwd_{l}", s["qkv"], dyc, r_row, d_row, s["c2"], s["tables"])
    dc = _lanes_from_heads(dc_rows.reshape(FOX_H, 1, -1))
    dfl, dbf = _forget_bwd(f"forget_bwd_{l}", dc, s["fl"], w["fox_b"])
    g["fox_b_f"] = dbf[0, :FOX_H]

    dx, h, dqkv, dfb, dg = _proj_bwd(f"proj_bwd_{l}", s["x1"], dx, dcin, dsg, dq, dk, dv, dfl, dgl, w["mix_norm"],
                                     w["w1_t"], w["wf_t"], w["wgl_t"])
    g["mix_norm"] = dg[0]
    c0, c1 = 3 * CONV_W, 3 * CONV_W + 2 * SG_W
    g["w_in"] = jnp.concatenate([
        _mm_tn(f"dwin_conv_{l}", dcin, h, c0, D_MODEL, tk=c0, tn=D_MODEL),
        _mm_tn(f"dwin_sg_{l}", dsg, h, c1 - c0, D_MODEL, tk=c1 - c0, tn=D_MODEL),
        _mm_tn(f"dwin_qkv_{l}", dqkv, h, 3 * FOX_W, D_MODEL, tk=3 * FOX_W, tn=D_MODEL),
        _mm_tn(f"dwin_f_{l}", dfb, h, LANES, D_MODEL, tk=LANES, tn=D_MODEL)[:FOX_H],
        _mm_tn(f"dwin_gl_{l}", dgl, h, 3 * D_MODEL, D_MODEL, tk=3 * D_MODEL // 2, tn=D_MODEL),
    ], axis=0)

    dx, gf = _ffn_grads(l, s["x0"], dx, s["a1"], s["b1"], w, "ffn1")
    g.update(gf)
    return dx, g


def _local_step(x, mem, target, wts, small):
    saved, lw = [], []
    for l in range(DEPTH):
        lw.append(_layer_weights(wts, small, l))
        x, s = _layer_fwd(l, x, mem, lw[l])
        saved.append(s)
    fin = small["final_norm"][None, :]
    dx, loss, dgf = _loss_head("loss_head", x, target, fin)
    layer_grads = [None] * DEPTH
    for l in reversed(range(DEPTH)):
        dx, layer_grads[l] = _layer_bwd(l, dx, mem, lw[l], saved[l])
    grads = {nm: jnp.stack([layer_grads[l][nm] for l in range(DEPTH)]) for nm in WEIGHTS if nm != "final_norm"}
    grads["final_norm"] = dgf[0]
    return loss[0, 0], dx, grads


def _small_slab(vals):
    rows = []
    for v in vals:
        flat = v.astype(F32).reshape(-1)
        n = -(-flat.shape[0] // LANES) * LANES
        rows.append(jnp.pad(flat, (0, n - flat.shape[0])).reshape(-1, LANES))
    slab = jnp.concatenate(rows)
    pad = -slab.shape[0] % 8
    return jnp.pad(slab, ((0, pad), (0, 0)))


def _small_unslab(slab, shapes):
    out, off = [], 0
    for shp in shapes:
        size = 1
        for s in shp:
            size *= s
        n_rows = -(-size // LANES)
        out.append(slab[off:off + n_rows].reshape(-1)[:size].reshape(shp))
        off += n_rows
    return out


def kernel(x, mem, ffn1_norm, ffn1_w_gate, ffn1_w_up, ffn1_w_down, mix_norm, w_in, conv_w, sg_norm, sg_w, sg_b, fox_b_f, w_branch, w_out, xa_norm, mem_norm, xa_wq, xa_wk, xa_wv, xa_wo, ffn2_norm, ffn2_w_gate, ffn2_w_up, ffn2_w_down, final_norm, loss_target, m_ffn1_norm, m_ffn1_w_gate, m_ffn1_w_up, m_ffn1_w_down, m_mix_norm, m_w_in, m_conv_w, m_sg_norm, m_sg_w, m_sg_b, m_fox_b_f, m_w_branch, m_w_out, m_xa_norm, m_mem_norm, m_xa_wq, m_xa_wk, m_xa_wv, m_xa_wo, m_ffn2_norm, m_ffn2_w_gate, m_ffn2_w_up, m_ffn2_w_down, m_final_norm, v_ffn1_norm, v_ffn1_w_gate, v_ffn1_w_up, v_ffn1_w_down, v_mix_norm, v_w_in, v_conv_w, v_sg_norm, v_sg_w, v_sg_b, v_fox_b_f, v_w_branch, v_w_out, v_xa_norm, v_mem_norm, v_xa_wq, v_xa_wk, v_xa_wv, v_xa_wo, v_ffn2_norm, v_ffn2_w_gate, v_ffn2_w_up, v_ffn2_w_down, v_final_norm):
    args = dict(locals())
    wv = {nm: args[nm] for nm in WEIGHTS}
    mv = {nm: args["m_" + nm] for nm in WEIGHTS}
    vv = {nm: args["v_" + nm] for nm in WEIGHTS}
    chip = 2 * lax.axis_index("x") + lax.axis_index("y")
    core = lax.axis_index("c")

    names = [nm for nm, _ in BIG]
    mine = {nm: _to_exchange(nm, wv[nm].astype(BF16)) for nm in names}
    row_counts = [mine[nm].shape[0] for nm in names]
    gathered = _gather_chips("gather_weights", _pack([mine[nm] for nm in names], BF16))
    wts = {}
    for nm, slab in zip(names, _unpack(gathered, row_counts)):
        p = _from_exchange(nm, jnp.where((jnp.arange(N_CHIPS) == chip)[:, None, None], mine[nm][None], slab), wv[nm].shape)
        if BIG_KIND[nm] == "branch":
            wts[nm] = jnp.concatenate([p[j] for j in range(N_CHIPS)], axis=3)
        else:
            wts[nm] = jnp.swapaxes(p, 0, 1).reshape(p.shape[1], N_CHIPS * p.shape[2], PACK_COLS)
    taps = _gather_all("gather_taps", _small_slab([conv_w]))
    taps = [_small_unslab(taps[2 * j], [conv_w.shape])[0] for j in range(N_CHIPS)]
    small = {nm: wv[nm] for nm in SMALL}
    small["conv_w"] = jnp.concatenate(taps, axis=2)

    loss, dx, grads = _local_step(x[0], mem[0], loss_target[0], wts, small)
    loss = lax.psum(loss, ("x", "y", "c"))

    small_shapes = [grads[nm].shape for nm in SMALL]
    slots = _gather_all("gather_small_grads", _small_slab([grads[nm] for nm in SMALL]))
    small_sum = _sum_slots("sum_small_grads", slots, slots.shape[1])
    small_g = dict(zip(SMALL, _small_unslab(small_sum, small_shapes)))
    taps_g = small_g["conv_w"]
    small_g["conv_w"] = lax.dynamic_slice_in_dim(taps_g, chip * conv_w.shape[2], conv_w.shape[2], axis=2)

    def grad_shard(nm, ax, j):
        g = grads[nm]
        ax = 1 if BIG_KIND[nm] == "cols" else ax
        size = g.shape[ax] // N_CHIPS
        return _to_exchange(nm, lax.slice_in_dim(g, j * size, (j + 1) * size, axis=ax), transposed=True)

    gp = jnp.stack([_pack([grad_shard(nm, ax, j) for nm, ax in BIG], BF16) for j in range(N_CHIPS)], axis=1)
    got = _swap_halves("reduce_swap_cores", gp)
    chip_sum = _add_pair("reduce_add_cores", gp, got, core.astype(jnp.int32).reshape(1))
    parts = _scatter_chips("reduce_scatter_chips", chip_sum)
    half = _sum_parts("reduce_add_chips", chip_sum, parts, chip.astype(jnp.int32).reshape(1))
    other = _join_halves("reduce_join_cores", half)
    full = jnp.where(core == 0, jnp.stack([half, other]), jnp.stack([other, half]))
    big_g = {}
    for nm, slab in zip(names, _unpack(full, row_counts)):
        g = _from_exchange(nm, slab, wv[nm].shape)
        big_g[nm] = jnp.swapaxes(g, 1, 2) if BIG_KIND[nm] == "cols" else g

    g_out = {**small_g, **big_g}
    delta, new_m, new_v = {}, {}, {}
    for nm, _ in BIG:
        shp = wv[nm].shape
        two_d = (-1, shp[-1])
        d, m2, v2 = _adamw("adamw_" + nm, wv[nm].reshape(two_d), g_out[nm].reshape(two_d), mv[nm].reshape(two_d),
                           vv[nm].reshape(two_d))
        delta[nm], new_m[nm], new_v[nm] = d.reshape(shp), m2.reshape(shp), v2.reshape(shp)
    slab_shapes = [wv[nm].shape for nm in SMALL]
    d, m2, v2 = _adamw("adamw_small", _small_slab([wv[nm] for nm in SMALL]), _small_slab([g_out[nm] for nm in SMALL]),
                       _small_slab([mv[nm] for nm in SMALL]), _small_slab([vv[nm] for nm in SMALL]))
    for out, slab in ((delta, d), (new_m, m2), (new_v, v2)):
        out.update(zip(SMALL, _small_unslab(slab, slab_shapes)))

    return (loss, dx[None], *[g_out[nm] for nm in WEIGHTS], *[delta[nm] for nm in WEIGHTS],
            *[new_m[nm] for nm in WEIGHTS], *[new_v[nm] for nm in WEIGHTS])
```

```python
import functools

import jax
import jax.numpy as jnp
from jax import lax
from jax.experimental import pallas as pl
from jax.experimental.pallas import tpu as pltpu

F32, BF16 = jnp.float32, jnp.bfloat16
MESH_ID = pl.DeviceIdType.MESH

D_MODEL = 1024
DEPTH = 2
D_FF = 2816
CONV_W = 512
SG_W = 512
SG_G = 4
CHUNK = 128
FOX_H = 8
FOX_D = 64
FOX_W = FOX_H * FOX_D
FOX_SCALE = FOX_D ** -0.5
LOG2E = 1.4426950408889634
XA_H = 4
XA_D = D_MODEL // XA_H
N_CHIPS = 4
RMS_EPS = 1e-6
W1_COLS = 3 * CONV_W + 2 * SG_W + 3 * FOX_W
LANES = 128
HALO = 16

ADAM_LR, ADAM_B1, ADAM_B2, ADAM_EPS, ADAM_WD, ADAM_STEP = 0.001, 0.9, 0.999, 1e-08, 0.01, 10

ROW_BLOCK = 512
FFN_BWD_ROW_BLOCK = 256
MM_TT = 2048
FLASH_BLOCK = 512
FLASH_STRIP = 32
PRUNE_LOG2 = 40.0
FF_SPLIT = 1408
VMEM_LIMIT = 56 * 1024 * 1024
PACK_COLS = 1024
PACK_ROW_BLOCK = 208

BIG = (
    ("ffn1_w_gate", 2), ("ffn1_w_up", 2), ("ffn1_w_down", 1), ("w_in", 2), ("w_branch", 3), ("w_out", 1),
    ("xa_wq", 1), ("xa_wk", 1), ("xa_wv", 1), ("xa_wo", 1), ("ffn2_w_gate", 2), ("ffn2_w_up", 2), ("ffn2_w_down", 1),
)
BIG_KIND = {nm: ("branch" if nm == "w_branch" else "cols" if ax == 2 else "rows") for nm, ax in BIG}
SMALL = ("ffn1_norm", "mix_norm", "conv_w", "sg_norm", "sg_w", "sg_b", "fox_b_f", "xa_norm", "mem_norm", "ffn2_norm",
         "final_norm")
WEIGHTS = ("ffn1_norm", "ffn1_w_gate", "ffn1_w_up", "ffn1_w_down", "mix_norm", "w_in", "conv_w", "sg_norm", "sg_w",
           "sg_b", "fox_b_f", "w_branch", "w_out", "xa_norm", "mem_norm", "xa_wq", "xa_wk", "xa_wv", "xa_wo",
           "ffn2_norm", "ffn2_w_gate", "ffn2_w_up", "ffn2_w_down", "final_norm")


def _dot(a, b):
    return jnp.dot(a, b, preferred_element_type=F32)


def _dot_nt(a, b):
    return lax.dot_general(a, b, (((1,), (1,)), ((), ())), preferred_element_type=F32)


def _dot_tn(a, b):
    return lax.dot_general(a, b, (((0,), (0,)), ((), ())), preferred_element_type=F32)


def _rms_stats(x):
    r = lax.rsqrt(jnp.mean(x * x, axis=-1, keepdims=True) + RMS_EPS)
    return x * r, r


def _rms_bwd(xh, r, g, dy):
    dg = jnp.sum(dy * xh, axis=0, keepdims=True)
    dxh = dy * g
    dx = r * (dxh - xh * jnp.mean(dxh * xh, axis=-1, keepdims=True))
    return dx, dg


def _gelu(x):
    k = 0.7978845608028654
    t = jnp.tanh(k * (x + 0.044715 * x * x * x))
    return 0.5 * x * (1.0 + t), t


def _gelu_grad(x, t):
    k = 0.7978845608028654
    return 0.5 * (1.0 + t) + 0.5 * x * (1.0 - t * t) * k * (1.0 + 3.0 * 0.044715 * x * x)


def _split3_dot(tri, x):
    hi = x.astype(BF16)
    r1 = x - hi.astype(F32)
    mid = r1.astype(BF16)
    lo = (r1 - mid.astype(F32)).astype(BF16)
    return _dot(tri, hi) + _dot(tri, mid) + _dot(tri, lo)


def _params(n_grid):
    return pltpu.CompilerParams(dimension_semantics=("arbitrary",) * n_grid, vmem_limit_bytes=VMEM_LIMIT)


def _full_spec(shape):
    nd = len(shape)
    return pl.BlockSpec(tuple(shape), lambda *_: (0,) * nd)


def _rowcall(name, body, n_tokens, tm, rows, consts, residents, row_outs, acc_outs, scratch=()):
    n = n_tokens // tm
    rows = [r if isinstance(r, tuple) else (r, pl.BlockSpec((tm, r.shape[1]), lambda i: (i, 0))) for r in rows]
    n_in, n_w = len(rows) + len(consts), len(residents)
    n_out = len(row_outs) + len(acc_outs)
    in_specs = ([r[1] for r in rows] + [_full_spec(c.shape) for c in consts]
                + [pl.BlockSpec(memory_space=pl.ANY)] * n_w)
    out_shape = ([jax.ShapeDtypeStruct((n_tokens, c), dt) for c, dt in row_outs]
                 + [jax.ShapeDtypeStruct(s, dt) for s, dt in acc_outs])
    out_specs = ([pl.BlockSpec((tm, c), lambda i: (i, 0)) for c, _ in row_outs]
                 + [_full_spec(s) for s, _ in acc_outs])
    scratch_shapes = [pltpu.VMEM(w.shape, w.dtype) for w in residents]
    if n_w:
        scratch_shapes.append(pltpu.SemaphoreType.DMA((n_w,)))
    scratch_shapes += list(scratch)

    def kern(*refs):
        ins, w_hbm = refs[:n_in], refs[n_in:n_in + n_w]
        outs = refs[n_in + n_w:n_in + n_w + n_out]
        rest = refs[n_in + n_w + n_out:]
        w_vmem = rest[:n_w]
        extra = rest[n_w + 1:] if n_w else rest
        i = pl.program_id(0)

        @pl.when(i == 0)
        def _():
            copies = [pltpu.make_async_copy(w_hbm[k], w_vmem[k], rest[n_w].at[k]) for k in range(n_w)]
            for cp in copies:
                cp.start()
            for cp in copies:
                cp.wait()
            for a in outs[len(row_outs):]:
                a[...] = jnp.zeros(a.shape, a.dtype)

        body(i, n, *ins, *w_vmem, *outs, *extra)

    res = pl.pallas_call(
        kern, name=name, grid=(n,), in_specs=in_specs, out_specs=out_specs, out_shape=out_shape,
        scratch_shapes=scratch_shapes, compiler_params=_params(1),
    )(*[r[0] for r in rows], *consts, *residents)
    return res


def _mm_tn(name, x, y, k_dim, n_dim, *, tk, tn, x_off=0, y_off=0, tt=MM_TT):
    n_tok = x.shape[0]
    tt = min(tt, n_tok)
    n_t = n_tok // tt
    xb, yb = x_off // tk, y_off // tn

    def kern(x_ref, y_ref, o_ref):
        @pl.when(pl.program_id(2) == 0)
        def _():
            o_ref[...] = jnp.zeros(o_ref.shape, F32)

        o_ref[...] += _dot_tn(x_ref[...], y_ref[...])

    return pl.pallas_call(
        kern, name=name, grid=(k_dim // tk, n_dim // tn, n_t),
        in_specs=[pl.BlockSpec((tt, tk), lambda k, n, t: (t, xb + k)),
                  pl.BlockSpec((tt, tn), lambda k, n, t: (t, yb + n))],
        out_specs=pl.BlockSpec((tk, tn), lambda k, n, t: (k, n)),
        out_shape=jax.ShapeDtypeStruct((k_dim, n_dim), F32),
        compiler_params=_params(3),
    )(x, y)


def _ffn_fwd(name, x, gain, wg_t, wu_t, wd):
    n_tok, tm = x.shape[0], min(ROW_BLOCK, x.shape[0])

    def body(i, n, x_ref, g_ref, wg_ref, wu_ref, wd_ref, xo_ref, a_ref, b_ref):
        x_v = x_ref[...]
        xh, _ = _rms_stats(x_v)
        h = (xh * g_ref[...]).astype(BF16)
        y = jnp.zeros((tm, D_MODEL), F32)
        for f0 in range(0, D_FF, FF_SPLIT):
            sl = slice(f0, f0 + FF_SPLIT)
            a = _dot_nt(h, wg_ref[sl, :])
            b = _dot_nt(h, wu_ref[sl, :])
            a_ref[:, sl] = a.astype(BF16)
            b_ref[:, sl] = b.astype(BF16)
            s = (a * jax.nn.sigmoid(a) * b).astype(BF16)
            y = y + _dot(s, wd_ref[sl, :])
        xo_ref[...] = x_v + 0.5 * y

    return _rowcall(name, body, n_tok, tm, [x], [gain], [wg_t, wu_t, wd],
                    [(D_MODEL, F32), (D_FF, BF16), (D_FF, BF16)], [])


def _ffn_bwd(name, x, dxo, a, b, gain, wd, wg_t, wu_t):
    n_tok, tm = x.shape[0], min(FFN_BWD_ROW_BLOCK, x.shape[0])

    def body(i, n, x_ref, dxo_ref, a_ref, b_ref, g_ref, wdt_ref, wgt_ref, wut_ref,
             dx_ref, da_ref, db_ref, s_ref, h_ref, dy_ref, dg_ref):
        g = g_ref[...]
        xh, r = _rms_stats(x_ref[...])
        h_ref[...] = (xh * g).astype(BF16)
        dxo_v = dxo_ref[...]
        dy = (0.5 * dxo_v).astype(BF16)
        dy_ref[...] = dy
        dh = jnp.zeros((tm, D_MODEL), F32)
        for f0 in range(0, D_FF, FF_SPLIT):
            sl = slice(f0, f0 + FF_SPLIT)
            a_v = a_ref[:, sl].astype(F32)
            b_v = b_ref[:, sl].astype(F32)
            ds = _dot_nt(dy, wdt_ref[sl, :])
            sig = jax.nn.sigmoid(a_v)
            sa = a_v * sig
            s_ref[:, sl] = (sa * b_v).astype(BF16)
            da = (ds * b_v * (sig * (1.0 + a_v * (1.0 - sig)))).astype(BF16)
            db = (ds * sa).astype(BF16)
            da_ref[:, sl] = da
            db_ref[:, sl] = db
            dh = dh + _dot(da, wgt_ref[sl, :]) + _dot(db, wut_ref[sl, :])
        dx, dg = _rms_bwd(xh, r, g, dh)
        dx_ref[...] = dxo_v + dx
        dg_ref[...] += dg

    return _rowcall(name, body, n_tok, tm, [x, dxo, a, b], [gain], [wd, wg_t, wu_t],
                    [(D_MODEL, F32), (D_FF, BF16), (D_FF, BF16), (D_FF, BF16), (D_MODEL, BF16), (D_MODEL, BF16)],
                    [((1, D_MODEL), F32)])


def _proj_fwd(name, x, gain, w1_t, wf_t, wgl_t):
    n_tok, tm = x.shape[0], min(ROW_BLOCK, x.shape[0])
    c0, c1 = 3 * CONV_W, 3 * CONV_W + 2 * SG_W

    def body(i, n, x_ref, g_ref, w1_ref, wf_ref, wgl_ref, cin_ref, sg_ref, qkv_ref, fl_ref, gl_ref):
        xh, _ = _rms_stats(x_ref[...])
        h = (xh * g_ref[...]).astype(BF16)
        cin_ref[...] = _dot_nt(h, w1_ref[0:c0, :]).astype(BF16)
        sg_ref[...] = _dot_nt(h, w1_ref[c0:c1, :]).astype(BF16)
        qkv_ref[...] = _dot_nt(h, w1_ref[c1:W1_COLS, :]).astype(BF16)
        fl_ref[...] = _dot_nt(h, wf_ref[...])
        gl_ref[...] = _dot_nt(h, wgl_ref[...]).astype(BF16)

    return _rowcall(name, body, n_tok, tm, [x], [gain], [w1_t, wf_t, wgl_t],
                    [(3 * CONV_W, BF16), (2 * SG_W, BF16), (3 * FOX_W, BF16), (LANES, F32), (3 * D_MODEL, BF16)], [])


def _proj_bwd(name, x, dxin, dcin, dsg, dq, dk, dv, dfl, dgl, gain, w1_t, wf_t, wgl_t):
    n_tok, tm = x.shape[0], min(ROW_BLOCK, x.shape[0])
    c0, c1 = 3 * CONV_W, 3 * CONV_W + 2 * SG_W

    def body(i, n, x_ref, dxin_ref, dcin_ref, dsg_ref, dq_ref, dk_ref, dv_ref, dfl_ref, dgl_ref, g_ref,
             w1t_ref, wft_ref, wglt_ref, dx_ref, h_ref, dqkv_ref, dfb_ref, dg_ref):
        g = g_ref[...]
        xh, r = _rms_stats(x_ref[...])
        h_ref[...] = (xh * g).astype(BF16)
        dfb = dfl_ref[...].astype(BF16)
        dfb_ref[...] = dfb
        dh = _dot(dcin_ref[...], w1t_ref[0:c0, :])
        dh = dh + _dot(dsg_ref[...], w1t_ref[c0:c1, :])
        for k, d_ref in enumerate((dq_ref, dk_ref, dv_ref)):
            d_b = d_ref[...].astype(BF16)
            dqkv_ref[:, k * FOX_W:(k + 1) * FOX_W] = d_b
            dh = dh + _dot(d_b, w1t_ref[c1 + k * FOX_W:c1 + (k + 1) * FOX_W, :])
        dh = dh + _dot(dfb, wft_ref[...])
        dh = dh + _dot(dgl_ref[...], wglt_ref[...])
        dx, dg = _rms_bwd(xh, r, g, dh)
        dx_ref[...] = dxin_ref[...] + dx
        dg_ref[...] += dg

    return _rowcall(name, body, n_tok, tm, [x, dxin, dcin, dsg, dq, dk, dv, dfl, dgl], [gain], [w1_t, wf_t, wgl_t],
                    [(D_MODEL, F32), (D_MODEL, BF16), (3 * FOX_W, BF16), (LANES, BF16)], [((1, D_MODEL), F32)])


def _conv_taps(z, prev_z, i):
    tm = z.shape[0]
    row = lax.broadcasted_iota(jnp.int32, (tm, 1), 0)
    live = (i > 0).astype(F32)
    p1, p2 = prev_z[HALO - 1:HALO, :] * live, prev_z[HALO - 2:HALO - 1, :] * live
    z1 = jnp.where(row == 0, p1, pltpu.roll(z, 1, 0))
    z2 = jnp.where(row == 0, p2, jnp.where(row == 1, p1, pltpu.roll(z, 2, 0)))
    return z1, z2


def _prev_spec(tm, cols):
    return pl.BlockSpec((HALO, cols), lambda i: (jnp.maximum(i * (tm // HALO) - 1, 0), 0))


def _next_spec(tm, cols, n):
    last = n * (tm // HALO) - 1
    return pl.BlockSpec((HALO, cols), lambda i: (jnp.minimum((i + 1) * (tm // HALO), last), 0))


def _conv_fwd(name, cin, cw):
    n_tok, tm = cin.shape[0], min(ROW_BLOCK, cin.shape[0])
    w = CONV_W

    def body(i, n, c_ref, p_ref, cw_ref, ya_ref):
        c_v, p_v = c_ref[...].astype(F32), p_ref[...].astype(F32)
        z = c_v[:, w:2 * w] * c_v[:, 2 * w:]
        z1, z2 = _conv_taps(z, p_v[:, w:2 * w] * p_v[:, 2 * w:], i)
        y = cw_ref[0:1, :] * z2 + cw_ref[1:2, :] * z1 + cw_ref[2:3, :] * z
        ya_ref[...] = (c_v[:, 0:w] * y).astype(BF16)

    return _rowcall(name, body, n_tok, tm, [cin, (cin, _prev_spec(tm, 3 * w))], [cw], [], [(w, BF16)], [])[0]


def _conv_bwd(name, cin, dya, cw):
    n_tok, tm = cin.shape[0], min(ROW_BLOCK, cin.shape[0])
    w = CONV_W
    n_blocks = n_tok // tm

    def body(i, n, c_ref, p_ref, nx_ref, dya_ref, ndya_ref, cw_ref, dc_ref, dcw_ref):
        c_v, p_v = c_ref[...].astype(F32), p_ref[...].astype(F32)
        ab, ac, ah = c_v[:, 0:w], c_v[:, w:2 * w], c_v[:, 2 * w:]
        z = ac * ah
        z1, z2 = _conv_taps(z, p_v[:, w:2 * w] * p_v[:, 2 * w:], i)
        w0, w1, w2 = cw_ref[0:1, :], cw_ref[1:2, :], cw_ref[2:3, :]
        y = w0 * z2 + w1 * z1 + w2 * z
        dya_v = dya_ref[...].astype(F32)
        dy = dya_v * ab
        live = (i < n - 1).astype(F32)
        ndy = ndya_ref[...].astype(F32) * nx_ref[:, 0:w].astype(F32) * live
        row = lax.broadcasted_iota(jnp.int32, (tm, 1), 0)
        dy1 = jnp.where(row == tm - 1, ndy[0:1, :], pltpu.roll(dy, tm - 1, 0))
        dy2 = jnp.where(row == tm - 1, ndy[1:2, :], jnp.where(row == tm - 2, ndy[0:1, :], pltpu.roll(dy, tm - 2, 0)))
        dz = w2 * dy + w1 * dy1 + w0 * dy2
        dc_ref[:, 0:w] = (dya_v * y).astype(BF16)
        dc_ref[:, w:2 * w] = (dz * ah).astype(BF16)
        dc_ref[:, 2 * w:] = (dz * ac).astype(BF16)
        dcw_ref[0:1, :] += jnp.sum(dy * z2, axis=0, keepdims=True)
        dcw_ref[1:2, :] += jnp.sum(dy * z1, axis=0, keepdims=True)
        dcw_ref[2:3, :] += jnp.sum(dy * z, axis=0, keepdims=True)

    return _rowcall(name, body, n_tok, tm,
                    [cin, (cin, _prev_spec(tm, 3 * w)), (cin, _next_spec(tm, 3 * w, n_blocks)),
                     dya, (dya, _next_spec(tm, w, n_blocks))],
                    [cw], [], [(3 * w, BF16)], [((8, w), F32)])


def _sg_common(sg_ref, gn_ref):
    s_v = sg_ref[...].astype(F32)
    u, v = s_v[:, 0:SG_W], s_v[:, SG_W:]
    ug, tu = _gelu(u)
    vg, tv = _gelu(v)
    vh, r = _rms_stats(vg)
    vn = (vh * gn_ref[...]).astype(BF16)
    return u, v, ug, tu, tv, vh, r, vn


def _sg_fwd(name, sgin, gn, sgw, bias_full):
    n_tok, tm = sgin.shape[0], min(ROW_BLOCK, sgin.shape[0])

    def body(i, n, sg_ref, gn_ref, w_ref, bias_ref, yb_ref):
        _, _, ug, _, _, _, _, vn = _sg_common(sg_ref, gn_ref)
        tril = lax.broadcasted_iota(jnp.int32, (CHUNK, CHUNK), 0) >= lax.broadcasted_iota(jnp.int32, (CHUNK, CHUNK), 1)
        wt = [jnp.where(tril, w_ref[g], 0.0).astype(BF16) for g in range(SG_G)]
        for c0 in range(0, tm, CHUNK):
            sv = jnp.concatenate(
                [_dot(wt[g], vn[c0:c0 + CHUNK, g * CHUNK:(g + 1) * CHUNK]) for g in range(SG_G)], axis=1)
            sv = sv + bias_ref[...]
            yb_ref[c0:c0 + CHUNK, :] = (ug[c0:c0 + CHUNK, :] * sv).astype(BF16)

    return _rowcall(name, body, n_tok, tm, [sgin], [gn, sgw, bias_full], [], [(SG_W, BF16)], [])[0]


def _sg_bwd(name, sgin, dyb, gn, sgw, sgw_t, bias_full):
    n_tok, tm = sgin.shape[0], min(ROW_BLOCK, sgin.shape[0])

    def body(i, n, sg_ref, dyb_ref, gn_ref, w_ref, wt_ref, bias_ref, dsg_ref, dgn_ref, dw_ref, db_ref):
        u, v, ug, tu, tv, vh, r, vn = _sg_common(sg_ref, gn_ref)
        r0 = lax.broadcasted_iota(jnp.int32, (CHUNK, CHUNK), 0)
        r1 = lax.broadcasted_iota(jnp.int32, (CHUNK, CHUNK), 1)
        wt = [jnp.where(r0 >= r1, w_ref[g], 0.0).astype(BF16) for g in range(SG_G)]
        wtt = [jnp.where(r0 <= r1, wt_ref[g], 0.0).astype(BF16) for g in range(SG_G)]
        dyb_v = dyb_ref[...].astype(F32)
        dvn_rows = []
        for c0 in range(0, tm, CHUNK):
            rows = slice(c0, c0 + CHUNK)
            svs, dvns = [], []
            dsv = dyb_v[rows, :] * ug[rows, :]
            for g in range(SG_G):
                cols = slice(g * CHUNK, (g + 1) * CHUNK)
                svs.append(_dot(wt[g], vn[rows, cols]))
                dsv_g = dsv[:, cols]
                dsv_b = dsv_g.astype(BF16)
                dvns.append(_dot(wtt[g], dsv_b))
                dw_ref[g] += jnp.where(r0 >= r1, _dot_nt(dsv_b, vn[rows, cols]), 0.0)
                db_ref[g] += jnp.broadcast_to(jnp.sum(dsv_g, axis=1, keepdims=True), (CHUNK, CHUNK))
            sv = jnp.concatenate(svs, axis=1) + bias_ref[...]
            dug = dyb_v[rows, :] * sv
            dsg_ref[rows, 0:SG_W] = (dug * _gelu_grad(u[rows, :], tu[rows, :])).astype(BF16)
            dvn_rows.append(jnp.concatenate(dvns, axis=1))
        dvn = jnp.concatenate(dvn_rows, axis=0)
        dvg, dgn = _rms_bwd(vh, r, gn_ref[...], dvn)
        dsg_ref[:, SG_W:] = (dvg * _gelu_grad(v, tv)).astype(BF16)
        dgn_ref[...] += dgn

    return _rowcall(name, body, n_tok, tm, [sgin, dyb], [gn, sgw, sgw_t, bias_full], [], [(2 * SG_W, BF16)],
                    [((1, SG_W), F32), ((SG_G, CHUNK, CHUNK), F32), ((SG_G, CHUNK, CHUNK), F32)])


def _forget_cumsum(name, fl, bf):
    n_tok, tm = fl.shape[0], min(ROW_BLOCK, fl.shape[0])

    def body(i, n, fl_ref, b_ref, c_ref, carry):
        @pl.when(i == 0)
        def _():
            carry[...] = jnp.zeros(carry.shape, F32)

        z = fl_ref[...] + b_ref[...]
        lf = jnp.minimum(z, 0.0) - jnp.log1p(jnp.exp(-jnp.abs(z)))
        tri = (lax.broadcasted_iota(jnp.int32, (tm, tm), 0) >= lax.broadcasted_iota(jnp.int32, (tm, tm), 1)).astype(BF16)
        c = _split3_dot(tri, lf) + carry[...]
        c_ref[...] = c * LOG2E
        carry[...] = c[tm - 1:tm, :]

    return _rowcall(name, body, n_tok, tm, [fl], [bf], [], [(LANES, F32)], [], scratch=[pltpu.VMEM((1, LANES), F32)])[0]


def _forget_bwd(name, dc, fl, bf):
    n_tok, tm = fl.shape[0], min(ROW_BLOCK, fl.shape[0])
    n = n_tok // tm
    rev = pl.BlockSpec((tm, LANES), lambda i: (n - 1 - i, 0))

    def kern(dc_ref, fl_ref, b_ref, dfl_ref, db_ref, carry):
        @pl.when(pl.program_id(0) == 0)
        def _():
            carry[...] = jnp.zeros(carry.shape, F32)
            db_ref[...] = jnp.zeros(db_ref.shape, F32)

        triu = (lax.broadcasted_iota(jnp.int32, (tm, tm), 0) <= lax.broadcasted_iota(jnp.int32, (tm, tm), 1)).astype(BF16)
        dlf = _split3_dot(triu, dc_ref[...]) + carry[...]
        carry[...] = dlf[0:1, :]
        z = fl_ref[...] + b_ref[...]
        dfl = dlf * jax.nn.sigmoid(-z)
        dfl_ref[...] = dfl
        db_ref[...] += jnp.sum(dfl, axis=0, keepdims=True)

    return pl.pallas_call(
        kern, name=name, grid=(n,), in_specs=[rev, rev, _full_spec((1, LANES))],
        out_specs=[rev, _full_spec((1, LANES))],
        out_shape=[jax.ShapeDtypeStruct((n_tok, LANES), F32), jax.ShapeDtypeStruct((1, LANES), F32)],
        scratch_shapes=[pltpu.VMEM((1, LANES), F32)], compiler_params=_params(1),
    )(dc, fl, bf)


def _lane_pick(c_blk, h):
    lane = lax.broadcasted_iota(jnp.int32, c_blk.shape, 1)
    return jnp.broadcast_to(jnp.sum(jnp.where(lane == h, c_blk, 0.0), axis=1, keepdims=True), c_blk.shape)


def _wide(stat, width):
    return jnp.tile(stat, (1, width // LANES))


def _pair_half(rows):
    return lax.broadcasted_iota(jnp.int32, (rows, LANES), 1) // FOX_D


def _col_to_row(col):
    pick = (lax.broadcasted_iota(jnp.int32, (8, LANES), 1) == 0).astype(BF16)
    hi = col.astype(BF16)
    r1 = col - hi.astype(F32)
    mid = r1.astype(BF16)
    lo = (r1 - mid.astype(F32)).astype(BF16)
    return (_dot_nt(pick, hi) + _dot_nt(pick, mid) + _dot_nt(pick, lo))[0:1, :]


def _tri_table(nb, lower):
    rows = [(a, b) for a in range(nb) for b in (range(a + 1) if lower else range(a, nb))]
    return tuple(jnp.asarray([r[k] for r in rows], jnp.int32) for k in range(2))


def _flash_bounds(name, qkv):
    n_tok = qkv.shape[0]
    blk = min(FLASH_BLOCK, n_tok)

    def kern(x_ref, o_ref):
        head = (lax.broadcasted_iota(jnp.int32, (FOX_W, LANES), 0) // FOX_D
                == lax.broadcasted_iota(jnp.int32, (FOX_W, LANES), 1)).astype(BF16)
        q, k = x_ref[:, 0:FOX_W].astype(F32), x_ref[:, FOX_W:2 * FOX_W].astype(F32)
        qn = jnp.sqrt(jnp.max(_dot((q * q).astype(BF16), head) * 1.01, axis=0, keepdims=True))
        kn = jnp.sqrt(jnp.max(_dot((k * k).astype(BF16), head) * 1.01, axis=0, keepdims=True))
        diag = jnp.min(_dot((q * k).astype(BF16), head), axis=0, keepdims=True) - qn * kn * (2.0 ** -8)
        o_ref[...] = jnp.concatenate([qn, kn, diag, jnp.zeros((5, LANES), F32)], axis=0)

    return pl.pallas_call(
        kern, name=name, grid=(n_tok // blk,),
        in_specs=[pl.BlockSpec((blk, 2 * FOX_W), lambda b: (b, 0))], out_specs=pl.BlockSpec((8, LANES), lambda b: (b, 0)),
        out_shape=jax.ShapeDtypeStruct((n_tok // blk * 8, LANES), F32), compiler_params=_params(1),
    )(qkv)


def _prune_tables(bounds, c2, blk):
    nb = c2.shape[0] // blk
    tab = bounds.reshape(nb, 8, LANES)
    c_blocks = c2.reshape(nb, blk, LANES)
    return tuple(t[:, :FOX_H].reshape(-1)
                 for t in (tab[:, 0], tab[:, 1], tab[:, 2], c_blocks[:, 0], c_blocks[:, blk - 1]))


def _block_is_live(tables, p, i, j):
    qn, kn, diag, c_first, c_last = tables
    a = FOX_SCALE * LOG2E
    live = []
    for e in range(2):
        h = 2 * p + e
        qi, kj = FOX_H * i + h, FOX_H * j + h
        u = a * qn[qi] * kn[kj] + c_first[qi] - c_last[kj]
        lo = a * diag[qi]
        live.append((u - lo) > -PRUNE_LOG2)
    return live


def _flash_fwd(name, qkv, c2, ct2, tables):
    n_tok = qkv.shape[0]
    blk = min(FLASH_BLOCK, n_tok)
    nb = n_tok // blk
    n_pair = FOX_H // 2
    it, jt = _tri_table(nb, True)

    strip = min(FLASH_STRIP, blk)

    def kern(it_ref, jt_ref, qn_ref, kn_ref, dg_ref, cf_ref, cl_ref, q_ref, k_ref, v_ref, c_ref, ct_ref, o_ref, lse_ref,
             m_s, l_s, acc_s, cq_s, qm_s, al_s, s_s, pb_s):
        n = pl.program_id(1)
        i, j = it_ref[n], jt_ref[n]
        half = _pair_half(blk)
        live = _block_is_live((qn_ref, kn_ref, dg_ref, cf_ref, cl_ref), pl.program_id(0), i, j)

        @pl.when(j == 0)
        def _():
            m_s[...] = jnp.full(m_s.shape, -jnp.inf, F32)
            l_s[...] = jnp.zeros(l_s.shape, F32)
            acc_s[...] = jnp.zeros(acc_s.shape, F32)
            for e in range(2):
                cq_s[e] = _lane_pick(c_ref[...], 2 * pl.program_id(0) + e)
                qm_s[e] = jnp.where(half == e, q_ref[...], jnp.zeros_like(q_ref[...]))

        def step(heads, on_diagonal):
            k_v, v_v = k_ref[...], v_ref[...]
            for e in heads:
                s_s[e] = _dot_nt(qm_s[e], k_v)
            for e in heads:
                ck = ct_ref[e]
                for r0 in range(0, blk, strip):
                    rs = slice(r0, r0 + strip)
                    t = s_s[e, rs, :] * (FOX_SCALE * LOG2E) - ck
                    if on_diagonal:
                        keep = (lax.broadcasted_iota(jnp.int32, (strip, blk), 1)
                                <= lax.broadcasted_iota(jnp.int32, (strip, blk), 0) + r0)
                        t = jnp.where(keep, t, -jnp.inf)
                    m_old, cq = m_s[e, rs, :], cq_s[e, rs, :]
                    m_new = jnp.maximum(m_old, jnp.max(t, axis=1, keepdims=True) + cq)
                    p = jnp.exp2(t - _wide(m_new - cq, blk))
                    alpha = jnp.exp2(m_old - m_new)
                    l_s[e, rs, :] = alpha * l_s[e, rs, :] + jnp.sum(p, axis=1, keepdims=True)
                    m_s[e, rs, :] = m_new
                    al_s[e, rs, :] = alpha
                    pb_s[e, rs, :] = p.astype(BF16)
            if len(heads) == 2:
                pv = jnp.where(half == 0, _dot(pb_s[0], v_v), _dot(pb_s[1], v_v))
                acc_s[...] = jnp.where(half == 0, al_s[0], al_s[1]) * acc_s[...] + pv
            else:
                e = heads[0]
                acc_s[...] = jnp.where(half == e, al_s[e] * acc_s[...] + _dot(pb_s[e], v_v), acc_s[...])

        below = j < i

        @pl.when(below & live[0] & live[1])
        def _():
            step((0, 1), False)

        @pl.when(below & live[0] & jnp.logical_not(live[1]))
        def _():
            step((0,), False)

        @pl.when(below & jnp.logical_not(live[0]) & live[1])
        def _():
            step((1,), False)

        @pl.when(j == i)
        def _():
            step((0, 1), True)
            o_ref[...] = (acc_s[...] / jnp.where(half == 0, l_s[0], l_s[1])).astype(BF16)
            rows = pl.ds(pl.multiple_of(i * blk, blk), blk)
            lse_ref[0, :, rows] = jnp.concatenate(
                [_col_to_row(m_s[e] + jnp.log(l_s[e]) * LOG2E) for e in range(2)], axis=0)

    def key_block(p, n, it_r, jt_r, tabs):
        i, j = it_r[n], jt_r[n]
        live = _block_is_live(tabs, p, i, j)
        return jnp.where(live[0] | live[1], j, i)

    def col_block(first, inner):
        def index(p, n, it_r, jt_r, *tabs):
            return ((key_block(p, n, it_r, jt_r, tabs) if inner else it_r[n]), first + p)
        return pl.BlockSpec((blk, LANES), index)

    pair_stat = (2, blk, LANES)
    grid_spec = pltpu.PrefetchScalarGridSpec(
        num_scalar_prefetch=2 + len(tables), grid=(n_pair, int(it.shape[0])),
        in_specs=[col_block(0, False), col_block(n_pair, True), col_block(2 * n_pair, True),
                  pl.BlockSpec((blk, LANES), lambda p, n, it_r, jt_r, *_: (it_r[n], 0)),
                  pl.BlockSpec((2, 1, blk), lambda p, n, it_r, jt_r, *tabs: (p, 0, key_block(p, n, it_r, jt_r, tabs)))],
        out_specs=[col_block(0, False), pl.BlockSpec((1, 2, n_tok), lambda p, n, it_r, jt_r, *_: (p, 0, 0))],
        scratch_shapes=[pltpu.VMEM(pair_stat, F32), pltpu.VMEM(pair_stat, F32), pltpu.VMEM((blk, LANES), F32),
                        pltpu.VMEM(pair_stat, F32), pltpu.VMEM(pair_stat, BF16), pltpu.VMEM(pair_stat, F32),
                        pltpu.VMEM((2, blk, blk), F32), pltpu.VMEM((2, blk, blk), BF16)],
    )
    return pl.pallas_call(
        kern, name=name, grid_spec=grid_spec,
        out_shape=[jax.ShapeDtypeStruct((n_tok, FOX_W), BF16), jax.ShapeDtypeStruct((n_pair, 2, n_tok), F32)],
        compiler_params=_params(2),
    )(it, jt, *tables, qkv, qkv, qkv, c2, ct2)


def _flash_delta(name, do, o):
    n_tok, tm = do.shape[0], min(ROW_BLOCK, do.shape[0])

    def body(i, n, do_ref, o_ref, d_ref):
        prod = do_ref[...].astype(F32) * o_ref[...].astype(F32)
        head = (lax.broadcasted_iota(jnp.int32, (FOX_W, LANES), 0) // FOX_D
                == lax.broadcasted_iota(jnp.int32, (FOX_W, LANES), 1)).astype(BF16)
        hi = prod.astype(BF16)
        r1 = prod - hi.astype(F32)
        mid = r1.astype(BF16)
        lo = (r1 - mid.astype(F32)).astype(BF16)
        d_ref[...] = _dot(hi, head) + _dot(mid, head) + _dot(lo, head)

    return _rowcall(name, body, n_tok, tm, [do, o], [], [], [(LANES, F32)], [])[0]


def _flash_bwd(name, qkv, do, r_row, d_row, c2, tables):
    n_tok = qkv.shape[0]
    blk = min(FLASH_BLOCK, n_tok)
    nb = n_tok // blk
    n_pair = FOX_H // 2
    jt, it = _tri_table(nb, False)

    def kern(jt_ref, it_ref, qn_ref, kn_ref, dg_ref, cf_ref, cl_ref, k_ref, v_ref, q_ref, do_ref, rr_ref, dr_ref, c_ref,
             dq_ref, dk_ref, dv_ref, dc_ref, km_s, vm_s, ck_s, dck_s):
        n = pl.program_id(1)
        j, i = jt_ref[n], it_ref[n]
        first = _pair_half(blk) == 0
        live = _block_is_live((qn_ref, kn_ref, dg_ref, cf_ref, cl_ref), pl.program_id(0), i, j)

        @pl.when(n == 0)
        def _():
            dq_ref[...] = jnp.zeros(dq_ref.shape, F32)
            dc_ref[...] = jnp.zeros(dc_ref.shape, F32)

        @pl.when(i == j)
        def _():
            dk_ref[...] = jnp.zeros(dk_ref.shape, F32)
            dv_ref[...] = jnp.zeros(dv_ref.shape, F32)
            dck_s[...] = jnp.zeros(dck_s.shape, F32)
            for e in range(2):
                mine = first if e == 0 else jnp.logical_not(first)
                km_s[e] = jnp.where(mine, k_ref[...], jnp.zeros_like(k_ref[...]))
                vm_s[e] = jnp.where(mine, v_ref[...], jnp.zeros_like(v_ref[...]))
                ck_s[e] = _lane_pick(c_ref[...], 2 * pl.program_id(0) + e)

        def step(heads, on_diagonal):
            q_v, do_v, k_v = q_ref[...], do_ref[...], k_ref[...]
            dvs, dks, dqs, sums = {}, {}, {}, {}
            for e in heads:
                t = _dot_nt(km_s[e], q_v) * (FOX_SCALE * LOG2E) - _wide(ck_s[e], blk) - rr_ref[e]
                if on_diagonal:
                    keep = (lax.broadcasted_iota(jnp.int32, (blk, blk), 1)
                            >= lax.broadcasted_iota(jnp.int32, (blk, blk), 0))
                    t = jnp.where(keep, t, -jnp.inf)
                p = jnp.exp2(t)
                ds = p * (_dot_nt(vm_s[e], do_v) - dr_ref[e])
                p_b, ds_b = p.astype(BF16), ds.astype(BF16)
                dvs[e] = _dot(p_b, do_v)
                dks[e] = _dot(ds_b, q_v)
                dqs[e] = _dot_tn(ds_b, k_v)
                dck_s[e] += jnp.broadcast_to(jnp.sum(ds, axis=1, keepdims=True), (blk, LANES))
                sums[e] = jnp.sum(ds, axis=0, keepdims=True)
            rows = pl.ds(pl.multiple_of(i * blk, blk), blk)

            def merged(parts):
                zero = jnp.zeros((blk, LANES), F32)
                return jnp.where(first, parts.get(0, zero), parts.get(1, zero))

            dv_ref[...] += merged(dvs)
            dk_ref[...] += merged(dks) * FOX_SCALE
            dq_ref[rows, :] += merged(dqs) * FOX_SCALE
            none = jnp.zeros((1, blk), F32)
            dc_ref[0, :, rows] += jnp.concatenate([sums.get(0, none), sums.get(1, none)], axis=0)

        above = i > j

        @pl.when(above & live[0] & live[1])
        def _():
            step((0, 1), False)

        @pl.when(above & live[0] & jnp.logical_not(live[1]))
        def _():
            step((0,), False)

        @pl.when(above & jnp.logical_not(live[0]) & live[1])
        def _():
            step((1,), False)

        @pl.when(i == j)
        def _():
            step((0, 1), True)

        @pl.when(i == nb - 1)
        def _():
            keys = pl.ds(pl.multiple_of(j * blk, blk), blk)
            dc_ref[0, :, keys] -= jnp.concatenate([_col_to_row(dck_s[e]) for e in range(2)], axis=0)

    def query_block(p, n, jt_r, it_r, tabs):
        j, i = jt_r[n], it_r[n]
        live = _block_is_live(tabs, p, i, j)
        return jnp.where(live[0] | live[1] | (i == j), i, nb - 1)

    def col_block(first_col, inner):
        def index(p, n, jt_r, it_r, *tabs):
            return ((query_block(p, n, jt_r, it_r, tabs) if inner else jt_r[n]), first_col + p)
        return pl.BlockSpec((blk, LANES), index)

    def row_stat():
        return pl.BlockSpec((2, 1, blk), lambda p, n, jt_r, it_r, *tabs: (p, 0, query_block(p, n, jt_r, it_r, tabs)))

    pair_stat = (2, blk, LANES)
    grid_spec = pltpu.PrefetchScalarGridSpec(
        num_scalar_prefetch=2 + len(tables), grid=(n_pair, int(jt.shape[0])),
        in_specs=[col_block(n_pair, False), col_block(2 * n_pair, False), col_block(0, True), col_block(0, True),
                  row_stat(), row_stat(), pl.BlockSpec((blk, LANES), lambda p, n, jt_r, it_r, *_: (jt_r[n], 0))],
        out_specs=[pl.BlockSpec((n_tok, LANES), lambda p, n, jt_r, it_r, *_: (0, p)),
                   col_block(0, False), col_block(0, False),
                   pl.BlockSpec((1, 2, n_tok), lambda p, n, jt_r, it_r, *_: (p, 0, 0))],
        scratch_shapes=[pltpu.VMEM(pair_stat, BF16), pltpu.VMEM(pair_stat, BF16), pltpu.VMEM(pair_stat, F32),
                        pltpu.VMEM(pair_stat, F32)],
    )
    wide = jax.ShapeDtypeStruct((n_tok, FOX_W), F32)
    return pl.pallas_call(
        kern, name=name, grid_spec=grid_spec,
        out_shape=[wide, wide, wide, jax.ShapeDtypeStruct((n_pair, 2, n_tok), F32)],
        compiler_params=_params(2),
    )(jt, it, *tables, qkv, qkv, qkv, do, r_row, d_row, c2)


def _merge_fwd(name, x, ya, yb, yc, gl, wb, wo):
    n_tok, tm = x.shape[0], min(ROW_BLOCK, x.shape[0])

    def body(i, n, x_ref, ya_ref, yb_ref, yc_ref, gl_ref, wb_ref, wo_ref, xo_ref):
        merged = jnp.zeros((tm, D_MODEL), F32)
        for k, y_ref in enumerate((ya_ref, yb_ref, yc_ref)):
            gate = jax.nn.sigmoid(gl_ref[:, k * D_MODEL:(k + 1) * D_MODEL].astype(F32))
            merged = merged + gate * _dot(y_ref[...], wb_ref[k])
        xo_ref[...] = x_ref[...] + _dot(merged.astype(BF16), wo_ref[...])

    return _rowcall(name, body, n_tok, tm, [x, ya, yb, yc, gl], [], [wb, wo], [(D_MODEL, F32)], [])[0]


def _merge_bwd(name, dxo, ya, yb, yc, gl, wb, wo):
    n_tok, tm = dxo.shape[0], min(ROW_BLOCK, dxo.shape[0])

    def body(i, n, dxo_ref, ya_ref, yb_ref, yc_ref, gl_ref, wb_ref, wo_ref,
             mg_ref, dob_ref, dgl_ref, dbr_ref, dya_ref, dyb_ref, dyc_ref):
        dob = dxo_ref[...].astype(BF16)
        dob_ref[...] = dob
        dm = _dot_nt(dob, wo_ref[...])
        merged = jnp.zeros((tm, D_MODEL), F32)
        for k, (y_ref, dy_ref) in enumerate(((ya_ref, dya_ref), (yb_ref, dyb_ref), (yc_ref, dyc_ref))):
            cols = slice(k * D_MODEL, (k + 1) * D_MODEL)
            gate = jax.nn.sigmoid(gl_ref[:, cols].astype(F32))
            br = _dot(y_ref[...], wb_ref[k])
            merged = merged + gate * br
            dgl_ref[:, cols] = (dm * br * gate * (1.0 - gate)).astype(BF16)
            dbr = (dm * gate).astype(BF16)
            dbr_ref[:, cols] = dbr
            dy_ref[...] = _dot_nt(dbr, wb_ref[k]).astype(BF16)
        mg_ref[...] = merged.astype(BF16)

    return _rowcall(name, body, n_tok, tm, [dxo, ya, yb, yc, gl], [], [wb, wo],
                    [(D_MODEL, BF16), (D_MODEL, BF16), (3 * D_MODEL, BF16), (3 * D_MODEL, BF16),
                     (CONV_W, BF16), (SG_W, BF16), (FOX_W, BF16)], [])


def _mem_fwd(name, mem, gain, wk, wv):
    n_mem = mem.shape[0]

    def kern(m_ref, g_ref, wk_ref, wv_ref, mn_ref, k_ref, v_ref):
        mh, _ = _rms_stats(m_ref[...])
        mn = (mh * g_ref[...]).astype(BF16)
        mn_ref[...] = mn
        k_ref[...] = _dot(mn, wk_ref[...]).astype(BF16)
        v_ref[...] = _dot(mn, wv_ref[...]).astype(BF16)

    shp = jax.ShapeDtypeStruct((n_mem, D_MODEL), BF16)
    return pl.pallas_call(
        kern, name=name, grid=(1,),
        in_specs=[_full_spec(mem.shape), _full_spec(gain.shape), _full_spec(wk.shape), _full_spec(wv.shape)],
        out_specs=[_full_spec(shp.shape)] * 3, out_shape=[shp] * 3, compiler_params=_params(1),
    )(mem, gain, wk, wv)


def _mem_bwd(name, mem, gain, dkx, dvx, wk, wv):
    n_mem = mem.shape[0]

    def kern(m_ref, g_ref, dk_ref, dv_ref, wk_ref, wv_ref, dkb_ref, dvb_ref, dg_ref):
        mh, _ = _rms_stats(m_ref[...])
        dkb, dvb = dk_ref[...].astype(BF16), dv_ref[...].astype(BF16)
        dkb_ref[...] = dkb
        dvb_ref[...] = dvb
        dm = _dot_nt(dkb, wk_ref[...]) + _dot_nt(dvb, wv_ref[...])
        dg_ref[...] = jnp.sum(dm * mh, axis=0, keepdims=True)

    shp = jax.ShapeDtypeStruct((n_mem, D_MODEL), BF16)
    args = (mem, gain, dkx, dvx, wk, wv)
    return pl.pallas_call(
        kern, name=name, grid=(1,), in_specs=[_full_spec(a.shape) for a in args],
        out_specs=[_full_spec(shp.shape)] * 2 + [_full_spec((1, D_MODEL))],
        out_shape=[shp, shp, jax.ShapeDtypeStruct((1, D_MODEL), F32)], compiler_params=_params(1),
    )(*args)


def _xa_probs(q_b, kx_ref, hd):
    cols = slice(hd * XA_D, (hd + 1) * XA_D)
    s = _dot_nt(q_b[:, cols], kx_ref[:, cols]) * (XA_D ** -0.5)
    e = jnp.exp(s - jnp.max(s, axis=1, keepdims=True))
    return e / jnp.sum(e, axis=1, keepdims=True)


def _xa_fwd(name, x, gain, kx, vx, wq, wo):
    n_tok, tm = x.shape[0], min(ROW_BLOCK, x.shape[0])

    def body(i, n, x_ref, g_ref, kx_ref, vx_ref, wq_ref, wo_ref, xo_ref):
        x_v = x_ref[...]
        xh, _ = _rms_stats(x_v)
        q_b = _dot((xh * g_ref[...]).astype(BF16), wq_ref[...]).astype(BF16)
        o = jnp.concatenate(
            [_dot(_xa_probs(q_b, kx_ref, hd).astype(BF16), vx_ref[:, hd * XA_D:(hd + 1) * XA_D]) for hd in range(XA_H)],
            axis=1)
        xo_ref[...] = x_v + _dot(o.astype(BF16), wo_ref[...])

    return _rowcall(name, body, n_tok, tm, [x], [gain, kx, vx], [wq, wo], [(D_MODEL, F32)], [])[0]


def _xa_bwd(name, x, dxo, gain, kx, vx, wq, wo):
    n_tok, tm = x.shape[0], min(ROW_BLOCK, x.shape[0])
    n_mem = kx.shape[0]

    def body(i, n, x_ref, dxo_ref, g_ref, kx_ref, vx_ref, wq_ref, wo_ref,
             dx_ref, h_ref, o_ref, dq_ref, dy_ref, dkx_ref, dvx_ref, dg_ref):
        g = g_ref[...]
        xh, r = _rms_stats(x_ref[...])
        h = (xh * g).astype(BF16)
        h_ref[...] = h
        q_b = _dot(h, wq_ref[...]).astype(BF16)
        dxo_v = dxo_ref[...]
        dy = dxo_v.astype(BF16)
        dy_ref[...] = dy
        do = _dot_nt(dy, wo_ref[...])
        for hd in range(XA_H):
            cols = slice(hd * XA_D, (hd + 1) * XA_D)
            p = _xa_probs(q_b, kx_ref, hd)
            p_b = p.astype(BF16)
            o_ref[:, cols] = _dot(p_b, vx_ref[:, cols]).astype(BF16)
            do_h = do[:, cols].astype(BF16)
            dvx_ref[:, cols] += _dot_tn(p_b, do_h)
            dp = _dot_nt(do_h, vx_ref[:, cols])
            ds = p * (dp - jnp.sum(dp * p, axis=1, keepdims=True))
            ds_b = (ds * (XA_D ** -0.5)).astype(BF16)
            dq_ref[:, cols] = _dot(ds_b, kx_ref[:, cols]).astype(BF16)
            dkx_ref[:, cols] += _dot_tn(ds_b, q_b[:, cols])
        dh = _dot_nt(dq_ref[...], wq_ref[...])
        dx, dg = _rms_bwd(xh, r, g, dh)
        dx_ref[...] = dxo_v + dx
        dg_ref[...] += dg

    return _rowcall(name, body, n_tok, tm, [x, dxo], [gain, kx, vx], [wq, wo],
                    [(D_MODEL, F32), (D_MODEL, BF16), (D_MODEL, BF16), (D_MODEL, BF16), (D_MODEL, BF16)],
                    [((n_mem, D_MODEL), F32), ((n_mem, D_MODEL), F32), ((1, D_MODEL), F32)])


def _loss_head(name, x, target, gain):
    n_tok, tm = x.shape[0], min(ROW_BLOCK, x.shape[0])

    def body(i, n, x_ref, t_ref, g_ref, dx_ref, loss_ref, dg_ref):
        g = g_ref[...]
        xh, r = _rms_stats(x_ref[...])
        err = xh * g - t_ref[...]
        loss_ref[...] += 0.5 * jnp.sum(jnp.sum(err * err, axis=1, keepdims=True) / D_MODEL, axis=0, keepdims=True)
        dx, dg = _rms_bwd(xh, r, g, err / D_MODEL)
        dx_ref[...] = dx
        dg_ref[...] += dg

    return _rowcall(name, body, n_tok, tm, [x, target], [gain], [], [(D_MODEL, F32)],
                    [((8, LANES), F32), ((1, D_MODEL), F32)])


def _adamw(name, w, g, m, v):
    rows, cols = w.shape
    tr = 128 if rows % 128 == 0 else rows

    def kern(w_ref, g_ref, m_ref, v_ref, d_ref, mo_ref, vo_ref):
        g_v = g_ref[...]
        m_new = ADAM_B1 * m_ref[...] + (1.0 - ADAM_B1) * g_v
        v_new = ADAM_B2 * v_ref[...] + (1.0 - ADAM_B2) * (g_v * g_v)
        m_hat = m_new / (1.0 - ADAM_B1 ** ADAM_STEP)
        v_hat = v_new / (1.0 - ADAM_B2 ** ADAM_STEP)
        d_ref[...] = -ADAM_LR * (m_hat / (jnp.sqrt(v_hat) + ADAM_EPS) + ADAM_WD * w_ref[...])
        mo_ref[...] = m_new
        vo_ref[...] = v_new

    spec = pl.BlockSpec((tr, cols), lambda i: (i, 0))
    shp = jax.ShapeDtypeStruct((rows, cols), F32)
    return pl.pallas_call(kern, name=name, grid=(rows // tr,), in_specs=[spec] * 4, out_specs=[spec] * 3,
                          out_shape=[shp] * 3, compiler_params=_params(1))(w, g, m, v)


def _add_pair(name, gp, got, core):
    _, n, rows, cols = gp.shape
    spec = pl.BlockSpec((1, PACK_ROW_BLOCK, cols), lambda j, r, core_ref: (j, r, 0))
    mine = pl.BlockSpec((1, PACK_ROW_BLOCK, cols), lambda j, r, core_ref: (core_ref[0] * n + j, r, 0))

    def kern(core_ref, a_ref, b_ref, o_ref):
        o_ref[...] = (a_ref[...].astype(F32) + b_ref[...].astype(F32)).astype(BF16)

    grid_spec = pltpu.PrefetchScalarGridSpec(num_scalar_prefetch=1, grid=(n, rows // PACK_ROW_BLOCK),
                                             in_specs=[mine, spec], out_specs=spec)
    return pl.pallas_call(kern, name=name, grid_spec=grid_spec, out_shape=jax.ShapeDtypeStruct(got.shape, BF16),
                          compiler_params=_params(2))(core, gp.reshape(2 * n, rows, cols), got)


def _sum_parts(name, s, got, chip):
    _, rows, cols = s.shape
    tr = PACK_ROW_BLOCK

    def kern(chip_ref, s_ref, g_ref, o_ref):
        acc = s_ref[0].astype(F32)
        for k in range(3):
            acc = acc + g_ref[k].astype(F32)
        o_ref[...] = acc

    grid_spec = pltpu.PrefetchScalarGridSpec(
        num_scalar_prefetch=1, grid=(rows // tr,),
        in_specs=[pl.BlockSpec((1, tr, cols), lambda r, chip_ref: (chip_ref[0], r, 0)),
                  pl.BlockSpec((3, tr, cols), lambda r, chip_ref: (0, r, 0))],
        out_specs=pl.BlockSpec((tr, cols), lambda r, chip_ref: (r, 0)))
    return pl.pallas_call(kern, name=name, grid_spec=grid_spec, out_shape=jax.ShapeDtypeStruct((rows, cols), F32),
                          compiler_params=_params(1))(chip, s, got)


def _sum_slots(name, a, tr):
    n, rows, cols = a.shape

    def kern(a_ref, o_ref):
        acc = a_ref[0].astype(F32)
        for k in range(1, n):
            acc = acc + a_ref[k].astype(F32)
        o_ref[...] = acc

    return pl.pallas_call(kern, name=name, grid=(rows // tr,),
                          in_specs=[pl.BlockSpec((n, tr, cols), lambda r: (0, r, 0))],
                          out_specs=pl.BlockSpec((tr, cols), lambda r: (r, 0)),
                          out_shape=jax.ShapeDtypeStruct((rows, cols), F32), compiler_params=_params(1))(a)


_ANY = pl.BlockSpec(memory_space=pl.ANY)


COMM_CHUNKS = 13


def _place():
    x, y, c = lax.axis_index("x"), lax.axis_index("y"), lax.axis_index("c")
    chips = [(1 - x, y), (x, 1 - y), (1 - x, 1 - y)]
    return x, y, c, chips


def _rows(ref, q, n):
    step = ref.shape[0] // n
    return ref.at[pl.ds(q * step, step)]


def _gather_chips(name, w):
    n = COMM_CHUNKS

    def kern(w_ref, o_ref, send_sems, recv_sems):
        x, y, c, _ = _place()
        me, here, sib = 2 * x + y, (x, y, c), (x, y, 1 - c)
        north = c == 1
        first_hop = (jnp.where(north, 1 - x, x), jnp.where(north, y, 1 - y))
        second_hop = (jnp.where(north, x, 1 - x), jnp.where(north, 1 - y, y))
        chip_a, chip_b, chip_d = 2 * first_hop[0] + first_hop[1], 2 * second_hop[0] + second_hop[1], 3 - me

        def copy(k, q, src, dst, to):
            return pltpu.make_async_remote_copy(src_ref=_rows(src, q, n), dst_ref=_rows(dst, q, n),
                                                send_sem=send_sems.at[k * n + q], recv_sem=recv_sems.at[k * n + q],
                                                device_id=to, device_id_type=MESH_ID)

        sent = [copy(k, q, w_ref.at[c], o_ref.at[me, c], (*hop, c))
                for k, hop in ((0, first_hop), (1, second_hop)) for q in range(n)]
        for cp in sent:
            cp.start()
        arrivals = ((0, chip_a, True), (1, chip_b, False), (2, chip_d, False))
        for k, chip, onward in arrivals:
            for q in range(n):
                landed = o_ref.at[chip, c]
                copy(k, q, landed, landed, here).wait_recv()
                if onward:
                    sent.append(copy(2, q, landed, landed, (*second_hop, c)))
                    sent[-1].start()
                sent.append(copy(3 + k, q, landed, landed, sib))
                sent[-1].start()
        for k, chip in ((3, chip_b), (4, chip_a), (5, chip_d)):
            for q in range(n):
                theirs = o_ref.at[chip, 1 - c]
                copy(k, q, theirs, theirs, here).wait_recv()
        for cp in sent:
            cp.wait_send()

    return pl.pallas_call(
        kern, name=name, in_specs=[_ANY], out_specs=_ANY,
        out_shape=jax.ShapeDtypeStruct((N_CHIPS,) + w.shape, w.dtype),
        scratch_shapes=[pltpu.SemaphoreType.DMA((6 * n,)), pltpu.SemaphoreType.DMA((6 * n,))],
        compiler_params=pltpu.CompilerParams(has_side_effects=True),
    )(w)


def _gather_all(name, v):
    def kern(v_ref, o_ref, send_sems, recv_sems, local_sem):
        x, y, c, _ = _place()
        me = 4 * x + 2 * y + c
        mine = pltpu.make_async_copy(v_ref, o_ref.at[me], local_sem)
        mine.start()
        peers = []
        for k in range(1, 8):
            px = 1 - x if k & 4 else x
            py = 1 - y if k & 2 else y
            pc = 1 - c if k & 1 else c
            peers.append((px, py, pc))
        copies = [pltpu.make_async_remote_copy(src_ref=v_ref, dst_ref=o_ref.at[me], send_sem=send_sems.at[k],
                                               recv_sem=recv_sems.at[k], device_id=p, device_id_type=MESH_ID)
                  for k, p in enumerate(peers)]
        for cp in copies:
            cp.start()
        for k, (px, py, pc) in enumerate(peers):
            pltpu.make_async_remote_copy(src_ref=v_ref, dst_ref=o_ref.at[4 * px + 2 * py + pc], send_sem=send_sems.at[k],
                                         recv_sem=recv_sems.at[k], device_id=(x, y, c), device_id_type=MESH_ID).wait_recv()
        for cp in copies:
            cp.wait_send()
        mine.wait()

    return pl.pallas_call(
        kern, name=name, in_specs=[_ANY], out_specs=_ANY, out_shape=jax.ShapeDtypeStruct((8,) + v.shape, v.dtype),
        scratch_shapes=[pltpu.SemaphoreType.DMA((7,)), pltpu.SemaphoreType.DMA((7,)), pltpu.SemaphoreType.DMA(())],
        compiler_params=pltpu.CompilerParams(has_side_effects=True),
    )(v)


def _swap_halves(name, gp):
    n = COMM_CHUNKS
    n_slots = gp.shape[1]

    def kern(g_ref, got_ref, send_sems, recv_sems):
        x, y, c, _ = _place()
        swaps = [pltpu.make_async_remote_copy(src_ref=_rows(g_ref.at[1 - c, j], q, n), dst_ref=_rows(got_ref.at[j], q, n),
                                              send_sem=send_sems.at[j * n + q], recv_sem=recv_sems.at[j * n + q],
                                              device_id=(x, y, 1 - c), device_id_type=MESH_ID)
                 for j in range(n_slots) for q in range(n)]
        for cp in swaps:
            cp.start()
        for cp in swaps:
            cp.wait()

    return pl.pallas_call(
        kern, name=name, in_specs=[_ANY], out_specs=_ANY, out_shape=jax.ShapeDtypeStruct(gp.shape[1:], gp.dtype),
        scratch_shapes=[pltpu.SemaphoreType.DMA((n_slots * n,)), pltpu.SemaphoreType.DMA((n_slots * n,))],
        compiler_params=pltpu.CompilerParams(has_side_effects=True),
    )(gp)


def _scatter_chips(name, s):
    n = COMM_CHUNKS

    def kern(s_ref, o_ref, send_sems, recv_sems):
        x, y, c, chips = _place()
        copies = [pltpu.make_async_remote_copy(src_ref=_rows(s_ref.at[2 * cx + cy], q, n), dst_ref=_rows(o_ref.at[k], q, n),
                                               send_sem=send_sems.at[k * n + q], recv_sem=recv_sems.at[k * n + q],
                                               device_id=(cx, cy, c), device_id_type=MESH_ID)
                  for k, (cx, cy) in enumerate(chips) for q in range(n)]
        for cp in copies:
            cp.start()
        for k in range(3):
            for q in range(n):
                landed = _rows(o_ref.at[k], q, n)
                pltpu.make_async_remote_copy(src_ref=landed, dst_ref=landed, send_sem=send_sems.at[k * n + q],
                                             recv_sem=recv_sems.at[k * n + q], device_id=(x, y, c),
                                             device_id_type=MESH_ID).wait_recv()
        for cp in copies:
            cp.wait_send()

    return pl.pallas_call(
        kern, name=name, in_specs=[_ANY], out_specs=_ANY, out_shape=jax.ShapeDtypeStruct((3,) + s.shape[1:], s.dtype),
        scratch_shapes=[pltpu.SemaphoreType.DMA((3 * n,)), pltpu.SemaphoreType.DMA((3 * n,))],
        compiler_params=pltpu.CompilerParams(has_side_effects=True),
    )(s)


def _join_halves(name, r):
    n = COMM_CHUNKS

    def kern(r_ref, o_ref, send_sems, recv_sems):
        x, y, c, _ = _place()
        swaps = [pltpu.make_async_remote_copy(src_ref=_rows(r_ref, q, n), dst_ref=_rows(o_ref, q, n),
                                              send_sem=send_sems.at[q], recv_sem=recv_sems.at[q],
                                              device_id=(x, y, 1 - c), device_id_type=MESH_ID) for q in range(n)]
        for cp in swaps:
            cp.start()
        for cp in swaps:
            cp.wait()

    return pl.pallas_call(
        kern, name=name, in_specs=[_ANY], out_specs=_ANY, out_shape=jax.ShapeDtypeStruct(r.shape, r.dtype),
        scratch_shapes=[pltpu.SemaphoreType.DMA((n,))] * 2, compiler_params=pltpu.CompilerParams(has_side_effects=True),
    )(r)


PACK_ALIGN = 16


def _exchange_rows(nm, shard_shape):
    if BIG_KIND[nm] == "cols":
        rows = shard_shape[-1]
    else:
        rows = 1
        for s in shard_shape[1:]:
            rows *= s
        rows //= PACK_COLS
    return -(-rows // PACK_ALIGN) * PACK_ALIGN


def _to_exchange(nm, a, transposed=False):
    if BIG_KIND[nm] == "cols":
        if not transposed:
            a = jnp.swapaxes(a, 1, 2)
        rows = -(-a.shape[1] // PACK_ALIGN) * PACK_ALIGN
        a = jnp.pad(a, ((0, 0), (0, rows - a.shape[1]), (0, 0)))
        return a.reshape(a.shape[0] * rows, PACK_COLS)
    return a.reshape(a.shape[0] * _exchange_rows(nm, a.shape), PACK_COLS)


def _from_exchange(nm, rows2d, shard_shape):
    lead = rows2d.shape[:-2]
    n_layers = shard_shape[0]
    if BIG_KIND[nm] == "cols":
        a = rows2d.reshape(lead + (n_layers, -1, PACK_COLS))
        return lax.slice_in_dim(a, 0, shard_shape[-1], axis=a.ndim - 2)
    return rows2d.reshape(lead + tuple(shard_shape))


def _pack(pieces, dtype):
    slab = jnp.concatenate([p.astype(dtype) for p in pieces])
    half = -(-slab.shape[0] // (2 * PACK_ROW_BLOCK)) * PACK_ROW_BLOCK
    slab = jnp.pad(slab, ((0, 2 * half - slab.shape[0]), (0, 0)))
    return slab.reshape(2, half, PACK_COLS)


def _unpack(packed, row_counts):
    lead = packed.shape[:-3]
    slab = packed.reshape(lead + (-1, PACK_COLS))
    out, off = [], 0
    for rows in row_counts:
        out.append(lax.slice_in_dim(slab, off, off + rows, axis=slab.ndim - 2))
        off += rows
    return out


def _t(a):
    return jnp.swapaxes(a, -1, -2)


def _rows_from_lanes(a):
    return a[:, :FOX_H].T[:, None, :]


def _lanes_from_heads(a):
    return jnp.pad(a[:, 0, :].T, ((0, 0), (0, LANES - FOX_H)))


def _layer_weights(wts, small, l):
    w_in_t = wts["w_in"][l]
    w = {
        "wf_t": jnp.pad(w_in_t[W1_COLS:W1_COLS + FOX_H], ((0, LANES - FOX_H), (0, 0))),
        "w1_t": w_in_t[:W1_COLS], "wgl_t": w_in_t[W1_COLS + FOX_H:],
    }
    for nm, _ in BIG:
        if nm != "w_in":
            w[nm + "_t" if BIG_KIND[nm] == "cols" else nm] = wts[nm][l]
    for nm in ("ffn1_norm", "mix_norm", "sg_norm", "xa_norm", "mem_norm", "ffn2_norm"):
        w[nm] = small[nm][l][None, :]
    w["conv_w"] = jnp.pad(small["conv_w"][l], ((0, 5), (0, 0)))
    w["sg_w"] = small["sg_w"][l]
    w["sg_w_t"] = _t(small["sg_w"][l])
    w["sg_bias"] = jnp.repeat(small["sg_b"][l].T, CHUNK, axis=1)
    w["fox_b"] = jnp.pad(small["fox_b_f"][l][None, :], ((0, 0), (0, LANES - FOX_H)))
    return w


def _layer_fwd(l, x, mem, w):
    s = {"x0": x}
    x1, s["a1"], s["b1"] = _ffn_fwd(f"ffn1_fwd_{l}", x, w["ffn1_norm"], w["ffn1_w_gate_t"], w["ffn1_w_up_t"],
                                    w["ffn1_w_down"])
    s["x1"] = x1
    cin, sgin, qkv, fl, gl = _proj_fwd(f"proj_fwd_{l}", x1, w["mix_norm"], w["w1_t"], w["wf_t"], w["wgl_t"])
    ya = _conv_fwd(f"conv_fwd_{l}", cin, w["conv_w"])
    yb = _sg_fwd(f"sg_fwd_{l}", sgin, w["sg_norm"], w["sg_w"], w["sg_bias"])
    c2 = _forget_cumsum(f"forget_fwd_{l}", fl, w["fox_b"])
    tables = _prune_tables(_flash_bounds(f"flash_bounds_{l}", qkv), c2, min(FLASH_BLOCK, qkv.shape[0]))
    yc, lse2 = _flash_fwd(f"flash_fwd_{l}", qkv, c2, _rows_from_lanes(c2), tables)
    x2 = _merge_fwd(f"merge_fwd_{l}", x1, ya, yb, yc, gl, w["w_branch"], w["w_out"])
    s.update(cin=cin, sgin=sgin, qkv=qkv, fl=fl, gl=gl, ya=ya, yb=yb, yc=yc, c2=c2, lse2=lse2, x2=x2, tables=tables)
    s["mn"], s["kx"], s["vx"] = _mem_fwd(f"mem_fwd_{l}", mem, w["mem_norm"], w["xa_wk"], w["xa_wv"])
    x3 = _xa_fwd(f"xa_fwd_{l}", x2, w["xa_norm"], s["kx"], s["vx"], w["xa_wq"], w["xa_wo"])
    s["x3"] = x3
    x4, s["a2"], s["b2"] = _ffn_fwd(f"ffn2_fwd_{l}", x3, w["ffn2_norm"], w["ffn2_w_gate_t"], w["ffn2_w_up_t"],
                                    w["ffn2_w_down"])
    return x4, s


def _ffn_grads(tag, x, dxo, a, b, w, pre):
    dx, da, db, sv, h, dy, dg = _ffn_bwd(f"{pre}_bwd_{tag}", x, dxo, a, b, w[pre + "_norm"], w[pre + "_w_down"],
                                         w[pre + "_w_gate_t"], w[pre + "_w_up_t"])
    g = {
        pre + "_norm": dg[0],
        pre + "_w_gate": _mm_tn(f"{pre}_dwg_{tag}", da, h, D_FF, D_MODEL, tk=FF_SPLIT, tn=D_MODEL),
        pre + "_w_up": _mm_tn(f"{pre}_dwu_{tag}", db, h, D_FF, D_MODEL, tk=FF_SPLIT, tn=D_MODEL),
        pre + "_w_down": _mm_tn(f"{pre}_dwd_{tag}", sv, dy, D_FF, D_MODEL, tk=FF_SPLIT, tn=D_MODEL),
    }
    return dx, g


def _layer_bwd(l, dx, mem, w, s):
    g = {}
    dx, gf = _ffn_grads(l, s["x3"], dx, s["a2"], s["b2"], w, "ffn2")
    g.update(gf)

    dx, h, o, dq, dy, dkx, dvx, dg = _xa_bwd(f"xa_bwd_{l}", s["x2"], dx, w["xa_norm"], s["kx"], s["vx"], w["xa_wq"],
                                              w["xa_wo"])
    g["xa_norm"] = dg[0]
    g["xa_wq"] = _mm_tn(f"xa_dwq_{l}", h, dq, D_MODEL, D_MODEL, tk=D_MODEL, tn=D_MODEL)
    g["xa_wo"] = _mm_tn(f"xa_dwo_{l}", o, dy, D_MODEL, D_MODEL, tk=D_MODEL, tn=D_MODEL)
    dkb, dvb, dgm = _mem_bwd(f"mem_bwd_{l}", mem, w["mem_norm"], dkx, dvx, w["xa_wk"], w["xa_wv"])
    g["mem_norm"] = dgm[0]
    g["xa_wk"] = _mm_tn(f"xa_dwk_{l}", s["mn"], dkb, D_MODEL, D_MODEL, tk=D_MODEL, tn=D_MODEL)
    g["xa_wv"] = _mm_tn(f"xa_dwv_{l}", s["mn"], dvb, D_MODEL, D_MODEL, tk=D_MODEL, tn=D_MODEL)

    mg, dob, dgl, dbr, dya, dyb, dyc = _merge_bwd(f"merge_bwd_{l}", dx, s["ya"], s["yb"], s["yc"], s["gl"], w["w_branch"],
                                                  w["w_out"])
    g["w_out"] = _mm_tn(f"dwout_{l}", mg, dob, D_MODEL, D_MODEL, tk=D_MODEL, tn=D_MODEL)
    g["w_branch"] = jnp.stack([
        _mm_tn(f"dwbranch{k}_{l}", y, dbr, CONV_W, D_MODEL, tk=CONV_W, tn=D_MODEL, y_off=k * D_MODEL)
        for k, y in enumerate((s["ya"], s["yb"], s["yc"]))])

    dcin, dcw = _conv_bwd(f"conv_bwd_{l}", s["cin"], dya, w["conv_w"])
    g["conv_w"] = dcw[:3]
    dsg, dgn, dsw, dsb = _sg_bwd(f"sg_bwd_{l}", s["sgin"], dyb, w["sg_norm"], w["sg_w"], w["sg_w_t"], w["sg_bias"])
    g["sg_norm"], g["sg_w"], g["sg_b"] = dgn[0], dsw, dsb[:, :, 0]
    d_row = _rows_from_lanes(_flash_delta(f"flash_delta_{l}", dyc, s["yc"]))
    r_row = s["lse2"].reshape(FOX_H, 1, -1) - _rows_from_lanes(s["c2"])
    dq, dk, dv, dc_rows = _flash_bwd(f"flash_bwd_{l}", s["qkv"], dyc, r_row, d_row, s["c2"], s["tables"])
    dc = _lanes_from_heads(dc_rows.reshape(FOX_H, 1, -1))
    dfl, dbf = _forget_bwd(f"forget_bwd_{l}", dc, s["fl"], w["fox_b"])
    g["fox_b_f"] = dbf[0, :FOX_H]

    dx, h, dqkv, dfb, dg = _proj_bwd(f"proj_bwd_{l}", s["x1"], dx, dcin, dsg, dq, dk, dv, dfl, dgl, w["mix_norm"],
                                     w["w1_t"], w["wf_t"], w["wgl_t"])
    g["mix_norm"] = dg[0]
    c0, c1 = 3 * CONV_W, 3 * CONV_W + 2 * SG_W
    g["w_in"] = jnp.concatenate([
        _mm_tn(f"dwin_conv_{l}", dcin, h, c0, D_MODEL, tk=c0, tn=D_MODEL),
        _mm_tn(f"dwin_sg_{l}", dsg, h, c1 - c0, D_MODEL, tk=c1 - c0, tn=D_MODEL),
        _mm_tn(f"dwin_qkv_{l}", dqkv, h, 3 * FOX_W, D_MODEL, tk=3 * FOX_W, tn=D_MODEL),
        _mm_tn(f"dwin_f_{l}", dfb, h, LANES, D_MODEL, tk=LANES, tn=D_MODEL)[:FOX_H],
        _mm_tn(f"dwin_gl_{l}", dgl, h, 3 * D_MODEL, D_MODEL, tk=3 * D_MODEL // 2, tn=D_MODEL),
    ], axis=0)

    dx, gf = _ffn_grads(l, s["x0"], dx, s["a1"], s["b1"], w, "ffn1")
    g.update(gf)
    return dx, g


def _local_step(x, mem, target, wts, small):
    saved, lw = [], []
    for l in range(DEPTH):
        lw.append(_layer_weights(wts, small, l))
        x, s = _layer_fwd(l, x, mem, lw[l])
        saved.append(s)
    fin = small["final_norm"][None, :]
    dx, loss, dgf = _loss_head("loss_head", x, target, fin)
    layer_grads = [None] * DEPTH
    for l in reversed(range(DEPTH)):
        dx, layer_grads[l] = _layer_bwd(l, dx, mem, lw[l], saved[l])
    grads = {nm: jnp.stack([layer_grads[l][nm] for l in range(DEPTH)]) for nm in WEIGHTS if nm != "final_norm"}
    grads["final_norm"] = dgf[0]
    return loss[0, 0], dx, grads


def _small_slab(vals):
    rows = []
    for v in vals:
        flat = v.astype(F32).reshape(-1)
        n = -(-flat.shape[0] // LANES) * LANES
        rows.append(jnp.pad(flat, (0, n - flat.shape[0])).reshape(-1, LANES))
    slab = jnp.concatenate(rows)
    pad = -slab.shape[0] % 8
    return jnp.pad(slab, ((0, pad), (0, 0)))


def _small_unslab(slab, shapes):
    out, off = [], 0
    for shp in shapes:
        size = 1
        for s in shp:
            size *= s
        n_rows = -(-size // LANES)
        out.append(slab[off:off + n_rows].reshape(-1)[:size].reshape(shp))
        off += n_rows
    return out


def kernel(x, mem, ffn1_norm, ffn1_w_gate, ffn1_w_up, ffn1_w_down, mix_norm, w_in, conv_w, sg_norm, sg_w, sg_b, fox_b_f, w_branch, w_out, xa_norm, mem_norm, xa_wq, xa_wk, xa_wv, xa_wo, ffn2_norm, ffn2_w_gate, ffn2_w_up, ffn2_w_down, final_norm, loss_target, m_ffn1_norm, m_ffn1_w_gate, m_ffn1_w_up, m_ffn1_w_down, m_mix_norm, m_w_in, m_conv_w, m_sg_norm, m_sg_w, m_sg_b, m_fox_b_f, m_w_branch, m_w_out, m_xa_norm, m_mem_norm, m_xa_wq, m_xa_wk, m_xa_wv, m_xa_wo, m_ffn2_norm, m_ffn2_w_gate, m_ffn2_w_up, m_ffn2_w_down, m_final_norm, v_ffn1_norm, v_ffn1_w_gate, v_ffn1_w_up, v_ffn1_w_down, v_mix_norm, v_w_in, v_conv_w, v_sg_norm, v_sg_w, v_sg_b, v_fox_b_f, v_w_branch, v_w_out, v_xa_norm, v_mem_norm, v_xa_wq, v_xa_wk, v_xa_wv, v_xa_wo, v_ffn2_norm, v_ffn2_w_gate, v_ffn2_w_up, v_ffn2_w_down, v_final_norm):
    args = dict(locals())
    wv = {nm: args[nm] for nm in WEIGHTS}
    mv = {nm: args["m_" + nm] for nm in WEIGHTS}
    vv = {nm: args["v_" + nm] for nm in WEIGHTS}
    chip = 2 * lax.axis_index("x") + lax.axis_index("y")
    core = lax.axis_index("c")

    names = [nm for nm, _ in BIG]
    mine = {nm: _to_exchange(nm, wv[nm].astype(BF16)) for nm in names}
    row_counts = [mine[nm].shape[0] for nm in names]
    gathered = _gather_chips("gather_weights", _pack([mine[nm] for nm in names], BF16))
    wts = {}
    for nm, slab in zip(names, _unpack(gathered, row_counts)):
        p = _from_exchange(nm, jnp.where((jnp.arange(N_CHIPS) == chip)[:, None, None], mine[nm][None], slab), wv[nm].shape)
        if BIG_KIND[nm] == "branch":
            wts[nm] = jnp.concatenate([p[j] for j in range(N_CHIPS)], axis=3)
        else:
            wts[nm] = jnp.swapaxes(p, 0, 1).reshape(p.shape[1], N_CHIPS * p.shape[2], PACK_COLS)
    taps = _gather_all("gather_taps", _small_slab([conv_w]))
    taps = [_small_unslab(taps[2 * j], [conv_w.shape])[0] for j in range(N_CHIPS)]
    small = {nm: wv[nm] for nm in SMALL}
    small["conv_w"] = jnp.concatenate(taps, axis=2)

    loss, dx, grads = _local_step(x[0], mem[0], loss_target[0], wts, small)
    loss = lax.psum(loss, ("x", "y", "c"))

    small_shapes = [grads[nm].shape for nm in SMALL]
    slots = _gather_all("gather_small_grads", _small_slab([grads[nm] for nm in SMALL]))
    small_sum = _sum_slots("sum_small_grads", slots, slots.shape[1])
    small_g = dict(zip(SMALL, _small_unslab(small_sum, small_shapes)))
    taps_g = small_g["conv_w"]
    small_g["conv_w"] = lax.dynamic_slice_in_dim(taps_g, chip * conv_w.shape[2], conv_w.shape[2], axis=2)

    def grad_shard(nm, ax, j):
        g = grads[nm]
        ax = 1 if BIG_KIND[nm] == "cols" else ax
        size = g.shape[ax] // N_CHIPS
        return _to_exchange(nm, lax.slice_in_dim(g, j * size, (j + 1) * size, axis=ax), transposed=True)

    gp = jnp.stack([_pack([grad_shard(nm, ax, j) for nm, ax in BIG], BF16) for j in range(N_CHIPS)], axis=1)
    got = _swap_halves("reduce_swap_cores", gp)
    chip_sum = _add_pair("reduce_add_cores", gp, got, core.astype(jnp.int32).reshape(1))
    parts = _scatter_chips("reduce_scatter_chips", chip_sum)
    half = _sum_parts("reduce_add_chips", chip_sum, parts, chip.astype(jnp.int32).reshape(1))
    other = _join_halves("reduce_join_cores", half)
    full = jnp.where(core == 0, jnp.stack([half, other]), jnp.stack([other, half]))
    big_g = {}
    for nm, slab in zip(names, _unpack(full, row_counts)):
        g = _from_exchange(nm, slab, wv[nm].shape)
        big_g[nm] = jnp.swapaxes(g, 1, 2) if BIG_KIND[nm] == "cols" else g

    g_out = {**small_g, **big_g}
    delta, new_m, new_v = {}, {}, {}
    for nm, _ in BIG:
        shp = wv[nm].shape
        two_d = (-1, shp[-1])
        d, m2, v2 = _adamw("adamw_" + nm, wv[nm].reshape(two_d), g_out[nm].reshape(two_d), mv[nm].reshape(two_d),
                           vv[nm].reshape(two_d))
        delta[nm], new_m[nm], new_v[nm] = d.reshape(shp), m2.reshape(shp), v2.reshape(shp)
    slab_shapes = [wv[nm].shape for nm in SMALL]
    d, m2, v2 = _adamw("adamw_small", _small_slab([wv[nm] for nm in SMALL]), _small_slab([g_out[nm] for nm in SMALL]),
                       _small_slab([mv[nm] for nm in SMALL]), _small_slab([vv[nm] for nm in SMALL]))
    for out, slab in ((delta, d), (new_m, m2), (new_v, v2)):
        out.update(zip(SMALL, _small_unslab(slab, slab_shapes)))

    return (loss, dx[None], *[g_out[nm] for nm in WEIGHTS], *[delta[nm] for nm in WEIGHTS],
            *[new_m[nm] for nm in WEIGHTS], *[new_v[nm] for nm in WEIGHTS])
```

```python
import functools

import jax
import jax.numpy as jnp
from jax import lax
from jax.experimental import pallas as pl
from jax.experimental.pallas import tpu as pltpu

F32, BF16 = jnp.float32, jnp.bfloat16
MESH_ID = pl.DeviceIdType.MESH

D_MODEL = 1024
DEPTH = 2
D_FF = 2816
CONV_W = 512
SG_W = 512
SG_G = 4
CHUNK = 128
FOX_H = 8
FOX_D = 64
FOX_W = FOX_H * FOX_D
FOX_SCALE = FOX_D ** -0.5
LOG2E = 1.4426950408889634
XA_H = 4
XA_D = D_MODEL // XA_H
N_CHIPS = 4
RMS_EPS = 1e-6
W1_COLS = 3 * CONV_W + 2 * SG_W + 3 * FOX_W
LANES = 128
HALO = 16

ADAM_LR, ADAM_B1, ADAM_B2, ADAM_EPS, ADAM_WD, ADAM_STEP = 0.001, 0.9, 0.999, 1e-08, 0.01, 10

ROW_BLOCK = 512
FFN_BWD_ROW_BLOCK = 256
MM_TT = 2048
FLASH_BLOCK = 512
FLASH_STRIP = 32
PRUNE_LOG2 = 40.0
FF_SPLIT = 1408
VMEM_LIMIT = 56 * 1024 * 1024
PACK_COLS = 1024
PACK_ROW_BLOCK = 208

BIG = (
    ("ffn1_w_gate", 2), ("ffn1_w_up", 2), ("ffn1_w_down", 1), ("w_in", 2), ("w_branch", 3), ("w_out", 1),
    ("xa_wq", 1), ("xa_wk", 1), ("xa_wv", 1), ("xa_wo", 1), ("ffn2_w_gate", 2), ("ffn2_w_up", 2), ("ffn2_w_down", 1),
)
BIG_KIND = {nm: ("branch" if nm == "w_branch" else "cols" if ax == 2 else "rows") for nm, ax in BIG}
SMALL = ("ffn1_norm", "mix_norm", "conv_w", "sg_norm", "sg_w", "sg_b", "fox_b_f", "xa_norm", "mem_norm", "ffn2_norm",
         "final_norm")
WEIGHTS = ("ffn1_norm", "ffn1_w_gate", "ffn1_w_up", "ffn1_w_down", "mix_norm", "w_in", "conv_w", "sg_norm", "sg_w",
           "sg_b", "fox_b_f", "w_branch", "w_out", "xa_norm", "mem_norm", "xa_wq", "xa_wk", "xa_wv", "xa_wo",
           "ffn2_norm", "ffn2_w_gate", "ffn2_w_up", "ffn2_w_down", "final_norm")


def _dot(a, b):
    return jnp.dot(a, b, preferred_element_type=F32)


def _dot_nt(a, b):
    return lax.dot_general(a, b, (((1,), (1,)), ((), ())), preferred_element_type=F32)


def _dot_tn(a, b):
    return lax.dot_general(a, b, (((0,), (0,)), ((), ())), preferred_element_type=F32)


def _rms_stats(x):
    r = lax.rsqrt(jnp.mean(x * x, axis=-1, keepdims=True) + RMS_EPS)
    return x * r, r


def _rms_bwd(xh, r, g, dy):
    dg = jnp.sum(dy * xh, axis=0, keepdims=True)
    dxh = dy * g
    dx = r * (dxh - xh * jnp.mean(dxh * xh, axis=-1, keepdims=True))
    return dx, dg


def _gelu(x):
    k = 0.7978845608028654
    t = jnp.tanh(k * (x + 0.044715 * x * x * x))
    return 0.5 * x * (1.0 + t), t


def _gelu_grad(x, t):
    k = 0.7978845608028654
    return 0.5 * (1.0 + t) + 0.5 * x * (1.0 - t * t) * k * (1.0 + 3.0 * 0.044715 * x * x)


def _split3_dot(tri, x):
    hi = x.astype(BF16)
    r1 = x - hi.astype(F32)
    mid = r1.astype(BF16)
    lo = (r1 - mid.astype(F32)).astype(BF16)
    return _dot(tri, hi) + _dot(tri, mid) + _dot(tri, lo)


def _params(n_grid):
    return pltpu.CompilerParams(dimension_semantics=("arbitrary",) * n_grid, vmem_limit_bytes=VMEM_LIMIT)


def _full_spec(shape):
    nd = len(shape)
    return pl.BlockSpec(tuple(shape), lambda *_: (0,) * nd)


def _rowcall(name, body, n_tokens, tm, rows, consts, residents, row_outs, acc_outs, scratch=()):
    n = n_tokens // tm
    rows = [r if isinstance(r, tuple) else (r, pl.BlockSpec((tm, r.shape[1]), lambda i: (i, 0))) for r in rows]
    n_in, n_w = len(rows) + len(consts), len(residents)
    n_out = len(row_outs) + len(acc_outs)
    in_specs = ([r[1] for r in rows] + [_full_spec(c.shape) for c in consts]
                + [pl.BlockSpec(memory_space=pl.ANY)] * n_w)
    out_shape = ([jax.ShapeDtypeStruct((n_tokens, c), dt) for c, dt in row_outs]
                 + [jax.ShapeDtypeStruct(s, dt) for s, dt in acc_outs])
    out_specs = ([pl.BlockSpec((tm, c), lambda i: (i, 0)) for c, _ in row_outs]
                 + [_full_spec(s) for s, _ in acc_outs])
    scratch_shapes = [pltpu.VMEM(w.shape, w.dtype) for w in residents]
    if n_w:
        scratch_shapes.append(pltpu.SemaphoreType.DMA((n_w,)))
    scratch_shapes += list(scratch)

    def kern(*refs):
        ins, w_hbm = refs[:n_in], refs[n_in:n_in + n_w]
        outs = refs[n_in + n_w:n_in + n_w + n_out]
        rest = refs[n_in + n_w + n_out:]
        w_vmem = rest[:n_w]
        extra = rest[n_w + 1:] if n_w else rest
        i = pl.program_id(0)

        @pl.when(i == 0)
        def _():
            copies = [pltpu.make_async_copy(w_hbm[k], w_vmem[k], rest[n_w].at[k]) for k in range(n_w)]
            for cp in copies:
                cp.start()
            for cp in copies:
                cp.wait()
            for a in outs[len(row_outs):]:
                a[...] = jnp.zeros(a.shape, a.dtype)

        body(i, n, *ins, *w_vmem, *outs, *extra)

    res = pl.pallas_call(
        kern, name=name, grid=(n,), in_specs=in_specs, out_specs=out_specs, out_shape=out_shape,
        scratch_shapes=scratch_shapes, compiler_params=_params(1),
    )(*[r[0] for r in rows], *consts, *residents)
    return res


def _mm_tn(name, x, y, k_dim, n_dim, *, tk, tn, x_off=0, y_off=0, tt=MM_TT):
    n_tok = x.shape[0]
    tt = min(tt, n_tok)
    n_t = n_tok // tt
    xb, yb = x_off // tk, y_off // tn

    def kern(x_ref, y_ref, o_ref):
        @pl.when(pl.program_id(2) == 0)
        def _():
            o_ref[...] = jnp.zeros(o_ref.shape, F32)

        o_ref[...] += _dot_tn(x_ref[...], y_ref[...])

    return pl.pallas_call(
        kern, name=name, grid=(k_dim // tk, n_dim // tn, n_t),
        in_specs=[pl.BlockSpec((tt, tk), lambda k, n, t: (t, xb + k)),
                  pl.BlockSpec((tt, tn), lambda k, n, t: (t, yb + n))],
        out_specs=pl.BlockSpec((tk, tn), lambda k, n, t: (k, n)),
        out_shape=jax.ShapeDtypeStruct((k_dim, n_dim), F32),
        compiler_params=_params(3),
    )(x, y)


def _ffn_fwd(name, x, gain, wg_t, wu_t, wd):
    n_tok, tm = x.shape[0], min(ROW_BLOCK, x.shape[0])

    def body(i, n, x_ref, g_ref, wg_ref, wu_ref, wd_ref, xo_ref, a_ref, b_ref):
        x_v = x_ref[...]
        xh, _ = _rms_stats(x_v)
        h = (xh * g_ref[...]).astype(BF16)
        y = jnp.zeros((tm, D_MODEL), F32)
        for f0 in range(0, D_FF, FF_SPLIT):
            sl = slice(f0, f0 + FF_SPLIT)
            a = _dot_nt(h, wg_ref[sl, :])
            b = _dot_nt(h, wu_ref[sl, :])
            a_ref[:, sl] = a.astype(BF16)
            b_ref[:, sl] = b.astype(BF16)
            s = (a * jax.nn.sigmoid(a) * b).astype(BF16)
            y = y + _dot(s, wd_ref[sl, :])
        xo_ref[...] = x_v + 0.5 * y

    return _rowcall(name, body, n_tok, tm, [x], [gain], [wg_t, wu_t, wd],
                    [(D_MODEL, F32), (D_FF, BF16), (D_FF, BF16)], [])


def _ffn_bwd(name, x, dxo, a, b, gain, wd, wg_t, wu_t):
    n_tok, tm = x.shape[0], min(FFN_BWD_ROW_BLOCK, x.shape[0])

    def body(i, n, x_ref, dxo_ref, a_ref, b_ref, g_ref, wdt_ref, wgt_ref, wut_ref,
             dx_ref, da_ref, db_ref, s_ref, h_ref, dy_ref, dg_ref):
        g = g_ref[...]
        xh, r = _rms_stats(x_ref[...])
        h_ref[...] = (xh * g).astype(BF16)
        dxo_v = dxo_ref[...]
        dy = (0.5 * dxo_v).astype(BF16)
        dy_ref[...] = dy
        dh = jnp.zeros((tm, D_MODEL), F32)
        for f0 in range(0, D_FF, FF_SPLIT):
            sl = slice(f0, f0 + FF_SPLIT)
            a_v = a_ref[:, sl].astype(F32)
            b_v = b_ref[:, sl].astype(F32)
            ds = _dot_nt(dy, wdt_ref[sl, :])
            sig = jax.nn.sigmoid(a_v)
            sa = a_v * sig
            s_ref[:, sl] = (sa * b_v).astype(BF16)
            da = (ds * b_v * (sig * (1.0 + a_v * (1.0 - sig)))).astype(BF16)
            db = (ds * sa).astype(BF16)
            da_ref[:, sl] = da
            db_ref[:, sl] = db
            dh = dh + _dot(da, wgt_ref[sl, :]) + _dot(db, wut_ref[sl, :])
        dx, dg = _rms_bwd(xh, r, g, dh)
        dx_ref[...] = dxo_v + dx
        dg_ref[...] += dg

    return _rowcall(name, body, n_tok, tm, [x, dxo, a, b], [gain], [wd, wg_t, wu_t],
                    [(D_MODEL, F32), (D_FF, BF16), (D_FF, BF16), (D_FF, BF16), (D_MODEL, BF16), (D_MODEL, BF16)],
                    [((1, D_MODEL), F32)])


def _proj_fwd(name, x, gain, w1_t, wf_t, wgl_t):
    n_tok, tm = x.shape[0], min(ROW_BLOCK, x.shape[0])
    c0, c1 = 3 * CONV_W, 3 * CONV_W + 2 * SG_W

    def body(i, n, x_ref, g_ref, w1_ref, wf_ref, wgl_ref, cin_ref, sg_ref, qkv_ref, fl_ref, gl_ref):
        xh, _ = _rms_stats(x_ref[...])
        h = (xh * g_ref[...]).astype(BF16)
        cin_ref[...] = _dot_nt(h, w1_ref[0:c0, :]).astype(BF16)
        sg_ref[...] = _dot_nt(h, w1_ref[c0:c1, :]).astype(BF16)
        qkv_ref[...] = _dot_nt(h, w1_ref[c1:W1_COLS, :]).astype(BF16)
        fl_ref[...] = _dot_nt(h, wf_ref[...])
        gl_ref[...] = _dot_nt(h, wgl_ref[...]).astype(BF16)

    return _rowcall(name, body, n_tok, tm, [x], [gain], [w1_t, wf_t, wgl_t],
                    [(3 * CONV_W, BF16), (2 * SG_W, BF16), (3 * FOX_W, BF16), (LANES, F32), (3 * D_MODEL, BF16)], [])


def _proj_bwd(name, x, dxin, dcin, dsg, dq, dk, dv, dfl, dgl, gain, w1_t, wf_t, wgl_t):
    n_tok, tm = x.shape[0], min(ROW_BLOCK, x.shape[0])
    c0, c1 = 3 * CONV_W, 3 * CONV_W + 2 * SG_W

    def body(i, n, x_ref, dxin_ref, dcin_ref, dsg_ref, dq_ref, dk_ref, dv_ref, dfl_ref, dgl_ref, g_ref,
             w1t_ref, wft_ref, wglt_ref, dx_ref, h_ref, dqkv_ref, dfb_ref, dg_ref):
        g = g_ref[...]
        xh, r = _rms_stats(x_ref[...])
        h_ref[...] = (xh * g).astype(BF16)
        dfb = dfl_ref[...].astype(BF16)
        dfb_ref[...] = dfb
        dh = _dot(dcin_ref[...], w1t_ref[0:c0, :])
        dh = dh + _dot(dsg_ref[...], w1t_ref[c0:c1, :])
        for k, d_ref in enumerate((dq_ref, dk_ref, dv_ref)):
            d_b = d_ref[...].astype(BF16)
            dqkv_ref[:, k * FOX_W:(k + 1) * FOX_W] = d_b
            dh = dh + _dot(d_b, w1t_ref[c1 + k * FOX_W:c1 + (k + 1) * FOX_W, :])
        dh = dh + _dot(dfb, wft_ref[...])
        dh = dh + _dot(dgl_ref[...], wglt_ref[...])
        dx, dg = _rms_bwd(xh, r, g, dh)
        dx_ref[...] = dxin_ref[...] + dx
        dg_ref[...] += dg

    return _rowcall(name, body, n_tok, tm, [x, dxin, dcin, dsg, dq, dk, dv, dfl, dgl], [gain], [w1_t, wf_t, wgl_t],
                    [(D_MODEL, F32), (D_MODEL, BF16), (3 * FOX_W, BF16), (LANES, BF16)], [((1, D_MODEL), F32)])


def _conv_taps(z, prev_z, i):
    tm = z.shape[0]
    row = lax.broadcasted_iota(jnp.int32, (tm, 1), 0)
    live = (i > 0).astype(F32)
    p1, p2 = prev_z[HALO - 1:HALO, :] * live, prev_z[HALO - 2:HALO - 1, :] * live
    z1 = jnp.where(row == 0, p1, pltpu.roll(z, 1, 0))
    z2 = jnp.where(row == 0, p2, jnp.where(row == 1, p1, pltpu.roll(z, 2, 0)))
    return z1, z2


def _prev_spec(tm, cols):
    return pl.BlockSpec((HALO, cols), lambda i: (jnp.maximum(i * (tm // HALO) - 1, 0), 0))


def _next_spec(tm, cols, n):
    last = n * (tm // HALO) - 1
    return pl.BlockSpec((HALO, cols), lambda i: (jnp.minimum((i + 1) * (tm // HALO), last), 0))


def _conv_fwd(name, cin, cw):
    n_tok, tm = cin.shape[0], min(ROW_BLOCK, cin.shape[0])
    w = CONV_W

    def body(i, n, c_ref, p_ref, cw_ref, ya_ref):
        c_v, p_v = c_ref[...].astype(F32), p_ref[...].astype(F32)
        z = c_v[:, w:2 * w] * c_v[:, 2 * w:]
        z1, z2 = _conv_taps(z, p_v[:, w:2 * w] * p_v[:, 2 * w:], i)
        y = cw_ref[0:1, :] * z2 + cw_ref[1:2, :] * z1 + cw_ref[2:3, :] * z
        ya_ref[...] = (c_v[:, 0:w] * y).astype(BF16)

    return _rowcall(name, body, n_tok, tm, [cin, (cin, _prev_spec(tm, 3 * w))], [cw], [], [(w, BF16)], [])[0]


def _conv_bwd(name, cin, dya, cw):
    n_tok, tm = cin.shape[0], min(ROW_BLOCK, cin.shape[0])
    w = CONV_W
    n_blocks = n_tok // tm

    def body(i, n, c_ref, p_ref, nx_ref, dya_ref, ndya_ref, cw_ref, dc_ref, dcw_ref):
        c_v, p_v = c_ref[...].astype(F32), p_ref[...].astype(F32)
        ab, ac, ah = c_v[:, 0:w], c_v[:, w:2 * w], c_v[:, 2 * w:]
        z = ac * ah
        z1, z2 = _conv_taps(z, p_v[:, w:2 * w] * p_v[:, 2 * w:], i)
        w0, w1, w2 = cw_ref[0:1, :], cw_ref[1:2, :], cw_ref[2:3, :]
        y = w0 * z2 + w1 * z1 + w2 * z
        dya_v = dya_ref[...].astype(F32)
        dy = dya_v * ab
        live = (i < n - 1).astype(F32)
        ndy = ndya_ref[...].astype(F32) * nx_ref[:, 0:w].astype(F32) * live
        row = lax.broadcasted_iota(jnp.int32, (tm, 1), 0)
        dy1 = jnp.where(row == tm - 1, ndy[0:1, :], pltpu.roll(dy, tm - 1, 0))
        dy2 = jnp.where(row == tm - 1, ndy[1:2, :], jnp.where(row == tm - 2, ndy[0:1, :], pltpu.roll(dy, tm - 2, 0)))
        dz = w2 * dy + w1 * dy1 + w0 * dy2
        dc_ref[:, 0:w] = (dya_v * y).astype(BF16)
        dc_ref[:, w:2 * w] = (dz * ah).astype(BF16)
        dc_ref[:, 2 * w:] = (dz * ac).astype(BF16)
        dcw_ref[0:1, :] += jnp.sum(dy * z2, axis=0, keepdims=True)
        dcw_ref[1:2, :] += jnp.sum(dy * z1, axis=0, keepdims=True)
        dcw_ref[2:3, :] += jnp.sum(dy * z, axis=0, keepdims=True)

    return _rowcall(name, body, n_tok, tm,
                    [cin, (cin, _prev_spec(tm, 3 * w)), (cin, _next_spec(tm, 3 * w, n_blocks)),
                     dya, (dya, _next_spec(tm, w, n_blocks))],
                    [cw], [], [(3 * w, BF16)], [((8, w), F32)])


def _sg_common(sg_ref, gn_ref):
    s_v = sg_ref[...].astype(F32)
    u, v = s_v[:, 0:SG_W], s_v[:, SG_W:]
    ug, tu = _gelu(u)
    vg, tv = _gelu(v)
    vh, r = _rms_stats(vg)
    vn = (vh * gn_ref[...]).astype(BF16)
    return u, v, ug, tu, tv, vh, r, vn


def _sg_fwd(name, sgin, gn, sgw, bias_full):
    n_tok, tm = sgin.shape[0], min(ROW_BLOCK, sgin.shape[0])

    def body(i, n, sg_ref, gn_ref, w_ref, bias_ref, yb_ref):
        _, _, ug, _, _, _, _, vn = _sg_common(sg_ref, gn_ref)
        tril = lax.broadcasted_iota(jnp.int32, (CHUNK, CHUNK), 0) >= lax.broadcasted_iota(jnp.int32, (CHUNK, CHUNK), 1)
        wt = [jnp.where(tril, w_ref[g], 0.0).astype(BF16) for g in range(SG_G)]
        for c0 in range(0, tm, CHUNK):
            sv = jnp.concatenate(
                [_dot(wt[g], vn[c0:c0 + CHUNK, g * CHUNK:(g + 1) * CHUNK]) for g in range(SG_G)], axis=1)
            sv = sv + bias_ref[...]
            yb_ref[c0:c0 + CHUNK, :] = (ug[c0:c0 + CHUNK, :] * sv).astype(BF16)

    return _rowcall(name, body, n_tok, tm, [sgin], [gn, sgw, bias_full], [], [(SG_W, BF16)], [])[0]


def _sg_bwd(name, sgin, dyb, gn, sgw, sgw_t, bias_full):
    n_tok, tm = sgin.shape[0], min(ROW_BLOCK, sgin.shape[0])

    def body(i, n, sg_ref, dyb_ref, gn_ref, w_ref, wt_ref, bias_ref, dsg_ref, dgn_ref, dw_ref, db_ref):
        u, v, ug, tu, tv, vh, r, vn = _sg_common(sg_ref, gn_ref)
        r0 = lax.broadcasted_iota(jnp.int32, (CHUNK, CHUNK), 0)
        r1 = lax.broadcasted_iota(jnp.int32, (CHUNK, CHUNK), 1)
        wt = [jnp.where(r0 >= r1, w_ref[g], 0.0).astype(BF16) for g in range(SG_G)]
        wtt = [jnp.where(r0 <= r1, wt_ref[g], 0.0).astype(BF16) for g in range(SG_G)]
        dyb_v = dyb_ref[...].astype(F32)
        dvn_rows = []
        for c0 in range(0, tm, CHUNK):
            rows = slice(c0, c0 + CHUNK)
            svs, dvns = [], []
            dsv = dyb_v[rows, :] * ug[rows, :]
            for g in range(SG_G):
                cols = slice(g * CHUNK, (g + 1) * CHUNK)
                svs.append(_dot(wt[g], vn[rows, cols]))
                dsv_g = dsv[:, cols]
                dsv_b = dsv_g.astype(BF16)
                dvns.append(_dot(wtt[g], dsv_b))
                dw_ref[g] += jnp.where(r0 >= r1, _dot_nt(dsv_b, vn[rows, cols]), 0.0)
                db_ref[g] += jnp.broadcast_to(jnp.sum(dsv_g, axis=1, keepdims=True), (CHUNK, CHUNK))
            sv = jnp.concatenate(svs, axis=1) + bias_ref[...]
            dug = dyb_v[rows, :] * sv
            dsg_ref[rows, 0:SG_W] = (dug * _gelu_grad(u[rows, :], tu[rows, :])).astype(BF16)
            dvn_rows.append(jnp.concatenate(dvns, axis=1))
        dvn = jnp.concatenate(dvn_rows, axis=0)
        dvg, dgn = _rms_bwd(vh, r, gn_ref[...], dvn)
        dsg_ref[:, SG_W:] = (dvg * _gelu_grad(v, tv)).astype(BF16)
        dgn_ref[...] += dgn

    return _rowcall(name, body, n_tok, tm, [sgin, dyb], [gn, sgw, sgw_t, bias_full], [], [(2 * SG_W, BF16)],
                    [((1, SG_W), F32), ((SG_G, CHUNK, CHUNK), F32), ((SG_G, CHUNK, CHUNK), F32)])


def _forget_cumsum(name, fl, bf):
    n_tok, tm = fl.shape[0], min(ROW_BLOCK, fl.shape[0])

    def body(i, n, fl_ref, b_ref, c_ref, carry):
        @pl.when(i == 0)
        def _():
            carry[...] = jnp.zeros(carry.shape, F32)

        z = fl_ref[...] + b_ref[...]
        lf = jnp.minimum(z, 0.0) - jnp.log1p(jnp.exp(-jnp.abs(z)))
        tri = (lax.broadcasted_iota(jnp.int32, (tm, tm), 0) >= lax.broadcasted_iota(jnp.int32, (tm, tm), 1)).astype(BF16)
        c = _split3_dot(tri, lf) + carry[...]
        c_ref[...] = c * LOG2E
        carry[...] = c[tm - 1:tm, :]

    return _rowcall(name, body, n_tok, tm, [fl], [bf], [], [(LANES, F32)], [], scratch=[pltpu.VMEM((1, LANES), F32)])[0]


def _forget_bwd(name, dc, fl, bf):
    n_tok, tm = fl.shape[0], min(ROW_BLOCK, fl.shape[0])
    n = n_tok // tm
    rev = pl.BlockSpec((tm, LANES), lambda i: (n - 1 - i, 0))

    def kern(dc_ref, fl_ref, b_ref, dfl_ref, db_ref, carry):
        @pl.when(pl.program_id(0) == 0)
        def _():
            carry[...] = jnp.zeros(carry.shape, F32)
            db_ref[...] = jnp.zeros(db_ref.shape, F32)

        triu = (lax.broadcasted_iota(jnp.int32, (tm, tm), 0) <= lax.broadcasted_iota(jnp.int32, (tm, tm), 1)).astype(BF16)
        dlf = _split3_dot(triu, dc_ref[...]) + carry[...]
        carry[...] = dlf[0:1, :]
        z = fl_ref[...] + b_ref[...]
        dfl = dlf * jax.nn.sigmoid(-z)
        dfl_ref[...] = dfl
        db_ref[...] += jnp.sum(dfl, axis=0, keepdims=True)

    return pl.pallas_call(
        kern, name=name, grid=(n,), in_specs=[rev, rev, _full_spec((1, LANES))],
        out_specs=[rev, _full_spec((1, LANES))],
        out_shape=[jax.ShapeDtypeStruct((n_tok, LANES), F32), jax.ShapeDtypeStruct((1, LANES), F32)],
        scratch_shapes=[pltpu.VMEM((1, LANES), F32)], compiler_params=_params(1),
    )(dc, fl, bf)


def _lane_pick(c_blk, h):
    lane = lax.broadcasted_iota(jnp.int32, c_blk.shape, 1)
    return jnp.broadcast_to(jnp.sum(jnp.where(lane == h, c_blk, 0.0), axis=1, keepdims=True), c_blk.shape)


def _wide(stat, width):
    return jnp.tile(stat, (1, width // LANES))


def _pair_half(rows):
    return lax.broadcasted_iota(jnp.int32, (rows, LANES), 1) // FOX_D


def _col_to_row(col):
    pick = (lax.broadcasted_iota(jnp.int32, (8, LANES), 1) == 0).astype(BF16)
    hi = col.astype(BF16)
    r1 = col - hi.astype(F32)
    mid = r1.astype(BF16)
    lo = (r1 - mid.astype(F32)).astype(BF16)
    return (_dot_nt(pick, hi) + _dot_nt(pick, mid) + _dot_nt(pick, lo))[0:1, :]


def _tri_table(nb, lower):
    rows = [(a, b) for a in range(nb) for b in (range(a + 1) if lower else range(a, nb))]
    return tuple(jnp.asarray([r[k] for r in rows], jnp.int32) for k in range(2))


def _flash_bounds(name, qkv):
    n_tok = qkv.shape[0]
    blk = min(FLASH_BLOCK, n_tok)

    def kern(x_ref, o_ref):
        head = (lax.broadcasted_iota(jnp.int32, (FOX_W, LANES), 0) // FOX_D
                == lax.broadcasted_iota(jnp.int32, (FOX_W, LANES), 1)).astype(BF16)
        q, k = x_ref[:, 0:FOX_W].astype(F32), x_ref[:, FOX_W:2 * FOX_W].astype(F32)
        qn = jnp.sqrt(jnp.max(_dot((q * q).astype(BF16), head) * 1.01, axis=0, keepdims=True))
        kn = jnp.sqrt(jnp.max(_dot((k * k).astype(BF16), head) * 1.01, axis=0, keepdims=True))
        diag = jnp.min(_dot((q * k).astype(BF16), head), axis=0, keepdims=True) - qn * kn * (2.0 ** -8)
        o_ref[...] = jnp.concatenate([qn, kn, diag, jnp.zeros((5, LANES), F32)], axis=0)

    return pl.pallas_call(
        kern, name=name, grid=(n_tok // blk,),
        in_specs=[pl.BlockSpec((blk, 2 * FOX_W), lambda b: (b, 0))], out_specs=pl.BlockSpec((8, LANES), lambda b: (b, 0)),
        out_shape=jax.ShapeDtypeStruct((n_tok // blk * 8, LANES), F32), compiler_params=_params(1),
    )(qkv)


def _prune_tables(bounds, c2, blk):
    nb = c2.shape[0] // blk
    tab = bounds.reshape(nb, 8, LANES)
    c_blocks = c2.reshape(nb, blk, LANES)
    return tuple(t[:, :FOX_H].reshape(-1)
                 for t in (tab[:, 0], tab[:, 1], tab[:, 2], c_blocks[:, 0], c_blocks[:, blk - 1]))


def _block_is_live(tables, p, i, j):
    qn, kn, diag, c_first, c_last = tables
    a = FOX_SCALE * LOG2E
    live = []
    for e in range(2):
        h = 2 * p + e
        qi, kj = FOX_H * i + h, FOX_H * j + h
        u = a * qn[qi] * kn[kj] + c_first[qi] - c_last[kj]
        lo = a * diag[qi]
        live.append((u - lo) > -PRUNE_LOG2)
    return live


def _flash_fwd(name, qkv, c2, ct2, tables):
    n_tok = qkv.shape[0]
    blk = min(FLASH_BLOCK, n_tok)
    nb = n_tok // blk
    n_pair = FOX_H // 2
    it, jt = _tri_table(nb, True)

    strip = min(FLASH_STRIP, blk)

    def kern(it_ref, jt_ref, qn_ref, kn_ref, dg_ref, cf_ref, cl_ref, q_ref, k_ref, v_ref, c_ref, ct_ref, o_ref, lse_ref,
             m_s, l_s, acc_s, cq_s, qm_s, al_s, s_s, pb_s):
        n = pl.program_id(1)
        i, j = it_ref[n], jt_ref[n]
        half = _pair_half(blk)
        live = _block_is_live((qn_ref, kn_ref, dg_ref, cf_ref, cl_ref), pl.program_id(0), i, j)

        @pl.when(j == 0)
        def _():
            m_s[...] = jnp.full(m_s.shape, -jnp.inf, F32)
            l_s[...] = jnp.zeros(l_s.shape, F32)
            acc_s[...] = jnp.zeros(acc_s.shape, F32)
            for e in range(2):
                cq_s[e] = _lane_pick(c_ref[...], 2 * pl.program_id(0) + e)
                qm_s[e] = jnp.where(half == e, q_ref[...], jnp.zeros_like(q_ref[...]))

        def step(heads, on_diagonal):
            k_v, v_v = k_ref[...], v_ref[...]
            for e in heads:
                s_s[e] = _dot_nt(qm_s[e], k_v)
            for e in heads:
                ck = ct_ref[e]
                for r0 in range(0, blk, strip):
                    rs = slice(r0, r0 + strip)
                    t = s_s[e, rs, :] * (FOX_SCALE * LOG2E) - ck
                    if on_diagonal:
                        keep = (lax.broadcasted_iota(jnp.int32, (strip, blk), 1)
                                <= lax.broadcasted_iota(jnp.int32, (strip, blk), 0) + r0)
                        t = jnp.where(keep, t, -jnp.inf)
                    m_old, cq = m_s[e, rs, :], cq_s[e, rs, :]
                    m_new = jnp.maximum(m_old, jnp.max(t, axis=1, keepdims=True) + cq)
                    p = jnp.exp2(t - _wide(m_new - cq, blk))
                    alpha = jnp.exp2(m_old - m_new)
                    l_s[e, rs, :] = alpha * l_s[e, rs, :] + jnp.sum(p, axis=1, keepdims=True)
                    m_s[e, rs, :] = m_new
                    al_s[e, rs, :] = alpha
                    pb_s[e, rs, :] = p.astype(BF16)
            if len(heads) == 2:
                pv = jnp.where(half == 0, _dot(pb_s[0], v_v), _dot(pb_s[1], v_v))
                acc_s[...] = jnp.where(half == 0, al_s[0], al_s[1]) * acc_s[...] + pv
            else:
                e = heads[0]
                acc_s[...] = jnp.where(half == e, al_s[e] * acc_s[...] + _dot(pb_s[e], v_v), acc_s[...])

        below = j < i

        @pl.when(below & live[0] & live[1])
        def _():
            step((0, 1), False)

        @pl.when(below & live[0] & jnp.logical_not(live[1]))
        def _():
            step((0,), False)

        @pl.when(below & jnp.logical_not(live[0]) & live[1])
        def _():
            step((1,), False)

        @pl.when(j == i)
        def _():
            step((0, 1), True)
            o_ref[...] = (acc_s[...] / jnp.where(half == 0, l_s[0], l_s[1])).astype(BF16)
            rows = pl.ds(pl.multiple_of(i * blk, blk), blk)
            lse_ref[0, :, rows] = jnp.concatenate(
                [_col_to_row(m_s[e] + jnp.log(l_s[e]) * LOG2E) for e in range(2)], axis=0)

    def key_block(p, n, it_r, jt_r, tabs):
        i, j = it_r[n], jt_r[n]
        live = _block_is_live(tabs, p, i, j)
        return jnp.where(live[0] | live[1], j, i)

    def col_block(first, inner):
        def index(p, n, it_r, jt_r, *tabs):
            return ((key_block(p, n, it_r, jt_r, tabs) if inner else it_r[n]), first + p)
        return pl.BlockSpec((blk, LANES), index)

    pair_stat = (2, blk, LANES)
    grid_spec = pltpu.PrefetchScalarGridSpec(
        num_scalar_prefetch=2 + len(tables), grid=(n_pair, int(it.shape[0])),
        in_specs=[col_block(0, False), col_block(n_pair, True), col_block(2 * n_pair, True),
                  pl.BlockSpec((blk, LANES), lambda p, n, it_r, jt_r, *_: (it_r[n], 0)),
                  pl.BlockSpec((2, 1, blk), lambda p, n, it_r, jt_r, *tabs: (p, 0, key_block(p, n, it_r, jt_r, tabs)))],
        out_specs=[col_block(0, False), pl.BlockSpec((1, 2, n_tok), lambda p, n, it_r, jt_r, *_: (p, 0, 0))],
        scratch_shapes=[pltpu.VMEM(pair_stat, F32), pltpu.VMEM(pair_stat, F32), pltpu.VMEM((blk, LANES), F32),
                        pltpu.VMEM(pair_stat, F32), pltpu.VMEM(pair_stat, BF16), pltpu.VMEM(pair_stat, F32),
                        pltpu.VMEM((2, blk, blk), F32), pltpu.VMEM((2, blk, blk), BF16)],
    )
    return pl.pallas_call(
        kern, name=name, grid_spec=grid_spec,
        out_shape=[jax.ShapeDtypeStruct((n_tok, FOX_W), BF16), jax.ShapeDtypeStruct((n_pair, 2, n_tok), F32)],
        compiler_params=_params(2),
    )(it, jt, *tables, qkv, qkv, qkv, c2, ct2)


def _flash_delta(name, do, o):
    n_tok, tm = do.shape[0], min(ROW_BLOCK, do.shape[0])

    def body(i, n, do_ref, o_ref, d_ref):
        prod = do_ref[...].astype(F32) * o_ref[...].astype(F32)
        head = (lax.broadcasted_iota(jnp.int32, (FOX_W, LANES), 0) // FOX_D
                == lax.broadcasted_iota(jnp.int32, (FOX_W, LANES), 1)).astype(BF16)
        hi = prod.astype(BF16)
        r1 = prod - hi.astype(F32)
        mid = r1.astype(BF16)
        lo = (r1 - mid.astype(F32)).astype(BF16)
        d_ref[...] = _dot(hi, head) + _dot(mid, head) + _dot(lo, head)

    return _rowcall(name, body, n_tok, tm, [do, o], [], [], [(LANES, F32)], [])[0]


def _flash_bwd(name, qkv, do, r_row, d_row, c2, tables):
    n_tok = qkv.shape[0]
    blk = min(FLASH_BLOCK, n_tok)
    nb = n_tok // blk
    n_pair = FOX_H // 2
    jt, it = _tri_table(nb, False)

    def kern(jt_ref, it_ref, qn_ref, kn_ref, dg_ref, cf_ref, cl_ref, k_ref, v_ref, q_ref, do_ref, rr_ref, dr_ref, c_ref,
             dq_ref, dk_ref, dv_ref, dc_ref, km_s, vm_s, ck_s, dck_s):
        n = pl.program_id(1)
        j, i = jt_ref[n], it_ref[n]
        first = _pair_half(blk) == 0
        live = _block_is_live((qn_ref, kn_ref, dg_ref, cf_ref, cl_ref), pl.program_id(0), i, j)

        @pl.when(n == 0)
        def _():
            dq_ref[...] = jnp.zeros(dq_ref.shape, F32)
            dc_ref[...] = jnp.zeros(dc_ref.shape, F32)

        @pl.when(i == j)
        def _():
            dk_ref[...] = jnp.zeros(dk_ref.shape, F32)
            dv_ref[...] = jnp.zeros(dv_ref.shape, F32)
            dck_s[...] = jnp.zeros(dck_s.shape, F32)
            for e in range(2):
                mine = first if e == 0 else jnp.logical_not(first)
                km_s[e] = jnp.where(mine, k_ref[...], jnp.zeros_like(k_ref[...]))
                vm_s[e] = jnp.where(mine, v_ref[...], jnp.zeros_like(v_ref[...]))
                ck_s[e] = _lane_pick(c_ref[...], 2 * pl.program_id(0) + e)

        def step(heads, on_diagonal):
            q_v, do_v, k_v = q_ref[...], do_ref[...], k_ref[...]
            dvs, dks, dqs, sums = {}, {}, {}, {}
            for e in heads:
                t = _dot_nt(km_s[e], q_v) * (FOX_SCALE * LOG2E) - _wide(ck_s[e], blk) - rr_ref[e]
                if on_diagonal:
                    keep = (lax.broadcasted_iota(jnp.int32, (blk, blk), 1)
                            >= lax.broadcasted_iota(jnp.int32, (blk, blk), 0))
                    t = jnp.where(keep, t, -jnp.inf)
                p = jnp.exp2(t)
                ds = p * (_dot_nt(vm_s[e], do_v) - dr_ref[e])
                p_b, ds_b = p.astype(BF16), ds.astype(BF16)
                dvs[e] = _dot(p_b, do_v)
                dks[e] = _dot(ds_b, q_v)
                dqs[e] = _dot_tn(ds_b, k_v)
                dck_s[e] += jnp.broadcast_to(jnp.sum(ds, axis=1, keepdims=True), (blk, LANES))
                sums[e] = jnp.sum(ds, axis=0, keepdims=True)
            rows = pl.ds(pl.multiple_of(i * blk, blk), blk)

            def merged(parts):
                zero = jnp.zeros((blk, LANES), F32)
                return jnp.where(first, parts.get(0, zero), parts.get(1, zero))

            dv_ref[...] += merged(dvs)
            dk_ref[...] += merged(dks) * FOX_SCALE
            dq_ref[rows, :] += merged(dqs) * FOX_SCALE
            none = jnp.zeros((1, blk), F32)
            dc_ref[0, :, rows] += jnp.concatenate([sums.get(0, none), sums.get(1, none)], axis=0)

        above = i > j

        @pl.when(above & live[0] & live[1])
        def _():
            step((0, 1), False)

        @pl.when(above & live[0] & jnp.logical_not(live[1]))
        def _():
            step((0,), False)

        @pl.when(above & jnp.logical_not(live[0]) & live[1])
        def _():
            step((1,), False)

        @pl.when(i == j)
        def _():
            step((0, 1), True)

        @pl.when(i == nb - 1)
        def _():
            keys = pl.ds(pl.multiple_of(j * blk, blk), blk)
            dc_ref[0, :, keys] -= jnp.concatenate([_col_to_row(dck_s[e]) for e in range(2)], axis=0)

    def query_block(p, n, jt_r, it_r, tabs):
        j, i = jt_r[n], it_r[n]
        live = _block_is_live(tabs, p, i, j)
        return jnp.where(live[0] | live[1] | (i == j), i, nb - 1)

    def col_block(first_col, inner):
        def index(p, n, jt_r, it_r, *tabs):
            return ((query_block(p, n, jt_r, it_r, tabs) if inner else jt_r[n]), first_col + p)
        return pl.BlockSpec((blk, LANES), index)

    def row_stat():
        return pl.BlockSpec((2, 1, blk), lambda p, n, jt_r, it_r, *tabs: (p, 0, query_block(p, n, jt_r, it_r, tabs)))

    pair_stat = (2, blk, LANES)
    grid_spec = pltpu.PrefetchScalarGridSpec(
        num_scalar_prefetch=2 + len(tables), grid=(n_pair, int(jt.shape[0])),
        in_specs=[col_block(n_pair, False), col_block(2 * n_pair, False), col_block(0, True), col_block(0, True),
                  row_stat(), row_stat(), pl.BlockSpec((blk, LANES), lambda p, n, jt_r, it_r, *_: (jt_r[n], 0))],
        out_specs=[pl.BlockSpec((n_tok, LANES), lambda p, n, jt_r, it_r, *_: (0, p)),
                   col_block(0, False), col_block(0, False),
                   pl.BlockSpec((1, 2, n_tok), lambda p, n, jt_r, it_r, *_: (p, 0, 0))],
        scratch_shapes=[pltpu.VMEM(pair_stat, BF16), pltpu.VMEM(pair_stat, BF16), pltpu.VMEM(pair_stat, F32),
                        pltpu.VMEM(pair_stat, F32)],
    )
    wide = jax.ShapeDtypeStruct((n_tok, FOX_W), F32)
    return pl.pallas_call(
        kern, name=name, grid_spec=grid_spec,
        out_shape=[wide, wide, wide, jax.ShapeDtypeStruct((n_pair, 2, n_tok), F32)],
        compiler_params=_params(2),
    )(jt, it, *tables, qkv, qkv, qkv, do, r_row, d_row, c2)


def _merge_fwd(name, x, ya, yb, yc, gl, wb, wo):
    n_tok, tm = x.shape[0], min(ROW_BLOCK, x.shape[0])

    def body(i, n, x_ref, ya_ref, yb_ref, yc_ref, gl_ref, wb_ref, wo_ref, xo_ref):
        merged = jnp.zeros((tm, D_MODEL), F32)
        for k, y_ref in enumerate((ya_ref, yb_ref, yc_ref)):
            gate = jax.nn.sigmoid(gl_ref[:, k * D_MODEL:(k + 1) * D_MODEL].astype(F32))
            merged = merged + gate * _dot(y_ref[...], wb_ref[k])
        xo_ref[...] = x_ref[...] + _dot(merged.astype(BF16), wo_ref[...])

    return _rowcall(name, body, n_tok, tm, [x, ya, yb, yc, gl], [], [wb, wo], [(D_MODEL, F32)], [])[0]


def _merge_bwd(name, dxo, ya, yb, yc, gl, wb, wo):
    n_tok, tm = dxo.shape[0], min(ROW_BLOCK, dxo.shape[0])

    def body(i, n, dxo_ref, ya_ref, yb_ref, yc_ref, gl_ref, wb_ref, wo_ref,
             mg_ref, dob_ref, dgl_ref, dbr_ref, dya_ref, dyb_ref, dyc_ref):
        dob = dxo_ref[...].astype(BF16)
        dob_ref[...] = dob
        dm = _dot_nt(dob, wo_ref[...])
        merged = jnp.zeros((tm, D_MODEL), F32)
        for k, (y_ref, dy_ref) in enumerate(((ya_ref, dya_ref), (yb_ref, dyb_ref), (yc_ref, dyc_ref))):
            cols = slice(k * D_MODEL, (k + 1) * D_MODEL)
            gate = jax.nn.sigmoid(gl_ref[:, cols].astype(F32))
            br = _dot(y_ref[...], wb_ref[k])
            merged = merged + gate * br
            dgl_ref[:, cols] = (dm * br * gate * (1.0 - gate)).astype(BF16)
            dbr = (dm * gate).astype(BF16)
            dbr_ref[:, cols] = dbr
            dy_ref[...] = _dot_nt(dbr, wb_ref[k]).astype(BF16)
        mg_ref[...] = merged.astype(BF16)

    return _rowcall(name, body, n_tok, tm, [dxo, ya, yb, yc, gl], [], [wb, wo],
                    [(D_MODEL, BF16), (D_MODEL, BF16), (3 * D_MODEL, BF16), (3 * D_MODEL, BF16),
                     (CONV_W, BF16), (SG_W, BF16), (FOX_W, BF16)], [])


def _mem_fwd(name, mem, gain, wk, wv):
    n_mem = mem.shape[0]

    def kern(m_ref, g_ref, wk_ref, wv_ref, mn_ref, k_ref, v_ref):
        mh, _ = _rms_stats(m_ref[...])
        mn = (mh * g_ref[...]).astype(BF16)
        mn_ref[...] = mn
        k_ref[...] = _dot(mn, wk_ref[...]).astype(BF16)
        v_ref[...] = _dot(mn, wv_ref[...]).astype(BF16)

    shp = jax.ShapeDtypeStruct((n_mem, D_MODEL), BF16)
    return pl.pallas_call(
        kern, name=name, grid=(1,),
        in_specs=[_full_spec(mem.shape), _full_spec(gain.shape), _full_spec(wk.shape), _full_spec(wv.shape)],
        out_specs=[_full_spec(shp.shape)] * 3, out_shape=[shp] * 3, compiler_params=_params(1),
    )(mem, gain, wk, wv)


def _mem_bwd(name, mem, gain, dkx, dvx, wk, wv):
    n_mem = mem.shape[0]

    def kern(m_ref, g_ref, dk_ref, dv_ref, wk_ref, wv_ref, dkb_ref, dvb_ref, dg_ref):
        mh, _ = _rms_stats(m_ref[...])
        dkb, dvb = dk_ref[...].astype(BF16), dv_ref[...].astype(BF16)
        dkb_ref[...] = dkb
        dvb_ref[...] = dvb
        dm = _dot_nt(dkb, wk_ref[...]) + _dot_nt(dvb, wv_ref[...])
        dg_ref[...] = jnp.sum(dm * mh, axis=0, keepdims=True)

    shp = jax.ShapeDtypeStruct((n_mem, D_MODEL), BF16)
    args = (mem, gain, dkx, dvx, wk, wv)
    return pl.pallas_call(
        kern, name=name, grid=(1,), in_specs=[_full_spec(a.shape) for a in args],
        out_specs=[_full_spec(shp.shape)] * 2 + [_full_spec((1, D_MODEL))],
        out_shape=[shp, shp, jax.ShapeDtypeStruct((1, D_MODEL), F32)], compiler_params=_params(1),
    )(*args)


def _xa_probs(q_b, kx_ref, hd):
    cols = slice(hd * XA_D, (hd + 1) * XA_D)
    s = _dot_nt(q_b[:, cols], kx_ref[:, cols]) * (XA_D ** -0.5)
    e = jnp.exp(s - jnp.max(s, axis=1, keepdims=True))
    return e / jnp.sum(e, axis=1, keepdims=True)


def _xa_fwd(name, x, gain, kx, vx, wq, wo):
    n_tok, tm = x.shape[0], min(ROW_BLOCK, x.shape[0])

    def body(i, n, x_ref, g_ref, kx_ref, vx_ref, wq_ref, wo_ref, xo_ref):
        x_v = x_ref[...]
        xh, _ = _rms_stats(x_v)
        q_b = _dot((xh * g_ref[...]).astype(BF16), wq_ref[...]).astype(BF16)
        o = jnp.concatenate(
            [_dot(_xa_probs(q_b, kx_ref, hd).astype(BF16), vx_ref[:, hd * XA_D:(hd + 1) * XA_D]) for hd in range(XA_H)],
            axis=1)
        xo_ref[...] = x_v + _dot(o.astype(BF16), wo_ref[...])

    return _rowcall(name, body, n_tok, tm, [x], [gain, kx, vx], [wq, wo], [(D_MODEL, F32)], [])[0]


def _xa_bwd(name, x, dxo, gain, kx, vx, wq, wo):
    n_tok, tm = x.shape[0], min(ROW_BLOCK, x.shape[0])
    n_mem = kx.shape[0]

    def body(i, n, x_ref, dxo_ref, g_ref, kx_ref, vx_ref, wq_ref, wo_ref,
             dx_ref, h_ref, o_ref, dq_ref, dy_ref, dkx_ref, dvx_ref, dg_ref):
        g = g_ref[...]
        xh, r = _rms_stats(x_ref[...])
        h = (xh * g).astype(BF16)
        h_ref[...] = h
        q_b = _dot(h, wq_ref[...]).astype(BF16)
        dxo_v = dxo_ref[...]
        dy = dxo_v.astype(BF16)
        dy_ref[...] = dy
        do = _dot_nt(dy, wo_ref[...])
        for hd in range(XA_H):
            cols = slice(hd * XA_D, (hd + 1) * XA_D)
            p = _xa_probs(q_b, kx_ref, hd)
            p_b = p.astype(BF16)
            o_ref[:, cols] = _dot(p_b, vx_ref[:, cols]).astype(BF16)
            do_h = do[:, cols].astype(BF16)
            dvx_ref[:, cols] += _dot_tn(p_b, do_h)
            dp = _dot_nt(do_h, vx_ref[:, cols])
            ds = p * (dp - jnp.sum(dp * p, axis=1, keepdims=True))
            ds_b = (ds * (XA_D ** -0.5)).astype(BF16)
            dq_ref[:, cols] = _dot(ds_b, kx_ref[:, cols]).astype(BF16)
            dkx_ref[:, cols] += _dot_tn(ds_b, q_b[:, cols])
        dh = _dot_nt(dq_ref[...], wq_ref[...])
        dx, dg = _rms_bwd(xh, r, g, dh)
        dx_ref[...] = dxo_v + dx
        dg_ref[...] += dg

    return _rowcall(name, body, n_tok, tm, [x, dxo], [gain, kx, vx], [wq, wo],
                    [(D_MODEL, F32), (D_MODEL, BF16), (D_MODEL, BF16), (D_MODEL, BF16), (D_MODEL, BF16)],
                    [((n_mem, D_MODEL), F32), ((n_mem, D_MODEL), F32), ((1, D_MODEL), F32)])


def _loss_head(name, x, target, gain):
    n_tok, tm = x.shape[0], min(ROW_BLOCK, x.shape[0])

    def body(i, n, x_ref, t_ref, g_ref, dx_ref, loss_ref, dg_ref):
        g = g_ref[...]
        xh, r = _rms_stats(x_ref[...])
        err = xh * g - t_ref[...]
        loss_ref[...] += 0.5 * jnp.sum(jnp.sum(err * err, axis=1, keepdims=True) / D_MODEL, axis=0, keepdims=True)
        dx, dg = _rms_bwd(xh, r, g, err / D_MODEL)
        dx_ref[...] = dx
        dg_ref[...] += dg

    return _rowcall(name, body, n_tok, tm, [x, target], [gain], [], [(D_MODEL, F32)],
                    [((8, LANES), F32), ((1, D_MODEL), F32)])


def _adamw(name, w, g, m, v):
    rows, cols = w.shape
    tr = 128 if rows % 128 == 0 else rows

    def kern(w_ref, g_ref, m_ref, v_ref, d_ref, mo_ref, vo_ref):
        g_v = g_ref[...]
        m_new = ADAM_B1 * m_ref[...] + (1.0 - ADAM_B1) * g_v
        v_new = ADAM_B2 * v_ref[...] + (1.0 - ADAM_B2) * (g_v * g_v)
        m_hat = m_new / (1.0 - ADAM_B1 ** ADAM_STEP)
        v_hat = v_new / (1.0 - ADAM_B2 ** ADAM_STEP)
        d_ref[...] = -ADAM_LR * (m_hat / (jnp.sqrt(v_hat) + ADAM_EPS) + ADAM_WD * w_ref[...])
        mo_ref[...] = m_new
        vo_ref[...] = v_new

    spec = pl.BlockSpec((tr, cols), lambda i: (i, 0))
    shp = jax.ShapeDtypeStruct((rows, cols), F32)
    return pl.pallas_call(kern, name=name, grid=(rows // tr,), in_specs=[spec] * 4, out_specs=[spec] * 3,
                          out_shape=[shp] * 3, compiler_params=_params(1))(w, g, m, v)


def _add_pair(name, gp, got, core):
    _, n, rows, cols = gp.shape
    spec = pl.BlockSpec((1, PACK_ROW_BLOCK, cols), lambda j, r, core_ref: (j, r, 0))
    mine = pl.BlockSpec((1, PACK_ROW_BLOCK, cols), lambda j, r, core_ref: (core_ref[0] * n + j, r, 0))

    def kern(core_ref, a_ref, b_ref, o_ref):
        o_ref[...] = (a_ref[...].astype(F32) + b_ref[...].astype(F32)).astype(BF16)

    grid_spec = pltpu.PrefetchScalarGridSpec(num_scalar_prefetch=1, grid=(n, rows // PACK_ROW_BLOCK),
                                             in_specs=[mine, spec], out_specs=spec)
    return pl.pallas_call(kern, name=name, grid_spec=grid_spec, out_shape=jax.ShapeDtypeStruct(got.shape, BF16),
                          compiler_params=_params(2))(core, gp.reshape(2 * n, rows, cols), got)


def _sum_parts(name, s, got, chip):
    _, rows, cols = s.shape
    tr = PACK_ROW_BLOCK

    def kern(chip_ref, s_ref, g_ref, o_ref):
        acc = s_ref[0].astype(F32)
        for k in range(3):
            acc = acc + g_ref[k].astype(F32)
        o_ref[...] = acc

    grid_spec = pltpu.PrefetchScalarGridSpec(
        num_scalar_prefetch=1, grid=(rows // tr,),
        in_specs=[pl.BlockSpec((1, tr, cols), lambda r, chip_ref: (chip_ref[0], r, 0)),
                  pl.BlockSpec((3, tr, cols), lambda r, chip_ref: (0, r, 0))],
        out_specs=pl.BlockSpec((tr, cols), lambda r, chip_ref: (r, 0)))
    return pl.pallas_call(kern, name=name, grid_spec=grid_spec, out_shape=jax.ShapeDtypeStruct((rows, cols), F32),
                          compiler_params=_params(1))(chip, s, got)


def _sum_slots(name, a, tr):
    n, rows, cols = a.shape

    def kern(a_ref, o_ref):
        acc = a_ref[0].astype(F32)
        for k in range(1, n):
            acc = acc + a_ref[k].astype(F32)
        o_ref[...] = acc

    return pl.pallas_call(kern, name=name, grid=(rows // tr,),
                          in_specs=[pl.BlockSpec((n, tr, cols), lambda r: (0, r, 0))],
                          out_specs=pl.BlockSpec((tr, cols), lambda r: (r, 0)),
                          out_shape=jax.ShapeDtypeStruct((rows, cols), F32), compiler_params=_params(1))(a)


_ANY = pl.BlockSpec(memory_space=pl.ANY)


COMM_CHUNKS = 13


def _place():
    x, y, c = lax.axis_index("x"), lax.axis_index("y"), lax.axis_index("c")
    chips = [(1 - x, y), (x, 1 - y), (1 - x, 1 - y)]
    return x, y, c, chips


def _rows(ref, q, n):
    step = ref.shape[0] // n
    return ref.at[pl.ds(q * step, step)]


def _gather_chips(name, w):
    n = COMM_CHUNKS

    def kern(w_ref, o_ref, send_sems, recv_sems):
        x, y, c, _ = _place()
        me, here, sib = 2 * x + y, (x, y, c), (x, y, 1 - c)
        north = c == 1
        first_hop = (jnp.where(north, 1 - x, x), jnp.where(north, y, 1 - y))
        second_hop = (jnp.where(north, x, 1 - x), jnp.where(north, 1 - y, y))
        chip_a, chip_b, chip_d = 2 * first_hop[0] + first_hop[1], 2 * second_hop[0] + second_hop[1], 3 - me

        def copy(k, q, src, dst, to):
            return pltpu.make_async_remote_copy(src_ref=_rows(src, q, n), dst_ref=_rows(dst, q, n),
                                                send_sem=send_sems.at[k * n + q], recv_sem=recv_sems.at[k * n + q],
                                                device_id=to, device_id_type=MESH_ID)

        sent = [copy(k, q, w_ref.at[c], o_ref.at[me, c], (*hop, c))
                for k, hop in ((0, first_hop), (1, second_hop)) for q in range(n)]
        for cp in sent:
            cp.start()
        arrivals = ((0, chip_a, True), (1, chip_b, False), (2, chip_d, False))
        for k, chip, onward in arrivals:
            for q in range(n):
                landed = o_ref.at[chip, c]
                copy(k, q, landed, landed, here).wait_recv()
                if onward:
                    sent.append(copy(2, q, landed, landed, (*second_hop, c)))
                    sent[-1].start()
                sent.append(copy(3 + k, q, landed, landed, sib))
                sent[-1].start()
        for k, chip in ((3, chip_b), (4, chip_a), (5, chip_d)):
            for q in range(n):
                theirs = o_ref.at[chip, 1 - c]
                copy(k, q, theirs, theirs, here).wait_recv()
        for cp in sent:
            cp.wait_send()

    return pl.pallas_call(
        kern, name=name, in_specs=[_ANY], out_specs=_ANY,
        out_shape=jax.ShapeDtypeStruct((N_CHIPS,) + w.shape, w.dtype),
        scratch_shapes=[pltpu.SemaphoreType.DMA((6 * n,)), pltpu.SemaphoreType.DMA((6 * n,))],
        compiler_params=pltpu.CompilerParams(has_side_effects=True),
    )(w)


def _gather_all(name, v):
    def kern(v_ref, o_ref, send_sems, recv_sems, local_sem):
        x, y, c, _ = _place()
        me = 4 * x + 2 * y + c
        mine = pltpu.make_async_copy(v_ref, o_ref.at[me], local_sem)
        mine.start()
        peers = []
        for k in range(1, 8):
            px = 1 - x if k & 4 else x
            py = 1 - y if k & 2 else y
            pc = 1 - c if k & 1 else c
            peers.append((px, py, pc))
        copies = [pltpu.make_async_remote_copy(src_ref=v_ref, dst_ref=o_ref.at[me], send_sem=send_sems.at[k],
                                               recv_sem=recv_sems.at[k], device_id=p, device_id_type=MESH_ID)
                  for k, p in enumerate(peers)]
        for cp in copies:
            cp.start()
        for k, (px, py, pc) in enumerate(peers):
            pltpu.make_async_remote_copy(src_ref=v_ref, dst_ref=o_ref.at[4 * px + 2 * py + pc], send_sem=send_sems.at[k],
                                         recv_sem=recv_sems.at[k], device_id=(x, y, c), device_id_type=MESH_ID).wait_recv()
        for cp in copies:
            cp.wait_send()
        mine.wait()

    return pl.pallas_call(
        kern, name=name, in_specs=[_ANY], out_specs=_ANY, out_shape=jax.ShapeDtypeStruct((8,) + v.shape, v.dtype),
        scratch_shapes=[pltpu.SemaphoreType.DMA((7,)), pltpu.SemaphoreType.DMA((7,)), pltpu.SemaphoreType.DMA(())],
        compiler_params=pltpu.CompilerParams(has_side_effects=True),
    )(v)


def _swap_halves(name, gp):
    n = COMM_CHUNKS
    n_slots = gp.shape[1]

    def kern(g_ref, got_ref, send_sems, recv_sems):
        x, y, c, _ = _place()
        swaps = [pltpu.make_async_remote_copy(src_ref=_rows(g_ref.at[1 - c, j], q, n), dst_ref=_rows(got_ref.at[j], q, n),
                                              send_sem=send_sems.at[j * n + q], recv_sem=recv_sems.at[j * n + q],
                                              device_id=(x, y, 1 - c), device_id_type=MESH_ID)
                 for j in range(n_slots) for q in range(n)]
        for cp in swaps:
            cp.start()
        for cp in swaps:
            cp.wait()

    return pl.pallas_call(
        kern, name=name, in_specs=[_ANY], out_specs=_ANY, out_shape=jax.ShapeDtypeStruct(gp.shape[1:], gp.dtype),
        scratch_shapes=[pltpu.SemaphoreType.DMA((n_slots * n,)), pltpu.SemaphoreType.DMA((n_slots * n,))],
        compiler_params=pltpu.CompilerParams(has_side_effects=True),
    )(gp)


def _scatter_chips(name, s):
    n = COMM_CHUNKS

    def kern(s_ref, o_ref, send_sems, recv_sems):
        x, y, c, chips = _place()
        copies = [pltpu.make_async_remote_copy(src_ref=_rows(s_ref.at[2 * cx + cy], q, n), dst_ref=_rows(o_ref.at[k], q, n),
                                               send_sem=send_sems.at[k * n + q], recv_sem=recv_sems.at[k * n + q],
                                               device_id=(cx, cy, c), device_id_type=MESH_ID)
                  for k, (cx, cy) in enumerate(chips) for q in range(n)]
        for cp in copies:
            cp.start()
        for k in range(3):
            for q in range(n):
                landed = _rows(o_ref.at[k], q, n)
                pltpu.make_async_remote_copy(src_ref=landed, dst_ref=landed, send_sem=send_sems.at[k * n + q],
                                             recv_sem=recv_sems.at[k * n + q], device_id=(x, y, c),
                                             device_id_type=MESH_ID).wait_recv()
        for cp in copies:
            cp.wait_send()

    return pl.pallas_call(
        kern, name=name, in_specs=[_ANY], out_specs=_ANY, out_shape=jax.ShapeDtypeStruct((3,) + s.shape[1:], s.dtype),
        scratch_shapes=[pltpu.SemaphoreType.DMA((3 * n,)), pltpu.SemaphoreType.DMA((3 * n,))],
        compiler_params=pltpu.CompilerParams(has_side_effects=True),
    )(s)


def _join_halves(name, r):
    n = COMM_CHUNKS

    def kern(r_ref, o_ref, send_sems, recv_sems):
        x, y, c, _ = _place()
        swaps = [pltpu.make_async_remote_copy(src_ref=_rows(r_ref, q, n), dst_ref=_rows(o_ref, q, n),
                                              send_sem=send_sems.at[q], recv_sem=recv_sems.at[q],
                                              device_id=(x, y, 1 - c), device_id_type=MESH_ID) for q in range(n)]
        for cp in swaps:
            cp.start()
        for cp in swaps:
            cp.wait()

    return pl.pallas_call(
        kern, name=name, in_specs=[_ANY], out_specs=_ANY, out_shape=jax.ShapeDtypeStruct(r.shape, r.dtype),
        scratch_shapes=[pltpu.SemaphoreType.DMA((n,))] * 2, compiler_params=pltpu.CompilerParams(has_side_effects=True),
    )(r)


PACK_ALIGN = 16


def _exchange_rows(nm, shard_shape):
    if BIG_KIND[nm] == "cols":
        rows = shard_shape[-1]
    else:
        rows = 1
        for s in shard_shape[1:]:
            rows *= s
        rows //= PACK_COLS
    return -(-rows // PACK_ALIGN) * PACK_ALIGN


def _to_exchange(nm, a, transposed=False):
    if BIG_KIND[nm] == "cols":
        if not transposed:
            a = jnp.swapaxes(a, 1, 2)
        rows = -(-a.shape[1] // PACK_ALIGN) * PACK_ALIGN
        a = _pad_to(a, 1, rows)
        return a.reshape(a.shape[0] * rows, PACK_COLS)
    return a.reshape(a.shape[0] * _exchange_rows(nm, a.shape), PACK_COLS)


def _from_exchange(nm, rows2d, shard_shape):
    lead = rows2d.shape[:-2]
    n_layers = shard_shape[0]
    if BIG_KIND[nm] == "cols":
        a = rows2d.reshape(lead + (n_layers, -1, PACK_COLS))
        return lax.slice_in_dim(a, 0, shard_shape[-1], axis=a.ndim - 2)
    return rows2d.reshape(lead + tuple(shard_shape))


def _pack(pieces, dtype):
    slab = jnp.concatenate([p.astype(dtype) for p in pieces])
    half = -(-slab.shape[0] // (2 * PACK_ROW_BLOCK)) * PACK_ROW_BLOCK
    slab = _pad_to(slab, 0, 2 * half)
    return slab.reshape(2, half, PACK_COLS)


def _unpack(packed, row_counts):
    lead = packed.shape[:-3]
    slab = packed.reshape(lead + (-1, PACK_COLS))
    out, off = [], 0
    for rows in row_counts:
        out.append(lax.slice_in_dim(slab, off, off + rows, axis=slab.ndim - 2))
        off += rows
    return out


def _t(a):
    return jnp.swapaxes(a, -1, -2)


def _rows_from_lanes(a):
    return a[:, :FOX_H].T[:, None, :]


def _lanes_from_heads(a):
    return jnp.pad(a[:, 0, :].T, ((0, 0), (0, LANES - FOX_H)))


def _layer_weights(wts, small, l):
    w_in_t = wts["w_in"][l]
    w = {
        "wf_t": jnp.pad(w_in_t[W1_COLS:W1_COLS + FOX_H], ((0, LANES - FOX_H), (0, 0))),
        "w1_t": w_in_t[:W1_COLS], "wgl_t": w_in_t[W1_COLS + FOX_H:],
    }
    for nm, _ in BIG:
        if nm != "w_in":
            w[nm + "_t" if BIG_KIND[nm] == "cols" else nm] = wts[nm][l]
    for nm in ("ffn1_norm", "mix_norm", "sg_norm", "xa_norm", "mem_norm", "ffn2_norm"):
        w[nm] = small[nm][l][None, :]
    w["conv_w"] = jnp.pad(small["conv_w"][l], ((0, 5), (0, 0)))
    w["sg_w"] = small["sg_w"][l]
    w["sg_w_t"] = _t(small["sg_w"][l])
    w["sg_bias"] = jnp.repeat(small["sg_b"][l].T, CHUNK, axis=1)
    w["fox_b"] = jnp.pad(small["fox_b_f"][l][None, :], ((0, 0), (0, LANES - FOX_H)))
    return w


def _layer_fwd(l, x, mem, w):
    s = {"x0": x}
    x1, s["a1"], s["b1"] = _ffn_fwd(f"ffn1_fwd_{l}", x, w["ffn1_norm"], w["ffn1_w_gate_t"], w["ffn1_w_up_t"],
                                    w["ffn1_w_down"])
    s["x1"] = x1
    cin, sgin, qkv, fl, gl = _proj_fwd(f"proj_fwd_{l}", x1, w["mix_norm"], w["w1_t"], w["wf_t"], w["wgl_t"])
    ya = _conv_fwd(f"conv_fwd_{l}", cin, w["conv_w"])
    yb = _sg_fwd(f"sg_fwd_{l}", sgin, w["sg_norm"], w["sg_w"], w["sg_bias"])
    c2 = _forget_cumsum(f"forget_fwd_{l}", fl, w["fox_b"])
    tables = _prune_tables(_flash_bounds(f"flash_bounds_{l}", qkv), c2, min(FLASH_BLOCK, qkv.shape[0]))
    yc, lse2 = _flash_fwd(f"flash_fwd_{l}", qkv, c2, _rows_from_lanes(c2), tables)
    x2 = _merge_fwd(f"merge_fwd_{l}", x1, ya, yb, yc, gl, w["w_branch"], w["w_out"])
    s.update(cin=cin, sgin=sgin, qkv=qkv, fl=fl, gl=gl, ya=ya, yb=yb, yc=yc, c2=c2, lse2=lse2, x2=x2, tables=tables)
    s["mn"], s["kx"], s["vx"] = _mem_fwd(f"mem_fwd_{l}", mem, w["mem_norm"], w["xa_wk"], w["xa_wv"])
    x3 = _xa_fwd(f"xa_fwd_{l}", x2, w["xa_norm"], s["kx"], s["vx"], w["xa_wq"], w["xa_wo"])
    s["x3"] = x3
    x4, s["a2"], s["b2"] = _ffn_fwd(f"ffn2_fwd_{l}", x3, w["ffn2_norm"], w["ffn2_w_gate_t"], w["ffn2_w_up_t"],
                                    w["ffn2_w_down"])
    return x4, s


def _ffn_grads(tag, x, dxo, a, b, w, pre):
    dx, da, db, sv, h, dy, dg = _ffn_bwd(f"{pre}_bwd_{tag}", x, dxo, a, b, w[pre + "_norm"], w[pre + "_w_down"],
                                         w[pre + "_w_gate_t"], w[pre + "_w_up_t"])
    g = {
        pre + "_norm": dg[0],
        pre + "_w_gate": _mm_tn(f"{pre}_dwg_{tag}", da, h, D_FF, D_MODEL, tk=FF_SPLIT, tn=D_MODEL),
        pre + "_w_up": _mm_tn(f"{pre}_dwu_{tag}", db, h, D_FF, D_MODEL, tk=FF_SPLIT, tn=D_MODEL),
        pre + "_w_down": _mm_tn(f"{pre}_dwd_{tag}", sv, dy, D_FF, D_MODEL, tk=FF_SPLIT, tn=D_MODEL),
    }
    return dx, g


def _layer_bwd(l, dx, mem, w, s):
    g = {}
    dx, gf = _ffn_grads(l, s["x3"], dx, s["a2"], s["b2"], w, "ffn2")
    g.update(gf)

    dx, h, o, dq, dy, dkx, dvx, dg = _xa_bwd(f"xa_bwd_{l}", s["x2"], dx, w["xa_norm"], s["kx"], s["vx"], w["xa_wq"],
                                              w["xa_wo"])
    g["xa_norm"] = dg[0]
    g["xa_wq"] = _mm_tn(f"xa_dwq_{l}", h, dq, D_MODEL, D_MODEL, tk=D_MODEL, tn=D_MODEL)
    g["xa_wo"] = _mm_tn(f"xa_dwo_{l}", o, dy, D_MODEL, D_MODEL, tk=D_MODEL, tn=D_MODEL)
    dkb, dvb, dgm = _mem_bwd(f"mem_bwd_{l}", mem, w["mem_norm"], dkx, dvx, w["xa_wk"], w["xa_wv"])
    g["mem_norm"] = dgm[0]
    g["xa_wk"] = _mm_tn(f"xa_dwk_{l}", s["mn"], dkb, D_MODEL, D_MODEL, tk=D_MODEL, tn=D_MODEL)
    g["xa_wv"] = _mm_tn(f"xa_dwv_{l}", s["mn"], dvb, D_MODEL, D_MODEL, tk=D_MODEL, tn=D_MODEL)

    mg, dob, dgl, dbr, dya, dyb, dyc = _merge_bwd(f"merge_bwd_{l}", dx, s["ya"], s["yb"], s["yc"], s["gl"], w["w_branch"],
                                                  w["w_out"])
    g["w_out"] = _mm_tn(f"dwout_{l}", mg, dob, D_MODEL, D_MODEL, tk=D_MODEL, tn=D_MODEL)
    g["w_branch"] = jnp.stack([
        _mm_tn(f"dwbranch{k}_{l}", y, dbr, CONV_W, D_MODEL, tk=CONV_W, tn=D_MODEL, y_off=k * D_MODEL)
        for k, y in enumerate((s["ya"], s["yb"], s["yc"]))])

    dcin, dcw = _conv_bwd(f"conv_bwd_{l}", s["cin"], dya, w["conv_w"])
    g["conv_w"] = dcw[:3]
    dsg, dgn, dsw, dsb = _sg_bwd(f"sg_bwd_{l}", s["sgin"], dyb, w["sg_norm"], w["sg_w"], w["sg_w_t"], w["sg_bias"])
    g["sg_norm"], g["sg_w"], g["sg_b"] = dgn[0], dsw, dsb[:, :, 0]
    d_row = _rows_from_lanes(_flash_delta(f"flash_delta_{l}", dyc, s["yc"]))
    r_row = s["lse2"].reshape(FOX_H, 1, -1) - _rows_from_lanes(s["c2"])
    dq, dk, dv, dc_rows = _flash_bwd(f"flash_bwd_{l}", s["qkv"], dyc, r_row, d_row, s["c2"], s["tables"])
    dc = _lanes_from_heads(dc_rows.reshape(FOX_H, 1, -1))
    dfl, dbf = _forget_bwd(f"forget_bwd_{l}", dc, s["fl"], w["fox_b"])
    g["fox_b_f"] = dbf[0, :FOX_H]

    dx, h, dqkv, dfb, dg = _proj_bwd(f"proj_bwd_{l}", s["x1"], dx, dcin, dsg, dq, dk, dv, dfl, dgl, w["mix_norm"],
                                     w["w1_t"], w["wf_t"], w["wgl_t"])
    g["mix_norm"] = dg[0]
    c0, c1 = 3 * CONV_W, 3 * CONV_W + 2 * SG_W
    g["w_in"] = jnp.concatenate([
        _mm_tn(f"dwin_conv_{l}", dcin, h, c0, D_MODEL, tk=c0, tn=D_MODEL),
        _mm_tn(f"dwin_sg_{l}", dsg, h, c1 - c0, D_MODEL, tk=c1 - c0, tn=D_MODEL),
        _mm_tn(f"dwin_qkv_{l}", dqkv, h, 3 * FOX_W, D_MODEL, tk=3 * FOX_W, tn=D_MODEL),
        _mm_tn(f"dwin_f_{l}", dfb, h, LANES, D_MODEL, tk=LANES, tn=D_MODEL)[:FOX_H],
        _mm_tn(f"dwin_gl_{l}", dgl, h, 3 * D_MODEL, D_MODEL, tk=3 * D_MODEL // 2, tn=D_MODEL),
    ], axis=0)

    dx, gf = _ffn_grads(l, s["x0"], dx, s["a1"], s["b1"], w, "ffn1")
    g.update(gf)
    return dx, g


def _local_step(x, mem, target, wts, small):
    saved, lw = [], []
    for l in range(DEPTH):
        lw.append(_layer_weights(wts, small, l))
        x, s = _layer_fwd(l, x, mem, lw[l])
        saved.append(s)
    fin = small["final_norm"][None, :]
    dx, loss, dgf = _loss_head("loss_head", x, target, fin)
    layer_grads = [None] * DEPTH
    for l in reversed(range(DEPTH)):
        dx, layer_grads[l] = _layer_bwd(l, dx, mem, lw[l], saved[l])
    grads = {nm: jnp.stack([layer_grads[l][nm] for l in range(DEPTH)]) for nm in WEIGHTS if nm != "final_norm"}
    grads["final_norm"] = dgf[0]
    return loss[0, 0], dx, grads


def _small_slab(vals):
    rows = []
    for v in vals:
        flat = v.astype(F32).reshape(-1)
        n = -(-flat.shape[0] // LANES) * LANES
        rows.append(_pad_to(flat, 0, n).reshape(-1, LANES))
    slab = jnp.concatenate(rows)
    return _pad_to(slab, 0, slab.shape[0] + (-slab.shape[0] % 8))


def _pad_to(a, axis, size):
    if a.shape[axis] == size:
        return a
    widths = [(0, 0)] * a.ndim
    widths[axis] = (0, size - a.shape[axis])
    return jnp.pad(a, widths)


def _small_unslab(slab, shapes):
    out, off = [], 0
    for shp in shapes:
        size = 1
        for s in shp:
            size *= s
        n_rows = -(-size // LANES)
        out.append(slab[off:off + n_rows].reshape(-1)[:size].reshape(shp))
        off += n_rows
    return out


def kernel(x, mem, ffn1_norm, ffn1_w_gate, ffn1_w_up, ffn1_w_down, mix_norm, w_in, conv_w, sg_norm, sg_w, sg_b, fox_b_f, w_branch, w_out, xa_norm, mem_norm, xa_wq, xa_wk, xa_wv, xa_wo, ffn2_norm, ffn2_w_gate, ffn2_w_up, ffn2_w_down, final_norm, loss_target, m_ffn1_norm, m_ffn1_w_gate, m_ffn1_w_up, m_ffn1_w_down, m_mix_norm, m_w_in, m_conv_w, m_sg_norm, m_sg_w, m_sg_b, m_fox_b_f, m_w_branch, m_w_out, m_xa_norm, m_mem_norm, m_xa_wq, m_xa_wk, m_xa_wv, m_xa_wo, m_ffn2_norm, m_ffn2_w_gate, m_ffn2_w_up, m_ffn2_w_down, m_final_norm, v_ffn1_norm, v_ffn1_w_gate, v_ffn1_w_up, v_ffn1_w_down, v_mix_norm, v_w_in, v_conv_w, v_sg_norm, v_sg_w, v_sg_b, v_fox_b_f, v_w_branch, v_w_out, v_xa_norm, v_mem_norm, v_xa_wq, v_xa_wk, v_xa_wv, v_xa_wo, v_ffn2_norm, v_ffn2_w_gate, v_ffn2_w_up, v_ffn2_w_down, v_final_norm):
    args = dict(locals())
    wv = {nm: args[nm] for nm in WEIGHTS}
    mv = {nm: args["m_" + nm] for nm in WEIGHTS}
    vv = {nm: args["v_" + nm] for nm in WEIGHTS}
    chip = 2 * lax.axis_index("x") + lax.axis_index("y")
    core = lax.axis_index("c")

    names = [nm for nm, _ in BIG]
    mine = {nm: _to_exchange(nm, wv[nm].astype(BF16)) for nm in names}
    row_counts = [mine[nm].shape[0] for nm in names]
    gathered = _gather_chips("gather_weights", _pack([mine[nm] for nm in names], BF16))
    wts = {}
    for nm, slab in zip(names, _unpack(gathered, row_counts)):
        p = _from_exchange(nm, jnp.where((jnp.arange(N_CHIPS) == chip)[:, None, None], mine[nm][None], slab), wv[nm].shape)
        if BIG_KIND[nm] == "branch":
            wts[nm] = jnp.concatenate([p[j] for j in range(N_CHIPS)], axis=3)
        else:
            wts[nm] = jnp.swapaxes(p, 0, 1).reshape(p.shape[1], N_CHIPS * p.shape[2], PACK_COLS)
    taps = _gather_all("gather_taps", _small_slab([conv_w]))
    taps = [_small_unslab(taps[2 * j], [conv_w.shape])[0] for j in range(N_CHIPS)]
    small = {nm: wv[nm] for nm in SMALL}
    small["conv_w"] = jnp.concatenate(taps, axis=2)

    loss, dx, grads = _local_step(x[0], mem[0], loss_target[0], wts, small)
    loss = lax.psum(loss, ("x", "y", "c"))

    small_shapes = [grads[nm].shape for nm in SMALL]
    slots = _gather_all("gather_small_grads", _small_slab([grads[nm] for nm in SMALL]))
    small_sum = _sum_slots("sum_small_grads", slots, slots.shape[1])
    small_g = dict(zip(SMALL, _small_unslab(small_sum, small_shapes)))
    taps_g = small_g["conv_w"]
    small_g["conv_w"] = lax.dynamic_slice_in_dim(taps_g, chip * conv_w.shape[2], conv_w.shape[2], axis=2)

    def grad_shard(nm, ax, j):
        g = grads[nm]
        ax = 1 if BIG_KIND[nm] == "cols" else ax
        size = g.shape[ax] // N_CHIPS
        return _to_exchange(nm, lax.slice_in_dim(g, j * size, (j + 1) * size, axis=ax), transposed=True)

    gp = jnp.stack([_pack([grad_shard(nm, ax, j) for nm, ax in BIG], BF16) for j in range(N_CHIPS)], axis=1)
    got = _swap_halves("reduce_swap_cores", gp)
    chip_sum = _add_pair("reduce_add_cores", gp, got, core.astype(jnp.int32).reshape(1))
    parts = _scatter_chips("reduce_scatter_chips", chip_sum)
    half = _sum_parts("reduce_add_chips", chip_sum, parts, chip.astype(jnp.int32).reshape(1))
    other = _join_halves("reduce_join_cores", half)
    full = jnp.where(core == 0, jnp.stack([half, other]), jnp.stack([other, half]))
    big_g = {}
    for nm, slab in zip(names, _unpack(full, row_counts)):
        g = _from_exchange(nm, slab, wv[nm].shape)
        big_g[nm] = jnp.swapaxes(g, 1, 2) if BIG_KIND[nm] == "cols" else g

    g_out = {**small_g, **big_g}
    delta, new_m, new_v = {}, {}, {}
    for nm, _ in BIG:
        shp = wv[nm].shape
        two_d = (-1, shp[-1])
        d, m2, v2 = _adamw("adamw_" + nm, wv[nm].reshape(two_d), g_out[nm].reshape(two_d), mv[nm].reshape(two_d),
                           vv[nm].reshape(two_d))
        delta[nm], new_m[nm], new_v[nm] = d.reshape(shp), m2.reshape(shp), v2.reshape(shp)
    slab_shapes = [wv[nm].shape for nm in SMALL]
    d, m2, v2 = _adamw("adamw_small", _small_slab([wv[nm] for nm in SMALL]), _small_slab([g_out[nm] for nm in SMALL]),
                       _small_slab([mv[nm] for nm in SMALL]), _small_slab([vv[nm] for nm in SMALL]))
    for out, slab in ((delta, d), (new_m, m2), (new_v, v2)):
        out.update(zip(SMALL, _small_unslab(slab, slab_shapes)))

    return (loss, dx[None], *[g_out[nm] for nm in WEIGHTS], *[delta[nm] for nm in WEIGHTS],
            *[new_m[nm] for nm in WEIGHTS], *[new_v[nm] for nm in WEIGHTS])
```

```python
import functools

import jax
import jax.numpy as jnp
from jax import lax
from jax.experimental import pallas as pl
from jax.experimental.pallas import tpu as pltpu

F32, BF16 = jnp.float32, jnp.bfloat16
MESH_ID = pl.DeviceIdType.MESH

D_MODEL = 1024
DEPTH = 2
D_FF = 2816
CONV_W = 512
SG_W = 512
SG_G = 4
CHUNK = 128
FOX_H = 8
FOX_D = 64
FOX_W = FOX_H * FOX_D
FOX_SCALE = FOX_D ** -0.5
LOG2E = 1.4426950408889634
XA_H = 4
XA_D = D_MODEL // XA_H
N_CHIPS = 4
RMS_EPS = 1e-6
W1_COLS = 3 * CONV_W + 2 * SG_W + 3 * FOX_W
LANES = 128
HALO = 16

ADAM_LR, ADAM_B1, ADAM_B2, ADAM_EPS, ADAM_WD, ADAM_STEP = 0.001, 0.9, 0.999, 1e-08, 0.01, 10

ROW_BLOCK = 512
FFN_BWD_ROW_BLOCK = 256
MM_TT = 2048
FLASH_BLOCK = 512
FLASH_STRIP = 32
PRUNE_LOG2 = 40.0
FF_SPLIT = 1408
VMEM_LIMIT = 56 * 1024 * 1024
PACK_COLS = 1024
PACK_ROW_BLOCK = 208

BIG = (
    ("ffn1_w_gate", 2), ("ffn1_w_up", 2), ("ffn1_w_down", 1), ("w_in", 2), ("w_branch", 3), ("w_out", 1),
    ("xa_wq", 1), ("xa_wk", 1), ("xa_wv", 1), ("xa_wo", 1), ("ffn2_w_gate", 2), ("ffn2_w_up", 2), ("ffn2_w_down", 1),
)
BIG_KIND = {nm: ("branch" if nm == "w_branch" else "cols" if ax == 2 else "rows") for nm, ax in BIG}
SMALL = ("ffn1_norm", "mix_norm", "conv_w", "sg_norm", "sg_w", "sg_b", "fox_b_f", "xa_norm", "mem_norm", "ffn2_norm",
         "final_norm")
WEIGHTS = ("ffn1_norm", "ffn1_w_gate", "ffn1_w_up", "ffn1_w_down", "mix_norm", "w_in", "conv_w", "sg_norm", "sg_w",
           "sg_b", "fox_b_f", "w_branch", "w_out", "xa_norm", "mem_norm", "xa_wq", "xa_wk", "xa_wv", "xa_wo",
           "ffn2_norm", "ffn2_w_gate", "ffn2_w_up", "ffn2_w_down", "final_norm")


def _dot(a, b):
    return jnp.dot(a, b, preferred_element_type=F32)


def _dot_nt(a, b):
    return lax.dot_general(a, b, (((1,), (1,)), ((), ())), preferred_element_type=F32)


def _dot_tn(a, b):
    return lax.dot_general(a, b, (((0,), (0,)), ((), ())), preferred_element_type=F32)


def _rms_stats(x):
    r = lax.rsqrt(jnp.mean(x * x, axis=-1, keepdims=True) + RMS_EPS)
    return x * r, r


def _rms_bwd(xh, r, g, dy):
    dg = jnp.sum(dy * xh, axis=0, keepdims=True)
    dxh = dy * g
    dx = r * (dxh - xh * jnp.mean(dxh * xh, axis=-1, keepdims=True))
    return dx, dg


def _gelu(x):
    k = 0.7978845608028654
    t = jnp.tanh(k * (x + 0.044715 * x * x * x))
    return 0.5 * x * (1.0 + t), t


def _gelu_grad(x, t):
    k = 0.7978845608028654
    return 0.5 * (1.0 + t) + 0.5 * x * (1.0 - t * t) * k * (1.0 + 3.0 * 0.044715 * x * x)


def _split3_dot(tri, x):
    hi = x.astype(BF16)
    r1 = x - hi.astype(F32)
    mid = r1.astype(BF16)
    lo = (r1 - mid.astype(F32)).astype(BF16)
    return _dot(tri, hi) + _dot(tri, mid) + _dot(tri, lo)


def _params(n_grid):
    return pltpu.CompilerParams(dimension_semantics=("arbitrary",) * n_grid, vmem_limit_bytes=VMEM_LIMIT)


def _full_spec(shape):
    nd = len(shape)
    return pl.BlockSpec(tuple(shape), lambda *_: (0,) * nd)


def _rowcall(name, body, n_tokens, tm, rows, consts, residents, row_outs, acc_outs, scratch=()):
    n = n_tokens // tm
    rows = [r if isinstance(r, tuple) else (r, pl.BlockSpec((tm, r.shape[1]), lambda i: (i, 0))) for r in rows]
    n_in, n_w = len(rows) + len(consts), len(residents)
    n_out = len(row_outs) + len(acc_outs)
    in_specs = ([r[1] for r in rows] + [_full_spec(c.shape) for c in consts]
                + [pl.BlockSpec(memory_space=pl.ANY)] * n_w)
    out_shape = ([jax.ShapeDtypeStruct((n_tokens, c), dt) for c, dt in row_outs]
                 + [jax.ShapeDtypeStruct(s, dt) for s, dt in acc_outs])
    out_specs = ([pl.BlockSpec((tm, c), lambda i: (i, 0)) for c, _ in row_outs]
                 + [_full_spec(s) for s, _ in acc_outs])
    scratch_shapes = [pltpu.VMEM(w.shape, w.dtype) for w in residents]
    if n_w:
        scratch_shapes.append(pltpu.SemaphoreType.DMA((n_w,)))
    scratch_shapes += list(scratch)

    def kern(*refs):
        ins, w_hbm = refs[:n_in], refs[n_in:n_in + n_w]
        outs = refs[n_in + n_w:n_in + n_w + n_out]
        rest = refs[n_in + n_w + n_out:]
        w_vmem = rest[:n_w]
        extra = rest[n_w + 1:] if n_w else rest
        i = pl.program_id(0)

        @pl.when(i == 0)
        def _():
            copies = [pltpu.make_async_copy(w_hbm[k], w_vmem[k], rest[n_w].at[k]) for k in range(n_w)]
            for cp in copies:
                cp.start()
            for cp in copies:
                cp.wait()
            for a in outs[len(row_outs):]:
                a[...] = jnp.zeros(a.shape, a.dtype)

        body(i, n, *ins, *w_vmem, *outs, *extra)

    res = pl.pallas_call(
        kern, name=name, grid=(n,), in_specs=in_specs, out_specs=out_specs, out_shape=out_shape,
        scratch_shapes=scratch_shapes, compiler_params=_params(1),
    )(*[r[0] for r in rows], *consts, *residents)
    return res


def _mm_tn(name, x, y, k_dim, n_dim, *, tk, tn, x_off=0, y_off=0, tt=MM_TT):
    n_tok = x.shape[0]
    tt = min(tt, n_tok)
    n_t = n_tok // tt
    xb, yb = x_off // tk, y_off // tn

    def kern(x_ref, y_ref, o_ref):
        @pl.when(pl.program_id(2) == 0)
        def _():
            o_ref[...] = jnp.zeros(o_ref.shape, F32)

        o_ref[...] += _dot_tn(x_ref[...], y_ref[...])

    return pl.pallas_call(
        kern, name=name, grid=(k_dim // tk, n_dim // tn, n_t),
        in_specs=[pl.BlockSpec((tt, tk), lambda k, n, t: (t, xb + k)),
                  pl.BlockSpec((tt, tn), lambda k, n, t: (t, yb + n))],
        out_specs=pl.BlockSpec((tk, tn), lambda k, n, t: (k, n)),
        out_shape=jax.ShapeDtypeStruct((k_dim, n_dim), F32),
        compiler_params=_params(3),
    )(x, y)


def _ffn_fwd(name, x, gain, wg_t, wu_t, wd):
    n_tok, tm = x.shape[0], min(ROW_BLOCK, x.shape[0])

    def body(i, n, x_ref, g_ref, wg_ref, wu_ref, wd_ref, xo_ref, a_ref, b_ref):
        x_v = x_ref[...]
        xh, _ = _rms_stats(x_v)
        h = (xh * g_ref[...]).astype(BF16)
        y = jnp.zeros((tm, D_MODEL), F32)
        for f0 in range(0, D_FF, FF_SPLIT):
            sl = slice(f0, f0 + FF_SPLIT)
            a = _dot_nt(h, wg_ref[sl, :])
            b = _dot_nt(h, wu_ref[sl, :])
            a_ref[:, sl] = a.astype(BF16)
            b_ref[:, sl] = b.astype(BF16)
            s = (a * jax.nn.sigmoid(a) * b).astype(BF16)
            y = y + _dot(s, wd_ref[sl, :])
        xo_ref[...] = x_v + 0.5 * y

    return _rowcall(name, body, n_tok, tm, [x], [gain], [wg_t, wu_t, wd],
                    [(D_MODEL, F32), (D_FF, BF16), (D_FF, BF16)], [])


def _ffn_bwd(name, x, dxo, a, b, gain, wd, wg_t, wu_t):
    n_tok, tm = x.shape[0], min(FFN_BWD_ROW_BLOCK, x.shape[0])

    def body(i, n, x_ref, dxo_ref, a_ref, b_ref, g_ref, wdt_ref, wgt_ref, wut_ref,
             dx_ref, da_ref, db_ref, s_ref, h_ref, dy_ref, dg_ref):
        g = g_ref[...]
        xh, r = _rms_stats(x_ref[...])
        h_ref[...] = (xh * g).astype(BF16)
        dxo_v = dxo_ref[...]
        dy = (0.5 * dxo_v).astype(BF16)
        dy_ref[...] = dy
        dh = jnp.zeros((tm, D_MODEL), F32)
        for f0 in range(0, D_FF, FF_SPLIT):
            sl = slice(f0, f0 + FF_SPLIT)
            a_v = a_ref[:, sl].astype(F32)
            b_v = b_ref[:, sl].astype(F32)
            ds = _dot_nt(dy, wdt_ref[sl, :])
            sig = jax.nn.sigmoid(a_v)
            sa = a_v * sig
            s_ref[:, sl] = (sa * b_v).astype(BF16)
            da = (ds * b_v * (sig * (1.0 + a_v * (1.0 - sig)))).astype(BF16)
            db = (ds * sa).astype(BF16)
            da_ref[:, sl] = da
            db_ref[:, sl] = db
            dh = dh + _dot(da, wgt_ref[sl, :]) + _dot(db, wut_ref[sl, :])
        dx, dg = _rms_bwd(xh, r, g, dh)
        dx_ref[...] = dxo_v + dx
        dg_ref[...] += dg

    return _rowcall(name, body, n_tok, tm, [x, dxo, a, b], [gain], [wd, wg_t, wu_t],
                    [(D_MODEL, F32), (D_FF, BF16), (D_FF, BF16), (D_FF, BF16), (D_MODEL, BF16), (D_MODEL, BF16)],
                    [((1, D_MODEL), F32)])


def _proj_fwd(name, x, gain, w1_t, wf_t, wgl_t):
    n_tok, tm = x.shape[0], min(ROW_BLOCK, x.shape[0])
    c0, c1 = 3 * CONV_W, 3 * CONV_W + 2 * SG_W

    def body(i, n, x_ref, g_ref, w1_ref, wf_ref, wgl_ref, cin_ref, sg_ref, qkv_ref, fl_ref, gl_ref):
        xh, _ = _rms_stats(x_ref[...])
        h = (xh * g_ref[...]).astype(BF16)
        cin_ref[...] = _dot_nt(h, w1_ref[0:c0, :]).astype(BF16)
        sg_ref[...] = _dot_nt(h, w1_ref[c0:c1, :]).astype(BF16)
        qkv_ref[...] = _dot_nt(h, w1_ref[c1:W1_COLS, :]).astype(BF16)
        fl_ref[...] = _dot_nt(h, wf_ref[...])
        gl_ref[...] = _dot_nt(h, wgl_ref[...]).astype(BF16)

    return _rowcall(name, body, n_tok, tm, [x], [gain], [w1_t, wf_t, wgl_t],
                    [(3 * CONV_W, BF16), (2 * SG_W, BF16), (3 * FOX_W, BF16), (LANES, F32), (3 * D_MODEL, BF16)], [])


def _proj_bwd(name, x, dxin, dcin, dsg, dq, dk, dv, dfl, dgl, gain, w1_t, wf_t, wgl_t):
    n_tok, tm = x.shape[0], min(ROW_BLOCK, x.shape[0])
    c0, c1 = 3 * CONV_W, 3 * CONV_W + 2 * SG_W

    def body(i, n, x_ref, dxin_ref, dcin_ref, dsg_ref, dq_ref, dk_ref, dv_ref, dfl_ref, dgl_ref, g_ref,
             w1t_ref, wft_ref, wglt_ref, dx_ref, h_ref, dqkv_ref, dfb_ref, dg_ref):
        g = g_ref[...]
        xh, r = _rms_stats(x_ref[...])
        h_ref[...] = (xh * g).astype(BF16)
        dfb = dfl_ref[...].astype(BF16)
        dfb_ref[...] = dfb
        dh = _dot(dcin_ref[...], w1t_ref[0:c0, :])
        dh = dh + _dot(dsg_ref[...], w1t_ref[c0:c1, :])
        for k, d_ref in enumerate((dq_ref, dk_ref, dv_ref)):
            d_b = d_ref[...].astype(BF16)
            dqkv_ref[:, k * FOX_W:(k + 1) * FOX_W] = d_b
            dh = dh + _dot(d_b, w1t_ref[c1 + k * FOX_W:c1 + (k + 1) * FOX_W, :])
        dh = dh + _dot(dfb, wft_ref[...])
        dh = dh + _dot(dgl_ref[...], wglt_ref[...])
        dx, dg = _rms_bwd(xh, r, g, dh)
        dx_ref[...] = dxin_ref[...] + dx
        dg_ref[...] += dg

    return _rowcall(name, body, n_tok, tm, [x, dxin, dcin, dsg, dq, dk, dv, dfl, dgl], [gain], [w1_t, wf_t, wgl_t],
                    [(D_MODEL, F32), (D_MODEL, BF16), (3 * FOX_W, BF16), (LANES, BF16)], [((1, D_MODEL), F32)])


def _conv_taps(z, prev_z, i):
    tm = z.shape[0]
    row = lax.broadcasted_iota(jnp.int32, (tm, 1), 0)
    live = (i > 0).astype(F32)
    p1, p2 = prev_z[HALO - 1:HALO, :] * live, prev_z[HALO - 2:HALO - 1, :] * live
    z1 = jnp.where(row == 0, p1, pltpu.roll(z, 1, 0))
    z2 = jnp.where(row == 0, p2, jnp.where(row == 1, p1, pltpu.roll(z, 2, 0)))
    return z1, z2


def _prev_spec(tm, cols):
    return pl.BlockSpec((HALO, cols), lambda i: (jnp.maximum(i * (tm // HALO) - 1, 0), 0))


def _next_spec(tm, cols, n):
    last = n * (tm // HALO) - 1
    return pl.BlockSpec((HALO, cols), lambda i: (jnp.minimum((i + 1) * (tm // HALO), last), 0))


def _conv_fwd(name, cin, cw):
    n_tok, tm = cin.shape[0], min(ROW_BLOCK, cin.shape[0])
    w = CONV_W

    def body(i, n, c_ref, p_ref, cw_ref, ya_ref):
        c_v, p_v = c_ref[...].astype(F32), p_ref[...].astype(F32)
        z = c_v[:, w:2 * w] * c_v[:, 2 * w:]
        z1, z2 = _conv_taps(z, p_v[:, w:2 * w] * p_v[:, 2 * w:], i)
        y = cw_ref[0:1, :] * z2 + cw_ref[1:2, :] * z1 + cw_ref[2:3, :] * z
        ya_ref[...] = (c_v[:, 0:w] * y).astype(BF16)

    return _rowcall(name, body, n_tok, tm, [cin, (cin, _prev_spec(tm, 3 * w))], [cw], [], [(w, BF16)], [])[0]


def _conv_bwd(name, cin, dya, cw):
    n_tok, tm = cin.shape[0], min(ROW_BLOCK, cin.shape[0])
    w = CONV_W
    n_blocks = n_tok // tm

    def body(i, n, c_ref, p_ref, nx_ref, dya_ref, ndya_ref, cw_ref, dc_ref, dcw_ref):
        c_v, p_v = c_ref[...].astype(F32), p_ref[...].astype(F32)
        ab, ac, ah = c_v[:, 0:w], c_v[:, w:2 * w], c_v[:, 2 * w:]
        z = ac * ah
        z1, z2 = _conv_taps(z, p_v[:, w:2 * w] * p_v[:, 2 * w:], i)
        w0, w1, w2 = cw_ref[0:1, :], cw_ref[1:2, :], cw_ref[2:3, :]
        y = w0 * z2 + w1 * z1 + w2 * z
        dya_v = dya_ref[...].astype(F32)
        dy = dya_v * ab
        live = (i < n - 1).astype(F32)
        ndy = ndya_ref[...].astype(F32) * nx_ref[:, 0:w].astype(F32) * live
        row = lax.broadcasted_iota(jnp.int32, (tm, 1), 0)
        dy1 = jnp.where(row == tm - 1, ndy[0:1, :], pltpu.roll(dy, tm - 1, 0))
        dy2 = jnp.where(row == tm - 1, ndy[1:2, :], jnp.where(row == tm - 2, ndy[0:1, :], pltpu.roll(dy, tm - 2, 0)))
        dz = w2 * dy + w1 * dy1 + w0 * dy2
        dc_ref[:, 0:w] = (dya_v * y).astype(BF16)
        dc_ref[:, w:2 * w] = (dz * ah).astype(BF16)
        dc_ref[:, 2 * w:] = (dz * ac).astype(BF16)
        dcw_ref[0:1, :] += jnp.sum(dy * z2, axis=0, keepdims=True)
        dcw_ref[1:2, :] += jnp.sum(dy * z1, axis=0, keepdims=True)
        dcw_ref[2:3, :] += jnp.sum(dy * z, axis=0, keepdims=True)

    return _rowcall(name, body, n_tok, tm,
                    [cin, (cin, _prev_spec(tm, 3 * w)), (cin, _next_spec(tm, 3 * w, n_blocks)),
                     dya, (dya, _next_spec(tm, w, n_blocks))],
                    [cw], [], [(3 * w, BF16)], [((8, w), F32)])


def _sg_common(sg_ref, gn_ref):
    s_v = sg_ref[...].astype(F32)
    u, v = s_v[:, 0:SG_W], s_v[:, SG_W:]
    ug, tu = _gelu(u)
    vg, tv = _gelu(v)
    vh, r = _rms_stats(vg)
    vn = (vh * gn_ref[...]).astype(BF16)
    return u, v, ug, tu, tv, vh, r, vn


def _sg_fwd(name, sgin, gn, sgw, bias_full):
    n_tok, tm = sgin.shape[0], min(ROW_BLOCK, sgin.shape[0])

    def body(i, n, sg_ref, gn_ref, w_ref, bias_ref, yb_ref):
        _, _, ug, _, _, _, _, vn = _sg_common(sg_ref, gn_ref)
        tril = lax.broadcasted_iota(jnp.int32, (CHUNK, CHUNK), 0) >= lax.broadcasted_iota(jnp.int32, (CHUNK, CHUNK), 1)
        wt = [jnp.where(tril, w_ref[g], 0.0).astype(BF16) for g in range(SG_G)]
        for c0 in range(0, tm, CHUNK):
            sv = jnp.concatenate(
                [_dot(wt[g], vn[c0:c0 + CHUNK, g * CHUNK:(g + 1) * CHUNK]) for g in range(SG_G)], axis=1)
            sv = sv + bias_ref[...]
            yb_ref[c0:c0 + CHUNK, :] = (ug[c0:c0 + CHUNK, :] * sv).astype(BF16)

    return _rowcall(name, body, n_tok, tm, [sgin], [gn, sgw, bias_full], [], [(SG_W, BF16)], [])[0]


def _sg_bwd(name, sgin, dyb, gn, sgw, sgw_t, bias_full):
    n_tok, tm = sgin.shape[0], min(ROW_BLOCK, sgin.shape[0])

    def body(i, n, sg_ref, dyb_ref, gn_ref, w_ref, wt_ref, bias_ref, dsg_ref, dgn_ref, dw_ref, db_ref):
        u, v, ug, tu, tv, vh, r, vn = _sg_common(sg_ref, gn_ref)
        r0 = lax.broadcasted_iota(jnp.int32, (CHUNK, CHUNK), 0)
        r1 = lax.broadcasted_iota(jnp.int32, (CHUNK, CHUNK), 1)
        wt = [jnp.where(r0 >= r1, w_ref[g], 0.0).astype(BF16) for g in range(SG_G)]
        wtt = [jnp.where(r0 <= r1, wt_ref[g], 0.0).astype(BF16) for g in range(SG_G)]
        dyb_v = dyb_ref[...].astype(F32)
        dvn_rows = []
        for c0 in range(0, tm, CHUNK):
            rows = slice(c0, c0 + CHUNK)
            svs, dvns = [], []
            dsv = dyb_v[rows, :] * ug[rows, :]
            for g in range(SG_G):
                cols = slice(g * CHUNK, (g + 1) * CHUNK)
                svs.append(_dot(wt[g], vn[rows, cols]))
                dsv_g = dsv[:, cols]
                dsv_b = dsv_g.astype(BF16)
                dvns.append(_dot(wtt[g], dsv_b))
                dw_ref[g] += jnp.where(r0 >= r1, _dot_nt(dsv_b, vn[rows, cols]), 0.0)
                db_ref[g] += jnp.broadcast_to(jnp.sum(dsv_g, axis=1, keepdims=True), (CHUNK, CHUNK))
            sv = jnp.concatenate(svs, axis=1) + bias_ref[...]
            dug = dyb_v[rows, :] * sv
            dsg_ref[rows, 0:SG_W] = (dug * _gelu_grad(u[rows, :], tu[rows, :])).astype(BF16)
            dvn_rows.append(jnp.concatenate(dvns, axis=1))
        dvn = jnp.concatenate(dvn_rows, axis=0)
        dvg, dgn = _rms_bwd(vh, r, gn_ref[...], dvn)
        dsg_ref[:, SG_W:] = (dvg * _gelu_grad(v, tv)).astype(BF16)
        dgn_ref[...] += dgn

    return _rowcall(name, body, n_tok, tm, [sgin, dyb], [gn, sgw, sgw_t, bias_full], [], [(2 * SG_W, BF16)],
                    [((1, SG_W), F32), ((SG_G, CHUNK, CHUNK), F32), ((SG_G, CHUNK, CHUNK), F32)])


def _forget_cumsum(name, fl, bf):
    n_tok, tm = fl.shape[0], min(ROW_BLOCK, fl.shape[0])

    def body(i, n, fl_ref, b_ref, c_ref, carry):
        @pl.when(i == 0)
        def _():
            carry[...] = jnp.zeros(carry.shape, F32)

        z = fl_ref[...] + b_ref[...]
        lf = jnp.minimum(z, 0.0) - jnp.log1p(jnp.exp(-jnp.abs(z)))
        tri = (lax.broadcasted_iota(jnp.int32, (tm, tm), 0) >= lax.broadcasted_iota(jnp.int32, (tm, tm), 1)).astype(BF16)
        c = _split3_dot(tri, lf) + carry[...]
        c_ref[...] = c * LOG2E
        carry[...] = c[tm - 1:tm, :]

    return _rowcall(name, body, n_tok, tm, [fl], [bf], [], [(LANES, F32)], [], scratch=[pltpu.VMEM((1, LANES), F32)])[0]


def _forget_bwd(name, dc, fl, bf):
    n_tok, tm = fl.shape[0], min(ROW_BLOCK, fl.shape[0])
    n = n_tok // tm
    rev = pl.BlockSpec((tm, LANES), lambda i: (n - 1 - i, 0))

    def kern(dc_ref, fl_ref, b_ref, dfl_ref, db_ref, carry):
        @pl.when(pl.program_id(0) == 0)
        def _():
            carry[...] = jnp.zeros(carry.shape, F32)
            db_ref[...] = jnp.zeros(db_ref.shape, F32)

        triu = (lax.broadcasted_iota(jnp.int32, (tm, tm), 0) <= lax.broadcasted_iota(jnp.int32, (tm, tm), 1)).astype(BF16)
        dlf = _split3_dot(triu, dc_ref[...]) + carry[...]
        carry[...] = dlf[0:1, :]
        z = fl_ref[...] + b_ref[...]
        dfl = dlf * jax.nn.sigmoid(-z)
        dfl_ref[...] = dfl
        db_ref[...] += jnp.sum(dfl, axis=0, keepdims=True)

    return pl.pallas_call(
        kern, name=name, grid=(n,), in_specs=[rev, rev, _full_spec((1, LANES))],
        out_specs=[rev, _full_spec((1, LANES))],
        out_shape=[jax.ShapeDtypeStruct((n_tok, LANES), F32), jax.ShapeDtypeStruct((1, LANES), F32)],
        scratch_shapes=[pltpu.VMEM((1, LANES), F32)], compiler_params=_params(1),
    )(dc, fl, bf)


def _lane_pick(c_blk, h):
    lane = lax.broadcasted_iota(jnp.int32, c_blk.shape, 1)
    return jnp.broadcast_to(jnp.sum(jnp.where(lane == h, c_blk, 0.0), axis=1, keepdims=True), c_blk.shape)


def _wide(stat, width):
    return jnp.tile(stat, (1, width // LANES))


def _pair_half(rows):
    return lax.broadcasted_iota(jnp.int32, (rows, LANES), 1) // FOX_D


def _col_to_row(col):
    pick = (lax.broadcasted_iota(jnp.int32, (8, LANES), 1) == 0).astype(BF16)
    hi = col.astype(BF16)
    r1 = col - hi.astype(F32)
    mid = r1.astype(BF16)
    lo = (r1 - mid.astype(F32)).astype(BF16)
    return (_dot_nt(pick, hi) + _dot_nt(pick, mid) + _dot_nt(pick, lo))[0:1, :]


def _tri_table(nb, lower):
    rows = [(a, b) for a in range(nb) for b in (range(a + 1) if lower else range(a, nb))]
    return tuple(jnp.asarray([r[k] for r in rows], jnp.int32) for k in range(2))


def _flash_bounds(name, qkv):
    n_tok = qkv.shape[0]
    blk = min(FLASH_BLOCK, n_tok)

    def kern(x_ref, o_ref):
        head = (lax.broadcasted_iota(jnp.int32, (FOX_W, LANES), 0) // FOX_D
                == lax.broadcasted_iota(jnp.int32, (FOX_W, LANES), 1)).astype(BF16)
        q, k = x_ref[:, 0:FOX_W].astype(F32), x_ref[:, FOX_W:2 * FOX_W].astype(F32)
        qn = jnp.sqrt(jnp.max(_dot((q * q).astype(BF16), head) * 1.01, axis=0, keepdims=True))
        kn = jnp.sqrt(jnp.max(_dot((k * k).astype(BF16), head) * 1.01, axis=0, keepdims=True))
        diag = jnp.min(_dot((q * k).astype(BF16), head), axis=0, keepdims=True) - qn * kn * (2.0 ** -8)
        o_ref[...] = jnp.concatenate([qn, kn, diag, jnp.zeros((5, LANES), F32)], axis=0)

    return pl.pallas_call(
        kern, name=name, grid=(n_tok // blk,),
        in_specs=[pl.BlockSpec((blk, 2 * FOX_W), lambda b: (b, 0))], out_specs=pl.BlockSpec((8, LANES), lambda b: (b, 0)),
        out_shape=jax.ShapeDtypeStruct((n_tok // blk * 8, LANES), F32), compiler_params=_params(1),
    )(qkv)


def _prune_tables(bounds, c2, blk):
    nb = c2.shape[0] // blk
    tab = bounds.reshape(nb, 8, LANES)
    c_blocks = c2.reshape(nb, blk, LANES)
    return tuple(t[:, :FOX_H].reshape(-1)
                 for t in (tab[:, 0], tab[:, 1], tab[:, 2], c_blocks[:, 0], c_blocks[:, blk - 1]))


def _block_is_live(tables, p, i, j):
    qn, kn, diag, c_first, c_last = tables
    a = FOX_SCALE * LOG2E
    live = []
    for e in range(2):
        h = 2 * p + e
        qi, kj = FOX_H * i + h, FOX_H * j + h
        u = a * qn[qi] * kn[kj] + c_first[qi] - c_last[kj]
        lo = a * diag[qi]
        live.append((u - lo) > -PRUNE_LOG2)
    return live


def _flash_fwd(name, qkv, c2, ct2, tables):
    n_tok = qkv.shape[0]
    blk = min(FLASH_BLOCK, n_tok)
    nb = n_tok // blk
    n_pair = FOX_H // 2
    it, jt = _tri_table(nb, True)

    strip = min(FLASH_STRIP, blk)

    def kern(it_ref, jt_ref, qn_ref, kn_ref, dg_ref, cf_ref, cl_ref, q_ref, k_ref, v_ref, c_ref, ct_ref, o_ref, lse_ref,
             m_s, l_s, acc_s, cq_s, qm_s, al_s, s_s, pb_s):
        n = pl.program_id(1)
        i, j = it_ref[n], jt_ref[n]
        half = _pair_half(blk)
        live = _block_is_live((qn_ref, kn_ref, dg_ref, cf_ref, cl_ref), pl.program_id(0), i, j)

        @pl.when(j == 0)
        def _():
            m_s[...] = jnp.full(m_s.shape, -jnp.inf, F32)
            l_s[...] = jnp.zeros(l_s.shape, F32)
            acc_s[...] = jnp.zeros(acc_s.shape, F32)
            for e in range(2):
                cq_s[e] = _lane_pick(c_ref[...], 2 * pl.program_id(0) + e)
                qm_s[e] = jnp.where(half == e, q_ref[...], jnp.zeros_like(q_ref[...]))

        def step(heads, on_diagonal):
            k_v, v_v = k_ref[...], v_ref[...]
            for e in heads:
                s_s[e] = _dot_nt(qm_s[e], k_v)
            for e in heads:
                ck = ct_ref[e]
                for r0 in range(0, blk, strip):
                    rs = slice(r0, r0 + strip)
                    t = s_s[e, rs, :] * (FOX_SCALE * LOG2E) - ck
                    if on_diagonal:
                        keep = (lax.broadcasted_iota(jnp.int32, (strip, blk), 1)
                                <= lax.broadcasted_iota(jnp.int32, (strip, blk), 0) + r0)
                        t = jnp.where(keep, t, -jnp.inf)
                    m_old, cq = m_s[e, rs, :], cq_s[e, rs, :]
                    m_new = jnp.maximum(m_old, jnp.max(t, axis=1, keepdims=True) + cq)
                    p = jnp.exp2(t - _wide(m_new - cq, blk))
                    alpha = jnp.exp2(m_old - m_new)
                    l_s[e, rs, :] = alpha * l_s[e, rs, :] + jnp.sum(p, axis=1, keepdims=True)
                    m_s[e, rs, :] = m_new
                    al_s[e, rs, :] = alpha
                    pb_s[e, rs, :] = p.astype(BF16)
            if len(heads) == 2:
                pv = jnp.where(half == 0, _dot(pb_s[0], v_v), _dot(pb_s[1], v_v))
                acc_s[...] = jnp.where(half == 0, al_s[0], al_s[1]) * acc_s[...] + pv
            else:
                e = heads[0]
                acc_s[...] = jnp.where(half == e, al_s[e] * acc_s[...] + _dot(pb_s[e], v_v), acc_s[...])

        below = j < i

        @pl.when(below & live[0] & live[1])
        def _():
            step((0, 1), False)

        @pl.when(below & live[0] & jnp.logical_not(live[1]))
        def _():
            step((0,), False)

        @pl.when(below & jnp.logical_not(live[0]) & live[1])
        def _():
            step((1,), False)

        @pl.when(j == i)
        def _():
            step((0, 1), True)
            o_ref[...] = (acc_s[...] / jnp.where(half == 0, l_s[0], l_s[1])).astype(BF16)
            rows = pl.ds(pl.multiple_of(i * blk, blk), blk)
            lse_ref[0, :, rows] = jnp.concatenate(
                [_col_to_row(m_s[e] + jnp.log(l_s[e]) * LOG2E) for e in range(2)], axis=0)

    def key_block(p, n, it_r, jt_r, tabs):
        i, j = it_r[n], jt_r[n]
        live = _block_is_live(tabs, p, i, j)
        return jnp.where(live[0] | live[1], j, i)

    def col_block(first, inner):
        def index(p, n, it_r, jt_r, *tabs):
            return ((key_block(p, n, it_r, jt_r, tabs) if inner else it_r[n]), first + p)
        return pl.BlockSpec((blk, LANES), index)

    pair_stat = (2, blk, LANES)
    grid_spec = pltpu.PrefetchScalarGridSpec(
        num_scalar_prefetch=2 + len(tables), grid=(n_pair, int(it.shape[0])),
        in_specs=[col_block(0, False), col_block(n_pair, True), col_block(2 * n_pair, True),
                  pl.BlockSpec((blk, LANES), lambda p, n, it_r, jt_r, *_: (it_r[n], 0)),
                  pl.BlockSpec((2, 1, blk), lambda p, n, it_r, jt_r, *tabs: (p, 0, key_block(p, n, it_r, jt_r, tabs)))],
        out_specs=[col_block(0, False), pl.BlockSpec((1, 2, n_tok), lambda p, n, it_r, jt_r, *_: (p, 0, 0))],
        scratch_shapes=[pltpu.VMEM(pair_stat, F32), pltpu.VMEM(pair_stat, F32), pltpu.VMEM((blk, LANES), F32),
                        pltpu.VMEM(pair_stat, F32), pltpu.VMEM(pair_stat, BF16), pltpu.VMEM(pair_stat, F32),
                        pltpu.VMEM((2, blk, blk), F32), pltpu.VMEM((2, blk, blk), BF16)],
    )
    return pl.pallas_call(
        kern, name=name, grid_spec=grid_spec,
        out_shape=[jax.ShapeDtypeStruct((n_tok, FOX_W), BF16), jax.ShapeDtypeStruct((n_pair, 2, n_tok), F32)],
        compiler_params=_params(2),
    )(it, jt, *tables, qkv, qkv, qkv, c2, ct2)


def _head_sums(do, o):
    prod = do.astype(F32) * o.astype(F32)
    head = (lax.broadcasted_iota(jnp.int32, (FOX_W, LANES), 0) // FOX_D
            == lax.broadcasted_iota(jnp.int32, (FOX_W, LANES), 1)).astype(BF16)
    hi = prod.astype(BF16)
    r1 = prod - hi.astype(F32)
    mid = r1.astype(BF16)
    lo = (r1 - mid.astype(F32)).astype(BF16)
    return _dot(hi, head) + _dot(mid, head) + _dot(lo, head)


def _flash_bwd(name, qkv, do, r_row, d_row, c2, tables):
    n_tok = qkv.shape[0]
    blk = min(FLASH_BLOCK, n_tok)
    nb = n_tok // blk
    n_pair = FOX_H // 2
    jt, it = _tri_table(nb, False)

    def kern(jt_ref, it_ref, qn_ref, kn_ref, dg_ref, cf_ref, cl_ref, k_ref, v_ref, q_ref, do_ref, rr_ref, dr_ref, c_ref,
             dq_ref, dk_ref, dv_ref, dc_ref, km_s, vm_s, ck_s, dck_s):
        n = pl.program_id(1)
        j, i = jt_ref[n], it_ref[n]
        first = _pair_half(blk) == 0
        live = _block_is_live((qn_ref, kn_ref, dg_ref, cf_ref, cl_ref), pl.program_id(0), i, j)

        @pl.when(n == 0)
        def _():
            dq_ref[...] = jnp.zeros(dq_ref.shape, F32)
            dc_ref[...] = jnp.zeros(dc_ref.shape, F32)

        @pl.when(i == j)
        def _():
            dk_ref[...] = jnp.zeros(dk_ref.shape, F32)
            dv_ref[...] = jnp.zeros(dv_ref.shape, F32)
            dck_s[...] = jnp.zeros(dck_s.shape, F32)
            for e in range(2):
                mine = first if e == 0 else jnp.logical_not(first)
                km_s[e] = jnp.where(mine, k_ref[...], jnp.zeros_like(k_ref[...]))
                vm_s[e] = jnp.where(mine, v_ref[...], jnp.zeros_like(v_ref[...]))
                ck_s[e] = _lane_pick(c_ref[...], 2 * pl.program_id(0) + e)

        def step(heads, on_diagonal):
            q_v, do_v, k_v = q_ref[...], do_ref[...], k_ref[...]
            dvs, dks, dqs, sums = {}, {}, {}, {}
            for e in heads:
                t = _dot_nt(km_s[e], q_v) * (FOX_SCALE * LOG2E) - _wide(ck_s[e], blk) - rr_ref[e]
                if on_diagonal:
                    keep = (lax.broadcasted_iota(jnp.int32, (blk, blk), 1)
                            >= lax.broadcasted_iota(jnp.int32, (blk, blk), 0))
                    t = jnp.where(keep, t, -jnp.inf)
                p = jnp.exp2(t)
                ds = p * (_dot_nt(vm_s[e], do_v) - dr_ref[e])
                p_b, ds_b = p.astype(BF16), ds.astype(BF16)
                dvs[e] = _dot(p_b, do_v)
                dks[e] = _dot(ds_b, q_v)
                dqs[e] = _dot_tn(ds_b, k_v)
                dck_s[e] += jnp.broadcast_to(jnp.sum(ds, axis=1, keepdims=True), (blk, LANES))
                sums[e] = jnp.sum(ds, axis=0, keepdims=True)
            rows = pl.ds(pl.multiple_of(i * blk, blk), blk)

            def merged(parts):
                zero = jnp.zeros((blk, LANES), F32)
                return jnp.where(first, parts.get(0, zero), parts.get(1, zero))

            dv_ref[...] += merged(dvs)
            dk_ref[...] += merged(dks) * FOX_SCALE
            dq_ref[rows, :] += merged(dqs) * FOX_SCALE
            none = jnp.zeros((1, blk), F32)
            dc_ref[0, :, rows] += jnp.concatenate([sums.get(0, none), sums.get(1, none)], axis=0)

        above = i > j

        @pl.when(above & live[0] & live[1])
        def _():
            step((0, 1), False)

        @pl.when(above & live[0] & jnp.logical_not(live[1]))
        def _():
            step((0,), False)

        @pl.when(above & jnp.logical_not(live[0]) & live[1])
        def _():
            step((1,), False)

        @pl.when(i == j)
        def _():
            step((0, 1), True)

        @pl.when(i == nb - 1)
        def _():
            keys = pl.ds(pl.multiple_of(j * blk, blk), blk)
            dc_ref[0, :, keys] -= jnp.concatenate([_col_to_row(dck_s[e]) for e in range(2)], axis=0)

    def query_block(p, n, jt_r, it_r, tabs):
        j, i = jt_r[n], it_r[n]
        live = _block_is_live(tabs, p, i, j)
        return jnp.where(live[0] | live[1] | (i == j), i, nb - 1)

    def col_block(first_col, inner):
        def index(p, n, jt_r, it_r, *tabs):
            return ((query_block(p, n, jt_r, it_r, tabs) if inner else jt_r[n]), first_col + p)
        return pl.BlockSpec((blk, LANES), index)

    def row_stat():
        return pl.BlockSpec((2, 1, blk), lambda p, n, jt_r, it_r, *tabs: (p, 0, query_block(p, n, jt_r, it_r, tabs)))

    pair_stat = (2, blk, LANES)
    grid_spec = pltpu.PrefetchScalarGridSpec(
        num_scalar_prefetch=2 + len(tables), grid=(n_pair, int(jt.shape[0])),
        in_specs=[col_block(n_pair, False), col_block(2 * n_pair, False), col_block(0, True), col_block(0, True),
                  row_stat(), row_stat(), pl.BlockSpec((blk, LANES), lambda p, n, jt_r, it_r, *_: (jt_r[n], 0))],
        out_specs=[pl.BlockSpec((n_tok, LANES), lambda p, n, jt_r, it_r, *_: (0, p)),
                   col_block(0, False), col_block(0, False),
                   pl.BlockSpec((1, 2, n_tok), lambda p, n, jt_r, it_r, *_: (p, 0, 0))],
        scratch_shapes=[pltpu.VMEM(pair_stat, BF16), pltpu.VMEM(pair_stat, BF16), pltpu.VMEM(pair_stat, F32),
                        pltpu.VMEM(pair_stat, F32)],
    )
    wide = jax.ShapeDtypeStruct((n_tok, FOX_W), F32)
    return pl.pallas_call(
        kern, name=name, grid_spec=grid_spec,
        out_shape=[wide, wide, wide, jax.ShapeDtypeStruct((n_pair, 2, n_tok), F32)],
        compiler_params=_params(2),
    )(jt, it, *tables, qkv, qkv, qkv, do, r_row, d_row, c2)


def _merge_fwd(name, x, ya, yb, yc, gl, wb, wo):
    n_tok, tm = x.shape[0], min(ROW_BLOCK, x.shape[0])

    def body(i, n, x_ref, ya_ref, yb_ref, yc_ref, gl_ref, wb_ref, wo_ref, xo_ref):
        merged = jnp.zeros((tm, D_MODEL), F32)
        for k, y_ref in enumerate((ya_ref, yb_ref, yc_ref)):
            gate = jax.nn.sigmoid(gl_ref[:, k * D_MODEL:(k + 1) * D_MODEL].astype(F32))
            merged = merged + gate * _dot(y_ref[...], wb_ref[k])
        xo_ref[...] = x_ref[...] + _dot(merged.astype(BF16), wo_ref[...])

    return _rowcall(name, body, n_tok, tm, [x, ya, yb, yc, gl], [], [wb, wo], [(D_MODEL, F32)], [])[0]


def _merge_bwd(name, dxo, ya, yb, yc, gl, wb, wo):
    n_tok, tm = dxo.shape[0], min(ROW_BLOCK, dxo.shape[0])

    def body(i, n, dxo_ref, ya_ref, yb_ref, yc_ref, gl_ref, wb_ref, wo_ref,
             mg_ref, dob_ref, dgl_ref, dbr_ref, dya_ref, dyb_ref, dyc_ref, delta_ref):
        dob = dxo_ref[...].astype(BF16)
        dob_ref[...] = dob
        dm = _dot_nt(dob, wo_ref[...])
        merged = jnp.zeros((tm, D_MODEL), F32)
        for k, (y_ref, dy_ref) in enumerate(((ya_ref, dya_ref), (yb_ref, dyb_ref), (yc_ref, dyc_ref))):
            cols = slice(k * D_MODEL, (k + 1) * D_MODEL)
            gate = jax.nn.sigmoid(gl_ref[:, cols].astype(F32))
            br = _dot(y_ref[...], wb_ref[k])
            merged = merged + gate * br
            dgl_ref[:, cols] = (dm * br * gate * (1.0 - gate)).astype(BF16)
            dbr = (dm * gate).astype(BF16)
            dbr_ref[:, cols] = dbr
            dy_ref[...] = _dot_nt(dbr, wb_ref[k]).astype(BF16)
        mg_ref[...] = merged.astype(BF16)
        delta_ref[...] = _head_sums(dyc_ref[...], yc_ref[...])

    return _rowcall(name, body, n_tok, tm, [dxo, ya, yb, yc, gl], [], [wb, wo],
                    [(D_MODEL, BF16), (D_MODEL, BF16), (3 * D_MODEL, BF16), (3 * D_MODEL, BF16),
                     (CONV_W, BF16), (SG_W, BF16), (FOX_W, BF16), (LANES, F32)], [])


def _mem_fwd(name, mem, gain, wk, wv):
    n_mem = mem.shape[0]

    def kern(m_ref, g_ref, wk_ref, wv_ref, mn_ref, k_ref, v_ref):
        mh, _ = _rms_stats(m_ref[...])
        mn = (mh * g_ref[...]).astype(BF16)
        mn_ref[...] = mn
        k_ref[...] = _dot(mn, wk_ref[...]).astype(BF16)
        v_ref[...] = _dot(mn, wv_ref[...]).astype(BF16)

    shp = jax.ShapeDtypeStruct((n_mem, D_MODEL), BF16)
    return pl.pallas_call(
        kern, name=name, grid=(1,),
        in_specs=[_full_spec(mem.shape), _full_spec(gain.shape), _full_spec(wk.shape), _full_spec(wv.shape)],
        out_specs=[_full_spec(shp.shape)] * 3, out_shape=[shp] * 3, compiler_params=_params(1),
    )(mem, gain, wk, wv)


def _mem_bwd(name, mem, gain, dkx, dvx, wk, wv):
    n_mem = mem.shape[0]

    def kern(m_ref, g_ref, dk_ref, dv_ref, wk_ref, wv_ref, dkb_ref, dvb_ref, dg_ref):
        mh, _ = _rms_stats(m_ref[...])
        dkb, dvb = dk_ref[...].astype(BF16), dv_ref[...].astype(BF16)
        dkb_ref[...] = dkb
        dvb_ref[...] = dvb
        dm = _dot_nt(dkb, wk_ref[...]) + _dot_nt(dvb, wv_ref[...])
        dg_ref[...] = jnp.sum(dm * mh, axis=0, keepdims=True)

    shp = jax.ShapeDtypeStruct((n_mem, D_MODEL), BF16)
    args = (mem, gain, dkx, dvx, wk, wv)
    return pl.pallas_call(
        kern, name=name, grid=(1,), in_specs=[_full_spec(a.shape) for a in args],
        out_specs=[_full_spec(shp.shape)] * 2 + [_full_spec((1, D_MODEL))],
        out_shape=[shp, shp, jax.ShapeDtypeStruct((1, D_MODEL), F32)], compiler_params=_params(1),
    )(*args)


def _xa_probs(q_b, kx_ref, hd):
    cols = slice(hd * XA_D, (hd + 1) * XA_D)
    s = _dot_nt(q_b[:, cols], kx_ref[:, cols]) * (XA_D ** -0.5)
    e = jnp.exp(s - jnp.max(s, axis=1, keepdims=True))
    return e / jnp.sum(e, axis=1, keepdims=True)


def _xa_fwd(name, x, gain, kx, vx, wq, wo):
    n_tok, tm = x.shape[0], min(ROW_BLOCK, x.shape[0])

    def body(i, n, x_ref, g_ref, kx_ref, vx_ref, wq_ref, wo_ref, xo_ref):
        x_v = x_ref[...]
        xh, _ = _rms_stats(x_v)
        q_b = _dot((xh * g_ref[...]).astype(BF16), wq_ref[...]).astype(BF16)
        o = jnp.concatenate(
            [_dot(_xa_probs(q_b, kx_ref, hd).astype(BF16), vx_ref[:, hd * XA_D:(hd + 1) * XA_D]) for hd in range(XA_H)],
            axis=1)
        xo_ref[...] = x_v + _dot(o.astype(BF16), wo_ref[...])

    return _rowcall(name, body, n_tok, tm, [x], [gain, kx, vx], [wq, wo], [(D_MODEL, F32)], [])[0]


def _xa_bwd(name, x, dxo, gain, kx, vx, wq, wo):
    n_tok, tm = x.shape[0], min(ROW_BLOCK, x.shape[0])
    n_mem = kx.shape[0]

    def body(i, n, x_ref, dxo_ref, g_ref, kx_ref, vx_ref, wq_ref, wo_ref,
             dx_ref, h_ref, o_ref, dq_ref, dy_ref, dkx_ref, dvx_ref, dg_ref):
        g = g_ref[...]
        xh, r = _rms_stats(x_ref[...])
        h = (xh * g).astype(BF16)
        h_ref[...] = h
        q_b = _dot(h, wq_ref[...]).astype(BF16)
        dxo_v = dxo_ref[...]
        dy = dxo_v.astype(BF16)
        dy_ref[...] = dy
        do = _dot_nt(dy, wo_ref[...])
        for hd in range(XA_H):
            cols = slice(hd * XA_D, (hd + 1) * XA_D)
            p = _xa_probs(q_b, kx_ref, hd)
            p_b = p.astype(BF16)
            o_ref[:, cols] = _dot(p_b, vx_ref[:, cols]).astype(BF16)
            do_h = do[:, cols].astype(BF16)
            dvx_ref[:, cols] += _dot_tn(p_b, do_h)
            dp = _dot_nt(do_h, vx_ref[:, cols])
            ds = p * (dp - jnp.sum(dp * p, axis=1, keepdims=True))
            ds_b = (ds * (XA_D ** -0.5)).astype(BF16)
            dq_ref[:, cols] = _dot(ds_b, kx_ref[:, cols]).astype(BF16)
            dkx_ref[:, cols] += _dot_tn(ds_b, q_b[:, cols])
        dh = _dot_nt(dq_ref[...], wq_ref[...])
        dx, dg = _rms_bwd(xh, r, g, dh)
        dx_ref[...] = dxo_v + dx
        dg_ref[...] += dg

    return _rowcall(name, body, n_tok, tm, [x, dxo], [gain, kx, vx], [wq, wo],
                    [(D_MODEL, F32), (D_MODEL, BF16), (D_MODEL, BF16), (D_MODEL, BF16), (D_MODEL, BF16)],
                    [((n_mem, D_MODEL), F32), ((n_mem, D_MODEL), F32), ((1, D_MODEL), F32)])


def _loss_head(name, x, target, gain):
    n_tok, tm = x.shape[0], min(ROW_BLOCK, x.shape[0])

    def body(i, n, x_ref, t_ref, g_ref, dx_ref, loss_ref, dg_ref):
        g = g_ref[...]
        xh, r = _rms_stats(x_ref[...])
        err = xh * g - t_ref[...]
        loss_ref[...] += 0.5 * jnp.sum(jnp.sum(err * err, axis=1, keepdims=True) / D_MODEL, axis=0, keepdims=True)
        dx, dg = _rms_bwd(xh, r, g, err / D_MODEL)
        dx_ref[...] = dx
        dg_ref[...] += dg

    return _rowcall(name, body, n_tok, tm, [x, target], [gain], [], [(D_MODEL, F32)],
                    [((8, LANES), F32), ((1, D_MODEL), F32)])


def _adamw(name, w, g, m, v):
    rows, cols = w.shape
    tr = 128 if rows % 128 == 0 else rows

    def kern(w_ref, g_ref, m_ref, v_ref, d_ref, mo_ref, vo_ref):
        g_v = g_ref[...]
        m_new = ADAM_B1 * m_ref[...] + (1.0 - ADAM_B1) * g_v
        v_new = ADAM_B2 * v_ref[...] + (1.0 - ADAM_B2) * (g_v * g_v)
        m_hat = m_new / (1.0 - ADAM_B1 ** ADAM_STEP)
        v_hat = v_new / (1.0 - ADAM_B2 ** ADAM_STEP)
        d_ref[...] = -ADAM_LR * (m_hat / (jnp.sqrt(v_hat) + ADAM_EPS) + ADAM_WD * w_ref[...])
        mo_ref[...] = m_new
        vo_ref[...] = v_new

    spec = pl.BlockSpec((tr, cols), lambda i: (i, 0))
    shp = jax.ShapeDtypeStruct((rows, cols), F32)
    return pl.pallas_call(kern, name=name, grid=(rows // tr,), in_specs=[spec] * 4, out_specs=[spec] * 3,
                          out_shape=[shp] * 3, compiler_params=_params(1))(w, g, m, v)


def _add_pair(name, gp, got, core):
    _, n, rows, cols = gp.shape
    spec = pl.BlockSpec((1, PACK_ROW_BLOCK, cols), lambda j, r, core_ref: (j, r, 0))
    mine = pl.BlockSpec((1, PACK_ROW_BLOCK, cols), lambda j, r, core_ref: (core_ref[0] * n + j, r, 0))

    def kern(core_ref, a_ref, b_ref, o_ref):
        o_ref[...] = (a_ref[...].astype(F32) + b_ref[...].astype(F32)).astype(BF16)

    grid_spec = pltpu.PrefetchScalarGridSpec(num_scalar_prefetch=1, grid=(n, rows // PACK_ROW_BLOCK),
                                             in_specs=[mine, spec], out_specs=spec)
    return pl.pallas_call(kern, name=name, grid_spec=grid_spec, out_shape=jax.ShapeDtypeStruct(got.shape, BF16),
                          compiler_params=_params(2))(core, gp.reshape(2 * n, rows, cols), got)


def _sum_parts(name, s, got, chip):
    _, rows, cols = s.shape
    tr = PACK_ROW_BLOCK

    def kern(chip_ref, s_ref, g_ref, o_ref):
        acc = s_ref[0].astype(F32)
        for k in range(3):
            acc = acc + g_ref[k].astype(F32)
        o_ref[...] = acc

    grid_spec = pltpu.PrefetchScalarGridSpec(
        num_scalar_prefetch=1, grid=(rows // tr,),
        in_specs=[pl.BlockSpec((1, tr, cols), lambda r, chip_ref: (chip_ref[0], r, 0)),
                  pl.BlockSpec((3, tr, cols), lambda r, chip_ref: (0, r, 0))],
        out_specs=pl.BlockSpec((tr, cols), lambda r, chip_ref: (r, 0)))
    return pl.pallas_call(kern, name=name, grid_spec=grid_spec, out_shape=jax.ShapeDtypeStruct((rows, cols), F32),
                          compiler_params=_params(1))(chip, s, got)


def _sum_slots(name, a, tr):
    n, rows, cols = a.shape

    def kern(a_ref, o_ref):
        acc = a_ref[0].astype(F32)
        for k in range(1, n):
            acc = acc + a_ref[k].astype(F32)
        o_ref[...] = acc

    return pl.pallas_call(kern, name=name, grid=(rows // tr,),
                          in_specs=[pl.BlockSpec((n, tr, cols), lambda r: (0, r, 0))],
                          out_specs=pl.BlockSpec((tr, cols), lambda r: (r, 0)),
                          out_shape=jax.ShapeDtypeStruct((rows, cols), F32), compiler_params=_params(1))(a)


_ANY = pl.BlockSpec(memory_space=pl.ANY)


COMM_CHUNKS = 13


def _place():
    x, y, c = lax.axis_index("x"), lax.axis_index("y"), lax.axis_index("c")
    chips = [(1 - x, y), (x, 1 - y), (1 - x, 1 - y)]
    return x, y, c, chips


def _rows(ref, q, n):
    step = ref.shape[0] // n
    return ref.at[pl.ds(q * step, step)]


def _gather_chips(name, w):
    n = COMM_CHUNKS

    def kern(w_ref, o_ref, send_sems, recv_sems):
        x, y, c, _ = _place()
        me, here, sib = 2 * x + y, (x, y, c), (x, y, 1 - c)
        north = c == 1
        first_hop = (jnp.where(north, 1 - x, x), jnp.where(north, y, 1 - y))
        second_hop = (jnp.where(north, x, 1 - x), jnp.where(north, 1 - y, y))
        chip_a, chip_b, chip_d = 2 * first_hop[0] + first_hop[1], 2 * second_hop[0] + second_hop[1], 3 - me

        def copy(k, q, src, dst, to):
            return pltpu.make_async_remote_copy(src_ref=_rows(src, q, n), dst_ref=_rows(dst, q, n),
                                                send_sem=send_sems.at[k * n + q], recv_sem=recv_sems.at[k * n + q],
                                                device_id=to, device_id_type=MESH_ID)

        sent = [copy(k, q, w_ref.at[c], o_ref.at[me, c], (*hop, c))
                for k, hop in ((0, first_hop), (1, second_hop)) for q in range(n)]
        for cp in sent:
            cp.start()
        arrivals = ((0, chip_a, True), (1, chip_b, False), (2, chip_d, False))
        for k, chip, onward in arrivals:
            for q in range(n):
                landed = o_ref.at[chip, c]
                copy(k, q, landed, landed, here).wait_recv()
                if onward:
                    sent.append(copy(2, q, landed, landed, (*second_hop, c)))
                    sent[-1].start()
                sent.append(copy(3 + k, q, landed, landed, sib))
                sent[-1].start()
        for k, chip in ((3, chip_b), (4, chip_a), (5, chip_d)):
            for q in range(n):
                theirs = o_ref.at[chip, 1 - c]
                copy(k, q, theirs, theirs, here).wait_recv()
        for cp in sent:
            cp.wait_send()

    return pl.pallas_call(
        kern, name=name, in_specs=[_ANY], out_specs=_ANY,
        out_shape=jax.ShapeDtypeStruct((N_CHIPS,) + w.shape, w.dtype),
        scratch_shapes=[pltpu.SemaphoreType.DMA((6 * n,)), pltpu.SemaphoreType.DMA((6 * n,))],
        compiler_params=pltpu.CompilerParams(has_side_effects=True),
    )(w)


def _gather_all(name, v):
    def kern(v_ref, o_ref, send_sems, recv_sems, local_sem):
        x, y, c, _ = _place()
        me = 4 * x + 2 * y + c
        mine = pltpu.make_async_copy(v_ref, o_ref.at[me], local_sem)
        mine.start()
        peers = []
        for k in range(1, 8):
            px = 1 - x if k & 4 else x
            py = 1 - y if k & 2 else y
            pc = 1 - c if k & 1 else c
            peers.append((px, py, pc))
        copies = [pltpu.make_async_remote_copy(src_ref=v_ref, dst_ref=o_ref.at[me], send_sem=send_sems.at[k],
                                               recv_sem=recv_sems.at[k], device_id=p, device_id_type=MESH_ID)
                  for k, p in enumerate(peers)]
        for cp in copies:
            cp.start()
        for k, (px, py, pc) in enumerate(peers):
            pltpu.make_async_remote_copy(src_ref=v_ref, dst_ref=o_ref.at[4 * px + 2 * py + pc], send_sem=send_sems.at[k],
                                         recv_sem=recv_sems.at[k], device_id=(x, y, c), device_id_type=MESH_ID).wait_recv()
        for cp in copies:
            cp.wait_send()
        mine.wait()

    return pl.pallas_call(
        kern, name=name, in_specs=[_ANY], out_specs=_ANY, out_shape=jax.ShapeDtypeStruct((8,) + v.shape, v.dtype),
        scratch_shapes=[pltpu.SemaphoreType.DMA((7,)), pltpu.SemaphoreType.DMA((7,)), pltpu.SemaphoreType.DMA(())],
        compiler_params=pltpu.CompilerParams(has_side_effects=True),
    )(v)


def _swap_halves(name, gp):
    n = COMM_CHUNKS
    n_slots = gp.shape[1]

    def kern(g_ref, got_ref, send_sems, recv_sems):
        x, y, c, _ = _place()
        swaps = [pltpu.make_async_remote_copy(src_ref=_rows(g_ref.at[1 - c, j], q, n), dst_ref=_rows(got_ref.at[j], q, n),
                                              send_sem=send_sems.at[j * n + q], recv_sem=recv_sems.at[j * n + q],
                                              device_id=(x, y, 1 - c), device_id_type=MESH_ID)
                 for j in range(n_slots) for q in range(n)]
        for cp in swaps:
            cp.start()
        for cp in swaps:
            cp.wait()

    return pl.pallas_call(
        kern, name=name, in_specs=[_ANY], out_specs=_ANY, out_shape=jax.ShapeDtypeStruct(gp.shape[1:], gp.dtype),
        scratch_shapes=[pltpu.SemaphoreType.DMA((n_slots * n,)), pltpu.SemaphoreType.DMA((n_slots * n,))],
        compiler_params=pltpu.CompilerParams(has_side_effects=True),
    )(gp)


def _scatter_chips(name, s):
    n = COMM_CHUNKS

    def kern(s_ref, o_ref, send_sems, recv_sems):
        x, y, c, chips = _place()
        copies = [pltpu.make_async_remote_copy(src_ref=_rows(s_ref.at[2 * cx + cy], q, n), dst_ref=_rows(o_ref.at[k], q, n),
                                               send_sem=send_sems.at[k * n + q], recv_sem=recv_sems.at[k * n + q],
                                               device_id=(cx, cy, c), device_id_type=MESH_ID)
                  for k, (cx, cy) in enumerate(chips) for q in range(n)]
        for cp in copies:
            cp.start()
        for k in range(3):
            for q in range(n):
                landed = _rows(o_ref.at[k], q, n)
                pltpu.make_async_remote_copy(src_ref=landed, dst_ref=landed, send_sem=send_sems.at[k * n + q],
                                             recv_sem=recv_sems.at[k * n + q], device_id=(x, y, c),
                                             device_id_type=MESH_ID).wait_recv()
        for cp in copies:
            cp.wait_send()

    return pl.pallas_call(
        kern, name=name, in_specs=[_ANY], out_specs=_ANY, out_shape=jax.ShapeDtypeStruct((3,) + s.shape[1:], s.dtype),
        scratch_shapes=[pltpu.SemaphoreType.DMA((3 * n,)), pltpu.SemaphoreType.DMA((3 * n,))],
        compiler_params=pltpu.CompilerParams(has_side_effects=True),
    )(s)


def _join_halves(name, r):
    n = COMM_CHUNKS

    def kern(r_ref, o_ref, send_sems, recv_sems):
        x, y, c, _ = _place()
        swaps = [pltpu.make_async_remote_copy(src_ref=_rows(r_ref, q, n), dst_ref=_rows(o_ref, q, n),
                                              send_sem=send_sems.at[q], recv_sem=recv_sems.at[q],
                                              device_id=(x, y, 1 - c), device_id_type=MESH_ID) for q in range(n)]
        for cp in swaps:
            cp.start()
        for cp in swaps:
            cp.wait()

    return pl.pallas_call(
        kern, name=name, in_specs=[_ANY], out_specs=_ANY, out_shape=jax.ShapeDtypeStruct(r.shape, r.dtype),
        scratch_shapes=[pltpu.SemaphoreType.DMA((n,))] * 2, compiler_params=pltpu.CompilerParams(has_side_effects=True),
    )(r)


PACK_ALIGN = 16


def _exchange_rows(nm, shard_shape):
    if BIG_KIND[nm] == "cols":
        rows = shard_shape[-1]
    else:
        rows = 1
        for s in shard_shape[1:]:
            rows *= s
        rows //= PACK_COLS
    return -(-rows // PACK_ALIGN) * PACK_ALIGN


def _to_exchange(nm, a, transposed=False):
    if BIG_KIND[nm] == "cols":
        if not transposed:
            a = jnp.swapaxes(a, 1, 2)
        rows = -(-a.shape[1] // PACK_ALIGN) * PACK_ALIGN
        a = _pad_to(a, 1, rows)
        return a.reshape(a.shape[0] * rows, PACK_COLS)
    return a.reshape(a.shape[0] * _exchange_rows(nm, a.shape), PACK_COLS)


def _from_exchange(nm, rows2d, shard_shape):
    lead = rows2d.shape[:-2]
    n_layers = shard_shape[0]
    if BIG_KIND[nm] == "cols":
        a = rows2d.reshape(lead + (n_layers, -1, PACK_COLS))
        return lax.slice_in_dim(a, 0, shard_shape[-1], axis=a.ndim - 2)
    return rows2d.reshape(lead + tuple(shard_shape))


def _pack(pieces, dtype):
    slab = jnp.concatenate([p.astype(dtype) for p in pieces])
    half = -(-slab.shape[0] // (2 * PACK_ROW_BLOCK)) * PACK_ROW_BLOCK
    slab = _pad_to(slab, 0, 2 * half)
    return slab.reshape(2, half, PACK_COLS)


def _unpack(packed, row_counts):
    lead = packed.shape[:-3]
    slab = packed.reshape(lead + (-1, PACK_COLS))
    out, off = [], 0
    for rows in row_counts:
        out.append(lax.slice_in_dim(slab, off, off + rows, axis=slab.ndim - 2))
        off += rows
    return out


def _t(a):
    return jnp.swapaxes(a, -1, -2)


def _rows_from_lanes(a):
    return a[:, :FOX_H].T[:, None, :]


def _lanes_from_heads(a):
    return jnp.pad(a[:, 0, :].T, ((0, 0), (0, LANES - FOX_H)))


def _layer_weights(wts, small, l):
    w_in_t = wts["w_in"][l]
    w = {
        "wf_t": jnp.pad(w_in_t[W1_COLS:W1_COLS + FOX_H], ((0, LANES - FOX_H), (0, 0))),
        "w1_t": w_in_t[:W1_COLS], "wgl_t": w_in_t[W1_COLS + FOX_H:],
    }
    for nm, _ in BIG:
        if nm != "w_in":
            w[nm + "_t" if BIG_KIND[nm] == "cols" else nm] = wts[nm][l]
    for nm in ("ffn1_norm", "mix_norm", "sg_norm", "xa_norm", "mem_norm", "ffn2_norm"):
        w[nm] = small[nm][l][None, :]
    w["conv_w"] = jnp.pad(small["conv_w"][l], ((0, 5), (0, 0)))
    w["sg_w"] = small["sg_w"][l]
    w["sg_w_t"] = _t(small["sg_w"][l])
    w["sg_bias"] = jnp.repeat(small["sg_b"][l].T, CHUNK, axis=1)
    w["fox_b"] = jnp.pad(small["fox_b_f"][l][None, :], ((0, 0), (0, LANES - FOX_H)))
    return w


def _layer_fwd(l, x, mem, w):
    s = {"x0": x}
    x1, s["a1"], s["b1"] = _ffn_fwd(f"ffn1_fwd_{l}", x, w["ffn1_norm"], w["ffn1_w_gate_t"], w["ffn1_w_up_t"],
                                    w["ffn1_w_down"])
    s["x1"] = x1
    cin, sgin, qkv, fl, gl = _proj_fwd(f"proj_fwd_{l}", x1, w["mix_norm"], w["w1_t"], w["wf_t"], w["wgl_t"])
    ya = _conv_fwd(f"conv_fwd_{l}", cin, w["conv_w"])
    yb = _sg_fwd(f"sg_fwd_{l}", sgin, w["sg_norm"], w["sg_w"], w["sg_bias"])
    c2 = _forget_cumsum(f"forget_fwd_{l}", fl, w["fox_b"])
    tables = _prune_tables(_flash_bounds(f"flash_bounds_{l}", qkv), c2, min(FLASH_BLOCK, qkv.shape[0]))
    yc, lse2 = _flash_fwd(f"flash_fwd_{l}", qkv, c2, _rows_from_lanes(c2), tables)
    x2 = _merge_fwd(f"merge_fwd_{l}", x1, ya, yb, yc, gl, w["w_branch"], w["w_out"])
    s.update(cin=cin, sgin=sgin, qkv=qkv, fl=fl, gl=gl, ya=ya, yb=yb, yc=yc, c2=c2, lse2=lse2, x2=x2, tables=tables)
    s["mn"], s["kx"], s["vx"] = _mem_fwd(f"mem_fwd_{l}", mem, w["mem_norm"], w["xa_wk"], w["xa_wv"])
    x3 = _xa_fwd(f"xa_fwd_{l}", x2, w["xa_norm"], s["kx"], s["vx"], w["xa_wq"], w["xa_wo"])
    s["x3"] = x3
    x4, s["a2"], s["b2"] = _ffn_fwd(f"ffn2_fwd_{l}", x3, w["ffn2_norm"], w["ffn2_w_gate_t"], w["ffn2_w_up_t"],
                                    w["ffn2_w_down"])
    return x4, s


def _ffn_grads(tag, x, dxo, a, b, w, pre):
    dx, da, db, sv, h, dy, dg = _ffn_bwd(f"{pre}_bwd_{tag}", x, dxo, a, b, w[pre + "_norm"], w[pre + "_w_down"],
                                         w[pre + "_w_gate_t"], w[pre + "_w_up_t"])
    g = {
        pre + "_norm": dg[0],
        pre + "_w_gate": _mm_tn(f"{pre}_dwg_{tag}", da, h, D_FF, D_MODEL, tk=FF_SPLIT, tn=D_MODEL),
        pre + "_w_up": _mm_tn(f"{pre}_dwu_{tag}", db, h, D_FF, D_MODEL, tk=FF_SPLIT, tn=D_MODEL),
        pre + "_w_down": _mm_tn(f"{pre}_dwd_{tag}", sv, dy, D_FF, D_MODEL, tk=FF_SPLIT, tn=D_MODEL),
    }
    return dx, g


def _layer_bwd(l, dx, mem, w, s):
    g = {}
    dx, gf = _ffn_grads(l, s["x3"], dx, s["a2"], s["b2"], w, "ffn2")
    g.update(gf)

    dx, h, o, dq, dy, dkx, dvx, dg = _xa_bwd(f"xa_bwd_{l}", s["x2"], dx, w["xa_norm"], s["kx"], s["vx"], w["xa_wq"],
                                              w["xa_wo"])
    g["xa_norm"] = dg[0]
    g["xa_wq"] = _mm_tn(f"xa_dwq_{l}", h, dq, D_MODEL, D_MODEL, tk=D_MODEL, tn=D_MODEL)
    g["xa_wo"] = _mm_tn(f"xa_dwo_{l}", o, dy, D_MODEL, D_MODEL, tk=D_MODEL, tn=D_MODEL)
    dkb, dvb, dgm = _mem_bwd(f"mem_bwd_{l}", mem, w["mem_norm"], dkx, dvx, w["xa_wk"], w["xa_wv"])
    g["mem_norm"] = dgm[0]
    g["xa_wk"] = _mm_tn(f"xa_dwk_{l}", s["mn"], dkb, D_MODEL, D_MODEL, tk=D_MODEL, tn=D_MODEL)
    g["xa_wv"] = _mm_tn(f"xa_dwv_{l}", s["mn"], dvb, D_MODEL, D_MODEL, tk=D_MODEL, tn=D_MODEL)

    mg, dob, dgl, dbr, dya, dyb, dyc, delta = _merge_bwd(f"merge_bwd_{l}", dx, s["ya"], s["yb"], s["yc"], s["gl"], w["w_branch"],
                                                  w["w_out"])
    g["w_out"] = _mm_tn(f"dwout_{l}", mg, dob, D_MODEL, D_MODEL, tk=D_MODEL, tn=D_MODEL)
    g["w_branch"] = jnp.stack([
        _mm_tn(f"dwbranch{k}_{l}", y, dbr, CONV_W, D_MODEL, tk=CONV_W, tn=D_MODEL, y_off=k * D_MODEL)
        for k, y in enumerate((s["ya"], s["yb"], s["yc"]))])

    dcin, dcw = _conv_bwd(f"conv_bwd_{l}", s["cin"], dya, w["conv_w"])
    g["conv_w"] = dcw[:3]
    dsg, dgn, dsw, dsb = _sg_bwd(f"sg_bwd_{l}", s["sgin"], dyb, w["sg_norm"], w["sg_w"], w["sg_w_t"], w["sg_bias"])
    g["sg_norm"], g["sg_w"], g["sg_b"] = dgn[0], dsw, dsb[:, :, 0]
    d_row = _rows_from_lanes(delta)
    r_row = s["lse2"].reshape(FOX_H, 1, -1) - _rows_from_lanes(s["c2"])
    dq, dk, dv, dc_rows = _flash_bwd(f"flash_bwd_{l}", s["qkv"], dyc, r_row, d_row, s["c2"], s["tables"])
    dc = _lanes_from_heads(dc_rows.reshape(FOX_H, 1, -1))
    dfl, dbf = _forget_bwd(f"forget_bwd_{l}", dc, s["fl"], w["fox_b"])
    g["fox_b_f"] = dbf[0, :FOX_H]

    dx, h, dqkv, dfb, dg = _proj_bwd(f"proj_bwd_{l}", s["x1"], dx, dcin, dsg, dq, dk, dv, dfl, dgl, w["mix_norm"],
                                     w["w1_t"], w["wf_t"], w["wgl_t"])
    g["mix_norm"] = dg[0]
    c0, c1 = 3 * CONV_W, 3 * CONV_W + 2 * SG_W
    g["w_in"] = jnp.concatenate([
        _mm_tn(f"dwin_conv_{l}", dcin, h, c0, D_MODEL, tk=c0, tn=D_MODEL),
        _mm_tn(f"dwin_sg_{l}", dsg, h, c1 - c0, D_MODEL, tk=c1 - c0, tn=D_MODEL),
        _mm_tn(f"dwin_qkv_{l}", dqkv, h, 3 * FOX_W, D_MODEL, tk=3 * FOX_W, tn=D_MODEL),
        _mm_tn(f"dwin_f_{l}", dfb, h, LANES, D_MODEL, tk=LANES, tn=D_MODEL)[:FOX_H],
        _mm_tn(f"dwin_gl_{l}", dgl, h, 3 * D_MODEL, D_MODEL, tk=3 * D_MODEL // 2, tn=D_MODEL),
    ], axis=0)

    dx, gf = _ffn_grads(l, s["x0"], dx, s["a1"], s["b1"], w, "ffn1")
    g.update(gf)
    return dx, g


def _local_step(x, mem, target, wts, small):
    saved, lw = [], []
    for l in range(DEPTH):
        lw.append(_layer_weights(wts, small, l))
        x, s = _layer_fwd(l, x, mem, lw[l])
        saved.append(s)
    fin = small["final_norm"][None, :]
    dx, loss, dgf = _loss_head("loss_head", x, target, fin)
    layer_grads = [None] * DEPTH
    for l in reversed(range(DEPTH)):
        dx, layer_grads[l] = _layer_bwd(l, dx, mem, lw[l], saved[l])
    grads = {nm: jnp.stack([layer_grads[l][nm] for l in range(DEPTH)]) for nm in WEIGHTS if nm != "final_norm"}
    grads["final_norm"] = dgf[0]
    return loss[0, 0], dx, grads


def _small_slab(vals):
    rows = []
    for v in vals:
        flat = v.astype(F32).reshape(-1)
        n = -(-flat.shape[0] // LANES) * LANES
        rows.append(_pad_to(flat, 0, n).reshape(-1, LANES))
    slab = jnp.concatenate(rows)
    return _pad_to(slab, 0, slab.shape[0] + (-slab.shape[0] % 8))


def _pad_to(a, axis, size):
    if a.shape[axis] == size:
        return a
    widths = [(0, 0)] * a.ndim
    widths[axis] = (0, size - a.shape[axis])
    return jnp.pad(a, widths)


def _small_unslab(slab, shapes):
    out, off = [], 0
    for shp in shapes:
        size = 1
        for s in shp:
            size *= s
        n_rows = -(-size // LANES)
        out.append(slab[off:off + n_rows].reshape(-1)[:size].reshape(shp))
        off += n_rows
    return out


def kernel(x, mem, ffn1_norm, ffn1_w_gate, ffn1_w_up, ffn1_w_down, mix_norm, w_in, conv_w, sg_norm, sg_w, sg_b, fox_b_f, w_branch, w_out, xa_norm, mem_norm, xa_wq, xa_wk, xa_wv, xa_wo, ffn2_norm, ffn2_w_gate, ffn2_w_up, ffn2_w_down, final_norm, loss_target, m_ffn1_norm, m_ffn1_w_gate, m_ffn1_w_up, m_ffn1_w_down, m_mix_norm, m_w_in, m_conv_w, m_sg_norm, m_sg_w, m_sg_b, m_fox_b_f, m_w_branch, m_w_out, m_xa_norm, m_mem_norm, m_xa_wq, m_xa_wk, m_xa_wv, m_xa_wo, m_ffn2_norm, m_ffn2_w_gate, m_ffn2_w_up, m_ffn2_w_down, m_final_norm, v_ffn1_norm, v_ffn1_w_gate, v_ffn1_w_up, v_ffn1_w_down, v_mix_norm, v_w_in, v_conv_w, v_sg_norm, v_sg_w, v_sg_b, v_fox_b_f, v_w_branch, v_w_out, v_xa_norm, v_mem_norm, v_xa_wq, v_xa_wk, v_xa_wv, v_xa_wo, v_ffn2_norm, v_ffn2_w_gate, v_ffn2_w_up, v_ffn2_w_down, v_final_norm):
    args = dict(locals())
    wv = {nm: args[nm] for nm in WEIGHTS}
    mv = {nm: args["m_" + nm] for nm in WEIGHTS}
    vv = {nm: args["v_" + nm] for nm in WEIGHTS}
    chip = 2 * lax.axis_index("x") + lax.axis_index("y")
    core = lax.axis_index("c")

    names = [nm for nm, _ in BIG]
    mine = {nm: _to_exchange(nm, wv[nm].astype(BF16)) for nm in names}
    row_counts = [mine[nm].shape[0] for nm in names]
    gathered = _gather_chips("gather_weights", _pack([mine[nm] for nm in names], BF16))
    wts = {}
    for nm, slab in zip(names, _unpack(gathered, row_counts)):
        p = _from_exchange(nm, jnp.where((jnp.arange(N_CHIPS) == chip)[:, None, None], mine[nm][None], slab), wv[nm].shape)
        if BIG_KIND[nm] == "branch":
            wts[nm] = jnp.concatenate([p[j] for j in range(N_CHIPS)], axis=3)
        else:
            wts[nm] = jnp.swapaxes(p, 0, 1).reshape(p.shape[1], N_CHIPS * p.shape[2], PACK_COLS)
    taps = _gather_all("gather_taps", _small_slab([conv_w]))
    taps = [_small_unslab(taps[2 * j], [conv_w.shape])[0] for j in range(N_CHIPS)]
    small = {nm: wv[nm] for nm in SMALL}
    small["conv_w"] = jnp.concatenate(taps, axis=2)

    loss, dx, grads = _local_step(x[0], mem[0], loss_target[0], wts, small)
    loss = lax.psum(loss, ("x", "y", "c"))

    small_shapes = [grads[nm].shape for nm in SMALL]
    slots = _gather_all("gather_small_grads", _small_slab([grads[nm] for nm in SMALL]))
    small_sum = _sum_slots("sum_small_grads", slots, slots.shape[1])
    small_g = dict(zip(SMALL, _small_unslab(small_sum, small_shapes)))
    taps_g = small_g["conv_w"]
    small_g["conv_w"] = lax.dynamic_slice_in_dim(taps_g, chip * conv_w.shape[2], conv_w.shape[2], axis=2)

    def grad_shard(nm, ax, j):
        g = grads[nm]
        ax = 1 if BIG_KIND[nm] == "cols" else ax
        size = g.shape[ax] // N_CHIPS
        return _to_exchange(nm, lax.slice_in_dim(g, j * size, (j + 1) * size, axis=ax), transposed=True)

    gp = jnp.stack([_pack([grad_shard(nm, ax, j) for nm, ax in BIG], BF16) for j in range(N_CHIPS)], axis=1)
    got = _swap_halves("reduce_swap_cores", gp)
    chip_sum = _add_pair("reduce_add_cores", gp, got, core.astype(jnp.int32).reshape(1))
    parts = _scatter_chips("reduce_scatter_chips", chip_sum)
    half = _sum_parts("reduce_add_chips", chip_sum, parts, chip.astype(jnp.int32).reshape(1))
    other = _join_halves("reduce_join_cores", half)
    full = jnp.where(core == 0, jnp.stack([half, other]), jnp.stack([other, half]))
    big_g = {}
    for nm, slab in zip(names, _unpack(full, row_counts)):
        g = _from_exchange(nm, slab, wv[nm].shape)
        big_g[nm] = jnp.swapaxes(g, 1, 2) if BIG_KIND[nm] == "cols" else g

    g_out = {**small_g, **big_g}
    delta, new_m, new_v = {}, {}, {}
    for nm, _ in BIG:
        shp = wv[nm].shape
        two_d = (-1, shp[-1])
        d, m2, v2 = _adamw("adamw_" + nm, wv[nm].reshape(two_d), g_out[nm].reshape(two_d), mv[nm].reshape(two_d),
                           vv[nm].reshape(two_d))
        delta[nm], new_m[nm], new_v[nm] = d.reshape(shp), m2.reshape(shp), v2.reshape(shp)
    slab_shapes = [wv[nm].shape for nm in SMALL]
    d, m2, v2 = _adamw("adamw_small", _small_slab([wv[nm] for nm in SMALL]), _small_slab([g_out[nm] for nm in SMALL]),
                       _small_slab([mv[nm] for nm in SMALL]), _small_slab([vv[nm] for nm in SMALL]))
    for out, slab in ((delta, d), (new_m, m2), (new_v, v2)):
        out.update(zip(SMALL, _small_unslab(slab, slab_shapes)))

    return (loss, dx[None], *[g_out[nm] for nm in WEIGHTS], *[delta[nm] for nm in WEIGHTS],
            *[new_m[nm] for nm in WEIGHTS], *[new_v[nm] for nm in WEIGHTS])
```
